```python
import jax, jax.numpy as jnp
from jax import lax
import numpy as np

D_MODEL = 1024
BATCH = 8
SEQ = 2048
DEPTH = 1

CONV_WIDTH = 512
CONV_KERNEL = 3
N_HEADS = 8
HEAD_DIM = 64
ATTN_WIDTH = N_HEADS * HEAD_DIM
MOBA_BLOCK = 256
MOBA_TOPK = 3
QUERY_CHUNK = 32
N_GROUPS = 4
EXPERTS_PER_GROUP = 8
N_EXPERTS = N_GROUPS * EXPERTS_PER_GROUP
EXPERT_TOPK = 2
D_EXPERT = 256
LN_EPS = 1e-5
IN_COLS = 3 * CONV_WIDTH + 3 * ATTN_WIDTH + 2 * D_MODEL
SPLITS = [CONV_WIDTH, 2 * CONV_WIDTH, 3 * CONV_WIDTH,
          3 * CONV_WIDTH + ATTN_WIDTH, 3 * CONV_WIDTH + 2 * ATTN_WIDTH,
          3 * CONV_WIDTH + 3 * ATTN_WIDTH, 3 * CONV_WIDTH + 3 * ATTN_WIDTH + D_MODEL]

kernel_name = 'hybrid_conv_moba_hmoe'


def layer_norm(x, g, b):
    xf = x.astype(jnp.float32)
    mu = jnp.mean(xf, axis=-1, keepdims=True)
    var = jnp.mean(jnp.square(xf - mu), axis=-1, keepdims=True)
    return ((xf - mu) * lax.rsqrt(var + LN_EPS) * g + b).astype(x.dtype)


def causal_depthwise_conv(u, w):
    return lax.conv_general_dilated(
        u, w[:, None, :].astype(u.dtype), window_strides=(1,),
        padding=[(CONV_KERNEL - 1, 0)], dimension_numbers=('NWC', 'WIO', 'NWC'),
        feature_group_count=u.shape[-1])


def alibi_slopes():
    return jnp.asarray(2.0 ** (-8.0 * np.arange(1, N_HEADS + 1) / N_HEADS), jnp.float32)


def moba_attention(q, k, v, slopes):
    nb_, h_, s_, hd = q.shape
    s_pad = -(-s_ // MOBA_BLOCK) * MOBA_BLOCK
    pad = ((0, 0), (0, 0), (0, s_pad - s_), (0, 0))
    nblk = s_pad // MOBA_BLOCK
    kb = jnp.pad(k, pad).reshape(nb_, h_, nblk, MOBA_BLOCK, hd)
    vb = jnp.pad(v, pad).reshape(nb_, h_, nblk, MOBA_BLOCK, hd)
    kmean = jnp.mean(kb, axis=3)
    topk = min(MOBA_TOPK, nblk)
    n_chunks = s_ // QUERY_CHUNK
    qc = q.reshape(nb_, h_, n_chunks, QUERY_CHUNK, hd).transpose(2, 0, 1, 3, 4)
    scale = HEAD_DIM ** -0.5
    bidx = jnp.arange(nb_)[:, None, None, None]
    hidx = jnp.arange(h_)[None, :, None, None]
    offs = jnp.arange(MOBA_BLOCK)
    blk_ids = jnp.arange(nblk)

    def chunk_step(args):
        ci, qi = args
        t = ci * QUERY_CHUNK + jnp.arange(QUERY_CHUNK)
        own = (ci * QUERY_CHUNK) // MOBA_BLOCK
        gate = jnp.einsum('bhqd,bhnd->bhqn', qi, kmean).astype(jnp.float32)
        gate = jnp.where(blk_ids < own, gate, -jnp.inf)
        _, sel = lax.top_k(gate, topk)
        sel_valid = sel < own
        k_sel = kb[bidx, hidx, sel]
        v_sel = vb[bidx, hidx, sel]
        pos_past = sel[..., None] * MOBA_BLOCK + offs
        s_past = jnp.einsum('bhqd,bhqnkd->bhqnk', qi, k_sel).astype(jnp.float32) * scale
        s_past = s_past - slopes[:, None, None, None] * (t[:, None, None] - pos_past)
        s_past = jnp.where(sel_valid[..., None], s_past, -jnp.inf)
        k_own = lax.dynamic_slice_in_dim(kb, own, 1, axis=2)[:, :, 0]
        v_own = lax.dynamic_slice_in_dim(vb, own, 1, axis=2)[:, :, 0]
        pos_own = own * MOBA_BLOCK + offs
        s_own = jnp.einsum('bhqd,bhkd->bhqk', qi, k_own).astype(jnp.float32) * scale
        s_own = s_own - slopes[:, None, None] * (t[:, None] - pos_own)
        s_own = jnp.where(pos_own <= t[:, None], s_own, -jnp.inf)
        scores = jnp.concatenate(
            [s_past.reshape(nb_, h_, QUERY_CHUNK, topk * MOBA_BLOCK), s_own], axis=-1)
        p = jax.nn.softmax(scores, axis=-1).astype(v.dtype)
        p_past = p[..., :topk * MOBA_BLOCK].reshape(nb_, h_, QUERY_CHUNK, topk, MOBA_BLOCK)
        p_own = p[..., topk * MOBA_BLOCK:]
        return (jnp.einsum('bhqnk,bhqnkd->bhqd', p_past, v_sel)
                + jnp.einsum('bhqk,bhkd->bhqd', p_own, v_own))

    out = lax.map(chunk_step, (jnp.arange(n_chunks), qc))
    return out.transpose(1, 2, 0, 3, 4).reshape(nb_, h_, s_, hd)


def hybrid_mixer(x, w_in, conv_w, w_out_conv, w_out_attn, w_o, slopes):
    nb_, s_, _ = x.shape
    proj = jnp.einsum('bsd,dc->bsc', x, w_in)
    c_b, c_c, c_x, q, k, v, g_a, g_b = jnp.split(proj, SPLITS, axis=-1)
    y_conv = jnp.einsum('bsc,cd->bsd', c_b * causal_depthwise_conv(c_c * c_x, conv_w), w_out_conv)
    heads = lambda z: z.reshape(nb_, s_, N_HEADS, HEAD_DIM).transpose(0, 2, 1, 3)
    o = moba_attention(heads(q), heads(k), heads(v), slopes)
    o = o.transpose(0, 2, 1, 3).reshape(nb_, s_, ATTN_WIDTH)
    y_attn = jnp.einsum('bsc,cd->bsd', o, w_out_attn)
    y = jax.nn.sigmoid(g_a) * y_conv + jax.nn.sigmoid(g_b) * y_attn
    return jnp.einsum('bsd,de->bse', y, w_o)


def hierarchical_moe(x, w_router_group, w_router_expert, w_gate, w_up, w_down):
    nb_, s_, d_ = x.shape
    xt = x.reshape(-1, d_)
    g_prob = jax.nn.softmax((xt @ w_router_group).astype(jnp.float32), axis=-1)
    g_top_p, g_top = lax.top_k(g_prob, 1)
    e_logits = (xt @ w_router_expert).astype(jnp.float32).reshape(-1, N_GROUPS, EXPERTS_PER_GROUP)
    e_in_group = jnp.take_along_axis(e_logits, g_top[:, :, None], axis=1)[:, 0]
    e_top_logit, e_top = lax.top_k(e_in_group, EXPERT_TOPK)
    gate = g_top_p * jax.nn.softmax(e_top_logit, axis=-1)
    expert_id = g_top * EXPERTS_PER_GROUP + e_top
    combine = jnp.sum(jax.nn.one_hot(expert_id, N_EXPERTS, dtype=jnp.float32)
                      * gate[..., None], axis=1)
    h = (jax.nn.silu(jnp.einsum('td,edf->tef', xt, w_gate))
         * jnp.einsum('td,edf->tef', xt, w_up)
         * combine[..., None].astype(x.dtype))
    return jnp.einsum('tef,efd->td', h, w_down).reshape(nb_, s_, d_)


def setup_inputs(seed: int = 0) -> dict:
    key = jax.random.key(seed)
    ks = jax.random.split(key, 16)
    beta = (8.0 * DEPTH) ** -0.25
    nrm = lambda k, shape, sc: jax.random.normal(k, shape, jnp.float32) * sc
    ones = lambda n: jnp.ones((n,), jnp.float32)
    col_scale = jnp.concatenate([
        ones(CONV_WIDTH), ones(CONV_WIDTH), beta * ones(CONV_WIDTH),
        ones(ATTN_WIDTH), ones(ATTN_WIDTH), beta * ones(ATTN_WIDTH), ones(2 * D_MODEL)])
    return {
        'x': nrm(ks[0], (BATCH, SEQ, D_MODEL), 1.0),
        'w_in': nrm(ks[1], (DEPTH, D_MODEL, IN_COLS), D_MODEL ** -0.5) * col_scale,
        'conv_w': nrm(ks[2], (DEPTH, CONV_KERNEL, CONV_WIDTH), CONV_KERNEL ** -0.5),
        'w_out_conv': nrm(ks[3], (DEPTH, CONV_WIDTH, D_MODEL), beta * CONV_WIDTH ** -0.5),
        'w_out_attn': nrm(ks[4], (DEPTH, ATTN_WIDTH, D_MODEL), beta * ATTN_WIDTH ** -0.5),
        'w_o': nrm(ks[5], (DEPTH, D_MODEL, D_MODEL), beta * D_MODEL ** -0.5),
        'ln1_g': 1.0 + nrm(ks[6], (DEPTH, D_MODEL), 0.02),
        'ln1_b': nrm(ks[7], (DEPTH, D_MODEL), 0.02),
        'w_router_group': nrm(ks[8], (DEPTH, D_MODEL, N_GROUPS), D_MODEL ** -0.5),
        'w_router_expert': nrm(ks[9], (DEPTH, D_MODEL, N_EXPERTS), D_MODEL ** -0.5),
        'w_gate': nrm(ks[10], (DEPTH, N_EXPERTS, D_MODEL, D_EXPERT), beta * D_MODEL ** -0.5),
        'w_up': nrm(ks[11], (DEPTH, N_EXPERTS, D_MODEL, D_EXPERT), beta * D_MODEL ** -0.5),
        'w_down': nrm(ks[12], (DEPTH, N_EXPERTS, D_EXPERT, D_MODEL), beta * D_EXPERT ** -0.5),
        'ln2_g': 1.0 + nrm(ks[13], (DEPTH, D_MODEL), 0.02),
        'ln2_b': nrm(ks[14], (DEPTH, D_MODEL), 0.02),
    }


def reference(x, w_in, conv_w, w_out_conv, w_out_attn, w_o, ln1_g, ln1_b,
              w_router_group, w_router_expert, w_gate, w_up, w_down, ln2_g, ln2_b):
    alpha = (2.0 * DEPTH) ** 0.25
    slopes = alibi_slopes()
    for l in range(DEPTH):
        mix = hybrid_mixer(x, w_in[l], conv_w[l], w_out_conv[l], w_out_attn[l], w_o[l], slopes)
        x = layer_norm(alpha * x + mix, ln1_g[l], ln1_b[l])
        ffn = hierarchical_moe(x, w_router_group[l], w_router_expert[l], w_gate[l], w_up[l], w_down[l])
        x = layer_norm(alpha * x + ffn, ln2_g[l], ln2_b[l])
    return x
```

```python
import functools

import jax
import jax.numpy as jnp
from jax import lax
from jax.experimental import pallas as pl
from jax.experimental.pallas import tpu as pltpu

F32 = jnp.float32
BF16 = jnp.bfloat16
U32 = jnp.uint32
I32 = jnp.int32

D_MODEL = 1024
BATCH = 8
SEQ = 2048
TOKENS = BATCH * SEQ
CONV_WIDTH = 512
N_HEADS = 8
HEAD_DIM = 64
ATTN_WIDTH = N_HEADS * HEAD_DIM
MOBA_BLOCK = 256
N_BLOCKS = SEQ // MOBA_BLOCK
MOBA_TOPK = 3
N_GROUPS = 4
EXPERTS_PER_GROUP = 8
N_EXPERTS = N_GROUPS * EXPERTS_PER_GROUP
D_EXPERT = 256
LN_EPS = 1e-5
ALPHA = 2.0 ** 0.25
IN_COLS = 3 * CONV_WIDTH + 3 * ATTN_WIDTH + 2 * D_MODEL
HALF = D_MODEL // 2

TM = 256
TE = 256
MAX_TILES = (2 * TOKENS) // TE + N_EXPERTS
ROWS = MAX_TILES * TE
ROUTER_ROWS = 40
VMEM_LIMIT = 56 * 1024 * 1024
NEG_INF = float("-inf")


def _sigmoid(z):
    return 1.0 / (1.0 + jnp.exp(-z))


def _proj_kernel(x_ref, w_in_ref, convw_ref, woc_ref, q_ref, k_ref, v_ref, za_ref, sgb_ref, ubuf):
    s = pl.program_id(1)
    xb = x_ref[0].astype(BF16)

    def proj(c0, c1):
        return jnp.dot(xb, w_in_ref[:, c0:c1], preferred_element_type=F32)

    c_b = proj(0, CONV_WIDTH)
    u = proj(CONV_WIDTH, 2 * CONV_WIDTH) * proj(2 * CONV_WIDTH, 3 * CONV_WIDTH)

    @pl.when(s == 0)
    def _():
        ubuf[0:8, :] = jnp.zeros((8, CONV_WIDTH), F32)

    ubuf[8:8 + TM, :] = u
    w = convw_ref[...]
    conv = w[2:3, :] * u + w[1:2, :] * ubuf[7:7 + TM, :] + w[0:1, :] * ubuf[6:6 + TM, :]
    ubuf[0:8, :] = u[TM - 8:TM, :]
    hc = (c_b * conv).astype(BF16)
    y_conv = jnp.dot(hc, woc_ref[...], preferred_element_type=F32)

    o = 3 * CONV_WIDTH
    q_ref[0] = (proj(o, o + ATTN_WIDTH) * (HEAD_DIM ** -0.5)).astype(BF16)
    k_ref[0] = proj(o + ATTN_WIDTH, o + 2 * ATTN_WIDTH).astype(BF16)
    v_ref[0] = proj(o + 2 * ATTN_WIDTH, o + 3 * ATTN_WIDTH).astype(BF16)
    o += 3 * ATTN_WIDTH
    za_ref[0] = _sigmoid(proj(o, o + D_MODEL)) * y_conv
    sgb_ref[0] = _sigmoid(proj(o + D_MODEL, o + 2 * D_MODEL))


def _proj_call(x, w_in, conv_w, w_out_conv):
    tok_spec = lambda c: pl.BlockSpec((1, TM, c), lambda b, s: (b, s, 0))
    full = lambda shape: pl.BlockSpec(shape, lambda b, s: (0,) * len(shape))
    return pl.pallas_call(
        _proj_kernel,
        grid=(BATCH, SEQ // TM),
        in_specs=[tok_spec(D_MODEL), full((D_MODEL, IN_COLS)), full((3, CONV_WIDTH)),
                  full((CONV_WIDTH, D_MODEL))],
        out_specs=[tok_spec(ATTN_WIDTH), tok_spec(ATTN_WIDTH), tok_spec(ATTN_WIDTH),
                   tok_spec(D_MODEL), tok_spec(D_MODEL)],
        out_shape=[jax.ShapeDtypeStruct((BATCH, SEQ, ATTN_WIDTH), BF16)] * 3
        + [jax.ShapeDtypeStruct((BATCH, SEQ, D_MODEL), F32)] * 2,
        scratch_shapes=[pltpu.VMEM((TM + 8, CONV_WIDTH), F32)],
        compiler_params=pltpu.CompilerParams(
            dimension_semantics=("arbitrary", "arbitrary"), vmem_limit_bytes=VMEM_LIMIT),
        name="proj",
    )(x, w_in, conv_w, w_out_conv)


def _attn_kernel(slopes_ref, q_ref, k_ref, v_ref, o_ref, vt_ref, kmean_ref, s_ref):
    hp = pl.program_id(1)
    qb = pl.program_id(2)
    blk = MOBA_BLOCK

    @pl.when(qb == 0)
    def _():
        for n in range(N_BLOCKS):
            vb = v_ref[0, n * blk:(n + 1) * blk, :].astype(F32)
            vt_ref[n] = vb.T.astype(BF16)
            kb = k_ref[0, n * blk:(n + 1) * blk, :].astype(F32)
            kmean_ref[n:n + 1, :] = jnp.mean(kb, axis=0, keepdims=True)

    q_t = q_ref[0].astype(F32).T
    feat = lax.broadcasted_iota(I32, (2 * HEAD_DIM, blk), 0)
    key_i = lax.broadcasted_iota(I32, (blk, blk), 0)
    qry_j = lax.broadcasted_iota(I32, (blk, blk), 1)
    dmat = (key_i - qry_j).astype(F32)
    blk_i = lax.broadcasted_iota(I32, (N_BLOCKS, blk), 0)
    kmean = kmean_ref[...].astype(BF16)
    qbf = qb.astype(F32)

    outs = []
    for j in range(2):
        slope = slopes_ref[2 * hp + j]
        q_tj = jnp.where((feat >= HEAD_DIM * j) & (feat < HEAD_DIM * (j + 1)), q_t, 0.0).astype(BF16)
        gate = jnp.dot(kmean, q_tj, preferred_element_type=F32)
        cnt = jnp.zeros((N_BLOCKS, blk), F32)
        for m in range(N_BLOCKS):
            gm = gate[m:m + 1, :]
            beats = (gm > gate) | ((gm == gate) & (blk_i > m))
            cnt = cnt + jnp.where(beats & (qb > m), 1.0, 0.0)
        sel = jnp.where((blk_i < qb) & (cnt < float(MOBA_TOPK)), 1.0, 0.0)

        def pass1(kb, m_run):
            start = pl.multiple_of(kb * blk, blk)
            kblk = k_ref[0, pl.ds(start, blk), :]
            s = jnp.dot(kblk, q_tj, preferred_element_type=F32)
            s = s + slope * (dmat + (kb.astype(F32) - qbf) * float(blk))
            selrow = jnp.sum(jnp.where(blk_i == kb, sel, 0.0), axis=0, keepdims=True)
            own = (kb == qb).astype(F32)
            thresh = own * 1e30 - 1e30
            ok = (selrow > 0.5) | (dmat <= thresh)
            s = jnp.where(ok, s, NEG_INF)
            s_ref[kb] = s
            return jnp.maximum(m_run, jnp.max(s, axis=0, keepdims=True))

        m_fin = lax.fori_loop(0, qb + 1, pass1, jnp.full((1, blk), NEG_INF, F32))

        def pass2(kb, carry):
            l_run, acc = carry
            p = jnp.exp(s_ref[kb] - m_fin)
            l_run = l_run + jnp.sum(p, axis=0, keepdims=True)
            vt = vt_ref[kb, HEAD_DIM * j:HEAD_DIM * (j + 1), :]
            acc = acc + jnp.dot(vt, p.astype(BF16), preferred_element_type=F32)
            return l_run, acc

        l_fin, acc = lax.fori_loop(
            0, qb + 1, pass2, (jnp.zeros((1, blk), F32), jnp.zeros((HEAD_DIM, blk), F32)))
        outs.append(acc / l_fin)

    o_t = jnp.concatenate(outs, axis=0)
    o_ref[0] = o_t.T.astype(BF16)


def _attn_call(slopes, q, k, v):
    return pl.pallas_call(
        _attn_kernel,
        grid_spec=pltpu.PrefetchScalarGridSpec(
            num_scalar_prefetch=1,
            grid=(BATCH, N_HEADS // 2, N_BLOCKS),
            in_specs=[
                pl.BlockSpec((1, MOBA_BLOCK, 128), lambda b, h, i, sl: (b, i, h)),
                pl.BlockSpec((1, SEQ, 128), lambda b, h, i, sl: (b, 0, h)),
                pl.BlockSpec((1, SEQ, 128), lambda b, h, i, sl: (b, 0, h)),
            ],
            out_specs=pl.BlockSpec((1, MOBA_BLOCK, 128), lambda b, h, i, sl: (b, i, h)),
            scratch_shapes=[
                pltpu.VMEM((N_BLOCKS, 128, MOBA_BLOCK), BF16),
                pltpu.VMEM((N_BLOCKS, 128), F32),
                pltpu.VMEM((N_BLOCKS, MOBA_BLOCK, MOBA_BLOCK), F32),
            ],
        ),
        out_shape=jax.ShapeDtypeStruct((BATCH, SEQ, ATTN_WIDTH), BF16),
        compiler_params=pltpu.CompilerParams(
            dimension_semantics=("arbitrary", "arbitrary", "arbitrary"), vmem_limit_bytes=VMEM_LIMIT),
        name="moba_attn",
    )(slopes, q, k, v)


def _merge_kernel(o_ref, za_ref, sgb_ref, x_ref, woa_ref, wo_ref, g_ref, b_ref, wr_hi_ref, wr_lo_ref,
                  x1_ref, ri_ref, rf_ref, cnt_ref):
    i = pl.program_id(0)
    y_attn = jnp.dot(o_ref[...], woa_ref[...], preferred_element_type=F32)
    y = za_ref[...] + sgb_ref[...] * y_attn
    mix = jnp.dot(y.astype(BF16), wo_ref[...], preferred_element_type=F32)
    h = ALPHA * x_ref[...] + mix
    mu = jnp.mean(h, axis=-1, keepdims=True)
    hc = h - mu
    var = jnp.mean(hc * hc, axis=-1, keepdims=True)
    x1 = hc * lax.rsqrt(var + LN_EPS) * g_ref[...] + b_ref[...]
    x1_ref[...] = x1

    xh = x1.astype(BF16)
    xl = (x1 - xh.astype(F32)).astype(BF16)
    nt = (((1,), (1,)), ((), ()))
    wh = wr_hi_ref[...]
    logits = (lax.dot_general(wh, xh, nt, preferred_element_type=F32)
              + lax.dot_general(wh, xl, nt, preferred_element_type=F32)
              + lax.dot_general(wr_lo_ref[...], xh, nt, preferred_element_type=F32))

    row8 = lax.broadcasted_iota(I32, (8, TM), 0).astype(F32)
    gl = jnp.where(row8 < float(N_GROUPS), logits[0:8, :], NEG_INF)
    gexp = jnp.exp(gl - jnp.max(gl, axis=0, keepdims=True))
    gprob = gexp / jnp.sum(gexp, axis=0, keepdims=True)
    ptop = jnp.max(gprob, axis=0, keepdims=True)
    gtop = jnp.min(jnp.where(gprob == ptop, row8, 8.0), axis=0, keepdims=True)
    el = logits[8:ROUTER_ROWS, :]
    eg = jnp.where(gtop == 0.0, el[0:8, :],
                   jnp.where(gtop == 1.0, el[8:16, :], jnp.where(gtop == 2.0, el[16:24, :], el[24:32, :])))
    m1 = jnp.max(eg, axis=0, keepdims=True)
    i1 = jnp.min(jnp.where(eg == m1, row8, 8.0), axis=0, keepdims=True)
    eg2 = jnp.where(row8 == i1, NEG_INF, eg)
    m2 = jnp.max(eg2, axis=0, keepdims=True)
    i2 = jnp.min(jnp.where(eg2 == m2, row8, 8.0), axis=0, keepdims=True)
    t2 = jnp.exp(m2 - m1)
    gate1 = ptop * (1.0 / (1.0 + t2))
    gate2 = ptop * (t2 / (1.0 + t2))
    e1 = gtop * float(EXPERTS_PER_GROUP) + i1
    e2 = gtop * float(EXPERTS_PER_GROUP) + i2

    erow = lax.broadcasted_iota(I32, (N_EXPERTS, TM), 0).astype(F32)
    oh1 = jnp.where(erow == e1, 1.0, 0.0)
    oh2 = jnp.where(erow == e2, 1.0, 0.0)
    oh = oh1 + oh2
    ta = lax.broadcasted_iota(I32, (TM, TM), 0)
    tb = lax.broadcasted_iota(I32, (TM, TM), 1)
    upper = jnp.where(ta < tb, 1.0, 0.0).astype(BF16)
    cum = jnp.dot(oh.astype(BF16), upper, preferred_element_type=F32)

    @pl.when(i == 0)
    def _():
        cnt_ref[...] = jnp.zeros((N_EXPERTS, 128), F32)

    base = cnt_ref[...][:, 0:1]
    tot = cum + base
    r1 = jnp.sum(oh1 * tot, axis=0, keepdims=True)
    r2 = jnp.sum(oh2 * tot, axis=0, keepdims=True)
    cnt_ref[...] = cnt_ref[...] + jnp.sum(oh, axis=1, keepdims=True)

    zero = jnp.zeros((1, TM), F32)
    ri_ref[...] = jnp.concatenate([e1, e2, r1, r2, zero, zero, zero, zero], axis=0).astype(I32)
    rf_ref[...] = jnp.concatenate([gate1, gate2, zero, zero, zero, zero, zero, zero], axis=0)


def _merge_call(o, za, sgb, x, woa, wo, g, b, wr_hi, wr_lo):
    tok = lambda c: pl.BlockSpec((TM, c), lambda i: (i, 0))
    full = lambda shape: pl.BlockSpec(shape, lambda i: (0,) * len(shape))
    lanes = pl.BlockSpec((8, TM), lambda i: (0, i))
    return pl.pallas_call(
        _merge_kernel,
        grid=(TOKENS // TM,),
        in_specs=[tok(ATTN_WIDTH), tok(D_MODEL), tok(D_MODEL), tok(D_MODEL),
                  full((ATTN_WIDTH, D_MODEL)), full((D_MODEL, D_MODEL)), full((1, D_MODEL)),
                  full((1, D_MODEL)), full((ROUTER_ROWS, D_MODEL)), full((ROUTER_ROWS, D_MODEL))],
        out_specs=[tok(D_MODEL), lanes, lanes, full((N_EXPERTS, 128))],
        out_shape=[jax.ShapeDtypeStruct((TOKENS, D_MODEL), F32),
                   jax.ShapeDtypeStruct((8, TOKENS), I32),
                   jax.ShapeDtypeStruct((8, TOKENS), F32),
                   jax.ShapeDtypeStruct((N_EXPERTS, 128), F32)],
        compiler_params=pltpu.CompilerParams(
            dimension_semantics=("arbitrary",), vmem_limit_bytes=VMEM_LIMIT),
        name="merge_ln1_route",
    )(o, za, sgb, x, woa, wo, g, b, wr_hi, wr_lo)


def _row_copy(src_ref, src_row, dst_ref, dst_row, sem):
    return pltpu.make_async_copy(src_ref.at[pl.ds(src_row, 1), :], dst_ref.at[pl.ds(dst_row, 1), :], sem)


def _dispatch_kernel(pos_ref, x1_ref, xs_in_ref, xs_ref, sem):
    del xs_in_ref
    i = pl.program_id(0)

    def issue(r, c):
        t = i * TM + r
        _row_copy(x1_ref, r, xs_ref, pos_ref[t], sem).start()
        _row_copy(x1_ref, r, xs_ref, pos_ref[TOKENS + t], sem).start()
        return c

    lax.fori_loop(0, TM, issue, 0, unroll=8)
    for _ in range(2):
        pltpu.make_async_copy(x1_ref, xs_ref.at[pl.ds(0, TM), :], sem).wait()


def _dispatch_call(pos, x1, xs_zero):
    return pl.pallas_call(
        _dispatch_kernel,
        grid_spec=pltpu.PrefetchScalarGridSpec(
            num_scalar_prefetch=1,
            grid=(TOKENS // TM,),
            in_specs=[pl.BlockSpec((TM, D_MODEL), lambda i, pos: (i, 0)),
                      pl.BlockSpec(memory_space=pl.ANY)],
            out_specs=pl.BlockSpec(memory_space=pl.ANY),
            scratch_shapes=[pltpu.SemaphoreType.DMA],
        ),
        out_shape=jax.ShapeDtypeStruct((ROWS, D_MODEL), F32),
        input_output_aliases={2: 0},
        compiler_params=pltpu.CompilerParams(dimension_semantics=("arbitrary",)),
        name="dispatch",
    )(pos, x1, xs_zero)


def _expert_kernel(te_ref, nt_ref, xs_ref, wg_ref, wu_ref, wd_ref, ys_ref):
    j = pl.program_id(0)

    @pl.when(j < nt_ref[0])
    def _():
        xb = xs_ref[...].astype(BF16)
        hg = jnp.dot(xb, wg_ref[0], preferred_element_type=F32)
        hu = jnp.dot(xb, wu_ref[0], preferred_element_type=F32)
        h = (hg * _sigmoid(hg) * hu).astype(BF16)
        ys_ref[...] = jnp.dot(h, wd_ref[0], preferred_element_type=F32)

    @pl.when(j >= nt_ref[0])
    def _():
        ys_ref[...] = jnp.zeros((TE, D_MODEL), F32)


def _expert_call(tile_expert, n_tiles, xs, wg, wu, wd):
    row = lambda j, te, nt: (jnp.minimum(j, nt[0] - 1), 0)
    out_row = lambda j, te, nt: (j, 0)
    wsel = lambda j, te, nt: (te[j], 0, 0)
    return pl.pallas_call(
        _expert_kernel,
        grid_spec=pltpu.PrefetchScalarGridSpec(
            num_scalar_prefetch=2,
            grid=(MAX_TILES,),
            in_specs=[pl.BlockSpec((TE, D_MODEL), row),
                      pl.BlockSpec((1, D_MODEL, D_EXPERT), wsel),
                      pl.BlockSpec((1, D_MODEL, D_EXPERT), wsel),
                      pl.BlockSpec((1, D_EXPERT, D_MODEL), wsel)],
            out_specs=pl.BlockSpec((TE, D_MODEL), out_row),
        ),
        out_shape=jax.ShapeDtypeStruct((ROWS, D_MODEL), F32),
        compiler_params=pltpu.CompilerParams(
            dimension_semantics=("arbitrary",), vmem_limit_bytes=VMEM_LIMIT),
        name="experts",
    )(tile_expert, n_tiles, xs, wg, wu, wd)


def _combine_kernel(pos_ref, x1_ref, rf_ref, g_ref, b_ref, ys_ref, out_ref, ybuf, sem):
    i = pl.program_id(0)

    def issue(r, c):
        t = i * TM + r
        _row_copy(ys_ref, pos_ref[t], ybuf.at[0], r, sem).start()
        _row_copy(ys_ref, pos_ref[TOKENS + t], ybuf.at[1], r, sem).start()
        return c

    lax.fori_loop(0, TM, issue, 0, unroll=8)
    gates = rf_ref[...].T
    for slot in range(2):
        pltpu.make_async_copy(ys_ref.at[pl.ds(0, TM), :], ybuf.at[slot], sem).wait()
    ffn = gates[:, 0:1] * ybuf[0] + gates[:, 1:2] * ybuf[1]
    h = ALPHA * x1_ref[...] + ffn
    mu = jnp.mean(h, axis=-1, keepdims=True)
    hc = h - mu
    var = jnp.mean(hc * hc, axis=-1, keepdims=True)
    out_ref[...] = hc * lax.rsqrt(var + LN_EPS) * g_ref[...] + b_ref[...]


def _combine_call(pos, x1, rf, g, b, ys):
    return pl.pallas_call(
        _combine_kernel,
        grid_spec=pltpu.PrefetchScalarGridSpec(
            num_scalar_prefetch=1,
            grid=(TOKENS // TM,),
            in_specs=[pl.BlockSpec((TM, D_MODEL), lambda i, pos: (i, 0)),
                      pl.BlockSpec((8, TM), lambda i, pos: (0, i)),
                      pl.BlockSpec((1, D_MODEL), lambda i, pos: (0, 0)),
                      pl.BlockSpec((1, D_MODEL), lambda i, pos: (0, 0)),
                      pl.BlockSpec(memory_space=pl.ANY)],
            out_specs=pl.BlockSpec((TM, D_MODEL), lambda i, pos: (i, 0)),
            scratch_shapes=[pltpu.VMEM((2, TM, D_MODEL), F32), pltpu.SemaphoreType.DMA],
        ),
        out_shape=jax.ShapeDtypeStruct((TOKENS, D_MODEL), F32),
        compiler_params=pltpu.CompilerParams(
            dimension_semantics=("arbitrary",), vmem_limit_bytes=VMEM_LIMIT),
        name="combine_ln2",
    )(pos, x1, rf, g, b, ys)


def _router_rows(w_router_group, w_router_expert):
    w = jnp.concatenate([w_router_group.T, jnp.zeros((4, D_MODEL), F32), w_router_expert.T], axis=0)
    hi = w.astype(BF16)
    lo = (w - hi.astype(F32)).astype(BF16)
    return hi, lo


def _layer(x, w_in, conv_w, w_out_conv, w_out_attn, w_o, ln1_g, ln1_b,
           w_router_group, w_router_expert, w_gate, w_up, w_down, ln2_g, ln2_b):
    slopes = jnp.asarray([2.0 ** (-8.0 * (h + 1) / N_HEADS) for h in range(N_HEADS)], F32)
    q, k, v, za, sgb = _proj_call(x, w_in.astype(BF16), conv_w, w_out_conv.astype(BF16))
    o = _attn_call(slopes, q, k, v)

    wr_hi, wr_lo = _router_rows(w_router_group, w_router_expert)
    x1, ri, rf, cnt = _merge_call(
        o.reshape(TOKENS, ATTN_WIDTH), za.reshape(TOKENS, D_MODEL), sgb.reshape(TOKENS, D_MODEL),
        x.reshape(TOKENS, D_MODEL), w_out_attn.astype(BF16), w_o.astype(BF16),
        ln1_g.reshape(1, D_MODEL), ln1_b.reshape(1, D_MODEL), wr_hi, wr_lo)

    counts = cnt[:, 0].astype(I32)
    padded = ((counts + TE - 1) // TE) * TE
    ends = jnp.cumsum(padded)
    offs = ends - padded
    pos = (jnp.take(offs, ri[0:2], axis=0) + ri[2:4]).reshape(2 * TOKENS)
    n_tiles = (ends[-1] // TE).reshape(1)
    tile_ids = jnp.arange(MAX_TILES, dtype=I32)
    tile_expert = jnp.minimum(
        jnp.sum((tile_ids[:, None] >= (ends // TE)[None, :]).astype(I32), axis=1), N_EXPERTS - 1)

    xs = _dispatch_call(pos, x1, jnp.zeros((ROWS, D_MODEL), F32))
    ys = _expert_call(tile_expert, n_tiles, xs, w_gate.astype(BF16), w_up.astype(BF16), w_down.astype(BF16))
    out = _combine_call(pos, x1, rf, ln2_g.reshape(1, D_MODEL), ln2_b.reshape(1, D_MODEL), ys)
    return out.reshape(BATCH, SEQ, D_MODEL)


def kernel(x, w_in, conv_w, w_out_conv, w_out_attn, w_o, ln1_g, ln1_b, w_router_group, w_router_expert, w_gate, w_up, w_down, ln2_g, ln2_b):
    depth = w_in.shape[0]
    for l in range(depth):
        x = _layer(x, w_in[l], conv_w[l], w_out_conv[l], w_out_attn[l], w_o[l], ln1_g[l], ln1_b[l],
                   w_router_group[l], w_router_expert[l], w_gate[l], w_up[l], w_down[l], ln2_g[l], ln2_b[l])
    return x
```

```python
import functools

import jax
import jax.numpy as jnp
from jax import lax
from jax.experimental import pallas as pl
from jax.experimental.pallas import tpu as pltpu

F32 = jnp.float32
BF16 = jnp.bfloat16
U32 = jnp.uint32
I32 = jnp.int32

D_MODEL = 1024
BATCH = 8
SEQ = 2048
TOKENS = BATCH * SEQ
CONV_WIDTH = 512
N_HEADS = 8
HEAD_DIM = 64
ATTN_WIDTH = N_HEADS * HEAD_DIM
MOBA_BLOCK = 256
N_BLOCKS = SEQ // MOBA_BLOCK
MOBA_TOPK = 3
N_GROUPS = 4
EXPERTS_PER_GROUP = 8
N_EXPERTS = N_GROUPS * EXPERTS_PER_GROUP
D_EXPERT = 256
LN_EPS = 1e-5
ALPHA = 2.0 ** 0.25
IN_COLS = 3 * CONV_WIDTH + 3 * ATTN_WIDTH + 2 * D_MODEL
HALF = D_MODEL // 2

TM = 256
TE = 256
PV_ROWS = HEAD_DIM + 16
MAX_TILES = (2 * TOKENS) // TE + N_EXPERTS
ROWS = MAX_TILES * TE
ROUTER_ROWS = 40
VMEM_LIMIT = 56 * 1024 * 1024
NEG_INF = float("-inf")


def _sigmoid(z):
    return 1.0 / (1.0 + jnp.exp(-z))


def _proj_kernel(x_ref, w_in_ref, convw_ref, woc_ref, q_ref, k_ref, v_ref, za_ref, sgb_ref, ubuf):
    s = pl.program_id(1)
    xb = x_ref[0].astype(BF16)

    def proj(c0, c1):
        return jnp.dot(xb, w_in_ref[:, c0:c1], preferred_element_type=F32)

    c_b = proj(0, CONV_WIDTH)
    u = proj(CONV_WIDTH, 2 * CONV_WIDTH) * proj(2 * CONV_WIDTH, 3 * CONV_WIDTH)

    @pl.when(s == 0)
    def _():
        ubuf[0:8, :] = jnp.zeros((8, CONV_WIDTH), F32)

    ubuf[8:8 + TM, :] = u
    w = convw_ref[...]
    conv = w[2:3, :] * u + w[1:2, :] * ubuf[7:7 + TM, :] + w[0:1, :] * ubuf[6:6 + TM, :]
    ubuf[0:8, :] = u[TM - 8:TM, :]
    hc = (c_b * conv).astype(BF16)
    y_conv = jnp.dot(hc, woc_ref[...], preferred_element_type=F32)

    o = 3 * CONV_WIDTH
    q_ref[0] = (proj(o, o + ATTN_WIDTH) * (HEAD_DIM ** -0.5)).astype(BF16)
    k_ref[0] = proj(o + ATTN_WIDTH, o + 2 * ATTN_WIDTH).astype(BF16)
    v_ref[0] = proj(o + 2 * ATTN_WIDTH, o + 3 * ATTN_WIDTH).astype(BF16)
    o += 3 * ATTN_WIDTH
    za_ref[0] = _sigmoid(proj(o, o + D_MODEL)) * y_conv
    sgb_ref[0] = _sigmoid(proj(o + D_MODEL, o + 2 * D_MODEL))


def _proj_call(x, w_in, conv_w, w_out_conv):
    tok_spec = lambda c: pl.BlockSpec((1, TM, c), lambda b, s: (b, s, 0))
    full = lambda shape: pl.BlockSpec(shape, lambda b, s: (0,) * len(shape))
    return pl.pallas_call(
        _proj_kernel,
        grid=(BATCH, SEQ // TM),
        in_specs=[tok_spec(D_MODEL), full((D_MODEL, IN_COLS)), full((3, CONV_WIDTH)),
                  full((CONV_WIDTH, D_MODEL))],
        out_specs=[tok_spec(ATTN_WIDTH), tok_spec(ATTN_WIDTH), tok_spec(ATTN_WIDTH),
                   tok_spec(D_MODEL), tok_spec(D_MODEL)],
        out_shape=[jax.ShapeDtypeStruct((BATCH, SEQ, ATTN_WIDTH), BF16)] * 3
        + [jax.ShapeDtypeStruct((BATCH, SEQ, D_MODEL), F32)] * 2,
        scratch_shapes=[pltpu.VMEM((TM + 8, CONV_WIDTH), F32)],
        compiler_params=pltpu.CompilerParams(
            dimension_semantics=("arbitrary", "arbitrary"), vmem_limit_bytes=VMEM_LIMIT),
        name="proj",
    )(x, w_in, conv_w, w_out_conv)


def _attn_kernel(slopes_ref, qa_ref, qb_ref, k_ref, v_ref, oa_ref, ob_ref,
                 kaug_ref, vt_ref, kmean_ref, qaug_ref, pv_ref, mloc_ref):
    hp = pl.program_id(1)
    j = pl.program_id(2)
    blk = MOBA_BLOCK

    @pl.when(j == 0)
    def _():
        klane = lax.broadcasted_iota(I32, (blk, 128), 1)
        koff = lax.broadcasted_iota(I32, (blk, 128), 0).astype(F32)
        k_extra = jnp.where(klane == 0, koff, jnp.where(klane == 1, 1.0, 0.0)).astype(BF16)
        orow = lax.broadcasted_iota(I32, (PV_ROWS - HEAD_DIM, blk), 0)
        ones_rows = jnp.where(orow == 0, 1.0, 0.0).astype(BF16)
        for n in range(N_BLOCKS):
            kblk = k_ref[0, n * blk:(n + 1) * blk, :]
            kaug_ref[n, :, 0:128] = kblk
            kaug_ref[n, :, 128:256] = k_extra
            kmean_ref[n:n + 1, :] = jnp.mean(kblk.astype(F32), axis=0, keepdims=True)
            v_t = v_ref[0, n * blk:(n + 1) * blk, :].astype(F32).T.astype(BF16)
            for hh in range(2):
                vt_ref[n, hh, 0:HEAD_DIM, :] = v_t[hh * HEAD_DIM:(hh + 1) * HEAD_DIM, :]
                vt_ref[n, hh, HEAD_DIM:PV_ROWS, :] = ones_rows

    lane = lax.broadcasted_iota(I32, (1, 2 * blk), 1)
    slope_row = jnp.where(lane < blk, slopes_ref[2 * hp], slopes_ref[2 * hp + 1])
    qoff_row = jnp.where(lane < blk, lane, lane - blk).astype(F32)
    feat = lax.broadcasted_iota(I32, (2 * HEAD_DIM, blk), 0)
    arow = lax.broadcasted_iota(I32, (2 * HEAD_DIM, 2 * blk), 0)
    q_extra = jnp.where(arow == 0, slope_row, jnp.where(arow == 1, -slope_row * qoff_row, 0.0)).astype(BF16)
    blk_i = lax.broadcasted_iota(I32, (N_BLOCKS, 2 * blk), 0)
    kmean = kmean_ref[...].astype(BF16)
    key_i = lax.broadcasted_iota(I32, (blk, 2 * blk), 0)
    qry_j = lax.broadcasted_iota(I32, (blk, 2 * blk), 1)
    causal = key_i <= jnp.where(qry_j < blk, qry_j, qry_j - blk)

    def prepare(q_ref, slot, qblock):
        q_t = q_ref[0].astype(F32).T
        qcat = jnp.concatenate([jnp.where(feat < HEAD_DIM, q_t, 0.0), jnp.where(feat >= HEAD_DIM, q_t, 0.0)],
                               axis=1).astype(BF16)
        qaug_ref[slot, 0:2 * HEAD_DIM, :] = qcat
        qaug_ref[slot, 2 * HEAD_DIM:4 * HEAD_DIM, :] = q_extra
        gate = jnp.dot(kmean, qcat, preferred_element_type=F32)
        cnt = jnp.zeros((N_BLOCKS, 2 * blk), F32)
        for m in range(N_BLOCKS):
            gm = gate[m:m + 1, :]
            beats = (gm > gate) | ((gm == gate) & (blk_i > m))
            cnt = cnt + jnp.where(beats & (qblock > m), 1.0, 0.0)
        return jnp.where((blk_i < qblock) & (cnt < float(MOBA_TOPK)), 1.0, 0.0)

    qblock_a = j
    qblock_b = N_BLOCKS - 1 - j
    sel_a = prepare(qa_ref, 0, qblock_a)
    sel_b = prepare(qb_ref, 1, qblock_b)

    def block_softmax(s, which, kb, own):
        t = jnp.dot(kaug_ref[kb], qaug_ref[which], preferred_element_type=F32)
        if own:
            t = jnp.where(causal, t, NEG_INF)
        m_loc = jnp.max(t, axis=0, keepdims=True)
        p = jnp.exp((t - m_loc).astype(BF16))
        pv_ref[s, 0] = jnp.dot(vt_ref[kb, 0], p[:, 0:blk], preferred_element_type=F32)
        pv_ref[s, 1] = jnp.dot(vt_ref[kb, 1], p[:, blk:2 * blk], preferred_element_type=F32)
        mloc_ref[s:s + 1, :] = m_loc

    n_mid = N_BLOCKS - 1
    block_softmax(0, 0, qblock_a, True)
    mids = []
    for s in range(1, n_mid + 1):
        is_a = s <= j
        which = jnp.where(is_a, 0, 1)
        kb = jnp.where(is_a, s - 1, s - 1 - j)
        block_softmax(s, which, kb, False)
        mids.append((is_a, kb))
    block_softmax(n_mid + 1, 1, qblock_b, True)

    def combine(o_ref, own_slot, sel, qblock, mine):
        neg = jnp.full((1, 2 * blk), -1e30, F32)
        pieces = [(own_slot, mloc_ref[own_slot:own_slot + 1, :])]
        for s, (is_a, kb) in enumerate(mids, start=1):
            selrow = jnp.sum(jnp.where(blk_i == kb, sel, 0.0), axis=0, keepdims=True)
            belongs = jnp.where(is_a, 1.0, 0.0) if mine else jnp.where(is_a, 0.0, 1.0)
            used = selrow * belongs > 0.5
            shift = slope_row * ((kb - qblock) * blk).astype(F32)
            pieces.append((s, jnp.where(used, mloc_ref[s:s + 1, :] + shift, neg)))
        m_all = pieces[0][1]
        for _, m_s in pieces[1:]:
            m_all = jnp.maximum(m_all, m_s)
        acc = [jnp.zeros((PV_ROWS, blk), F32), jnp.zeros((PV_ROWS, blk), F32)]
        for s, m_s in pieces:
            w = jnp.exp(m_s - m_all)
            for hh in range(2):
                acc[hh] = acc[hh] + pv_ref[s, hh] * w[:, hh * blk:(hh + 1) * blk]
        o_t = jnp.concatenate([a[0:HEAD_DIM, :] / a[HEAD_DIM:HEAD_DIM + 1, :] for a in acc], axis=0)
        o_ref[0] = o_t.T.astype(BF16)

    combine(oa_ref, 0, sel_a, qblock_a, True)
    combine(ob_ref, n_mid + 1, sel_b, qblock_b, False)


def _attn_call(slopes, q, k, v):
    half = N_BLOCKS // 2
    o_a, o_b = pl.pallas_call(
        _attn_kernel,
        grid_spec=pltpu.PrefetchScalarGridSpec(
            num_scalar_prefetch=1,
            grid=(BATCH, N_HEADS // 2, half),
            in_specs=[
                pl.BlockSpec((1, MOBA_BLOCK, 128), lambda b, h, j, sl: (b, j, h)),
                pl.BlockSpec((1, MOBA_BLOCK, 128), lambda b, h, j, sl: (b, N_BLOCKS - 1 - j, h)),
                pl.BlockSpec((1, SEQ, 128), lambda b, h, j, sl: (b, 0, h)),
                pl.BlockSpec((1, SEQ, 128), lambda b, h, j, sl: (b, 0, h)),
            ],
            out_specs=[pl.BlockSpec((1, MOBA_BLOCK, 128), lambda b, h, j, sl: (b, j, h)),
                       pl.BlockSpec((1, MOBA_BLOCK, 128), lambda b, h, j, sl: (b, j, h))],
            scratch_shapes=[
                pltpu.VMEM((N_BLOCKS, MOBA_BLOCK, 256), BF16),
                pltpu.VMEM((N_BLOCKS, 2, PV_ROWS, MOBA_BLOCK), BF16),
                pltpu.VMEM((N_BLOCKS, 128), F32),
                pltpu.VMEM((2, 256, 2 * MOBA_BLOCK), BF16),
                pltpu.VMEM((N_BLOCKS + 1, 2, PV_ROWS, MOBA_BLOCK), F32),
                pltpu.VMEM((16, 2 * MOBA_BLOCK), F32),
            ],
        ),
        out_shape=[jax.ShapeDtypeStruct((BATCH, SEQ // 2, ATTN_WIDTH), BF16)] * 2,
        compiler_params=pltpu.CompilerParams(
            dimension_semantics=("arbitrary", "arbitrary", "arbitrary"), vmem_limit_bytes=VMEM_LIMIT),
        name="moba_attn",
    )(slopes, q, q, k, v)
    o_b = o_b.reshape(BATCH, half, MOBA_BLOCK, ATTN_WIDTH)[:, ::-1].reshape(BATCH, SEQ // 2, ATTN_WIDTH)
    return jnp.concatenate([o_a, o_b], axis=1)


def _merge_kernel(o_ref, za_ref, sgb_ref, x_ref, woa_ref, wo_ref, g_ref, b_ref, wr_hi_ref, wr_lo_ref,
                  x1_ref, ri_ref, rf_ref, cnt_ref):
    i = pl.program_id(0)
    y_attn = jnp.dot(o_ref[...], woa_ref[...], preferred_element_type=F32)
    y = za_ref[...] + sgb_ref[...] * y_attn
    mix = jnp.dot(y.astype(BF16), wo_ref[...], preferred_element_type=F32)
    h = ALPHA * x_ref[...] + mix
    mu = jnp.mean(h, axis=-1, keepdims=True)
    hc = h - mu
    var = jnp.mean(hc * hc, axis=-1, keepdims=True)
    x1 = hc * lax.rsqrt(var + LN_EPS) * g_ref[...] + b_ref[...]
    x1_ref[...] = x1

    xh = x1.astype(BF16)
    xl = (x1 - xh.astype(F32)).astype(BF16)
    nt = (((1,), (1,)), ((), ()))
    wh = wr_hi_ref[...]
    logits = (lax.dot_general(wh, xh, nt, preferred_element_type=F32)
              + lax.dot_general(wh, xl, nt, preferred_element_type=F32)
              + lax.dot_general(wr_lo_ref[...], xh, nt, preferred_element_type=F32))

    row8 = lax.broadcasted_iota(I32, (8, TM), 0).astype(F32)
    gl = jnp.where(row8 < float(N_GROUPS), logits[0:8, :], NEG_INF)
    gexp = jnp.exp(gl - jnp.max(gl, axis=0, keepdims=True))
    gprob = gexp / jnp.sum(gexp, axis=0, keepdims=True)
    ptop = jnp.max(gprob, axis=0, keepdims=True)
    gtop = jnp.min(jnp.where(gprob == ptop, row8, 8.0), axis=0, keepdims=True)
    el = logits[8:ROUTER_ROWS, :]
    eg = jnp.where(gtop == 0.0, el[0:8, :],
                   jnp.where(gtop == 1.0, el[8:16, :], jnp.where(gtop == 2.0, el[16:24, :], el[24:32, :])))
    m1 = jnp.max(eg, axis=0, keepdims=True)
    i1 = jnp.min(jnp.where(eg == m1, row8, 8.0), axis=0, keepdims=True)
    eg2 = jnp.where(row8 == i1, NEG_INF, eg)
    m2 = jnp.max(eg2, axis=0, keepdims=True)
    i2 = jnp.min(jnp.where(eg2 == m2, row8, 8.0), axis=0, keepdims=True)
    t2 = jnp.exp(m2 - m1)
    gate1 = ptop * (1.0 / (1.0 + t2))
    gate2 = ptop * (t2 / (1.0 + t2))
    e1 = gtop * float(EXPERTS_PER_GROUP) + i1
    e2 = gtop * float(EXPERTS_PER_GROUP) + i2

    erow = lax.broadcasted_iota(I32, (N_EXPERTS, TM), 0).astype(F32)
    oh1 = jnp.where(erow == e1, 1.0, 0.0)
    oh2 = jnp.where(erow == e2, 1.0, 0.0)
    oh = oh1 + oh2
    ta = lax.broadcasted_iota(I32, (TM, TM), 0)
    tb = lax.broadcasted_iota(I32, (TM, TM), 1)
    upper = jnp.where(ta < tb, 1.0, 0.0).astype(BF16)
    cum = jnp.dot(oh.astype(BF16), upper, preferred_element_type=F32)

    @pl.when(i == 0)
    def _():
        cnt_ref[...] = jnp.zeros((N_EXPERTS, 128), F32)

    base = cnt_ref[...][:, 0:1]
    tot = cum + base
    r1 = jnp.sum(oh1 * tot, axis=0, keepdims=True)
    r2 = jnp.sum(oh2 * tot, axis=0, keepdims=True)
    cnt_ref[...] = cnt_ref[...] + jnp.sum(oh, axis=1, keepdims=True)

    zero = jnp.zeros((1, TM), F32)
    ri_ref[...] = jnp.concatenate([e1, e2, r1, r2, zero, zero, zero, zero], axis=0).astype(I32)
    rf_ref[...] = jnp.concatenate([gate1, gate2, zero, zero, zero, zero, zero, zero], axis=0)


def _merge_call(o, za, sgb, x, woa, wo, g, b, wr_hi, wr_lo):
    tok = lambda c: pl.BlockSpec((TM, c), lambda i: (i, 0))
    full = lambda shape: pl.BlockSpec(shape, lambda i: (0,) * len(shape))
    lanes = pl.BlockSpec((8, TM), lambda i: (0, i))
    return pl.pallas_call(
        _merge_kernel,
        grid=(TOKENS // TM,),
        in_specs=[tok(ATTN_WIDTH), tok(D_MODEL), tok(D_MODEL), tok(D_MODEL),
                  full((ATTN_WIDTH, D_MODEL)), full((D_MODEL, D_MODEL)), full((1, D_MODEL)),
                  full((1, D_MODEL)), full((ROUTER_ROWS, D_MODEL)), full((ROUTER_ROWS, D_MODEL))],
        out_specs=[tok(D_MODEL), lanes, lanes, full((N_EXPERTS, 128))],
        out_shape=[jax.ShapeDtypeStruct((TOKENS, D_MODEL), F32),
                   jax.ShapeDtypeStruct((8, TOKENS), I32),
                   jax.ShapeDtypeStruct((8, TOKENS), F32),
                   jax.ShapeDtypeStruct((N_EXPERTS, 128), F32)],
        compiler_params=pltpu.CompilerParams(
            dimension_semantics=("arbitrary",), vmem_limit_bytes=VMEM_LIMIT),
        name="merge_ln1_route",
    )(o, za, sgb, x, woa, wo, g, b, wr_hi, wr_lo)


def _row_copy(src_ref, src_row, dst_ref, dst_row, sem):
    return pltpu.make_async_copy(src_ref.at[pl.ds(src_row, 1), :], dst_ref.at[pl.ds(dst_row, 1), :], sem)


def _dispatch_kernel(pos_ref, x1_ref, xs_in_ref, xs_ref, sem):
    del xs_in_ref
    i = pl.program_id(0)

    def issue(r, c):
        t = i * TM + r
        _row_copy(x1_ref, r, xs_ref, pos_ref[t], sem).start()
        _row_copy(x1_ref, r, xs_ref, pos_ref[TOKENS + t], sem).start()
        return c

    lax.fori_loop(0, TM, issue, 0, unroll=8)
    for _ in range(2):
        pltpu.make_async_copy(x1_ref, xs_ref.at[pl.ds(0, TM), :], sem).wait()


def _dispatch_call(pos, x1, xs_zero):
    return pl.pallas_call(
        _dispatch_kernel,
        grid_spec=pltpu.PrefetchScalarGridSpec(
            num_scalar_prefetch=1,
            grid=(TOKENS // TM,),
            in_specs=[pl.BlockSpec((TM, D_MODEL), lambda i, pos: (i, 0)),
                      pl.BlockSpec(memory_space=pl.ANY)],
            out_specs=pl.BlockSpec(memory_space=pl.ANY),
            scratch_shapes=[pltpu.SemaphoreType.DMA],
        ),
        out_shape=jax.ShapeDtypeStruct((ROWS, D_MODEL), F32),
        input_output_aliases={2: 0},
        compiler_params=pltpu.CompilerParams(dimension_semantics=("arbitrary",)),
        name="dispatch",
    )(pos, x1, xs_zero)


def _expert_kernel(te_ref, nt_ref, xs_ref, wg_ref, wu_ref, wd_ref, ys_ref):
    j = pl.program_id(0)

    @pl.when(j < nt_ref[0])
    def _():
        xb = xs_ref[...].astype(BF16)
        hg = jnp.dot(xb, wg_ref[0], preferred_element_type=F32)
        hu = jnp.dot(xb, wu_ref[0], preferred_element_type=F32)
        h = (hg * _sigmoid(hg) * hu).astype(BF16)
        ys_ref[...] = jnp.dot(h, wd_ref[0], preferred_element_type=F32)

    @pl.when(j >= nt_ref[0])
    def _():
        ys_ref[...] = jnp.zeros((TE, D_MODEL), F32)


def _expert_call(tile_expert, n_tiles, xs, wg, wu, wd):
    row = lambda j, te, nt: (jnp.minimum(j, nt[0] - 1), 0)
    out_row = lambda j, te, nt: (j, 0)
    wsel = lambda j, te, nt: (te[j], 0, 0)
    return pl.pallas_call(
        _expert_kernel,
        grid_spec=pltpu.PrefetchScalarGridSpec(
            num_scalar_prefetch=2,
            grid=(MAX_TILES,),
            in_specs=[pl.BlockSpec((TE, D_MODEL), row),
                      pl.BlockSpec((1, D_MODEL, D_EXPERT), wsel),
                      pl.BlockSpec((1, D_MODEL, D_EXPERT), wsel),
                      pl.BlockSpec((1, D_EXPERT, D_MODEL), wsel)],
            out_specs=pl.BlockSpec((TE, D_MODEL), out_row),
        ),
        out_shape=jax.ShapeDtypeStruct((ROWS, D_MODEL), F32),
        compiler_params=pltpu.CompilerParams(
            dimension_semantics=("arbitrary",), vmem_limit_bytes=VMEM_LIMIT),
        name="experts",
    )(tile_expert, n_tiles, xs, wg, wu, wd)


def _combine_kernel(pos_ref, x1_ref, rf_ref, g_ref, b_ref, ys_ref, out_ref, ybuf, sem):
    i = pl.program_id(0)

    def issue(r, c):
        t = i * TM + r
        _row_copy(ys_ref, pos_ref[t], ybuf.at[0], r, sem).start()
        _row_copy(ys_ref, pos_ref[TOKENS + t], ybuf.at[1], r, sem).start()
        return c

    lax.fori_loop(0, TM, issue, 0, unroll=8)
    gates = rf_ref[...].T
    for slot in range(2):
        pltpu.make_async_copy(ys_ref.at[pl.ds(0, TM), :], ybuf.at[slot], sem).wait()
    ffn = gates[:, 0:1] * ybuf[0] + gates[:, 1:2] * ybuf[1]
    h = ALPHA * x1_ref[...] + ffn
    mu = jnp.mean(h, axis=-1, keepdims=True)
    hc = h - mu
    var = jnp.mean(hc * hc, axis=-1, keepdims=True)
    out_ref[...] = hc * lax.rsqrt(var + LN_EPS) * g_ref[...] + b_ref[...]


def _combine_call(pos, x1, rf, g, b, ys):
    return pl.pallas_call(
        _combine_kernel,
        grid_spec=pltpu.PrefetchScalarGridSpec(
            num_scalar_prefetch=1,
            grid=(TOKENS // TM,),
            in_specs=[pl.BlockSpec((TM, D_MODEL), lambda i, pos: (i, 0)),
                      pl.BlockSpec((8, TM), lambda i, pos: (0, i)),
                      pl.BlockSpec((1, D_MODEL), lambda i, pos: (0, 0)),
                      pl.BlockSpec((1, D_MODEL), lambda i, pos: (0, 0)),
                      pl.BlockSpec(memory_space=pl.ANY)],
            out_specs=pl.BlockSpec((TM, D_MODEL), lambda i, pos: (i, 0)),
            scratch_shapes=[pltpu.VMEM((2, TM, D_MODEL), F32), pltpu.SemaphoreType.DMA],
        ),
        out_shape=jax.ShapeDtypeStruct((TOKENS, D_MODEL), F32),
        compiler_params=pltpu.CompilerParams(
            dimension_semantics=("arbitrary",), vmem_limit_bytes=VMEM_LIMIT),
        name="combine_ln2",
    )(pos, x1, rf, g, b, ys)


def _router_rows(w_router_group, w_router_expert):
    w = jnp.concatenate([w_router_group.T, jnp.zeros((4, D_MODEL), F32), w_router_expert.T], axis=0)
    hi = w.astype(BF16)
    lo = (w - hi.astype(F32)).astype(BF16)
    return hi, lo


def _layer(x, w_in, conv_w, w_out_conv, w_out_attn, w_o, ln1_g, ln1_b,
           w_router_group, w_router_expert, w_gate, w_up, w_down, ln2_g, ln2_b):
    slopes = jnp.asarray([2.0 ** (-8.0 * (h + 1) / N_HEADS) for h in range(N_HEADS)], F32)
    q, k, v, za, sgb = _proj_call(x, w_in.astype(BF16), conv_w, w_out_conv.astype(BF16))
    o = _attn_call(slopes, q, k, v)

    wr_hi, wr_lo = _router_rows(w_router_group, w_router_expert)
    x1, ri, rf, cnt = _merge_call(
        o.reshape(TOKENS, ATTN_WIDTH), za.reshape(TOKENS, D_MODEL), sgb.reshape(TOKENS, D_MODEL),
        x.reshape(TOKENS, D_MODEL), w_out_attn.astype(BF16), w_o.astype(BF16),
        ln1_g.reshape(1, D_MODEL), ln1_b.reshape(1, D_MODEL), wr_hi, wr_lo)

    counts = cnt[:, 0].astype(I32)
    padded = ((counts + TE - 1) // TE) * TE
    ends = jnp.cumsum(padded)
    offs = ends - padded
    expert_ids = jnp.arange(N_EXPERTS, dtype=I32)[:, None, None]
    pair_offs = jnp.sum(jnp.where(ri[0:2][None] == expert_ids, offs[:, None, None], 0), axis=0)
    pos = (pair_offs + ri[2:4]).reshape(2 * TOKENS)
    n_tiles = (ends[-1] // TE).reshape(1)
    tile_ids = jnp.arange(MAX_TILES, dtype=I32)
    tile_expert = jnp.minimum(
        jnp.sum((tile_ids[:, None] >= (ends // TE)[None, :]).astype(I32), axis=1), N_EXPERTS - 1)

    xs = _dispatch_call(pos, x1, jnp.zeros((ROWS, D_MODEL), F32))
    ys = _expert_call(tile_expert, n_tiles, xs, w_gate.astype(BF16), w_up.astype(BF16), w_down.astype(BF16))
    out = _combine_call(pos, x1, rf, ln2_g.reshape(1, D_MODEL), ln2_b.reshape(1, D_MODEL), ys)
    return out.reshape(BATCH, SEQ, D_MODEL)


def kernel(x, w_in, conv_w, w_out_conv, w_out_attn, w_o, ln1_g, ln1_b, w_router_group, w_router_expert, w_gate, w_up, w_down, ln2_g, ln2_b):
    depth = w_in.shape[0]
    for l in range(depth):
        x = _layer(x, w_in[l], conv_w[l], w_out_conv[l], w_out_attn[l], w_o[l], ln1_g[l], ln1_b[l],
                   w_router_group[l], w_router_expert[l], w_gate[l], w_up[l], w_down[l], ln2_g[l], ln2_b[l])
    return x
```

```python
import functools

import jax
import jax.numpy as jnp
from jax import lax
from jax.experimental import pallas as pl
from jax.experimental.pallas import tpu as pltpu

F32 = jnp.float32
BF16 = jnp.bfloat16
U32 = jnp.uint32
I32 = jnp.int32

D_MODEL = 1024
BATCH = 8
SEQ = 2048
TOKENS = BATCH * SEQ
CONV_WIDTH = 512
N_HEADS = 8
HEAD_DIM = 64
ATTN_WIDTH = N_HEADS * HEAD_DIM
MOBA_BLOCK = 256
N_BLOCKS = SEQ // MOBA_BLOCK
MOBA_TOPK = 3
N_GROUPS = 4
EXPERTS_PER_GROUP = 8
N_EXPERTS = N_GROUPS * EXPERTS_PER_GROUP
D_EXPERT = 256
LN_EPS = 1e-5
ALPHA = 2.0 ** 0.25
IN_COLS = 3 * CONV_WIDTH + 3 * ATTN_WIDTH + 2 * D_MODEL
HALF = D_MODEL // 2

TM = 256
TE = 256
PV_ROWS = HEAD_DIM + 16
GRAN = 8
TILE_GRANS = TE // GRAN
N_TOK_TILES = TOKENS // TM
LOCAL_ROWS = -(-(2 * TM + N_EXPERTS * (GRAN - 1)) // 256) * 256
LOCAL_GRANS = LOCAL_ROWS // GRAN
SPARE_GRAN = N_TOK_TILES * LOCAL_GRANS
MAX_TILES = (2 * TOKENS + N_TOK_TILES * N_EXPERTS * (GRAN - 1)) // TE + N_EXPERTS
ROUTER_ROWS = 40
VMEM_LIMIT = 56 * 1024 * 1024
NEG_INF = float("-inf")


def _sigmoid(z):
    return 1.0 / (1.0 + jnp.exp(-z))


def _proj_kernel(x_ref, w_in_ref, convw_ref, woc_ref, q_ref, k_ref, v_ref, za_ref, sgb_ref, ubuf):
    s = pl.program_id(1)
    xb = x_ref[0].astype(BF16)

    def proj(c0, c1):
        return jnp.dot(xb, w_in_ref[:, c0:c1], preferred_element_type=F32)

    c_b = proj(0, CONV_WIDTH)
    u = proj(CONV_WIDTH, 2 * CONV_WIDTH) * proj(2 * CONV_WIDTH, 3 * CONV_WIDTH)

    @pl.when(s == 0)
    def _():
        ubuf[0:8, :] = jnp.zeros((8, CONV_WIDTH), F32)

    ubuf[8:8 + TM, :] = u
    w = convw_ref[...]
    conv = w[2:3, :] * u + w[1:2, :] * ubuf[7:7 + TM, :] + w[0:1, :] * ubuf[6:6 + TM, :]
    ubuf[0:8, :] = u[TM - 8:TM, :]
    hc = (c_b * conv).astype(BF16)
    y_conv = jnp.dot(hc, woc_ref[...], preferred_element_type=F32)

    o = 3 * CONV_WIDTH
    q_ref[0] = (proj(o, o + ATTN_WIDTH) * (HEAD_DIM ** -0.5)).astype(BF16)
    k_ref[0] = proj(o + ATTN_WIDTH, o + 2 * ATTN_WIDTH).astype(BF16)
    v_ref[0] = proj(o + 2 * ATTN_WIDTH, o + 3 * ATTN_WIDTH).astype(BF16)
    o += 3 * ATTN_WIDTH
    za_ref[0] = _sigmoid(proj(o, o + D_MODEL)) * y_conv
    sgb_ref[0] = _sigmoid(proj(o + D_MODEL, o + 2 * D_MODEL))


def _proj_call(x, w_in, conv_w, w_out_conv):
    tok_spec = lambda c: pl.BlockSpec((1, TM, c), lambda b, s: (b, s, 0))
    full = lambda shape: pl.BlockSpec(shape, lambda b, s: (0,) * len(shape))
    return pl.pallas_call(
        _proj_kernel,
        grid=(BATCH, SEQ // TM),
        in_specs=[tok_spec(D_MODEL), full((D_MODEL, IN_COLS)), full((3, CONV_WIDTH)),
                  full((CONV_WIDTH, D_MODEL))],
        out_specs=[tok_spec(ATTN_WIDTH), tok_spec(ATTN_WIDTH), tok_spec(ATTN_WIDTH),
                   tok_spec(D_MODEL), tok_spec(D_MODEL)],
        out_shape=[jax.ShapeDtypeStruct((BATCH, SEQ, ATTN_WIDTH), BF16)] * 3
        + [jax.ShapeDtypeStruct((BATCH, SEQ, D_MODEL), F32)] * 2,
        scratch_shapes=[pltpu.VMEM((TM + 8, CONV_WIDTH), F32)],
        compiler_params=pltpu.CompilerParams(
            dimension_semantics=("arbitrary", "arbitrary"), vmem_limit_bytes=VMEM_LIMIT),
        name="proj",
    )(x, w_in, conv_w, w_out_conv)


def _attn_kernel(slopes_ref, qa_ref, qb_ref, k_ref, v_ref, oa_ref, ob_ref,
                 kaug_ref, vt_ref, kmean_ref, qaug_ref, pv_ref, mloc_ref):
    hp = pl.program_id(1)
    j = pl.program_id(2)
    blk = MOBA_BLOCK

    @pl.when(j == 0)
    def _():
        klane = lax.broadcasted_iota(I32, (blk, 128), 1)
        koff = lax.broadcasted_iota(I32, (blk, 128), 0).astype(F32)
        k_extra = jnp.where(klane == 0, koff, jnp.where(klane == 1, 1.0, 0.0)).astype(BF16)
        orow = lax.broadcasted_iota(I32, (PV_ROWS - HEAD_DIM, blk), 0)
        ones_rows = jnp.where(orow == 0, 1.0, 0.0).astype(BF16)
        for n in range(N_BLOCKS):
            kblk = k_ref[0, n * blk:(n + 1) * blk, :]
            kaug_ref[n, :, 0:128] = kblk
            kaug_ref[n, :, 128:256] = k_extra
            kmean_ref[n:n + 1, :] = jnp.mean(kblk.astype(F32), axis=0, keepdims=True)
            v_t = v_ref[0, n * blk:(n + 1) * blk, :].astype(F32).T.astype(BF16)
            for hh in range(2):
                vt_ref[n, hh, 0:HEAD_DIM, :] = v_t[hh * HEAD_DIM:(hh + 1) * HEAD_DIM, :]
                vt_ref[n, hh, HEAD_DIM:PV_ROWS, :] = ones_rows

    lane = lax.broadcasted_iota(I32, (1, 2 * blk), 1)
    slope_row = jnp.where(lane < blk, slopes_ref[2 * hp], slopes_ref[2 * hp + 1])
    qoff_row = jnp.where(lane < blk, lane, lane - blk).astype(F32)
    feat = lax.broadcasted_iota(I32, (2 * HEAD_DIM, blk), 0)
    arow = lax.broadcasted_iota(I32, (2 * HEAD_DIM, 2 * blk), 0)
    q_extra = jnp.where(arow == 0, slope_row, jnp.where(arow == 1, -slope_row * qoff_row, 0.0)).astype(BF16)
    blk_i = lax.broadcasted_iota(I32, (N_BLOCKS, 2 * blk), 0)
    kmean = kmean_ref[...].astype(BF16)
    key_i = lax.broadcasted_iota(I32, (blk, 2 * blk), 0)
    qry_j = lax.broadcasted_iota(I32, (blk, 2 * blk), 1)
    causal = key_i <= jnp.where(qry_j < blk, qry_j, qry_j - blk)

    def prepare(q_ref, slot, qblock):
        q_t = q_ref[0].astype(F32).T
        qcat = jnp.concatenate([jnp.where(feat < HEAD_DIM, q_t, 0.0), jnp.where(feat >= HEAD_DIM, q_t, 0.0)],
                               axis=1).astype(BF16)
        qaug_ref[slot, 0:2 * HEAD_DIM, :] = qcat
        qaug_ref[slot, 2 * HEAD_DIM:4 * HEAD_DIM, :] = q_extra
        gate = jnp.dot(kmean, qcat, preferred_element_type=F32)
        cnt = jnp.zeros((N_BLOCKS, 2 * blk), F32)
        for m in range(N_BLOCKS):
            gm = gate[m:m + 1, :]
            beats = (gm > gate) | ((gm == gate) & (blk_i > m))
            cnt = cnt + jnp.where(beats & (qblock > m), 1.0, 0.0)
        return jnp.where((blk_i < qblock) & (cnt < float(MOBA_TOPK)), 1.0, 0.0)

    qblock_a = j
    qblock_b = N_BLOCKS - 1 - j
    sel_a = prepare(qa_ref, 0, qblock_a)
    sel_b = prepare(qb_ref, 1, qblock_b)

    def block_softmax(s, which, kb, own):
        t = jnp.dot(kaug_ref[kb], qaug_ref[which], preferred_element_type=F32)
        if own:
            t = jnp.where(causal, t, NEG_INF)
        m_loc = jnp.max(t, axis=0, keepdims=True)
        p = jnp.exp((t - m_loc).astype(BF16))
        pv_ref[s, 0] = jnp.dot(vt_ref[kb, 0], p[:, 0:blk], preferred_element_type=F32)
        pv_ref[s, 1] = jnp.dot(vt_ref[kb, 1], p[:, blk:2 * blk], preferred_element_type=F32)
        mloc_ref[s:s + 1, :] = m_loc

    n_mid = N_BLOCKS - 1
    block_softmax(0, 0, qblock_a, True)
    mids = []
    for s in range(1, n_mid + 1):
        is_a = s <= j
        which = jnp.where(is_a, 0, 1)
        kb = jnp.where(is_a, s - 1, s - 1 - j)
        block_softmax(s, which, kb, False)
        mids.append((is_a, kb))
    block_softmax(n_mid + 1, 1, qblock_b, True)

    def combine(o_ref, own_slot, sel, qblock, mine):
        neg = jnp.full((1, 2 * blk), -1e30, F32)
        pieces = [(own_slot, mloc_ref[own_slot:own_slot + 1, :])]
        for s, (is_a, kb) in enumerate(mids, start=1):
            selrow = jnp.sum(jnp.where(blk_i == kb, sel, 0.0), axis=0, keepdims=True)
            belongs = jnp.where(is_a, 1.0, 0.0) if mine else jnp.where(is_a, 0.0, 1.0)
            used = selrow * belongs > 0.5
            shift = slope_row * ((kb - qblock) * blk).astype(F32)
            pieces.append((s, jnp.where(used, mloc_ref[s:s + 1, :] + shift, neg)))
        m_all = pieces[0][1]
        for _, m_s in pieces[1:]:
            m_all = jnp.maximum(m_all, m_s)
        acc = [jnp.zeros((PV_ROWS, blk), F32), jnp.zeros((PV_ROWS, blk), F32)]
        for s, m_s in pieces:
            w = jnp.exp(m_s - m_all)
            for hh in range(2):
                acc[hh] = acc[hh] + pv_ref[s, hh] * w[:, hh * blk:(hh + 1) * blk]
        o_t = jnp.concatenate([a[0:HEAD_DIM, :] / a[HEAD_DIM:HEAD_DIM + 1, :] for a in acc], axis=0)
        o_ref[0] = o_t.T.astype(BF16)

    combine(oa_ref, 0, sel_a, qblock_a, True)
    combine(ob_ref, n_mid + 1, sel_b, qblock_b, False)


def _attn_call(slopes, q, k, v):
    half = N_BLOCKS // 2
    o_a, o_b = pl.pallas_call(
        _attn_kernel,
        grid_spec=pltpu.PrefetchScalarGridSpec(
            num_scalar_prefetch=1,
            grid=(BATCH, N_HEADS // 2, half),
            in_specs=[
                pl.BlockSpec((1, MOBA_BLOCK, 128), lambda b, h, j, sl: (b, j, h)),
                pl.BlockSpec((1, MOBA_BLOCK, 128), lambda b, h, j, sl: (b, N_BLOCKS - 1 - j, h)),
                pl.BlockSpec((1, SEQ, 128), lambda b, h, j, sl: (b, 0, h)),
                pl.BlockSpec((1, SEQ, 128), lambda b, h, j, sl: (b, 0, h)),
            ],
            out_specs=[pl.BlockSpec((1, MOBA_BLOCK, 128), lambda b, h, j, sl: (b, j, h)),
                       pl.BlockSpec((1, MOBA_BLOCK, 128), lambda b, h, j, sl: (b, j, h))],
            scratch_shapes=[
                pltpu.VMEM((N_BLOCKS, MOBA_BLOCK, 256), BF16),
                pltpu.VMEM((N_BLOCKS, 2, PV_ROWS, MOBA_BLOCK), BF16),
                pltpu.VMEM((N_BLOCKS, 128), F32),
                pltpu.VMEM((2, 256, 2 * MOBA_BLOCK), BF16),
                pltpu.VMEM((N_BLOCKS + 1, 2, PV_ROWS, MOBA_BLOCK), F32),
                pltpu.VMEM((16, 2 * MOBA_BLOCK), F32),
            ],
        ),
        out_shape=[jax.ShapeDtypeStruct((BATCH, SEQ // 2, ATTN_WIDTH), BF16)] * 2,
        compiler_params=pltpu.CompilerParams(
            dimension_semantics=("arbitrary", "arbitrary", "arbitrary"), vmem_limit_bytes=VMEM_LIMIT),
        name="moba_attn",
    )(slopes, q, q, k, v)
    o_b = o_b.reshape(BATCH, half, MOBA_BLOCK, ATTN_WIDTH)[:, ::-1].reshape(BATCH, SEQ // 2, ATTN_WIDTH)
    return jnp.concatenate([o_a, o_b], axis=1)


def _merge_kernel(o_ref, za_ref, sgb_ref, x_ref, woa_ref, wo_ref, g_ref, b_ref, wr_hi_ref, wr_lo_ref,
                  x1_ref, xs_ref, rf_ref, mt_ref):
    y_attn = jnp.dot(o_ref[...], woa_ref[...], preferred_element_type=F32)
    y = za_ref[...] + sgb_ref[...] * y_attn
    mix = jnp.dot(y.astype(BF16), wo_ref[...], preferred_element_type=F32)
    h = ALPHA * x_ref[...] + mix
    mu = jnp.mean(h, axis=-1, keepdims=True)
    hc = h - mu
    var = jnp.mean(hc * hc, axis=-1, keepdims=True)
    x1 = hc * lax.rsqrt(var + LN_EPS) * g_ref[...] + b_ref[...]
    x1_ref[...] = x1

    xh = x1.astype(BF16)
    xl = (x1 - xh.astype(F32)).astype(BF16)
    wh = wr_hi_ref[...]
    logits = (jnp.dot(xh, wh, preferred_element_type=F32)
              + jnp.dot(xl, wh, preferred_element_type=F32)
              + jnp.dot(xh, wr_lo_ref[...], preferred_element_type=F32)).T

    row8 = lax.broadcasted_iota(I32, (8, TM), 0).astype(F32)
    gl = jnp.where(row8 < float(N_GROUPS), logits[0:8, :], NEG_INF)
    gexp = jnp.exp(gl - jnp.max(gl, axis=0, keepdims=True))
    gprob = gexp / jnp.sum(gexp, axis=0, keepdims=True)
    ptop = jnp.max(gprob, axis=0, keepdims=True)
    gtop = jnp.min(jnp.where(gprob == ptop, row8, 8.0), axis=0, keepdims=True)
    el = logits[8:ROUTER_ROWS, :]
    eg = jnp.where(gtop == 0.0, el[0:8, :],
                   jnp.where(gtop == 1.0, el[8:16, :], jnp.where(gtop == 2.0, el[16:24, :], el[24:32, :])))
    m1 = jnp.max(eg, axis=0, keepdims=True)
    i1 = jnp.min(jnp.where(eg == m1, row8, 8.0), axis=0, keepdims=True)
    eg2 = jnp.where(row8 == i1, NEG_INF, eg)
    m2 = jnp.max(eg2, axis=0, keepdims=True)
    i2 = jnp.min(jnp.where(eg2 == m2, row8, 8.0), axis=0, keepdims=True)
    t2 = jnp.exp(m2 - m1)
    gate1 = ptop * (1.0 / (1.0 + t2))
    gate2 = ptop * (t2 / (1.0 + t2))
    e1 = gtop * float(EXPERTS_PER_GROUP) + i1
    e2 = gtop * float(EXPERTS_PER_GROUP) + i2

    erow = lax.broadcasted_iota(I32, (N_EXPERTS, TM), 0).astype(F32)
    oh1 = jnp.where(erow == e1, 1.0, 0.0)
    oh2 = jnp.where(erow == e2, 1.0, 0.0)
    oh = oh1 + oh2
    ta = lax.broadcasted_iota(I32, (TM, TM), 0)
    tb = lax.broadcasted_iota(I32, (TM, TM), 1)
    upper = jnp.where(ta < tb, 1.0, 0.0).astype(BF16)
    cum = jnp.dot(oh.astype(BF16), upper, preferred_element_type=F32)

    n_e = jnp.sum(oh, axis=1, keepdims=True)
    m_e = jnp.floor((n_e + float(GRAN - 1)) * (1.0 / GRAN))
    m_rep = jnp.broadcast_to(m_e, (N_EXPERTS, 128))
    ea = lax.broadcasted_iota(I32, (N_EXPERTS, N_EXPERTS), 0)
    eb = lax.broadcasted_iota(I32, (N_EXPERTS, N_EXPERTS), 1)
    lower = jnp.where(eb < ea, 1.0, 0.0).astype(BF16)
    run_start = jnp.dot(lower, m_rep.astype(BF16), preferred_element_type=F32)
    tot = cum + float(GRAN) * run_start[:, 0:1]
    lp1 = jnp.sum(oh1 * tot, axis=0, keepdims=True)
    lp2 = jnp.sum(oh2 * tot, axis=0, keepdims=True)
    lrow = lax.broadcasted_iota(I32, (LOCAL_ROWS, TM), 0).astype(F32)
    perm = jnp.where((lrow == lp1) | (lrow == lp2), 1.0, 0.0).astype(BF16)
    xs_ref[...] = jnp.dot(perm, xh, preferred_element_type=F32)

    zero = jnp.zeros((1, TM), F32)
    rf_ref[...] = jnp.concatenate([gate1, gate2, lp1, lp2, zero, zero, zero, zero], axis=0)
    mt_ref[...] = m_rep


def _merge_call(o, za, sgb, x, woa, wo, g, b, wr_hi, wr_lo):
    tok = lambda c: pl.BlockSpec((TM, c), lambda i: (i, 0))
    full = lambda shape: pl.BlockSpec(shape, lambda i: (0,) * len(shape))
    lanes = pl.BlockSpec((8, TM), lambda i: (0, i))
    return pl.pallas_call(
        _merge_kernel,
        grid=(TOKENS // TM,),
        in_specs=[tok(ATTN_WIDTH), tok(D_MODEL), tok(D_MODEL), tok(D_MODEL),
                  full((ATTN_WIDTH, D_MODEL)), full((D_MODEL, D_MODEL)), full((1, D_MODEL)),
                  full((1, D_MODEL)), full((D_MODEL, 128)), full((D_MODEL, 128))],
        out_specs=[tok(D_MODEL), pl.BlockSpec((LOCAL_ROWS, D_MODEL), lambda i: (i, 0)), lanes,
                   pl.BlockSpec((N_EXPERTS, 128), lambda i: (i, 0))],
        out_shape=[jax.ShapeDtypeStruct((TOKENS, D_MODEL), F32),
                   jax.ShapeDtypeStruct((N_TOK_TILES * LOCAL_ROWS, D_MODEL), F32),
                   jax.ShapeDtypeStruct((8, TOKENS), F32),
                   jax.ShapeDtypeStruct((N_TOK_TILES * N_EXPERTS, 128), F32)],
        compiler_params=pltpu.CompilerParams(
            dimension_semantics=("arbitrary",), vmem_limit_bytes=VMEM_LIMIT),
        name="merge_ln1_route",
    )(o, za, sgb, x, woa, wo, g, b, wr_hi, wr_lo)


def _granule_copy(src_ref, src_gran, dst_ref, dst_gran, sem):
    src = pl.multiple_of(src_gran * GRAN, GRAN)
    dst = pl.multiple_of(dst_gran * GRAN, GRAN)
    return pltpu.make_async_copy(src_ref.at[pl.ds(src, GRAN), :], dst_ref.at[pl.ds(dst, GRAN), :], sem)


def _expert_kernel(te_ref, nt_ref, gsrc_ref, gdst_ref, ug_ref, xs_ref, wg_ref, wu_ref, wd_ref, ys_ref,
                   xbuf, ybuf, zbuf, in_sem, out_sem, zero_sem):
    j = pl.program_id(0)
    n_tiles = nt_ref[0]
    slot = lax.rem(j, 2)

    def tile_gather(tile, s):
        for g in range(TILE_GRANS):
            _granule_copy(xs_ref, gsrc_ref[tile * TILE_GRANS + g], xbuf.at[s], g, in_sem.at[s]).start()

    @pl.when(j == 0)
    def _():
        tile_gather(0, 0)
        ybuf[1] = jnp.zeros((TE, D_MODEL), F32)
        zbuf[...] = jnp.zeros((GRAN, D_MODEL), F32)
        for half in range(2):
            spare = pltpu.make_async_copy(
                ybuf.at[1], ys_ref.at[pl.ds((SPARE_GRAN + half * TILE_GRANS) * GRAN, TE), :], out_sem.at[1])
            spare.start()
            spare.wait()

    @pl.when(j < N_TOK_TILES)
    def _():
        def zero_copy(g):
            return _granule_copy(zbuf, 0, ys_ref, j * LOCAL_GRANS + g, zero_sem)

        def start(g, c):
            zero_copy(g).start()
            return c

        def wait(g, c):
            zero_copy(g).wait()
            return c

        lax.fori_loop(ug_ref[j], LOCAL_GRANS, start, 0)
        lax.fori_loop(ug_ref[j], LOCAL_GRANS, wait, 0)

    @pl.when(j + 1 < n_tiles)
    def _():
        tile_gather(j + 1, 1 - slot)

    @pl.when(jnp.logical_and(j >= 2, j - 2 < n_tiles))
    def _():
        pltpu.make_async_copy(ybuf.at[slot], ys_ref.at[pl.ds(0, TE), :], out_sem.at[slot]).wait()

    @pl.when(j < n_tiles)
    def _():
        pltpu.make_async_copy(xs_ref.at[pl.ds(0, TE), :], xbuf.at[slot], in_sem.at[slot]).wait()
        xb = xbuf[slot].astype(BF16)
        hg = jnp.dot(xb, wg_ref[0], preferred_element_type=F32)
        hu = jnp.dot(xb, wu_ref[0], preferred_element_type=F32)
        h = (hg * _sigmoid(hg) * hu).astype(BF16)
        ybuf[slot] = jnp.dot(h, wd_ref[0], preferred_element_type=F32)
        for g in range(TILE_GRANS):
            _granule_copy(ybuf.at[slot], g, ys_ref, gdst_ref[j * TILE_GRANS + g], out_sem.at[slot]).start()


def _expert_call(tile_expert, n_tiles, gsrc, gdst, used_grans, xs, wg, wu, wd):
    wsel = lambda j, te, nt, gs, gd, ug: (te[j], 0, 0)
    return pl.pallas_call(
        _expert_kernel,
        grid_spec=pltpu.PrefetchScalarGridSpec(
            num_scalar_prefetch=5,
            grid=(MAX_TILES + 2,),
            in_specs=[pl.BlockSpec(memory_space=pl.ANY),
                      pl.BlockSpec((1, D_MODEL, D_EXPERT), wsel),
                      pl.BlockSpec((1, D_MODEL, D_EXPERT), wsel),
                      pl.BlockSpec((1, D_EXPERT, D_MODEL), wsel)],
            out_specs=pl.BlockSpec(memory_space=pl.ANY),
            scratch_shapes=[pltpu.VMEM((2, TE, D_MODEL), F32), pltpu.VMEM((2, TE, D_MODEL), F32),
                            pltpu.VMEM((GRAN, D_MODEL), F32),
                            pltpu.SemaphoreType.DMA((2,)), pltpu.SemaphoreType.DMA((2,)),
                            pltpu.SemaphoreType.DMA],
        ),
        out_shape=jax.ShapeDtypeStruct(((SPARE_GRAN + 2 * TILE_GRANS) * GRAN, D_MODEL), F32),
        compiler_params=pltpu.CompilerParams(
            dimension_semantics=("arbitrary",), vmem_limit_bytes=VMEM_LIMIT),
        name="experts",
    )(tile_expert, n_tiles, gsrc, gdst, used_grans, xs, wg, wu, wd)


def _combine_kernel(ys_ref, x1_ref, rf_ref, g_ref, b_ref, out_ref):
    route = rf_ref[...].T
    col = lax.broadcasted_iota(I32, (TM, LOCAL_ROWS), 1).astype(F32)
    unsort = (jnp.where(col == route[:, 2:3], route[:, 0:1], 0.0)
              + jnp.where(col == route[:, 3:4], route[:, 1:2], 0.0)).astype(BF16)
    ffn = jnp.dot(unsort, ys_ref[...].astype(BF16), preferred_element_type=F32)
    h = ALPHA * x1_ref[...] + ffn
    mu = jnp.mean(h, axis=-1, keepdims=True)
    hc = h - mu
    var = jnp.mean(hc * hc, axis=-1, keepdims=True)
    out_ref[...] = hc * lax.rsqrt(var + LN_EPS) * g_ref[...] + b_ref[...]


def _combine_call(ys, x1, rf, g, b):
    return pl.pallas_call(
        _combine_kernel,
        grid=(N_TOK_TILES,),
        in_specs=[pl.BlockSpec((LOCAL_ROWS, D_MODEL), lambda i: (i, 0)),
                  pl.BlockSpec((TM, D_MODEL), lambda i: (i, 0)),
                  pl.BlockSpec((8, TM), lambda i: (0, i)),
                  pl.BlockSpec((1, D_MODEL), lambda i: (0, 0)),
                  pl.BlockSpec((1, D_MODEL), lambda i: (0, 0))],
        out_specs=pl.BlockSpec((TM, D_MODEL), lambda i: (i, 0)),
        out_shape=jax.ShapeDtypeStruct((TOKENS, D_MODEL), F32),
        compiler_params=pltpu.CompilerParams(
            dimension_semantics=("arbitrary",), vmem_limit_bytes=VMEM_LIMIT),
        name="combine_ln2",
    )(ys, x1, rf, g, b)


def _router_cols(w_router_group, w_router_expert):
    w = jnp.concatenate([w_router_group, jnp.zeros((D_MODEL, 4), F32), w_router_expert,
                         jnp.zeros((D_MODEL, 128 - ROUTER_ROWS), F32)], axis=1)
    hi = w.astype(BF16)
    lo = (w - hi.astype(F32)).astype(BF16)
    return hi, lo


def _layer(x, w_in, conv_w, w_out_conv, w_out_attn, w_o, ln1_g, ln1_b,
           w_router_group, w_router_expert, w_gate, w_up, w_down, ln2_g, ln2_b):
    slopes = jnp.asarray([2.0 ** (-8.0 * (h + 1) / N_HEADS) for h in range(N_HEADS)], F32)
    q, k, v, za, sgb = _proj_call(x, w_in.astype(BF16), conv_w, w_out_conv.astype(BF16))
    o = _attn_call(slopes, q, k, v)

    wr_hi, wr_lo = _router_cols(w_router_group, w_router_expert)
    x1, xs, rf, mt = _merge_call(
        o.reshape(TOKENS, ATTN_WIDTH), za.reshape(TOKENS, D_MODEL), sgb.reshape(TOKENS, D_MODEL),
        x.reshape(TOKENS, D_MODEL), w_out_attn.astype(BF16), w_o.astype(BF16),
        ln1_g.reshape(1, D_MODEL), ln1_b.reshape(1, D_MODEL), wr_hi, wr_lo)

    grans = mt.reshape(N_TOK_TILES, N_EXPERTS, 128)[:, :, 0].astype(I32)
    local_start = jnp.cumsum(grans, axis=1) - grans
    grans_t = grans.T
    tiles_e = (jnp.sum(grans_t, axis=1) + TILE_GRANS - 1) // TILE_GRANS
    tile_end = jnp.cumsum(tiles_e)
    n_tiles = tile_end[-1].reshape(1)
    tile_ids = jnp.arange(MAX_TILES + 2, dtype=I32)
    tile_expert = jnp.minimum(
        jnp.sum((tile_ids[:, None] >= tile_end[None, :]).astype(I32), axis=1), N_EXPERTS - 1)
    run_slot = (TILE_GRANS * (tile_end - tiles_e)[:, None] + jnp.cumsum(grans_t, axis=1) - grans_t).reshape(-1)
    run_len = grans_t.reshape(-1)
    run_src = (jnp.arange(N_TOK_TILES, dtype=I32)[None, :] * LOCAL_GRANS + local_start.T).reshape(-1)
    slots = jnp.arange(MAX_TILES * TILE_GRANS, dtype=I32)
    k = slots[:, None] - run_slot[None, :]
    hit = (k >= 0) & (k < run_len[None, :])
    gran = jnp.sum(jnp.where(hit, run_src[None, :] + k, 0), axis=1)
    filled = jnp.sum(hit.astype(I32), axis=1) > 0
    gsrc = jnp.where(filled, gran, 0)
    gdst = jnp.where(filled, gran, SPARE_GRAN + slots % (2 * TILE_GRANS))

    ys = _expert_call(tile_expert, n_tiles, gsrc, gdst, jnp.sum(grans, axis=1), xs,
                      w_gate.astype(BF16), w_up.astype(BF16), w_down.astype(BF16))
    out = _combine_call(ys, x1, rf, ln2_g.reshape(1, D_MODEL), ln2_b.reshape(1, D_MODEL))
    return out.reshape(BATCH, SEQ, D_MODEL)


def kernel(x, w_in, conv_w, w_out_conv, w_out_attn, w_o, ln1_g, ln1_b, w_router_group, w_router_expert, w_gate, w_up, w_down, ln2_g, ln2_b):
    depth = w_in.shape[0]
    for l in range(depth):
        x = _layer(x, w_in[l], conv_w[l], w_out_conv[l], w_out_attn[l], w_o[l], ln1_g[l], ln1_b[l],
                   w_router_group[l], w_router_expert[l], w_gate[l], w_up[l], w_down[l], ln2_g[l], ln2_b[l])
    return x
```

```python
import functools

import jax
import jax.numpy as jnp
from jax import lax
from jax.experimental import pallas as pl
from jax.experimental.pallas import tpu as pltpu

F32 = jnp.float32
BF16 = jnp.bfloat16
U32 = jnp.uint32
I32 = jnp.int32

D_MODEL = 1024
BATCH = 8
SEQ = 2048
TOKENS = BATCH * SEQ
CONV_WIDTH = 512
N_HEADS = 8
HEAD_DIM = 64
ATTN_WIDTH = N_HEADS * HEAD_DIM
MOBA_BLOCK = 256
N_BLOCKS = SEQ // MOBA_BLOCK
MOBA_TOPK = 3
N_GROUPS = 4
EXPERTS_PER_GROUP = 8
N_EXPERTS = N_GROUPS * EXPERTS_PER_GROUP
D_EXPERT = 256
LN_EPS = 1e-5
ALPHA = 2.0 ** 0.25
IN_COLS = 3 * CONV_WIDTH + 3 * ATTN_WIDTH + 2 * D_MODEL
HALF = D_MODEL // 2

TM = 256
TE = 256
PV_ROWS = HEAD_DIM + 16
GRAN = 8
TILE_GRANS = TE // GRAN
N_TOK_TILES = TOKENS // TM
LOCAL_ROWS = -(-(2 * TM + N_EXPERTS * (GRAN - 1)) // 256) * 256
LOCAL_GRANS = LOCAL_ROWS // GRAN
SPARE_GRAN = N_TOK_TILES * LOCAL_GRANS
MAX_TILES = (2 * TOKENS + N_TOK_TILES * N_EXPERTS * (GRAN - 1)) // TE + N_EXPERTS
ROUTER_ROWS = 40
VMEM_LIMIT = 56 * 1024 * 1024
NEG_INF = float("-inf")


def _sigmoid(z):
    return 1.0 / (1.0 + jnp.exp(-z))


def _proj_kernel(x_ref, w_in_f32_ref, convw_ref, woc_f32_ref, q_ref, k_ref, v_ref, za_ref, sgb_ref,
                 ubuf, w_in_ref, woc_ref):
    s = pl.program_id(1)

    @pl.when((pl.program_id(0) == 0) & (s == 0))
    def _():
        for c in range(0, IN_COLS, CONV_WIDTH):
            w_in_ref[:, c:c + CONV_WIDTH] = w_in_f32_ref[:, c:c + CONV_WIDTH].astype(BF16)
        woc_ref[...] = woc_f32_ref[...].astype(BF16)

    xb = x_ref[0].astype(BF16)

    def proj(c0, c1):
        return jnp.dot(xb, w_in_ref[:, c0:c1], preferred_element_type=F32)

    c_b = proj(0, CONV_WIDTH)
    u = proj(CONV_WIDTH, 2 * CONV_WIDTH) * proj(2 * CONV_WIDTH, 3 * CONV_WIDTH)

    @pl.when(s == 0)
    def _():
        ubuf[0:8, :] = jnp.zeros((8, CONV_WIDTH), F32)

    ubuf[8:8 + TM, :] = u
    w = convw_ref[...]
    conv = w[2:3, :] * u + w[1:2, :] * ubuf[7:7 + TM, :] + w[0:1, :] * ubuf[6:6 + TM, :]
    ubuf[0:8, :] = u[TM - 8:TM, :]
    hc = (c_b * conv).astype(BF16)
    y_conv = jnp.dot(hc, woc_ref[...], preferred_element_type=F32)

    o = 3 * CONV_WIDTH
    q_ref[0] = (proj(o, o + ATTN_WIDTH) * (HEAD_DIM ** -0.5)).astype(BF16)
    k_ref[0] = proj(o + ATTN_WIDTH, o + 2 * ATTN_WIDTH).astype(BF16)
    v_ref[0] = proj(o + 2 * ATTN_WIDTH, o + 3 * ATTN_WIDTH).astype(BF16)
    o += 3 * ATTN_WIDTH
    za_ref[0] = _sigmoid(proj(o, o + D_MODEL)) * y_conv
    sgb_ref[0] = _sigmoid(proj(o + D_MODEL, o + 2 * D_MODEL))


def _proj_call(x, w_in, conv_w, w_out_conv):
    tok_spec = lambda c: pl.BlockSpec((1, TM, c), lambda b, s: (b, s, 0))
    full = lambda shape: pl.BlockSpec(shape, lambda b, s: (0,) * len(shape))
    once = lambda shape: pl.BlockSpec(shape, lambda b, s: (0,) * len(shape), pipeline_mode=pl.Buffered(1))
    return pl.pallas_call(
        _proj_kernel,
        grid=(BATCH, SEQ // TM),
        in_specs=[tok_spec(D_MODEL), once((D_MODEL, IN_COLS)), full((3, CONV_WIDTH)),
                  once((CONV_WIDTH, D_MODEL))],
        out_specs=[tok_spec(ATTN_WIDTH), tok_spec(ATTN_WIDTH), tok_spec(ATTN_WIDTH),
                   tok_spec(D_MODEL), tok_spec(D_MODEL)],
        out_shape=[jax.ShapeDtypeStruct((BATCH, SEQ, ATTN_WIDTH), BF16)] * 3
        + [jax.ShapeDtypeStruct((BATCH, SEQ, D_MODEL), F32)] * 2,
        scratch_shapes=[pltpu.VMEM((TM + 8, CONV_WIDTH), F32), pltpu.VMEM((D_MODEL, IN_COLS), BF16),
                        pltpu.VMEM((CONV_WIDTH, D_MODEL), BF16)],
        compiler_params=pltpu.CompilerParams(
            dimension_semantics=("arbitrary", "arbitrary"), vmem_limit_bytes=VMEM_LIMIT),
        name="proj",
    )(x, w_in, conv_w, w_out_conv)


def _attn_kernel(slopes_ref, qa_ref, qb_ref, k_ref, v_ref, oa_ref, ob_ref,
                 kaug_ref, vt_ref, kmean_ref, qaug_ref, pv_ref, mloc_ref):
    hp = pl.program_id(1)
    j = pl.program_id(2)
    blk = MOBA_BLOCK

    @pl.when(j == 0)
    def _():
        klane = lax.broadcasted_iota(I32, (blk, 128), 1)
        koff = lax.broadcasted_iota(I32, (blk, 128), 0).astype(F32)
        k_extra = jnp.where(klane == 0, koff, jnp.where(klane == 1, 1.0, 0.0)).astype(BF16)
        orow = lax.broadcasted_iota(I32, (PV_ROWS - HEAD_DIM, blk), 0)
        ones_rows = jnp.where(orow == 0, 1.0, 0.0).astype(BF16)
        for n in range(N_BLOCKS):
            kblk = k_ref[0, n * blk:(n + 1) * blk, :]
            kaug_ref[n, :, 0:128] = kblk
            kaug_ref[n, :, 128:256] = k_extra
            kmean_ref[n:n + 1, :] = jnp.mean(kblk.astype(F32), axis=0, keepdims=True)
            v_t = v_ref[0, n * blk:(n + 1) * blk, :].astype(F32).T.astype(BF16)
            for hh in range(2):
                vt_ref[n, hh, 0:HEAD_DIM, :] = v_t[hh * HEAD_DIM:(hh + 1) * HEAD_DIM, :]
                vt_ref[n, hh, HEAD_DIM:PV_ROWS, :] = ones_rows

    lane = lax.broadcasted_iota(I32, (1, 2 * blk), 1)
    slope_row = jnp.where(lane < blk, slopes_ref[2 * hp], slopes_ref[2 * hp + 1])
    qoff_row = jnp.where(lane < blk, lane, lane - blk).astype(F32)
    feat = lax.broadcasted_iota(I32, (2 * HEAD_DIM, blk), 0)
    arow = lax.broadcasted_iota(I32, (2 * HEAD_DIM, 2 * blk), 0)
    q_extra = jnp.where(arow == 0, slope_row, jnp.where(arow == 1, -slope_row * qoff_row, 0.0)).astype(BF16)
    blk_i = lax.broadcasted_iota(I32, (N_BLOCKS, 2 * blk), 0)
    kmean = kmean_ref[...].astype(BF16)
    key_i = lax.broadcasted_iota(I32, (blk, 2 * blk), 0)
    qry_j = lax.broadcasted_iota(I32, (blk, 2 * blk), 1)
    causal = key_i <= jnp.where(qry_j < blk, qry_j, qry_j - blk)

    def prepare(q_ref, slot, qblock):
        q_t = q_ref[0].astype(F32).T
        qcat = jnp.concatenate([jnp.where(feat < HEAD_DIM, q_t, 0.0), jnp.where(feat >= HEAD_DIM, q_t, 0.0)],
                               axis=1).astype(BF16)
        qaug_ref[slot, 0:2 * HEAD_DIM, :] = qcat
        qaug_ref[slot, 2 * HEAD_DIM:4 * HEAD_DIM, :] = q_extra
        gate = jnp.dot(kmean, qcat, preferred_element_type=F32)
        cnt = jnp.zeros((N_BLOCKS, 2 * blk), F32)
        for m in range(N_BLOCKS):
            gm = gate[m:m + 1, :]
            beats = (gm > gate) | ((gm == gate) & (blk_i > m))
            cnt = cnt + jnp.where(beats & (qblock > m), 1.0, 0.0)
        return jnp.where((blk_i < qblock) & (cnt < float(MOBA_TOPK)), 1.0, 0.0)

    qblock_a = j
    qblock_b = N_BLOCKS - 1 - j
    sel_a = prepare(qa_ref, 0, qblock_a)
    sel_b = prepare(qb_ref, 1, qblock_b)

    def block_softmax(s, which, kb, own):
        t = jnp.dot(kaug_ref[kb], qaug_ref[which], preferred_element_type=F32)
        if own:
            t = jnp.where(causal, t, NEG_INF)
        m_loc = jnp.max(t, axis=0, keepdims=True)
        p = jnp.exp((t - m_loc).astype(BF16))
        pv_ref[s, 0] = jnp.dot(vt_ref[kb, 0], p[:, 0:blk], preferred_element_type=F32)
        pv_ref[s, 1] = jnp.dot(vt_ref[kb, 1], p[:, blk:2 * blk], preferred_element_type=F32)
        mloc_ref[s:s + 1, :] = m_loc

    n_mid = N_BLOCKS - 1
    block_softmax(0, 0, qblock_a, True)
    mids = []
    for s in range(1, n_mid + 1):
        is_a = s <= j
        which = jnp.where(is_a, 0, 1)
        kb = jnp.where(is_a, s - 1, s - 1 - j)
        block_softmax(s, which, kb, False)
        mids.append((is_a, kb))
    block_softmax(n_mid + 1, 1, qblock_b, True)

    def combine(o_ref, own_slot, sel, qblock, mine):
        neg = jnp.full((1, 2 * blk), -1e30, F32)
        pieces = [(own_slot, mloc_ref[own_slot:own_slot + 1, :])]
        for s, (is_a, kb) in enumerate(mids, start=1):
            selrow = jnp.sum(jnp.where(blk_i == kb, sel, 0.0), axis=0, keepdims=True)
            belongs = jnp.where(is_a, 1.0, 0.0) if mine else jnp.where(is_a, 0.0, 1.0)
            used = selrow * belongs > 0.5
            shift = slope_row * ((kb - qblock) * blk).astype(F32)
            pieces.append((s, jnp.where(used, mloc_ref[s:s + 1, :] + shift, neg)))
        m_all = pieces[0][1]
        for _, m_s in pieces[1:]:
            m_all = jnp.maximum(m_all, m_s)
        acc = [jnp.zeros((PV_ROWS, blk), F32), jnp.zeros((PV_ROWS, blk), F32)]
        for s, m_s in pieces:
            w = jnp.exp(m_s - m_all)
            for hh in range(2):
                acc[hh] = acc[hh] + pv_ref[s, hh] * w[:, hh * blk:(hh + 1) * blk]
        o_t = jnp.concatenate([a[0:HEAD_DIM, :] / a[HEAD_DIM:HEAD_DIM + 1, :] for a in acc], axis=0)
        o_ref[0] = o_t.T.astype(BF16)

    combine(oa_ref, 0, sel_a, qblock_a, True)
    combine(ob_ref, n_mid + 1, sel_b, qblock_b, False)


def _attn_call(slopes, q, k, v):
    half = N_BLOCKS // 2
    o_a, o_b = pl.pallas_call(
        _attn_kernel,
        grid_spec=pltpu.PrefetchScalarGridSpec(
            num_scalar_prefetch=1,
            grid=(BATCH, N_HEADS // 2, half),
            in_specs=[
                pl.BlockSpec((1, MOBA_BLOCK, 128), lambda b, h, j, sl: (b, j, h)),
                pl.BlockSpec((1, MOBA_BLOCK, 128), lambda b, h, j, sl: (b, N_BLOCKS - 1 - j, h)),
                pl.BlockSpec((1, SEQ, 128), lambda b, h, j, sl: (b, 0, h)),
                pl.BlockSpec((1, SEQ, 128), lambda b, h, j, sl: (b, 0, h)),
            ],
            out_specs=[pl.BlockSpec((1, MOBA_BLOCK, 128), lambda b, h, j, sl: (b, j, h)),
                       pl.BlockSpec((1, MOBA_BLOCK, 128), lambda b, h, j, sl: (b, j, h))],
            scratch_shapes=[
                pltpu.VMEM((N_BLOCKS, MOBA_BLOCK, 256), BF16),
                pltpu.VMEM((N_BLOCKS, 2, PV_ROWS, MOBA_BLOCK), BF16),
                pltpu.VMEM((N_BLOCKS, 128), F32),
                pltpu.VMEM((2, 256, 2 * MOBA_BLOCK), BF16),
                pltpu.VMEM((N_BLOCKS + 1, 2, PV_ROWS, MOBA_BLOCK), F32),
                pltpu.VMEM((16, 2 * MOBA_BLOCK), F32),
            ],
        ),
        out_shape=[jax.ShapeDtypeStruct((BATCH, SEQ // 2, ATTN_WIDTH), BF16)] * 2,
        compiler_params=pltpu.CompilerParams(
            dimension_semantics=("arbitrary", "arbitrary", "arbitrary"), vmem_limit_bytes=VMEM_LIMIT),
        name="moba_attn",
    )(slopes, q, q, k, v)
    return o_a, o_b


def _merge_kernel(oa_ref, ob_ref, za_ref, sgb_ref, x_ref, woa_f32_ref, wo_f32_ref, g_ref, b_ref,
                  wr_hi_ref, wr_lo_ref, x1_ref, xs_ref, rf_ref, mt_ref, woa_ref, wo_ref):
    i = pl.program_id(0)

    @pl.when(i == 0)
    def _():
        woa_ref[...] = woa_f32_ref[...].astype(BF16)
        wo_ref[...] = wo_f32_ref[...].astype(BF16)

    o = jnp.where(lax.rem(i, N_BLOCKS) < N_BLOCKS // 2, oa_ref[0], ob_ref[0])
    y_attn = jnp.dot(o, woa_ref[...], preferred_element_type=F32)
    y = za_ref[...] + sgb_ref[...] * y_attn
    mix = jnp.dot(y.astype(BF16), wo_ref[...], preferred_element_type=F32)
    h = ALPHA * x_ref[...] + mix
    mu = jnp.mean(h, axis=-1, keepdims=True)
    hc = h - mu
    var = jnp.mean(hc * hc, axis=-1, keepdims=True)
    x1 = hc * lax.rsqrt(var + LN_EPS) * g_ref[...] + b_ref[...]
    x1_ref[...] = x1

    xh = x1.astype(BF16)
    xl = (x1 - xh.astype(F32)).astype(BF16)
    wh = wr_hi_ref[...]
    logits = (jnp.dot(xh, wh, preferred_element_type=F32)
              + jnp.dot(xl, wh, preferred_element_type=F32)
              + jnp.dot(xh, wr_lo_ref[...], preferred_element_type=F32)).T

    row8 = lax.broadcasted_iota(I32, (8, TM), 0).astype(F32)
    gl = jnp.where(row8 < float(N_GROUPS), logits[0:8, :], NEG_INF)
    gexp = jnp.exp(gl - jnp.max(gl, axis=0, keepdims=True))
    gprob = gexp / jnp.sum(gexp, axis=0, keepdims=True)
    ptop = jnp.max(gprob, axis=0, keepdims=True)
    gtop = jnp.min(jnp.where(gprob == ptop, row8, 8.0), axis=0, keepdims=True)
    el = logits[8:ROUTER_ROWS, :]
    eg = jnp.where(gtop == 0.0, el[0:8, :],
                   jnp.where(gtop == 1.0, el[8:16, :], jnp.where(gtop == 2.0, el[16:24, :], el[24:32, :])))
    m1 = jnp.max(eg, axis=0, keepdims=True)
    i1 = jnp.min(jnp.where(eg == m1, row8, 8.0), axis=0, keepdims=True)
    eg2 = jnp.where(row8 == i1, NEG_INF, eg)
    m2 = jnp.max(eg2, axis=0, keepdims=True)
    i2 = jnp.min(jnp.where(eg2 == m2, row8, 8.0), axis=0, keepdims=True)
    t2 = jnp.exp(m2 - m1)
    gate1 = ptop * (1.0 / (1.0 + t2))
    gate2 = ptop * (t2 / (1.0 + t2))
    e1 = gtop * float(EXPERTS_PER_GROUP) + i1
    e2 = gtop * float(EXPERTS_PER_GROUP) + i2

    erow = lax.broadcasted_iota(I32, (N_EXPERTS, TM), 0).astype(F32)
    oh1 = jnp.where(erow == e1, 1.0, 0.0)
    oh2 = jnp.where(erow == e2, 1.0, 0.0)
    oh = oh1 + oh2
    ta = lax.broadcasted_iota(I32, (TM, TM), 0)
    tb = lax.broadcasted_iota(I32, (TM, TM), 1)
    upper = jnp.where(ta < tb, 1.0, 0.0).astype(BF16)
    cum = jnp.dot(oh.astype(BF16), upper, preferred_element_type=F32)

    n_e = jnp.sum(oh, axis=1, keepdims=True)
    m_e = jnp.floor((n_e + float(GRAN - 1)) * (1.0 / GRAN))
    m_rep = jnp.broadcast_to(m_e, (N_EXPERTS, 128))
    ea = lax.broadcasted_iota(I32, (N_EXPERTS, N_EXPERTS), 0)
    eb = lax.broadcasted_iota(I32, (N_EXPERTS, N_EXPERTS), 1)
    lower = jnp.where(eb < ea, 1.0, 0.0).astype(BF16)
    run_start = jnp.dot(lower, m_rep.astype(BF16), preferred_element_type=F32)
    tot = cum + float(GRAN) * run_start[:, 0:1]
    lp1 = jnp.sum(oh1 * tot, axis=0, keepdims=True)
    lp2 = jnp.sum(oh2 * tot, axis=0, keepdims=True)
    lrow = lax.broadcasted_iota(I32, (LOCAL_ROWS, TM), 0).astype(F32)
    perm = jnp.where((lrow == lp1) | (lrow == lp2), 1.0, 0.0).astype(BF16)
    xs_ref[...] = jnp.dot(perm, xh, preferred_element_type=F32)

    zero = jnp.zeros((1, TM), F32)
    rf_ref[...] = jnp.concatenate([gate1, gate2, lp1, lp2, zero, zero, zero, zero], axis=0)
    mt_ref[...] = m_rep


def _merge_call(o_a, o_b, za, sgb, x, woa, wo, g, b, wr_hi, wr_lo):
    tok = lambda c: pl.BlockSpec((TM, c), lambda i: (i, 0))
    full = lambda shape: pl.BlockSpec(shape, lambda i: (0,) * len(shape))
    lanes = pl.BlockSpec((8, TM), lambda i: (0, i))
    half = N_BLOCKS // 2
    o_a_spec = pl.BlockSpec((1, TM, ATTN_WIDTH), lambda i: (i // N_BLOCKS, jnp.minimum(i % N_BLOCKS, half - 1), 0))
    o_b_spec = pl.BlockSpec(
        (1, TM, ATTN_WIDTH), lambda i: (i // N_BLOCKS, jnp.minimum(N_BLOCKS - 1 - i % N_BLOCKS, half - 1), 0))
    return pl.pallas_call(
        _merge_kernel,
        grid=(TOKENS // TM,),
        in_specs=[o_a_spec, o_b_spec, tok(D_MODEL), tok(D_MODEL), tok(D_MODEL),
                  full((ATTN_WIDTH, D_MODEL)), full((D_MODEL, D_MODEL)), full((1, D_MODEL)),
                  full((1, D_MODEL)), full((D_MODEL, 128)), full((D_MODEL, 128))],
        out_specs=[tok(D_MODEL), pl.BlockSpec((LOCAL_ROWS, D_MODEL), lambda i: (i, 0)), lanes,
                   pl.BlockSpec((N_EXPERTS, 128), lambda i: (i, 0))],
        out_shape=[jax.ShapeDtypeStruct((TOKENS, D_MODEL), F32),
                   jax.ShapeDtypeStruct((N_TOK_TILES * LOCAL_ROWS, D_MODEL), F32),
                   jax.ShapeDtypeStruct((8, TOKENS), F32),
                   jax.ShapeDtypeStruct((N_TOK_TILES * N_EXPERTS, 128), F32)],
        scratch_shapes=[pltpu.VMEM((ATTN_WIDTH, D_MODEL), BF16), pltpu.VMEM((D_MODEL, D_MODEL), BF16)],
        compiler_params=pltpu.CompilerParams(
            dimension_semantics=("arbitrary",), vmem_limit_bytes=VMEM_LIMIT),
        name="merge_ln1_route",
    )(o_a, o_b, za, sgb, x, woa, wo, g, b, wr_hi, wr_lo)


def _granule_copy(src_ref, src_gran, dst_ref, dst_gran, sem):
    src = pl.multiple_of(src_gran * GRAN, GRAN)
    dst = pl.multiple_of(dst_gran * GRAN, GRAN)
    return pltpu.make_async_copy(src_ref.at[pl.ds(src, GRAN), :], dst_ref.at[pl.ds(dst, GRAN), :], sem)


def _expert_kernel(te_ref, nt_ref, gsrc_ref, gdst_ref, ug_ref, xs_ref, wg_ref, wu_ref, wd_ref, ys_ref,
                   xbuf, ybuf, zbuf, in_sem, out_sem, zero_sem):
    j = pl.program_id(0)
    n_tiles = nt_ref[0]
    slot = lax.rem(j, 2)

    def tile_gather(tile, s):
        for g in range(TILE_GRANS):
            _granule_copy(xs_ref, gsrc_ref[tile * TILE_GRANS + g], xbuf.at[s], g,
                          in_sem.at[s]).start(priority=g % 2)

    @pl.when(j == 0)
    def _():
        tile_gather(0, 0)
        ybuf[1] = jnp.zeros((TE, D_MODEL), F32)
        zbuf[...] = jnp.zeros((GRAN, D_MODEL), F32)
        for half in range(2):
            spare = pltpu.make_async_copy(
                ybuf.at[1], ys_ref.at[pl.ds((SPARE_GRAN + half * TILE_GRANS) * GRAN, TE), :], out_sem.at[1])
            spare.start()
            spare.wait()

    @pl.when(j < N_TOK_TILES)
    def _():
        def zero_copy(g):
            return _granule_copy(zbuf, 0, ys_ref, j * LOCAL_GRANS + g, zero_sem)

        def start(g, c):
            zero_copy(g).start()
            return c

        def wait(g, c):
            zero_copy(g).wait()
            return c

        lax.fori_loop(ug_ref[j], LOCAL_GRANS, start, 0)
        lax.fori_loop(ug_ref[j], LOCAL_GRANS, wait, 0)

    @pl.when(j + 1 < n_tiles)
    def _():
        tile_gather(j + 1, 1 - slot)

    @pl.when(jnp.logical_and(j >= 2, j - 2 < n_tiles))
    def _():
        pltpu.make_async_copy(ybuf.at[slot], ys_ref.at[pl.ds(0, TE), :], out_sem.at[slot]).wait()

    @pl.when(j < n_tiles)
    def _():
        pltpu.make_async_copy(xs_ref.at[pl.ds(0, TE), :], xbuf.at[slot], in_sem.at[slot]).wait()
        xb = xbuf[slot].astype(BF16)
        hg = jnp.dot(xb, wg_ref[0].astype(BF16), preferred_element_type=F32)
        hu = jnp.dot(xb, wu_ref[0].astype(BF16), preferred_element_type=F32)
        h = (hg * _sigmoid(hg) * hu).astype(BF16)
        ybuf[slot] = jnp.dot(h, wd_ref[0].astype(BF16), preferred_element_type=F32)
        for g in range(TILE_GRANS):
            _granule_copy(ybuf.at[slot], g, ys_ref, gdst_ref[j * TILE_GRANS + g],
                          out_sem.at[slot]).start(priority=g % 2)


def _expert_call(tile_expert, n_tiles, gsrc, gdst, used_grans, xs, wg, wu, wd):
    wsel = lambda j, te, nt, gs, gd, ug: (te[j], 0, 0)
    return pl.pallas_call(
        _expert_kernel,
        grid_spec=pltpu.PrefetchScalarGridSpec(
            num_scalar_prefetch=5,
            grid=(MAX_TILES + 2,),
            in_specs=[pl.BlockSpec(memory_space=pl.ANY),
                      pl.BlockSpec((1, D_MODEL, D_EXPERT), wsel),
                      pl.BlockSpec((1, D_MODEL, D_EXPERT), wsel),
                      pl.BlockSpec((1, D_EXPERT, D_MODEL), wsel)],
            out_specs=pl.BlockSpec(memory_space=pl.ANY),
            scratch_shapes=[pltpu.VMEM((2, TE, D_MODEL), F32), pltpu.VMEM((2, TE, D_MODEL), F32),
                            pltpu.VMEM((GRAN, D_MODEL), F32),
                            pltpu.SemaphoreType.DMA((2,)), pltpu.SemaphoreType.DMA((2,)),
                            pltpu.SemaphoreType.DMA],
        ),
        out_shape=jax.ShapeDtypeStruct(((SPARE_GRAN + 2 * TILE_GRANS) * GRAN, D_MODEL), F32),
        compiler_params=pltpu.CompilerParams(
            dimension_semantics=("arbitrary",), vmem_limit_bytes=VMEM_LIMIT),
        name="experts",
    )(tile_expert, n_tiles, gsrc, gdst, used_grans, xs, wg, wu, wd)


def _combine_kernel(ys_ref, x1_ref, rf_ref, g_ref, b_ref, out_ref):
    route = rf_ref[...].T
    col = lax.broadcasted_iota(I32, (TM, LOCAL_ROWS), 1).astype(F32)
    unsort = (jnp.where(col == route[:, 2:3], route[:, 0:1], 0.0)
              + jnp.where(col == route[:, 3:4], route[:, 1:2], 0.0)).astype(BF16)
    ffn = jnp.dot(unsort, ys_ref[...].astype(BF16), preferred_element_type=F32)
    h = ALPHA * x1_ref[...] + ffn
    mu = jnp.mean(h, axis=-1, keepdims=True)
    hc = h - mu
    var = jnp.mean(hc * hc, axis=-1, keepdims=True)
    out_ref[...] = hc * lax.rsqrt(var + LN_EPS) * g_ref[...] + b_ref[...]


def _combine_call(ys, x1, rf, g, b):
    return pl.pallas_call(
        _combine_kernel,
        grid=(N_TOK_TILES,),
        in_specs=[pl.BlockSpec((LOCAL_ROWS, D_MODEL), lambda i: (i, 0)),
                  pl.BlockSpec((TM, D_MODEL), lambda i: (i, 0)),
                  pl.BlockSpec((8, TM), lambda i: (0, i)),
                  pl.BlockSpec((1, D_MODEL), lambda i: (0, 0)),
                  pl.BlockSpec((1, D_MODEL), lambda i: (0, 0))],
        out_specs=pl.BlockSpec((TM, D_MODEL), lambda i: (i, 0)),
        out_shape=jax.ShapeDtypeStruct((TOKENS, D_MODEL), F32),
        compiler_params=pltpu.CompilerParams(
            dimension_semantics=("arbitrary",), vmem_limit_bytes=VMEM_LIMIT),
        name="combine_ln2",
    )(ys, x1, rf, g, b)


def _router_cols(w_router_group, w_router_expert):
    w = jnp.concatenate([w_router_group, jnp.zeros((D_MODEL, 4), F32), w_router_expert,
                         jnp.zeros((D_MODEL, 128 - ROUTER_ROWS), F32)], axis=1)
    hi = w.astype(BF16)
    lo = (w - hi.astype(F32)).astype(BF16)
    return hi, lo


def _layer(x, w_in, conv_w, w_out_conv, w_out_attn, w_o, ln1_g, ln1_b,
           w_router_group, w_router_expert, w_gate, w_up, w_down, ln2_g, ln2_b):
    slopes = jnp.asarray([2.0 ** (-8.0 * (h + 1) / N_HEADS) for h in range(N_HEADS)], F32)
    q, k, v, za, sgb = _proj_call(x, w_in, conv_w, w_out_conv)
    o_a, o_b = _attn_call(slopes, q, k, v)

    wr_hi, wr_lo = _router_cols(w_router_group, w_router_expert)
    x1, xs, rf, mt = _merge_call(
        o_a, o_b, za.reshape(TOKENS, D_MODEL), sgb.reshape(TOKENS, D_MODEL),
        x.reshape(TOKENS, D_MODEL), w_out_attn, w_o,
        ln1_g.reshape(1, D_MODEL), ln1_b.reshape(1, D_MODEL), wr_hi, wr_lo)

    grans = mt.reshape(N_TOK_TILES, N_EXPERTS, 128)[:, :, 0].astype(I32)
    local_start = jnp.cumsum(grans, axis=1) - grans
    grans_t = grans.T
    tiles_e = (jnp.sum(grans_t, axis=1) + TILE_GRANS - 1) // TILE_GRANS
    tile_end = jnp.cumsum(tiles_e)
    n_tiles = tile_end[-1].reshape(1)
    tile_ids = jnp.arange(MAX_TILES + 2, dtype=I32)
    tile_expert = jnp.minimum(
        jnp.sum((tile_ids[:, None] >= tile_end[None, :]).astype(I32), axis=1), N_EXPERTS - 1)
    run_slot = TILE_GRANS * (tile_end - tiles_e)[:, None] + jnp.cumsum(grans_t, axis=1) - grans_t
    run_src = jnp.arange(N_TOK_TILES, dtype=I32)[None, :] * LOCAL_GRANS + local_start.T
    pick = (tile_expert[:MAX_TILES, None] == jnp.arange(N_EXPERTS, dtype=I32)[None, :])[:, :, None]
    t_slot = jnp.sum(jnp.where(pick, run_slot[None], 0), axis=1)
    t_len = jnp.sum(jnp.where(pick, grans_t[None], 0), axis=1)
    t_src = jnp.sum(jnp.where(pick, run_src[None], 0), axis=1)
    slots = jnp.arange(MAX_TILES * TILE_GRANS, dtype=I32).reshape(MAX_TILES, TILE_GRANS)
    k = slots[:, :, None] - t_slot[:, None, :]
    hit = (k >= 0) & (k < t_len[:, None, :])
    gran = jnp.sum(jnp.where(hit, t_src[:, None, :] + k, 0), axis=2).reshape(-1)
    filled = (jnp.sum(hit.astype(I32), axis=2) > 0).reshape(-1)
    slots = slots.reshape(-1)
    gsrc = jnp.where(filled, gran, 0)
    gdst = jnp.where(filled, gran, SPARE_GRAN + slots % (2 * TILE_GRANS))

    ys = _expert_call(tile_expert, n_tiles, gsrc, gdst, jnp.sum(grans, axis=1), xs, w_gate, w_up, w_down)
    out = _combine_call(ys, x1, rf, ln2_g.reshape(1, D_MODEL), ln2_b.reshape(1, D_MODEL))
    return out.reshape(BATCH, SEQ, D_MODEL)


def kernel(x, w_in, conv_w, w_out_conv, w_out_attn, w_o, ln1_g, ln1_b, w_router_group, w_router_expert, w_gate, w_up, w_down, ln2_g, ln2_b):
    depth = w_in.shape[0]
    for l in range(depth):
        x = _layer(x, w_in[l], conv_w[l], w_out_conv[l], w_out_attn[l], w_o[l], ln1_g[l], ln1_b[l],
                   w_router_group[l], w_router_expert[l], w_gate[l], w_up[l], w_down[l], ln2_g[l], ln2_b[l])
    return x
```

```python
import functools

import jax
import jax.numpy as jnp
from jax import lax
from jax.experimental import pallas as pl
from jax.experimental.pallas import tpu as pltpu

F32 = jnp.float32
BF16 = jnp.bfloat16
U32 = jnp.uint32
I32 = jnp.int32

D_MODEL = 1024
BATCH = 8
SEQ = 2048
TOKENS = BATCH * SEQ
CONV_WIDTH = 512
N_HEADS = 8
HEAD_DIM = 64
ATTN_WIDTH = N_HEADS * HEAD_DIM
MOBA_BLOCK = 256
N_BLOCKS = SEQ // MOBA_BLOCK
MOBA_TOPK = 3
N_GROUPS = 4
EXPERTS_PER_GROUP = 8
N_EXPERTS = N_GROUPS * EXPERTS_PER_GROUP
D_EXPERT = 256
LN_EPS = 1e-5
ALPHA = 2.0 ** 0.25
IN_COLS = 3 * CONV_WIDTH + 3 * ATTN_WIDTH + 2 * D_MODEL
HALF = D_MODEL // 2

TM = 256
TE = 256
PV_ROWS = HEAD_DIM + 16
GRAN = 8
TILE_GRANS = TE // GRAN
N_TOK_TILES = TOKENS // TM
LOCAL_ROWS = -(-(2 * TM + N_EXPERTS * (GRAN - 1)) // 256) * 256
LOCAL_GRANS = LOCAL_ROWS // GRAN
SPARE_GRAN = N_TOK_TILES * LOCAL_GRANS
MAX_TILES = (2 * TOKENS + N_TOK_TILES * N_EXPERTS * (GRAN - 1)) // TE + N_EXPERTS
ROUTER_ROWS = 40
VMEM_LIMIT = 56 * 1024 * 1024
NEG_INF = float("-inf")


def _sigmoid(z):
    return 1.0 / (1.0 + jnp.exp(-z))


def _proj_kernel(x_ref, w_in_f32_ref, convw_ref, woc_f32_ref, q_ref, k_ref, v_ref, za_ref, sgb_ref,
                 ubuf, w_in_ref, woc_ref):
    s = pl.program_id(1)

    @pl.when((pl.program_id(0) == 0) & (s == 0))
    def _():
        for c in range(0, IN_COLS, CONV_WIDTH):
            w_in_ref[:, c:c + CONV_WIDTH] = w_in_f32_ref[:, c:c + CONV_WIDTH].astype(BF16)
        woc_ref[...] = woc_f32_ref[...].astype(BF16)

    xb = x_ref[0].astype(BF16)

    def proj(c0, c1):
        return jnp.dot(xb, w_in_ref[:, c0:c1], preferred_element_type=F32)

    c_b = proj(0, CONV_WIDTH)
    u = proj(CONV_WIDTH, 2 * CONV_WIDTH) * proj(2 * CONV_WIDTH, 3 * CONV_WIDTH)

    @pl.when(s == 0)
    def _():
        ubuf[0:8, :] = jnp.zeros((8, CONV_WIDTH), F32)

    ubuf[8:8 + TM, :] = u
    w = convw_ref[...]
    conv = w[2:3, :] * u + w[1:2, :] * ubuf[7:7 + TM, :] + w[0:1, :] * ubuf[6:6 + TM, :]
    ubuf[0:8, :] = u[TM - 8:TM, :]
    hc = (c_b * conv).astype(BF16)
    y_conv = jnp.dot(hc, woc_ref[...], preferred_element_type=F32)

    o = 3 * CONV_WIDTH
    q_ref[0] = (proj(o, o + ATTN_WIDTH) * (HEAD_DIM ** -0.5)).astype(BF16)
    k_ref[0] = proj(o + ATTN_WIDTH, o + 2 * ATTN_WIDTH).astype(BF16)
    v_ref[0] = proj(o + 2 * ATTN_WIDTH, o + 3 * ATTN_WIDTH).astype(BF16)
    o += 3 * ATTN_WIDTH
    za_ref[0] = _sigmoid(proj(o, o + D_MODEL)) * y_conv
    sgb_ref[0] = _sigmoid(proj(o + D_MODEL, o + 2 * D_MODEL))


def _proj_call(x, w_in, conv_w, w_out_conv):
    tok_spec = lambda c: pl.BlockSpec((1, TM, c), lambda b, s: (b, s, 0))
    full = lambda shape: pl.BlockSpec(shape, lambda b, s: (0,) * len(shape))
    once = lambda shape: pl.BlockSpec(shape, lambda b, s: (0,) * len(shape), pipeline_mode=pl.Buffered(1))
    return pl.pallas_call(
        _proj_kernel,
        grid=(BATCH, SEQ // TM),
        in_specs=[tok_spec(D_MODEL), once((D_MODEL, IN_COLS)), full((3, CONV_WIDTH)),
                  once((CONV_WIDTH, D_MODEL))],
        out_specs=[tok_spec(ATTN_WIDTH), tok_spec(ATTN_WIDTH), tok_spec(ATTN_WIDTH),
                   tok_spec(D_MODEL), tok_spec(D_MODEL)],
        out_shape=[jax.ShapeDtypeStruct((BATCH, SEQ, ATTN_WIDTH), BF16)] * 3
        + [jax.ShapeDtypeStruct((BATCH, SEQ, D_MODEL), F32)] * 2,
        scratch_shapes=[pltpu.VMEM((TM + 8, CONV_WIDTH), F32), pltpu.VMEM((D_MODEL, IN_COLS), BF16),
                        pltpu.VMEM((CONV_WIDTH, D_MODEL), BF16)],
        compiler_params=pltpu.CompilerParams(
            dimension_semantics=("arbitrary", "arbitrary"), vmem_limit_bytes=VMEM_LIMIT),
        name="proj",
    )(x, w_in, conv_w, w_out_conv)


def _attn_kernel(slopes_ref, qa_ref, qb_ref, k_ref, v_ref, oa_ref, ob_ref,
                 kaug_ref, vt_ref, kmean_ref, qaug_ref, pv_ref, mloc_ref):
    hp = pl.program_id(1)
    j = pl.program_id(2)
    blk = MOBA_BLOCK

    @pl.when(j == 0)
    def _():
        klane = lax.broadcasted_iota(I32, (blk, 128), 1)
        koff = lax.broadcasted_iota(I32, (blk, 128), 0).astype(F32)
        k_extra = jnp.where(klane == 0, koff, jnp.where(klane == 1, 1.0, 0.0)).astype(BF16)
        orow = lax.broadcasted_iota(I32, (PV_ROWS - HEAD_DIM, blk), 0)
        ones_rows = jnp.where(orow == 0, 1.0, 0.0).astype(BF16)
        for n in range(N_BLOCKS):
            kblk = k_ref[0, n * blk:(n + 1) * blk, :]
            kaug_ref[n, :, 0:128] = kblk
            kaug_ref[n, :, 128:256] = k_extra
            kmean_ref[n:n + 1, :] = jnp.mean(kblk.astype(F32), axis=0, keepdims=True)
            v_t = v_ref[0, n * blk:(n + 1) * blk, :].astype(F32).T.astype(BF16)
            for hh in range(2):
                vt_ref[n, hh, 0:HEAD_DIM, :] = v_t[hh * HEAD_DIM:(hh + 1) * HEAD_DIM, :]
                vt_ref[n, hh, HEAD_DIM:PV_ROWS, :] = ones_rows

    lane = lax.broadcasted_iota(I32, (1, 2 * blk), 1)
    slope_row = jnp.where(lane < blk, slopes_ref[2 * hp], slopes_ref[2 * hp + 1])
    qoff_row = jnp.where(lane < blk, lane, lane - blk).astype(F32)
    feat = lax.broadcasted_iota(I32, (2 * HEAD_DIM, blk), 0)
    arow = lax.broadcasted_iota(I32, (2 * HEAD_DIM, 2 * blk), 0)
    q_extra = jnp.where(arow == 0, slope_row, jnp.where(arow == 1, -slope_row * qoff_row, 0.0)).astype(BF16)
    blk_i = lax.broadcasted_iota(I32, (N_BLOCKS, 2 * blk), 0)
    kmean = kmean_ref[...].astype(BF16)
    key_i = lax.broadcasted_iota(I32, (blk, 2 * blk), 0)
    qry_j = lax.broadcasted_iota(I32, (blk, 2 * blk), 1)
    causal = key_i <= jnp.where(qry_j < blk, qry_j, qry_j - blk)

    def prepare(q_ref, slot, qblock):
        q_t = q_ref[0].astype(F32).T
        qcat = jnp.concatenate([jnp.where(feat < HEAD_DIM, q_t, 0.0), jnp.where(feat >= HEAD_DIM, q_t, 0.0)],
                               axis=1).astype(BF16)
        qaug_ref[slot, 0:2 * HEAD_DIM, :] = qcat
        qaug_ref[slot, 2 * HEAD_DIM:4 * HEAD_DIM, :] = q_extra
        gate = jnp.dot(kmean, qcat, preferred_element_type=F32)
        cnt = jnp.zeros((N_BLOCKS, 2 * blk), F32)
        for m in range(N_BLOCKS):
            gm = gate[m:m + 1, :]
            beats = (gm > gate) | ((gm == gate) & (blk_i > m))
            cnt = cnt + jnp.where(beats & (qblock > m), 1.0, 0.0)
        return jnp.where((blk_i < qblock) & (cnt < float(MOBA_TOPK)), 1.0, 0.0)

    qblock_a = j
    qblock_b = N_BLOCKS - 1 - j
    sel_a = prepare(qa_ref, 0, qblock_a)
    sel_b = prepare(qb_ref, 1, qblock_b)

    def block_softmax(s, which, kb, own):
        t = jnp.dot(kaug_ref[kb], qaug_ref[which], preferred_element_type=F32)
        if own:
            t = jnp.where(causal, t, NEG_INF)
        m_loc = jnp.max(t, axis=0, keepdims=True)
        p = jnp.exp((t - m_loc).astype(BF16))
        pv_ref[s, 0] = jnp.dot(vt_ref[kb, 0], p[:, 0:blk], preferred_element_type=F32)
        pv_ref[s, 1] = jnp.dot(vt_ref[kb, 1], p[:, blk:2 * blk], preferred_element_type=F32)
        mloc_ref[s:s + 1, :] = m_loc

    n_mid = N_BLOCKS - 1
    block_softmax(0, 0, qblock_a, True)
    mids = []
    for s in range(1, n_mid + 1):
        is_a = s <= j
        which = jnp.where(is_a, 0, 1)
        kb = jnp.where(is_a, s - 1, s - 1 - j)
        block_softmax(s, which, kb, False)
        mids.append((is_a, kb))
    block_softmax(n_mid + 1, 1, qblock_b, True)

    def combine(o_ref, own_slot, sel, qblock, mine):
        neg = jnp.full((1, 2 * blk), -1e30, F32)
        pieces = [(own_slot, mloc_ref[own_slot:own_slot + 1, :])]
        for s, (is_a, kb) in enumerate(mids, start=1):
            selrow = jnp.sum(jnp.where(blk_i == kb, sel, 0.0), axis=0, keepdims=True)
            belongs = jnp.where(is_a, 1.0, 0.0) if mine else jnp.where(is_a, 0.0, 1.0)
            used = selrow * belongs > 0.5
            shift = slope_row * ((kb - qblock) * blk).astype(F32)
            pieces.append((s, jnp.where(used, mloc_ref[s:s + 1, :] + shift, neg)))
        m_all = pieces[0][1]
        for _, m_s in pieces[1:]:
            m_all = jnp.maximum(m_all, m_s)
        acc = [jnp.zeros((PV_ROWS, blk), F32), jnp.zeros((PV_ROWS, blk), F32)]
        for s, m_s in pieces:
            w = jnp.exp(m_s - m_all)
            for hh in range(2):
                acc[hh] = acc[hh] + pv_ref[s, hh] * w[:, hh * blk:(hh + 1) * blk]
        o_t = jnp.concatenate([a[0:HEAD_DIM, :] / a[HEAD_DIM:HEAD_DIM + 1, :] for a in acc], axis=0)
        o_ref[0] = o_t.T.astype(BF16)

    combine(oa_ref, 0, sel_a, qblock_a, True)
    combine(ob_ref, n_mid + 1, sel_b, qblock_b, False)


def _attn_call(slopes, q, k, v):
    half = N_BLOCKS // 2
    o_a, o_b = pl.pallas_call(
        _attn_kernel,
        grid_spec=pltpu.PrefetchScalarGridSpec(
            num_scalar_prefetch=1,
            grid=(BATCH, N_HEADS // 2, half),
            in_specs=[
                pl.BlockSpec((1, MOBA_BLOCK, 128), lambda b, h, j, sl: (b, j, h)),
                pl.BlockSpec((1, MOBA_BLOCK, 128), lambda b, h, j, sl: (b, N_BLOCKS - 1 - j, h)),
                pl.BlockSpec((1, SEQ, 128), lambda b, h, j, sl: (b, 0, h)),
                pl.BlockSpec((1, SEQ, 128), lambda b, h, j, sl: (b, 0, h)),
            ],
            out_specs=[pl.BlockSpec((1, MOBA_BLOCK, 128), lambda b, h, j, sl: (b, j, h)),
                       pl.BlockSpec((1, MOBA_BLOCK, 128), lambda b, h, j, sl: (b, j, h))],
            scratch_shapes=[
                pltpu.VMEM((N_BLOCKS, MOBA_BLOCK, 256), BF16),
                pltpu.VMEM((N_BLOCKS, 2, PV_ROWS, MOBA_BLOCK), BF16),
                pltpu.VMEM((N_BLOCKS, 128), F32),
                pltpu.VMEM((2, 256, 2 * MOBA_BLOCK), BF16),
                pltpu.VMEM((N_BLOCKS + 1, 2, PV_ROWS, MOBA_BLOCK), F32),
                pltpu.VMEM((16, 2 * MOBA_BLOCK), F32),
            ],
        ),
        out_shape=[jax.ShapeDtypeStruct((BATCH, SEQ // 2, ATTN_WIDTH), BF16)] * 2,
        compiler_params=pltpu.CompilerParams(
            dimension_semantics=("arbitrary", "arbitrary", "arbitrary"), vmem_limit_bytes=VMEM_LIMIT),
        name="moba_attn",
    )(slopes, q, q, k, v)
    return o_a, o_b


def _merge_kernel(oa_ref, ob_ref, za_ref, sgb_ref, x_ref, woa_f32_ref, wo_f32_ref, g_ref, b_ref,
                  wr_hi_ref, wr_lo_ref, x1_ref, xs_ref, rf_ref, mt_ref, woa_ref, wo_ref):
    i = pl.program_id(0)

    @pl.when(i == 0)
    def _():
        woa_ref[...] = woa_f32_ref[...].astype(BF16)
        wo_ref[...] = wo_f32_ref[...].astype(BF16)

    o = jnp.where(lax.rem(i, N_BLOCKS) < N_BLOCKS // 2, oa_ref[0], ob_ref[0])
    y_attn = jnp.dot(o, woa_ref[...], preferred_element_type=F32)
    y = za_ref[...] + sgb_ref[...] * y_attn
    mix = jnp.dot(y.astype(BF16), wo_ref[...], preferred_element_type=F32)
    h = ALPHA * x_ref[...] + mix
    mu = jnp.mean(h, axis=-1, keepdims=True)
    hc = h - mu
    var = jnp.mean(hc * hc, axis=-1, keepdims=True)
    x1 = hc * lax.rsqrt(var + LN_EPS) * g_ref[...] + b_ref[...]
    x1_ref[...] = x1

    xh = x1.astype(BF16)
    xl = (x1 - xh.astype(F32)).astype(BF16)
    wh = wr_hi_ref[...]
    logits = (jnp.dot(xh, wh, preferred_element_type=F32)
              + jnp.dot(xl, wh, preferred_element_type=F32)
              + jnp.dot(xh, wr_lo_ref[...], preferred_element_type=F32)).T

    row8 = lax.broadcasted_iota(I32, (8, TM), 0).astype(F32)
    gl = jnp.where(row8 < float(N_GROUPS), logits[0:8, :], NEG_INF)
    gexp = jnp.exp(gl - jnp.max(gl, axis=0, keepdims=True))
    gprob = gexp / jnp.sum(gexp, axis=0, keepdims=True)
    ptop = jnp.max(gprob, axis=0, keepdims=True)
    gtop = jnp.min(jnp.where(gprob == ptop, row8, 8.0), axis=0, keepdims=True)
    el = logits[8:ROUTER_ROWS, :]
    eg = jnp.where(gtop == 0.0, el[0:8, :],
                   jnp.where(gtop == 1.0, el[8:16, :], jnp.where(gtop == 2.0, el[16:24, :], el[24:32, :])))
    m1 = jnp.max(eg, axis=0, keepdims=True)
    i1 = jnp.min(jnp.where(eg == m1, row8, 8.0), axis=0, keepdims=True)
    eg2 = jnp.where(row8 == i1, NEG_INF, eg)
    m2 = jnp.max(eg2, axis=0, keepdims=True)
    i2 = jnp.min(jnp.where(eg2 == m2, row8, 8.0), axis=0, keepdims=True)
    t2 = jnp.exp(m2 - m1)
    gate1 = ptop * (1.0 / (1.0 + t2))
    gate2 = ptop * (t2 / (1.0 + t2))
    e1 = gtop * float(EXPERTS_PER_GROUP) + i1
    e2 = gtop * float(EXPERTS_PER_GROUP) + i2

    erow = lax.broadcasted_iota(I32, (N_EXPERTS, TM), 0).astype(F32)
    oh1 = jnp.where(erow == e1, 1.0, 0.0)
    oh2 = jnp.where(erow == e2, 1.0, 0.0)
    oh = oh1 + oh2
    ta = lax.broadcasted_iota(I32, (TM, TM), 0)
    tb = lax.broadcasted_iota(I32, (TM, TM), 1)
    upper = jnp.where(ta < tb, 1.0, 0.0).astype(BF16)
    cum = jnp.dot(oh.astype(BF16), upper, preferred_element_type=F32)

    n_e = jnp.sum(oh, axis=1, keepdims=True)
    m_e = jnp.floor((n_e + float(GRAN - 1)) * (1.0 / GRAN))
    m_rep = jnp.broadcast_to(m_e, (N_EXPERTS, 128))
    ea = lax.broadcasted_iota(I32, (N_EXPERTS, N_EXPERTS), 0)
    eb = lax.broadcasted_iota(I32, (N_EXPERTS, N_EXPERTS), 1)
    lower = jnp.where(eb < ea, 1.0, 0.0).astype(BF16)
    run_start = jnp.dot(lower, m_rep.astype(BF16), preferred_element_type=F32)
    tot = cum + float(GRAN) * run_start[:, 0:1]
    lp1 = jnp.sum(oh1 * tot, axis=0, keepdims=True)
    lp2 = jnp.sum(oh2 * tot, axis=0, keepdims=True)
    lrow = lax.broadcasted_iota(I32, (LOCAL_ROWS, TM), 0).astype(F32)
    perm = jnp.where((lrow == lp1) | (lrow == lp2), 1.0, 0.0).astype(BF16)
    xs_ref[...] = jnp.dot(perm, xh, preferred_element_type=F32)

    zero = jnp.zeros((1, TM), F32)
    rf_ref[...] = jnp.concatenate([gate1, gate2, lp1, lp2, zero, zero, zero, zero], axis=0)
    mt_ref[...] = m_rep


def _merge_call(o_a, o_b, za, sgb, x, woa, wo, g, b, wr_hi, wr_lo):
    tok = lambda c: pl.BlockSpec((TM, c), lambda i: (i, 0))
    full = lambda shape: pl.BlockSpec(shape, lambda i: (0,) * len(shape))
    lanes = pl.BlockSpec((8, TM), lambda i: (0, i))
    half = N_BLOCKS // 2
    o_a_spec = pl.BlockSpec((1, TM, ATTN_WIDTH), lambda i: (i // N_BLOCKS, jnp.minimum(i % N_BLOCKS, half - 1), 0))
    o_b_spec = pl.BlockSpec(
        (1, TM, ATTN_WIDTH), lambda i: (i // N_BLOCKS, jnp.minimum(N_BLOCKS - 1 - i % N_BLOCKS, half - 1), 0))
    return pl.pallas_call(
        _merge_kernel,
        grid=(TOKENS // TM,),
        in_specs=[o_a_spec, o_b_spec, tok(D_MODEL), tok(D_MODEL), tok(D_MODEL),
                  full((ATTN_WIDTH, D_MODEL)), full((D_MODEL, D_MODEL)), full((1, D_MODEL)),
                  full((1, D_MODEL)), full((D_MODEL, 128)), full((D_MODEL, 128))],
        out_specs=[tok(D_MODEL), pl.BlockSpec((LOCAL_ROWS, D_MODEL), lambda i: (i, 0)), lanes,
                   pl.BlockSpec((N_EXPERTS, 128), lambda i: (i, 0))],
        out_shape=[jax.ShapeDtypeStruct((TOKENS, D_MODEL), F32),
                   jax.ShapeDtypeStruct((N_TOK_TILES * LOCAL_ROWS, D_MODEL), F32),
                   jax.ShapeDtypeStruct((8, TOKENS), F32),
                   jax.ShapeDtypeStruct((N_TOK_TILES * N_EXPERTS, 128), F32)],
        scratch_shapes=[pltpu.VMEM((ATTN_WIDTH, D_MODEL), BF16), pltpu.VMEM((D_MODEL, D_MODEL), BF16)],
        compiler_params=pltpu.CompilerParams(
            dimension_semantics=("arbitrary",), vmem_limit_bytes=VMEM_LIMIT),
        name="merge_ln1_route",
    )(o_a, o_b, za, sgb, x, woa, wo, g, b, wr_hi, wr_lo)


def _granule_copy(src_ref, src_gran, dst_ref, dst_gran, sem):
    src = pl.multiple_of(src_gran * GRAN, GRAN)
    dst = pl.multiple_of(dst_gran * GRAN, GRAN)
    return pltpu.make_async_copy(src_ref.at[pl.ds(src, GRAN), :], dst_ref.at[pl.ds(dst, GRAN), :], sem)


def _expert_kernel(te_ref, nt_ref, gsrc_ref, gdst_ref, ug_ref, xs_ref, wg_ref, wu_ref, wd_ref, ys_ref,
                   xbuf, ybuf, zbuf, in_sem, out_sem, zero_sem):
    j = pl.program_id(0)
    n_tiles = nt_ref[0]
    slot = lax.rem(j, 2)

    def tile_gather(tile, s):
        for g in range(TILE_GRANS):
            _granule_copy(xs_ref, gsrc_ref[tile * TILE_GRANS + g], xbuf.at[s], g,
                          in_sem.at[s]).start(priority=g % 2)

    @pl.when(j == 0)
    def _():
        tile_gather(0, 0)
        ybuf[1] = jnp.zeros((TE, D_MODEL), F32)
        zbuf[...] = jnp.zeros((GRAN, D_MODEL), F32)
        for half in range(2):
            spare = pltpu.make_async_copy(
                ybuf.at[1], ys_ref.at[pl.ds((SPARE_GRAN + half * TILE_GRANS) * GRAN, TE), :], out_sem.at[1])
            spare.start()
            spare.wait()

    def zero_copy(t, g):
        return _granule_copy(zbuf, 0, ys_ref, t * LOCAL_GRANS + g, zero_sem)

    @pl.when(jnp.logical_and(j >= 1, j <= N_TOK_TILES))
    def _():
        def wait(g, c):
            zero_copy(j - 1, g).wait()
            return c

        lax.fori_loop(ug_ref[j - 1], LOCAL_GRANS, wait, 0)

    @pl.when(j < N_TOK_TILES)
    def _():
        def start(g, c):
            zero_copy(j, g).start()
            return c

        lax.fori_loop(ug_ref[j], LOCAL_GRANS, start, 0)

    @pl.when(j + 1 < n_tiles)
    def _():
        tile_gather(j + 1, 1 - slot)

    @pl.when(jnp.logical_and(j >= 2, j - 2 < n_tiles))
    def _():
        pltpu.make_async_copy(ybuf.at[slot], ys_ref.at[pl.ds(0, TE), :], out_sem.at[slot]).wait()

    @pl.when(j < n_tiles)
    def _():
        pltpu.make_async_copy(xs_ref.at[pl.ds(0, TE), :], xbuf.at[slot], in_sem.at[slot]).wait()
        xb = xbuf[slot].astype(BF16)
        hg = jnp.dot(xb, wg_ref[0].astype(BF16), preferred_element_type=F32)
        hu = jnp.dot(xb, wu_ref[0].astype(BF16), preferred_element_type=F32)
        h = (hg * _sigmoid(hg) * hu).astype(BF16)
        ybuf[slot] = jnp.dot(h, wd_ref[0].astype(BF16), preferred_element_type=F32)
        for g in range(TILE_GRANS):
            _granule_copy(ybuf.at[slot], g, ys_ref, gdst_ref[j * TILE_GRANS + g],
                          out_sem.at[slot]).start(priority=g % 2)


def _expert_call(tile_expert, n_tiles, gsrc, gdst, used_grans, xs, wg, wu, wd):
    wsel = lambda j, te, nt, gs, gd, ug: (te[j], 0, 0)
    return pl.pallas_call(
        _expert_kernel,
        grid_spec=pltpu.PrefetchScalarGridSpec(
            num_scalar_prefetch=5,
            grid=(MAX_TILES + 2,),
            in_specs=[pl.BlockSpec(memory_space=pl.ANY),
                      pl.BlockSpec((1, D_MODEL, D_EXPERT), wsel),
                      pl.BlockSpec((1, D_MODEL, D_EXPERT), wsel),
                      pl.BlockSpec((1, D_EXPERT, D_MODEL), wsel)],
            out_specs=pl.BlockSpec(memory_space=pl.ANY),
            scratch_shapes=[pltpu.VMEM((2, TE, D_MODEL), F32), pltpu.VMEM((2, TE, D_MODEL), F32),
                            pltpu.VMEM((GRAN, D_MODEL), F32),
                            pltpu.SemaphoreType.DMA((2,)), pltpu.SemaphoreType.DMA((2,)),
                            pltpu.SemaphoreType.DMA],
        ),
        out_shape=jax.ShapeDtypeStruct(((SPARE_GRAN + 2 * TILE_GRANS) * GRAN, D_MODEL), F32),
        compiler_params=pltpu.CompilerParams(
            dimension_semantics=("arbitrary",), vmem_limit_bytes=VMEM_LIMIT),
        name="experts",
    )(tile_expert, n_tiles, gsrc, gdst, used_grans, xs, wg, wu, wd)


def _combine_kernel(ys_ref, x1_ref, rf_ref, g_ref, b_ref, out_ref):
    route = rf_ref[...].T
    col = lax.broadcasted_iota(I32, (TM, LOCAL_ROWS), 1).astype(F32)
    unsort = (jnp.where(col == route[:, 2:3], route[:, 0:1], 0.0)
              + jnp.where(col == route[:, 3:4], route[:, 1:2], 0.0)).astype(BF16)
    ffn = jnp.dot(unsort, ys_ref[...].astype(BF16), preferred_element_type=F32)
    h = ALPHA * x1_ref[...] + ffn
    mu = jnp.mean(h, axis=-1, keepdims=True)
    hc = h - mu
    var = jnp.mean(hc * hc, axis=-1, keepdims=True)
    out_ref[...] = hc * lax.rsqrt(var + LN_EPS) * g_ref[...] + b_ref[...]


def _combine_call(ys, x1, rf, g, b):
    return pl.pallas_call(
        _combine_kernel,
        grid=(N_TOK_TILES,),
        in_specs=[pl.BlockSpec((LOCAL_ROWS, D_MODEL), lambda i: (i, 0)),
                  pl.BlockSpec((TM, D_MODEL), lambda i: (i, 0)),
                  pl.BlockSpec((8, TM), lambda i: (0, i)),
                  pl.BlockSpec((1, D_MODEL), lambda i: (0, 0)),
                  pl.BlockSpec((1, D_MODEL), lambda i: (0, 0))],
        out_specs=pl.BlockSpec((TM, D_MODEL), lambda i: (i, 0)),
        out_shape=jax.ShapeDtypeStruct((TOKENS, D_MODEL), F32),
        compiler_params=pltpu.CompilerParams(
            dimension_semantics=("arbitrary",), vmem_limit_bytes=VMEM_LIMIT),
        name="combine_ln2",
    )(ys, x1, rf, g, b)


def _router_cols(w_router_group, w_router_expert):
    w = jnp.concatenate([w_router_group, jnp.zeros((D_MODEL, 4), F32), w_router_expert,
                         jnp.zeros((D_MODEL, 128 - ROUTER_ROWS), F32)], axis=1)
    hi = w.astype(BF16)
    lo = (w - hi.astype(F32)).astype(BF16)
    return hi, lo


def _layer(x, w_in, conv_w, w_out_conv, w_out_attn, w_o, ln1_g, ln1_b,
           w_router_group, w_router_expert, w_gate, w_up, w_down, ln2_g, ln2_b):
    slopes = jnp.asarray([2.0 ** (-8.0 * (h + 1) / N_HEADS) for h in range(N_HEADS)], F32)
    q, k, v, za, sgb = _proj_call(x, w_in, conv_w, w_out_conv)
    o_a, o_b = _attn_call(slopes, q, k, v)

    wr_hi, wr_lo = _router_cols(w_router_group, w_router_expert)
    x1, xs, rf, mt = _merge_call(
        o_a, o_b, za.reshape(TOKENS, D_MODEL), sgb.reshape(TOKENS, D_MODEL),
        x.reshape(TOKENS, D_MODEL), w_out_attn, w_o,
        ln1_g.reshape(1, D_MODEL), ln1_b.reshape(1, D_MODEL), wr_hi, wr_lo)

    grans = mt.reshape(N_TOK_TILES, N_EXPERTS, 128)[:, :, 0].astype(I32)
    local_start = jnp.cumsum(grans, axis=1) - grans
    grans_t = grans.T
    tiles_e = (jnp.sum(grans_t, axis=1) + TILE_GRANS - 1) // TILE_GRANS
    tile_end = jnp.cumsum(tiles_e)
    n_tiles = tile_end[-1].reshape(1)
    tile_ids = jnp.arange(MAX_TILES + 2, dtype=I32)
    tile_expert = jnp.minimum(
        jnp.sum((tile_ids[:, None] >= tile_end[None, :]).astype(I32), axis=1), N_EXPERTS - 1)
    run_slot = TILE_GRANS * (tile_end - tiles_e)[:, None] + jnp.cumsum(grans_t, axis=1) - grans_t
    run_src = jnp.arange(N_TOK_TILES, dtype=I32)[None, :] * LOCAL_GRANS + local_start.T
    pick = (tile_expert[:MAX_TILES, None] == jnp.arange(N_EXPERTS, dtype=I32)[None, :])[:, :, None]
    t_slot = jnp.sum(jnp.where(pick, run_slot[None], 0), axis=1)
    t_len = jnp.sum(jnp.where(pick, grans_t[None], 0), axis=1)
    t_src = jnp.sum(jnp.where(pick, run_src[None], 0), axis=1)
    slots = jnp.arange(MAX_TILES * TILE_GRANS, dtype=I32).reshape(MAX_TILES, TILE_GRANS)
    k = slots[:, :, None] - t_slot[:, None, :]
    hit = (k >= 0) & (k < t_len[:, None, :])
    gran = jnp.sum(jnp.where(hit, t_src[:, None, :] + k, 0), axis=2).reshape(-1)
    filled = (jnp.sum(hit.astype(I32), axis=2) > 0).reshape(-1)
    slots = slots.reshape(-1)
    gsrc = jnp.where(filled, gran, 0)
    gdst = jnp.where(filled, gran, SPARE_GRAN + slots % (2 * TILE_GRANS))

    ys = _expert_call(tile_expert, n_tiles, gsrc, gdst, jnp.sum(grans, axis=1), xs, w_gate, w_up, w_down)
    out = _combine_call(ys, x1, rf, ln2_g.reshape(1, D_MODEL), ln2_b.reshape(1, D_MODEL))
    return out.reshape(BATCH, SEQ, D_MODEL)


def kernel(x, w_in, conv_w, w_out_conv, w_out_attn, w_o, ln1_g, ln1_b, w_router_group, w_router_expert, w_gate, w_up, w_down, ln2_g, ln2_b):
    depth = w_in.shape[0]
    for l in range(depth):
        x = _layer(x, w_in[l], conv_w[l], w_out_conv[l], w_out_attn[l], w_o[l], ln1_g[l], ln1_b[l],
                   w_router_group[l], w_router_expert[l], w_gate[l], w_up[l], w_down[l], ln2_g[l], ln2_b[l])
    return x
```

```python
import functools

import jax
import jax.numpy as jnp
from jax import lax
from jax.experimental import pallas as pl
from jax.experimental.pallas import tpu as pltpu

F32 = jnp.float32
BF16 = jnp.bfloat16
U32 = jnp.uint32
I32 = jnp.int32

D_MODEL = 1024
BATCH = 8
SEQ = 2048
TOKENS = BATCH * SEQ
CONV_WIDTH = 512
N_HEADS = 8
HEAD_DIM = 64
ATTN_WIDTH = N_HEADS * HEAD_DIM
MOBA_BLOCK = 256
N_BLOCKS = SEQ // MOBA_BLOCK
MOBA_TOPK = 3
N_GROUPS = 4
EXPERTS_PER_GROUP = 8
N_EXPERTS = N_GROUPS * EXPERTS_PER_GROUP
D_EXPERT = 256
LN_EPS = 1e-5
ALPHA = 2.0 ** 0.25
IN_COLS = 3 * CONV_WIDTH + 3 * ATTN_WIDTH + 2 * D_MODEL
HALF = D_MODEL // 2

TM = 256
TE = 256
PV_ROWS = HEAD_DIM + 16
GRAN = 8
TILE_GRANS = TE // GRAN
N_TOK_TILES = TOKENS // TM
LOCAL_ROWS = -(-(2 * TM + N_EXPERTS * (GRAN - 1)) // 256) * 256
LOCAL_GRANS = LOCAL_ROWS // GRAN
SPARE_GRAN = N_TOK_TILES * LOCAL_GRANS
MAX_TILES = (2 * TOKENS + N_TOK_TILES * N_EXPERTS * (GRAN - 1)) // TE + N_EXPERTS
ROUTER_ROWS = 40
VMEM_LIMIT = 56 * 1024 * 1024
NEG_INF = float("-inf")


def _sigmoid(z):
    return 1.0 / (1.0 + jnp.exp(-z))


def _proj_kernel(x_ref, w_in_f32_ref, convw_ref, woc_f32_ref, q_ref, k_ref, v_ref, za_ref, sgb_ref,
                 ubuf, w_in_ref, woc_ref):
    s = pl.program_id(1)

    @pl.when((pl.program_id(0) == 0) & (s == 0))
    def _():
        for c in range(0, IN_COLS, CONV_WIDTH):
            w_in_ref[:, c:c + CONV_WIDTH] = w_in_f32_ref[:, c:c + CONV_WIDTH].astype(BF16)
        woc_ref[...] = woc_f32_ref[...].astype(BF16)

    xb = x_ref[0].astype(BF16)

    def proj(c0, c1):
        return jnp.dot(xb, w_in_ref[:, c0:c1], preferred_element_type=F32)

    c_b = proj(0, CONV_WIDTH)
    u = proj(CONV_WIDTH, 2 * CONV_WIDTH) * proj(2 * CONV_WIDTH, 3 * CONV_WIDTH)

    @pl.when(s == 0)
    def _():
        ubuf[0:8, :] = jnp.zeros((8, CONV_WIDTH), F32)

    ubuf[8:8 + TM, :] = u
    w = convw_ref[...]
    conv = w[2:3, :] * u + w[1:2, :] * ubuf[7:7 + TM, :] + w[0:1, :] * ubuf[6:6 + TM, :]
    ubuf[0:8, :] = u[TM - 8:TM, :]
    hc = (c_b * conv).astype(BF16)
    y_conv = jnp.dot(hc, woc_ref[...], preferred_element_type=F32)

    o = 3 * CONV_WIDTH
    q_ref[0] = (proj(o, o + ATTN_WIDTH) * (HEAD_DIM ** -0.5)).astype(BF16)
    k_ref[0] = proj(o + ATTN_WIDTH, o + 2 * ATTN_WIDTH).astype(BF16)
    v_ref[0] = proj(o + 2 * ATTN_WIDTH, o + 3 * ATTN_WIDTH).astype(BF16)
    o += 3 * ATTN_WIDTH
    za_ref[0] = _sigmoid(proj(o, o + D_MODEL)) * y_conv
    sgb_ref[0] = _sigmoid(proj(o + D_MODEL, o + 2 * D_MODEL))


def _proj_call(x, w_in, conv_w, w_out_conv):
    tok_spec = lambda c: pl.BlockSpec((1, TM, c), lambda b, s: (b, s, 0))
    full = lambda shape: pl.BlockSpec(shape, lambda b, s: (0,) * len(shape))
    once = lambda shape: pl.BlockSpec(shape, lambda b, s: (0,) * len(shape), pipeline_mode=pl.Buffered(1))
    return pl.pallas_call(
        _proj_kernel,
        grid=(BATCH, SEQ // TM),
        in_specs=[tok_spec(D_MODEL), once((D_MODEL, IN_COLS)), full((3, CONV_WIDTH)),
                  once((CONV_WIDTH, D_MODEL))],
        out_specs=[tok_spec(ATTN_WIDTH), tok_spec(ATTN_WIDTH), tok_spec(ATTN_WIDTH),
                   tok_spec(D_MODEL), tok_spec(D_MODEL)],
        out_shape=[jax.ShapeDtypeStruct((BATCH, SEQ, ATTN_WIDTH), BF16)] * 3
        + [jax.ShapeDtypeStruct((BATCH, SEQ, D_MODEL), F32)] * 2,
        scratch_shapes=[pltpu.VMEM((TM + 8, CONV_WIDTH), F32), pltpu.VMEM((D_MODEL, IN_COLS), BF16),
                        pltpu.VMEM((CONV_WIDTH, D_MODEL), BF16)],
        compiler_params=pltpu.CompilerParams(
            dimension_semantics=("arbitrary", "arbitrary"), vmem_limit_bytes=VMEM_LIMIT),
        name="proj",
    )(x, w_in, conv_w, w_out_conv)


def _attn_kernel(slopes_ref, qa_ref, qb_ref, k_ref, v_ref, oa_ref, ob_ref,
                 kaug_ref, vt_ref, kmean_ref, qaug_ref, pv_ref, mloc_ref, t_ref, p_ref):
    hp = pl.program_id(1)
    j = pl.program_id(2)
    blk = MOBA_BLOCK

    @pl.when(j == 0)
    def _():
        klane = lax.broadcasted_iota(I32, (blk, 128), 1)
        koff = lax.broadcasted_iota(I32, (blk, 128), 0).astype(F32)
        k_extra = jnp.where(klane == 0, koff, jnp.where(klane == 1, 1.0, 0.0)).astype(BF16)
        orow = lax.broadcasted_iota(I32, (PV_ROWS - HEAD_DIM, blk), 0)
        ones_rows = jnp.where(orow == 0, 1.0, 0.0).astype(BF16)
        for n in range(N_BLOCKS):
            kblk = k_ref[0, n * blk:(n + 1) * blk, :]
            kaug_ref[n, :, 0:128] = kblk
            kaug_ref[n, :, 128:256] = k_extra
            kmean_ref[n:n + 1, :] = jnp.mean(kblk.astype(F32), axis=0, keepdims=True)
            v_t = v_ref[0, n * blk:(n + 1) * blk, :].astype(F32).T.astype(BF16)
            for hh in range(2):
                vt_ref[n, hh, 0:HEAD_DIM, :] = v_t[hh * HEAD_DIM:(hh + 1) * HEAD_DIM, :]
                vt_ref[n, hh, HEAD_DIM:PV_ROWS, :] = ones_rows

    lane = lax.broadcasted_iota(I32, (1, 2 * blk), 1)
    slope_row = jnp.where(lane < blk, slopes_ref[2 * hp], slopes_ref[2 * hp + 1])
    qoff_row = jnp.where(lane < blk, lane, lane - blk).astype(F32)
    feat = lax.broadcasted_iota(I32, (2 * HEAD_DIM, blk), 0)
    arow = lax.broadcasted_iota(I32, (2 * HEAD_DIM, 2 * blk), 0)
    q_extra = jnp.where(arow == 0, slope_row, jnp.where(arow == 1, -slope_row * qoff_row, 0.0)).astype(BF16)
    blk_i = lax.broadcasted_iota(I32, (N_BLOCKS, 2 * blk), 0)
    kmean = kmean_ref[...].astype(BF16)
    key_i = lax.broadcasted_iota(I32, (blk, 2 * blk), 0)
    qry_j = lax.broadcasted_iota(I32, (blk, 2 * blk), 1)
    causal = key_i <= jnp.where(qry_j < blk, qry_j, qry_j - blk)

    def prepare(q_ref, slot, qblock):
        q_t = q_ref[0].astype(F32).T
        qcat = jnp.concatenate([jnp.where(feat < HEAD_DIM, q_t, 0.0), jnp.where(feat >= HEAD_DIM, q_t, 0.0)],
                               axis=1).astype(BF16)
        qaug_ref[slot, 0:2 * HEAD_DIM, :] = qcat
        qaug_ref[slot, 2 * HEAD_DIM:4 * HEAD_DIM, :] = q_extra
        gate = jnp.dot(kmean, qcat, preferred_element_type=F32)
        cnt = jnp.zeros((N_BLOCKS, 2 * blk), F32)
        for m in range(N_BLOCKS):
            gm = gate[m:m + 1, :]
            beats = (gm > gate) | ((gm == gate) & (blk_i > m))
            cnt = cnt + jnp.where(beats & (qblock > m), 1.0, 0.0)
        return jnp.where((blk_i < qblock) & (cnt < float(MOBA_TOPK)), 1.0, 0.0)

    qblock_a = j
    qblock_b = N_BLOCKS - 1 - j
    sel_a = prepare(qa_ref, 0, qblock_a)
    sel_b = prepare(qb_ref, 1, qblock_b)

    n_mid = N_BLOCKS - 1
    slots = [(0, 0, qblock_a, True)]
    mids = []
    for s in range(1, n_mid + 1):
        is_a = s <= j
        slots.append((s, jnp.where(is_a, 0, 1), jnp.where(is_a, s - 1, s - 1 - j), False))
        mids.append((is_a, slots[-1][2]))
    slots.append((n_mid + 1, 1, qblock_b, True))

    for s, which, kb, _ in slots:
        t_ref[s] = jnp.dot(kaug_ref[kb], qaug_ref[which], preferred_element_type=F32)
    for s, _, _, own in slots:
        t = t_ref[s]
        if own:
            t = jnp.where(causal, t, NEG_INF)
        m_loc = jnp.max(t, axis=0, keepdims=True)
        p_ref[s] = jnp.exp((t - m_loc).astype(BF16))
        mloc_ref[s:s + 1, :] = m_loc
    for s, _, kb, _ in slots:
        pv_ref[s, 0] = jnp.dot(vt_ref[kb, 0], p_ref[s, :, 0:blk], preferred_element_type=F32)
        pv_ref[s, 1] = jnp.dot(vt_ref[kb, 1], p_ref[s, :, blk:2 * blk], preferred_element_type=F32)

    def combine(o_ref, own_slot, sel, qblock, mine):
        neg = jnp.full((1, 2 * blk), -1e30, F32)
        pieces = [(own_slot, mloc_ref[own_slot:own_slot + 1, :])]
        for s, (is_a, kb) in enumerate(mids, start=1):
            selrow = jnp.sum(jnp.where(blk_i == kb, sel, 0.0), axis=0, keepdims=True)
            belongs = jnp.where(is_a, 1.0, 0.0) if mine else jnp.where(is_a, 0.0, 1.0)
            used = selrow * belongs > 0.5
            shift = slope_row * ((kb - qblock) * blk).astype(F32)
            pieces.append((s, jnp.where(used, mloc_ref[s:s + 1, :] + shift, neg)))
        m_all = pieces[0][1]
        for _, m_s in pieces[1:]:
            m_all = jnp.maximum(m_all, m_s)
        acc = [jnp.zeros((PV_ROWS, blk), F32), jnp.zeros((PV_ROWS, blk), F32)]
        for s, m_s in pieces:
            w = jnp.exp(m_s - m_all)
            for hh in range(2):
                acc[hh] = acc[hh] + pv_ref[s, hh] * w[:, hh * blk:(hh + 1) * blk]
        o_t = jnp.concatenate([a[0:HEAD_DIM, :] / a[HEAD_DIM:HEAD_DIM + 1, :] for a in acc], axis=0)
        o_ref[0] = o_t.T.astype(BF16)

    combine(oa_ref, 0, sel_a, qblock_a, True)
    combine(ob_ref, n_mid + 1, sel_b, qblock_b, False)


def _attn_call(slopes, q, k, v):
    half = N_BLOCKS // 2
    o_a, o_b = pl.pallas_call(
        _attn_kernel,
        grid_spec=pltpu.PrefetchScalarGridSpec(
            num_scalar_prefetch=1,
            grid=(BATCH, N_HEADS // 2, half),
            in_specs=[
                pl.BlockSpec((1, MOBA_BLOCK, 128), lambda b, h, j, sl: (b, j, h)),
                pl.BlockSpec((1, MOBA_BLOCK, 128), lambda b, h, j, sl: (b, N_BLOCKS - 1 - j, h)),
                pl.BlockSpec((1, SEQ, 128), lambda b, h, j, sl: (b, 0, h)),
                pl.BlockSpec((1, SEQ, 128), lambda b, h, j, sl: (b, 0, h)),
            ],
            out_specs=[pl.BlockSpec((1, MOBA_BLOCK, 128), lambda b, h, j, sl: (b, j, h)),
                       pl.BlockSpec((1, MOBA_BLOCK, 128), lambda b, h, j, sl: (b, j, h))],
            scratch_shapes=[
                pltpu.VMEM((N_BLOCKS, MOBA_BLOCK, 256), BF16),
                pltpu.VMEM((N_BLOCKS, 2, PV_ROWS, MOBA_BLOCK), BF16),
                pltpu.VMEM((N_BLOCKS, 128), F32),
                pltpu.VMEM((2, 256, 2 * MOBA_BLOCK), BF16),
                pltpu.VMEM((N_BLOCKS + 1, 2, PV_ROWS, MOBA_BLOCK), F32),
                pltpu.VMEM((16, 2 * MOBA_BLOCK), F32),
                pltpu.VMEM((N_BLOCKS + 1, MOBA_BLOCK, 2 * MOBA_BLOCK), F32),
                pltpu.VMEM((N_BLOCKS + 1, MOBA_BLOCK, 2 * MOBA_BLOCK), BF16),
            ],
        ),
        out_shape=[jax.ShapeDtypeStruct((BATCH, SEQ // 2, ATTN_WIDTH), BF16)] * 2,
        compiler_params=pltpu.CompilerParams(
            dimension_semantics=("arbitrary", "arbitrary", "arbitrary"), vmem_limit_bytes=VMEM_LIMIT),
        name="moba_attn",
    )(slopes, q, q, k, v)
    return o_a, o_b


def _merge_kernel(oa_ref, ob_ref, za_ref, sgb_ref, x_ref, woa_f32_ref, wo_f32_ref, g_ref, b_ref,
                  wr_hi_ref, wr_lo_ref, x1_ref, xs_ref, rf_ref, mt_ref, woa_ref, wo_ref):
    i = pl.program_id(0)

    @pl.when(i == 0)
    def _():
        woa_ref[...] = woa_f32_ref[...].astype(BF16)
        wo_ref[...] = wo_f32_ref[...].astype(BF16)

    o = jnp.where(lax.rem(i, N_BLOCKS) < N_BLOCKS // 2, oa_ref[0], ob_ref[0])
    y_attn = jnp.dot(o, woa_ref[...], preferred_element_type=F32)
    y = za_ref[...] + sgb_ref[...] * y_attn
    mix = jnp.dot(y.astype(BF16), wo_ref[...], preferred_element_type=F32)
    h = ALPHA * x_ref[...] + mix
    mu = jnp.mean(h, axis=-1, keepdims=True)
    hc = h - mu
    var = jnp.mean(hc * hc, axis=-1, keepdims=True)
    x1 = hc * lax.rsqrt(var + LN_EPS) * g_ref[...] + b_ref[...]
    x1_ref[...] = x1

    xh = x1.astype(BF16)
    xl = (x1 - xh.astype(F32)).astype(BF16)
    wh = wr_hi_ref[...]
    logits = (jnp.dot(xh, wh, preferred_element_type=F32)
              + jnp.dot(xl, wh, preferred_element_type=F32)
              + jnp.dot(xh, wr_lo_ref[...], preferred_element_type=F32)).T

    row8 = lax.broadcasted_iota(I32, (8, TM), 0).astype(F32)
    gl = jnp.where(row8 < float(N_GROUPS), logits[0:8, :], NEG_INF)
    gexp = jnp.exp(gl - jnp.max(gl, axis=0, keepdims=True))
    gprob = gexp / jnp.sum(gexp, axis=0, keepdims=True)
    ptop = jnp.max(gprob, axis=0, keepdims=True)
    gtop = jnp.min(jnp.where(gprob == ptop, row8, 8.0), axis=0, keepdims=True)
    el = logits[8:ROUTER_ROWS, :]
    eg = jnp.where(gtop == 0.0, el[0:8, :],
                   jnp.where(gtop == 1.0, el[8:16, :], jnp.where(gtop == 2.0, el[16:24, :], el[24:32, :])))
    m1 = jnp.max(eg, axis=0, keepdims=True)
    i1 = jnp.min(jnp.where(eg == m1, row8, 8.0), axis=0, keepdims=True)
    eg2 = jnp.where(row8 == i1, NEG_INF, eg)
    m2 = jnp.max(eg2, axis=0, keepdims=True)
    i2 = jnp.min(jnp.where(eg2 == m2, row8, 8.0), axis=0, keepdims=True)
    t2 = jnp.exp(m2 - m1)
    gate1 = ptop * (1.0 / (1.0 + t2))
    gate2 = ptop * (t2 / (1.0 + t2))
    e1 = gtop * float(EXPERTS_PER_GROUP) + i1
    e2 = gtop * float(EXPERTS_PER_GROUP) + i2

    erow = lax.broadcasted_iota(I32, (N_EXPERTS, TM), 0).astype(F32)
    oh1 = jnp.where(erow == e1, 1.0, 0.0)
    oh2 = jnp.where(erow == e2, 1.0, 0.0)
    oh = oh1 + oh2
    ta = lax.broadcasted_iota(I32, (TM, TM), 0)
    tb = lax.broadcasted_iota(I32, (TM, TM), 1)
    upper = jnp.where(ta < tb, 1.0, 0.0).astype(BF16)
    cum = jnp.dot(oh.astype(BF16), upper, preferred_element_type=F32)

    n_e = jnp.sum(oh, axis=1, keepdims=True)
    m_e = jnp.floor((n_e + float(GRAN - 1)) * (1.0 / GRAN))
    m_rep = jnp.broadcast_to(m_e, (N_EXPERTS, 128))
    ea = lax.broadcasted_iota(I32, (N_EXPERTS, N_EXPERTS), 0)
    eb = lax.broadcasted_iota(I32, (N_EXPERTS, N_EXPERTS), 1)
    lower = jnp.where(eb < ea, 1.0, 0.0).astype(BF16)
    run_start = jnp.dot(lower, m_rep.astype(BF16), preferred_element_type=F32)
    tot = cum + float(GRAN) * run_start[:, 0:1]
    lp1 = jnp.sum(oh1 * tot, axis=0, keepdims=True)
    lp2 = jnp.sum(oh2 * tot, axis=0, keepdims=True)
    lrow = lax.broadcasted_iota(I32, (LOCAL_ROWS, TM), 0).astype(F32)
    perm = jnp.where((lrow == lp1) | (lrow == lp2), 1.0, 0.0).astype(BF16)
    xs_ref[...] = jnp.dot(perm, xh, preferred_element_type=F32)

    zero = jnp.zeros((1, TM), F32)
    rf_ref[...] = jnp.concatenate([gate1, gate2, lp1, lp2, zero, zero, zero, zero], axis=0)
    mt_ref[...] = m_rep


def _merge_call(o_a, o_b, za, sgb, x, woa, wo, g, b, wr_hi, wr_lo):
    tok = lambda c: pl.BlockSpec((TM, c), lambda i: (i, 0))
    full = lambda shape: pl.BlockSpec(shape, lambda i: (0,) * len(shape))
    lanes = pl.BlockSpec((8, TM), lambda i: (0, i))
    half = N_BLOCKS // 2
    o_a_spec = pl.BlockSpec((1, TM, ATTN_WIDTH), lambda i: (i // N_BLOCKS, jnp.minimum(i % N_BLOCKS, half - 1), 0))
    o_b_spec = pl.BlockSpec(
        (1, TM, ATTN_WIDTH), lambda i: (i // N_BLOCKS, jnp.minimum(N_BLOCKS - 1 - i % N_BLOCKS, half - 1), 0))
    return pl.pallas_call(
        _merge_kernel,
        grid=(TOKENS // TM,),
        in_specs=[o_a_spec, o_b_spec, tok(D_MODEL), tok(D_MODEL), tok(D_MODEL),
                  full((ATTN_WIDTH, D_MODEL)), full((D_MODEL, D_MODEL)), full((1, D_MODEL)),
                  full((1, D_MODEL)), full((D_MODEL, 128)), full((D_MODEL, 128))],
        out_specs=[tok(D_MODEL), pl.BlockSpec((LOCAL_ROWS, D_MODEL), lambda i: (i, 0)), lanes,
                   pl.BlockSpec((N_EXPERTS, 128), lambda i: (i, 0))],
        out_shape=[jax.ShapeDtypeStruct((TOKENS, D_MODEL), F32),
                   jax.ShapeDtypeStruct((N_TOK_TILES * LOCAL_ROWS, D_MODEL), F32),
                   jax.ShapeDtypeStruct((8, TOKENS), F32),
                   jax.ShapeDtypeStruct((N_TOK_TILES * N_EXPERTS, 128), F32)],
        scratch_shapes=[pltpu.VMEM((ATTN_WIDTH, D_MODEL), BF16), pltpu.VMEM((D_MODEL, D_MODEL), BF16)],
        compiler_params=pltpu.CompilerParams(
            dimension_semantics=("arbitrary",), vmem_limit_bytes=VMEM_LIMIT),
        name="merge_ln1_route",
    )(o_a, o_b, za, sgb, x, woa, wo, g, b, wr_hi, wr_lo)


def _granule_copy(src_ref, src_gran, dst_ref, dst_gran, sem):
    src = pl.multiple_of(src_gran * GRAN, GRAN)
    dst = pl.multiple_of(dst_gran * GRAN, GRAN)
    return pltpu.make_async_copy(src_ref.at[pl.ds(src, GRAN), :], dst_ref.at[pl.ds(dst, GRAN), :], sem)


def _expert_kernel(te_ref, nt_ref, gsrc_ref, gdst_ref, ug_ref, xs_ref, wg_ref, wu_ref, wd_ref, ys_ref,
                   xbuf, ybuf, zbuf, in_sem, out_sem, zero_sem):
    j = pl.program_id(0)
    n_tiles = nt_ref[0]
    slot = lax.rem(j, 2)

    def tile_gather(tile, s):
        for g in range(TILE_GRANS):
            _granule_copy(xs_ref, gsrc_ref[tile * TILE_GRANS + g], xbuf.at[s], g,
                          in_sem.at[s]).start(priority=g % 2)

    @pl.when(j == 0)
    def _():
        tile_gather(0, 0)
        ybuf[1] = jnp.zeros((TE, D_MODEL), F32)
        zbuf[...] = jnp.zeros((GRAN, D_MODEL), F32)
        for half in range(2):
            spare = pltpu.make_async_copy(
                ybuf.at[1], ys_ref.at[pl.ds((SPARE_GRAN + half * TILE_GRANS) * GRAN, TE), :], out_sem.at[1])
            spare.start()
            spare.wait()

    def zero_copy(t, g):
        return _granule_copy(zbuf, 0, ys_ref, t * LOCAL_GRANS + g, zero_sem)

    @pl.when(jnp.logical_and(j >= 1, j <= N_TOK_TILES))
    def _():
        def wait(g, c):
            zero_copy(j - 1, g).wait()
            return c

        lax.fori_loop(ug_ref[j - 1], LOCAL_GRANS, wait, 0)

    @pl.when(j < N_TOK_TILES)
    def _():
        def start(g, c):
            zero_copy(j, g).start()
            return c

        lax.fori_loop(ug_ref[j], LOCAL_GRANS, start, 0)

    @pl.when(j + 1 < n_tiles)
    def _():
        tile_gather(j + 1, 1 - slot)

    @pl.when(jnp.logical_and(j >= 2, j - 2 < n_tiles))
    def _():
        pltpu.make_async_copy(ybuf.at[slot], ys_ref.at[pl.ds(0, TE), :], out_sem.at[slot]).wait()

    @pl.when(j < n_tiles)
    def _():
        pltpu.make_async_copy(xs_ref.at[pl.ds(0, TE), :], xbuf.at[slot], in_sem.at[slot]).wait()
        xb = xbuf[slot].astype(BF16)
        hg = jnp.dot(xb, wg_ref[0].astype(BF16), preferred_element_type=F32)
        hu = jnp.dot(xb, wu_ref[0].astype(BF16), preferred_element_type=F32)
        h = (hg * _sigmoid(hg) * hu).astype(BF16)
        ybuf[slot] = jnp.dot(h, wd_ref[0].astype(BF16), preferred_element_type=F32)
        for g in range(TILE_GRANS):
            _granule_copy(ybuf.at[slot], g, ys_ref, gdst_ref[j * TILE_GRANS + g],
                          out_sem.at[slot]).start(priority=g % 2)


def _expert_call(tile_expert, n_tiles, gsrc, gdst, used_grans, xs, wg, wu, wd):
    wsel = lambda j, te, nt, gs, gd, ug: (te[j], 0, 0)
    return pl.pallas_call(
        _expert_kernel,
        grid_spec=pltpu.PrefetchScalarGridSpec(
            num_scalar_prefetch=5,
            grid=(MAX_TILES + 2,),
            in_specs=[pl.BlockSpec(memory_space=pl.ANY),
                      pl.BlockSpec((1, D_MODEL, D_EXPERT), wsel),
                      pl.BlockSpec((1, D_MODEL, D_EXPERT), wsel),
                      pl.BlockSpec((1, D_EXPERT, D_MODEL), wsel)],
            out_specs=pl.BlockSpec(memory_space=pl.ANY),
            scratch_shapes=[pltpu.VMEM((2, TE, D_MODEL), F32), pltpu.VMEM((2, TE, D_MODEL), F32),
                            pltpu.VMEM((GRAN, D_MODEL), F32),
                            pltpu.SemaphoreType.DMA((2,)), pltpu.SemaphoreType.DMA((2,)),
                            pltpu.SemaphoreType.DMA],
        ),
        out_shape=jax.ShapeDtypeStruct(((SPARE_GRAN + 2 * TILE_GRANS) * GRAN, D_MODEL), F32),
        compiler_params=pltpu.CompilerParams(
            dimension_semantics=("arbitrary",), vmem_limit_bytes=VMEM_LIMIT),
        name="experts",
    )(tile_expert, n_tiles, gsrc, gdst, used_grans, xs, wg, wu, wd)


def _combine_kernel(ys_ref, x1_ref, rf_ref, g_ref, b_ref, out_ref):
    route = rf_ref[...].T
    col = lax.broadcasted_iota(I32, (TM, LOCAL_ROWS), 1).astype(F32)
    unsort = (jnp.where(col == route[:, 2:3], route[:, 0:1], 0.0)
              + jnp.where(col == route[:, 3:4], route[:, 1:2], 0.0)).astype(BF16)
    ffn = jnp.dot(unsort, ys_ref[...].astype(BF16), preferred_element_type=F32)
    h = ALPHA * x1_ref[...] + ffn
    mu = jnp.mean(h, axis=-1, keepdims=True)
    hc = h - mu
    var = jnp.mean(hc * hc, axis=-1, keepdims=True)
    out_ref[...] = hc * lax.rsqrt(var + LN_EPS) * g_ref[...] + b_ref[...]


def _combine_call(ys, x1, rf, g, b):
    return pl.pallas_call(
        _combine_kernel,
        grid=(N_TOK_TILES,),
        in_specs=[pl.BlockSpec((LOCAL_ROWS, D_MODEL), lambda i: (i, 0)),
                  pl.BlockSpec((TM, D_MODEL), lambda i: (i, 0)),
                  pl.BlockSpec((8, TM), lambda i: (0, i)),
                  pl.BlockSpec((1, D_MODEL), lambda i: (0, 0)),
                  pl.BlockSpec((1, D_MODEL), lambda i: (0, 0))],
        out_specs=pl.BlockSpec((TM, D_MODEL), lambda i: (i, 0)),
        out_shape=jax.ShapeDtypeStruct((TOKENS, D_MODEL), F32),
        compiler_params=pltpu.CompilerParams(
            dimension_semantics=("arbitrary",), vmem_limit_bytes=VMEM_LIMIT),
        name="combine_ln2",
    )(ys, x1, rf, g, b)


def _router_cols(w_router_group, w_router_expert):
    w = jnp.concatenate([w_router_group, jnp.zeros((D_MODEL, 4), F32), w_router_expert,
                         jnp.zeros((D_MODEL, 128 - ROUTER_ROWS), F32)], axis=1)
    hi = w.astype(BF16)
    lo = (w - hi.astype(F32)).astype(BF16)
    return hi, lo


def _layer(x, w_in, conv_w, w_out_conv, w_out_attn, w_o, ln1_g, ln1_b,
           w_router_group, w_router_expert, w_gate, w_up, w_down, ln2_g, ln2_b):
    slopes = jnp.asarray([2.0 ** (-8.0 * (h + 1) / N_HEADS) for h in range(N_HEADS)], F32)
    q, k, v, za, sgb = _proj_call(x, w_in, conv_w, w_out_conv)
    o_a, o_b = _attn_call(slopes, q, k, v)

    wr_hi, wr_lo = _router_cols(w_router_group, w_router_expert)
    x1, xs, rf, mt = _merge_call(
        o_a, o_b, za.reshape(TOKENS, D_MODEL), sgb.reshape(TOKENS, D_MODEL),
        x.reshape(TOKENS, D_MODEL), w_out_attn, w_o,
        ln1_g.reshape(1, D_MODEL), ln1_b.reshape(1, D_MODEL), wr_hi, wr_lo)

    grans = mt.reshape(N_TOK_TILES, N_EXPERTS, 128)[:, :, 0].astype(I32)
    local_start = jnp.cumsum(grans, axis=1) - grans
    grans_t = grans.T
    tiles_e = (jnp.sum(grans_t, axis=1) + TILE_GRANS - 1) // TILE_GRANS
    tile_end = jnp.cumsum(tiles_e)
    n_tiles = tile_end[-1].reshape(1)
    tile_ids = jnp.arange(MAX_TILES + 2, dtype=I32)
    tile_expert = jnp.minimum(
        jnp.sum((tile_ids[:, None] >= tile_end[None, :]).astype(I32), axis=1), N_EXPERTS - 1)
    run_slot = TILE_GRANS * (tile_end - tiles_e)[:, None] + jnp.cumsum(grans_t, axis=1) - grans_t
    run_src = jnp.arange(N_TOK_TILES, dtype=I32)[None, :] * LOCAL_GRANS + local_start.T
    pick = (tile_expert[:MAX_TILES, None] == jnp.arange(N_EXPERTS, dtype=I32)[None, :])[:, :, None]
    t_slot = jnp.sum(jnp.where(pick, run_slot[None], 0), axis=1)
    t_len = jnp.sum(jnp.where(pick, grans_t[None], 0), axis=1)
    t_src = jnp.sum(jnp.where(pick, run_src[None], 0), axis=1)
    slots = jnp.arange(MAX_TILES * TILE_GRANS, dtype=I32).reshape(MAX_TILES, TILE_GRANS)
    k = slots[:, :, None] - t_slot[:, None, :]
    hit = (k >= 0) & (k < t_len[:, None, :])
    gran = jnp.sum(jnp.where(hit, t_src[:, None, :] + k, 0), axis=2).reshape(-1)
    filled = (jnp.sum(hit.astype(I32), axis=2) > 0).reshape(-1)
    slots = slots.reshape(-1)
    gsrc = jnp.where(filled, gran, 0)
    gdst = jnp.where(filled, gran, SPARE_GRAN + slots % (2 * TILE_GRANS))

    ys = _expert_call(tile_expert, n_tiles, gsrc, gdst, jnp.sum(grans, axis=1), xs, w_gate, w_up, w_down)
    out = _combine_call(ys, x1, rf, ln2_g.reshape(1, D_MODEL), ln2_b.reshape(1, D_MODEL))
    return out.reshape(BATCH, SEQ, D_MODEL)


def kernel(x, w_in, conv_w, w_out_conv, w_out_attn, w_o, ln1_g, ln1_b, w_router_group, w_router_expert, w_gate, w_up, w_down, ln2_g, ln2_b):
    depth = w_in.shape[0]
    for l in range(depth):
        x = _layer(x, w_in[l], conv_w[l], w_out_conv[l], w_out_attn[l], w_o[l], ln1_g[l], ln1_b[l],
                   w_router_group[l], w_router_expert[l], w_gate[l], w_up[l], w_down[l], ln2_g[l], ln2_b[l])
    return x
```

```python
import functools

import jax
import jax.numpy as jnp
from jax import lax
from jax.experimental import pallas as pl
from jax.experimental.pallas import tpu as pltpu

F32 = jnp.float32
BF16 = jnp.bfloat16
U32 = jnp.uint32
I32 = jnp.int32

D_MODEL = 1024
BATCH = 8
SEQ = 2048
TOKENS = BATCH * SEQ
CONV_WIDTH = 512
N_HEADS = 8
HEAD_DIM = 64
ATTN_WIDTH = N_HEADS * HEAD_DIM
MOBA_BLOCK = 256
N_BLOCKS = SEQ // MOBA_BLOCK
MOBA_TOPK = 3
N_GROUPS = 4
EXPERTS_PER_GROUP = 8
N_EXPERTS = N_GROUPS * EXPERTS_PER_GROUP
D_EXPERT = 256
LN_EPS = 1e-5
ALPHA = 2.0 ** 0.25
IN_COLS = 3 * CONV_WIDTH + 3 * ATTN_WIDTH + 2 * D_MODEL
HALF = D_MODEL // 2

TM = 256
TE = 256
PV_ROWS = HEAD_DIM + 16
GRAN = 16
TILE_GRANS = TE // GRAN
N_TOK_TILES = TOKENS // TM
LOCAL_ROWS = -(-(2 * TM + N_EXPERTS * (GRAN - 1)) // 256) * 256
LOCAL_GRANS = LOCAL_ROWS // GRAN
SPARE_GRAN = N_TOK_TILES * LOCAL_GRANS
MAX_TILES = (2 * TOKENS + N_TOK_TILES * N_EXPERTS * (GRAN - 1)) // TE + N_EXPERTS
ROUTER_ROWS = 40
VMEM_LIMIT = 56 * 1024 * 1024
NEG_INF = float("-inf")


def _sigmoid(z):
    return 1.0 / (1.0 + jnp.exp(-z))


def _proj_kernel(x_ref, w_in_f32_ref, convw_ref, woc_f32_ref, q_ref, k_ref, v_ref, za_ref, sgb_ref,
                 ubuf, w_in_ref, woc_ref):
    s = pl.program_id(1)

    @pl.when((pl.program_id(0) == 0) & (s == 0))
    def _():
        for c in range(0, IN_COLS, CONV_WIDTH):
            w_in_ref[:, c:c + CONV_WIDTH] = w_in_f32_ref[:, c:c + CONV_WIDTH].astype(BF16)
        woc_ref[...] = woc_f32_ref[...].astype(BF16)

    xb = x_ref[0].astype(BF16)

    def proj(c0, c1):
        return jnp.dot(xb, w_in_ref[:, c0:c1], preferred_element_type=F32)

    c_b = proj(0, CONV_WIDTH)
    u = proj(CONV_WIDTH, 2 * CONV_WIDTH) * proj(2 * CONV_WIDTH, 3 * CONV_WIDTH)

    @pl.when(s == 0)
    def _():
        ubuf[0:8, :] = jnp.zeros((8, CONV_WIDTH), F32)

    ubuf[8:8 + TM, :] = u
    w = convw_ref[...]
    conv = w[2:3, :] * u + w[1:2, :] * ubuf[7:7 + TM, :] + w[0:1, :] * ubuf[6:6 + TM, :]
    ubuf[0:8, :] = u[TM - 8:TM, :]
    hc = (c_b * conv).astype(BF16)
    y_conv = jnp.dot(hc, woc_ref[...], preferred_element_type=F32)

    o = 3 * CONV_WIDTH
    q_ref[0] = (proj(o, o + ATTN_WIDTH) * (HEAD_DIM ** -0.5)).astype(BF16)
    k_ref[0] = proj(o + ATTN_WIDTH, o + 2 * ATTN_WIDTH).astype(BF16)
    v_ref[0] = proj(o + 2 * ATTN_WIDTH, o + 3 * ATTN_WIDTH).astype(BF16)
    o += 3 * ATTN_WIDTH
    za_ref[0] = (_sigmoid(proj(o, o + D_MODEL)) * y_conv).astype(BF16)
    sgb_ref[0] = _sigmoid(proj(o + D_MODEL, o + 2 * D_MODEL)).astype(BF16)


def _proj_call(x, w_in, conv_w, w_out_conv):
    tok_spec = lambda c: pl.BlockSpec((1, TM, c), lambda b, s: (b, s, 0))
    full = lambda shape: pl.BlockSpec(shape, lambda b, s: (0,) * len(shape))
    once = lambda shape: pl.BlockSpec(shape, lambda b, s: (0,) * len(shape), pipeline_mode=pl.Buffered(1))
    return pl.pallas_call(
        _proj_kernel,
        grid=(BATCH, SEQ // TM),
        in_specs=[tok_spec(D_MODEL), once((D_MODEL, IN_COLS)), full((3, CONV_WIDTH)),
                  once((CONV_WIDTH, D_MODEL))],
        out_specs=[tok_spec(ATTN_WIDTH), tok_spec(ATTN_WIDTH), tok_spec(ATTN_WIDTH),
                   tok_spec(D_MODEL), tok_spec(D_MODEL)],
        out_shape=[jax.ShapeDtypeStruct((BATCH, SEQ, ATTN_WIDTH), BF16)] * 3
        + [jax.ShapeDtypeStruct((BATCH, SEQ, D_MODEL), BF16)] * 2,
        scratch_shapes=[pltpu.VMEM((TM + 8, CONV_WIDTH), F32), pltpu.VMEM((D_MODEL, IN_COLS), BF16),
                        pltpu.VMEM((CONV_WIDTH, D_MODEL), BF16)],
        compiler_params=pltpu.CompilerParams(
            dimension_semantics=("arbitrary", "arbitrary"), vmem_limit_bytes=VMEM_LIMIT),
        name="proj",
    )(x, w_in, conv_w, w_out_conv)


def _attn_kernel(slopes_ref, qa_ref, qb_ref, k_ref, v_ref, oa_ref, ob_ref,
                 kaug_ref, vt_ref, kmean_ref, qaug_ref, pv_ref, mloc_ref, t_ref, p_ref):
    hp = pl.program_id(1)
    j = pl.program_id(2)
    blk = MOBA_BLOCK

    @pl.when(j == 0)
    def _():
        klane = lax.broadcasted_iota(I32, (blk, 128), 1)
        koff = lax.broadcasted_iota(I32, (blk, 128), 0).astype(F32)
        k_extra = jnp.where(klane == 0, koff, jnp.where(klane == 1, 1.0, 0.0)).astype(BF16)
        orow = lax.broadcasted_iota(I32, (PV_ROWS - HEAD_DIM, blk), 0)
        ones_rows = jnp.where(orow == 0, 1.0, 0.0).astype(BF16)
        for n in range(N_BLOCKS):
            kblk = k_ref[0, n * blk:(n + 1) * blk, :]
            kaug_ref[n, :, 0:128] = kblk
            kaug_ref[n, :, 128:256] = k_extra
            kmean_ref[n:n + 1, :] = jnp.mean(kblk.astype(F32), axis=0, keepdims=True)
            v_t = v_ref[0, n * blk:(n + 1) * blk, :].astype(F32).T.astype(BF16)
            for hh in range(2):
                vt_ref[n, hh, 0:HEAD_DIM, :] = v_t[hh * HEAD_DIM:(hh + 1) * HEAD_DIM, :]
                vt_ref[n, hh, HEAD_DIM:PV_ROWS, :] = ones_rows

    lane = lax.broadcasted_iota(I32, (1, 2 * blk), 1)
    slope_row = jnp.where(lane < blk, slopes_ref[2 * hp], slopes_ref[2 * hp + 1])
    qoff_row = jnp.where(lane < blk, lane, lane - blk).astype(F32)
    feat = lax.broadcasted_iota(I32, (2 * HEAD_DIM, blk), 0)
    arow = lax.broadcasted_iota(I32, (2 * HEAD_DIM, 2 * blk), 0)
    q_extra = jnp.where(arow == 0, slope_row, jnp.where(arow == 1, -slope_row * qoff_row, 0.0)).astype(BF16)
    blk_i = lax.broadcasted_iota(I32, (N_BLOCKS, 2 * blk), 0)
    kmean = kmean_ref[...].astype(BF16)
    key_i = lax.broadcasted_iota(I32, (blk, 2 * blk), 0)
    qry_j = lax.broadcasted_iota(I32, (blk, 2 * blk), 1)
    causal = key_i <= jnp.where(qry_j < blk, qry_j, qry_j - blk)

    def prepare(q_ref, slot, qblock):
        q_t = q_ref[0].astype(F32).T
        qcat = jnp.concatenate([jnp.where(feat < HEAD_DIM, q_t, 0.0), jnp.where(feat >= HEAD_DIM, q_t, 0.0)],
                               axis=1).astype(BF16)
        qaug_ref[slot, 0:2 * HEAD_DIM, :] = qcat
        qaug_ref[slot, 2 * HEAD_DIM:4 * HEAD_DIM, :] = q_extra
        gate = jnp.dot(kmean, qcat, preferred_element_type=F32)
        cnt = jnp.zeros((N_BLOCKS, 2 * blk), F32)
        for m in range(N_BLOCKS):
            gm = gate[m:m + 1, :]
            beats = (gm > gate) | ((gm == gate) & (blk_i > m))
            cnt = cnt + jnp.where(beats & (qblock > m), 1.0, 0.0)
        return jnp.where((blk_i < qblock) & (cnt < float(MOBA_TOPK)), 1.0, 0.0)

    qblock_a = j
    qblock_b = N_BLOCKS - 1 - j
    sel_a = prepare(qa_ref, 0, qblock_a)
    sel_b = prepare(qb_ref, 1, qblock_b)

    n_mid = N_BLOCKS - 1
    slots = [(0, 0, qblock_a, True)]
    mids = []
    for s in range(1, n_mid + 1):
        is_a = s <= j
        slots.append((s, jnp.where(is_a, 0, 1), jnp.where(is_a, s - 1, s - 1 - j), False))
        mids.append((is_a, slots[-1][2]))
    slots.append((n_mid + 1, 1, qblock_b, True))

    for s, which, kb, _ in slots:
        t_ref[s] = jnp.dot(kaug_ref[kb], qaug_ref[which], preferred_element_type=F32)
    for s, _, _, own in slots:
        t = t_ref[s]
        if own:
            t = jnp.where(causal, t, NEG_INF)
        m_loc = jnp.max(t, axis=0, keepdims=True)
        p_ref[s] = jnp.exp((t - m_loc).astype(BF16))
        mloc_ref[s:s + 1, :] = m_loc
    for s, _, kb, _ in slots:
        pv_ref[s, 0] = jnp.dot(vt_ref[kb, 0], p_ref[s, :, 0:blk], preferred_element_type=F32)
        pv_ref[s, 1] = jnp.dot(vt_ref[kb, 1], p_ref[s, :, blk:2 * blk], preferred_element_type=F32)

    def combine(o_ref, own_slot, sel, qblock, mine):
        neg = jnp.full((1, 2 * blk), -1e30, F32)
        pieces = [(own_slot, mloc_ref[own_slot:own_slot + 1, :])]
        for s, (is_a, kb) in enumerate(mids, start=1):
            selrow = jnp.sum(jnp.where(blk_i == kb, sel, 0.0), axis=0, keepdims=True)
            belongs = jnp.where(is_a, 1.0, 0.0) if mine else jnp.where(is_a, 0.0, 1.0)
            used = selrow * belongs > 0.5
            shift = slope_row * ((kb - qblock) * blk).astype(F32)
            pieces.append((s, jnp.where(used, mloc_ref[s:s + 1, :] + shift, neg)))
        m_all = pieces[0][1]
        for _, m_s in pieces[1:]:
            m_all = jnp.maximum(m_all, m_s)
        acc = [jnp.zeros((PV_ROWS, blk), F32), jnp.zeros((PV_ROWS, blk), F32)]
        for s, m_s in pieces:
            w = jnp.exp(m_s - m_all)
            for hh in range(2):
                acc[hh] = acc[hh] + pv_ref[s, hh] * w[:, hh * blk:(hh + 1) * blk]
        o_t = jnp.concatenate([a[0:HEAD_DIM, :] / a[HEAD_DIM:HEAD_DIM + 1, :] for a in acc], axis=0)
        o_ref[0] = o_t.T.astype(BF16)

    combine(oa_ref, 0, sel_a, qblock_a, True)
    combine(ob_ref, n_mid + 1, sel_b, qblock_b, False)


def _attn_call(slopes, q, k, v):
    half = N_BLOCKS // 2
    o_a, o_b = pl.pallas_call(
        _attn_kernel,
        grid_spec=pltpu.PrefetchScalarGridSpec(
            num_scalar_prefetch=1,
            grid=(BATCH, N_HEADS // 2, half),
            in_specs=[
                pl.BlockSpec((1, MOBA_BLOCK, 128), lambda b, h, j, sl: (b, j, h)),
                pl.BlockSpec((1, MOBA_BLOCK, 128), lambda b, h, j, sl: (b, N_BLOCKS - 1 - j, h)),
                pl.BlockSpec((1, SEQ, 128), lambda b, h, j, sl: (b, 0, h)),
                pl.BlockSpec((1, SEQ, 128), lambda b, h, j, sl: (b, 0, h)),
            ],
            out_specs=[pl.BlockSpec((1, MOBA_BLOCK, 128), lambda b, h, j, sl: (b, j, h)),
                       pl.BlockSpec((1, MOBA_BLOCK, 128), lambda b, h, j, sl: (b, j, h))],
            scratch_shapes=[
                pltpu.VMEM((N_BLOCKS, MOBA_BLOCK, 256), BF16),
                pltpu.VMEM((N_BLOCKS, 2, PV_ROWS, MOBA_BLOCK), BF16),
                pltpu.VMEM((N_BLOCKS, 128), F32),
                pltpu.VMEM((2, 256, 2 * MOBA_BLOCK), BF16),
                pltpu.VMEM((N_BLOCKS + 1, 2, PV_ROWS, MOBA_BLOCK), F32),
                pltpu.VMEM((16, 2 * MOBA_BLOCK), F32),
                pltpu.VMEM((N_BLOCKS + 1, MOBA_BLOCK, 2 * MOBA_BLOCK), F32),
                pltpu.VMEM((N_BLOCKS + 1, MOBA_BLOCK, 2 * MOBA_BLOCK), BF16),
            ],
        ),
        out_shape=[jax.ShapeDtypeStruct((BATCH, SEQ // 2, ATTN_WIDTH), BF16)] * 2,
        compiler_params=pltpu.CompilerParams(
            dimension_semantics=("arbitrary", "arbitrary", "arbitrary"), vmem_limit_bytes=VMEM_LIMIT),
        name="moba_attn",
    )(slopes, q, q, k, v)
    return o_a, o_b


def _merge_kernel(oa_ref, ob_ref, za_ref, sgb_ref, x_ref, woa_f32_ref, wo_f32_ref, g_ref, b_ref,
                  wr_hi_ref, wr_lo_ref, x1_ref, xs_ref, rf_ref, mt_ref, woa_ref, wo_ref):
    i = pl.program_id(0)

    @pl.when(i == 0)
    def _():
        woa_ref[...] = woa_f32_ref[...].astype(BF16)
        wo_ref[...] = wo_f32_ref[...].astype(BF16)

    o = jnp.where(lax.rem(i, N_BLOCKS) < N_BLOCKS // 2, oa_ref[0], ob_ref[0])
    y_attn = jnp.dot(o, woa_ref[...], preferred_element_type=F32)
    y = za_ref[...].astype(F32) + sgb_ref[...].astype(F32) * y_attn
    mix = jnp.dot(y.astype(BF16), wo_ref[...], preferred_element_type=F32)
    h = ALPHA * x_ref[...] + mix
    mu = jnp.mean(h, axis=-1, keepdims=True)
    hc = h - mu
    var = jnp.mean(hc * hc, axis=-1, keepdims=True)
    x1 = hc * lax.rsqrt(var + LN_EPS) * g_ref[...] + b_ref[...]
    x1_ref[...] = x1

    xh = x1.astype(BF16)
    xl = (x1 - xh.astype(F32)).astype(BF16)
    wh = wr_hi_ref[...]
    logits = (jnp.dot(xh, wh, preferred_element_type=F32)
              + jnp.dot(xl, wh, preferred_element_type=F32)
              + jnp.dot(xh, wr_lo_ref[...], preferred_element_type=F32)).T

    row8 = lax.broadcasted_iota(I32, (8, TM), 0).astype(F32)
    gl = jnp.where(row8 < float(N_GROUPS), logits[0:8, :], NEG_INF)
    gexp = jnp.exp(gl - jnp.max(gl, axis=0, keepdims=True))
    gprob = gexp / jnp.sum(gexp, axis=0, keepdims=True)
    ptop = jnp.max(gprob, axis=0, keepdims=True)
    gtop = jnp.min(jnp.where(gprob == ptop, row8, 8.0), axis=0, keepdims=True)
    el = logits[8:ROUTER_ROWS, :]
    eg = jnp.where(gtop == 0.0, el[0:8, :],
                   jnp.where(gtop == 1.0, el[8:16, :], jnp.where(gtop == 2.0, el[16:24, :], el[24:32, :])))
    m1 = jnp.max(eg, axis=0, keepdims=True)
    i1 = jnp.min(jnp.where(eg == m1, row8, 8.0), axis=0, keepdims=True)
    eg2 = jnp.where(row8 == i1, NEG_INF, eg)
    m2 = jnp.max(eg2, axis=0, keepdims=True)
    i2 = jnp.min(jnp.where(eg2 == m2, row8, 8.0), axis=0, keepdims=True)
    t2 = jnp.exp(m2 - m1)
    gate1 = ptop * (1.0 / (1.0 + t2))
    gate2 = ptop * (t2 / (1.0 + t2))
    e1 = gtop * float(EXPERTS_PER_GROUP) + i1
    e2 = gtop * float(EXPERTS_PER_GROUP) + i2

    erow = lax.broadcasted_iota(I32, (N_EXPERTS, TM), 0).astype(F32)
    oh1 = jnp.where(erow == e1, 1.0, 0.0)
    oh2 = jnp.where(erow == e2, 1.0, 0.0)
    oh = oh1 + oh2
    ta = lax.broadcasted_iota(I32, (TM, TM), 0)
    tb = lax.broadcasted_iota(I32, (TM, TM), 1)
    upper = jnp.where(ta < tb, 1.0, 0.0).astype(BF16)
    cum = jnp.dot(oh.astype(BF16), upper, preferred_element_type=F32)

    n_e = jnp.sum(oh, axis=1, keepdims=True)
    m_e = jnp.floor((n_e + float(GRAN - 1)) * (1.0 / GRAN))
    m_rep = jnp.broadcast_to(m_e, (N_EXPERTS, 128))
    ea = lax.broadcasted_iota(I32, (N_EXPERTS, N_EXPERTS), 0)
    eb = lax.broadcasted_iota(I32, (N_EXPERTS, N_EXPERTS), 1)
    lower = jnp.where(eb < ea, 1.0, 0.0).astype(BF16)
    run_start = jnp.dot(lower, m_rep.astype(BF16), preferred_element_type=F32)
    tot = cum + float(GRAN) * run_start[:, 0:1]
    lp1 = jnp.sum(oh1 * tot, axis=0, keepdims=True)
    lp2 = jnp.sum(oh2 * tot, axis=0, keepdims=True)
    lrow = lax.broadcasted_iota(I32, (LOCAL_ROWS, TM), 0).astype(F32)
    perm = jnp.where((lrow == lp1) | (lrow == lp2), 1.0, 0.0).astype(BF16)
    xs_ref[...] = jnp.dot(perm, xh, preferred_element_type=F32).astype(BF16)

    zero = jnp.zeros((1, TM), F32)
    rf_ref[...] = jnp.concatenate([gate1, gate2, lp1, lp2, zero, zero, zero, zero], axis=0)
    mt_ref[...] = m_rep


def _merge_call(o_a, o_b, za, sgb, x, woa, wo, g, b, wr_hi, wr_lo):
    tok = lambda c: pl.BlockSpec((TM, c), lambda i: (i, 0))
    full = lambda shape: pl.BlockSpec(shape, lambda i: (0,) * len(shape))
    lanes = pl.BlockSpec((8, TM), lambda i: (0, i))
    half = N_BLOCKS // 2
    o_a_spec = pl.BlockSpec((1, TM, ATTN_WIDTH), lambda i: (i // N_BLOCKS, jnp.minimum(i % N_BLOCKS, half - 1), 0))
    o_b_spec = pl.BlockSpec(
        (1, TM, ATTN_WIDTH), lambda i: (i // N_BLOCKS, jnp.minimum(N_BLOCKS - 1 - i % N_BLOCKS, half - 1), 0))
    return pl.pallas_call(
        _merge_kernel,
        grid=(TOKENS // TM,),
        in_specs=[o_a_spec, o_b_spec, tok(D_MODEL), tok(D_MODEL), tok(D_MODEL),
                  full((ATTN_WIDTH, D_MODEL)), full((D_MODEL, D_MODEL)), full((1, D_MODEL)),
                  full((1, D_MODEL)), full((D_MODEL, 128)), full((D_MODEL, 128))],
        out_specs=[tok(D_MODEL), pl.BlockSpec((LOCAL_ROWS, D_MODEL), lambda i: (i, 0)), lanes,
                   pl.BlockSpec((N_EXPERTS, 128), lambda i: (i, 0))],
        out_shape=[jax.ShapeDtypeStruct((TOKENS, D_MODEL), F32),
                   jax.ShapeDtypeStruct((N_TOK_TILES * LOCAL_ROWS, D_MODEL), BF16),
                   jax.ShapeDtypeStruct((8, TOKENS), F32),
                   jax.ShapeDtypeStruct((N_TOK_TILES * N_EXPERTS, 128), F32)],
        scratch_shapes=[pltpu.VMEM((ATTN_WIDTH, D_MODEL), BF16), pltpu.VMEM((D_MODEL, D_MODEL), BF16)],
        compiler_params=pltpu.CompilerParams(
            dimension_semantics=("arbitrary",), vmem_limit_bytes=VMEM_LIMIT),
        name="merge_ln1_route",
    )(o_a, o_b, za, sgb, x, woa, wo, g, b, wr_hi, wr_lo)


def _granule_copy(src_ref, src_gran, dst_ref, dst_gran, sem):
    src = pl.multiple_of(src_gran * GRAN, GRAN)
    dst = pl.multiple_of(dst_gran * GRAN, GRAN)
    return pltpu.make_async_copy(src_ref.at[pl.ds(src, GRAN), :], dst_ref.at[pl.ds(dst, GRAN), :], sem)


def _expert_kernel(te_ref, nt_ref, gsrc_ref, gdst_ref, ug_ref, xs_ref, wg_ref, wu_ref, wd_ref, ys_ref,
                   xbuf, ybuf, zbuf, in_sem, out_sem, zero_sem):
    j = pl.program_id(0)
    n_tiles = nt_ref[0]
    slot = lax.rem(j, 2)

    def tile_gather(tile, s):
        for g in range(TILE_GRANS):
            _granule_copy(xs_ref, gsrc_ref[tile * TILE_GRANS + g], xbuf.at[s], g,
                          in_sem.at[s]).start(priority=g % 2)

    @pl.when(j == 0)
    def _():
        tile_gather(0, 0)
        ybuf[1] = jnp.zeros((TE, D_MODEL), BF16)
        zbuf[...] = jnp.zeros((GRAN, D_MODEL), BF16)
        for half in range(2):
            spare = pltpu.make_async_copy(
                ybuf.at[1], ys_ref.at[pl.ds((SPARE_GRAN + half * TILE_GRANS) * GRAN, TE), :], out_sem.at[1])
            spare.start()
            spare.wait()

    def zero_copy(t, g):
        return _granule_copy(zbuf, 0, ys_ref, t * LOCAL_GRANS + g, zero_sem)

    @pl.when(jnp.logical_and(j >= 1, j <= N_TOK_TILES))
    def _():
        def wait(g, c):
            zero_copy(j - 1, g).wait()
            return c

        lax.fori_loop(ug_ref[j - 1], LOCAL_GRANS, wait, 0)

    @pl.when(j < N_TOK_TILES)
    def _():
        def start(g, c):
            zero_copy(j, g).start()
            return c

        lax.fori_loop(ug_ref[j], LOCAL_GRANS, start, 0)

    @pl.when(j + 1 < n_tiles)
    def _():
        tile_gather(j + 1, 1 - slot)

    @pl.when(jnp.logical_and(j >= 2, j - 2 < n_tiles))
    def _():
        pltpu.make_async_copy(ybuf.at[slot], ys_ref.at[pl.ds(0, TE), :], out_sem.at[slot]).wait()

    @pl.when(j < n_tiles)
    def _():
        pltpu.make_async_copy(xs_ref.at[pl.ds(0, TE), :], xbuf.at[slot], in_sem.at[slot]).wait()
        xb = xbuf[slot]
        hg = jnp.dot(xb, wg_ref[0].astype(BF16), preferred_element_type=F32)
        hu = jnp.dot(xb, wu_ref[0].astype(BF16), preferred_element_type=F32)
        h = (hg * _sigmoid(hg) * hu).astype(BF16)
        ybuf[slot] = jnp.dot(h, wd_ref[0].astype(BF16), preferred_element_type=F32).astype(BF16)
        for g in range(TILE_GRANS):
            _granule_copy(ybuf.at[slot], g, ys_ref, gdst_ref[j * TILE_GRANS + g],
                          out_sem.at[slot]).start(priority=g % 2)


def _expert_call(tile_expert, n_tiles, gsrc, gdst, used_grans, xs, wg, wu, wd):
    wsel = lambda j, te, nt, gs, gd, ug: (te[j], 0, 0)
    return pl.pallas_call(
        _expert_kernel,
        grid_spec=pltpu.PrefetchScalarGridSpec(
            num_scalar_prefetch=5,
            grid=(MAX_TILES + 2,),
            in_specs=[pl.BlockSpec(memory_space=pl.ANY),
                      pl.BlockSpec((1, D_MODEL, D_EXPERT), wsel),
                      pl.BlockSpec((1, D_MODEL, D_EXPERT), wsel),
                      pl.BlockSpec((1, D_EXPERT, D_MODEL), wsel)],
            out_specs=pl.BlockSpec(memory_space=pl.ANY),
            scratch_shapes=[pltpu.VMEM((2, TE, D_MODEL), BF16), pltpu.VMEM((2, TE, D_MODEL), BF16),
                            pltpu.VMEM((GRAN, D_MODEL), BF16),
                            pltpu.SemaphoreType.DMA((2,)), pltpu.SemaphoreType.DMA((2,)),
                            pltpu.SemaphoreType.DMA],
        ),
        out_shape=jax.ShapeDtypeStruct(((SPARE_GRAN + 2 * TILE_GRANS) * GRAN, D_MODEL), BF16),
        compiler_params=pltpu.CompilerParams(
            dimension_semantics=("arbitrary",), vmem_limit_bytes=VMEM_LIMIT),
        name="experts",
    )(tile_expert, n_tiles, gsrc, gdst, used_grans, xs, wg, wu, wd)


def _combine_kernel(ys_ref, x1_ref, rf_ref, g_ref, b_ref, out_ref):
    route = rf_ref[...].T
    col = lax.broadcasted_iota(I32, (TM, LOCAL_ROWS), 1).astype(F32)
    unsort = (jnp.where(col == route[:, 2:3], route[:, 0:1], 0.0)
              + jnp.where(col == route[:, 3:4], route[:, 1:2], 0.0)).astype(BF16)
    ffn = jnp.dot(unsort, ys_ref[...], preferred_element_type=F32)
    h = ALPHA * x1_ref[...] + ffn
    mu = jnp.mean(h, axis=-1, keepdims=True)
    hc = h - mu
    var = jnp.mean(hc * hc, axis=-1, keepdims=True)
    out_ref[...] = hc * lax.rsqrt(var + LN_EPS) * g_ref[...] + b_ref[...]


def _combine_call(ys, x1, rf, g, b):
    return pl.pallas_call(
        _combine_kernel,
        grid=(N_TOK_TILES,),
        in_specs=[pl.BlockSpec((LOCAL_ROWS, D_MODEL), lambda i: (i, 0)),
                  pl.BlockSpec((TM, D_MODEL), lambda i: (i, 0)),
                  pl.BlockSpec((8, TM), lambda i: (0, i)),
                  pl.BlockSpec((1, D_MODEL), lambda i: (0, 0)),
                  pl.BlockSpec((1, D_MODEL), lambda i: (0, 0))],
        out_specs=pl.BlockSpec((TM, D_MODEL), lambda i: (i, 0)),
        out_shape=jax.ShapeDtypeStruct((TOKENS, D_MODEL), F32),
        compiler_params=pltpu.CompilerParams(
            dimension_semantics=("arbitrary",), vmem_limit_bytes=VMEM_LIMIT),
        name="combine_ln2",
    )(ys, x1, rf, g, b)


def _router_cols(w_router_group, w_router_expert):
    w = jnp.concatenate([w_router_group, jnp.zeros((D_MODEL, 4), F32), w_router_expert,
                         jnp.zeros((D_MODEL, 128 - ROUTER_ROWS), F32)], axis=1)
    hi = w.astype(BF16)
    lo = (w - hi.astype(F32)).astype(BF16)
    return hi, lo


def _layer(x, w_in, conv_w, w_out_conv, w_out_attn, w_o, ln1_g, ln1_b,
           w_router_group, w_router_expert, w_gate, w_up, w_down, ln2_g, ln2_b):
    slopes = jnp.asarray([2.0 ** (-8.0 * (h + 1) / N_HEADS) for h in range(N_HEADS)], F32)
    q, k, v, za, sgb = _proj_call(x, w_in, conv_w, w_out_conv)
    o_a, o_b = _attn_call(slopes, q, k, v)

    wr_hi, wr_lo = _router_cols(w_router_group, w_router_expert)
    x1, xs, rf, mt = _merge_call(
        o_a, o_b, za.reshape(TOKENS, D_MODEL), sgb.reshape(TOKENS, D_MODEL),
        x.reshape(TOKENS, D_MODEL), w_out_attn, w_o,
        ln1_g.reshape(1, D_MODEL), ln1_b.reshape(1, D_MODEL), wr_hi, wr_lo)

    grans = mt.reshape(N_TOK_TILES, N_EXPERTS, 128)[:, :, 0].astype(I32)
    local_start = jnp.cumsum(grans, axis=1) - grans
    grans_t = grans.T
    tiles_e = (jnp.sum(grans_t, axis=1) + TILE_GRANS - 1) // TILE_GRANS
    tile_end = jnp.cumsum(tiles_e)
    n_tiles = tile_end[-1].reshape(1)
    tile_ids = jnp.arange(MAX_TILES + 2, dtype=I32)
    tile_expert = jnp.minimum(
        jnp.sum((tile_ids[:, None] >= tile_end[None, :]).astype(I32), axis=1), N_EXPERTS - 1)
    run_slot = TILE_GRANS * (tile_end - tiles_e)[:, None] + jnp.cumsum(grans_t, axis=1) - grans_t
    run_src = jnp.arange(N_TOK_TILES, dtype=I32)[None, :] * LOCAL_GRANS + local_start.T
    pick = (tile_expert[:MAX_TILES, None] == jnp.arange(N_EXPERTS, dtype=I32)[None, :])[:, :, None]
    t_slot = jnp.sum(jnp.where(pick, run_slot[None], 0), axis=1)
    t_len = jnp.sum(jnp.where(pick, grans_t[None], 0), axis=1)
    t_src = jnp.sum(jnp.where(pick, run_src[None], 0), axis=1)
    slots = jnp.arange(MAX_TILES * TILE_GRANS, dtype=I32).reshape(MAX_TILES, TILE_GRANS)
    k = slots[:, :, None] - t_slot[:, None, :]
    hit = (k >= 0) & (k < t_len[:, None, :])
    gran = jnp.sum(jnp.where(hit, t_src[:, None, :] + k, 0), axis=2).reshape(-1)
    filled = (jnp.sum(hit.astype(I32), axis=2) > 0).reshape(-1)
    slots = slots.reshape(-1)
    gsrc = jnp.where(filled, gran, 0)
    gdst = jnp.where(filled, gran, SPARE_GRAN + slots % (2 * TILE_GRANS))

    ys = _expert_call(tile_expert, n_tiles, gsrc, gdst, jnp.sum(grans, axis=1), xs, w_gate, w_up, w_down)
    out = _combine_call(ys, x1, rf, ln2_g.reshape(1, D_MODEL), ln2_b.reshape(1, D_MODEL))
    return out.reshape(BATCH, SEQ, D_MODEL)


def kernel(x, w_in, conv_w, w_out_conv, w_out_attn, w_o, ln1_g, ln1_b, w_router_group, w_router_expert, w_gate, w_up, w_down, ln2_g, ln2_b):
    depth = w_in.shape[0]
    for l in range(depth):
        x = _layer(x, w_in[l], conv_w[l], w_out_conv[l], w_out_attn[l], w_o[l], ln1_g[l], ln1_b[l],
                   w_router_group[l], w_router_expert[l], w_gate[l], w_up[l], w_down[l], ln2_g[l], ln2_b[l])
    return x
```

```python
import functools

import jax
import jax.numpy as jnp
from jax import lax
from jax.experimental import pallas as pl
from jax.experimental.pallas import tpu as pltpu

F32 = jnp.float32
BF16 = jnp.bfloat16
U32 = jnp.uint32
I32 = jnp.int32

D_MODEL = 1024
BATCH = 8
SEQ = 2048
TOKENS = BATCH * SEQ
CONV_WIDTH = 512
N_HEADS = 8
HEAD_DIM = 64
ATTN_WIDTH = N_HEADS * HEAD_DIM
MOBA_BLOCK = 256
N_BLOCKS = SEQ // MOBA_BLOCK
MOBA_TOPK = 3
N_GROUPS = 4
EXPERTS_PER_GROUP = 8
N_EXPERTS = N_GROUPS * EXPERTS_PER_GROUP
D_EXPERT = 256
LN_EPS = 1e-5
ALPHA = 2.0 ** 0.25
IN_COLS = 3 * CONV_WIDTH + 3 * ATTN_WIDTH + 2 * D_MODEL
HALF = D_MODEL // 2

TM = 256
TE = 256
PV_ROWS = HEAD_DIM + 16
GRAN = 16
TILE_GRANS = TE // GRAN
N_TOK_TILES = TOKENS // TM
LOCAL_ROWS = -(-(2 * TM + N_EXPERTS * (GRAN - 1)) // 256) * 256
LOCAL_GRANS = LOCAL_ROWS // GRAN
SPARE_GRAN = N_TOK_TILES * LOCAL_GRANS
MAX_TILES = (2 * TOKENS + N_TOK_TILES * N_EXPERTS * (GRAN - 1)) // TE + N_EXPERTS
NBUF = 4
ROUTER_ROWS = 40
VMEM_LIMIT = 56 * 1024 * 1024
NEG_INF = float("-inf")


def _sigmoid(z):
    return 1.0 / (1.0 + jnp.exp(-z))


def _proj_kernel(x_ref, w_in_f32_ref, convw_ref, woc_f32_ref, q_ref, k_ref, v_ref, za_ref, sgb_ref,
                 ubuf, w_in_ref, woc_ref):
    s = pl.program_id(1)

    @pl.when((pl.program_id(0) == 0) & (s == 0))
    def _():
        for c in range(0, IN_COLS, CONV_WIDTH):
            w_in_ref[:, c:c + CONV_WIDTH] = w_in_f32_ref[:, c:c + CONV_WIDTH].astype(BF16)
        woc_ref[...] = woc_f32_ref[...].astype(BF16)

    xb = x_ref[0].astype(BF16)

    def proj(c0, c1):
        return jnp.dot(xb, w_in_ref[:, c0:c1], preferred_element_type=F32)

    c_b = proj(0, CONV_WIDTH)
    u = proj(CONV_WIDTH, 2 * CONV_WIDTH) * proj(2 * CONV_WIDTH, 3 * CONV_WIDTH)

    @pl.when(s == 0)
    def _():
        ubuf[0:8, :] = jnp.zeros((8, CONV_WIDTH), F32)

    ubuf[8:8 + TM, :] = u
    w = convw_ref[...]
    conv = w[2:3, :] * u + w[1:2, :] * ubuf[7:7 + TM, :] + w[0:1, :] * ubuf[6:6 + TM, :]
    ubuf[0:8, :] = u[TM - 8:TM, :]
    hc = (c_b * conv).astype(BF16)
    y_conv = jnp.dot(hc, woc_ref[...], preferred_element_type=F32)

    o = 3 * CONV_WIDTH
    q_ref[0] = (proj(o, o + ATTN_WIDTH) * (HEAD_DIM ** -0.5)).astype(BF16)
    k_ref[0] = proj(o + ATTN_WIDTH, o + 2 * ATTN_WIDTH).astype(BF16)
    v_ref[0] = proj(o + 2 * ATTN_WIDTH, o + 3 * ATTN_WIDTH).astype(BF16)
    o += 3 * ATTN_WIDTH
    za_ref[0] = (_sigmoid(proj(o, o + D_MODEL)) * y_conv).astype(BF16)
    sgb_ref[0] = _sigmoid(proj(o + D_MODEL, o + 2 * D_MODEL)).astype(BF16)


def _proj_call(x, w_in, conv_w, w_out_conv):
    tok_spec = lambda c: pl.BlockSpec((1, TM, c), lambda b, s: (b, s, 0))
    full = lambda shape: pl.BlockSpec(shape, lambda b, s: (0,) * len(shape))
    once = lambda shape: pl.BlockSpec(shape, lambda b, s: (0,) * len(shape), pipeline_mode=pl.Buffered(1))
    return pl.pallas_call(
        _proj_kernel,
        grid=(BATCH, SEQ // TM),
        in_specs=[tok_spec(D_MODEL), once((D_MODEL, IN_COLS)), full((3, CONV_WIDTH)),
                  once((CONV_WIDTH, D_MODEL))],
        out_specs=[tok_spec(ATTN_WIDTH), tok_spec(ATTN_WIDTH), tok_spec(ATTN_WIDTH),
                   tok_spec(D_MODEL), tok_spec(D_MODEL)],
        out_shape=[jax.ShapeDtypeStruct((BATCH, SEQ, ATTN_WIDTH), BF16)] * 3
        + [jax.ShapeDtypeStruct((BATCH, SEQ, D_MODEL), BF16)] * 2,
        scratch_shapes=[pltpu.VMEM((TM + 8, CONV_WIDTH), F32), pltpu.VMEM((D_MODEL, IN_COLS), BF16),
                        pltpu.VMEM((CONV_WIDTH, D_MODEL), BF16)],
        compiler_params=pltpu.CompilerParams(
            dimension_semantics=("arbitrary", "arbitrary"), vmem_limit_bytes=VMEM_LIMIT),
        name="proj",
    )(x, w_in, conv_w, w_out_conv)


def _attn_kernel(slopes_ref, qa_ref, qb_ref, k_ref, v_ref, oa_ref, ob_ref,
                 kaug_ref, vt_ref, kmean_ref, qaug_ref, pv_ref, mloc_ref, t_ref, p_ref):
    hp = pl.program_id(1)
    j = pl.program_id(2)
    blk = MOBA_BLOCK

    @pl.when(j == 0)
    def _():
        klane = lax.broadcasted_iota(I32, (blk, 128), 1)
        koff = lax.broadcasted_iota(I32, (blk, 128), 0).astype(F32)
        k_extra = jnp.where(klane == 0, koff, jnp.where(klane == 1, 1.0, 0.0)).astype(BF16)
        orow = lax.broadcasted_iota(I32, (PV_ROWS - HEAD_DIM, blk), 0)
        ones_rows = jnp.where(orow == 0, 1.0, 0.0).astype(BF16)
        for n in range(N_BLOCKS):
            kblk = k_ref[0, n * blk:(n + 1) * blk, :]
            kaug_ref[n, :, 0:128] = kblk
            kaug_ref[n, :, 128:256] = k_extra
            kmean_ref[n:n + 1, :] = jnp.mean(kblk.astype(F32), axis=0, keepdims=True)
            v_t = v_ref[0, n * blk:(n + 1) * blk, :].astype(F32).T.astype(BF16)
            for hh in range(2):
                vt_ref[n, hh, 0:HEAD_DIM, :] = v_t[hh * HEAD_DIM:(hh + 1) * HEAD_DIM, :]
                vt_ref[n, hh, HEAD_DIM:PV_ROWS, :] = ones_rows

    lane = lax.broadcasted_iota(I32, (1, 2 * blk), 1)
    slope_row = jnp.where(lane < blk, slopes_ref[2 * hp], slopes_ref[2 * hp + 1])
    qoff_row = jnp.where(lane < blk, lane, lane - blk).astype(F32)
    feat = lax.broadcasted_iota(I32, (2 * HEAD_DIM, blk), 0)
    arow = lax.broadcasted_iota(I32, (2 * HEAD_DIM, 2 * blk), 0)
    q_extra = jnp.where(arow == 0, slope_row, jnp.where(arow == 1, -slope_row * qoff_row, 0.0)).astype(BF16)
    blk_i = lax.broadcasted_iota(I32, (N_BLOCKS, 2 * blk), 0)
    kmean = kmean_ref[...].astype(BF16)
    key_i = lax.broadcasted_iota(I32, (blk, 2 * blk), 0)
    qry_j = lax.broadcasted_iota(I32, (blk, 2 * blk), 1)
    causal = key_i <= jnp.where(qry_j < blk, qry_j, qry_j - blk)

    def prepare(q_ref, slot, qblock):
        q_t = q_ref[0].astype(F32).T
        qcat = jnp.concatenate([jnp.where(feat < HEAD_DIM, q_t, 0.0), jnp.where(feat >= HEAD_DIM, q_t, 0.0)],
                               axis=1).astype(BF16)
        qaug_ref[slot, 0:2 * HEAD_DIM, :] = qcat
        qaug_ref[slot, 2 * HEAD_DIM:4 * HEAD_DIM, :] = q_extra
        gate = jnp.dot(kmean, qcat, preferred_element_type=F32)
        cnt = jnp.zeros((N_BLOCKS, 2 * blk), F32)
        for m in range(N_BLOCKS):
            gm = gate[m:m + 1, :]
            beats = (gm > gate) | ((gm == gate) & (blk_i > m))
            cnt = cnt + jnp.where(beats & (qblock > m), 1.0, 0.0)
        return jnp.where((blk_i < qblock) & (cnt < float(MOBA_TOPK)), 1.0, 0.0)

    qblock_a = j
    qblock_b = N_BLOCKS - 1 - j
    sel_a = prepare(qa_ref, 0, qblock_a)
    sel_b = prepare(qb_ref, 1, qblock_b)

    n_mid = N_BLOCKS - 1
    slots = [(0, 0, qblock_a, True)]
    mids = []
    for s in range(1, n_mid + 1):
        is_a = s <= j
        slots.append((s, jnp.where(is_a, 0, 1), jnp.where(is_a, s - 1, s - 1 - j), False))
        mids.append((is_a, slots[-1][2]))
    slots.append((n_mid + 1, 1, qblock_b, True))

    for s, which, kb, _ in slots:
        t_ref[s] = jnp.dot(kaug_ref[kb], qaug_ref[which], preferred_element_type=F32)
    for s, _, _, own in slots:
        t = t_ref[s]
        if own:
            t = jnp.where(causal, t, NEG_INF)
        m_loc = jnp.max(t, axis=0, keepdims=True)
        p_ref[s] = jnp.exp((t - m_loc).astype(BF16))
        mloc_ref[s:s + 1, :] = m_loc
    for s, _, kb, _ in slots:
        pv_ref[s, 0] = jnp.dot(vt_ref[kb, 0], p_ref[s, :, 0:blk], preferred_element_type=F32)
        pv_ref[s, 1] = jnp.dot(vt_ref[kb, 1], p_ref[s, :, blk:2 * blk], preferred_element_type=F32)

    def combine(o_ref, own_slot, sel, qblock, mine):
        neg = jnp.full((1, 2 * blk), -1e30, F32)
        pieces = [(own_slot, mloc_ref[own_slot:own_slot + 1, :])]
        for s, (is_a, kb) in enumerate(mids, start=1):
            selrow = jnp.sum(jnp.where(blk_i == kb, sel, 0.0), axis=0, keepdims=True)
            belongs = jnp.where(is_a, 1.0, 0.0) if mine else jnp.where(is_a, 0.0, 1.0)
            used = selrow * belongs > 0.5
            shift = slope_row * ((kb - qblock) * blk).astype(F32)
            pieces.append((s, jnp.where(used, mloc_ref[s:s + 1, :] + shift, neg)))
        m_all = pieces[0][1]
        for _, m_s in pieces[1:]:
            m_all = jnp.maximum(m_all, m_s)
        acc = [jnp.zeros((PV_ROWS, blk), F32), jnp.zeros((PV_ROWS, blk), F32)]
        for s, m_s in pieces:
            w = jnp.exp(m_s - m_all)
            for hh in range(2):
                acc[hh] = acc[hh] + pv_ref[s, hh] * w[:, hh * blk:(hh + 1) * blk]
        o_t = jnp.concatenate([a[0:HEAD_DIM, :] / a[HEAD_DIM:HEAD_DIM + 1, :] for a in acc], axis=0)
        o_ref[0] = o_t.T.astype(BF16)

    combine(oa_ref, 0, sel_a, qblock_a, True)
    combine(ob_ref, n_mid + 1, sel_b, qblock_b, False)


def _attn_call(slopes, q, k, v):
    half = N_BLOCKS // 2
    o_a, o_b = pl.pallas_call(
        _attn_kernel,
        grid_spec=pltpu.PrefetchScalarGridSpec(
            num_scalar_prefetch=1,
            grid=(BATCH, N_HEADS // 2, half),
            in_specs=[
                pl.BlockSpec((1, MOBA_BLOCK, 128), lambda b, h, j, sl: (b, j, h)),
                pl.BlockSpec((1, MOBA_BLOCK, 128), lambda b, h, j, sl: (b, N_BLOCKS - 1 - j, h)),
                pl.BlockSpec((1, SEQ, 128), lambda b, h, j, sl: (b, 0, h)),
                pl.BlockSpec((1, SEQ, 128), lambda b, h, j, sl: (b, 0, h)),
            ],
            out_specs=[pl.BlockSpec((1, MOBA_BLOCK, 128), lambda b, h, j, sl: (b, j, h)),
                       pl.BlockSpec((1, MOBA_BLOCK, 128), lambda b, h, j, sl: (b, j, h))],
            scratch_shapes=[
                pltpu.VMEM((N_BLOCKS, MOBA_BLOCK, 256), BF16),
                pltpu.VMEM((N_BLOCKS, 2, PV_ROWS, MOBA_BLOCK), BF16),
                pltpu.VMEM((N_BLOCKS, 128), F32),
                pltpu.VMEM((2, 256, 2 * MOBA_BLOCK), BF16),
                pltpu.VMEM((N_BLOCKS + 1, 2, PV_ROWS, MOBA_BLOCK), F32),
                pltpu.VMEM((16, 2 * MOBA_BLOCK), F32),
                pltpu.VMEM((N_BLOCKS + 1, MOBA_BLOCK, 2 * MOBA_BLOCK), F32),
                pltpu.VMEM((N_BLOCKS + 1, MOBA_BLOCK, 2 * MOBA_BLOCK), BF16),
            ],
        ),
        out_shape=[jax.ShapeDtypeStruct((BATCH, SEQ // 2, ATTN_WIDTH), BF16)] * 2,
        compiler_params=pltpu.CompilerParams(
            dimension_semantics=("arbitrary", "arbitrary", "arbitrary"), vmem_limit_bytes=VMEM_LIMIT),
        name="moba_attn",
    )(slopes, q, q, k, v)
    return o_a, o_b


def _merge_kernel(oa_ref, ob_ref, za_ref, sgb_ref, x_ref, woa_f32_ref, wo_f32_ref, g_ref, b_ref,
                  wr_hi_ref, wr_lo_ref, x1_ref, xs_ref, rf_ref, mt_ref, woa_ref, wo_ref):
    i = pl.program_id(0)

    @pl.when(i == 0)
    def _():
        woa_ref[...] = woa_f32_ref[...].astype(BF16)
        wo_ref[...] = wo_f32_ref[...].astype(BF16)

    o = jnp.where(lax.rem(i, N_BLOCKS) < N_BLOCKS // 2, oa_ref[0], ob_ref[0])
    y_attn = jnp.dot(o, woa_ref[...], preferred_element_type=F32)
    y = za_ref[...].astype(F32) + sgb_ref[...].astype(F32) * y_attn
    mix = jnp.dot(y.astype(BF16), wo_ref[...], preferred_element_type=F32)
    h = ALPHA * x_ref[...] + mix
    mu = jnp.mean(h, axis=-1, keepdims=True)
    hc = h - mu
    var = jnp.mean(hc * hc, axis=-1, keepdims=True)
    x1 = hc * lax.rsqrt(var + LN_EPS) * g_ref[...] + b_ref[...]
    x1_ref[...] = x1

    xh = x1.astype(BF16)
    xl = (x1 - xh.astype(F32)).astype(BF16)
    wh = wr_hi_ref[...]
    logits = (jnp.dot(xh, wh, preferred_element_type=F32)
              + jnp.dot(xl, wh, preferred_element_type=F32)
              + jnp.dot(xh, wr_lo_ref[...], preferred_element_type=F32)).T

    row8 = lax.broadcasted_iota(I32, (8, TM), 0).astype(F32)
    gl = jnp.where(row8 < float(N_GROUPS), logits[0:8, :], NEG_INF)
    gexp = jnp.exp(gl - jnp.max(gl, axis=0, keepdims=True))
    gprob = gexp / jnp.sum(gexp, axis=0, keepdims=True)
    ptop = jnp.max(gprob, axis=0, keepdims=True)
    gtop = jnp.min(jnp.where(gprob == ptop, row8, 8.0), axis=0, keepdims=True)
    el = logits[8:ROUTER_ROWS, :]
    eg = jnp.where(gtop == 0.0, el[0:8, :],
                   jnp.where(gtop == 1.0, el[8:16, :], jnp.where(gtop == 2.0, el[16:24, :], el[24:32, :])))
    m1 = jnp.max(eg, axis=0, keepdims=True)
    i1 = jnp.min(jnp.where(eg == m1, row8, 8.0), axis=0, keepdims=True)
    eg2 = jnp.where(row8 == i1, NEG_INF, eg)
    m2 = jnp.max(eg2, axis=0, keepdims=True)
    i2 = jnp.min(jnp.where(eg2 == m2, row8, 8.0), axis=0, keepdims=True)
    t2 = jnp.exp(m2 - m1)
    gate1 = ptop * (1.0 / (1.0 + t2))
    gate2 = ptop * (t2 / (1.0 + t2))
    e1 = gtop * float(EXPERTS_PER_GROUP) + i1
    e2 = gtop * float(EXPERTS_PER_GROUP) + i2

    erow = lax.broadcasted_iota(I32, (N_EXPERTS, TM), 0).astype(F32)
    oh1 = jnp.where(erow == e1, 1.0, 0.0)
    oh2 = jnp.where(erow == e2, 1.0, 0.0)
    oh = oh1 + oh2
    ta = lax.broadcasted_iota(I32, (TM, TM), 0)
    tb = lax.broadcasted_iota(I32, (TM, TM), 1)
    upper = jnp.where(ta < tb, 1.0, 0.0).astype(BF16)
    cum = jnp.dot(oh.astype(BF16), upper, preferred_element_type=F32)

    n_e = jnp.sum(oh, axis=1, keepdims=True)
    m_e = jnp.floor((n_e + float(GRAN - 1)) * (1.0 / GRAN))
    m_rep = jnp.broadcast_to(m_e, (N_EXPERTS, 128))
    ea = lax.broadcasted_iota(I32, (N_EXPERTS, N_EXPERTS), 0)
    eb = lax.broadcasted_iota(I32, (N_EXPERTS, N_EXPERTS), 1)
    lower = jnp.where(eb < ea, 1.0, 0.0).astype(BF16)
    run_start = jnp.dot(lower, m_rep.astype(BF16), preferred_element_type=F32)
    tot = cum + float(GRAN) * run_start[:, 0:1]
    lp1 = jnp.sum(oh1 * tot, axis=0, keepdims=True)
    lp2 = jnp.sum(oh2 * tot, axis=0, keepdims=True)
    lrow = lax.broadcasted_iota(I32, (LOCAL_ROWS, TM), 0).astype(F32)
    perm = jnp.where((lrow == lp1) | (lrow == lp2), 1.0, 0.0).astype(BF16)
    xs_ref[...] = jnp.dot(perm, xh, preferred_element_type=F32).astype(BF16)

    zero = jnp.zeros((1, TM), F32)
    rf_ref[...] = jnp.concatenate([gate1, gate2, lp1, lp2, zero, zero, zero, zero], axis=0)
    mt_ref[...] = m_rep


def _merge_call(o_a, o_b, za, sgb, x, woa, wo, g, b, wr_hi, wr_lo):
    tok = lambda c: pl.BlockSpec((TM, c), lambda i: (i, 0))
    full = lambda shape: pl.BlockSpec(shape, lambda i: (0,) * len(shape))
    lanes = pl.BlockSpec((8, TM), lambda i: (0, i))
    half = N_BLOCKS // 2
    o_a_spec = pl.BlockSpec((1, TM, ATTN_WIDTH), lambda i: (i // N_BLOCKS, jnp.minimum(i % N_BLOCKS, half - 1), 0))
    o_b_spec = pl.BlockSpec(
        (1, TM, ATTN_WIDTH), lambda i: (i // N_BLOCKS, jnp.minimum(N_BLOCKS - 1 - i % N_BLOCKS, half - 1), 0))
    return pl.pallas_call(
        _merge_kernel,
        grid=(TOKENS // TM,),
        in_specs=[o_a_spec, o_b_spec, tok(D_MODEL), tok(D_MODEL), tok(D_MODEL),
                  full((ATTN_WIDTH, D_MODEL)), full((D_MODEL, D_MODEL)), full((1, D_MODEL)),
                  full((1, D_MODEL)), full((D_MODEL, 128)), full((D_MODEL, 128))],
        out_specs=[tok(D_MODEL), pl.BlockSpec((LOCAL_ROWS, D_MODEL), lambda i: (i, 0)), lanes,
                   pl.BlockSpec((N_EXPERTS, 128), lambda i: (i, 0))],
        out_shape=[jax.ShapeDtypeStruct((TOKENS, D_MODEL), F32),
                   jax.ShapeDtypeStruct((N_TOK_TILES * LOCAL_ROWS, D_MODEL), BF16),
                   jax.ShapeDtypeStruct((8, TOKENS), F32),
                   jax.ShapeDtypeStruct((N_TOK_TILES * N_EXPERTS, 128), F32)],
        scratch_shapes=[pltpu.VMEM((ATTN_WIDTH, D_MODEL), BF16), pltpu.VMEM((D_MODEL, D_MODEL), BF16)],
        compiler_params=pltpu.CompilerParams(
            dimension_semantics=("arbitrary",), vmem_limit_bytes=VMEM_LIMIT),
        name="merge_ln1_route",
    )(o_a, o_b, za, sgb, x, woa, wo, g, b, wr_hi, wr_lo)


def _granule_copy(src_ref, src_gran, dst_ref, dst_gran, sem):
    src = pl.multiple_of(src_gran * GRAN, GRAN)
    dst = pl.multiple_of(dst_gran * GRAN, GRAN)
    return pltpu.make_async_copy(src_ref.at[pl.ds(src, GRAN), :], dst_ref.at[pl.ds(dst, GRAN), :], sem)


def _expert_kernel(te_ref, nt_ref, gsrc_ref, gdst_ref, ug_ref, xs_ref, wg_ref, wu_ref, wd_ref, ys_ref,
                   xbuf, ybuf, zbuf, in_sem, out_sem, zero_sem):
    j = pl.program_id(0)
    n_tiles = nt_ref[0]
    slot = lax.rem(j, NBUF)

    def tile_gather(tile, s):
        for g in range(TILE_GRANS):
            _granule_copy(xs_ref, gsrc_ref[tile * TILE_GRANS + g], xbuf.at[s], g,
                          in_sem.at[s]).start(priority=g % 2)

    @pl.when(j == 0)
    def _():
        tile_gather(0, 0)
        ybuf[NBUF - 1] = jnp.zeros((TE, D_MODEL), BF16)
        zbuf[...] = jnp.zeros((GRAN, D_MODEL), BF16)
        for part in range(NBUF):
            spare = pltpu.make_async_copy(
                ybuf.at[NBUF - 1], ys_ref.at[pl.ds((SPARE_GRAN + part * TILE_GRANS) * GRAN, TE), :],
                out_sem.at[NBUF - 1])
            spare.start()
            spare.wait()

    for ahead in range(1, NBUF - 1):
        @pl.when(jnp.logical_and(j == 0, ahead < n_tiles))
        def _():
            tile_gather(ahead, ahead)

    def zero_copy(t, g):
        return _granule_copy(zbuf, 0, ys_ref, t * LOCAL_GRANS + g, zero_sem)

    @pl.when(jnp.logical_and(j >= 1, j <= N_TOK_TILES))
    def _():
        def wait(g, c):
            zero_copy(j - 1, g).wait()
            return c

        lax.fori_loop(ug_ref[j - 1], LOCAL_GRANS, wait, 0)

    @pl.when(j < N_TOK_TILES)
    def _():
        def start(g, c):
            zero_copy(j, g).start()
            return c

        lax.fori_loop(ug_ref[j], LOCAL_GRANS, start, 0)

    @pl.when(j + NBUF - 1 < n_tiles)
    def _():
        tile_gather(j + NBUF - 1, lax.rem(j + NBUF - 1, NBUF))

    @pl.when(jnp.logical_and(j >= NBUF, j - NBUF < n_tiles))
    def _():
        pltpu.make_async_copy(ybuf.at[slot], ys_ref.at[pl.ds(0, TE), :], out_sem.at[slot]).wait()

    @pl.when(j < n_tiles)
    def _():
        pltpu.make_async_copy(xs_ref.at[pl.ds(0, TE), :], xbuf.at[slot], in_sem.at[slot]).wait()
        xb = xbuf[slot]
        hg = jnp.dot(xb, wg_ref[0].astype(BF16), preferred_element_type=F32)
        hu = jnp.dot(xb, wu_ref[0].astype(BF16), preferred_element_type=F32)
        h = (hg * _sigmoid(hg) * hu).astype(BF16)
        ybuf[slot] = jnp.dot(h, wd_ref[0].astype(BF16), preferred_element_type=F32).astype(BF16)
        for g in range(TILE_GRANS):
            _granule_copy(ybuf.at[slot], g, ys_ref, gdst_ref[j * TILE_GRANS + g],
                          out_sem.at[slot]).start(priority=g % 2)


def _expert_call(tile_expert, n_tiles, gsrc, gdst, used_grans, xs, wg, wu, wd):
    wsel = lambda j, te, nt, gs, gd, ug: (te[j], 0, 0)
    return pl.pallas_call(
        _expert_kernel,
        grid_spec=pltpu.PrefetchScalarGridSpec(
            num_scalar_prefetch=5,
            grid=(MAX_TILES + NBUF,),
            in_specs=[pl.BlockSpec(memory_space=pl.ANY),
                      pl.BlockSpec((1, D_MODEL, D_EXPERT), wsel),
                      pl.BlockSpec((1, D_MODEL, D_EXPERT), wsel),
                      pl.BlockSpec((1, D_EXPERT, D_MODEL), wsel)],
            out_specs=pl.BlockSpec(memory_space=pl.ANY),
            scratch_shapes=[pltpu.VMEM((NBUF, TE, D_MODEL), BF16), pltpu.VMEM((NBUF, TE, D_MODEL), BF16),
                            pltpu.VMEM((GRAN, D_MODEL), BF16),
                            pltpu.SemaphoreType.DMA((NBUF,)), pltpu.SemaphoreType.DMA((NBUF,)),
                            pltpu.SemaphoreType.DMA],
        ),
        out_shape=jax.ShapeDtypeStruct(((SPARE_GRAN + NBUF * TILE_GRANS) * GRAN, D_MODEL), BF16),
        compiler_params=pltpu.CompilerParams(
            dimension_semantics=("arbitrary",), vmem_limit_bytes=VMEM_LIMIT),
        name="experts",
    )(tile_expert, n_tiles, gsrc, gdst, used_grans, xs, wg, wu, wd)


def _combine_kernel(ys_ref, x1_ref, rf_ref, g_ref, b_ref, out_ref):
    route = rf_ref[...].T
    col = lax.broadcasted_iota(I32, (TM, LOCAL_ROWS), 1).astype(F32)
    unsort = (jnp.where(col == route[:, 2:3], route[:, 0:1], 0.0)
              + jnp.where(col == route[:, 3:4], route[:, 1:2], 0.0)).astype(BF16)
    ffn = jnp.dot(unsort, ys_ref[...], preferred_element_type=F32)
    h = ALPHA * x1_ref[...] + ffn
    mu = jnp.mean(h, axis=-1, keepdims=True)
    hc = h - mu
    var = jnp.mean(hc * hc, axis=-1, keepdims=True)
    out_ref[...] = hc * lax.rsqrt(var + LN_EPS) * g_ref[...] + b_ref[...]


def _combine_call(ys, x1, rf, g, b):
    return pl.pallas_call(
        _combine_kernel,
        grid=(N_TOK_TILES,),
        in_specs=[pl.BlockSpec((LOCAL_ROWS, D_MODEL), lambda i: (i, 0)),
                  pl.BlockSpec((TM, D_MODEL), lambda i: (i, 0)),
                  pl.BlockSpec((8, TM), lambda i: (0, i)),
                  pl.BlockSpec((1, D_MODEL), lambda i: (0, 0)),
                  pl.BlockSpec((1, D_MODEL), lambda i: (0, 0))],
        out_specs=pl.BlockSpec((TM, D_MODEL), lambda i: (i, 0)),
        out_shape=jax.ShapeDtypeStruct((TOKENS, D_MODEL), F32),
        compiler_params=pltpu.CompilerParams(
            dimension_semantics=("arbitrary",), vmem_limit_bytes=VMEM_LIMIT),
        name="combine_ln2",
    )(ys, x1, rf, g, b)


def _router_cols(w_router_group, w_router_expert):
    w = jnp.concatenate([w_router_group, jnp.zeros((D_MODEL, 4), F32), w_router_expert,
                         jnp.zeros((D_MODEL, 128 - ROUTER_ROWS), F32)], axis=1)
    hi = w.astype(BF16)
    lo = (w - hi.astype(F32)).astype(BF16)
    return hi, lo


def _layer(x, w_in, conv_w, w_out_conv, w_out_attn, w_o, ln1_g, ln1_b,
           w_router_group, w_router_expert, w_gate, w_up, w_down, ln2_g, ln2_b):
    slopes = jnp.asarray([2.0 ** (-8.0 * (h + 1) / N_HEADS) for h in range(N_HEADS)], F32)
    q, k, v, za, sgb = _proj_call(x, w_in, conv_w, w_out_conv)
    o_a, o_b = _attn_call(slopes, q, k, v)

    wr_hi, wr_lo = _router_cols(w_router_group, w_router_expert)
    x1, xs, rf, mt = _merge_call(
        o_a, o_b, za.reshape(TOKENS, D_MODEL), sgb.reshape(TOKENS, D_MODEL),
        x.reshape(TOKENS, D_MODEL), w_out_attn, w_o,
        ln1_g.reshape(1, D_MODEL), ln1_b.reshape(1, D_MODEL), wr_hi, wr_lo)

    grans = mt.reshape(N_TOK_TILES, N_EXPERTS, 128)[:, :, 0].astype(I32)
    local_start = jnp.cumsum(grans, axis=1) - grans
    grans_t = grans.T
    tiles_e = (jnp.sum(grans_t, axis=1) + TILE_GRANS - 1) // TILE_GRANS
    tile_end = jnp.cumsum(tiles_e)
    n_tiles = tile_end[-1].reshape(1)
    tile_ids = jnp.arange(MAX_TILES + NBUF, dtype=I32)
    tile_expert = jnp.minimum(
        jnp.sum((tile_ids[:, None] >= tile_end[None, :]).astype(I32), axis=1), N_EXPERTS - 1)
    run_slot = TILE_GRANS * (tile_end - tiles_e)[:, None] + jnp.cumsum(grans_t, axis=1) - grans_t
    run_src = jnp.arange(N_TOK_TILES, dtype=I32)[None, :] * LOCAL_GRANS + local_start.T
    pick = (tile_expert[:MAX_TILES, None] == jnp.arange(N_EXPERTS, dtype=I32)[None, :])[:, :, None]
    t_slot = jnp.sum(jnp.where(pick, run_slot[None], 0), axis=1)
    t_len = jnp.sum(jnp.where(pick, grans_t[None], 0), axis=1)
    t_src = jnp.sum(jnp.where(pick, run_src[None], 0), axis=1)
    slots = jnp.arange(MAX_TILES * TILE_GRANS, dtype=I32).reshape(MAX_TILES, TILE_GRANS)
    k = slots[:, :, None] - t_slot[:, None, :]
    hit = (k >= 0) & (k < t_len[:, None, :])
    gran = jnp.sum(jnp.where(hit, t_src[:, None, :] + k, 0), axis=2).reshape(-1)
    filled = (jnp.sum(hit.astype(I32), axis=2) > 0).reshape(-1)
    slots = slots.reshape(-1)
    gsrc = jnp.where(filled, gran, 0)
    gdst = jnp.where(filled, gran, SPARE_GRAN + slots % (NBUF * TILE_GRANS))

    ys = _expert_call(tile_expert, n_tiles, gsrc, gdst, jnp.sum(grans, axis=1), xs, w_gate, w_up, w_down)
    out = _combine_call(ys, x1, rf, ln2_g.reshape(1, D_MODEL), ln2_b.reshape(1, D_MODEL))
    return out.reshape(BATCH, SEQ, D_MODEL)


def kernel(x, w_in, conv_w, w_out_conv, w_out_attn, w_o, ln1_g, ln1_b, w_router_group, w_router_expert, w_gate, w_up, w_down, ln2_g, ln2_b):
    depth = w_in.shape[0]
    for l in range(depth):
        x = _layer(x, w_in[l], conv_w[l], w_out_conv[l], w_out_attn[l], w_o[l], ln1_g[l], ln1_b[l],
                   w_router_group[l], w_router_expert[l], w_gate[l], w_up[l], w_down[l], ln2_g[l], ln2_b[l])
    return x
```

```python
import functools

import jax
import jax.numpy as jnp
from jax import lax
from jax.experimental import pallas as pl
from jax.experimental.pallas import tpu as pltpu

F32 = jnp.float32
BF16 = jnp.bfloat16
U32 = jnp.uint32
I32 = jnp.int32

D_MODEL = 1024
BATCH = 8
SEQ = 2048
TOKENS = BATCH * SEQ
CONV_WIDTH = 512
N_HEADS = 8
HEAD_DIM = 64
ATTN_WIDTH = N_HEADS * HEAD_DIM
MOBA_BLOCK = 256
N_BLOCKS = SEQ // MOBA_BLOCK
MOBA_TOPK = 3
N_GROUPS = 4
EXPERTS_PER_GROUP = 8
N_EXPERTS = N_GROUPS * EXPERTS_PER_GROUP
D_EXPERT = 256
LN_EPS = 1e-5
ALPHA = 2.0 ** 0.25
IN_COLS = 3 * CONV_WIDTH + 3 * ATTN_WIDTH + 2 * D_MODEL
HALF = D_MODEL // 2

TM = 256
TE = 256
PV_ROWS = HEAD_DIM + 16
GRAN = 16
TILE_GRANS = TE // GRAN
N_TOK_TILES = TOKENS // TM
LOCAL_ROWS = -(-(2 * TM + N_EXPERTS * (GRAN - 1)) // 256) * 256
LOCAL_GRANS = LOCAL_ROWS // GRAN
SPARE_GRAN = N_TOK_TILES * LOCAL_GRANS
MAX_TILES = (2 * TOKENS + N_TOK_TILES * N_EXPERTS * (GRAN - 1)) // TE + N_EXPERTS
MERGE_SUB = 2
NBUF = 4
ROUTER_ROWS = 40
VMEM_LIMIT = 56 * 1024 * 1024
NEG_INF = float("-inf")


def _sigmoid(z):
    return 1.0 / (1.0 + jnp.exp(-z))


def _proj_kernel(x_ref, w_in_f32_ref, convw_ref, woc_f32_ref, q_ref, k_ref, v_ref, za_ref, sgb_ref,
                 ubuf, w_in_ref, woc_ref):
    s = pl.program_id(1)

    @pl.when((pl.program_id(0) == 0) & (s == 0))
    def _():
        for c in range(0, IN_COLS, CONV_WIDTH):
            w_in_ref[:, c:c + CONV_WIDTH] = w_in_f32_ref[:, c:c + CONV_WIDTH].astype(BF16)
        woc_ref[...] = woc_f32_ref[...].astype(BF16)

    xb = x_ref[0].astype(BF16)

    def proj(c0, c1):
        return jnp.dot(xb, w_in_ref[:, c0:c1], preferred_element_type=F32)

    c_b = proj(0, CONV_WIDTH)
    u = proj(CONV_WIDTH, 2 * CONV_WIDTH) * proj(2 * CONV_WIDTH, 3 * CONV_WIDTH)

    @pl.when(s == 0)
    def _():
        ubuf[0:8, :] = jnp.zeros((8, CONV_WIDTH), F32)

    ubuf[8:8 + TM, :] = u
    w = convw_ref[...]
    conv = w[2:3, :] * u + w[1:2, :] * ubuf[7:7 + TM, :] + w[0:1, :] * ubuf[6:6 + TM, :]
    ubuf[0:8, :] = u[TM - 8:TM, :]
    hc = (c_b * conv).astype(BF16)
    y_conv = jnp.dot(hc, woc_ref[...], preferred_element_type=F32)

    o = 3 * CONV_WIDTH
    q_ref[0] = (proj(o, o + ATTN_WIDTH) * (HEAD_DIM ** -0.5)).astype(BF16)
    k_ref[0] = proj(o + ATTN_WIDTH, o + 2 * ATTN_WIDTH).astype(BF16)
    v_ref[0] = proj(o + 2 * ATTN_WIDTH, o + 3 * ATTN_WIDTH).astype(BF16)
    o += 3 * ATTN_WIDTH
    za_ref[0] = (_sigmoid(proj(o, o + D_MODEL)) * y_conv).astype(BF16)
    sgb_ref[0] = _sigmoid(proj(o + D_MODEL, o + 2 * D_MODEL)).astype(BF16)


def _proj_call(x, w_in, conv_w, w_out_conv):
    tok_spec = lambda c: pl.BlockSpec((1, TM, c), lambda b, s: (b, s, 0))
    full = lambda shape: pl.BlockSpec(shape, lambda b, s: (0,) * len(shape))
    once = lambda shape: pl.BlockSpec(shape, lambda b, s: (0,) * len(shape), pipeline_mode=pl.Buffered(1))
    return pl.pallas_call(
        _proj_kernel,
        grid=(BATCH, SEQ // TM),
        in_specs=[tok_spec(D_MODEL), once((D_MODEL, IN_COLS)), full((3, CONV_WIDTH)),
                  once((CONV_WIDTH, D_MODEL))],
        out_specs=[tok_spec(ATTN_WIDTH), tok_spec(ATTN_WIDTH), tok_spec(ATTN_WIDTH),
                   tok_spec(D_MODEL), tok_spec(D_MODEL)],
        out_shape=[jax.ShapeDtypeStruct((BATCH, SEQ, ATTN_WIDTH), BF16)] * 3
        + [jax.ShapeDtypeStruct((BATCH, SEQ, D_MODEL), BF16)] * 2,
        scratch_shapes=[pltpu.VMEM((TM + 8, CONV_WIDTH), F32), pltpu.VMEM((D_MODEL, IN_COLS), BF16),
                        pltpu.VMEM((CONV_WIDTH, D_MODEL), BF16)],
        compiler_params=pltpu.CompilerParams(
            dimension_semantics=("arbitrary", "arbitrary"), vmem_limit_bytes=VMEM_LIMIT),
        name="proj",
    )(x, w_in, conv_w, w_out_conv)


def _attn_kernel(slopes_ref, qa_ref, qb_ref, k_ref, v_ref, oa_ref, ob_ref,
                 kaug_ref, vt_ref, kmean_ref, qaug_ref, pv_ref, mloc_ref, t_ref, p_ref):
    hp = pl.program_id(1)
    j = pl.program_id(2)
    blk = MOBA_BLOCK

    @pl.when(j == 0)
    def _():
        klane = lax.broadcasted_iota(I32, (blk, 128), 1)
        koff = lax.broadcasted_iota(I32, (blk, 128), 0).astype(F32)
        k_extra = jnp.where(klane == 0, koff, jnp.where(klane == 1, 1.0, 0.0)).astype(BF16)
        orow = lax.broadcasted_iota(I32, (PV_ROWS - HEAD_DIM, blk), 0)
        ones_rows = jnp.where(orow == 0, 1.0, 0.0).astype(BF16)
        for n in range(N_BLOCKS):
            kblk = k_ref[0, n * blk:(n + 1) * blk, :]
            kaug_ref[n, :, 0:128] = kblk
            kaug_ref[n, :, 128:256] = k_extra
            kmean_ref[n:n + 1, :] = jnp.mean(kblk.astype(F32), axis=0, keepdims=True)
            v_t = v_ref[0, n * blk:(n + 1) * blk, :].astype(F32).T.astype(BF16)
            for hh in range(2):
                vt_ref[n, hh, 0:HEAD_DIM, :] = v_t[hh * HEAD_DIM:(hh + 1) * HEAD_DIM, :]
                vt_ref[n, hh, HEAD_DIM:PV_ROWS, :] = ones_rows

    lane = lax.broadcasted_iota(I32, (1, 2 * blk), 1)
    slope_row = jnp.where(lane < blk, slopes_ref[2 * hp], slopes_ref[2 * hp + 1])
    qoff_row = jnp.where(lane < blk, lane, lane - blk).astype(F32)
    feat = lax.broadcasted_iota(I32, (2 * HEAD_DIM, blk), 0)
    arow = lax.broadcasted_iota(I32, (2 * HEAD_DIM, 2 * blk), 0)
    q_extra = jnp.where(arow == 0, slope_row, jnp.where(arow == 1, -slope_row * qoff_row, 0.0)).astype(BF16)
    blk_i = lax.broadcasted_iota(I32, (N_BLOCKS, 2 * blk), 0)
    kmean = kmean_ref[...].astype(BF16)
    key_i = lax.broadcasted_iota(I32, (blk, 2 * blk), 0)
    qry_j = lax.broadcasted_iota(I32, (blk, 2 * blk), 1)
    causal = key_i <= jnp.where(qry_j < blk, qry_j, qry_j - blk)

    def prepare(q_ref, slot, qblock):
        q_t = q_ref[0].astype(F32).T
        qcat = jnp.concatenate([jnp.where(feat < HEAD_DIM, q_t, 0.0), jnp.where(feat >= HEAD_DIM, q_t, 0.0)],
                               axis=1).astype(BF16)
        qaug_ref[slot, 0:2 * HEAD_DIM, :] = qcat
        qaug_ref[slot, 2 * HEAD_DIM:4 * HEAD_DIM, :] = q_extra
        gate = jnp.dot(kmean, qcat, preferred_element_type=F32)
        cnt = jnp.zeros((N_BLOCKS, 2 * blk), F32)
        for m in range(N_BLOCKS):
            gm = gate[m:m + 1, :]
            beats = (gm > gate) | ((gm == gate) & (blk_i > m))
            cnt = cnt + jnp.where(beats & (qblock > m), 1.0, 0.0)
        return jnp.where((blk_i < qblock) & (cnt < float(MOBA_TOPK)), 1.0, 0.0)

    qblock_a = j
    qblock_b = N_BLOCKS - 1 - j
    sel_a = prepare(qa_ref, 0, qblock_a)
    sel_b = prepare(qb_ref, 1, qblock_b)

    n_mid = N_BLOCKS - 1
    slots = [(0, 0, qblock_a, True)]
    mids = []
    for s in range(1, n_mid + 1):
        is_a = s <= j
        slots.append((s, jnp.where(is_a, 0, 1), jnp.where(is_a, s - 1, s - 1 - j), False))
        mids.append((is_a, slots[-1][2]))
    slots.append((n_mid + 1, 1, qblock_b, True))

    for s, which, kb, _ in slots:
        t_ref[s] = jnp.dot(kaug_ref[kb], qaug_ref[which], preferred_element_type=F32)
    for s, _, _, own in slots:
        t = t_ref[s]
        if own:
            t = jnp.where(causal, t, NEG_INF)
        m_loc = jnp.max(t, axis=0, keepdims=True)
        p_ref[s] = jnp.exp((t - m_loc).astype(BF16))
        mloc_ref[s:s + 1, :] = m_loc
    for s, _, kb, _ in slots:
        pv_ref[s, 0] = jnp.dot(vt_ref[kb, 0], p_ref[s, :, 0:blk], preferred_element_type=F32)
        pv_ref[s, 1] = jnp.dot(vt_ref[kb, 1], p_ref[s, :, blk:2 * blk], preferred_element_type=F32)

    def combine(o_ref, own_slot, sel, qblock, mine):
        neg = jnp.full((1, 2 * blk), -1e30, F32)
        pieces = [(own_slot, mloc_ref[own_slot:own_slot + 1, :])]
        for s, (is_a, kb) in enumerate(mids, start=1):
            selrow = jnp.sum(jnp.where(blk_i == kb, sel, 0.0), axis=0, keepdims=True)
            belongs = jnp.where(is_a, 1.0, 0.0) if mine else jnp.where(is_a, 0.0, 1.0)
            used = selrow * belongs > 0.5
            shift = slope_row * ((kb - qblock) * blk).astype(F32)
            pieces.append((s, jnp.where(used, mloc_ref[s:s + 1, :] + shift, neg)))
        m_all = pieces[0][1]
        for _, m_s in pieces[1:]:
            m_all = jnp.maximum(m_all, m_s)
        acc = [jnp.zeros((PV_ROWS, blk), F32), jnp.zeros((PV_ROWS, blk), F32)]
        for s, m_s in pieces:
            w = jnp.exp(m_s - m_all)
            for hh in range(2):
                acc[hh] = acc[hh] + pv_ref[s, hh] * w[:, hh * blk:(hh + 1) * blk]
        o_t = jnp.concatenate([a[0:HEAD_DIM, :] / a[HEAD_DIM:HEAD_DIM + 1, :] for a in acc], axis=0)
        o_ref[0] = o_t.T.astype(BF16)

    combine(oa_ref, 0, sel_a, qblock_a, True)
    combine(ob_ref, n_mid + 1, sel_b, qblock_b, False)


def _attn_call(slopes, q, k, v):
    half = N_BLOCKS // 2
    o_a, o_b = pl.pallas_call(
        _attn_kernel,
        grid_spec=pltpu.PrefetchScalarGridSpec(
            num_scalar_prefetch=1,
            grid=(BATCH, N_HEADS // 2, half),
            in_specs=[
                pl.BlockSpec((1, MOBA_BLOCK, 128), lambda b, h, j, sl: (b, j, h)),
                pl.BlockSpec((1, MOBA_BLOCK, 128), lambda b, h, j, sl: (b, N_BLOCKS - 1 - j, h)),
                pl.BlockSpec((1, SEQ, 128), lambda b, h, j, sl: (b, 0, h)),
                pl.BlockSpec((1, SEQ, 128), lambda b, h, j, sl: (b, 0, h)),
            ],
            out_specs=[pl.BlockSpec((1, MOBA_BLOCK, 128), lambda b, h, j, sl: (b, j, h)),
                       pl.BlockSpec((1, MOBA_BLOCK, 128), lambda b, h, j, sl: (b, half - 1 - j, h))],
            scratch_shapes=[
                pltpu.VMEM((N_BLOCKS, MOBA_BLOCK, 256), BF16),
                pltpu.VMEM((N_BLOCKS, 2, PV_ROWS, MOBA_BLOCK), BF16),
                pltpu.VMEM((N_BLOCKS, 128), F32),
                pltpu.VMEM((2, 256, 2 * MOBA_BLOCK), BF16),
                pltpu.VMEM((N_BLOCKS + 1, 2, PV_ROWS, MOBA_BLOCK), F32),
                pltpu.VMEM((16, 2 * MOBA_BLOCK), F32),
                pltpu.VMEM((N_BLOCKS + 1, MOBA_BLOCK, 2 * MOBA_BLOCK), F32),
                pltpu.VMEM((N_BLOCKS + 1, MOBA_BLOCK, 2 * MOBA_BLOCK), BF16),
            ],
        ),
        out_shape=[jax.ShapeDtypeStruct((BATCH, SEQ // 2, ATTN_WIDTH), BF16)] * 2,
        compiler_params=pltpu.CompilerParams(
            dimension_semantics=("arbitrary", "arbitrary", "arbitrary"), vmem_limit_bytes=VMEM_LIMIT),
        name="moba_attn",
    )(slopes, q, q, k, v)
    return o_a, o_b


def _route(logits):
    row8 = lax.broadcasted_iota(I32, (8, TM), 0).astype(F32)
    gl = jnp.where(row8 < float(N_GROUPS), logits[0:8, :], NEG_INF)
    gexp = jnp.exp(gl - jnp.max(gl, axis=0, keepdims=True))
    gprob = gexp / jnp.sum(gexp, axis=0, keepdims=True)
    ptop = jnp.max(gprob, axis=0, keepdims=True)
    gtop = jnp.min(jnp.where(gprob == ptop, row8, 8.0), axis=0, keepdims=True)
    el = logits[8:ROUTER_ROWS, :]
    eg = jnp.where(gtop == 0.0, el[0:8, :],
                   jnp.where(gtop == 1.0, el[8:16, :], jnp.where(gtop == 2.0, el[16:24, :], el[24:32, :])))
    m1 = jnp.max(eg, axis=0, keepdims=True)
    i1 = jnp.min(jnp.where(eg == m1, row8, 8.0), axis=0, keepdims=True)
    eg2 = jnp.where(row8 == i1, NEG_INF, eg)
    m2 = jnp.max(eg2, axis=0, keepdims=True)
    i2 = jnp.min(jnp.where(eg2 == m2, row8, 8.0), axis=0, keepdims=True)
    t2 = jnp.exp(m2 - m1)
    gate1 = ptop * (1.0 / (1.0 + t2))
    gate2 = ptop * (t2 / (1.0 + t2))
    erow = lax.broadcasted_iota(I32, (N_EXPERTS, TM), 0).astype(F32)
    oh1 = jnp.where(erow == gtop * float(EXPERTS_PER_GROUP) + i1, 1.0, 0.0)
    oh2 = jnp.where(erow == gtop * float(EXPERTS_PER_GROUP) + i2, 1.0, 0.0)
    return gate1, gate2, oh1, oh2


def _merge_kernel(oa_ref, ob_ref, za_ref, sgb_ref, x_ref, woa_f32_ref, wo_f32_ref, g_ref, b_ref,
                  wr_hi_ref, wr_lo_ref, x1_ref, xs_ref, rf_ref, mt_ref, woa_ref, wo_ref):
    i = pl.program_id(0)

    @pl.when(i == 0)
    def _():
        woa_ref[...] = woa_f32_ref[...].astype(BF16)
        wo_ref[...] = wo_f32_ref[...].astype(BF16)

    subs = range(MERGE_SUB)
    rows = [slice(s * TM, (s + 1) * TM) for s in subs]
    steps_per_batch = N_BLOCKS // MERGE_SUB
    in_oa = lax.rem(i, steps_per_batch) < steps_per_batch // 2
    o = [jnp.where(in_oa, oa_ref[0, r, :], ob_ref[0, r, :]) for r in rows]
    y_attn = [jnp.dot(o[s], woa_ref[...], preferred_element_type=F32) for s in subs]
    y = [(za_ref[rows[s], :].astype(F32) + sgb_ref[rows[s], :].astype(F32) * y_attn[s]).astype(BF16)
         for s in subs]
    mix = [jnp.dot(y[s], wo_ref[...], preferred_element_type=F32) for s in subs]
    x1 = []
    for s in subs:
        h = ALPHA * x_ref[rows[s], :] + mix[s]
        mu = jnp.mean(h, axis=-1, keepdims=True)
        hc = h - mu
        var = jnp.mean(hc * hc, axis=-1, keepdims=True)
        x1.append(hc * lax.rsqrt(var + LN_EPS) * g_ref[...] + b_ref[...])
        x1_ref[rows[s], :] = x1[s]

    xh = [x1[s].astype(BF16) for s in subs]
    xl = [(x1[s] - xh[s].astype(F32)).astype(BF16) for s in subs]
    wh = wr_hi_ref[...]
    logits = [(jnp.dot(xh[s], wh, preferred_element_type=F32)
               + jnp.dot(xl[s], wh, preferred_element_type=F32)
               + jnp.dot(xh[s], wr_lo_ref[...], preferred_element_type=F32)).T for s in subs]
    routes = [_route(logits[s]) for s in subs]

    ta = lax.broadcasted_iota(I32, (TM, TM), 0)
    tb = lax.broadcasted_iota(I32, (TM, TM), 1)
    upper = jnp.where(ta < tb, 1.0, 0.0).astype(BF16)
    ea = lax.broadcasted_iota(I32, (N_EXPERTS, N_EXPERTS), 0)
    eb = lax.broadcasted_iota(I32, (N_EXPERTS, N_EXPERTS), 1)
    lower = jnp.where(eb < ea, 1.0, 0.0).astype(BF16)
    lrow = lax.broadcasted_iota(I32, (LOCAL_ROWS, TM), 0).astype(F32)
    zero = jnp.zeros((1, TM), F32)
    cum = [jnp.dot((routes[s][2] + routes[s][3]).astype(BF16), upper, preferred_element_type=F32) for s in subs]
    perm = []
    for s in subs:
        gate1, gate2, oh1, oh2 = routes[s]
        n_e = jnp.sum(oh1 + oh2, axis=1, keepdims=True)
        m_rep = jnp.broadcast_to(jnp.floor((n_e + float(GRAN - 1)) * (1.0 / GRAN)), (N_EXPERTS, 128))
        run_start = jnp.dot(lower, m_rep.astype(BF16), preferred_element_type=F32)
        tot = cum[s] + float(GRAN) * run_start[:, 0:1]
        lp1 = jnp.sum(oh1 * tot, axis=0, keepdims=True)
        lp2 = jnp.sum(oh2 * tot, axis=0, keepdims=True)
        perm.append(jnp.where((lrow == lp1) | (lrow == lp2), 1.0, 0.0).astype(BF16))
        rf_ref[:, rows[s]] = jnp.concatenate([gate1, gate2, lp1, lp2, zero, zero, zero, zero], axis=0)
        mt_ref[s * N_EXPERTS:(s + 1) * N_EXPERTS, :] = m_rep
    for s in subs:
        xs_ref[s * LOCAL_ROWS:(s + 1) * LOCAL_ROWS, :] = jnp.dot(
            perm[s], xh[s], preferred_element_type=F32).astype(BF16)


def _merge_call(o_a, o_b, za, sgb, x, woa, wo, g, b, wr_hi, wr_lo):
    tm = MERGE_SUB * TM
    tok = lambda c: pl.BlockSpec((tm, c), lambda i: (i, 0))
    full = lambda shape: pl.BlockSpec(shape, lambda i: (0,) * len(shape))
    per_batch = SEQ // tm
    half = per_batch // 2
    o_a_spec = pl.BlockSpec((1, tm, ATTN_WIDTH), lambda i: (i // per_batch, jnp.minimum(i % per_batch, half - 1), 0))
    o_b_spec = pl.BlockSpec((1, tm, ATTN_WIDTH), lambda i: (i // per_batch, jnp.maximum(i % per_batch - half, 0), 0))
    return pl.pallas_call(
        _merge_kernel,
        grid=(TOKENS // tm,),
        in_specs=[o_a_spec, o_b_spec, tok(D_MODEL), tok(D_MODEL), tok(D_MODEL),
                  full((ATTN_WIDTH, D_MODEL)), full((D_MODEL, D_MODEL)), full((1, D_MODEL)),
                  full((1, D_MODEL)), full((D_MODEL, 128)), full((D_MODEL, 128))],
        out_specs=[tok(D_MODEL), pl.BlockSpec((MERGE_SUB * LOCAL_ROWS, D_MODEL), lambda i: (i, 0)),
                   pl.BlockSpec((8, tm), lambda i: (0, i)),
                   pl.BlockSpec((MERGE_SUB * N_EXPERTS, 128), lambda i: (i, 0))],
        out_shape=[jax.ShapeDtypeStruct((TOKENS, D_MODEL), F32),
                   jax.ShapeDtypeStruct((N_TOK_TILES * LOCAL_ROWS, D_MODEL), BF16),
                   jax.ShapeDtypeStruct((8, TOKENS), F32),
                   jax.ShapeDtypeStruct((N_TOK_TILES * N_EXPERTS, 128), F32)],
        scratch_shapes=[pltpu.VMEM((ATTN_WIDTH, D_MODEL), BF16), pltpu.VMEM((D_MODEL, D_MODEL), BF16)],
        compiler_params=pltpu.CompilerParams(
            dimension_semantics=("arbitrary",), vmem_limit_bytes=VMEM_LIMIT),
        name="merge_ln1_route",
    )(o_a, o_b, za, sgb, x, woa, wo, g, b, wr_hi, wr_lo)


def _granule_copy(src_ref, src_gran, dst_ref, dst_gran, sem):
    src = pl.multiple_of(src_gran * GRAN, GRAN)
    dst = pl.multiple_of(dst_gran * GRAN, GRAN)
    return pltpu.make_async_copy(src_ref.at[pl.ds(src, GRAN), :], dst_ref.at[pl.ds(dst, GRAN), :], sem)


def _expert_kernel(te_ref, nt_ref, gsrc_ref, gdst_ref, ug_ref, xs_ref, wg_ref, wu_ref, wd_ref, ys_ref,
                   xbuf, ybuf, zbuf, in_sem, out_sem, zero_sem):
    j = pl.program_id(0)
    n_tiles = nt_ref[0]
    slot = lax.rem(j, NBUF)

    def tile_gather(tile, s):
        for g in range(TILE_GRANS):
            _granule_copy(xs_ref, gsrc_ref[tile * TILE_GRANS + g], xbuf.at[s], g,
                          in_sem.at[s]).start(priority=g % 2)

    @pl.when(j == 0)
    def _():
        tile_gather(0, 0)
        ybuf[NBUF - 1] = jnp.zeros((TE, D_MODEL), BF16)
        zbuf[...] = jnp.zeros((GRAN, D_MODEL), BF16)
        for part in range(NBUF):
            spare = pltpu.make_async_copy(
                ybuf.at[NBUF - 1], ys_ref.at[pl.ds((SPARE_GRAN + part * TILE_GRANS) * GRAN, TE), :],
                out_sem.at[NBUF - 1])
            spare.start()
            spare.wait()

    for ahead in range(1, NBUF - 1):
        @pl.when(jnp.logical_and(j == 0, ahead < n_tiles))
        def _():
            tile_gather(ahead, ahead)

    def zero_copy(t, g):
        return _granule_copy(zbuf, 0, ys_ref, t * LOCAL_GRANS + g, zero_sem)

    @pl.when(jnp.logical_and(j >= 1, j <= N_TOK_TILES))
    def _():
        def wait(g, c):
            zero_copy(j - 1, g).wait()
            return c

        lax.fori_loop(ug_ref[j - 1], LOCAL_GRANS, wait, 0)

    @pl.when(j < N_TOK_TILES)
    def _():
        def start(g, c):
            zero_copy(j, g).start()
            return c

        lax.fori_loop(ug_ref[j], LOCAL_GRANS, start, 0)

    @pl.when(j + NBUF - 1 < n_tiles)
    def _():
        tile_gather(j + NBUF - 1, lax.rem(j + NBUF - 1, NBUF))

    @pl.when(jnp.logical_and(j >= NBUF, j - NBUF < n_tiles))
    def _():
        pltpu.make_async_copy(ybuf.at[slot], ys_ref.at[pl.ds(0, TE), :], out_sem.at[slot]).wait()

    @pl.when(j < n_tiles)
    def _():
        pltpu.make_async_copy(xs_ref.at[pl.ds(0, TE), :], xbuf.at[slot], in_sem.at[slot]).wait()
        xb = xbuf[slot]
        hg = jnp.dot(xb, wg_ref[0].astype(BF16), preferred_element_type=F32)
        hu = jnp.dot(xb, wu_ref[0].astype(BF16), preferred_element_type=F32)
        h = (hg * _sigmoid(hg) * hu).astype(BF16)
        ybuf[slot] = jnp.dot(h, wd_ref[0].astype(BF16), preferred_element_type=F32).astype(BF16)
        for g in range(TILE_GRANS):
            _granule_copy(ybuf.at[slot], g, ys_ref, gdst_ref[j * TILE_GRANS + g],
                          out_sem.at[slot]).start(priority=g % 2)


def _expert_call(tile_expert, n_tiles, gsrc, gdst, used_grans, xs, wg, wu, wd):
    wsel = lambda j, te, nt, gs, gd, ug: (te[j], 0, 0)
    return pl.pallas_call(
        _expert_kernel,
        grid_spec=pltpu.PrefetchScalarGridSpec(
            num_scalar_prefetch=5,
            grid=(MAX_TILES + NBUF,),
            in_specs=[pl.BlockSpec(memory_space=pl.ANY),
                      pl.BlockSpec((1, D_MODEL, D_EXPERT), wsel),
                      pl.BlockSpec((1, D_MODEL, D_EXPERT), wsel),
                      pl.BlockSpec((1, D_EXPERT, D_MODEL), wsel)],
            out_specs=pl.BlockSpec(memory_space=pl.ANY),
            scratch_shapes=[pltpu.VMEM((NBUF, TE, D_MODEL), BF16), pltpu.VMEM((NBUF, TE, D_MODEL), BF16),
                            pltpu.VMEM((GRAN, D_MODEL), BF16),
                            pltpu.SemaphoreType.DMA((NBUF,)), pltpu.SemaphoreType.DMA((NBUF,)),
                            pltpu.SemaphoreType.DMA],
        ),
        out_shape=jax.ShapeDtypeStruct(((SPARE_GRAN + NBUF * TILE_GRANS) * GRAN, D_MODEL), BF16),
        compiler_params=pltpu.CompilerParams(
            dimension_semantics=("arbitrary",), vmem_limit_bytes=VMEM_LIMIT),
        name="experts",
    )(tile_expert, n_tiles, gsrc, gdst, used_grans, xs, wg, wu, wd)


def _combine_kernel(ys_ref, x1_ref, rf_ref, g_ref, b_ref, out_ref):
    route = rf_ref[...].T
    col = lax.broadcasted_iota(I32, (TM, LOCAL_ROWS), 1).astype(F32)
    unsort = (jnp.where(col == route[:, 2:3], route[:, 0:1], 0.0)
              + jnp.where(col == route[:, 3:4], route[:, 1:2], 0.0)).astype(BF16)
    ffn = jnp.dot(unsort, ys_ref[...], preferred_element_type=F32)
    h = ALPHA * x1_ref[...] + ffn
    mu = jnp.mean(h, axis=-1, keepdims=True)
    hc = h - mu
    var = jnp.mean(hc * hc, axis=-1, keepdims=True)
    out_ref[...] = hc * lax.rsqrt(var + LN_EPS) * g_ref[...] + b_ref[...]


def _combine_call(ys, x1, rf, g, b):
    return pl.pallas_call(
        _combine_kernel,
        grid=(N_TOK_TILES,),
        in_specs=[pl.BlockSpec((LOCAL_ROWS, D_MODEL), lambda i: (i, 0)),
                  pl.BlockSpec((TM, D_MODEL), lambda i: (i, 0)),
                  pl.BlockSpec((8, TM), lambda i: (0, i)),
                  pl.BlockSpec((1, D_MODEL), lambda i: (0, 0)),
                  pl.BlockSpec((1, D_MODEL), lambda i: (0, 0))],
        out_specs=pl.BlockSpec((TM, D_MODEL), lambda i: (i, 0)),
        out_shape=jax.ShapeDtypeStruct((TOKENS, D_MODEL), F32),
        compiler_params=pltpu.CompilerParams(
            dimension_semantics=("arbitrary",), vmem_limit_bytes=VMEM_LIMIT),
        name="combine_ln2",
    )(ys, x1, rf, g, b)


def _router_cols(w_router_group, w_router_expert):
    w = jnp.concatenate([w_router_group, jnp.zeros((D_MODEL, 4), F32), w_router_expert,
                         jnp.zeros((D_MODEL, 128 - ROUTER_ROWS), F32)], axis=1)
    hi = w.astype(BF16)
    lo = (w - hi.astype(F32)).astype(BF16)
    return hi, lo


def _layer(x, w_in, conv_w, w_out_conv, w_out_attn, w_o, ln1_g, ln1_b,
           w_router_group, w_router_expert, w_gate, w_up, w_down, ln2_g, ln2_b):
    slopes = jnp.asarray([2.0 ** (-8.0 * (h + 1) / N_HEADS) for h in range(N_HEADS)], F32)
    q, k, v, za, sgb = _proj_call(x, w_in, conv_w, w_out_conv)
    o_a, o_b = _attn_call(slopes, q, k, v)

    wr_hi, wr_lo = _router_cols(w_router_group, w_router_expert)
    x1, xs, rf, mt = _merge_call(
        o_a, o_b, za.reshape(TOKENS, D_MODEL), sgb.reshape(TOKENS, D_MODEL),
        x.reshape(TOKENS, D_MODEL), w_out_attn, w_o,
        ln1_g.reshape(1, D_MODEL), ln1_b.reshape(1, D_MODEL), wr_hi, wr_lo)

    grans = mt.reshape(N_TOK_TILES, N_EXPERTS, 128)[:, :, 0].astype(I32)
    local_start = jnp.cumsum(grans, axis=1) - grans
    grans_t = grans.T
    tiles_e = (jnp.sum(grans_t, axis=1) + TILE_GRANS - 1) // TILE_GRANS
    tile_end = jnp.cumsum(tiles_e)
    n_tiles = tile_end[-1].reshape(1)
    tile_ids = jnp.arange(MAX_TILES + NBUF, dtype=I32)
    tile_expert = jnp.minimum(
        jnp.sum((tile_ids[:, None] >= tile_end[None, :]).astype(I32), axis=1), N_EXPERTS - 1)
    run_slot = TILE_GRANS * (tile_end - tiles_e)[:, None] + jnp.cumsum(grans_t, axis=1) - grans_t
    run_src = jnp.arange(N_TOK_TILES, dtype=I32)[None, :] * LOCAL_GRANS + local_start.T
    pick = (tile_expert[:MAX_TILES, None] == jnp.arange(N_EXPERTS, dtype=I32)[None, :])[:, :, None]
    t_slot = jnp.sum(jnp.where(pick, run_slot[None], 0), axis=1)
    t_len = jnp.sum(jnp.where(pick, grans_t[None], 0), axis=1)
    t_src = jnp.sum(jnp.where(pick, run_src[None], 0), axis=1)
    slots = jnp.arange(MAX_TILES * TILE_GRANS, dtype=I32).reshape(MAX_TILES, TILE_GRANS)
    k = slots[:, :, None] - t_slot[:, None, :]
    hit = (k >= 0) & (k < t_len[:, None, :])
    gran = jnp.sum(jnp.where(hit, t_src[:, None, :] + k, 0), axis=2).reshape(-1)
    filled = (jnp.sum(hit.astype(I32), axis=2) > 0).reshape(-1)
    slots = slots.reshape(-1)
    gsrc = jnp.where(filled, gran, 0)
    gdst = jnp.where(filled, gran, SPARE_GRAN + slots % (NBUF * TILE_GRANS))

    ys = _expert_call(tile_expert, n_tiles, gsrc, gdst, jnp.sum(grans, axis=1), xs, w_gate, w_up, w_down)
    out = _combine_call(ys, x1, rf, ln2_g.reshape(1, D_MODEL), ln2_b.reshape(1, D_MODEL))
    return out.reshape(BATCH, SEQ, D_MODEL)


def kernel(x, w_in, conv_w, w_out_conv, w_out_attn, w_o, ln1_g, ln1_b, w_router_group, w_router_expert, w_gate, w_up, w_down, ln2_g, ln2_b):
    depth = w_in.shape[0]
    for l in range(depth):
        x = _layer(x, w_in[l], conv_w[l], w_out_conv[l], w_out_attn[l], w_o[l], ln1_g[l], ln1_b[l],
                   w_router_group[l], w_router_expert[l], w_gate[l], w_up[l], w_down[l], ln2_g[l], ln2_b[l])
    return x
```

```python
import functools

import jax
import jax.numpy as jnp
from jax import lax
from jax.experimental import pallas as pl
from jax.experimental.pallas import tpu as pltpu

F32 = jnp.float32
BF16 = jnp.bfloat16
U32 = jnp.uint32
I32 = jnp.int32

D_MODEL = 1024
BATCH = 8
SEQ = 2048
TOKENS = BATCH * SEQ
CONV_WIDTH = 512
N_HEADS = 8
HEAD_DIM = 64
ATTN_WIDTH = N_HEADS * HEAD_DIM
MOBA_BLOCK = 256
N_BLOCKS = SEQ // MOBA_BLOCK
MOBA_TOPK = 3
N_GROUPS = 4
EXPERTS_PER_GROUP = 8
N_EXPERTS = N_GROUPS * EXPERTS_PER_GROUP
D_EXPERT = 256
LN_EPS = 1e-5
ALPHA = 2.0 ** 0.25
IN_COLS = 3 * CONV_WIDTH + 3 * ATTN_WIDTH + 2 * D_MODEL
HALF = D_MODEL // 2

TM = 256
TE = 256
PV_ROWS = HEAD_DIM + 16
GRAN = 16
TILE_GRANS = TE // GRAN
N_TOK_TILES = TOKENS // TM
LOCAL_ROWS = -(-(2 * TM + N_EXPERTS * (GRAN - 1)) // 256) * 256
LOCAL_GRANS = LOCAL_ROWS // GRAN
SPARE_GRAN = N_TOK_TILES * LOCAL_GRANS
MAX_TILES = (2 * TOKENS + N_TOK_TILES * N_EXPERTS * (GRAN - 1)) // TE + N_EXPERTS
MERGE_SUB = 2
EXP_SUB = 2
STEP_ROWS = EXP_SUB * TE
STEP_GRANS = EXP_SUB * TILE_GRANS
MAX_STEPS = -(-MAX_TILES // EXP_SUB)
NBUF = 4
ROUTER_ROWS = 40
VMEM_LIMIT = 56 * 1024 * 1024
NEG_INF = float("-inf")


def _sigmoid(z):
    return 1.0 / (1.0 + jnp.exp(-z))


def _proj_kernel(x_ref, w_in_f32_ref, convw_ref, woc_f32_ref, q_ref, k_ref, v_ref, za_ref, sgb_ref,
                 ubuf, w_in_ref, woc_ref):
    s = pl.program_id(1)

    @pl.when((pl.program_id(0) == 0) & (s == 0))
    def _():
        for c in range(0, IN_COLS, CONV_WIDTH):
            w_in_ref[:, c:c + CONV_WIDTH] = w_in_f32_ref[:, c:c + CONV_WIDTH].astype(BF16)
        woc_ref[...] = woc_f32_ref[...].astype(BF16)

    xb = x_ref[0].astype(BF16)

    def proj(c0, c1):
        return jnp.dot(xb, w_in_ref[:, c0:c1], preferred_element_type=F32)

    c_b = proj(0, CONV_WIDTH)
    u = proj(CONV_WIDTH, 2 * CONV_WIDTH) * proj(2 * CONV_WIDTH, 3 * CONV_WIDTH)

    @pl.when(s == 0)
    def _():
        ubuf[0:8, :] = jnp.zeros((8, CONV_WIDTH), F32)

    ubuf[8:8 + TM, :] = u
    w = convw_ref[...]
    conv = w[2:3, :] * u + w[1:2, :] * ubuf[7:7 + TM, :] + w[0:1, :] * ubuf[6:6 + TM, :]
    ubuf[0:8, :] = u[TM - 8:TM, :]
    hc = (c_b * conv).astype(BF16)
    y_conv = jnp.dot(hc, woc_ref[...], preferred_element_type=F32)

    o = 3 * CONV_WIDTH
    q_ref[0] = (proj(o, o + ATTN_WIDTH) * (HEAD_DIM ** -0.5)).astype(BF16)
    k_ref[0] = proj(o + ATTN_WIDTH, o + 2 * ATTN_WIDTH).astype(BF16)
    v_ref[0] = proj(o + 2 * ATTN_WIDTH, o + 3 * ATTN_WIDTH).astype(BF16)
    o += 3 * ATTN_WIDTH
    za_ref[0] = (_sigmoid(proj(o, o + D_MODEL)) * y_conv).astype(BF16)
    sgb_ref[0] = _sigmoid(proj(o + D_MODEL, o + 2 * D_MODEL)).astype(BF16)


def _proj_call(x, w_in, conv_w, w_out_conv):
    tok_spec = lambda c: pl.BlockSpec((1, TM, c), lambda b, s: (b, s, 0))
    full = lambda shape: pl.BlockSpec(shape, lambda b, s: (0,) * len(shape))
    once = lambda shape: pl.BlockSpec(shape, lambda b, s: (0,) * len(shape), pipeline_mode=pl.Buffered(1))
    return pl.pallas_call(
        _proj_kernel,
        grid=(BATCH, SEQ // TM),
        in_specs=[tok_spec(D_MODEL), once((D_MODEL, IN_COLS)), full((3, CONV_WIDTH)),
                  once((CONV_WIDTH, D_MODEL))],
        out_specs=[tok_spec(ATTN_WIDTH), tok_spec(ATTN_WIDTH), tok_spec(ATTN_WIDTH),
                   tok_spec(D_MODEL), tok_spec(D_MODEL)],
        out_shape=[jax.ShapeDtypeStruct((BATCH, SEQ, ATTN_WIDTH), BF16)] * 3
        + [jax.ShapeDtypeStruct((BATCH, SEQ, D_MODEL), BF16)] * 2,
        scratch_shapes=[pltpu.VMEM((TM + 8, CONV_WIDTH), F32), pltpu.VMEM((D_MODEL, IN_COLS), BF16),
                        pltpu.VMEM((CONV_WIDTH, D_MODEL), BF16)],
        compiler_params=pltpu.CompilerParams(
            dimension_semantics=("arbitrary", "arbitrary"), vmem_limit_bytes=VMEM_LIMIT),
        name="proj",
    )(x, w_in, conv_w, w_out_conv)


def _attn_kernel(slopes_ref, qa_ref, qb_ref, k_ref, v_ref, oa_ref, ob_ref,
                 kaug_ref, vt_ref, kmean_ref, qaug_ref, pv_ref, mloc_ref, t_ref, p_ref):
    hp = pl.program_id(1)
    j = pl.program_id(2)
    blk = MOBA_BLOCK

    @pl.when(j == 0)
    def _():
        klane = lax.broadcasted_iota(I32, (blk, 128), 1)
        koff = lax.broadcasted_iota(I32, (blk, 128), 0).astype(F32)
        k_extra = jnp.where(klane == 0, koff, jnp.where(klane == 1, 1.0, 0.0)).astype(BF16)
        orow = lax.broadcasted_iota(I32, (PV_ROWS - HEAD_DIM, blk), 0)
        ones_rows = jnp.where(orow == 0, 1.0, 0.0).astype(BF16)
        for n in range(N_BLOCKS):
            kblk = k_ref[0, n * blk:(n + 1) * blk, :]
            kaug_ref[n, :, 0:128] = kblk
            kaug_ref[n, :, 128:256] = k_extra
            kmean_ref[n:n + 1, :] = jnp.mean(kblk.astype(F32), axis=0, keepdims=True)
            v_t = v_ref[0, n * blk:(n + 1) * blk, :].astype(F32).T.astype(BF16)
            for hh in range(2):
                vt_ref[n, hh, 0:HEAD_DIM, :] = v_t[hh * HEAD_DIM:(hh + 1) * HEAD_DIM, :]
                vt_ref[n, hh, HEAD_DIM:PV_ROWS, :] = ones_rows

    lane = lax.broadcasted_iota(I32, (1, 2 * blk), 1)
    slope_row = jnp.where(lane < blk, slopes_ref[2 * hp], slopes_ref[2 * hp + 1])
    qoff_row = jnp.where(lane < blk, lane, lane - blk).astype(F32)
    feat = lax.broadcasted_iota(I32, (2 * HEAD_DIM, blk), 0)
    arow = lax.broadcasted_iota(I32, (2 * HEAD_DIM, 2 * blk), 0)
    q_extra = jnp.where(arow == 0, slope_row, jnp.where(arow == 1, -slope_row * qoff_row, 0.0)).astype(BF16)
    blk_i = lax.broadcasted_iota(I32, (N_BLOCKS, 2 * blk), 0)
    kmean = kmean_ref[...].astype(BF16)
    key_i = lax.broadcasted_iota(I32, (blk, 2 * blk), 0)
    qry_j = lax.broadcasted_iota(I32, (blk, 2 * blk), 1)
    causal = key_i <= jnp.where(qry_j < blk, qry_j, qry_j - blk)

    def prepare(q_ref, slot, qblock):
        q_t = q_ref[0].astype(F32).T
        qcat = jnp.concatenate([jnp.where(feat < HEAD_DIM, q_t, 0.0), jnp.where(feat >= HEAD_DIM, q_t, 0.0)],
                               axis=1).astype(BF16)
        qaug_ref[slot, 0:2 * HEAD_DIM, :] = qcat
        qaug_ref[slot, 2 * HEAD_DIM:4 * HEAD_DIM, :] = q_extra
        gate = jnp.dot(kmean, qcat, preferred_element_type=F32)
        cnt = jnp.zeros((N_BLOCKS, 2 * blk), F32)
        for m in range(N_BLOCKS):
            gm = gate[m:m + 1, :]
            beats = (gm > gate) | ((gm == gate) & (blk_i > m))
            cnt = cnt + jnp.where(beats & (qblock > m), 1.0, 0.0)
        return jnp.where((blk_i < qblock) & (cnt < float(MOBA_TOPK)), 1.0, 0.0)

    qblock_a = j
    qblock_b = N_BLOCKS - 1 - j
    sel_a = prepare(qa_ref, 0, qblock_a)
    sel_b = prepare(qb_ref, 1, qblock_b)

    n_mid = N_BLOCKS - 1
    slots = [(0, 0, qblock_a, True)]
    mids = []
    for s in range(1, n_mid + 1):
        is_a = s <= j
        slots.append((s, jnp.where(is_a, 0, 1), jnp.where(is_a, s - 1, s - 1 - j), False))
        mids.append((is_a, slots[-1][2]))
    slots.append((n_mid + 1, 1, qblock_b, True))

    for s, which, kb, _ in slots:
        t_ref[s] = jnp.dot(kaug_ref[kb], qaug_ref[which], preferred_element_type=F32)
    for s, _, _, own in slots:
        t = t_ref[s]
        if own:
            t = jnp.where(causal, t, NEG_INF)
        m_loc = jnp.max(t, axis=0, keepdims=True)
        p_ref[s] = jnp.exp((t - m_loc).astype(BF16))
        mloc_ref[s:s + 1, :] = m_loc
    for s, _, kb, _ in slots:
        pv_ref[s, 0] = jnp.dot(vt_ref[kb, 0], p_ref[s, :, 0:blk], preferred_element_type=F32)
        pv_ref[s, 1] = jnp.dot(vt_ref[kb, 1], p_ref[s, :, blk:2 * blk], preferred_element_type=F32)

    def combine(o_ref, own_slot, sel, qblock, mine):
        neg = jnp.full((1, 2 * blk), -1e30, F32)
        pieces = [(own_slot, mloc_ref[own_slot:own_slot + 1, :])]
        for s, (is_a, kb) in enumerate(mids, start=1):
            selrow = jnp.sum(jnp.where(blk_i == kb, sel, 0.0), axis=0, keepdims=True)
            belongs = jnp.where(is_a, 1.0, 0.0) if mine else jnp.where(is_a, 0.0, 1.0)
            used = selrow * belongs > 0.5
            shift = slope_row * ((kb - qblock) * blk).astype(F32)
            pieces.append((s, jnp.where(used, mloc_ref[s:s + 1, :] + shift, neg)))
        m_all = pieces[0][1]
        for _, m_s in pieces[1:]:
            m_all = jnp.maximum(m_all, m_s)
        acc = [jnp.zeros((PV_ROWS, blk), F32), jnp.zeros((PV_ROWS, blk), F32)]
        for s, m_s in pieces:
            w = jnp.exp(m_s - m_all)
            for hh in range(2):
                acc[hh] = acc[hh] + pv_ref[s, hh] * w[:, hh * blk:(hh + 1) * blk]
        o_t = jnp.concatenate([a[0:HEAD_DIM, :] / a[HEAD_DIM:HEAD_DIM + 1, :] for a in acc], axis=0)
        o_ref[0] = o_t.T.astype(BF16)

    combine(oa_ref, 0, sel_a, qblock_a, True)
    combine(ob_ref, n_mid + 1, sel_b, qblock_b, False)


def _attn_call(slopes, q, k, v):
    half = N_BLOCKS // 2
    o_a, o_b = pl.pallas_call(
        _attn_kernel,
        grid_spec=pltpu.PrefetchScalarGridSpec(
            num_scalar_prefetch=1,
            grid=(BATCH, N_HEADS // 2, half),
            in_specs=[
                pl.BlockSpec((1, MOBA_BLOCK, 128), lambda b, h, j, sl: (b, j, h)),
                pl.BlockSpec((1, MOBA_BLOCK, 128), lambda b, h, j, sl: (b, N_BLOCKS - 1 - j, h)),
                pl.BlockSpec((1, SEQ, 128), lambda b, h, j, sl: (b, 0, h)),
                pl.BlockSpec((1, SEQ, 128), lambda b, h, j, sl: (b, 0, h)),
            ],
            out_specs=[pl.BlockSpec((1, MOBA_BLOCK, 128), lambda b, h, j, sl: (b, j, h)),
                       pl.BlockSpec((1, MOBA_BLOCK, 128), lambda b, h, j, sl: (b, half - 1 - j, h))],
            scratch_shapes=[
                pltpu.VMEM((N_BLOCKS, MOBA_BLOCK, 256), BF16),
                pltpu.VMEM((N_BLOCKS, 2, PV_ROWS, MOBA_BLOCK), BF16),
                pltpu.VMEM((N_BLOCKS, 128), F32),
                pltpu.VMEM((2, 256, 2 * MOBA_BLOCK), BF16),
                pltpu.VMEM((N_BLOCKS + 1, 2, PV_ROWS, MOBA_BLOCK), F32),
                pltpu.VMEM((16, 2 * MOBA_BLOCK), F32),
                pltpu.VMEM((N_BLOCKS + 1, MOBA_BLOCK, 2 * MOBA_BLOCK), F32),
                pltpu.VMEM((N_BLOCKS + 1, MOBA_BLOCK, 2 * MOBA_BLOCK), BF16),
            ],
        ),
        out_shape=[jax.ShapeDtypeStruct((BATCH, SEQ // 2, ATTN_WIDTH), BF16)] * 2,
        compiler_params=pltpu.CompilerParams(
            dimension_semantics=("arbitrary", "arbitrary", "arbitrary"), vmem_limit_bytes=VMEM_LIMIT),
        name="moba_attn",
    )(slopes, q, q, k, v)
    return o_a, o_b


def _route(logits):
    row8 = lax.broadcasted_iota(I32, (8, TM), 0).astype(F32)
    gl = jnp.where(row8 < float(N_GROUPS), logits[0:8, :], NEG_INF)
    gexp = jnp.exp(gl - jnp.max(gl, axis=0, keepdims=True))
    gprob = gexp / jnp.sum(gexp, axis=0, keepdims=True)
    ptop = jnp.max(gprob, axis=0, keepdims=True)
    gtop = jnp.min(jnp.where(gprob == ptop, row8, 8.0), axis=0, keepdims=True)
    el = logits[8:ROUTER_ROWS, :]
    eg = jnp.where(gtop == 0.0, el[0:8, :],
                   jnp.where(gtop == 1.0, el[8:16, :], jnp.where(gtop == 2.0, el[16:24, :], el[24:32, :])))
    m1 = jnp.max(eg, axis=0, keepdims=True)
    i1 = jnp.min(jnp.where(eg == m1, row8, 8.0), axis=0, keepdims=True)
    eg2 = jnp.where(row8 == i1, NEG_INF, eg)
    m2 = jnp.max(eg2, axis=0, keepdims=True)
    i2 = jnp.min(jnp.where(eg2 == m2, row8, 8.0), axis=0, keepdims=True)
    t2 = jnp.exp(m2 - m1)
    gate1 = ptop * (1.0 / (1.0 + t2))
    gate2 = ptop * (t2 / (1.0 + t2))
    erow = lax.broadcasted_iota(I32, (N_EXPERTS, TM), 0).astype(F32)
    oh1 = jnp.where(erow == gtop * float(EXPERTS_PER_GROUP) + i1, 1.0, 0.0)
    oh2 = jnp.where(erow == gtop * float(EXPERTS_PER_GROUP) + i2, 1.0, 0.0)
    return gate1, gate2, oh1, oh2


def _merge_kernel(oa_ref, ob_ref, za_ref, sgb_ref, x_ref, woa_f32_ref, wo_f32_ref, g_ref, b_ref,
                  wr_hi_ref, wr_lo_ref, x1_ref, xs_ref, rf_ref, mt_ref, woa_ref, wo_ref):
    i = pl.program_id(0)

    @pl.when(i == 0)
    def _():
        woa_ref[...] = woa_f32_ref[...].astype(BF16)
        wo_ref[...] = wo_f32_ref[...].astype(BF16)

    subs = range(MERGE_SUB)
    rows = [slice(s * TM, (s + 1) * TM) for s in subs]
    steps_per_batch = N_BLOCKS // MERGE_SUB
    in_oa = lax.rem(i, steps_per_batch) < steps_per_batch // 2
    o = [jnp.where(in_oa, oa_ref[0, r, :], ob_ref[0, r, :]) for r in rows]
    y_attn = [jnp.dot(o[s], woa_ref[...], preferred_element_type=F32) for s in subs]
    y = [(za_ref[rows[s], :].astype(F32) + sgb_ref[rows[s], :].astype(F32) * y_attn[s]).astype(BF16)
         for s in subs]
    mix = [jnp.dot(y[s], wo_ref[...], preferred_element_type=F32) for s in subs]
    x1 = []
    for s in subs:
        h = ALPHA * x_ref[rows[s], :] + mix[s]
        mu = jnp.mean(h, axis=-1, keepdims=True)
        hc = h - mu
        var = jnp.mean(hc * hc, axis=-1, keepdims=True)
        x1.append(hc * lax.rsqrt(var + LN_EPS) * g_ref[...] + b_ref[...])
        x1_ref[rows[s], :] = x1[s]

    xh = [x1[s].astype(BF16) for s in subs]
    xl = [(x1[s] - xh[s].astype(F32)).astype(BF16) for s in subs]
    wh = wr_hi_ref[...]
    logits = [(jnp.dot(xh[s], wh, preferred_element_type=F32)
               + jnp.dot(xl[s], wh, preferred_element_type=F32)
               + jnp.dot(xh[s], wr_lo_ref[...], preferred_element_type=F32)).T for s in subs]
    routes = [_route(logits[s]) for s in subs]

    ta = lax.broadcasted_iota(I32, (TM, TM), 0)
    tb = lax.broadcasted_iota(I32, (TM, TM), 1)
    upper = jnp.where(ta < tb, 1.0, 0.0).astype(BF16)
    ea = lax.broadcasted_iota(I32, (N_EXPERTS, N_EXPERTS), 0)
    eb = lax.broadcasted_iota(I32, (N_EXPERTS, N_EXPERTS), 1)
    lower = jnp.where(eb < ea, 1.0, 0.0).astype(BF16)
    lrow = lax.broadcasted_iota(I32, (LOCAL_ROWS, TM), 0).astype(F32)
    zero = jnp.zeros((1, TM), F32)
    cum = [jnp.dot((routes[s][2] + routes[s][3]).astype(BF16), upper, preferred_element_type=F32) for s in subs]
    perm = []
    for s in subs:
        gate1, gate2, oh1, oh2 = routes[s]
        n_e = jnp.sum(oh1 + oh2, axis=1, keepdims=True)
        m_rep = jnp.broadcast_to(jnp.floor((n_e + float(GRAN - 1)) * (1.0 / GRAN)), (N_EXPERTS, 128))
        run_start = jnp.dot(lower, m_rep.astype(BF16), preferred_element_type=F32)
        tot = cum[s] + float(GRAN) * run_start[:, 0:1]
        lp1 = jnp.sum(oh1 * tot, axis=0, keepdims=True)
        lp2 = jnp.sum(oh2 * tot, axis=0, keepdims=True)
        perm.append(jnp.where((lrow == lp1) | (lrow == lp2), 1.0, 0.0).astype(BF16))
        rf_ref[:, rows[s]] = jnp.concatenate([gate1, gate2, lp1, lp2, zero, zero, zero, zero], axis=0)
        mt_ref[s * N_EXPERTS:(s + 1) * N_EXPERTS, :] = m_rep
    for s in subs:
        xs_ref[s * LOCAL_ROWS:(s + 1) * LOCAL_ROWS, :] = jnp.dot(
            perm[s], xh[s], preferred_element_type=F32).astype(BF16)


def _merge_call(o_a, o_b, za, sgb, x, woa, wo, g, b, wr_hi, wr_lo):
    tm = MERGE_SUB * TM
    tok = lambda c: pl.BlockSpec((tm, c), lambda i: (i, 0))
    full = lambda shape: pl.BlockSpec(shape, lambda i: (0,) * len(shape))
    per_batch = SEQ // tm
    half = per_batch // 2
    o_a_spec = pl.BlockSpec((1, tm, ATTN_WIDTH), lambda i: (i // per_batch, jnp.minimum(i % per_batch, half - 1), 0))
    o_b_spec = pl.BlockSpec((1, tm, ATTN_WIDTH), lambda i: (i // per_batch, jnp.maximum(i % per_batch - half, 0), 0))
    return pl.pallas_call(
        _merge_kernel,
        grid=(TOKENS // tm,),
        in_specs=[o_a_spec, o_b_spec, tok(D_MODEL), tok(D_MODEL), tok(D_MODEL),
                  full((ATTN_WIDTH, D_MODEL)), full((D_MODEL, D_MODEL)), full((1, D_MODEL)),
                  full((1, D_MODEL)), full((D_MODEL, 128)), full((D_MODEL, 128))],
        out_specs=[tok(D_MODEL), pl.BlockSpec((MERGE_SUB * LOCAL_ROWS, D_MODEL), lambda i: (i, 0)),
                   pl.BlockSpec((8, tm), lambda i: (0, i)),
                   pl.BlockSpec((MERGE_SUB * N_EXPERTS, 128), lambda i: (i, 0))],
        out_shape=[jax.ShapeDtypeStruct((TOKENS, D_MODEL), F32),
                   jax.ShapeDtypeStruct((N_TOK_TILES * LOCAL_ROWS, D_MODEL), BF16),
                   jax.ShapeDtypeStruct((8, TOKENS), F32),
                   jax.ShapeDtypeStruct((N_TOK_TILES * N_EXPERTS, 128), F32)],
        scratch_shapes=[pltpu.VMEM((ATTN_WIDTH, D_MODEL), BF16), pltpu.VMEM((D_MODEL, D_MODEL), BF16)],
        compiler_params=pltpu.CompilerParams(
            dimension_semantics=("arbitrary",), vmem_limit_bytes=VMEM_LIMIT),
        name="merge_ln1_route",
    )(o_a, o_b, za, sgb, x, woa, wo, g, b, wr_hi, wr_lo)


def _granule_copy(src_ref, src_gran, dst_ref, dst_gran, sem):
    src = pl.multiple_of(src_gran * GRAN, GRAN)
    dst = pl.multiple_of(dst_gran * GRAN, GRAN)
    return pltpu.make_async_copy(src_ref.at[pl.ds(src, GRAN), :], dst_ref.at[pl.ds(dst, GRAN), :], sem)


def _expert_kernel(te_ref, nt_ref, gsrc_ref, gdst_ref, ug_ref, xs_ref, *refs):
    w_refs = refs[:3 * EXP_SUB]
    ys_ref, xbuf, ybuf, zbuf, in_sem, out_sem, zero_sem = refs[3 * EXP_SUB:]
    j = pl.program_id(0)
    n_tiles = nt_ref[0]
    slot = lax.rem(j, NBUF)

    def tile_gather(step, s):
        for g in range(STEP_GRANS):
            _granule_copy(xs_ref, gsrc_ref[step * STEP_GRANS + g], xbuf.at[s], g,
                          in_sem.at[s]).start(priority=g % 2)

    @pl.when(j == 0)
    def _():
        tile_gather(0, 0)
        ybuf[NBUF - 1] = jnp.zeros((STEP_ROWS, D_MODEL), BF16)
        zbuf[...] = jnp.zeros((GRAN, D_MODEL), BF16)
        for part in range(NBUF):
            spare = pltpu.make_async_copy(
                ybuf.at[NBUF - 1], ys_ref.at[pl.ds((SPARE_GRAN + part * STEP_GRANS) * GRAN, STEP_ROWS), :],
                out_sem.at[NBUF - 1])
            spare.start()
            spare.wait()

    for ahead in range(1, NBUF - 1):
        @pl.when(jnp.logical_and(j == 0, ahead < n_tiles))
        def _():
            tile_gather(ahead, ahead)

    def zero_copy(t, g):
        return _granule_copy(zbuf, 0, ys_ref, t * LOCAL_GRANS + g, zero_sem)

    @pl.when(jnp.logical_and(j >= 1, j <= N_TOK_TILES))
    def _():
        def wait(g, c):
            zero_copy(j - 1, g).wait()
            return c

        lax.fori_loop(ug_ref[j - 1], LOCAL_GRANS, wait, 0)

    @pl.when(j < N_TOK_TILES)
    def _():
        def start(g, c):
            zero_copy(j, g).start()
            return c

        lax.fori_loop(ug_ref[j], LOCAL_GRANS, start, 0)

    @pl.when(j + NBUF - 1 < n_tiles)
    def _():
        tile_gather(j + NBUF - 1, lax.rem(j + NBUF - 1, NBUF))

    @pl.when(jnp.logical_and(j >= NBUF, j - NBUF < n_tiles))
    def _():
        pltpu.make_async_copy(ybuf.at[slot], ys_ref.at[pl.ds(0, STEP_ROWS), :], out_sem.at[slot]).wait()

    @pl.when(j < n_tiles)
    def _():
        pltpu.make_async_copy(xs_ref.at[pl.ds(0, STEP_ROWS), :], xbuf.at[slot], in_sem.at[slot]).wait()
        subs = range(EXP_SUB)
        wg, wu, wd = ([w_refs[3 * s + k][0].astype(BF16) for s in subs] for k in range(3))
        xb = [xbuf[slot, s * TE:(s + 1) * TE, :] for s in subs]
        hg = [jnp.dot(xb[s], wg[s], preferred_element_type=F32) for s in subs]
        hu = [jnp.dot(xb[s], wu[s], preferred_element_type=F32) for s in subs]
        h = [(hg[s] * _sigmoid(hg[s]) * hu[s]).astype(BF16) for s in subs]
        for s in subs:
            ybuf[slot, s * TE:(s + 1) * TE, :] = jnp.dot(h[s], wd[s], preferred_element_type=F32).astype(BF16)
        for g in range(STEP_GRANS):
            _granule_copy(ybuf.at[slot], g, ys_ref, gdst_ref[j * STEP_GRANS + g],
                          out_sem.at[slot]).start(priority=g % 2)


def _expert_call(tile_expert, n_steps, gsrc, gdst, used_grans, xs, wg, wu, wd):
    w_specs, w_args = [], []
    for s in range(EXP_SUB):
        wsel = lambda j, te, nt, gs, gd, ug, s=s: (te[EXP_SUB * j + s], 0, 0)
        w_specs += [pl.BlockSpec((1, D_MODEL, D_EXPERT), wsel), pl.BlockSpec((1, D_MODEL, D_EXPERT), wsel),
                    pl.BlockSpec((1, D_EXPERT, D_MODEL), wsel)]
        w_args += [wg, wu, wd]
    return pl.pallas_call(
        _expert_kernel,
        grid_spec=pltpu.PrefetchScalarGridSpec(
            num_scalar_prefetch=5,
            grid=(MAX_STEPS + NBUF,),
            in_specs=[pl.BlockSpec(memory_space=pl.ANY)] + w_specs,
            out_specs=pl.BlockSpec(memory_space=pl.ANY),
            scratch_shapes=[pltpu.VMEM((NBUF, STEP_ROWS, D_MODEL), BF16),
                            pltpu.VMEM((NBUF, STEP_ROWS, D_MODEL), BF16),
                            pltpu.VMEM((GRAN, D_MODEL), BF16),
                            pltpu.SemaphoreType.DMA((NBUF,)), pltpu.SemaphoreType.DMA((NBUF,)),
                            pltpu.SemaphoreType.DMA],
        ),
        out_shape=jax.ShapeDtypeStruct(((SPARE_GRAN + NBUF * STEP_GRANS) * GRAN, D_MODEL), BF16),
        compiler_params=pltpu.CompilerParams(
            dimension_semantics=("arbitrary",), vmem_limit_bytes=VMEM_LIMIT),
        name="experts",
    )(tile_expert, n_steps, gsrc, gdst, used_grans, xs, *w_args)


def _combine_kernel(ys_ref, x1_ref, rf_ref, g_ref, b_ref, out_ref):
    subs = range(MERGE_SUB)
    col = lax.broadcasted_iota(I32, (TM, LOCAL_ROWS), 1).astype(F32)
    route = [rf_ref[:, s * TM:(s + 1) * TM].T for s in subs]
    unsort = [(jnp.where(col == r[:, 2:3], r[:, 0:1], 0.0)
               + jnp.where(col == r[:, 3:4], r[:, 1:2], 0.0)).astype(BF16) for r in route]
    ffn = [jnp.dot(unsort[s], ys_ref[s * LOCAL_ROWS:(s + 1) * LOCAL_ROWS, :], preferred_element_type=F32)
           for s in subs]
    for s in subs:
        h = ALPHA * x1_ref[s * TM:(s + 1) * TM, :] + ffn[s]
        mu = jnp.mean(h, axis=-1, keepdims=True)
        hc = h - mu
        var = jnp.mean(hc * hc, axis=-1, keepdims=True)
        out_ref[s * TM:(s + 1) * TM, :] = hc * lax.rsqrt(var + LN_EPS) * g_ref[...] + b_ref[...]


def _combine_call(ys, x1, rf, g, b):
    tm = MERGE_SUB * TM
    return pl.pallas_call(
        _combine_kernel,
        grid=(TOKENS // tm,),
        in_specs=[pl.BlockSpec((MERGE_SUB * LOCAL_ROWS, D_MODEL), lambda i: (i, 0)),
                  pl.BlockSpec((tm, D_MODEL), lambda i: (i, 0)),
                  pl.BlockSpec((8, tm), lambda i: (0, i)),
                  pl.BlockSpec((1, D_MODEL), lambda i: (0, 0)),
                  pl.BlockSpec((1, D_MODEL), lambda i: (0, 0))],
        out_specs=pl.BlockSpec((tm, D_MODEL), lambda i: (i, 0)),
        out_shape=jax.ShapeDtypeStruct((TOKENS, D_MODEL), F32),
        compiler_params=pltpu.CompilerParams(
            dimension_semantics=("arbitrary",), vmem_limit_bytes=VMEM_LIMIT),
        name="combine_ln2",
    )(ys, x1, rf, g, b)


def _router_cols(w_router_group, w_router_expert):
    w = jnp.concatenate([w_router_group, jnp.zeros((D_MODEL, 4), F32), w_router_expert,
                         jnp.zeros((D_MODEL, 128 - ROUTER_ROWS), F32)], axis=1)
    hi = w.astype(BF16)
    lo = (w - hi.astype(F32)).astype(BF16)
    return hi, lo


def _layer(x, w_in, conv_w, w_out_conv, w_out_attn, w_o, ln1_g, ln1_b,
           w_router_group, w_router_expert, w_gate, w_up, w_down, ln2_g, ln2_b):
    slopes = jnp.asarray([2.0 ** (-8.0 * (h + 1) / N_HEADS) for h in range(N_HEADS)], F32)
    q, k, v, za, sgb = _proj_call(x, w_in, conv_w, w_out_conv)
    o_a, o_b = _attn_call(slopes, q, k, v)

    wr_hi, wr_lo = _router_cols(w_router_group, w_router_expert)
    x1, xs, rf, mt = _merge_call(
        o_a, o_b, za.reshape(TOKENS, D_MODEL), sgb.reshape(TOKENS, D_MODEL),
        x.reshape(TOKENS, D_MODEL), w_out_attn, w_o,
        ln1_g.reshape(1, D_MODEL), ln1_b.reshape(1, D_MODEL), wr_hi, wr_lo)

    grans = mt.reshape(N_TOK_TILES, N_EXPERTS, 128)[:, :, 0].astype(I32)
    local_start = jnp.cumsum(grans, axis=1) - grans
    grans_t = grans.T
    tiles_e = (jnp.sum(grans_t, axis=1) + TILE_GRANS - 1) // TILE_GRANS
    tile_end = jnp.cumsum(tiles_e)
    n_steps = ((tile_end[-1] + EXP_SUB - 1) // EXP_SUB).reshape(1)
    all_tiles = MAX_STEPS * EXP_SUB
    tile_ids = jnp.arange(all_tiles + NBUF * EXP_SUB, dtype=I32)
    tile_expert = jnp.minimum(
        jnp.sum((tile_ids[:, None] >= tile_end[None, :]).astype(I32), axis=1), N_EXPERTS - 1)
    run_slot = TILE_GRANS * (tile_end - tiles_e)[:, None] + jnp.cumsum(grans_t, axis=1) - grans_t
    run_src = jnp.arange(N_TOK_TILES, dtype=I32)[None, :] * LOCAL_GRANS + local_start.T
    pick = (tile_expert[:all_tiles, None] == jnp.arange(N_EXPERTS, dtype=I32)[None, :])[:, :, None]
    t_slot = jnp.sum(jnp.where(pick, run_slot[None], 0), axis=1)
    t_len = jnp.sum(jnp.where(pick, grans_t[None], 0), axis=1)
    t_src = jnp.sum(jnp.where(pick, run_src[None], 0), axis=1)
    slots = jnp.arange(all_tiles * TILE_GRANS, dtype=I32).reshape(all_tiles, TILE_GRANS)
    k = slots[:, :, None] - t_slot[:, None, :]
    hit = (k >= 0) & (k < t_len[:, None, :])
    gran = jnp.sum(jnp.where(hit, t_src[:, None, :] + k, 0), axis=2).reshape(-1)
    filled = (jnp.sum(hit.astype(I32), axis=2) > 0).reshape(-1)
    slots = slots.reshape(-1)
    gsrc = jnp.where(filled, gran, 0)
    gdst = jnp.where(filled, gran, SPARE_GRAN + slots % (NBUF * STEP_GRANS))

    ys = _expert_call(tile_expert, n_steps, gsrc, gdst, jnp.sum(grans, axis=1), xs, w_gate, w_up, w_down)
    out = _combine_call(ys, x1, rf, ln2_g.reshape(1, D_MODEL), ln2_b.reshape(1, D_MODEL))
    return out.reshape(BATCH, SEQ, D_MODEL)


def kernel(x, w_in, conv_w, w_out_conv, w_out_attn, w_o, ln1_g, ln1_b, w_router_group, w_router_expert, w_gate, w_up, w_down, ln2_g, ln2_b):
    depth = w_in.shape[0]
    for l in range(depth):
        x = _layer(x, w_in[l], conv_w[l], w_out_conv[l], w_out_attn[l], w_o[l], ln1_g[l], ln1_b[l],
                   w_router_group[l], w_router_expert[l], w_gate[l], w_up[l], w_down[l], ln2_g[l], ln2_b[l])
    return x
```

```python
import functools

import jax
import jax.numpy as jnp
from jax import lax
from jax.experimental import pallas as pl
from jax.experimental.pallas import tpu as pltpu

F32 = jnp.float32
BF16 = jnp.bfloat16
U32 = jnp.uint32
I32 = jnp.int32

D_MODEL = 1024
BATCH = 8
SEQ = 2048
TOKENS = BATCH * SEQ
CONV_WIDTH = 512
N_HEADS = 8
HEAD_DIM = 64
ATTN_WIDTH = N_HEADS * HEAD_DIM
MOBA_BLOCK = 256
N_BLOCKS = SEQ // MOBA_BLOCK
MOBA_TOPK = 3
N_GROUPS = 4
EXPERTS_PER_GROUP = 8
N_EXPERTS = N_GROUPS * EXPERTS_PER_GROUP
D_EXPERT = 256
LN_EPS = 1e-5
ALPHA = 2.0 ** 0.25
IN_COLS = 3 * CONV_WIDTH + 3 * ATTN_WIDTH + 2 * D_MODEL
HALF = D_MODEL // 2

TM = 256
TE = 512
CHAIN_ROWS = 256
PV_ROWS = HEAD_DIM + 16
GRAN = 16
TILE_GRANS = TE // GRAN
N_TOK_TILES = TOKENS // TM
LOCAL_ROWS = -(-(2 * TM + N_EXPERTS * (GRAN - 1)) // 256) * 256
LOCAL_GRANS = LOCAL_ROWS // GRAN
SPARE_GRAN = N_TOK_TILES * LOCAL_GRANS
MAX_TILES = (2 * TOKENS + N_TOK_TILES * N_EXPERTS * (GRAN - 1)) // TE + N_EXPERTS
MERGE_SUB = 2
EXP_SUB = 1
STEP_ROWS = EXP_SUB * TE
STEP_GRANS = EXP_SUB * TILE_GRANS
MAX_STEPS = -(-MAX_TILES // EXP_SUB)
NBUF = 4
ROUTER_ROWS = 40
VMEM_LIMIT = 56 * 1024 * 1024
NEG_INF = float("-inf")


def _sigmoid(z):
    return 1.0 / (1.0 + jnp.exp(-z))


def _proj_kernel(x_ref, w_in_f32_ref, convw_ref, woc_f32_ref, q_ref, k_ref, v_ref, za_ref, sgb_ref,
                 ubuf, w_in_ref, woc_ref):
    s = pl.program_id(1)

    @pl.when((pl.program_id(0) == 0) & (s == 0))
    def _():
        for c in range(0, IN_COLS, CONV_WIDTH):
            w_in_ref[:, c:c + CONV_WIDTH] = w_in_f32_ref[:, c:c + CONV_WIDTH].astype(BF16)
        woc_ref[...] = woc_f32_ref[...].astype(BF16)

    xb = x_ref[0].astype(BF16)

    def proj(c0, c1):
        return jnp.dot(xb, w_in_ref[:, c0:c1], preferred_element_type=F32)

    c_b = proj(0, CONV_WIDTH)
    u = proj(CONV_WIDTH, 2 * CONV_WIDTH) * proj(2 * CONV_WIDTH, 3 * CONV_WIDTH)

    @pl.when(s == 0)
    def _():
        ubuf[0:8, :] = jnp.zeros((8, CONV_WIDTH), F32)

    ubuf[8:8 + TM, :] = u
    w = convw_ref[...]
    conv = w[2:3, :] * u + w[1:2, :] * ubuf[7:7 + TM, :] + w[0:1, :] * ubuf[6:6 + TM, :]
    ubuf[0:8, :] = u[TM - 8:TM, :]
    hc = (c_b * conv).astype(BF16)
    y_conv = jnp.dot(hc, woc_ref[...], preferred_element_type=F32)

    o = 3 * CONV_WIDTH
    q_ref[0] = (proj(o, o + ATTN_WIDTH) * (HEAD_DIM ** -0.5)).astype(BF16)
    k_ref[0] = proj(o + ATTN_WIDTH, o + 2 * ATTN_WIDTH).astype(BF16)
    v_ref[0] = proj(o + 2 * ATTN_WIDTH, o + 3 * ATTN_WIDTH).astype(BF16)
    o += 3 * ATTN_WIDTH
    za_ref[0] = (_sigmoid(proj(o, o + D_MODEL)) * y_conv).astype(BF16)
    sgb_ref[0] = _sigmoid(proj(o + D_MODEL, o + 2 * D_MODEL)).astype(BF16)


def _proj_call(x, w_in, conv_w, w_out_conv):
    tok_spec = lambda c: pl.BlockSpec((1, TM, c), lambda b, s: (b, s, 0))
    full = lambda shape: pl.BlockSpec(shape, lambda b, s: (0,) * len(shape))
    once = lambda shape: pl.BlockSpec(shape, lambda b, s: (0,) * len(shape), pipeline_mode=pl.Buffered(1))
    return pl.pallas_call(
        _proj_kernel,
        grid=(BATCH, SEQ // TM),
        in_specs=[tok_spec(D_MODEL), once((D_MODEL, IN_COLS)), full((3, CONV_WIDTH)),
                  once((CONV_WIDTH, D_MODEL))],
        out_specs=[tok_spec(ATTN_WIDTH), tok_spec(ATTN_WIDTH), tok_spec(ATTN_WIDTH),
                   tok_spec(D_MODEL), tok_spec(D_MODEL)],
        out_shape=[jax.ShapeDtypeStruct((BATCH, SEQ, ATTN_WIDTH), BF16)] * 3
        + [jax.ShapeDtypeStruct((BATCH, SEQ, D_MODEL), BF16)] * 2,
        scratch_shapes=[pltpu.VMEM((TM + 8, CONV_WIDTH), F32), pltpu.VMEM((D_MODEL, IN_COLS), BF16),
                        pltpu.VMEM((CONV_WIDTH, D_MODEL), BF16)],
        compiler_params=pltpu.CompilerParams(
            dimension_semantics=("arbitrary", "arbitrary"), vmem_limit_bytes=VMEM_LIMIT),
        name="proj",
    )(x, w_in, conv_w, w_out_conv)


def _attn_kernel(slopes_ref, qa_ref, qb_ref, k_ref, v_ref, oa_ref, ob_ref,
                 kaug_ref, vt_ref, kmean_ref, qaug_ref, pv_ref, mloc_ref, t_ref, p_ref):
    hp = pl.program_id(1)
    j = pl.program_id(2)
    blk = MOBA_BLOCK

    @pl.when(j == 0)
    def _():
        klane = lax.broadcasted_iota(I32, (blk, 128), 1)
        koff = lax.broadcasted_iota(I32, (blk, 128), 0).astype(F32)
        k_extra = jnp.where(klane == 0, koff, jnp.where(klane == 1, 1.0, 0.0)).astype(BF16)
        orow = lax.broadcasted_iota(I32, (PV_ROWS - HEAD_DIM, blk), 0)
        ones_rows = jnp.where(orow == 0, 1.0, 0.0).astype(BF16)
        for n in range(N_BLOCKS):
            kblk = k_ref[0, n * blk:(n + 1) * blk, :]
            kaug_ref[n, :, 0:128] = kblk
            kaug_ref[n, :, 128:256] = k_extra
            kmean_ref[n:n + 1, :] = jnp.mean(kblk.astype(F32), axis=0, keepdims=True)
            v_t = v_ref[0, n * blk:(n + 1) * blk, :].astype(F32).T.astype(BF16)
            for hh in range(2):
                vt_ref[n, hh, 0:HEAD_DIM, :] = v_t[hh * HEAD_DIM:(hh + 1) * HEAD_DIM, :]
                vt_ref[n, hh, HEAD_DIM:PV_ROWS, :] = ones_rows

    lane = lax.broadcasted_iota(I32, (1, 2 * blk), 1)
    slope_row = jnp.where(lane < blk, slopes_ref[2 * hp], slopes_ref[2 * hp + 1])
    qoff_row = jnp.where(lane < blk, lane, lane - blk).astype(F32)
    feat = lax.broadcasted_iota(I32, (2 * HEAD_DIM, blk), 0)
    arow = lax.broadcasted_iota(I32, (2 * HEAD_DIM, 2 * blk), 0)
    q_extra = jnp.where(arow == 0, slope_row, jnp.where(arow == 1, -slope_row * qoff_row, 0.0)).astype(BF16)
    blk_i = lax.broadcasted_iota(I32, (N_BLOCKS, 2 * blk), 0)
    kmean = kmean_ref[...].astype(BF16)
    key_i = lax.broadcasted_iota(I32, (blk, 2 * blk), 0)
    qry_j = lax.broadcasted_iota(I32, (blk, 2 * blk), 1)
    causal = key_i <= jnp.where(qry_j < blk, qry_j, qry_j - blk)

    def prepare(q_ref, slot, qblock):
        q_t = q_ref[0].astype(F32).T
        qcat = jnp.concatenate([jnp.where(feat < HEAD_DIM, q_t, 0.0), jnp.where(feat >= HEAD_DIM, q_t, 0.0)],
                               axis=1).astype(BF16)
        qaug_ref[slot, 0:2 * HEAD_DIM, :] = qcat
        qaug_ref[slot, 2 * HEAD_DIM:4 * HEAD_DIM, :] = q_extra
        gate = jnp.dot(kmean, qcat, preferred_element_type=F32)
        cnt = jnp.zeros((N_BLOCKS, 2 * blk), F32)
        for m in range(N_BLOCKS):
            gm = gate[m:m + 1, :]
            beats = (gm > gate) | ((gm == gate) & (blk_i > m))
            cnt = cnt + jnp.where(beats & (qblock > m), 1.0, 0.0)
        return jnp.where((blk_i < qblock) & (cnt < float(MOBA_TOPK)), 1.0, 0.0)

    qblock_a = j
    qblock_b = N_BLOCKS - 1 - j
    sel_a = prepare(qa_ref, 0, qblock_a)
    sel_b = prepare(qb_ref, 1, qblock_b)

    n_mid = N_BLOCKS - 1
    slots = [(0, 0, qblock_a, True)]
    mids = []
    for s in range(1, n_mid + 1):
        is_a = s <= j
        slots.append((s, jnp.where(is_a, 0, 1), jnp.where(is_a, s - 1, s - 1 - j), False))
        mids.append((is_a, slots[-1][2]))
    slots.append((n_mid + 1, 1, qblock_b, True))

    for s, which, kb, _ in slots:
        t_ref[s] = jnp.dot(kaug_ref[kb], qaug_ref[which], preferred_element_type=F32).astype(BF16)
    for s, _, _, own in slots:
        t = t_ref[s]
        if own:
            t = jnp.where(causal, t, NEG_INF)
        m_loc = jnp.max(t, axis=0, keepdims=True)
        p_ref[s] = jnp.exp(t - m_loc)
        mloc_ref[s:s + 1, :] = m_loc.astype(F32)
    for s, _, kb, _ in slots:
        pv_ref[s, 0] = jnp.dot(vt_ref[kb, 0], p_ref[s, :, 0:blk], preferred_element_type=F32)
        pv_ref[s, 1] = jnp.dot(vt_ref[kb, 1], p_ref[s, :, blk:2 * blk], preferred_element_type=F32)

    def combine(o_ref, own_slot, sel, qblock, mine):
        neg = jnp.full((1, 2 * blk), -1e30, F32)
        pieces = [(own_slot, mloc_ref[own_slot:own_slot + 1, :])]
        for s, (is_a, kb) in enumerate(mids, start=1):
            selrow = jnp.sum(jnp.where(blk_i == kb, sel, 0.0), axis=0, keepdims=True)
            belongs = jnp.where(is_a, 1.0, 0.0) if mine else jnp.where(is_a, 0.0, 1.0)
            used = selrow * belongs > 0.5
            shift = slope_row * ((kb - qblock) * blk).astype(F32)
            pieces.append((s, jnp.where(used, mloc_ref[s:s + 1, :] + shift, neg)))
        m_all = pieces[0][1]
        for _, m_s in pieces[1:]:
            m_all = jnp.maximum(m_all, m_s)
        acc = [jnp.zeros((PV_ROWS, blk), F32), jnp.zeros((PV_ROWS, blk), F32)]
        for s, m_s in pieces:
            w = jnp.exp(m_s - m_all)
            for hh in range(2):
                acc[hh] = acc[hh] + pv_ref[s, hh] * w[:, hh * blk:(hh + 1) * blk]
        o_t = jnp.concatenate([a[0:HEAD_DIM, :] / a[HEAD_DIM:HEAD_DIM + 1, :] for a in acc], axis=0)
        o_ref[0] = o_t.T.astype(BF16)

    combine(oa_ref, 0, sel_a, qblock_a, True)
    combine(ob_ref, n_mid + 1, sel_b, qblock_b, False)


def _attn_call(slopes, q, k, v):
    half = N_BLOCKS // 2
    o_a, o_b = pl.pallas_call(
        _attn_kernel,
        grid_spec=pltpu.PrefetchScalarGridSpec(
            num_scalar_prefetch=1,
            grid=(BATCH, N_HEADS // 2, half),
            in_specs=[
                pl.BlockSpec((1, MOBA_BLOCK, 128), lambda b, h, j, sl: (b, j, h)),
                pl.BlockSpec((1, MOBA_BLOCK, 128), lambda b, h, j, sl: (b, N_BLOCKS - 1 - j, h)),
                pl.BlockSpec((1, SEQ, 128), lambda b, h, j, sl: (b, 0, h)),
                pl.BlockSpec((1, SEQ, 128), lambda b, h, j, sl: (b, 0, h)),
            ],
            out_specs=[pl.BlockSpec((1, MOBA_BLOCK, 128), lambda b, h, j, sl: (b, j, h)),
                       pl.BlockSpec((1, MOBA_BLOCK, 128), lambda b, h, j, sl: (b, half - 1 - j, h))],
            scratch_shapes=[
                pltpu.VMEM((N_BLOCKS, MOBA_BLOCK, 256), BF16),
                pltpu.VMEM((N_BLOCKS, 2, PV_ROWS, MOBA_BLOCK), BF16),
                pltpu.VMEM((N_BLOCKS, 128), F32),
                pltpu.VMEM((2, 256, 2 * MOBA_BLOCK), BF16),
                pltpu.VMEM((N_BLOCKS + 1, 2, PV_ROWS, MOBA_BLOCK), F32),
                pltpu.VMEM((16, 2 * MOBA_BLOCK), F32),
                pltpu.VMEM((N_BLOCKS + 1, MOBA_BLOCK, 2 * MOBA_BLOCK), BF16),
                pltpu.VMEM((N_BLOCKS + 1, MOBA_BLOCK, 2 * MOBA_BLOCK), BF16),
            ],
        ),
        out_shape=[jax.ShapeDtypeStruct((BATCH, SEQ // 2, ATTN_WIDTH), BF16)] * 2,
        compiler_params=pltpu.CompilerParams(
            dimension_semantics=("arbitrary", "arbitrary", "arbitrary"), vmem_limit_bytes=VMEM_LIMIT),
        name="moba_attn",
    )(slopes, q, q, k, v)
    return o_a, o_b


def _route(logits):
    row8 = lax.broadcasted_iota(I32, (8, TM), 0).astype(F32)
    gl = jnp.where(row8 < float(N_GROUPS), logits[0:8, :], NEG_INF)
    gexp = jnp.exp(gl - jnp.max(gl, axis=0, keepdims=True))
    gprob = gexp / jnp.sum(gexp, axis=0, keepdims=True)
    ptop = jnp.max(gprob, axis=0, keepdims=True)
    gtop = jnp.min(jnp.where(gprob == ptop, row8, 8.0), axis=0, keepdims=True)
    el = logits[8:ROUTER_ROWS, :]
    eg = jnp.where(gtop == 0.0, el[0:8, :],
                   jnp.where(gtop == 1.0, el[8:16, :], jnp.where(gtop == 2.0, el[16:24, :], el[24:32, :])))
    m1 = jnp.max(eg, axis=0, keepdims=True)
    i1 = jnp.min(jnp.where(eg == m1, row8, 8.0), axis=0, keepdims=True)
    eg2 = jnp.where(row8 == i1, NEG_INF, eg)
    m2 = jnp.max(eg2, axis=0, keepdims=True)
    i2 = jnp.min(jnp.where(eg2 == m2, row8, 8.0), axis=0, keepdims=True)
    t2 = jnp.exp(m2 - m1)
    gate1 = ptop * (1.0 / (1.0 + t2))
    gate2 = ptop * (t2 / (1.0 + t2))
    erow = lax.broadcasted_iota(I32, (N_EXPERTS, TM), 0).astype(F32)
    oh1 = jnp.where(erow == gtop * float(EXPERTS_PER_GROUP) + i1, 1.0, 0.0)
    oh2 = jnp.where(erow == gtop * float(EXPERTS_PER_GROUP) + i2, 1.0, 0.0)
    return gate1, gate2, oh1, oh2


def _merge_kernel(oa_ref, ob_ref, za_ref, sgb_ref, x_ref, woa_f32_ref, wo_f32_ref, g_ref, b_ref,
                  wr_hi_ref, wr_lo_ref, x1_ref, xs_ref, rf_ref, mt_ref, woa_ref, wo_ref):
    i = pl.program_id(0)

    @pl.when(i == 0)
    def _():
        woa_ref[...] = woa_f32_ref[...].astype(BF16)
        wo_ref[...] = wo_f32_ref[...].astype(BF16)

    subs = range(MERGE_SUB)
    rows = [slice(s * TM, (s + 1) * TM) for s in subs]
    steps_per_batch = N_BLOCKS // MERGE_SUB
    in_oa = lax.rem(i, steps_per_batch) < steps_per_batch // 2
    o = [jnp.where(in_oa, oa_ref[0, r, :], ob_ref[0, r, :]) for r in rows]
    y_attn = [jnp.dot(o[s], woa_ref[...], preferred_element_type=F32) for s in subs]
    y = [(za_ref[rows[s], :].astype(F32) + sgb_ref[rows[s], :].astype(F32) * y_attn[s]).astype(BF16)
         for s in subs]
    mix = [jnp.dot(y[s], wo_ref[...], preferred_element_type=F32) for s in subs]
    x1 = []
    for s in subs:
        h = ALPHA * x_ref[rows[s], :] + mix[s]
        mu = jnp.mean(h, axis=-1, keepdims=True)
        hc = h - mu
        var = jnp.mean(hc * hc, axis=-1, keepdims=True)
        x1.append(hc * lax.rsqrt(var + LN_EPS) * g_ref[...] + b_ref[...])
        x1_ref[rows[s], :] = x1[s]

    xh = [x1[s].astype(BF16) for s in subs]
    xl = [(x1[s] - xh[s].astype(F32)).astype(BF16) for s in subs]
    wh = wr_hi_ref[...]
    logits = [(jnp.dot(xh[s], wh, preferred_element_type=F32)
               + jnp.dot(xl[s], wh, preferred_element_type=F32)
               + jnp.dot(xh[s], wr_lo_ref[...], preferred_element_type=F32)).T for s in subs]
    routes = [_route(logits[s]) for s in subs]

    ta = lax.broadcasted_iota(I32, (TM, TM), 0)
    tb = lax.broadcasted_iota(I32, (TM, TM), 1)
    upper = jnp.where(ta < tb, 1.0, 0.0).astype(BF16)
    ea = lax.broadcasted_iota(I32, (N_EXPERTS, N_EXPERTS), 0)
    eb = lax.broadcasted_iota(I32, (N_EXPERTS, N_EXPERTS), 1)
    lower = jnp.where(eb < ea, 1.0, 0.0).astype(BF16)
    lrow = lax.broadcasted_iota(I32, (LOCAL_ROWS, TM), 0).astype(F32)
    zero = jnp.zeros((1, TM), F32)
    cum = [jnp.dot((routes[s][2] + routes[s][3]).astype(BF16), upper, preferred_element_type=F32) for s in subs]
    perm = []
    for s in subs:
        gate1, gate2, oh1, oh2 = routes[s]
        n_e = jnp.sum(oh1 + oh2, axis=1, keepdims=True)
        m_rep = jnp.broadcast_to(jnp.floor((n_e + float(GRAN - 1)) * (1.0 / GRAN)), (N_EXPERTS, 128))
        run_start = jnp.dot(lower, m_rep.astype(BF16), preferred_element_type=F32)
        tot = cum[s] + float(GRAN) * run_start[:, 0:1]
        lp1 = jnp.sum(oh1 * tot, axis=0, keepdims=True)
        lp2 = jnp.sum(oh2 * tot, axis=0, keepdims=True)
        perm.append(jnp.where((lrow == lp1) | (lrow == lp2), 1.0, 0.0).astype(BF16))
        rf_ref[:, rows[s]] = jnp.concatenate([gate1, gate2, lp1, lp2, zero, zero, zero, zero], axis=0)
        mt_ref[s * N_EXPERTS:(s + 1) * N_EXPERTS, :] = m_rep
    for s in subs:
        xs_ref[s * LOCAL_ROWS:(s + 1) * LOCAL_ROWS, :] = jnp.dot(
            perm[s], xh[s], preferred_element_type=F32).astype(BF16)


def _merge_call(o_a, o_b, za, sgb, x, woa, wo, g, b, wr_hi, wr_lo):
    tm = MERGE_SUB * TM
    tok = lambda c: pl.BlockSpec((tm, c), lambda i: (i, 0))
    full = lambda shape: pl.BlockSpec(shape, lambda i: (0,) * len(shape))
    per_batch = SEQ // tm
    half = per_batch // 2
    o_a_spec = pl.BlockSpec((1, tm, ATTN_WIDTH), lambda i: (i // per_batch, jnp.minimum(i % per_batch, half - 1), 0))
    o_b_spec = pl.BlockSpec((1, tm, ATTN_WIDTH), lambda i: (i // per_batch, jnp.maximum(i % per_batch - half, 0), 0))
    return pl.pallas_call(
        _merge_kernel,
        grid=(TOKENS // tm,),
        in_specs=[o_a_spec, o_b_spec, tok(D_MODEL), tok(D_MODEL), tok(D_MODEL),
                  full((ATTN_WIDTH, D_MODEL)), full((D_MODEL, D_MODEL)), full((1, D_MODEL)),
                  full((1, D_MODEL)), full((D_MODEL, 128)), full((D_MODEL, 128))],
        out_specs=[tok(D_MODEL), pl.BlockSpec((MERGE_SUB * LOCAL_ROWS, D_MODEL), lambda i: (i, 0)),
                   pl.BlockSpec((8, tm), lambda i: (0, i)),
                   pl.BlockSpec((MERGE_SUB * N_EXPERTS, 128), lambda i: (i, 0))],
        out_shape=[jax.ShapeDtypeStruct((TOKENS, D_MODEL), F32),
                   jax.ShapeDtypeStruct((N_TOK_TILES * LOCAL_ROWS, D_MODEL), BF16),
                   jax.ShapeDtypeStruct((8, TOKENS), F32),
                   jax.ShapeDtypeStruct((N_TOK_TILES * N_EXPERTS, 128), F32)],
        scratch_shapes=[pltpu.VMEM((ATTN_WIDTH, D_MODEL), BF16), pltpu.VMEM((D_MODEL, D_MODEL), BF16)],
        compiler_params=pltpu.CompilerParams(
            dimension_semantics=("arbitrary",), vmem_limit_bytes=VMEM_LIMIT),
        name="merge_ln1_route",
    )(o_a, o_b, za, sgb, x, woa, wo, g, b, wr_hi, wr_lo)


def _granule_copy(src_ref, src_gran, dst_ref, dst_gran, sem):
    src = pl.multiple_of(src_gran * GRAN, GRAN)
    dst = pl.multiple_of(dst_gran * GRAN, GRAN)
    return pltpu.make_async_copy(src_ref.at[pl.ds(src, GRAN), :], dst_ref.at[pl.ds(dst, GRAN), :], sem)


def _expert_kernel(te_ref, nt_ref, gsrc_ref, gdst_ref, ug_ref, xs_ref, *refs):
    w_refs = refs[:3 * EXP_SUB]
    ys_ref, xbuf, ybuf, zbuf, in_sem, out_sem, zero_sem = refs[3 * EXP_SUB:]
    j = pl.program_id(0)
    n_tiles = nt_ref[0]
    slot = lax.rem(j, NBUF)

    def tile_gather(step, s):
        for g in range(STEP_GRANS):
            _granule_copy(xs_ref, gsrc_ref[step * STEP_GRANS + g], xbuf.at[s], g,
                          in_sem.at[s]).start(priority=g % 2)

    @pl.when(j == 0)
    def _():
        tile_gather(0, 0)
        ybuf[NBUF - 1] = jnp.zeros((STEP_ROWS, D_MODEL), BF16)
        zbuf[...] = jnp.zeros((GRAN, D_MODEL), BF16)
        for part in range(NBUF):
            spare = pltpu.make_async_copy(
                ybuf.at[NBUF - 1], ys_ref.at[pl.ds((SPARE_GRAN + part * STEP_GRANS) * GRAN, STEP_ROWS), :],
                out_sem.at[NBUF - 1])
            spare.start()
            spare.wait()

    for ahead in range(1, NBUF - 1):
        @pl.when(jnp.logical_and(j == 0, ahead < n_tiles))
        def _():
            tile_gather(ahead, ahead)

    def zero_copy(t, g):
        return _granule_copy(zbuf, 0, ys_ref, t * LOCAL_GRANS + g, zero_sem)

    @pl.when(jnp.logical_and(j >= 1, j <= N_TOK_TILES))
    def _():
        def wait(g, c):
            zero_copy(j - 1, g).wait()
            return c

        lax.fori_loop(ug_ref[j - 1], LOCAL_GRANS, wait, 0)

    @pl.when(j < N_TOK_TILES)
    def _():
        def start(g, c):
            zero_copy(j, g).start()
            return c

        lax.fori_loop(ug_ref[j], LOCAL_GRANS, start, 0)

    @pl.when(j + NBUF - 1 < n_tiles)
    def _():
        tile_gather(j + NBUF - 1, lax.rem(j + NBUF - 1, NBUF))

    @pl.when(jnp.logical_and(j >= NBUF, j - NBUF < n_tiles))
    def _():
        pltpu.make_async_copy(ybuf.at[slot], ys_ref.at[pl.ds(0, STEP_ROWS), :], out_sem.at[slot]).wait()

    @pl.when(j < n_tiles)
    def _():
        pltpu.make_async_copy(xs_ref.at[pl.ds(0, STEP_ROWS), :], xbuf.at[slot], in_sem.at[slot]).wait()
        wg, wu, wd = ([w_refs[3 * s + k][0].astype(BF16) for s in range(EXP_SUB)] for k in range(3))
        chains = [(s, s * TE + r) for s in range(EXP_SUB) for r in range(0, TE, CHAIN_ROWS)]
        xb = [xbuf[slot, r:r + CHAIN_ROWS, :] for _, r in chains]
        hg = [jnp.dot(x, wg[s], preferred_element_type=F32) for x, (s, _) in zip(xb, chains)]
        hu = [jnp.dot(x, wu[s], preferred_element_type=F32) for x, (s, _) in zip(xb, chains)]
        h = [(a * _sigmoid(a) * b).astype(BF16) for a, b in zip(hg, hu)]
        for hc, (s, r) in zip(h, chains):
            ybuf[slot, r:r + CHAIN_ROWS, :] = jnp.dot(hc, wd[s], preferred_element_type=F32).astype(BF16)
        for g in range(STEP_GRANS):
            _granule_copy(ybuf.at[slot], g, ys_ref, gdst_ref[j * STEP_GRANS + g],
                          out_sem.at[slot]).start(priority=g % 2)


def _expert_call(tile_expert, n_steps, gsrc, gdst, used_grans, xs, wg, wu, wd):
    w_specs, w_args = [], []
    for s in range(EXP_SUB):
        wsel = lambda j, te, nt, gs, gd, ug, s=s: (te[EXP_SUB * j + s], 0, 0)
        w_specs += [pl.BlockSpec((1, D_MODEL, D_EXPERT), wsel), pl.BlockSpec((1, D_MODEL, D_EXPERT), wsel),
                    pl.BlockSpec((1, D_EXPERT, D_MODEL), wsel)]
        w_args += [wg, wu, wd]
    return pl.pallas_call(
        _expert_kernel,
        grid_spec=pltpu.PrefetchScalarGridSpec(
            num_scalar_prefetch=5,
            grid=(MAX_STEPS + NBUF,),
            in_specs=[pl.BlockSpec(memory_space=pl.ANY)] + w_specs,
            out_specs=pl.BlockSpec(memory_space=pl.ANY),
            scratch_shapes=[pltpu.VMEM((NBUF, STEP_ROWS, D_MODEL), BF16),
                            pltpu.VMEM((NBUF, STEP_ROWS, D_MODEL), BF16),
                            pltpu.VMEM((GRAN, D_MODEL), BF16),
                            pltpu.SemaphoreType.DMA((NBUF,)), pltpu.SemaphoreType.DMA((NBUF,)),
                            pltpu.SemaphoreType.DMA],
        ),
        out_shape=jax.ShapeDtypeStruct(((SPARE_GRAN + NBUF * STEP_GRANS) * GRAN, D_MODEL), BF16),
        compiler_params=pltpu.CompilerParams(
            dimension_semantics=("arbitrary",), vmem_limit_bytes=VMEM_LIMIT),
        name="experts",
    )(tile_expert, n_steps, gsrc, gdst, used_grans, xs, *w_args)


def _combine_kernel(ys_ref, x1_ref, rf_ref, g_ref, b_ref, out_ref):
    subs = range(MERGE_SUB)
    col = lax.broadcasted_iota(I32, (TM, LOCAL_ROWS), 1).astype(F32)
    route = [rf_ref[:, s * TM:(s + 1) * TM].T for s in subs]
    unsort = [(jnp.where(col == r[:, 2:3], r[:, 0:1], 0.0)
               + jnp.where(col == r[:, 3:4], r[:, 1:2], 0.0)).astype(BF16) for r in route]
    ffn = [jnp.dot(unsort[s], ys_ref[s * LOCAL_ROWS:(s + 1) * LOCAL_ROWS, :], preferred_element_type=F32)
           for s in subs]
    for s in subs:
        h = ALPHA * x1_ref[s * TM:(s + 1) * TM, :] + ffn[s]
        mu = jnp.mean(h, axis=-1, keepdims=True)
        hc = h - mu
        var = jnp.mean(hc * hc, axis=-1, keepdims=True)
        out_ref[s * TM:(s + 1) * TM, :] = hc * lax.rsqrt(var + LN_EPS) * g_ref[...] + b_ref[...]


def _combine_call(ys, x1, rf, g, b):
    tm = MERGE_SUB * TM
    return pl.pallas_call(
        _combine_kernel,
        grid=(TOKENS // tm,),
        in_specs=[pl.BlockSpec((MERGE_SUB * LOCAL_ROWS, D_MODEL), lambda i: (i, 0)),
                  pl.BlockSpec((tm, D_MODEL), lambda i: (i, 0)),
                  pl.BlockSpec((8, tm), lambda i: (0, i)),
                  pl.BlockSpec((1, D_MODEL), lambda i: (0, 0)),
                  pl.BlockSpec((1, D_MODEL), lambda i: (0, 0))],
        out_specs=pl.BlockSpec((tm, D_MODEL), lambda i: (i, 0)),
        out_shape=jax.ShapeDtypeStruct((TOKENS, D_MODEL), F32),
        compiler_params=pltpu.CompilerParams(
            dimension_semantics=("arbitrary",), vmem_limit_bytes=VMEM_LIMIT),
        name="combine_ln2",
    )(ys, x1, rf, g, b)


def _router_cols(w_router_group, w_router_expert):
    w = jnp.concatenate([w_router_group, jnp.zeros((D_MODEL, 4), F32), w_router_expert,
                         jnp.zeros((D_MODEL, 128 - ROUTER_ROWS), F32)], axis=1)
    hi = w.astype(BF16)
    lo = (w - hi.astype(F32)).astype(BF16)
    return hi, lo


def _layer(x, w_in, conv_w, w_out_conv, w_out_attn, w_o, ln1_g, ln1_b,
           w_router_group, w_router_expert, w_gate, w_up, w_down, ln2_g, ln2_b):
    slopes = jnp.asarray([2.0 ** (-8.0 * (h + 1) / N_HEADS) for h in range(N_HEADS)], F32)
    q, k, v, za, sgb = _proj_call(x, w_in, conv_w, w_out_conv)
    o_a, o_b = _attn_call(slopes, q, k, v)

    wr_hi, wr_lo = _router_cols(w_router_group, w_router_expert)
    x1, xs, rf, mt = _merge_call(
        o_a, o_b, za.reshape(TOKENS, D_MODEL), sgb.reshape(TOKENS, D_MODEL),
        x.reshape(TOKENS, D_MODEL), w_out_attn, w_o,
        ln1_g.reshape(1, D_MODEL), ln1_b.reshape(1, D_MODEL), wr_hi, wr_lo)

    grans = mt.reshape(N_TOK_TILES, N_EXPERTS, 128)[:, :, 0].astype(I32)
    local_start = jnp.cumsum(grans, axis=1) - grans
    grans_t = grans.T
    tiles_e = (jnp.sum(grans_t, axis=1) + TILE_GRANS - 1) // TILE_GRANS
    tile_end = jnp.cumsum(tiles_e)
    n_steps = ((tile_end[-1] + EXP_SUB - 1) // EXP_SUB).reshape(1)
    all_tiles = MAX_STEPS * EXP_SUB
    tile_ids = jnp.arange(all_tiles + NBUF * EXP_SUB, dtype=I32)
    tile_expert = jnp.minimum(
        jnp.sum((tile_ids[:, None] >= tile_end[None, :]).astype(I32), axis=1), N_EXPERTS - 1)
    run_slot = TILE_GRANS * (tile_end - tiles_e)[:, None] + jnp.cumsum(grans_t, axis=1) - grans_t
    run_src = jnp.arange(N_TOK_TILES, dtype=I32)[None, :] * LOCAL_GRANS + local_start.T
    pick = (tile_expert[:all_tiles, None] == jnp.arange(N_EXPERTS, dtype=I32)[None, :])[:, :, None]
    t_slot = jnp.sum(jnp.where(pick, run_slot[None], 0), axis=1)
    t_len = jnp.sum(jnp.where(pick, grans_t[None], 0), axis=1)
    t_src = jnp.sum(jnp.where(pick, run_src[None], 0), axis=1)
    slots = jnp.arange(all_tiles * TILE_GRANS, dtype=I32).reshape(all_tiles, TILE_GRANS)
    k = slots[:, :, None] - t_slot[:, None, :]
    hit = (k >= 0) & (k < t_len[:, None, :])
    gran = jnp.sum(jnp.where(hit, t_src[:, None, :] + k, 0), axis=2).reshape(-1)
    filled = (jnp.sum(hit.astype(I32), axis=2) > 0).reshape(-1)
    slots = slots.reshape(-1)
    gsrc = jnp.where(filled, gran, 0)
    gdst = jnp.where(filled, gran, SPARE_GRAN + slots % (NBUF * STEP_GRANS))

    ys = _expert_call(tile_expert, n_steps, gsrc, gdst, jnp.sum(grans, axis=1), xs, w_gate, w_up, w_down)
    out = _combine_call(ys, x1, rf, ln2_g.reshape(1, D_MODEL), ln2_b.reshape(1, D_MODEL))
    return out.reshape(BATCH, SEQ, D_MODEL)


def kernel(x, w_in, conv_w, w_out_conv, w_out_attn, w_o, ln1_g, ln1_b, w_router_group, w_router_expert, w_gate, w_up, w_down, ln2_g, ln2_b):
    depth = w_in.shape[0]
    for l in range(depth):
        x = _layer(x, w_in[l], conv_w[l], w_out_conv[l], w_out_attn[l], w_o[l], ln1_g[l], ln1_b[l],
                   w_router_group[l], w_router_expert[l], w_gate[l], w_up[l], w_down[l], ln2_g[l], ln2_b[l])
    return x
```

```python
import functools

import jax
import jax.numpy as jnp
from jax import lax
from jax.experimental import pallas as pl
from jax.experimental.pallas import tpu as pltpu

F32 = jnp.float32
BF16 = jnp.bfloat16
U32 = jnp.uint32
I32 = jnp.int32

D_MODEL = 1024
BATCH = 8
SEQ = 2048
TOKENS = BATCH * SEQ
CONV_WIDTH = 512
N_HEADS = 8
HEAD_DIM = 64
ATTN_WIDTH = N_HEADS * HEAD_DIM
MOBA_BLOCK = 256
N_BLOCKS = SEQ // MOBA_BLOCK
MOBA_TOPK = 3
N_GROUPS = 4
EXPERTS_PER_GROUP = 8
N_EXPERTS = N_GROUPS * EXPERTS_PER_GROUP
D_EXPERT = 256
LN_EPS = 1e-5
ALPHA = 2.0 ** 0.25
IN_COLS = 3 * CONV_WIDTH + 3 * ATTN_WIDTH + 2 * D_MODEL
HALF = D_MODEL // 2

TM = 256
TM_PROJ = 512
TE = 512
CHAIN_ROWS = 256
PV_ROWS = HEAD_DIM + 16
GRAN = 16
TILE_GRANS = TE // GRAN
N_TOK_TILES = TOKENS // TM
LOCAL_ROWS = -(-(2 * TM + N_EXPERTS * (GRAN - 1)) // 256) * 256
LOCAL_GRANS = LOCAL_ROWS // GRAN
SPARE_GRAN = N_TOK_TILES * LOCAL_GRANS
MAX_TILES = (2 * TOKENS + N_TOK_TILES * N_EXPERTS * (GRAN - 1)) // TE + N_EXPERTS
MERGE_SUB = 2
EXP_SUB = 1
STEP_ROWS = EXP_SUB * TE
STEP_GRANS = EXP_SUB * TILE_GRANS
MAX_STEPS = -(-MAX_TILES // EXP_SUB)
NBUF = 4
ROUTER_ROWS = 40
VMEM_LIMIT = 56 * 1024 * 1024
NEG_INF = float("-inf")


def _sigmoid(z):
    return 1.0 / (1.0 + jnp.exp(-z))


def _proj_kernel(x_ref, w_in_hbm_ref, convw_ref, woc_f32_ref, q_ref, k_ref, v_ref, za_ref, sgb_ref,
                 ubuf, w_in_ref, woc_ref, stage, stage_sem):
    s = pl.program_id(1)
    tm = TM_PROJ

    @pl.when((pl.program_id(0) == 0) & (s == 0))
    def _():
        def chunk(c):
            return pltpu.make_async_copy(w_in_hbm_ref.at[:, c * CONV_WIDTH:(c + 1) * CONV_WIDTH],
                                         stage.at[c % 2], stage_sem.at[c % 2])

        n_chunks = IN_COLS // CONV_WIDTH
        chunk(0).start()
        for c in range(n_chunks):
            if c + 1 < n_chunks:
                chunk(c + 1).start()
            chunk(c).wait()
            w_in_ref[:, c * CONV_WIDTH:(c + 1) * CONV_WIDTH] = stage[c % 2].astype(BF16)
        woc_ref[...] = woc_f32_ref[...].astype(BF16)

    xb = x_ref[0].astype(BF16)

    def proj(c0, c1):
        return jnp.dot(xb, w_in_ref[:, c0:c1], preferred_element_type=F32)

    c_b = proj(0, CONV_WIDTH)
    u = proj(CONV_WIDTH, 2 * CONV_WIDTH) * proj(2 * CONV_WIDTH, 3 * CONV_WIDTH)

    @pl.when(s == 0)
    def _():
        ubuf[0:8, :] = jnp.zeros((8, CONV_WIDTH), F32)

    ubuf[8:8 + tm, :] = u
    w = convw_ref[...]
    conv = w[2:3, :] * u + w[1:2, :] * ubuf[7:7 + tm, :] + w[0:1, :] * ubuf[6:6 + tm, :]
    ubuf[0:8, :] = u[tm - 8:tm, :]
    hc = (c_b * conv).astype(BF16)
    y_conv = jnp.dot(hc, woc_ref[...], preferred_element_type=F32)

    o = 3 * CONV_WIDTH
    q_ref[0] = (proj(o, o + ATTN_WIDTH) * (HEAD_DIM ** -0.5)).astype(BF16)
    k_ref[0] = proj(o + ATTN_WIDTH, o + 2 * ATTN_WIDTH).astype(BF16)
    v_ref[0] = proj(o + 2 * ATTN_WIDTH, o + 3 * ATTN_WIDTH).astype(BF16)
    o += 3 * ATTN_WIDTH
    za_ref[0] = (_sigmoid(proj(o, o + D_MODEL)) * y_conv).astype(BF16)
    sgb_ref[0] = _sigmoid(proj(o + D_MODEL, o + 2 * D_MODEL)).astype(BF16)


def _proj_call(x, w_in, conv_w, w_out_conv):
    tok_spec = lambda c: pl.BlockSpec((1, TM_PROJ, c), lambda b, s: (b, s, 0))
    full = lambda shape: pl.BlockSpec(shape, lambda b, s: (0,) * len(shape))
    once = lambda shape: pl.BlockSpec(shape, lambda b, s: (0,) * len(shape), pipeline_mode=pl.Buffered(1))
    return pl.pallas_call(
        _proj_kernel,
        grid=(BATCH, SEQ // TM_PROJ),
        in_specs=[tok_spec(D_MODEL), pl.BlockSpec(memory_space=pl.ANY), full((3, CONV_WIDTH)),
                  once((CONV_WIDTH, D_MODEL))],
        out_specs=[tok_spec(ATTN_WIDTH), tok_spec(ATTN_WIDTH), tok_spec(ATTN_WIDTH),
                   tok_spec(D_MODEL), tok_spec(D_MODEL)],
        out_shape=[jax.ShapeDtypeStruct((BATCH, SEQ, ATTN_WIDTH), BF16)] * 3
        + [jax.ShapeDtypeStruct((BATCH, SEQ, D_MODEL), BF16)] * 2,
        scratch_shapes=[pltpu.VMEM((TM_PROJ + 8, CONV_WIDTH), F32), pltpu.VMEM((D_MODEL, IN_COLS), BF16),
                        pltpu.VMEM((CONV_WIDTH, D_MODEL), BF16),
                        pltpu.VMEM((2, D_MODEL, CONV_WIDTH), F32), pltpu.SemaphoreType.DMA((2,))],
        compiler_params=pltpu.CompilerParams(
            dimension_semantics=("arbitrary", "arbitrary"), vmem_limit_bytes=VMEM_LIMIT),
        name="proj",
    )(x, w_in, conv_w, w_out_conv)


def _attn_kernel(slopes_ref, qa_ref, qb_ref, k_ref, v_ref, oa_ref, ob_ref,
                 kaug_ref, vt_ref, kmean_ref, qaug_ref, pv_ref, mloc_ref, t_ref, p_ref):
    hp = pl.program_id(1)
    j = pl.program_id(2)
    blk = MOBA_BLOCK

    @pl.when(j == 0)
    def _():
        klane = lax.broadcasted_iota(I32, (blk, 128), 1)
        koff = lax.broadcasted_iota(I32, (blk, 128), 0).astype(F32)
        k_extra = jnp.where(klane == 0, koff, jnp.where(klane == 1, 1.0, 0.0)).astype(BF16)
        orow = lax.broadcasted_iota(I32, (PV_ROWS - HEAD_DIM, blk), 0)
        ones_rows = jnp.where(orow == 0, 1.0, 0.0).astype(BF16)
        for n in range(N_BLOCKS):
            kblk = k_ref[0, n * blk:(n + 1) * blk, :]
            kaug_ref[n, :, 0:128] = kblk
            kaug_ref[n, :, 128:256] = k_extra
            kmean_ref[n:n + 1, :] = jnp.mean(kblk.astype(F32), axis=0, keepdims=True)
            v_t = v_ref[0, n * blk:(n + 1) * blk, :].astype(F32).T.astype(BF16)
            for hh in range(2):
                vt_ref[n, hh, 0:HEAD_DIM, :] = v_t[hh * HEAD_DIM:(hh + 1) * HEAD_DIM, :]
                vt_ref[n, hh, HEAD_DIM:PV_ROWS, :] = ones_rows

    lane = lax.broadcasted_iota(I32, (1, 2 * blk), 1)
    slope_row = jnp.where(lane < blk, slopes_ref[2 * hp], slopes_ref[2 * hp + 1])
    qoff_row = jnp.where(lane < blk, lane, lane - blk).astype(F32)
    feat = lax.broadcasted_iota(I32, (2 * HEAD_DIM, blk), 0)
    arow = lax.broadcasted_iota(I32, (2 * HEAD_DIM, 2 * blk), 0)
    q_extra = jnp.where(arow == 0, slope_row, jnp.where(arow == 1, -slope_row * qoff_row, 0.0)).astype(BF16)
    blk_i = lax.broadcasted_iota(I32, (N_BLOCKS, 2 * blk), 0)
    kmean = kmean_ref[...].astype(BF16)
    key_i = lax.broadcasted_iota(I32, (blk, 2 * blk), 0)
    qry_j = lax.broadcasted_iota(I32, (blk, 2 * blk), 1)
    causal = key_i <= jnp.where(qry_j < blk, qry_j, qry_j - blk)

    def prepare(q_ref, slot, qblock):
        q_t = q_ref[0].astype(F32).T
        qcat = jnp.concatenate([jnp.where(feat < HEAD_DIM, q_t, 0.0), jnp.where(feat >= HEAD_DIM, q_t, 0.0)],
                               axis=1).astype(BF16)
        qaug_ref[slot, 0:2 * HEAD_DIM, :] = qcat
        qaug_ref[slot, 2 * HEAD_DIM:4 * HEAD_DIM, :] = q_extra
        gate = jnp.dot(kmean, qcat, preferred_element_type=F32)
        cnt = jnp.zeros((N_BLOCKS, 2 * blk), F32)
        for m in range(N_BLOCKS):
            gm = gate[m:m + 1, :]
            beats = (gm > gate) | ((gm == gate) & (blk_i > m))
            cnt = cnt + jnp.where(beats & (qblock > m), 1.0, 0.0)
        return jnp.where((blk_i < qblock) & (cnt < float(MOBA_TOPK)), 1.0, 0.0)

    qblock_a = j
    qblock_b = N_BLOCKS - 1 - j
    sel_a = prepare(qa_ref, 0, qblock_a)
    sel_b = prepare(qb_ref, 1, qblock_b)

    n_mid = N_BLOCKS - 1
    slots = [(0, 0, qblock_a, True)]
    mids = []
    for s in range(1, n_mid + 1):
        is_a = s <= j
        slots.append((s, jnp.where(is_a, 0, 1), jnp.where(is_a, s - 1, s - 1 - j), False))
        mids.append((is_a, slots[-1][2]))
    slots.append((n_mid + 1, 1, qblock_b, True))

    for s, which, kb, _ in slots:
        t_ref[s] = jnp.dot(kaug_ref[kb], qaug_ref[which], preferred_element_type=F32)
    for s, _, _, own in slots:
        t = t_ref[s]
        if own:
            t = jnp.where(causal, t, NEG_INF)
        m_loc = jnp.max(t, axis=0, keepdims=True)
        p_ref[s] = jnp.exp((t - m_loc).astype(BF16))
        mloc_ref[s:s + 1, :] = m_loc
    for s, _, kb, _ in slots:
        pv_ref[s, 0] = jnp.dot(vt_ref[kb, 0], p_ref[s, :, 0:blk], preferred_element_type=F32)
        pv_ref[s, 1] = jnp.dot(vt_ref[kb, 1], p_ref[s, :, blk:2 * blk], preferred_element_type=F32)

    def combine(o_ref, own_slot, sel, qblock, mine):
        neg = jnp.full((1, 2 * blk), -1e30, F32)
        pieces = [(own_slot, mloc_ref[own_slot:own_slot + 1, :])]
        for s, (is_a, kb) in enumerate(mids, start=1):
            selrow = jnp.sum(jnp.where(blk_i == kb, sel, 0.0), axis=0, keepdims=True)
            belongs = jnp.where(is_a, 1.0, 0.0) if mine else jnp.where(is_a, 0.0, 1.0)
            used = selrow * belongs > 0.5
            shift = slope_row * ((kb - qblock) * blk).astype(F32)
            pieces.append((s, jnp.where(used, mloc_ref[s:s + 1, :] + shift, neg)))
        m_all = pieces[0][1]
        for _, m_s in pieces[1:]:
            m_all = jnp.maximum(m_all, m_s)
        acc = [jnp.zeros((PV_ROWS, blk), F32), jnp.zeros((PV_ROWS, blk), F32)]
        for s, m_s in pieces:
            w = jnp.exp(m_s - m_all)
            for hh in range(2):
                acc[hh] = acc[hh] + pv_ref[s, hh] * w[:, hh * blk:(hh + 1) * blk]
        o_t = jnp.concatenate([a[0:HEAD_DIM, :] / a[HEAD_DIM:HEAD_DIM + 1, :] for a in acc], axis=0)
        o_ref[0] = o_t.T.astype(BF16)

    combine(oa_ref, 0, sel_a, qblock_a, True)
    combine(ob_ref, n_mid + 1, sel_b, qblock_b, False)


def _attn_call(slopes, q, k, v):
    half = N_BLOCKS // 2
    o_a, o_b = pl.pallas_call(
        _attn_kernel,
        grid_spec=pltpu.PrefetchScalarGridSpec(
            num_scalar_prefetch=1,
            grid=(BATCH, N_HEADS // 2, half),
            in_specs=[
                pl.BlockSpec((1, MOBA_BLOCK, 128), lambda b, h, j, sl: (b, j, h)),
                pl.BlockSpec((1, MOBA_BLOCK, 128), lambda b, h, j, sl: (b, N_BLOCKS - 1 - j, h)),
                pl.BlockSpec((1, SEQ, 128), lambda b, h, j, sl: (b, 0, h)),
                pl.BlockSpec((1, SEQ, 128), lambda b, h, j, sl: (b, 0, h)),
            ],
            out_specs=[pl.BlockSpec((1, MOBA_BLOCK, 128), lambda b, h, j, sl: (b, j, h)),
                       pl.BlockSpec((1, MOBA_BLOCK, 128), lambda b, h, j, sl: (b, half - 1 - j, h))],
            scratch_shapes=[
                pltpu.VMEM((N_BLOCKS, MOBA_BLOCK, 256), BF16),
                pltpu.VMEM((N_BLOCKS, 2, PV_ROWS, MOBA_BLOCK), BF16),
                pltpu.VMEM((N_BLOCKS, 128), F32),
                pltpu.VMEM((2, 256, 2 * MOBA_BLOCK), BF16),
                pltpu.VMEM((N_BLOCKS + 1, 2, PV_ROWS, MOBA_BLOCK), F32),
                pltpu.VMEM((16, 2 * MOBA_BLOCK), F32),
                pltpu.VMEM((N_BLOCKS + 1, MOBA_BLOCK, 2 * MOBA_BLOCK), F32),
                pltpu.VMEM((N_BLOCKS + 1, MOBA_BLOCK, 2 * MOBA_BLOCK), BF16),
            ],
        ),
        out_shape=[jax.ShapeDtypeStruct((BATCH, SEQ // 2, ATTN_WIDTH), BF16)] * 2,
        compiler_params=pltpu.CompilerParams(
            dimension_semantics=("arbitrary", "arbitrary", "arbitrary"), vmem_limit_bytes=VMEM_LIMIT),
        name="moba_attn",
    )(slopes, q, q, k, v)
    return o_a, o_b


def _route(logits):
    row8 = lax.broadcasted_iota(I32, (8, TM), 0).astype(F32)
    gl = jnp.where(row8 < float(N_GROUPS), logits[0:8, :], NEG_INF)
    gexp = jnp.exp(gl - jnp.max(gl, axis=0, keepdims=True))
    gprob = gexp / jnp.sum(gexp, axis=0, keepdims=True)
    ptop = jnp.max(gprob, axis=0, keepdims=True)
    gtop = jnp.min(jnp.where(gprob == ptop, row8, 8.0), axis=0, keepdims=True)
    el = logits[8:ROUTER_ROWS, :]
    eg = jnp.where(gtop == 0.0, el[0:8, :],
                   jnp.where(gtop == 1.0, el[8:16, :], jnp.where(gtop == 2.0, el[16:24, :], el[24:32, :])))
    m1 = jnp.max(eg, axis=0, keepdims=True)
    i1 = jnp.min(jnp.where(eg == m1, row8, 8.0), axis=0, keepdims=True)
    eg2 = jnp.where(row8 == i1, NEG_INF, eg)
    m2 = jnp.max(eg2, axis=0, keepdims=True)
    i2 = jnp.min(jnp.where(eg2 == m2, row8, 8.0), axis=0, keepdims=True)
    t2 = jnp.exp(m2 - m1)
    gate1 = ptop * (1.0 / (1.0 + t2))
    gate2 = ptop * (t2 / (1.0 + t2))
    erow = lax.broadcasted_iota(I32, (N_EXPERTS, TM), 0).astype(F32)
    oh1 = jnp.where(erow == gtop * float(EXPERTS_PER_GROUP) + i1, 1.0, 0.0)
    oh2 = jnp.where(erow == gtop * float(EXPERTS_PER_GROUP) + i2, 1.0, 0.0)
    return gate1, gate2, oh1, oh2


def _merge_kernel(oa_ref, ob_ref, za_ref, sgb_ref, x_ref, woa_f32_ref, wo_f32_ref, g_ref, b_ref,
                  wr_hi_ref, wr_lo_ref, x1_ref, xs_ref, rf_ref, mt_ref, woa_ref, wo_ref):
    i = pl.program_id(0)

    @pl.when(i == 0)
    def _():
        woa_ref[...] = woa_f32_ref[...].astype(BF16)
        wo_ref[...] = wo_f32_ref[...].astype(BF16)

    subs = range(MERGE_SUB)
    rows = [slice(s * TM, (s + 1) * TM) for s in subs]
    steps_per_batch = N_BLOCKS // MERGE_SUB
    in_oa = lax.rem(i, steps_per_batch) < steps_per_batch // 2
    o = [jnp.where(in_oa, oa_ref[0, r, :], ob_ref[0, r, :]) for r in rows]
    y_attn = [jnp.dot(o[s], woa_ref[...], preferred_element_type=F32) for s in subs]
    y = [(za_ref[rows[s], :].astype(F32) + sgb_ref[rows[s], :].astype(F32) * y_attn[s]).astype(BF16)
         for s in subs]
    mix = [jnp.dot(y[s], wo_ref[...], preferred_element_type=F32) for s in subs]
    x1 = []
    for s in subs:
        h = ALPHA * x_ref[rows[s], :] + mix[s]
        mu = jnp.mean(h, axis=-1, keepdims=True)
        hc = h - mu
        var = jnp.mean(hc * hc, axis=-1, keepdims=True)
        x1.append(hc * lax.rsqrt(var + LN_EPS) * g_ref[...] + b_ref[...])
        x1_ref[rows[s], :] = x1[s]

    xh = [x1[s].astype(BF16) for s in subs]
    xl = [(x1[s] - xh[s].astype(F32)).astype(BF16) for s in subs]
    wh = wr_hi_ref[...]
    logits = [(jnp.dot(xh[s], wh, preferred_element_type=F32)
               + jnp.dot(xl[s], wh, preferred_element_type=F32)
               + jnp.dot(xh[s], wr_lo_ref[...], preferred_element_type=F32)).T for s in subs]
    routes = [_route(logits[s]) for s in subs]

    ta = lax.broadcasted_iota(I32, (TM, TM), 0)
    tb = lax.broadcasted_iota(I32, (TM, TM), 1)
    upper = jnp.where(ta < tb, 1.0, 0.0).astype(BF16)
    ea = lax.broadcasted_iota(I32, (N_EXPERTS, N_EXPERTS), 0)
    eb = lax.broadcasted_iota(I32, (N_EXPERTS, N_EXPERTS), 1)
    lower = jnp.where(eb < ea, 1.0, 0.0).astype(BF16)
    lrow = lax.broadcasted_iota(I32, (LOCAL_ROWS, TM), 0).astype(F32)
    zero = jnp.zeros((1, TM), F32)
    cum = [jnp.dot((routes[s][2] + routes[s][3]).astype(BF16), upper, preferred_element_type=F32) for s in subs]
    perm = []
    for s in subs:
        gate1, gate2, oh1, oh2 = routes[s]
        n_e = jnp.sum(oh1 + oh2, axis=1, keepdims=True)
        m_rep = jnp.broadcast_to(jnp.floor((n_e + float(GRAN - 1)) * (1.0 / GRAN)), (N_EXPERTS, 128))
        run_start = jnp.dot(lower, m_rep.astype(BF16), preferred_element_type=F32)
        tot = cum[s] + float(GRAN) * run_start[:, 0:1]
        lp1 = jnp.sum(oh1 * tot, axis=0, keepdims=True)
        lp2 = jnp.sum(oh2 * tot, axis=0, keepdims=True)
        perm.append(jnp.where((lrow == lp1) | (lrow == lp2), 1.0, 0.0).astype(BF16))
        rf_ref[:, rows[s]] = jnp.concatenate([gate1, gate2, lp1, lp2, zero, zero, zero, zero], axis=0)
        mt_ref[s * N_EXPERTS:(s + 1) * N_EXPERTS, :] = m_rep
    for s in subs:
        xs_ref[s * LOCAL_ROWS:(s + 1) * LOCAL_ROWS, :] = jnp.dot(
            perm[s], xh[s], preferred_element_type=F32).astype(BF16)


def _merge_call(o_a, o_b, za, sgb, x, woa, wo, g, b, wr_hi, wr_lo):
    tm = MERGE_SUB * TM
    tok = lambda c: pl.BlockSpec((tm, c), lambda i: (i, 0))
    full = lambda shape: pl.BlockSpec(shape, lambda i: (0,) * len(shape))
    per_batch = SEQ // tm
    half = per_batch // 2
    o_a_spec = pl.BlockSpec((1, tm, ATTN_WIDTH), lambda i: (i // per_batch, jnp.minimum(i % per_batch, half - 1), 0))
    o_b_spec = pl.BlockSpec((1, tm, ATTN_WIDTH), lambda i: (i // per_batch, jnp.maximum(i % per_batch - half, 0), 0))
    return pl.pallas_call(
        _merge_kernel,
        grid=(TOKENS // tm,),
        in_specs=[o_a_spec, o_b_spec, tok(D_MODEL), tok(D_MODEL), tok(D_MODEL),
                  full((ATTN_WIDTH, D_MODEL)), full((D_MODEL, D_MODEL)), full((1, D_MODEL)),
                  full((1, D_MODEL)), full((D_MODEL, 128)), full((D_MODEL, 128))],
        out_specs=[tok(D_MODEL), pl.BlockSpec((MERGE_SUB * LOCAL_ROWS, D_MODEL), lambda i: (i, 0)),
                   pl.BlockSpec((8, tm), lambda i: (0, i)),
                   pl.BlockSpec((MERGE_SUB * N_EXPERTS, 128), lambda i: (i, 0))],
        out_shape=[jax.ShapeDtypeStruct((TOKENS, D_MODEL), F32),
                   jax.ShapeDtypeStruct((N_TOK_TILES * LOCAL_ROWS, D_MODEL), BF16),
                   jax.ShapeDtypeStruct((8, TOKENS), F32),
                   jax.ShapeDtypeStruct((N_TOK_TILES * N_EXPERTS, 128), F32)],
        scratch_shapes=[pltpu.VMEM((ATTN_WIDTH, D_MODEL), BF16), pltpu.VMEM((D_MODEL, D_MODEL), BF16)],
        compiler_params=pltpu.CompilerParams(
            dimension_semantics=("arbitrary",), vmem_limit_bytes=VMEM_LIMIT),
        name="merge_ln1_route",
    )(o_a, o_b, za, sgb, x, woa, wo, g, b, wr_hi, wr_lo)


def _granule_copy(src_ref, src_gran, dst_ref, dst_gran, sem):
    src = pl.multiple_of(src_gran * GRAN, GRAN)
    dst = pl.multiple_of(dst_gran * GRAN, GRAN)
    return pltpu.make_async_copy(src_ref.at[pl.ds(src, GRAN), :], dst_ref.at[pl.ds(dst, GRAN), :], sem)


def _expert_kernel(te_ref, nt_ref, gsrc_ref, gdst_ref, ug_ref, xs_ref, *refs):
    w_refs = refs[:3 * EXP_SUB]
    ys_ref, xbuf, ybuf, zbuf, in_sem, out_sem, zero_sem = refs[3 * EXP_SUB:]
    j = pl.program_id(0)
    n_tiles = nt_ref[0]
    slot = lax.rem(j, NBUF)

    def tile_gather(step, s):
        for g in range(STEP_GRANS):
            _granule_copy(xs_ref, gsrc_ref[step * STEP_GRANS + g], xbuf.at[s], g,
                          in_sem.at[s]).start(priority=g % 2)

    @pl.when(j == 0)
    def _():
        tile_gather(0, 0)
        ybuf[NBUF - 1] = jnp.zeros((STEP_ROWS, D_MODEL), BF16)
        zbuf[...] = jnp.zeros((GRAN, D_MODEL), BF16)
        for part in range(NBUF):
            spare = pltpu.make_async_copy(
                ybuf.at[NBUF - 1], ys_ref.at[pl.ds((SPARE_GRAN + part * STEP_GRANS) * GRAN, STEP_ROWS), :],
                out_sem.at[NBUF - 1])
            spare.start()
            spare.wait()

        for ahead in range(1, NBUF - 1):
            tile_gather(ahead, ahead)

    def zero_copy(t, g):
        return _granule_copy(zbuf, 0, ys_ref, t * LOCAL_GRANS + g, zero_sem)

    @pl.when(jnp.logical_and(j >= 1, j <= N_TOK_TILES))
    def _():
        def wait(g, c):
            zero_copy(j - 1, g).wait()
            return c

        lax.fori_loop(ug_ref[j - 1], LOCAL_GRANS, wait, 0)

    @pl.when(j < N_TOK_TILES)
    def _():
        def start(g, c):
            zero_copy(j, g).start()
            return c

        lax.fori_loop(ug_ref[j], LOCAL_GRANS, start, 0)

    def gather_wait():
        pltpu.make_async_copy(xs_ref.at[pl.ds(0, STEP_ROWS), :], xbuf.at[slot], in_sem.at[slot]).wait()

    @pl.when(jnp.logical_and(j >= NBUF, j - NBUF < n_tiles))
    def _():
        pltpu.make_async_copy(ybuf.at[slot], ys_ref.at[pl.ds(0, STEP_ROWS), :], out_sem.at[slot]).wait()

    @pl.when(jnp.logical_and(j >= n_tiles, j < n_tiles + NBUF - 1))
    def _():
        gather_wait()

    @pl.when(j < n_tiles)
    def _():
        gather_wait()
        wg, wu, wd = ([w_refs[3 * s + k][0].astype(BF16) for s in range(EXP_SUB)] for k in range(3))
        chains = [(s, s * TE + r) for s in range(EXP_SUB) for r in range(0, TE, CHAIN_ROWS)]
        xb = [xbuf[slot, r:r + CHAIN_ROWS, :] for _, r in chains]
        hg = [jnp.dot(x, wg[s], preferred_element_type=F32) for x, (s, _) in zip(xb, chains)]
        hu = [jnp.dot(x, wu[s], preferred_element_type=F32) for x, (s, _) in zip(xb, chains)]
        tile_gather(j + NBUF - 1, lax.rem(j + NBUF - 1, NBUF))
        h = [(a * _sigmoid(a) * b).astype(BF16) for a, b in zip(hg, hu)]
        for hc, (s, r) in zip(h, chains):
            ybuf[slot, r:r + CHAIN_ROWS, :] = jnp.dot(hc, wd[s], preferred_element_type=F32).astype(BF16)
        for g in range(STEP_GRANS):
            _granule_copy(ybuf.at[slot], g, ys_ref, gdst_ref[j * STEP_GRANS + g],
                          out_sem.at[slot]).start(priority=g % 2)


def _expert_call(tile_expert, n_steps, gsrc, gdst, used_grans, xs, wg, wu, wd):
    w_specs, w_args = [], []
    for s in range(EXP_SUB):
        wsel = lambda j, te, nt, gs, gd, ug, s=s: (te[EXP_SUB * j + s], 0, 0)
        w_specs += [pl.BlockSpec((1, D_MODEL, D_EXPERT), wsel), pl.BlockSpec((1, D_MODEL, D_EXPERT), wsel),
                    pl.BlockSpec((1, D_EXPERT, D_MODEL), wsel)]
        w_args += [wg, wu, wd]
    return pl.pallas_call(
        _expert_kernel,
        grid_spec=pltpu.PrefetchScalarGridSpec(
            num_scalar_prefetch=5,
            grid=(MAX_STEPS + NBUF,),
            in_specs=[pl.BlockSpec(memory_space=pl.ANY)] + w_specs,
            out_specs=pl.BlockSpec(memory_space=pl.ANY),
            scratch_shapes=[pltpu.VMEM((NBUF, STEP_ROWS, D_MODEL), BF16),
                            pltpu.VMEM((NBUF, STEP_ROWS, D_MODEL), BF16),
                            pltpu.VMEM((GRAN, D_MODEL), BF16),
                            pltpu.SemaphoreType.DMA((NBUF,)), pltpu.SemaphoreType.DMA((NBUF,)),
                            pltpu.SemaphoreType.DMA],
        ),
        out_shape=jax.ShapeDtypeStruct(((SPARE_GRAN + NBUF * STEP_GRANS) * GRAN, D_MODEL), BF16),
        compiler_params=pltpu.CompilerParams(
            dimension_semantics=("arbitrary",), vmem_limit_bytes=VMEM_LIMIT),
        name="experts",
    )(tile_expert, n_steps, gsrc, gdst, used_grans, xs, *w_args)


def _combine_kernel(ys_ref, x1_ref, rf_ref, g_ref, b_ref, out_ref):
    subs = range(MERGE_SUB)
    col = lax.broadcasted_iota(I32, (TM, LOCAL_ROWS), 1).astype(F32)
    route = [rf_ref[:, s * TM:(s + 1) * TM].T for s in subs]
    unsort = [(jnp.where(col == r[:, 2:3], r[:, 0:1], 0.0)
               + jnp.where(col == r[:, 3:4], r[:, 1:2], 0.0)).astype(BF16) for r in route]
    ffn = [jnp.dot(unsort[s], ys_ref[s * LOCAL_ROWS:(s + 1) * LOCAL_ROWS, :], preferred_element_type=F32)
           for s in subs]
    for s in subs:
        h = ALPHA * x1_ref[s * TM:(s + 1) * TM, :] + ffn[s]
        mu = jnp.mean(h, axis=-1, keepdims=True)
        hc = h - mu
        var = jnp.mean(hc * hc, axis=-1, keepdims=True)
        out_ref[s * TM:(s + 1) * TM, :] = hc * lax.rsqrt(var + LN_EPS) * g_ref[...] + b_ref[...]


def _combine_call(ys, x1, rf, g, b):
    tm = MERGE_SUB * TM
    return pl.pallas_call(
        _combine_kernel,
        grid=(TOKENS // tm,),
        in_specs=[pl.BlockSpec((MERGE_SUB * LOCAL_ROWS, D_MODEL), lambda i: (i, 0)),
                  pl.BlockSpec((tm, D_MODEL), lambda i: (i, 0)),
                  pl.BlockSpec((8, tm), lambda i: (0, i)),
                  pl.BlockSpec((1, D_MODEL), lambda i: (0, 0)),
                  pl.BlockSpec((1, D_MODEL), lambda i: (0, 0))],
        out_specs=pl.BlockSpec((tm, D_MODEL), lambda i: (i, 0)),
        out_shape=jax.ShapeDtypeStruct((TOKENS, D_MODEL), F32),
        compiler_params=pltpu.CompilerParams(
            dimension_semantics=("arbitrary",), vmem_limit_bytes=VMEM_LIMIT),
        name="combine_ln2",
    )(ys, x1, rf, g, b)


def _router_cols(w_router_group, w_router_expert):
    w = jnp.concatenate([w_router_group, jnp.zeros((D_MODEL, 4), F32), w_router_expert,
                         jnp.zeros((D_MODEL, 128 - ROUTER_ROWS), F32)], axis=1)
    hi = w.astype(BF16)
    lo = (w - hi.astype(F32)).astype(BF16)
    return hi, lo


def _layer(x, w_in, conv_w, w_out_conv, w_out_attn, w_o, ln1_g, ln1_b,
           w_router_group, w_router_expert, w_gate, w_up, w_down, ln2_g, ln2_b):
    slopes = jnp.asarray([2.0 ** (-8.0 * (h + 1) / N_HEADS) for h in range(N_HEADS)], F32)
    q, k, v, za, sgb = _proj_call(x, w_in, conv_w, w_out_conv)
    o_a, o_b = _attn_call(slopes, q, k, v)

    wr_hi, wr_lo = _router_cols(w_router_group, w_router_expert)
    x1, xs, rf, mt = _merge_call(
        o_a, o_b, za.reshape(TOKENS, D_MODEL), sgb.reshape(TOKENS, D_MODEL),
        x.reshape(TOKENS, D_MODEL), w_out_attn, w_o,
        ln1_g.reshape(1, D_MODEL), ln1_b.reshape(1, D_MODEL), wr_hi, wr_lo)

    grans = mt.reshape(N_TOK_TILES, N_EXPERTS, 128)[:, :, 0].astype(I32)
    local_start = jnp.cumsum(grans, axis=1) - grans
    grans_t = grans.T
    tiles_e = (jnp.sum(grans_t, axis=1) + TILE_GRANS - 1) // TILE_GRANS
    tile_end = jnp.cumsum(tiles_e)
    n_steps = ((tile_end[-1] + EXP_SUB - 1) // EXP_SUB).reshape(1)
    all_tiles = (MAX_STEPS + NBUF) * EXP_SUB
    tile_ids = jnp.arange(all_tiles, dtype=I32)
    tile_expert = jnp.minimum(
        jnp.sum((tile_ids[:, None] >= tile_end[None, :]).astype(I32), axis=1), N_EXPERTS - 1)
    run_slot = TILE_GRANS * (tile_end - tiles_e)[:, None] + jnp.cumsum(grans_t, axis=1) - grans_t
    run_src = jnp.arange(N_TOK_TILES, dtype=I32)[None, :] * LOCAL_GRANS + local_start.T
    pick = (tile_expert[:all_tiles, None] == jnp.arange(N_EXPERTS, dtype=I32)[None, :])[:, :, None]
    t_slot = jnp.sum(jnp.where(pick, run_slot[None], 0), axis=1)
    t_len = jnp.sum(jnp.where(pick, grans_t[None], 0), axis=1)
    t_src = jnp.sum(jnp.where(pick, run_src[None], 0), axis=1)
    slots = jnp.arange(all_tiles * TILE_GRANS, dtype=I32).reshape(all_tiles, TILE_GRANS)
    k = slots[:, :, None] - t_slot[:, None, :]
    hit = (k >= 0) & (k < t_len[:, None, :])
    gran = jnp.sum(jnp.where(hit, t_src[:, None, :] + k, 0), axis=2).reshape(-1)
    filled = (jnp.sum(hit.astype(I32), axis=2) > 0).reshape(-1)
    slots = slots.reshape(-1)
    gsrc = jnp.where(filled, gran, 0)
    gdst = jnp.where(filled, gran, SPARE_GRAN + slots % (NBUF * STEP_GRANS))

    ys = _expert_call(tile_expert, n_steps, gsrc, gdst, jnp.sum(grans, axis=1), xs, w_gate, w_up, w_down)
    out = _combine_call(ys, x1, rf, ln2_g.reshape(1, D_MODEL), ln2_b.reshape(1, D_MODEL))
    return out.reshape(BATCH, SEQ, D_MODEL)


def kernel(x, w_in, conv_w, w_out_conv, w_out_attn, w_o, ln1_g, ln1_b, w_router_group, w_router_expert, w_gate, w_up, w_down, ln2_g, ln2_b):
    depth = w_in.shape[0]
    for l in range(depth):
        x = _layer(x, w_in[l], conv_w[l], w_out_conv[l], w_out_attn[l], w_o[l], ln1_g[l], ln1_b[l],
                   w_router_group[l], w_router_expert[l], w_gate[l], w_up[l], w_down[l], ln2_g[l], ln2_b[l])
    return x
```

```python
import functools

import jax
import jax.numpy as jnp
from jax import lax
from jax.experimental import pallas as pl
from jax.experimental.pallas import tpu as pltpu

F32 = jnp.float32
BF16 = jnp.bfloat16
U32 = jnp.uint32
I32 = jnp.int32

D_MODEL = 1024
BATCH = 8
SEQ = 2048
TOKENS = BATCH * SEQ
CONV_WIDTH = 512
N_HEADS = 8
HEAD_DIM = 64
ATTN_WIDTH = N_HEADS * HEAD_DIM
MOBA_BLOCK = 256
N_BLOCKS = SEQ // MOBA_BLOCK
MOBA_TOPK = 3
N_GROUPS = 4
EXPERTS_PER_GROUP = 8
N_EXPERTS = N_GROUPS * EXPERTS_PER_GROUP
D_EXPERT = 256
LN_EPS = 1e-5
ALPHA = 2.0 ** 0.25
IN_COLS = 3 * CONV_WIDTH + 3 * ATTN_WIDTH + 2 * D_MODEL
HALF = D_MODEL // 2

TM = 256
TM_PROJ = 512
TE = 512
CHAIN_ROWS = 256
PV_ROWS = HEAD_DIM + 16
GRAN = 16
TILE_GRANS = TE // GRAN
N_TOK_TILES = TOKENS // TM
LOCAL_ROWS = -(-(2 * TM + N_EXPERTS * (GRAN - 1)) // 256) * 256
LOCAL_GRANS = LOCAL_ROWS // GRAN
SPARE_GRAN = N_TOK_TILES * LOCAL_GRANS
MAX_TILES = (2 * TOKENS + N_TOK_TILES * N_EXPERTS * (GRAN - 1)) // TE + N_EXPERTS
PAIRS = 2
MERGE_SUB = 2
EXP_SUB = 1
STEP_ROWS = EXP_SUB * TE
STEP_GRANS = EXP_SUB * TILE_GRANS
MAX_STEPS = -(-MAX_TILES // EXP_SUB)
NBUF = 4
ROUTER_ROWS = 40
VMEM_LIMIT = 56 * 1024 * 1024
NEG_INF = float("-inf")


def _sigmoid(z):
    return 1.0 / (1.0 + jnp.exp(-z))


def _proj_kernel(x_ref, w_in_hbm_ref, convw_ref, woc_f32_ref, q_ref, k_ref, v_ref, za_ref, sgb_ref,
                 ubuf, w_in_ref, woc_ref, stage, stage_sem):
    s = pl.program_id(1)
    tm = TM_PROJ

    @pl.when((pl.program_id(0) == 0) & (s == 0))
    def _():
        def chunk(c):
            return pltpu.make_async_copy(w_in_hbm_ref.at[:, c * CONV_WIDTH:(c + 1) * CONV_WIDTH],
                                         stage.at[c % 2], stage_sem.at[c % 2])

        n_chunks = IN_COLS // CONV_WIDTH
        chunk(0).start()
        for c in range(n_chunks):
            if c + 1 < n_chunks:
                chunk(c + 1).start()
            chunk(c).wait()
            w_in_ref[:, c * CONV_WIDTH:(c + 1) * CONV_WIDTH] = stage[c % 2].astype(BF16)
        woc_ref[...] = woc_f32_ref[...].astype(BF16)

    xb = x_ref[0].astype(BF16)

    def proj(c0, c1):
        return jnp.dot(xb, w_in_ref[:, c0:c1], preferred_element_type=F32)

    c_b = proj(0, CONV_WIDTH)
    u = proj(CONV_WIDTH, 2 * CONV_WIDTH) * proj(2 * CONV_WIDTH, 3 * CONV_WIDTH)

    @pl.when(s == 0)
    def _():
        ubuf[0:8, :] = jnp.zeros((8, CONV_WIDTH), F32)

    ubuf[8:8 + tm, :] = u
    w = convw_ref[...]
    conv = w[2:3, :] * u + w[1:2, :] * ubuf[7:7 + tm, :] + w[0:1, :] * ubuf[6:6 + tm, :]
    ubuf[0:8, :] = u[tm - 8:tm, :]
    hc = (c_b * conv).astype(BF16)
    y_conv = jnp.dot(hc, woc_ref[...], preferred_element_type=F32)

    o = 3 * CONV_WIDTH
    q_ref[0] = (proj(o, o + ATTN_WIDTH) * (HEAD_DIM ** -0.5)).astype(BF16)
    k_ref[0] = proj(o + ATTN_WIDTH, o + 2 * ATTN_WIDTH).astype(BF16)
    v_ref[0] = proj(o + 2 * ATTN_WIDTH, o + 3 * ATTN_WIDTH).astype(BF16)
    o += 3 * ATTN_WIDTH
    za_ref[0] = (_sigmoid(proj(o, o + D_MODEL)) * y_conv).astype(BF16)
    sgb_ref[0] = _sigmoid(proj(o + D_MODEL, o + 2 * D_MODEL)).astype(BF16)


def _proj_call(x, w_in, conv_w, w_out_conv):
    tok_spec = lambda c: pl.BlockSpec((1, TM_PROJ, c), lambda b, s: (b, s, 0))
    full = lambda shape: pl.BlockSpec(shape, lambda b, s: (0,) * len(shape))
    once = lambda shape: pl.BlockSpec(shape, lambda b, s: (0,) * len(shape), pipeline_mode=pl.Buffered(1))
    return pl.pallas_call(
        _proj_kernel,
        grid=(BATCH, SEQ // TM_PROJ),
        in_specs=[tok_spec(D_MODEL), pl.BlockSpec(memory_space=pl.ANY), full((3, CONV_WIDTH)),
                  once((CONV_WIDTH, D_MODEL))],
        out_specs=[tok_spec(ATTN_WIDTH), tok_spec(ATTN_WIDTH), tok_spec(ATTN_WIDTH),
                   tok_spec(D_MODEL), tok_spec(D_MODEL)],
        out_shape=[jax.ShapeDtypeStruct((BATCH, SEQ, ATTN_WIDTH), BF16)] * 3
        + [jax.ShapeDtypeStruct((BATCH, SEQ, D_MODEL), BF16)] * 2,
        scratch_shapes=[pltpu.VMEM((TM_PROJ + 8, CONV_WIDTH), F32), pltpu.VMEM((D_MODEL, IN_COLS), BF16),
                        pltpu.VMEM((CONV_WIDTH, D_MODEL), BF16),
                        pltpu.VMEM((2, D_MODEL, CONV_WIDTH), F32), pltpu.SemaphoreType.DMA((2,))],
        compiler_params=pltpu.CompilerParams(
            dimension_semantics=("arbitrary", "arbitrary"), vmem_limit_bytes=VMEM_LIMIT),
        name="proj",
    )(x, w_in, conv_w, w_out_conv)


def _attn_kernel(slopes_ref, qa_ref, qb_ref, k_ref, v_ref, oa_ref, ob_ref,
                 kaug_ref, vt_ref, kmean_ref, qaug_ref, pv_ref, mloc_ref, t_ref, p_ref):
    hq = pl.program_id(1)
    j = pl.program_id(2)
    blk = MOBA_BLOCK
    pairs = range(PAIRS)
    lanes = [slice(128 * pp, 128 * (pp + 1)) for pp in pairs]

    @pl.when(j == 0)
    def _():
        klane = lax.broadcasted_iota(I32, (blk, 128), 1)
        koff = lax.broadcasted_iota(I32, (blk, 128), 0).astype(F32)
        k_extra = jnp.where(klane == 0, koff, jnp.where(klane == 1, 1.0, 0.0)).astype(BF16)
        orow = lax.broadcasted_iota(I32, (PV_ROWS - HEAD_DIM, blk), 0)
        ones_rows = jnp.where(orow == 0, 1.0, 0.0).astype(BF16)
        for pp in pairs:
            for n in range(N_BLOCKS):
                kblk = k_ref[0, n * blk:(n + 1) * blk, lanes[pp]]
                kaug_ref[pp, n, :, 0:128] = kblk
                kaug_ref[pp, n, :, 128:256] = k_extra
                kmean_ref[pp, n:n + 1, :] = jnp.mean(kblk.astype(F32), axis=0, keepdims=True)
                v_t = v_ref[0, n * blk:(n + 1) * blk, lanes[pp]].astype(F32).T.astype(BF16)
                for hh in range(2):
                    vt_ref[pp, n, hh, 0:HEAD_DIM, :] = v_t[hh * HEAD_DIM:(hh + 1) * HEAD_DIM, :]
                    vt_ref[pp, n, hh, HEAD_DIM:PV_ROWS, :] = ones_rows

    lane = lax.broadcasted_iota(I32, (1, 2 * blk), 1)
    qoff_row = jnp.where(lane < blk, lane, lane - blk).astype(F32)
    feat = lax.broadcasted_iota(I32, (2 * HEAD_DIM, blk), 0)
    arow = lax.broadcasted_iota(I32, (2 * HEAD_DIM, 2 * blk), 0)
    blk_i = lax.broadcasted_iota(I32, (N_BLOCKS, 2 * blk), 0)
    key_i = lax.broadcasted_iota(I32, (blk, 2 * blk), 0)
    qry_j = lax.broadcasted_iota(I32, (blk, 2 * blk), 1)
    causal = key_i <= jnp.where(qry_j < blk, qry_j, qry_j - blk)
    slope_rows, q_extras = [], []
    for pp in pairs:
        head = 2 * (PAIRS * hq + pp)
        slope_rows.append(jnp.where(lane < blk, slopes_ref[head], slopes_ref[head + 1]))
        q_extras.append(jnp.where(arow == 0, slope_rows[pp],
                                  jnp.where(arow == 1, -slope_rows[pp] * qoff_row, 0.0)).astype(BF16))

    def prepare(q_ref, pp, slot, qblock):
        q_t = q_ref[0, :, lanes[pp]].astype(F32).T
        qcat = jnp.concatenate([jnp.where(feat < HEAD_DIM, q_t, 0.0), jnp.where(feat >= HEAD_DIM, q_t, 0.0)],
                               axis=1).astype(BF16)
        qaug_ref[pp, slot, 0:2 * HEAD_DIM, :] = qcat
        qaug_ref[pp, slot, 2 * HEAD_DIM:4 * HEAD_DIM, :] = q_extras[pp]
        gate = jnp.dot(kmean_ref[pp].astype(BF16), qcat, preferred_element_type=F32)
        cnt = jnp.zeros((N_BLOCKS, 2 * blk), F32)
        for m in range(N_BLOCKS):
            gm = gate[m:m + 1, :]
            beats = (gm > gate) | ((gm == gate) & (blk_i > m))
            cnt = cnt + jnp.where(beats & (qblock > m), 1.0, 0.0)
        return jnp.where((blk_i < qblock) & (cnt < float(MOBA_TOPK)), 1.0, 0.0)

    qblock_a = j
    qblock_b = N_BLOCKS - 1 - j
    sel_a = [prepare(qa_ref, pp, 0, qblock_a) for pp in pairs]
    sel_b = [prepare(qb_ref, pp, 1, qblock_b) for pp in pairs]

    n_mid = N_BLOCKS - 1
    slots = [(0, 0, qblock_a, True)]
    mids = []
    for s in range(1, n_mid + 1):
        is_a = s <= j
        slots.append((s, jnp.where(is_a, 0, 1), jnp.where(is_a, s - 1, s - 1 - j), False))
        mids.append((is_a, slots[-1][2]))
    slots.append((n_mid + 1, 1, qblock_b, True))

    for pp in pairs:
        for s, which, kb, _ in slots:
            t_ref[pp, s] = jnp.dot(kaug_ref[pp, kb], qaug_ref[pp, which], preferred_element_type=F32)
    for pp in pairs:
        for s, _, _, own in slots:
            t = t_ref[pp, s]
            if own:
                t = jnp.where(causal, t, NEG_INF)
            m_loc = jnp.max(t, axis=0, keepdims=True)
            p_ref[pp, s] = jnp.exp((t - m_loc).astype(BF16))
            mloc_ref[pp, s:s + 1, :] = m_loc
    for pp in pairs:
        for s, _, kb, _ in slots:
            pv_ref[pp, s, 0] = jnp.dot(vt_ref[pp, kb, 0], p_ref[pp, s, :, 0:blk], preferred_element_type=F32)
            pv_ref[pp, s, 1] = jnp.dot(vt_ref[pp, kb, 1], p_ref[pp, s, :, blk:2 * blk],
                                       preferred_element_type=F32)

    def combine(o_ref, pp, own_slot, sel, qblock, mine):
        neg = jnp.full((1, 2 * blk), -1e30, F32)
        pieces = [(own_slot, mloc_ref[pp, own_slot:own_slot + 1, :])]
        for s, (is_a, kb) in enumerate(mids, start=1):
            selrow = jnp.sum(jnp.where(blk_i == kb, sel, 0.0), axis=0, keepdims=True)
            belongs = jnp.where(is_a, 1.0, 0.0) if mine else jnp.where(is_a, 0.0, 1.0)
            used = selrow * belongs > 0.5
            shift = slope_rows[pp] * ((kb - qblock) * blk).astype(F32)
            pieces.append((s, jnp.where(used, mloc_ref[pp, s:s + 1, :] + shift, neg)))
        m_all = pieces[0][1]
        for _, m_s in pieces[1:]:
            m_all = jnp.maximum(m_all, m_s)
        acc = [jnp.zeros((PV_ROWS, blk), F32), jnp.zeros((PV_ROWS, blk), F32)]
        for s, m_s in pieces:
            w = jnp.exp(m_s - m_all)
            for hh in range(2):
                acc[hh] = acc[hh] + pv_ref[pp, s, hh] * w[:, hh * blk:(hh + 1) * blk]
        o_t = jnp.concatenate([a[0:HEAD_DIM, :] / a[HEAD_DIM:HEAD_DIM + 1, :] for a in acc], axis=0)
        o_ref[0, :, lanes[pp]] = o_t.T.astype(BF16)

    for pp in pairs:
        combine(oa_ref, pp, 0, sel_a[pp], qblock_a, True)
        combine(ob_ref, pp, n_mid + 1, sel_b[pp], qblock_b, False)


def _attn_call(slopes, q, k, v):
    half = N_BLOCKS // 2
    width = 128 * PAIRS
    o_a, o_b = pl.pallas_call(
        _attn_kernel,
        grid_spec=pltpu.PrefetchScalarGridSpec(
            num_scalar_prefetch=1,
            grid=(BATCH, N_HEADS // (2 * PAIRS), half),
            in_specs=[
                pl.BlockSpec((1, MOBA_BLOCK, width), lambda b, h, j, sl: (b, j, h)),
                pl.BlockSpec((1, MOBA_BLOCK, width), lambda b, h, j, sl: (b, N_BLOCKS - 1 - j, h)),
                pl.BlockSpec((1, SEQ, width), lambda b, h, j, sl: (b, 0, h)),
                pl.BlockSpec((1, SEQ, width), lambda b, h, j, sl: (b, 0, h)),
            ],
            out_specs=[pl.BlockSpec((1, MOBA_BLOCK, width), lambda b, h, j, sl: (b, j, h)),
                       pl.BlockSpec((1, MOBA_BLOCK, width), lambda b, h, j, sl: (b, half - 1 - j, h))],
            scratch_shapes=[
                pltpu.VMEM((PAIRS, N_BLOCKS, MOBA_BLOCK, 256), BF16),
                pltpu.VMEM((PAIRS, N_BLOCKS, 2, PV_ROWS, MOBA_BLOCK), BF16),
                pltpu.VMEM((PAIRS, N_BLOCKS, 128), F32),
                pltpu.VMEM((PAIRS, 2, 256, 2 * MOBA_BLOCK), BF16),
                pltpu.VMEM((PAIRS, N_BLOCKS + 1, 2, PV_ROWS, MOBA_BLOCK), F32),
                pltpu.VMEM((PAIRS, 16, 2 * MOBA_BLOCK), F32),
                pltpu.VMEM((PAIRS, N_BLOCKS + 1, MOBA_BLOCK, 2 * MOBA_BLOCK), F32),
                pltpu.VMEM((PAIRS, N_BLOCKS + 1, MOBA_BLOCK, 2 * MOBA_BLOCK), BF16),
            ],
        ),
        out_shape=[jax.ShapeDtypeStruct((BATCH, SEQ // 2, ATTN_WIDTH), BF16)] * 2,
        compiler_params=pltpu.CompilerParams(
            dimension_semantics=("arbitrary", "arbitrary", "arbitrary"), vmem_limit_bytes=VMEM_LIMIT),
        name="moba_attn",
    )(slopes, q, q, k, v)
    return o_a, o_b


def _route(logits):
    row8 = lax.broadcasted_iota(I32, (8, TM), 0).astype(F32)
    gl = jnp.where(row8 < float(N_GROUPS), logits[0:8, :], NEG_INF)
    gexp = jnp.exp(gl - jnp.max(gl, axis=0, keepdims=True))
    gprob = gexp / jnp.sum(gexp, axis=0, keepdims=True)
    ptop = jnp.max(gprob, axis=0, keepdims=True)
    gtop = jnp.min(jnp.where(gprob == ptop, row8, 8.0), axis=0, keepdims=True)
    el = logits[8:ROUTER_ROWS, :]
    eg = jnp.where(gtop == 0.0, el[0:8, :],
                   jnp.where(gtop == 1.0, el[8:16, :], jnp.where(gtop == 2.0, el[16:24, :], el[24:32, :])))
    m1 = jnp.max(eg, axis=0, keepdims=True)
    i1 = jnp.min(jnp.where(eg == m1, row8, 8.0), axis=0, keepdims=True)
    eg2 = jnp.where(row8 == i1, NEG_INF, eg)
    m2 = jnp.max(eg2, axis=0, keepdims=True)
    i2 = jnp.min(jnp.where(eg2 == m2, row8, 8.0), axis=0, keepdims=True)
    t2 = jnp.exp(m2 - m1)
    gate1 = ptop * (1.0 / (1.0 + t2))
    gate2 = ptop * (t2 / (1.0 + t2))
    erow = lax.broadcasted_iota(I32, (N_EXPERTS, TM), 0).astype(F32)
    oh1 = jnp.where(erow == gtop * float(EXPERTS_PER_GROUP) + i1, 1.0, 0.0)
    oh2 = jnp.where(erow == gtop * float(EXPERTS_PER_GROUP) + i2, 1.0, 0.0)
    return gate1, gate2, oh1, oh2


def _merge_kernel(oa_ref, ob_ref, za_ref, sgb_ref, x_ref, woa_f32_ref, wo_f32_ref, g_ref, b_ref,
                  wr_hi_ref, wr_lo_ref, x1_ref, xs_ref, rf_ref, mt_ref, woa_ref, wo_ref):
    i = pl.program_id(0)

    @pl.when(i == 0)
    def _():
        woa_ref[...] = woa_f32_ref[...].astype(BF16)
        wo_ref[...] = wo_f32_ref[...].astype(BF16)

    subs = range(MERGE_SUB)
    rows = [slice(s * TM, (s + 1) * TM) for s in subs]
    steps_per_batch = N_BLOCKS // MERGE_SUB
    in_oa = lax.rem(i, steps_per_batch) < steps_per_batch // 2
    o = [jnp.where(in_oa, oa_ref[0, r, :], ob_ref[0, r, :]) for r in rows]
    y_attn = [jnp.dot(o[s], woa_ref[...], preferred_element_type=F32) for s in subs]
    y = [(za_ref[rows[s], :].astype(F32) + sgb_ref[rows[s], :].astype(F32) * y_attn[s]).astype(BF16)
         for s in subs]
    mix = [jnp.dot(y[s], wo_ref[...], preferred_element_type=F32) for s in subs]
    x1 = []
    for s in subs:
        h = ALPHA * x_ref[rows[s], :] + mix[s]
        mu = jnp.mean(h, axis=-1, keepdims=True)
        hc = h - mu
        var = jnp.mean(hc * hc, axis=-1, keepdims=True)
        x1.append(hc * lax.rsqrt(var + LN_EPS) * g_ref[...] + b_ref[...])
        x1_ref[rows[s], :] = x1[s]

    xh = [x1[s].astype(BF16) for s in subs]
    xl = [(x1[s] - xh[s].astype(F32)).astype(BF16) for s in subs]
    wh = wr_hi_ref[...]
    logits = [(jnp.dot(xh[s], wh, preferred_element_type=F32)
               + jnp.dot(xl[s], wh, preferred_element_type=F32)
               + jnp.dot(xh[s], wr_lo_ref[...], preferred_element_type=F32)).T for s in subs]
    routes = [_route(logits[s]) for s in subs]

    ta = lax.broadcasted_iota(I32, (TM, TM), 0)
    tb = lax.broadcasted_iota(I32, (TM, TM), 1)
    upper = jnp.where(ta < tb, 1.0, 0.0).astype(BF16)
    ea = lax.broadcasted_iota(I32, (N_EXPERTS, N_EXPERTS), 0)
    eb = lax.broadcasted_iota(I32, (N_EXPERTS, N_EXPERTS), 1)
    lower = jnp.where(eb < ea, 1.0, 0.0).astype(BF16)
    lrow = lax.broadcasted_iota(I32, (LOCAL_ROWS, TM), 0).astype(F32)
    zero = jnp.zeros((1, TM), F32)
    cum = [jnp.dot((routes[s][2] + routes[s][3]).astype(BF16), upper, preferred_element_type=F32) for s in subs]
    perm = []
    for s in subs:
        gate1, gate2, oh1, oh2 = routes[s]
        n_e = jnp.sum(oh1 + oh2, axis=1, keepdims=True)
        m_rep = jnp.broadcast_to(jnp.floor((n_e + float(GRAN - 1)) * (1.0 / GRAN)), (N_EXPERTS, 128))
        run_start = jnp.dot(lower, m_rep.astype(BF16), preferred_element_type=F32)
        tot = cum[s] + float(GRAN) * run_start[:, 0:1]
        lp1 = jnp.sum(oh1 * tot, axis=0, keepdims=True)
        lp2 = jnp.sum(oh2 * tot, axis=0, keepdims=True)
        perm.append(jnp.where((lrow == lp1) | (lrow == lp2), 1.0, 0.0).astype(BF16))
        rf_ref[:, rows[s]] = jnp.concatenate([gate1, gate2, lp1, lp2, zero, zero, zero, zero], axis=0)
        mt_ref[s * N_EXPERTS:(s + 1) * N_EXPERTS, :] = m_rep
    for s in subs:
        xs_ref[s * LOCAL_ROWS:(s + 1) * LOCAL_ROWS, :] = jnp.dot(
            perm[s], xh[s], preferred_element_type=F32).astype(BF16)


def _merge_call(o_a, o_b, za, sgb, x, woa, wo, g, b, wr_hi, wr_lo):
    tm = MERGE_SUB * TM
    tok = lambda c: pl.BlockSpec((tm, c), lambda i: (i, 0))
    full = lambda shape: pl.BlockSpec(shape, lambda i: (0,) * len(shape))
    per_batch = SEQ // tm
    half = per_batch // 2
    o_a_spec = pl.BlockSpec((1, tm, ATTN_WIDTH), lambda i: (i // per_batch, jnp.minimum(i % per_batch, half - 1), 0))
    o_b_spec = pl.BlockSpec((1, tm, ATTN_WIDTH), lambda i: (i // per_batch, jnp.maximum(i % per_batch - half, 0), 0))
    return pl.pallas_call(
        _merge_kernel,
        grid=(TOKENS // tm,),
        in_specs=[o_a_spec, o_b_spec, tok(D_MODEL), tok(D_MODEL), tok(D_MODEL),
                  full((ATTN_WIDTH, D_MODEL)), full((D_MODEL, D_MODEL)), full((1, D_MODEL)),
                  full((1, D_MODEL)), full((D_MODEL, 128)), full((D_MODEL, 128))],
        out_specs=[tok(D_MODEL), pl.BlockSpec((MERGE_SUB * LOCAL_ROWS, D_MODEL), lambda i: (i, 0)),
                   pl.BlockSpec((8, tm), lambda i: (0, i)),
                   pl.BlockSpec((MERGE_SUB * N_EXPERTS, 128), lambda i: (i, 0))],
        out_shape=[jax.ShapeDtypeStruct((TOKENS, D_MODEL), F32),
                   jax.ShapeDtypeStruct((N_TOK_TILES * LOCAL_ROWS, D_MODEL), BF16),
                   jax.ShapeDtypeStruct((8, TOKENS), F32),
                   jax.ShapeDtypeStruct((N_TOK_TILES * N_EXPERTS, 128), F32)],
        scratch_shapes=[pltpu.VMEM((ATTN_WIDTH, D_MODEL), BF16), pltpu.VMEM((D_MODEL, D_MODEL), BF16)],
        compiler_params=pltpu.CompilerParams(
            dimension_semantics=("arbitrary",), vmem_limit_bytes=VMEM_LIMIT),
        name="merge_ln1_route",
    )(o_a, o_b, za, sgb, x, woa, wo, g, b, wr_hi, wr_lo)


def _granule_copy(src_ref, src_gran, dst_ref, dst_gran, sem):
    src = pl.multiple_of(src_gran * GRAN, GRAN)
    dst = pl.multiple_of(dst_gran * GRAN, GRAN)
    return pltpu.make_async_copy(src_ref.at[pl.ds(src, GRAN), :], dst_ref.at[pl.ds(dst, GRAN), :], sem)


def _expert_kernel(te_ref, nt_ref, gsrc_ref, gdst_ref, ug_ref, xs_ref, *refs):
    w_refs = refs[:3 * EXP_SUB]
    ys_ref, xbuf, ybuf, zbuf, in_sem, out_sem, zero_sem = refs[3 * EXP_SUB:]
    j = pl.program_id(0)
    n_tiles = nt_ref[0]
    slot = lax.rem(j, NBUF)

    def tile_gather(step, s):
        for g in range(STEP_GRANS):
            _granule_copy(xs_ref, gsrc_ref[step * STEP_GRANS + g], xbuf.at[s], g,
                          in_sem.at[s]).start(priority=g % 2)

    @pl.when(j == 0)
    def _():
        tile_gather(0, 0)
        ybuf[NBUF - 1] = jnp.zeros((STEP_ROWS, D_MODEL), BF16)
        zbuf[...] = jnp.zeros((GRAN, D_MODEL), BF16)
        for part in range(NBUF):
            spare = pltpu.make_async_copy(
                ybuf.at[NBUF - 1], ys_ref.at[pl.ds((SPARE_GRAN + part * STEP_GRANS) * GRAN, STEP_ROWS), :],
                out_sem.at[NBUF - 1])
            spare.start()
            spare.wait()

        for ahead in range(1, NBUF - 1):
            tile_gather(ahead, ahead)

    def zero_copy(t, g):
        return _granule_copy(zbuf, 0, ys_ref, t * LOCAL_GRANS + g, zero_sem)

    @pl.when(jnp.logical_and(j >= 1, j <= N_TOK_TILES))
    def _():
        def wait(g, c):
            zero_copy(j - 1, g).wait()
            return c

        lax.fori_loop(ug_ref[j - 1], LOCAL_GRANS, wait, 0)

    @pl.when(j < N_TOK_TILES)
    def _():
        def start(g, c):
            zero_copy(j, g).start()
            return c

        lax.fori_loop(ug_ref[j], LOCAL_GRANS, start, 0)

    def gather_wait():
        pltpu.make_async_copy(xs_ref.at[pl.ds(0, STEP_ROWS), :], xbuf.at[slot], in_sem.at[slot]).wait()

    @pl.when(jnp.logical_and(j >= NBUF, j - NBUF < n_tiles))
    def _():
        pltpu.make_async_copy(ybuf.at[slot], ys_ref.at[pl.ds(0, STEP_ROWS), :], out_sem.at[slot]).wait()

    @pl.when(jnp.logical_and(j >= n_tiles, j < n_tiles + NBUF - 1))
    def _():
        gather_wait()

    @pl.when(j < n_tiles)
    def _():
        gather_wait()
        wg, wu, wd = ([w_refs[3 * s + k][0].astype(BF16) for s in range(EXP_SUB)] for k in range(3))
        chains = [(s, s * TE + r) for s in range(EXP_SUB) for r in range(0, TE, CHAIN_ROWS)]
        xb = [xbuf[slot, r:r + CHAIN_ROWS, :] for _, r in chains]
        hg = [jnp.dot(x, wg[s], preferred_element_type=F32) for x, (s, _) in zip(xb, chains)]
        hu = [jnp.dot(x, wu[s], preferred_element_type=F32) for x, (s, _) in zip(xb, chains)]
        tile_gather(j + NBUF - 1, lax.rem(j + NBUF - 1, NBUF))
        h = [(a * _sigmoid(a) * b).astype(BF16) for a, b in zip(hg, hu)]
        for hc, (s, r) in zip(h, chains):
            ybuf[slot, r:r + CHAIN_ROWS, :] = jnp.dot(hc, wd[s], preferred_element_type=F32).astype(BF16)
        for g in range(STEP_GRANS):
            _granule_copy(ybuf.at[slot], g, ys_ref, gdst_ref[j * STEP_GRANS + g],
                          out_sem.at[slot]).start(priority=g % 2)


def _expert_call(tile_expert, n_steps, gsrc, gdst, used_grans, xs, wg, wu, wd):
    w_specs, w_args = [], []
    for s in range(EXP_SUB):
        wsel = lambda j, te, nt, gs, gd, ug, s=s: (te[EXP_SUB * j + s], 0, 0)
        w_specs += [pl.BlockSpec((1, D_MODEL, D_EXPERT), wsel), pl.BlockSpec((1, D_MODEL, D_EXPERT), wsel),
                    pl.BlockSpec((1, D_EXPERT, D_MODEL), wsel)]
        w_args += [wg, wu, wd]
    return pl.pallas_call(
        _expert_kernel,
        grid_spec=pltpu.PrefetchScalarGridSpec(
            num_scalar_prefetch=5,
            grid=(MAX_STEPS + NBUF,),
            in_specs=[pl.BlockSpec(memory_space=pl.ANY)] + w_specs,
            out_specs=pl.BlockSpec(memory_space=pl.ANY),
            scratch_shapes=[pltpu.VMEM((NBUF, STEP_ROWS, D_MODEL), BF16),
                            pltpu.VMEM((NBUF, STEP_ROWS, D_MODEL), BF16),
                            pltpu.VMEM((GRAN, D_MODEL), BF16),
                            pltpu.SemaphoreType.DMA((NBUF,)), pltpu.SemaphoreType.DMA((NBUF,)),
                            pltpu.SemaphoreType.DMA],
        ),
        out_shape=jax.ShapeDtypeStruct(((SPARE_GRAN + NBUF * STEP_GRANS) * GRAN, D_MODEL), BF16),
        compiler_params=pltpu.CompilerParams(
            dimension_semantics=("arbitrary",), vmem_limit_bytes=VMEM_LIMIT),
        name="experts",
    )(tile_expert, n_steps, gsrc, gdst, used_grans, xs, *w_args)


def _combine_kernel(ys_ref, x1_ref, rf_ref, g_ref, b_ref, out_ref):
    subs = range(MERGE_SUB)
    col = lax.broadcasted_iota(I32, (TM, LOCAL_ROWS), 1).astype(F32)
    route = [rf_ref[:, s * TM:(s + 1) * TM].T for s in subs]
    unsort = [(jnp.where(col == r[:, 2:3], r[:, 0:1], 0.0)
               + jnp.where(col == r[:, 3:4], r[:, 1:2], 0.0)).astype(BF16) for r in route]
    ffn = [jnp.dot(unsort[s], ys_ref[s * LOCAL_ROWS:(s + 1) * LOCAL_ROWS, :], preferred_element_type=F32)
           for s in subs]
    for s in subs:
        h = ALPHA * x1_ref[s * TM:(s + 1) * TM, :] + ffn[s]
        mu = jnp.mean(h, axis=-1, keepdims=True)
        hc = h - mu
        var = jnp.mean(hc * hc, axis=-1, keepdims=True)
        out_ref[s * TM:(s + 1) * TM, :] = hc * lax.rsqrt(var + LN_EPS) * g_ref[...] + b_ref[...]


def _combine_call(ys, x1, rf, g, b):
    tm = MERGE_SUB * TM
    return pl.pallas_call(
        _combine_kernel,
        grid=(TOKENS // tm,),
        in_specs=[pl.BlockSpec((MERGE_SUB * LOCAL_ROWS, D_MODEL), lambda i: (i, 0)),
                  pl.BlockSpec((tm, D_MODEL), lambda i: (i, 0)),
                  pl.BlockSpec((8, tm), lambda i: (0, i)),
                  pl.BlockSpec((1, D_MODEL), lambda i: (0, 0)),
                  pl.BlockSpec((1, D_MODEL), lambda i: (0, 0))],
        out_specs=pl.BlockSpec((tm, D_MODEL), lambda i: (i, 0)),
        out_shape=jax.ShapeDtypeStruct((TOKENS, D_MODEL), F32),
        compiler_params=pltpu.CompilerParams(
            dimension_semantics=("arbitrary",), vmem_limit_bytes=VMEM_LIMIT),
        name="combine_ln2",
    )(ys, x1, rf, g, b)


def _router_cols(w_router_group, w_router_expert):
    w = jnp.concatenate([w_router_group, jnp.zeros((D_MODEL, 4), F32), w_router_expert,
                         jnp.zeros((D_MODEL, 128 - ROUTER_ROWS), F32)], axis=1)
    hi = w.astype(BF16)
    lo = (w - hi.astype(F32)).astype(BF16)
    return hi, lo


def _layer(x, w_in, conv_w, w_out_conv, w_out_attn, w_o, ln1_g, ln1_b,
           w_router_group, w_router_expert, w_gate, w_up, w_down, ln2_g, ln2_b):
    slopes = jnp.asarray([2.0 ** (-8.0 * (h + 1) / N_HEADS) for h in range(N_HEADS)], F32)
    q, k, v, za, sgb = _proj_call(x, w_in, conv_w, w_out_conv)
    o_a, o_b = _attn_call(slopes, q, k, v)

    wr_hi, wr_lo = _router_cols(w_router_group, w_router_expert)
    x1, xs, rf, mt = _merge_call(
        o_a, o_b, za.reshape(TOKENS, D_MODEL), sgb.reshape(TOKENS, D_MODEL),
        x.reshape(TOKENS, D_MODEL), w_out_attn, w_o,
        ln1_g.reshape(1, D_MODEL), ln1_b.reshape(1, D_MODEL), wr_hi, wr_lo)

    grans = mt.reshape(N_TOK_TILES, N_EXPERTS, 128)[:, :, 0].astype(I32)
    local_start = jnp.cumsum(grans, axis=1) - grans
    grans_t = grans.T
    tiles_e = (jnp.sum(grans_t, axis=1) + TILE_GRANS - 1) // TILE_GRANS
    tile_end = jnp.cumsum(tiles_e)
    n_steps = ((tile_end[-1] + EXP_SUB - 1) // EXP_SUB).reshape(1)
    all_tiles = (MAX_STEPS + NBUF) * EXP_SUB
    tile_ids = jnp.arange(all_tiles, dtype=I32)
    tile_expert = jnp.minimum(
        jnp.sum((tile_ids[:, None] >= tile_end[None, :]).astype(I32), axis=1), N_EXPERTS - 1)
    run_slot = TILE_GRANS * (tile_end - tiles_e)[:, None] + jnp.cumsum(grans_t, axis=1) - grans_t
    run_src = jnp.arange(N_TOK_TILES, dtype=I32)[None, :] * LOCAL_GRANS + local_start.T
    pick = (tile_expert[:all_tiles, None] == jnp.arange(N_EXPERTS, dtype=I32)[None, :])[:, :, None]
    t_slot = jnp.sum(jnp.where(pick, run_slot[None], 0), axis=1)
    t_len = jnp.sum(jnp.where(pick, grans_t[None], 0), axis=1)
    t_src = jnp.sum(jnp.where(pick, run_src[None], 0), axis=1)
    slots = jnp.arange(all_tiles * TILE_GRANS, dtype=I32).reshape(all_tiles, TILE_GRANS)
    k = slots[:, :, None] - t_slot[:, None, :]
    hit = (k >= 0) & (k < t_len[:, None, :])
    gran = jnp.sum(jnp.where(hit, t_src[:, None, :] + k, 0), axis=2).reshape(-1)
    filled = (jnp.sum(hit.astype(I32), axis=2) > 0).reshape(-1)
    slots = slots.reshape(-1)
    gsrc = jnp.where(filled, gran, 0)
    gdst = jnp.where(filled, gran, SPARE_GRAN + slots % (NBUF * STEP_GRANS))

    ys = _expert_call(tile_expert, n_steps, gsrc, gdst, jnp.sum(grans, axis=1), xs, w_gate, w_up, w_down)
    out = _combine_call(ys, x1, rf, ln2_g.reshape(1, D_MODEL), ln2_b.reshape(1, D_MODEL))
    return out.reshape(BATCH, SEQ, D_MODEL)


def kernel(x, w_in, conv_w, w_out_conv, w_out_attn, w_o, ln1_g, ln1_b, w_router_group, w_router_expert, w_gate, w_up, w_down, ln2_g, ln2_b):
    depth = w_in.shape[0]
    for l in range(depth):
        x = _layer(x, w_in[l], conv_w[l], w_out_conv[l], w_out_attn[l], w_o[l], ln1_g[l], ln1_b[l],
                   w_router_group[l], w_router_expert[l], w_gate[l], w_up[l], w_down[l], ln2_g[l], ln2_b[l])
    return x
```

```python
import functools

import jax
import jax.numpy as jnp
from jax import lax
from jax.experimental import pallas as pl
from jax.experimental.pallas import tpu as pltpu

F32 = jnp.float32
BF16 = jnp.bfloat16
U32 = jnp.uint32
I32 = jnp.int32

D_MODEL = 1024
BATCH = 8
SEQ = 2048
TOKENS = BATCH * SEQ
CONV_WIDTH = 512
N_HEADS = 8
HEAD_DIM = 64
ATTN_WIDTH = N_HEADS * HEAD_DIM
MOBA_BLOCK = 256
N_BLOCKS = SEQ // MOBA_BLOCK
MOBA_TOPK = 3
N_GROUPS = 4
EXPERTS_PER_GROUP = 8
N_EXPERTS = N_GROUPS * EXPERTS_PER_GROUP
D_EXPERT = 256
LN_EPS = 1e-5
ALPHA = 2.0 ** 0.25
IN_COLS = 3 * CONV_WIDTH + 3 * ATTN_WIDTH + 2 * D_MODEL
HALF = D_MODEL // 2

TM = 256
TM_PROJ = 512
TE = 512
CHAIN_ROWS = 256
PV_ROWS = HEAD_DIM + 16
GRAN = 16
TILE_GRANS = TE // GRAN
N_TOK_TILES = TOKENS // TM
LOCAL_ROWS = -(-(2 * TM + N_EXPERTS * (GRAN - 1)) // 256) * 256
LOCAL_GRANS = LOCAL_ROWS // GRAN
SPARE_GRAN = N_TOK_TILES * LOCAL_GRANS
MAX_TILES = (2 * TOKENS + N_TOK_TILES * N_EXPERTS * (GRAN - 1)) // TE + N_EXPERTS
PAIRS = 2
MERGE_SUB = 2
EXP_SUB = 1
STEP_ROWS = EXP_SUB * TE
STEP_GRANS = EXP_SUB * TILE_GRANS
MAX_STEPS = -(-MAX_TILES // EXP_SUB)
NBUF = 4
ROUTER_ROWS = 40
VMEM_LIMIT = 56 * 1024 * 1024
NEG_INF = float("-inf")


def _sigmoid(z):
    return 1.0 / (1.0 + jnp.exp(-z))


def _proj_kernel(x_ref, w_in_hbm_ref, convw_ref, woc_f32_ref, q_ref, k_ref, v_ref, za_ref, sgb_ref,
                 ubuf, w_in_ref, woc_ref, stage, stage_sem):
    s = pl.program_id(1)
    tm = TM_PROJ

    @pl.when((pl.program_id(0) == 0) & (s == 0))
    def _():
        def chunk(c):
            return pltpu.make_async_copy(w_in_hbm_ref.at[:, c * CONV_WIDTH:(c + 1) * CONV_WIDTH],
                                         stage.at[c % 2], stage_sem.at[c % 2])

        n_chunks = IN_COLS // CONV_WIDTH
        chunk(0).start()
        for c in range(n_chunks):
            if c + 1 < n_chunks:
                chunk(c + 1).start()
            chunk(c).wait()
            w_in_ref[:, c * CONV_WIDTH:(c + 1) * CONV_WIDTH] = stage[c % 2].astype(BF16)
        woc_ref[...] = woc_f32_ref[...].astype(BF16)

    xb = x_ref[0].astype(BF16)

    def proj(c0, c1):
        return jnp.dot(xb, w_in_ref[:, c0:c1], preferred_element_type=F32)

    c_b = proj(0, CONV_WIDTH)
    u = proj(CONV_WIDTH, 2 * CONV_WIDTH) * proj(2 * CONV_WIDTH, 3 * CONV_WIDTH)

    @pl.when(s == 0)
    def _():
        ubuf[0:8, :] = jnp.zeros((8, CONV_WIDTH), F32)

    ubuf[8:8 + tm, :] = u
    w = convw_ref[...]
    conv = w[2:3, :] * u + w[1:2, :] * ubuf[7:7 + tm, :] + w[0:1, :] * ubuf[6:6 + tm, :]
    ubuf[0:8, :] = u[tm - 8:tm, :]
    hc = (c_b * conv).astype(BF16)
    y_conv = jnp.dot(hc, woc_ref[...], preferred_element_type=F32)

    o = 3 * CONV_WIDTH
    q_ref[0] = (proj(o, o + ATTN_WIDTH) * (HEAD_DIM ** -0.5)).astype(BF16)
    k_ref[0] = proj(o + ATTN_WIDTH, o + 2 * ATTN_WIDTH).astype(BF16)
    v_ref[0] = proj(o + 2 * ATTN_WIDTH, o + 3 * ATTN_WIDTH).astype(BF16)
    o += 3 * ATTN_WIDTH
    za_ref[0] = (_sigmoid(proj(o, o + D_MODEL)) * y_conv).astype(BF16)
    sgb_ref[0] = _sigmoid(proj(o + D_MODEL, o + 2 * D_MODEL)).astype(BF16)


def _proj_call(x, w_in, conv_w, w_out_conv):
    tok_spec = lambda c: pl.BlockSpec((1, TM_PROJ, c), lambda b, s: (b, s, 0))
    full = lambda shape: pl.BlockSpec(shape, lambda b, s: (0,) * len(shape))
    once = lambda shape: pl.BlockSpec(shape, lambda b, s: (0,) * len(shape), pipeline_mode=pl.Buffered(1))
    return pl.pallas_call(
        _proj_kernel,
        grid=(BATCH, SEQ // TM_PROJ),
        in_specs=[tok_spec(D_MODEL), pl.BlockSpec(memory_space=pl.ANY), full((3, CONV_WIDTH)),
                  once((CONV_WIDTH, D_MODEL))],
        out_specs=[tok_spec(ATTN_WIDTH), tok_spec(ATTN_WIDTH), tok_spec(ATTN_WIDTH),
                   tok_spec(D_MODEL), tok_spec(D_MODEL)],
        out_shape=[jax.ShapeDtypeStruct((BATCH, SEQ, ATTN_WIDTH), BF16)] * 3
        + [jax.ShapeDtypeStruct((BATCH, SEQ, D_MODEL), BF16)] * 2,
        scratch_shapes=[pltpu.VMEM((TM_PROJ + 8, CONV_WIDTH), F32), pltpu.VMEM((D_MODEL, IN_COLS), BF16),
                        pltpu.VMEM((CONV_WIDTH, D_MODEL), BF16),
                        pltpu.VMEM((2, D_MODEL, CONV_WIDTH), F32), pltpu.SemaphoreType.DMA((2,))],
        compiler_params=pltpu.CompilerParams(
            dimension_semantics=("arbitrary", "arbitrary"), vmem_limit_bytes=VMEM_LIMIT),
        name="proj",
    )(x, w_in, conv_w, w_out_conv)


def _attn_kernel(slopes_ref, qa_ref, qb_ref, k_ref, v_ref, oa_ref, ob_ref,
                 kaug_ref, vt_ref, kmean_ref, qaug_ref, pv_ref, mloc_ref, t_ref, p_ref):
    hq = pl.program_id(1)
    j = pl.program_id(2)
    blk = MOBA_BLOCK
    pairs = range(PAIRS)
    lanes = [slice(128 * pp, 128 * (pp + 1)) for pp in pairs]

    @pl.when(j == 0)
    def _():
        klane = lax.broadcasted_iota(I32, (blk, 128), 1)
        koff = lax.broadcasted_iota(I32, (blk, 128), 0).astype(F32)
        k_extra = jnp.where(klane == 0, koff, jnp.where(klane == 1, 1.0, 0.0)).astype(BF16)
        orow = lax.broadcasted_iota(I32, (PV_ROWS - HEAD_DIM, blk), 0)
        ones_rows = jnp.where(orow == 0, 1.0, 0.0).astype(BF16)
        for pp in pairs:
            for n in range(N_BLOCKS):
                kblk = k_ref[0, n * blk:(n + 1) * blk, lanes[pp]]
                kaug_ref[pp, n, :, 0:128] = kblk
                kaug_ref[pp, n, :, 128:256] = k_extra
                kmean_ref[pp, n:n + 1, :] = jnp.mean(kblk.astype(F32), axis=0, keepdims=True)
                v_t = v_ref[0, n * blk:(n + 1) * blk, lanes[pp]].astype(F32).T.astype(BF16)
                for hh in range(2):
                    vt_ref[pp, n, hh, 0:HEAD_DIM, :] = v_t[hh * HEAD_DIM:(hh + 1) * HEAD_DIM, :]
                    vt_ref[pp, n, hh, HEAD_DIM:PV_ROWS, :] = ones_rows

    lane = lax.broadcasted_iota(I32, (1, 2 * blk), 1)
    qoff_row = jnp.where(lane < blk, lane, lane - blk).astype(F32)
    feat = lax.broadcasted_iota(I32, (2 * HEAD_DIM, blk), 0)
    arow = lax.broadcasted_iota(I32, (2 * HEAD_DIM, 2 * blk), 0)
    blk_i = lax.broadcasted_iota(I32, (N_BLOCKS, 2 * blk), 0)
    key_i = lax.broadcasted_iota(I32, (blk, 2 * blk), 0)
    qry_j = lax.broadcasted_iota(I32, (blk, 2 * blk), 1)
    causal = key_i <= jnp.where(qry_j < blk, qry_j, qry_j - blk)
    slope_rows, q_extras = [], []
    for pp in pairs:
        head = 2 * (PAIRS * hq + pp)
        slope_rows.append(jnp.where(lane < blk, slopes_ref[head], slopes_ref[head + 1]))
        q_extras.append(jnp.where(arow == 0, slope_rows[pp],
                                  jnp.where(arow == 1, -slope_rows[pp] * qoff_row, 0.0)).astype(BF16))

    def prepare(q_ref, pp, slot, qblock):
        q_t = q_ref[0, :, lanes[pp]].astype(F32).T
        qcat = jnp.concatenate([jnp.where(feat < HEAD_DIM, q_t, 0.0), jnp.where(feat >= HEAD_DIM, q_t, 0.0)],
                               axis=1).astype(BF16)
        qaug_ref[pp, slot, 0:2 * HEAD_DIM, :] = qcat
        qaug_ref[pp, slot, 2 * HEAD_DIM:4 * HEAD_DIM, :] = q_extras[pp]
        gate = jnp.dot(kmean_ref[pp].astype(BF16), qcat, preferred_element_type=F32)
        cnt = jnp.zeros((N_BLOCKS, 2 * blk), F32)
        for m in range(N_BLOCKS):
            gm = gate[m:m + 1, :]
            beats = (gm > gate) | ((gm == gate) & (blk_i > m))
            cnt = cnt + jnp.where(beats & (qblock > m), 1.0, 0.0)
        return jnp.where((blk_i < qblock) & (cnt < float(MOBA_TOPK)), 1.0, 0.0)

    qblock_a = j
    qblock_b = N_BLOCKS - 1 - j
    sel_a = [prepare(qa_ref, pp, 0, qblock_a) for pp in pairs]
    sel_b = [prepare(qb_ref, pp, 1, qblock_b) for pp in pairs]

    n_mid = N_BLOCKS - 1
    slots = [(0, 0, qblock_a, True)]
    mids = []
    for s in range(1, n_mid + 1):
        is_a = s <= j
        slots.append((s, jnp.where(is_a, 0, 1), jnp.where(is_a, s - 1, s - 1 - j), False))
        mids.append((is_a, slots[-1][2]))
    slots.append((n_mid + 1, 1, qblock_b, True))

    for pp in pairs:
        for s, which, kb, _ in slots:
            t_ref[pp, s] = jnp.dot(kaug_ref[pp, kb], qaug_ref[pp, which], preferred_element_type=F32)
    for pp in pairs:
        for s, _, _, own in slots:
            t = t_ref[pp, s]
            if own:
                t = jnp.where(causal, t, NEG_INF)
            m_loc = jnp.max(t, axis=0, keepdims=True)
            p_ref[pp, s] = jnp.exp((t - m_loc).astype(BF16))
            mloc_ref[pp, s:s + 1, :] = m_loc
    for pp in pairs:
        for s, _, kb, _ in slots:
            pv_ref[pp, s, 0] = jnp.dot(vt_ref[pp, kb, 0], p_ref[pp, s, :, 0:blk], preferred_element_type=F32)
            pv_ref[pp, s, 1] = jnp.dot(vt_ref[pp, kb, 1], p_ref[pp, s, :, blk:2 * blk],
                                       preferred_element_type=F32)

    def combine(o_ref, pp, own_slot, sel, qblock, mine):
        neg = jnp.full((1, 2 * blk), -1e30, F32)
        pieces = [(own_slot, mloc_ref[pp, own_slot:own_slot + 1, :])]
        for s, (is_a, kb) in enumerate(mids, start=1):
            selrow = jnp.sum(jnp.where(blk_i == kb, sel, 0.0), axis=0, keepdims=True)
            belongs = jnp.where(is_a, 1.0, 0.0) if mine else jnp.where(is_a, 0.0, 1.0)
            used = selrow * belongs > 0.5
            shift = slope_rows[pp] * ((kb - qblock) * blk).astype(F32)
            pieces.append((s, jnp.where(used, mloc_ref[pp, s:s + 1, :] + shift, neg)))
        m_all = pieces[0][1]
        for _, m_s in pieces[1:]:
            m_all = jnp.maximum(m_all, m_s)
        acc = [jnp.zeros((PV_ROWS, blk), F32), jnp.zeros((PV_ROWS, blk), F32)]
        for s, m_s in pieces:
            w = jnp.exp(m_s - m_all)
            for hh in range(2):
                acc[hh] = acc[hh] + pv_ref[pp, s, hh] * w[:, hh * blk:(hh + 1) * blk]
        o_t = jnp.concatenate([a[0:HEAD_DIM, :] / a[HEAD_DIM:HEAD_DIM + 1, :] for a in acc], axis=0)
        o_ref[0, :, lanes[pp]] = o_t.T.astype(BF16)

    for pp in pairs:
        combine(oa_ref, pp, 0, sel_a[pp], qblock_a, True)
        combine(ob_ref, pp, n_mid + 1, sel_b[pp], qblock_b, False)


def _attn_call(slopes, q, k, v):
    half = N_BLOCKS // 2
    width = 128 * PAIRS
    o_a, o_b = pl.pallas_call(
        _attn_kernel,
        grid_spec=pltpu.PrefetchScalarGridSpec(
            num_scalar_prefetch=1,
            grid=(BATCH, N_HEADS // (2 * PAIRS), half),
            in_specs=[
                pl.BlockSpec((1, MOBA_BLOCK, width), lambda b, h, j, sl: (b, j, h)),
                pl.BlockSpec((1, MOBA_BLOCK, width), lambda b, h, j, sl: (b, N_BLOCKS - 1 - j, h)),
                pl.BlockSpec((1, SEQ, width), lambda b, h, j, sl: (b, 0, h)),
                pl.BlockSpec((1, SEQ, width), lambda b, h, j, sl: (b, 0, h)),
            ],
            out_specs=[pl.BlockSpec((1, MOBA_BLOCK, width), lambda b, h, j, sl: (b, j, h)),
                       pl.BlockSpec((1, MOBA_BLOCK, width), lambda b, h, j, sl: (b, half - 1 - j, h))],
            scratch_shapes=[
                pltpu.VMEM((PAIRS, N_BLOCKS, MOBA_BLOCK, 256), BF16),
                pltpu.VMEM((PAIRS, N_BLOCKS, 2, PV_ROWS, MOBA_BLOCK), BF16),
                pltpu.VMEM((PAIRS, N_BLOCKS, 128), F32),
                pltpu.VMEM((PAIRS, 2, 256, 2 * MOBA_BLOCK), BF16),
                pltpu.VMEM((PAIRS, N_BLOCKS + 1, 2, PV_ROWS, MOBA_BLOCK), F32),
                pltpu.VMEM((PAIRS, 16, 2 * MOBA_BLOCK), F32),
                pltpu.VMEM((PAIRS, N_BLOCKS + 1, MOBA_BLOCK, 2 * MOBA_BLOCK), F32),
                pltpu.VMEM((PAIRS, N_BLOCKS + 1, MOBA_BLOCK, 2 * MOBA_BLOCK), BF16),
            ],
        ),
        out_shape=[jax.ShapeDtypeStruct((BATCH, SEQ // 2, ATTN_WIDTH), BF16)] * 2,
        compiler_params=pltpu.CompilerParams(
            dimension_semantics=("arbitrary", "arbitrary", "arbitrary"), vmem_limit_bytes=VMEM_LIMIT),
        name="moba_attn",
    )(slopes, q, q, k, v)
    return o_a, o_b


def _route(logits):
    row8 = lax.broadcasted_iota(I32, (8, TM), 0).astype(F32)
    gl = jnp.where(row8 < float(N_GROUPS), logits[0:8, :], NEG_INF)
    gexp = jnp.exp(gl - jnp.max(gl, axis=0, keepdims=True))
    gprob = gexp / jnp.sum(gexp, axis=0, keepdims=True)
    ptop = jnp.max(gprob, axis=0, keepdims=True)
    gtop = jnp.min(jnp.where(gprob == ptop, row8, 8.0), axis=0, keepdims=True)
    el = logits[8:ROUTER_ROWS, :]
    eg = jnp.where(gtop == 0.0, el[0:8, :],
                   jnp.where(gtop == 1.0, el[8:16, :], jnp.where(gtop == 2.0, el[16:24, :], el[24:32, :])))
    m1 = jnp.max(eg, axis=0, keepdims=True)
    i1 = jnp.min(jnp.where(eg == m1, row8, 8.0), axis=0, keepdims=True)
    eg2 = jnp.where(row8 == i1, NEG_INF, eg)
    m2 = jnp.max(eg2, axis=0, keepdims=True)
    i2 = jnp.min(jnp.where(eg2 == m2, row8, 8.0), axis=0, keepdims=True)
    t2 = jnp.exp(m2 - m1)
    gate1 = ptop * (1.0 / (1.0 + t2))
    gate2 = ptop * (t2 / (1.0 + t2))
    erow = lax.broadcasted_iota(I32, (N_EXPERTS, TM), 0).astype(F32)
    oh1 = jnp.where(erow == gtop * float(EXPERTS_PER_GROUP) + i1, 1.0, 0.0)
    oh2 = jnp.where(erow == gtop * float(EXPERTS_PER_GROUP) + i2, 1.0, 0.0)
    return gate1, gate2, oh1, oh2


def _merge_kernel(oa_ref, ob_ref, za_ref, sgb_ref, x_ref, woa_f32_ref, wo_f32_ref, g_ref, b_ref,
                  wr_hi_ref, wr_lo_ref, x1_ref, xs_ref, rf_ref, mt_ref, woa_ref, wo_ref):
    i = pl.program_id(0)

    @pl.when(i == 0)
    def _():
        woa_ref[...] = woa_f32_ref[...].astype(BF16)
        wo_ref[...] = wo_f32_ref[...].astype(BF16)

    subs = range(MERGE_SUB)
    rows = [slice(s * TM, (s + 1) * TM) for s in subs]
    steps_per_batch = N_BLOCKS // MERGE_SUB
    in_oa = lax.rem(i, steps_per_batch) < steps_per_batch // 2
    o = [jnp.where(in_oa, oa_ref[0, r, :], ob_ref[0, r, :]) for r in rows]
    y_attn = [jnp.dot(o[s], woa_ref[...], preferred_element_type=F32) for s in subs]
    y = [(za_ref[rows[s], :].astype(F32) + sgb_ref[rows[s], :].astype(F32) * y_attn[s]).astype(BF16)
         for s in subs]
    mix = [jnp.dot(y[s], wo_ref[...], preferred_element_type=F32) for s in subs]
    x1 = []
    for s in subs:
        h = ALPHA * x_ref[rows[s], :] + mix[s]
        mu = jnp.mean(h, axis=-1, keepdims=True)
        hc = h - mu
        var = jnp.mean(hc * hc, axis=-1, keepdims=True)
        x1.append(hc * lax.rsqrt(var + LN_EPS) * g_ref[...] + b_ref[...])
        x1_ref[rows[s], :] = x1[s]

    xh = [x1[s].astype(BF16) for s in subs]
    xl = [(x1[s] - xh[s].astype(F32)).astype(BF16) for s in subs]
    wh = wr_hi_ref[...]
    logits = [(jnp.dot(xh[s], wh, preferred_element_type=F32)
               + jnp.dot(xl[s], wh, preferred_element_type=F32)
               + jnp.dot(xh[s], wr_lo_ref[...], preferred_element_type=F32)).T for s in subs]
    routes = [_route(logits[s]) for s in subs]

    ta = lax.broadcasted_iota(I32, (TM, TM), 0)
    tb = lax.broadcasted_iota(I32, (TM, TM), 1)
    upper = jnp.where(ta < tb, 1.0, 0.0).astype(BF16)
    ea = lax.broadcasted_iota(I32, (N_EXPERTS, N_EXPERTS), 0)
    eb = lax.broadcasted_iota(I32, (N_EXPERTS, N_EXPERTS), 1)
    lower = jnp.where(eb < ea, 1.0, 0.0).astype(BF16)
    lrow = lax.broadcasted_iota(I32, (LOCAL_ROWS, TM), 0).astype(F32)
    zero = jnp.zeros((1, TM), F32)
    cum = [jnp.dot((routes[s][2] + routes[s][3]).astype(BF16), upper, preferred_element_type=F32) for s in subs]
    perm = []
    for s in subs:
        gate1, gate2, oh1, oh2 = routes[s]
        n_e = jnp.sum(oh1 + oh2, axis=1, keepdims=True)
        m_rep = jnp.broadcast_to(jnp.floor((n_e + float(GRAN - 1)) * (1.0 / GRAN)), (N_EXPERTS, 128))
        run_start = jnp.dot(lower, m_rep.astype(BF16), preferred_element_type=F32)
        tot = cum[s] + float(GRAN) * run_start[:, 0:1]
        lp1 = jnp.sum(oh1 * tot, axis=0, keepdims=True)
        lp2 = jnp.sum(oh2 * tot, axis=0, keepdims=True)
        perm.append(jnp.where((lrow == lp1) | (lrow == lp2), 1.0, 0.0).astype(BF16))
        rf_ref[:, rows[s]] = jnp.concatenate([gate1, gate2, lp1, lp2, zero, zero, zero, zero], axis=0)
        mt_ref[s * N_EXPERTS:(s + 1) * N_EXPERTS, :] = m_rep
    for s in subs:
        xs_ref[s * LOCAL_ROWS:(s + 1) * LOCAL_ROWS, :] = jnp.dot(
            perm[s], xh[s], preferred_element_type=F32).astype(BF16)


def _merge_call(o_a, o_b, za, sgb, x, woa, wo, g, b, wr_hi, wr_lo):
    tm = MERGE_SUB * TM
    tok = lambda c: pl.BlockSpec((tm, c), lambda i: (i, 0))
    full = lambda shape: pl.BlockSpec(shape, lambda i: (0,) * len(shape))
    per_batch = SEQ // tm
    half = per_batch // 2
    o_a_spec = pl.BlockSpec((1, tm, ATTN_WIDTH), lambda i: (i // per_batch, jnp.minimum(i % per_batch, half - 1), 0))
    o_b_spec = pl.BlockSpec((1, tm, ATTN_WIDTH), lambda i: (i // per_batch, jnp.maximum(i % per_batch - half, 0), 0))
    return pl.pallas_call(
        _merge_kernel,
        grid=(TOKENS // tm,),
        in_specs=[o_a_spec, o_b_spec, tok(D_MODEL), tok(D_MODEL), tok(D_MODEL),
                  full((ATTN_WIDTH, D_MODEL)), full((D_MODEL, D_MODEL)), full((1, D_MODEL)),
                  full((1, D_MODEL)), full((D_MODEL, 128)), full((D_MODEL, 128))],
        out_specs=[tok(D_MODEL), pl.BlockSpec((MERGE_SUB * LOCAL_ROWS, D_MODEL), lambda i: (i, 0)),
                   pl.BlockSpec((8, tm), lambda i: (0, i)),
                   pl.BlockSpec((MERGE_SUB * N_EXPERTS, 128), lambda i: (i, 0))],
        out_shape=[jax.ShapeDtypeStruct((TOKENS, D_MODEL), F32),
                   jax.ShapeDtypeStruct((N_TOK_TILES * LOCAL_ROWS, D_MODEL), BF16),
                   jax.ShapeDtypeStruct((8, TOKENS), F32),
                   jax.ShapeDtypeStruct((N_TOK_TILES * N_EXPERTS, 128), F32)],
        scratch_shapes=[pltpu.VMEM((ATTN_WIDTH, D_MODEL), BF16), pltpu.VMEM((D_MODEL, D_MODEL), BF16)],
        compiler_params=pltpu.CompilerParams(
            dimension_semantics=("arbitrary",), vmem_limit_bytes=VMEM_LIMIT),
        name="merge_ln1_route",
    )(o_a, o_b, za, sgb, x, woa, wo, g, b, wr_hi, wr_lo)


def _granule_copy(src_ref, src_gran, dst_ref, dst_gran, sem):
    src = pl.multiple_of(src_gran * GRAN, GRAN)
    dst = pl.multiple_of(dst_gran * GRAN, GRAN)
    return pltpu.make_async_copy(src_ref.at[pl.ds(src, GRAN), :], dst_ref.at[pl.ds(dst, GRAN), :], sem)


def _expert_kernel(te_ref, nt_ref, gsrc_ref, gdst_ref, ug_ref, xs_ref, wg_f32_ref, wu_f32_ref, wd_f32_ref,
                   ys_ref, xbuf, ybuf, zbuf, wg_ref, wu_ref, wd_ref, in_sem, out_sem, zero_sem):
    j = pl.program_id(0)
    n_tiles = nt_ref[0]
    slot = lax.rem(j, NBUF)
    prev_slot = lax.rem(j + NBUF - 1, NBUF)

    def tile_gather(step, s):
        for g in range(STEP_GRANS):
            _granule_copy(xs_ref, gsrc_ref[step * STEP_GRANS + g], xbuf.at[s], g,
                          in_sem.at[s]).start(priority=g % 2)

    def prev_scatter():
        for g in range(STEP_GRANS):
            _granule_copy(ybuf.at[prev_slot], g, ys_ref, gdst_ref[j * STEP_GRANS + g],
                          out_sem.at[prev_slot]).start(priority=g % 2)

    @pl.when(j == 0)
    def _():
        tile_gather(0, 0)
        ybuf[NBUF - 1] = jnp.zeros((STEP_ROWS, D_MODEL), BF16)
        zbuf[...] = jnp.zeros((GRAN, D_MODEL), BF16)
        for part in range(NBUF):
            spare = pltpu.make_async_copy(
                ybuf.at[NBUF - 1], ys_ref.at[pl.ds((SPARE_GRAN + part * STEP_GRANS) * GRAN, STEP_ROWS), :],
                out_sem.at[NBUF - 1])
            spare.start()
            spare.wait()

        for ahead in range(1, NBUF - 1):
            tile_gather(ahead, ahead)

    def zero_copy(t, g):
        return _granule_copy(zbuf, 0, ys_ref, t * LOCAL_GRANS + g, zero_sem)

    @pl.when(jnp.logical_and(j >= 1, j <= N_TOK_TILES))
    def _():
        def wait(g, c):
            zero_copy(j - 1, g).wait()
            return c

        lax.fori_loop(ug_ref[j - 1], LOCAL_GRANS, wait, 0)

    @pl.when(j < N_TOK_TILES)
    def _():
        def start(g, c):
            zero_copy(j, g).start()
            return c

        lax.fori_loop(ug_ref[j], LOCAL_GRANS, start, 0)

    def gather_wait():
        pltpu.make_async_copy(xs_ref.at[pl.ds(0, STEP_ROWS), :], xbuf.at[slot], in_sem.at[slot]).wait()

    @pl.when(jnp.logical_and(j >= NBUF - 1, j - NBUF < n_tiles))
    def _():
        pltpu.make_async_copy(ybuf.at[slot], ys_ref.at[pl.ds(0, STEP_ROWS), :], out_sem.at[slot]).wait()

    @pl.when(jnp.logical_and(j >= n_tiles, j < n_tiles + NBUF - 1))
    def _():
        gather_wait()

    @pl.when(j == n_tiles)
    def _():
        prev_scatter()

    @pl.when(jnp.logical_and(j < n_tiles, jnp.logical_or(j == 0, te_ref[j] != te_ref[jnp.maximum(j - 1, 0)])))
    def _():
        wg_ref[...] = wg_f32_ref[0].astype(BF16)
        wu_ref[...] = wu_f32_ref[0].astype(BF16)
        wd_ref[...] = wd_f32_ref[0].astype(BF16)

    @pl.when(j < n_tiles)
    def _():
        gather_wait()
        chains = range(0, TE, CHAIN_ROWS)
        xb = [xbuf[slot, r:r + CHAIN_ROWS, :] for r in chains]
        hg = [jnp.dot(x, wg_ref[...], preferred_element_type=F32) for x in xb]
        hu = [jnp.dot(x, wu_ref[...], preferred_element_type=F32) for x in xb]
        tile_gather(j + NBUF - 1, prev_slot)
        prev_scatter()
        h = [(a * _sigmoid(a) * b).astype(BF16) for a, b in zip(hg, hu)]
        for hc, r in zip(h, chains):
            ybuf[slot, r:r + CHAIN_ROWS, :] = jnp.dot(hc, wd_ref[...], preferred_element_type=F32).astype(BF16)


def _expert_call(tile_expert, n_steps, gsrc, gdst, used_grans, xs, wg, wu, wd):
    wsel = lambda j, te, nt, gs, gd, ug: (te[j], 0, 0)
    return pl.pallas_call(
        _expert_kernel,
        grid_spec=pltpu.PrefetchScalarGridSpec(
            num_scalar_prefetch=5,
            grid=(MAX_STEPS + NBUF,),
            in_specs=[pl.BlockSpec(memory_space=pl.ANY),
                      pl.BlockSpec((1, D_MODEL, D_EXPERT), wsel), pl.BlockSpec((1, D_MODEL, D_EXPERT), wsel),
                      pl.BlockSpec((1, D_EXPERT, D_MODEL), wsel)],
            out_specs=pl.BlockSpec(memory_space=pl.ANY),
            scratch_shapes=[pltpu.VMEM((NBUF, STEP_ROWS, D_MODEL), BF16),
                            pltpu.VMEM((NBUF, STEP_ROWS, D_MODEL), BF16),
                            pltpu.VMEM((GRAN, D_MODEL), BF16),
                            pltpu.VMEM((D_MODEL, D_EXPERT), BF16), pltpu.VMEM((D_MODEL, D_EXPERT), BF16),
                            pltpu.VMEM((D_EXPERT, D_MODEL), BF16),
                            pltpu.SemaphoreType.DMA((NBUF,)), pltpu.SemaphoreType.DMA((NBUF,)),
                            pltpu.SemaphoreType.DMA],
        ),
        out_shape=jax.ShapeDtypeStruct(((SPARE_GRAN + NBUF * STEP_GRANS) * GRAN, D_MODEL), BF16),
        compiler_params=pltpu.CompilerParams(
            dimension_semantics=("arbitrary",), vmem_limit_bytes=VMEM_LIMIT),
        name="experts",
    )(tile_expert, n_steps, gsrc, gdst, used_grans, xs, wg, wu, wd)


def _combine_kernel(ys_ref, x1_ref, rf_ref, g_ref, b_ref, out_ref):
    subs = range(MERGE_SUB)
    col = lax.broadcasted_iota(I32, (TM, LOCAL_ROWS), 1).astype(F32)
    route = [rf_ref[:, s * TM:(s + 1) * TM].T for s in subs]
    unsort = [(jnp.where(col == r[:, 2:3], r[:, 0:1], 0.0)
               + jnp.where(col == r[:, 3:4], r[:, 1:2], 0.0)).astype(BF16) for r in route]
    ffn = [jnp.dot(unsort[s], ys_ref[s * LOCAL_ROWS:(s + 1) * LOCAL_ROWS, :], preferred_element_type=F32)
           for s in subs]
    for s in subs:
        h = ALPHA * x1_ref[s * TM:(s + 1) * TM, :] + ffn[s]
        mu = jnp.mean(h, axis=-1, keepdims=True)
        hc = h - mu
        var = jnp.mean(hc * hc, axis=-1, keepdims=True)
        out_ref[s * TM:(s + 1) * TM, :] = hc * lax.rsqrt(var + LN_EPS) * g_ref[...] + b_ref[...]


def _combine_call(ys, x1, rf, g, b):
    tm = MERGE_SUB * TM
    return pl.pallas_call(
        _combine_kernel,
        grid=(TOKENS // tm,),
        in_specs=[pl.BlockSpec((MERGE_SUB * LOCAL_ROWS, D_MODEL), lambda i: (i, 0)),
                  pl.BlockSpec((tm, D_MODEL), lambda i: (i, 0)),
                  pl.BlockSpec((8, tm), lambda i: (0, i)),
                  pl.BlockSpec((1, D_MODEL), lambda i: (0, 0)),
                  pl.BlockSpec((1, D_MODEL), lambda i: (0, 0))],
        out_specs=pl.BlockSpec((tm, D_MODEL), lambda i: (i, 0)),
        out_shape=jax.ShapeDtypeStruct((TOKENS, D_MODEL), F32),
        compiler_params=pltpu.CompilerParams(
            dimension_semantics=("arbitrary",), vmem_limit_bytes=VMEM_LIMIT),
        name="combine_ln2",
    )(ys, x1, rf, g, b)


def _router_cols(w_router_group, w_router_expert):
    w = jnp.concatenate([w_router_group, jnp.zeros((D_MODEL, 4), F32), w_router_expert,
                         jnp.zeros((D_MODEL, 128 - ROUTER_ROWS), F32)], axis=1)
    hi = w.astype(BF16)
    lo = (w - hi.astype(F32)).astype(BF16)
    return hi, lo


def _layer(x, w_in, conv_w, w_out_conv, w_out_attn, w_o, ln1_g, ln1_b,
           w_router_group, w_router_expert, w_gate, w_up, w_down, ln2_g, ln2_b):
    slopes = jnp.asarray([2.0 ** (-8.0 * (h + 1) / N_HEADS) for h in range(N_HEADS)], F32)
    q, k, v, za, sgb = _proj_call(x, w_in, conv_w, w_out_conv)
    o_a, o_b = _attn_call(slopes, q, k, v)

    wr_hi, wr_lo = _router_cols(w_router_group, w_router_expert)
    x1, xs, rf, mt = _merge_call(
        o_a, o_b, za.reshape(TOKENS, D_MODEL), sgb.reshape(TOKENS, D_MODEL),
        x.reshape(TOKENS, D_MODEL), w_out_attn, w_o,
        ln1_g.reshape(1, D_MODEL), ln1_b.reshape(1, D_MODEL), wr_hi, wr_lo)

    grans = mt.reshape(N_TOK_TILES, N_EXPERTS, 128)[:, :, 0].astype(I32)
    local_start = jnp.cumsum(grans, axis=1) - grans
    grans_t = grans.T
    tiles_e = (jnp.sum(grans_t, axis=1) + TILE_GRANS - 1) // TILE_GRANS
    tile_end = jnp.cumsum(tiles_e)
    n_steps = ((tile_end[-1] + EXP_SUB - 1) // EXP_SUB).reshape(1)
    all_tiles = (MAX_STEPS + NBUF) * EXP_SUB
    tile_ids = jnp.arange(all_tiles, dtype=I32)
    tile_expert = jnp.minimum(
        jnp.sum((tile_ids[:, None] >= tile_end[None, :]).astype(I32), axis=1), N_EXPERTS - 1)
    run_slot = TILE_GRANS * (tile_end - tiles_e)[:, None] + jnp.cumsum(grans_t, axis=1) - grans_t
    run_src = jnp.arange(N_TOK_TILES, dtype=I32)[None, :] * LOCAL_GRANS + local_start.T
    pick = (tile_expert[:all_tiles, None] == jnp.arange(N_EXPERTS, dtype=I32)[None, :])[:, :, None]
    t_slot = jnp.sum(jnp.where(pick, run_slot[None], 0), axis=1)
    t_len = jnp.sum(jnp.where(pick, grans_t[None], 0), axis=1)
    t_src = jnp.sum(jnp.where(pick, run_src[None], 0), axis=1)
    slots = jnp.arange(all_tiles * TILE_GRANS, dtype=I32).reshape(all_tiles, TILE_GRANS)
    k = slots[:, :, None] - t_slot[:, None, :]
    hit = (k >= 0) & (k < t_len[:, None, :])
    gran = jnp.sum(jnp.where(hit, t_src[:, None, :] + k, 0), axis=2).reshape(-1)
    filled = (jnp.sum(hit.astype(I32), axis=2) > 0).reshape(-1)
    slots = slots.reshape(-1)
    gsrc = jnp.where(filled, gran, 0)
    gdst = jnp.where(filled, gran, SPARE_GRAN + slots % (NBUF * STEP_GRANS))
    gdst = jnp.concatenate([SPARE_GRAN + (NBUF - 1) * STEP_GRANS + jnp.arange(STEP_GRANS, dtype=I32), gdst])

    ys = _expert_call(tile_expert, n_steps, gsrc, gdst, jnp.sum(grans, axis=1), xs, w_gate, w_up, w_down)
    out = _combine_call(ys, x1, rf, ln2_g.reshape(1, D_MODEL), ln2_b.reshape(1, D_MODEL))
    return out.reshape(BATCH, SEQ, D_MODEL)


def kernel(x, w_in, conv_w, w_out_conv, w_out_attn, w_o, ln1_g, ln1_b, w_router_group, w_router_expert, w_gate, w_up, w_down, ln2_g, ln2_b):
    depth = w_in.shape[0]
    for l in range(depth):
        x = _layer(x, w_in[l], conv_w[l], w_out_conv[l], w_out_attn[l], w_o[l], ln1_g[l], ln1_b[l],
                   w_router_group[l], w_router_expert[l], w_gate[l], w_up[l], w_down[l], ln2_g[l], ln2_b[l])
    return x
```

```python
import functools

import jax
import jax.numpy as jnp
from jax import lax
from jax.experimental import pallas as pl
from jax.experimental.pallas import tpu as pltpu

F32 = jnp.float32
BF16 = jnp.bfloat16
U32 = jnp.uint32
I32 = jnp.int32

D_MODEL = 1024
BATCH = 8
SEQ = 2048
TOKENS = BATCH * SEQ
CONV_WIDTH = 512
N_HEADS = 8
HEAD_DIM = 64
ATTN_WIDTH = N_HEADS * HEAD_DIM
MOBA_BLOCK = 256
N_BLOCKS = SEQ // MOBA_BLOCK
MOBA_TOPK = 3
N_GROUPS = 4
EXPERTS_PER_GROUP = 8
N_EXPERTS = N_GROUPS * EXPERTS_PER_GROUP
D_EXPERT = 256
LN_EPS = 1e-5
ALPHA = 2.0 ** 0.25
IN_COLS = 3 * CONV_WIDTH + 3 * ATTN_WIDTH + 2 * D_MODEL
HALF = D_MODEL // 2

TM = 256
TM_PROJ = 512
TE = 512
CHAIN_ROWS = 256
PV_ROWS = HEAD_DIM + 16
GRAN = 16
TILE_GRANS = TE // GRAN
N_TOK_TILES = TOKENS // TM
LOCAL_ROWS = -(-(2 * TM + N_EXPERTS * (GRAN - 1)) // 256) * 256
LOCAL_GRANS = LOCAL_ROWS // GRAN
SPARE_GRAN = N_TOK_TILES * LOCAL_GRANS
MAX_TILES = (2 * TOKENS + N_TOK_TILES * N_EXPERTS * (GRAN - 1)) // TE + N_EXPERTS
PAIRS = 2
MERGE_SUB = 2
EXP_SUB = 1
STEP_ROWS = EXP_SUB * TE
STEP_GRANS = EXP_SUB * TILE_GRANS
MAX_STEPS = -(-MAX_TILES // EXP_SUB)
NBUF = 4
ROUTER_ROWS = 40
VMEM_LIMIT = 56 * 1024 * 1024
NEG_INF = float("-inf")


def _sigmoid(z):
    return 1.0 / (1.0 + jnp.exp(-z))


def _proj_kernel(x_ref, w_in_hbm_ref, convw_ref, woc_f32_ref, q_ref, k_ref, v_ref, za_ref, sgb_ref,
                 ubuf, w_in_ref, woc_ref, stage, stage_sem):
    s = pl.program_id(1)
    tm = TM_PROJ

    @pl.when((pl.program_id(0) == 0) & (s == 0))
    def _():
        def chunk(c):
            return pltpu.make_async_copy(w_in_hbm_ref.at[:, c * CONV_WIDTH:(c + 1) * CONV_WIDTH],
                                         stage.at[c % 2], stage_sem.at[c % 2])

        n_chunks = IN_COLS // CONV_WIDTH
        chunk(0).start()
        for c in range(n_chunks):
            if c + 1 < n_chunks:
                chunk(c + 1).start()
            chunk(c).wait()
            w_in_ref[:, c * CONV_WIDTH:(c + 1) * CONV_WIDTH] = stage[c % 2].astype(BF16)
        woc_ref[...] = woc_f32_ref[...].astype(BF16)

    xb = x_ref[0].astype(BF16)

    def proj(c0, c1):
        return jnp.dot(xb, w_in_ref[:, c0:c1], preferred_element_type=F32)

    c_b = proj(0, CONV_WIDTH)
    u = proj(CONV_WIDTH, 2 * CONV_WIDTH) * proj(2 * CONV_WIDTH, 3 * CONV_WIDTH)

    @pl.when(s == 0)
    def _():
        ubuf[0:8, :] = jnp.zeros((8, CONV_WIDTH), F32)

    ubuf[8:8 + tm, :] = u
    w = convw_ref[...]
    conv = w[2:3, :] * u + w[1:2, :] * ubuf[7:7 + tm, :] + w[0:1, :] * ubuf[6:6 + tm, :]
    ubuf[0:8, :] = u[tm - 8:tm, :]
    hc = (c_b * conv).astype(BF16)
    y_conv = jnp.dot(hc, woc_ref[...], preferred_element_type=F32)

    o = 3 * CONV_WIDTH
    q_ref[0] = (proj(o, o + ATTN_WIDTH) * (HEAD_DIM ** -0.5)).astype(BF16)
    k_ref[0] = proj(o + ATTN_WIDTH, o + 2 * ATTN_WIDTH).astype(BF16)
    v_ref[0] = proj(o + 2 * ATTN_WIDTH, o + 3 * ATTN_WIDTH).astype(BF16)
    o += 3 * ATTN_WIDTH
    za_ref[0] = (_sigmoid(proj(o, o + D_MODEL)) * y_conv).astype(BF16)
    sgb_ref[0] = _sigmoid(proj(o + D_MODEL, o + 2 * D_MODEL)).astype(BF16)


def _proj_call(x, w_in, conv_w, w_out_conv):
    tok_spec = lambda c: pl.BlockSpec((1, TM_PROJ, c), lambda b, s: (b, s, 0))
    full = lambda shape: pl.BlockSpec(shape, lambda b, s: (0,) * len(shape))
    once = lambda shape: pl.BlockSpec(shape, lambda b, s: (0,) * len(shape), pipeline_mode=pl.Buffered(1))
    return pl.pallas_call(
        _proj_kernel,
        grid=(BATCH, SEQ // TM_PROJ),
        in_specs=[tok_spec(D_MODEL), pl.BlockSpec(memory_space=pl.ANY), full((3, CONV_WIDTH)),
                  once((CONV_WIDTH, D_MODEL))],
        out_specs=[tok_spec(ATTN_WIDTH), tok_spec(ATTN_WIDTH), tok_spec(ATTN_WIDTH),
                   tok_spec(D_MODEL), tok_spec(D_MODEL)],
        out_shape=[jax.ShapeDtypeStruct((BATCH, SEQ, ATTN_WIDTH), BF16)] * 3
        + [jax.ShapeDtypeStruct((BATCH, SEQ, D_MODEL), BF16)] * 2,
        scratch_shapes=[pltpu.VMEM((TM_PROJ + 8, CONV_WIDTH), F32), pltpu.VMEM((D_MODEL, IN_COLS), BF16),
                        pltpu.VMEM((CONV_WIDTH, D_MODEL), BF16),
                        pltpu.VMEM((2, D_MODEL, CONV_WIDTH), F32), pltpu.SemaphoreType.DMA((2,))],
        compiler_params=pltpu.CompilerParams(
            dimension_semantics=("arbitrary", "arbitrary"), vmem_limit_bytes=VMEM_LIMIT),
        name="proj",
    )(x, w_in, conv_w, w_out_conv)


def _attn_kernel(slopes_ref, qa_ref, qb_ref, k_ref, v_ref, oa_ref, ob_ref,
                 kaug_ref, vt_ref, kmean_ref, qaug_ref, pv_ref, mloc_ref, t_ref, p_ref):
    hq = pl.program_id(1)
    j = pl.program_id(2)
    blk = MOBA_BLOCK
    pairs = range(PAIRS)
    lanes = [slice(128 * pp, 128 * (pp + 1)) for pp in pairs]

    @pl.when(j == 0)
    def _():
        klane = lax.broadcasted_iota(I32, (blk, 128), 1)
        koff = lax.broadcasted_iota(I32, (blk, 128), 0).astype(F32)
        k_extra = jnp.where(klane == 0, koff, jnp.where(klane == 1, 1.0, 0.0)).astype(BF16)
        orow = lax.broadcasted_iota(I32, (PV_ROWS - HEAD_DIM, blk), 0)
        ones_rows = jnp.where(orow == 0, 1.0, 0.0).astype(BF16)
        for pp in pairs:
            for n in range(N_BLOCKS):
                kblk = k_ref[0, n * blk:(n + 1) * blk, lanes[pp]]
                kaug_ref[pp, n, :, 0:128] = kblk
                kaug_ref[pp, n, :, 128:256] = k_extra
                kmean_ref[pp, n:n + 1, :] = jnp.mean(kblk.astype(F32), axis=0, keepdims=True)
                v_t = v_ref[0, n * blk:(n + 1) * blk, lanes[pp]].astype(F32).T.astype(BF16)
                for hh in range(2):
                    vt_ref[pp, n, hh, 0:HEAD_DIM, :] = v_t[hh * HEAD_DIM:(hh + 1) * HEAD_DIM, :]
                    vt_ref[pp, n, hh, HEAD_DIM:PV_ROWS, :] = ones_rows

    lane = lax.broadcasted_iota(I32, (1, 2 * blk), 1)
    qoff_row = jnp.where(lane < blk, lane, lane - blk).astype(F32)
    feat = lax.broadcasted_iota(I32, (2 * HEAD_DIM, blk), 0)
    arow = lax.broadcasted_iota(I32, (2 * HEAD_DIM, 2 * blk), 0)
    blk_i = lax.broadcasted_iota(I32, (N_BLOCKS, 2 * blk), 0)
    key_i = lax.broadcasted_iota(I32, (blk, 2 * blk), 0)
    qry_j = lax.broadcasted_iota(I32, (blk, 2 * blk), 1)
    causal = key_i <= jnp.where(qry_j < blk, qry_j, qry_j - blk)
    slope_rows, q_extras = [], []
    for pp in pairs:
        head = 2 * (PAIRS * hq + pp)
        slope_rows.append(jnp.where(lane < blk, slopes_ref[head], slopes_ref[head + 1]))
        q_extras.append(jnp.where(arow == 0, slope_rows[pp],
                                  jnp.where(arow == 1, -slope_rows[pp] * qoff_row, 0.0)).astype(BF16))

    def prepare(q_ref, pp, slot, qblock):
        q_t = q_ref[0, :, lanes[pp]].astype(F32).T
        qcat = jnp.concatenate([jnp.where(feat < HEAD_DIM, q_t, 0.0), jnp.where(feat >= HEAD_DIM, q_t, 0.0)],
                               axis=1).astype(BF16)
        qaug_ref[pp, slot, 0:2 * HEAD_DIM, :] = qcat
        qaug_ref[pp, slot, 2 * HEAD_DIM:4 * HEAD_DIM, :] = q_extras[pp]
        gate = jnp.dot(kmean_ref[pp].astype(BF16), qcat, preferred_element_type=F32)
        cnt = jnp.zeros((N_BLOCKS, 2 * blk), F32)
        for m in range(N_BLOCKS):
            gm = gate[m:m + 1, :]
            beats = (gm > gate) | ((gm == gate) & (blk_i > m))
            cnt = cnt + jnp.where(beats & (qblock > m), 1.0, 0.0)
        return jnp.where((blk_i < qblock) & (cnt < float(MOBA_TOPK)), 1.0, 0.0)

    qblock_a = j
    qblock_b = N_BLOCKS - 1 - j
    sel_a = [prepare(qa_ref, pp, 0, qblock_a) for pp in pairs]
    sel_b = [prepare(qb_ref, pp, 1, qblock_b) for pp in pairs]

    n_mid = N_BLOCKS - 1
    slots = [(0, 0, qblock_a, True)]
    mids = []
    for s in range(1, n_mid + 1):
        is_a = s <= j
        slots.append((s, jnp.where(is_a, 0, 1), jnp.where(is_a, s - 1, s - 1 - j), False))
        mids.append((is_a, slots[-1][2]))
    slots.append((n_mid + 1, 1, qblock_b, True))

    for pp in pairs:
        for s, which, kb, _ in slots:
            t_ref[pp, s] = jnp.dot(kaug_ref[pp, kb], qaug_ref[pp, which], preferred_element_type=F32)
    for pp in pairs:
        for s, _, _, own in slots:
            t = t_ref[pp, s]
            if own:
                t = jnp.where(causal, t, NEG_INF)
            m_loc = jnp.max(t, axis=0, keepdims=True)
            p_ref[pp, s] = jnp.exp((t - m_loc).astype(BF16))
            mloc_ref[pp, s:s + 1, :] = m_loc
    for pp in pairs:
        for s, _, kb, _ in slots:
            pv_ref[pp, s, 0] = jnp.dot(vt_ref[pp, kb, 0], p_ref[pp, s, :, 0:blk], preferred_element_type=F32)
            pv_ref[pp, s, 1] = jnp.dot(vt_ref[pp, kb, 1], p_ref[pp, s, :, blk:2 * blk],
                                       preferred_element_type=F32)

    def combine(o_ref, pp, own_slot, sel, qblock, mine):
        neg = jnp.full((1, 2 * blk), -1e30, F32)
        pieces = [(own_slot, mloc_ref[pp, own_slot:own_slot + 1, :])]
        for s, (is_a, kb) in enumerate(mids, start=1):
            selrow = jnp.sum(jnp.where(blk_i == kb, sel, 0.0), axis=0, keepdims=True)
            belongs = jnp.where(is_a, 1.0, 0.0) if mine else jnp.where(is_a, 0.0, 1.0)
            used = selrow * belongs > 0.5
            shift = slope_rows[pp] * ((kb - qblock) * blk).astype(F32)
            pieces.append((s, jnp.where(used, mloc_ref[pp, s:s + 1, :] + shift, neg)))
        m_all = pieces[0][1]
        for _, m_s in pieces[1:]:
            m_all = jnp.maximum(m_all, m_s)
        acc = [jnp.zeros((PV_ROWS, blk), F32), jnp.zeros((PV_ROWS, blk), F32)]
        for s, m_s in pieces:
            w = jnp.exp(m_s - m_all)
            for hh in range(2):
                acc[hh] = acc[hh] + pv_ref[pp, s, hh] * w[:, hh * blk:(hh + 1) * blk]
        o_t = jnp.concatenate([a[0:HEAD_DIM, :] / a[HEAD_DIM:HEAD_DIM + 1, :] for a in acc], axis=0)
        o_ref[0, :, lanes[pp]] = o_t.T.astype(BF16)

    for pp in pairs:
        combine(oa_ref, pp, 0, sel_a[pp], qblock_a, True)
        combine(ob_ref, pp, n_mid + 1, sel_b[pp], qblock_b, False)


def _attn_call(slopes, q, k, v):
    half = N_BLOCKS // 2
    width = 128 * PAIRS
    o_a, o_b = pl.pallas_call(
        _attn_kernel,
        grid_spec=pltpu.PrefetchScalarGridSpec(
            num_scalar_prefetch=1,
            grid=(BATCH, N_HEADS // (2 * PAIRS), half),
            in_specs=[
                pl.BlockSpec((1, MOBA_BLOCK, width), lambda b, h, j, sl: (b, j, h)),
                pl.BlockSpec((1, MOBA_BLOCK, width), lambda b, h, j, sl: (b, N_BLOCKS - 1 - j, h)),
                pl.BlockSpec((1, SEQ, width), lambda b, h, j, sl: (b, 0, h)),
                pl.BlockSpec((1, SEQ, width), lambda b, h, j, sl: (b, 0, h)),
            ],
            out_specs=[pl.BlockSpec((1, MOBA_BLOCK, width), lambda b, h, j, sl: (b, j, h)),
                       pl.BlockSpec((1, MOBA_BLOCK, width), lambda b, h, j, sl: (b, half - 1 - j, h))],
            scratch_shapes=[
                pltpu.VMEM((PAIRS, N_BLOCKS, MOBA_BLOCK, 256), BF16),
                pltpu.VMEM((PAIRS, N_BLOCKS, 2, PV_ROWS, MOBA_BLOCK), BF16),
                pltpu.VMEM((PAIRS, N_BLOCKS, 128), F32),
                pltpu.VMEM((PAIRS, 2, 256, 2 * MOBA_BLOCK), BF16),
                pltpu.VMEM((PAIRS, N_BLOCKS + 1, 2, PV_ROWS, MOBA_BLOCK), F32),
                pltpu.VMEM((PAIRS, 16, 2 * MOBA_BLOCK), F32),
                pltpu.VMEM((PAIRS, N_BLOCKS + 1, MOBA_BLOCK, 2 * MOBA_BLOCK), F32),
                pltpu.VMEM((PAIRS, N_BLOCKS + 1, MOBA_BLOCK, 2 * MOBA_BLOCK), BF16),
            ],
        ),
        out_shape=[jax.ShapeDtypeStruct((BATCH, SEQ // 2, ATTN_WIDTH), BF16)] * 2,
        compiler_params=pltpu.CompilerParams(
            dimension_semantics=("arbitrary", "arbitrary", "arbitrary"), vmem_limit_bytes=VMEM_LIMIT),
        name="moba_attn",
    )(slopes, q, q, k, v)
    return o_a, o_b


def _route(logits):
    row8 = lax.broadcasted_iota(I32, (8, TM), 0).astype(F32)
    gl = jnp.where(row8 < float(N_GROUPS), logits[0:8, :], NEG_INF)
    gexp = jnp.exp(gl - jnp.max(gl, axis=0, keepdims=True))
    gprob = gexp / jnp.sum(gexp, axis=0, keepdims=True)
    ptop = jnp.max(gprob, axis=0, keepdims=True)
    gtop = jnp.min(jnp.where(gprob == ptop, row8, 8.0), axis=0, keepdims=True)
    el = logits[8:ROUTER_ROWS, :]
    eg = jnp.where(gtop == 0.0, el[0:8, :],
                   jnp.where(gtop == 1.0, el[8:16, :], jnp.where(gtop == 2.0, el[16:24, :], el[24:32, :])))
    m1 = jnp.max(eg, axis=0, keepdims=True)
    i1 = jnp.min(jnp.where(eg == m1, row8, 8.0), axis=0, keepdims=True)
    eg2 = jnp.where(row8 == i1, NEG_INF, eg)
    m2 = jnp.max(eg2, axis=0, keepdims=True)
    i2 = jnp.min(jnp.where(eg2 == m2, row8, 8.0), axis=0, keepdims=True)
    t2 = jnp.exp(m2 - m1)
    gate1 = ptop * (1.0 / (1.0 + t2))
    gate2 = ptop * (t2 / (1.0 + t2))
    erow = lax.broadcasted_iota(I32, (N_EXPERTS, TM), 0).astype(F32)
    oh1 = jnp.where(erow == gtop * float(EXPERTS_PER_GROUP) + i1, 1.0, 0.0)
    oh2 = jnp.where(erow == gtop * float(EXPERTS_PER_GROUP) + i2, 1.0, 0.0)
    return gate1, gate2, oh1, oh2


def _merge_kernel(oa_ref, ob_ref, za_ref, sgb_ref, x_ref, woa_f32_ref, wo_f32_ref, g_ref, b_ref,
                  wr_hi_ref, wr_lo_ref, x1_ref, xs_ref, rf_ref, mt_ref, woa_ref, wo_ref):
    i = pl.program_id(0)

    @pl.when(i == 0)
    def _():
        woa_ref[...] = woa_f32_ref[...].astype(BF16)
        wo_ref[...] = wo_f32_ref[...].astype(BF16)

    subs = range(MERGE_SUB)
    rows = [slice(s * TM, (s + 1) * TM) for s in subs]
    steps_per_batch = N_BLOCKS // MERGE_SUB
    in_oa = lax.rem(i, steps_per_batch) < steps_per_batch // 2
    o = [jnp.where(in_oa, oa_ref[0, r, :], ob_ref[0, r, :]) for r in rows]
    y_attn = [jnp.dot(o[s], woa_ref[...], preferred_element_type=F32) for s in subs]
    y = [(za_ref[rows[s], :].astype(F32) + sgb_ref[rows[s], :].astype(F32) * y_attn[s]).astype(BF16)
         for s in subs]
    mix = [jnp.dot(y[s], wo_ref[...], preferred_element_type=F32) for s in subs]
    x1 = []
    for s in subs:
        h = ALPHA * x_ref[rows[s], :] + mix[s]
        mu = jnp.mean(h, axis=-1, keepdims=True)
        hc = h - mu
        var = jnp.mean(hc * hc, axis=-1, keepdims=True)
        x1.append(hc * lax.rsqrt(var + LN_EPS) * g_ref[...] + b_ref[...])
        x1_ref[rows[s], :] = x1[s]

    xh = [x1[s].astype(BF16) for s in subs]
    xl = [(x1[s] - xh[s].astype(F32)).astype(BF16) for s in subs]
    wh = wr_hi_ref[...]
    logits = [(jnp.dot(xh[s], wh, preferred_element_type=F32)
               + jnp.dot(xl[s], wh, preferred_element_type=F32)
               + jnp.dot(xh[s], wr_lo_ref[...], preferred_element_type=F32)).T for s in subs]
    routes = [_route(logits[s]) for s in subs]

    ta = lax.broadcasted_iota(I32, (TM, TM), 0)
    tb = lax.broadcasted_iota(I32, (TM, TM), 1)
    upper = jnp.where(ta < tb, 1.0, 0.0).astype(BF16)
    ea = lax.broadcasted_iota(I32, (N_EXPERTS, N_EXPERTS), 0)
    eb = lax.broadcasted_iota(I32, (N_EXPERTS, N_EXPERTS), 1)
    lower = jnp.where(eb < ea, 1.0, 0.0).astype(BF16)
    lrow = lax.broadcasted_iota(I32, (LOCAL_ROWS, TM), 0).astype(F32)
    zero = jnp.zeros((1, TM), F32)
    cum = [jnp.dot((routes[s][2] + routes[s][3]).astype(BF16), upper, preferred_element_type=F32) for s in subs]
    perm = []
    for s in subs:
        gate1, gate2, oh1, oh2 = routes[s]
        n_e = jnp.sum(oh1 + oh2, axis=1, keepdims=True)
        m_rep = jnp.broadcast_to(jnp.floor((n_e + float(GRAN - 1)) * (1.0 / GRAN)), (N_EXPERTS, 128))
        run_start = jnp.dot(lower, m_rep.astype(BF16), preferred_element_type=F32)
        tot = cum[s] + float(GRAN) * run_start[:, 0:1]
        lp1 = jnp.sum(oh1 * tot, axis=0, keepdims=True)
        lp2 = jnp.sum(oh2 * tot, axis=0, keepdims=True)
        perm.append(jnp.where((lrow == lp1) | (lrow == lp2), 1.0, 0.0).astype(BF16))
        rf_ref[:, rows[s]] = jnp.concatenate([gate1, gate2, lp1, lp2, zero, zero, zero, zero], axis=0)
        mt_ref[s * N_EXPERTS:(s + 1) * N_EXPERTS, :] = m_rep
    for s in subs:
        xs_ref[s * LOCAL_ROWS:(s + 1) * LOCAL_ROWS, :] = jnp.dot(
            perm[s], xh[s], preferred_element_type=F32).astype(BF16)


def _merge_call(o_a, o_b, za, sgb, x, woa, wo, g, b, wr_hi, wr_lo):
    tm = MERGE_SUB * TM
    tok = lambda c: pl.BlockSpec((tm, c), lambda i: (i, 0))
    full = lambda shape: pl.BlockSpec(shape, lambda i: (0,) * len(shape))
    per_batch = SEQ // tm
    half = per_batch // 2
    o_a_spec = pl.BlockSpec((1, tm, ATTN_WIDTH), lambda i: (i // per_batch, jnp.minimum(i % per_batch, half - 1), 0))
    o_b_spec = pl.BlockSpec((1, tm, ATTN_WIDTH), lambda i: (i // per_batch, jnp.maximum(i % per_batch - half, 0), 0))
    return pl.pallas_call(
        _merge_kernel,
        grid=(TOKENS // tm,),
        in_specs=[o_a_spec, o_b_spec, tok(D_MODEL), tok(D_MODEL), tok(D_MODEL),
                  full((ATTN_WIDTH, D_MODEL)), full((D_MODEL, D_MODEL)), full((1, D_MODEL)),
                  full((1, D_MODEL)), full((D_MODEL, 128)), full((D_MODEL, 128))],
        out_specs=[tok(D_MODEL), pl.BlockSpec((MERGE_SUB * LOCAL_ROWS, D_MODEL), lambda i: (i, 0)),
                   pl.BlockSpec((8, tm), lambda i: (0, i)),
                   pl.BlockSpec((MERGE_SUB * N_EXPERTS, 128), lambda i: (i, 0))],
        out_shape=[jax.ShapeDtypeStruct((TOKENS, D_MODEL), F32),
                   jax.ShapeDtypeStruct((N_TOK_TILES * LOCAL_ROWS, D_MODEL), BF16),
                   jax.ShapeDtypeStruct((8, TOKENS), F32),
                   jax.ShapeDtypeStruct((N_TOK_TILES * N_EXPERTS, 128), F32)],
        scratch_shapes=[pltpu.VMEM((ATTN_WIDTH, D_MODEL), BF16), pltpu.VMEM((D_MODEL, D_MODEL), BF16)],
        compiler_params=pltpu.CompilerParams(
            dimension_semantics=("arbitrary",), vmem_limit_bytes=VMEM_LIMIT),
        name="merge_ln1_route",
    )(o_a, o_b, za, sgb, x, woa, wo, g, b, wr_hi, wr_lo)


def _granule_copy(src_ref, src_gran, dst_ref, dst_gran, sem):
    src = pl.multiple_of(src_gran * GRAN, GRAN)
    dst = pl.multiple_of(dst_gran * GRAN, GRAN)
    return pltpu.make_async_copy(src_ref.at[pl.ds(src, GRAN), :], dst_ref.at[pl.ds(dst, GRAN), :], sem)


def _expert_kernel(te_ref, nt_ref, gsrc_ref, gdst_ref, ug_ref, wpar_ref, wnext_ref,
                   xs_ref, wg_hbm_ref, wu_hbm_ref, wd_hbm_ref,
                   ys_ref, xbuf, ybuf, zbuf, wg_ref, wu_ref, wd_ref, wg_stage, wu_stage, wd_stage,
                   in_sem, out_sem, zero_sem, w_sem):
    j = pl.program_id(0)
    n_tiles = nt_ref[0]
    slot = lax.rem(j, NBUF)
    prev_slot = lax.rem(j + NBUF - 1, NBUF)

    def tile_gather(step, s):
        for g in range(STEP_GRANS):
            _granule_copy(xs_ref, gsrc_ref[step * STEP_GRANS + g], xbuf.at[s], g,
                          in_sem.at[s]).start(priority=g % 2)

    def prev_scatter():
        for g in range(STEP_GRANS):
            _granule_copy(ybuf.at[prev_slot], g, ys_ref, gdst_ref[j * STEP_GRANS + g],
                          out_sem.at[prev_slot]).start(priority=g % 2)

    @pl.when(j == 0)
    def _():
        tile_gather(0, 0)
        ybuf[NBUF - 1] = jnp.zeros((STEP_ROWS, D_MODEL), BF16)
        zbuf[...] = jnp.zeros((GRAN, D_MODEL), BF16)
        for part in range(NBUF):
            spare = pltpu.make_async_copy(
                ybuf.at[NBUF - 1], ys_ref.at[pl.ds((SPARE_GRAN + part * STEP_GRANS) * GRAN, STEP_ROWS), :],
                out_sem.at[NBUF - 1])
            spare.start()
            spare.wait()

        for ahead in range(1, NBUF - 1):
            tile_gather(ahead, ahead)

    def zero_copy(t, g):
        return _granule_copy(zbuf, 0, ys_ref, t * LOCAL_GRANS + g, zero_sem)

    @pl.when(jnp.logical_and(j >= 1, j <= N_TOK_TILES))
    def _():
        def wait(g, c):
            zero_copy(j - 1, g).wait()
            return c

        lax.fori_loop(ug_ref[j - 1], LOCAL_GRANS, wait, 0)

    @pl.when(j < N_TOK_TILES)
    def _():
        def start(g, c):
            zero_copy(j, g).start()
            return c

        lax.fori_loop(ug_ref[j], LOCAL_GRANS, start, 0)

    def gather_wait():
        pltpu.make_async_copy(xs_ref.at[pl.ds(0, STEP_ROWS), :], xbuf.at[slot], in_sem.at[slot]).wait()

    @pl.when(jnp.logical_and(j >= NBUF - 1, j - NBUF < n_tiles))
    def _():
        pltpu.make_async_copy(ybuf.at[slot], ys_ref.at[pl.ds(0, STEP_ROWS), :], out_sem.at[slot]).wait()

    @pl.when(jnp.logical_and(j >= n_tiles, j < n_tiles + NBUF - 1))
    def _():
        gather_wait()

    @pl.when(j == n_tiles)
    def _():
        prev_scatter()

    def weight_copies(expert, s):
        return [pltpu.make_async_copy(hbm.at[expert], stage.at[s], w_sem.at[s])
                for hbm, stage in ((wg_hbm_ref, wg_stage), (wu_hbm_ref, wu_stage), (wd_hbm_ref, wd_stage))]

    @pl.when(j == 0)
    def _():
        for cp in weight_copies(te_ref[0], 0):
            cp.start()

    @pl.when(jnp.logical_and(j < n_tiles, jnp.logical_or(j == 0, te_ref[j] != te_ref[jnp.maximum(j - 1, 0)])))
    def _():
        s = wpar_ref[j]
        for cp in weight_copies(te_ref[j], s):
            cp.wait()
        wg_ref[...] = wg_stage[s].astype(BF16)
        wu_ref[...] = wu_stage[s].astype(BF16)
        wd_ref[...] = wd_stage[s].astype(BF16)
        nxt = wnext_ref[te_ref[j]]

        @pl.when(nxt >= 0)
        def _():
            for cp in weight_copies(nxt, 1 - s):
                cp.start()

    @pl.when(j < n_tiles)
    def _():
        gather_wait()
        chains = range(0, TE, CHAIN_ROWS)
        xb = [xbuf[slot, r:r + CHAIN_ROWS, :] for r in chains]
        hg = [jnp.dot(x, wg_ref[...], preferred_element_type=F32) for x in xb]
        hu = [jnp.dot(x, wu_ref[...], preferred_element_type=F32) for x in xb]
        tile_gather(j + NBUF - 1, prev_slot)
        prev_scatter()
        h = [(a * _sigmoid(a) * b).astype(BF16) for a, b in zip(hg, hu)]
        for hc, r in zip(h, chains):
            ybuf[slot, r:r + CHAIN_ROWS, :] = jnp.dot(hc, wd_ref[...], preferred_element_type=F32).astype(BF16)


def _expert_call(tile_expert, n_steps, gsrc, gdst, used_grans, w_parity, w_next, xs, wg, wu, wd):
    hbm = pl.BlockSpec(memory_space=pl.ANY)
    return pl.pallas_call(
        _expert_kernel,
        grid_spec=pltpu.PrefetchScalarGridSpec(
            num_scalar_prefetch=7,
            grid=(MAX_STEPS + NBUF,),
            in_specs=[hbm, hbm, hbm, hbm],
            out_specs=hbm,
            scratch_shapes=[pltpu.VMEM((NBUF, STEP_ROWS, D_MODEL), BF16),
                            pltpu.VMEM((NBUF, STEP_ROWS, D_MODEL), BF16),
                            pltpu.VMEM((GRAN, D_MODEL), BF16),
                            pltpu.VMEM((D_MODEL, D_EXPERT), BF16), pltpu.VMEM((D_MODEL, D_EXPERT), BF16),
                            pltpu.VMEM((D_EXPERT, D_MODEL), BF16),
                            pltpu.VMEM((2, D_MODEL, D_EXPERT), F32), pltpu.VMEM((2, D_MODEL, D_EXPERT), F32),
                            pltpu.VMEM((2, D_EXPERT, D_MODEL), F32),
                            pltpu.SemaphoreType.DMA((NBUF,)), pltpu.SemaphoreType.DMA((NBUF,)),
                            pltpu.SemaphoreType.DMA, pltpu.SemaphoreType.DMA((2,))],
        ),
        out_shape=jax.ShapeDtypeStruct(((SPARE_GRAN + NBUF * STEP_GRANS) * GRAN, D_MODEL), BF16),
        compiler_params=pltpu.CompilerParams(
            dimension_semantics=("arbitrary",), vmem_limit_bytes=VMEM_LIMIT),
        name="experts",
    )(tile_expert, n_steps, gsrc, gdst, used_grans, w_parity, w_next, xs, wg, wu, wd)


def _combine_kernel(ys_ref, x1_ref, rf_ref, g_ref, b_ref, out_ref):
    subs = range(MERGE_SUB)
    col = lax.broadcasted_iota(I32, (TM, LOCAL_ROWS), 1).astype(F32)
    route = [rf_ref[:, s * TM:(s + 1) * TM].T for s in subs]
    unsort = [(jnp.where(col == r[:, 2:3], r[:, 0:1], 0.0)
               + jnp.where(col == r[:, 3:4], r[:, 1:2], 0.0)).astype(BF16) for r in route]
    ffn = [jnp.dot(unsort[s], ys_ref[s * LOCAL_ROWS:(s + 1) * LOCAL_ROWS, :], preferred_element_type=F32)
           for s in subs]
    for s in subs:
        h = ALPHA * x1_ref[s * TM:(s + 1) * TM, :] + ffn[s]
        mu = jnp.mean(h, axis=-1, keepdims=True)
        hc = h - mu
        var = jnp.mean(hc * hc, axis=-1, keepdims=True)
        out_ref[s * TM:(s + 1) * TM, :] = hc * lax.rsqrt(var + LN_EPS) * g_ref[...] + b_ref[...]


def _combine_call(ys, x1, rf, g, b):
    tm = MERGE_SUB * TM
    return pl.pallas_call(
        _combine_kernel,
        grid=(TOKENS // tm,),
        in_specs=[pl.BlockSpec((MERGE_SUB * LOCAL_ROWS, D_MODEL), lambda i: (i, 0)),
                  pl.BlockSpec((tm, D_MODEL), lambda i: (i, 0)),
                  pl.BlockSpec((8, tm), lambda i: (0, i)),
                  pl.BlockSpec((1, D_MODEL), lambda i: (0, 0)),
                  pl.BlockSpec((1, D_MODEL), lambda i: (0, 0))],
        out_specs=pl.BlockSpec((tm, D_MODEL), lambda i: (i, 0)),
        out_shape=jax.ShapeDtypeStruct((TOKENS, D_MODEL), F32),
        compiler_params=pltpu.CompilerParams(
            dimension_semantics=("arbitrary",), vmem_limit_bytes=VMEM_LIMIT),
        name="combine_ln2",
    )(ys, x1, rf, g, b)


def _router_cols(w_router_group, w_router_expert):
    w = jnp.concatenate([w_router_group, jnp.zeros((D_MODEL, 4), F32), w_router_expert,
                         jnp.zeros((D_MODEL, 128 - ROUTER_ROWS), F32)], axis=1)
    hi = w.astype(BF16)
    lo = (w - hi.astype(F32)).astype(BF16)
    return hi, lo


def _layer(x, w_in, conv_w, w_out_conv, w_out_attn, w_o, ln1_g, ln1_b,
           w_router_group, w_router_expert, w_gate, w_up, w_down, ln2_g, ln2_b):
    slopes = jnp.asarray([2.0 ** (-8.0 * (h + 1) / N_HEADS) for h in range(N_HEADS)], F32)
    q, k, v, za, sgb = _proj_call(x, w_in, conv_w, w_out_conv)
    o_a, o_b = _attn_call(slopes, q, k, v)

    wr_hi, wr_lo = _router_cols(w_router_group, w_router_expert)
    x1, xs, rf, mt = _merge_call(
        o_a, o_b, za.reshape(TOKENS, D_MODEL), sgb.reshape(TOKENS, D_MODEL),
        x.reshape(TOKENS, D_MODEL), w_out_attn, w_o,
        ln1_g.reshape(1, D_MODEL), ln1_b.reshape(1, D_MODEL), wr_hi, wr_lo)

    grans = mt.reshape(N_TOK_TILES, N_EXPERTS, 128)[:, :, 0].astype(I32)
    local_start = jnp.cumsum(grans, axis=1) - grans
    grans_t = grans.T
    tiles_e = (jnp.sum(grans_t, axis=1) + TILE_GRANS - 1) // TILE_GRANS
    tile_end = jnp.cumsum(tiles_e)
    n_steps = ((tile_end[-1] + EXP_SUB - 1) // EXP_SUB).reshape(1)
    all_tiles = (MAX_STEPS + NBUF) * EXP_SUB
    tile_ids = jnp.arange(all_tiles, dtype=I32)
    tile_expert = jnp.minimum(
        jnp.sum((tile_ids[:, None] >= tile_end[None, :]).astype(I32), axis=1), N_EXPERTS - 1)
    run_slot = TILE_GRANS * (tile_end - tiles_e)[:, None] + jnp.cumsum(grans_t, axis=1) - grans_t
    run_src = jnp.arange(N_TOK_TILES, dtype=I32)[None, :] * LOCAL_GRANS + local_start.T
    pick = (tile_expert[:all_tiles, None] == jnp.arange(N_EXPERTS, dtype=I32)[None, :])[:, :, None]
    t_slot = jnp.sum(jnp.where(pick, run_slot[None], 0), axis=1)
    t_len = jnp.sum(jnp.where(pick, grans_t[None], 0), axis=1)
    t_src = jnp.sum(jnp.where(pick, run_src[None], 0), axis=1)
    slots = jnp.arange(all_tiles * TILE_GRANS, dtype=I32).reshape(all_tiles, TILE_GRANS)
    k = slots[:, :, None] - t_slot[:, None, :]
    hit = (k >= 0) & (k < t_len[:, None, :])
    gran = jnp.sum(jnp.where(hit, t_src[:, None, :] + k, 0), axis=2).reshape(-1)
    filled = (jnp.sum(hit.astype(I32), axis=2) > 0).reshape(-1)
    slots = slots.reshape(-1)
    gsrc = jnp.where(filled, gran, 0)
    gdst = jnp.where(filled, gran, SPARE_GRAN + slots % (NBUF * STEP_GRANS))
    gdst = jnp.concatenate([SPARE_GRAN + (NBUF - 1) * STEP_GRANS + jnp.arange(STEP_GRANS, dtype=I32), gdst])

    starts = jnp.concatenate([jnp.ones((1,), I32), (tile_expert[1:] != tile_expert[:-1]).astype(I32)])
    w_parity = (jnp.cumsum(starts) - 1) % 2
    ids = jnp.arange(N_EXPERTS, dtype=I32)
    later = jnp.where((tiles_e > 0)[None, :] & (ids[None, :] > ids[:, None]), ids[None, :], N_EXPERTS)
    w_next = jnp.min(later, axis=1)
    w_next = jnp.where(w_next == N_EXPERTS, -1, w_next)

    ys = _expert_call(tile_expert, n_steps, gsrc, gdst, jnp.sum(grans, axis=1), w_parity, w_next, xs,
                      w_gate, w_up, w_down)
    out = _combine_call(ys, x1, rf, ln2_g.reshape(1, D_MODEL), ln2_b.reshape(1, D_MODEL))
    return out.reshape(BATCH, SEQ, D_MODEL)


def kernel(x, w_in, conv_w, w_out_conv, w_out_attn, w_o, ln1_g, ln1_b, w_router_group, w_router_expert, w_gate, w_up, w_down, ln2_g, ln2_b):
    depth = w_in.shape[0]
    for l in range(depth):
        x = _layer(x, w_in[l], conv_w[l], w_out_conv[l], w_out_attn[l], w_o[l], ln1_g[l], ln1_b[l],
                   w_router_group[l], w_router_expert[l], w_gate[l], w_up[l], w_down[l], ln2_g[l], ln2_b[l])
    return x
```

```python
import functools

import jax
import jax.numpy as jnp
from jax import lax
from jax.experimental import pallas as pl
from jax.experimental.pallas import tpu as pltpu

F32 = jnp.float32
BF16 = jnp.bfloat16
U32 = jnp.uint32
I32 = jnp.int32

D_MODEL = 1024
BATCH = 8
SEQ = 2048
TOKENS = BATCH * SEQ
CONV_WIDTH = 512
N_HEADS = 8
HEAD_DIM = 64
ATTN_WIDTH = N_HEADS * HEAD_DIM
MOBA_BLOCK = 256
N_BLOCKS = SEQ // MOBA_BLOCK
MOBA_TOPK = 3
N_GROUPS = 4
EXPERTS_PER_GROUP = 8
N_EXPERTS = N_GROUPS * EXPERTS_PER_GROUP
D_EXPERT = 256
LN_EPS = 1e-5
ALPHA = 2.0 ** 0.25
IN_COLS = 3 * CONV_WIDTH + 3 * ATTN_WIDTH + 2 * D_MODEL
HALF = D_MODEL // 2

TM = 256
TM_PROJ = 512
TE = 512
CHAIN_ROWS = 256
PV_ROWS = HEAD_DIM + 16
GRAN = 8
TILE_GRANS = TE // GRAN
N_TOK_TILES = TOKENS // TM
LOCAL_ROWS = -(-(2 * TM + N_EXPERTS * (GRAN - 1)) // 256) * 256
LOCAL_GRANS = LOCAL_ROWS // GRAN
SPARE_GRAN = N_TOK_TILES * LOCAL_GRANS
MAX_TILES = (2 * TOKENS + N_TOK_TILES * N_EXPERTS * (GRAN - 1)) // TE + N_EXPERTS
PAIRS = 2
MERGE_SUB = 2
EXP_SUB = 1
STEP_ROWS = EXP_SUB * TE
STEP_GRANS = EXP_SUB * TILE_GRANS
MAX_STEPS = -(-MAX_TILES // EXP_SUB)
NBUF = 4
ROUTER_ROWS = 40
VMEM_LIMIT = 56 * 1024 * 1024
NEG_INF = float("-inf")


def _sigmoid(z):
    return 1.0 / (1.0 + jnp.exp(-z))


def _proj_kernel(x_ref, w_in_hbm_ref, convw_ref, woc_f32_ref, q_ref, k_ref, v_ref, za_ref, sgb_ref,
                 ubuf, w_in_ref, woc_ref, stage, stage_sem):
    s = pl.program_id(1)
    tm = TM_PROJ

    @pl.when((pl.program_id(0) == 0) & (s == 0))
    def _():
        def chunk(c):
            return pltpu.make_async_copy(w_in_hbm_ref.at[:, c * CONV_WIDTH:(c + 1) * CONV_WIDTH],
                                         stage.at[c % 2], stage_sem.at[c % 2])

        n_chunks = IN_COLS // CONV_WIDTH
        chunk(0).start()
        for c in range(n_chunks):
            if c + 1 < n_chunks:
                chunk(c + 1).start()
            chunk(c).wait()
            w_in_ref[:, c * CONV_WIDTH:(c + 1) * CONV_WIDTH] = stage[c % 2].astype(BF16)
        woc_ref[...] = woc_f32_ref[...].astype(BF16)

    xb = x_ref[0].astype(BF16)

    def proj(c0, c1):
        return jnp.dot(xb, w_in_ref[:, c0:c1], preferred_element_type=F32)

    c_b = proj(0, CONV_WIDTH)
    u = proj(CONV_WIDTH, 2 * CONV_WIDTH) * proj(2 * CONV_WIDTH, 3 * CONV_WIDTH)

    @pl.when(s == 0)
    def _():
        ubuf[0:8, :] = jnp.zeros((8, CONV_WIDTH), F32)

    ubuf[8:8 + tm, :] = u
    w = convw_ref[...]
    conv = w[2:3, :] * u + w[1:2, :] * ubuf[7:7 + tm, :] + w[0:1, :] * ubuf[6:6 + tm, :]
    ubuf[0:8, :] = u[tm - 8:tm, :]
    hc = (c_b * conv).astype(BF16)
    y_conv = jnp.dot(hc, woc_ref[...], preferred_element_type=F32)

    o = 3 * CONV_WIDTH
    q_ref[0] = (proj(o, o + ATTN_WIDTH) * (HEAD_DIM ** -0.5)).astype(BF16)
    k_ref[0] = proj(o + ATTN_WIDTH, o + 2 * ATTN_WIDTH).astype(BF16)
    v_ref[0] = proj(o + 2 * ATTN_WIDTH, o + 3 * ATTN_WIDTH).astype(BF16)
    o += 3 * ATTN_WIDTH
    za_ref[0] = (_sigmoid(proj(o, o + D_MODEL)) * y_conv).astype(BF16)
    sgb_ref[0] = _sigmoid(proj(o + D_MODEL, o + 2 * D_MODEL)).astype(BF16)


def _proj_call(x, w_in, conv_w, w_out_conv):
    tok_spec = lambda c: pl.BlockSpec((1, TM_PROJ, c), lambda b, s: (b, s, 0))
    full = lambda shape: pl.BlockSpec(shape, lambda b, s: (0,) * len(shape))
    once = lambda shape: pl.BlockSpec(shape, lambda b, s: (0,) * len(shape), pipeline_mode=pl.Buffered(1))
    return pl.pallas_call(
        _proj_kernel,
        grid=(BATCH, SEQ // TM_PROJ),
        in_specs=[tok_spec(D_MODEL), pl.BlockSpec(memory_space=pl.ANY), full((3, CONV_WIDTH)),
                  once((CONV_WIDTH, D_MODEL))],
        out_specs=[tok_spec(ATTN_WIDTH), tok_spec(ATTN_WIDTH), tok_spec(ATTN_WIDTH),
                   tok_spec(D_MODEL), tok_spec(D_MODEL)],
        out_shape=[jax.ShapeDtypeStruct((BATCH, SEQ, ATTN_WIDTH), BF16)] * 3
        + [jax.ShapeDtypeStruct((BATCH, SEQ, D_MODEL), BF16)] * 2,
        scratch_shapes=[pltpu.VMEM((TM_PROJ + 8, CONV_WIDTH), F32), pltpu.VMEM((D_MODEL, IN_COLS), BF16),
                        pltpu.VMEM((CONV_WIDTH, D_MODEL), BF16),
                        pltpu.VMEM((2, D_MODEL, CONV_WIDTH), F32), pltpu.SemaphoreType.DMA((2,))],
        compiler_params=pltpu.CompilerParams(
            dimension_semantics=("arbitrary", "arbitrary"), vmem_limit_bytes=VMEM_LIMIT),
        name="proj",
    )(x, w_in, conv_w, w_out_conv)


def _attn_kernel(slopes_ref, qa_ref, qb_ref, k_ref, v_ref, oa_ref, ob_ref,
                 kaug_ref, vt_ref, kmean_ref, qaug_ref, pv_ref, mloc_ref, t_ref, p_ref):
    hq = pl.program_id(1)
    j = pl.program_id(2)
    blk = MOBA_BLOCK
    pairs = range(PAIRS)
    lanes = [slice(128 * pp, 128 * (pp + 1)) for pp in pairs]

    @pl.when(j == 0)
    def _():
        klane = lax.broadcasted_iota(I32, (blk, 128), 1)
        koff = lax.broadcasted_iota(I32, (blk, 128), 0).astype(F32)
        k_extra = jnp.where(klane == 0, koff, jnp.where(klane == 1, 1.0, 0.0)).astype(BF16)
        orow = lax.broadcasted_iota(I32, (PV_ROWS - HEAD_DIM, blk), 0)
        ones_rows = jnp.where(orow == 0, 1.0, 0.0).astype(BF16)
        for pp in pairs:
            for n in range(N_BLOCKS):
                kblk = k_ref[0, n * blk:(n + 1) * blk, lanes[pp]]
                kaug_ref[pp, n, :, 0:128] = kblk
                kaug_ref[pp, n, :, 128:256] = k_extra
                kmean_ref[pp, n:n + 1, :] = jnp.mean(kblk.astype(F32), axis=0, keepdims=True)
                v_t = v_ref[0, n * blk:(n + 1) * blk, lanes[pp]].astype(F32).T.astype(BF16)
                for hh in range(2):
                    vt_ref[pp, n, hh, 0:HEAD_DIM, :] = v_t[hh * HEAD_DIM:(hh + 1) * HEAD_DIM, :]
                    vt_ref[pp, n, hh, HEAD_DIM:PV_ROWS, :] = ones_rows

    lane = lax.broadcasted_iota(I32, (1, 2 * blk), 1)
    qoff_row = jnp.where(lane < blk, lane, lane - blk).astype(F32)
    feat = lax.broadcasted_iota(I32, (2 * HEAD_DIM, blk), 0)
    arow = lax.broadcasted_iota(I32, (2 * HEAD_DIM, 2 * blk), 0)
    blk_i = lax.broadcasted_iota(I32, (N_BLOCKS, 2 * blk), 0)
    key_i = lax.broadcasted_iota(I32, (blk, 2 * blk), 0)
    qry_j = lax.broadcasted_iota(I32, (blk, 2 * blk), 1)
    causal = key_i <= jnp.where(qry_j < blk, qry_j, qry_j - blk)
    slope_rows, q_extras = [], []
    for pp in pairs:
        head = 2 * (PAIRS * hq + pp)
        slope_rows.append(jnp.where(lane < blk, slopes_ref[head], slopes_ref[head + 1]))
        q_extras.append(jnp.where(arow == 0, slope_rows[pp],
                                  jnp.where(arow == 1, -slope_rows[pp] * qoff_row, 0.0)).astype(BF16))

    def prepare(q_ref, pp, slot, qblock):
        q_t = q_ref[0, :, lanes[pp]].astype(F32).T
        qcat = jnp.concatenate([jnp.where(feat < HEAD_DIM, q_t, 0.0), jnp.where(feat >= HEAD_DIM, q_t, 0.0)],
                               axis=1).astype(BF16)
        qaug_ref[pp, slot, 0:2 * HEAD_DIM, :] = qcat
        qaug_ref[pp, slot, 2 * HEAD_DIM:4 * HEAD_DIM, :] = q_extras[pp]
        gate = jnp.dot(kmean_ref[pp].astype(BF16), qcat, preferred_element_type=F32)
        cnt = jnp.zeros((N_BLOCKS, 2 * blk), F32)
        for m in range(N_BLOCKS):
            gm = gate[m:m + 1, :]
            beats = (gm > gate) | ((gm == gate) & (blk_i > m))
            cnt = cnt + jnp.where(beats & (qblock > m), 1.0, 0.0)
        return jnp.where((blk_i < qblock) & (cnt < float(MOBA_TOPK)), 1.0, 0.0)

    qblock_a = j
    qblock_b = N_BLOCKS - 1 - j
    sel_a = [prepare(qa_ref, pp, 0, qblock_a) for pp in pairs]
    sel_b = [prepare(qb_ref, pp, 1, qblock_b) for pp in pairs]

    n_mid = N_BLOCKS - 1
    slots = [(0, 0, qblock_a, True)]
    mids = []
    for s in range(1, n_mid + 1):
        is_a = s <= j
        slots.append((s, jnp.where(is_a, 0, 1), jnp.where(is_a, s - 1, s - 1 - j), False))
        mids.append((is_a, slots[-1][2]))
    slots.append((n_mid + 1, 1, qblock_b, True))

    for pp in pairs:
        for s, which, kb, _ in slots:
            t_ref[pp, s] = jnp.dot(kaug_ref[pp, kb], qaug_ref[pp, which], preferred_element_type=F32)
    for pp in pairs:
        for s, _, _, own in slots:
            t = t_ref[pp, s]
            if own:
                t = jnp.where(causal, t, NEG_INF)
            m_loc = jnp.max(t, axis=0, keepdims=True)
            p_ref[pp, s] = jnp.exp((t - m_loc).astype(BF16))
            mloc_ref[pp, s:s + 1, :] = m_loc
    for pp in pairs:
        for s, _, kb, _ in slots:
            pv_ref[pp, s, 0] = jnp.dot(vt_ref[pp, kb, 0], p_ref[pp, s, :, 0:blk], preferred_element_type=F32)
            pv_ref[pp, s, 1] = jnp.dot(vt_ref[pp, kb, 1], p_ref[pp, s, :, blk:2 * blk],
                                       preferred_element_type=F32)

    def combine(o_ref, pp, own_slot, sel, qblock, mine):
        neg = jnp.full((1, 2 * blk), -1e30, F32)
        pieces = [(own_slot, mloc_ref[pp, own_slot:own_slot + 1, :])]
        for s, (is_a, kb) in enumerate(mids, start=1):
            selrow = jnp.sum(jnp.where(blk_i == kb, sel, 0.0), axis=0, keepdims=True)
            belongs = jnp.where(is_a, 1.0, 0.0) if mine else jnp.where(is_a, 0.0, 1.0)
            used = selrow * belongs > 0.5
            shift = slope_rows[pp] * ((kb - qblock) * blk).astype(F32)
            pieces.append((s, jnp.where(used, mloc_ref[pp, s:s + 1, :] + shift, neg)))
        m_all = pieces[0][1]
        for _, m_s in pieces[1:]:
            m_all = jnp.maximum(m_all, m_s)
        acc = [jnp.zeros((PV_ROWS, blk), F32), jnp.zeros((PV_ROWS, blk), F32)]
        for s, m_s in pieces:
            w = jnp.exp(m_s - m_all)
            for hh in range(2):
                acc[hh] = acc[hh] + pv_ref[pp, s, hh] * w[:, hh * blk:(hh + 1) * blk]
        o_t = jnp.concatenate([a[0:HEAD_DIM, :] / a[HEAD_DIM:HEAD_DIM + 1, :] for a in acc], axis=0)
        o_ref[0, :, lanes[pp]] = o_t.T.astype(BF16)

    for pp in pairs:
        combine(oa_ref, pp, 0, sel_a[pp], qblock_a, True)
        combine(ob_ref, pp, n_mid + 1, sel_b[pp], qblock_b, False)


def _attn_call(slopes, q, k, v):
    half = N_BLOCKS // 2
    width = 128 * PAIRS
    o_a, o_b = pl.pallas_call(
        _attn_kernel,
        grid_spec=pltpu.PrefetchScalarGridSpec(
            num_scalar_prefetch=1,
            grid=(BATCH, N_HEADS // (2 * PAIRS), half),
            in_specs=[
                pl.BlockSpec((1, MOBA_BLOCK, width), lambda b, h, j, sl: (b, j, h)),
                pl.BlockSpec((1, MOBA_BLOCK, width), lambda b, h, j, sl: (b, N_BLOCKS - 1 - j, h)),
                pl.BlockSpec((1, SEQ, width), lambda b, h, j, sl: (b, 0, h)),
                pl.BlockSpec((1, SEQ, width), lambda b, h, j, sl: (b, 0, h)),
            ],
            out_specs=[pl.BlockSpec((1, MOBA_BLOCK, width), lambda b, h, j, sl: (b, j, h)),
                       pl.BlockSpec((1, MOBA_BLOCK, width), lambda b, h, j, sl: (b, half - 1 - j, h))],
            scratch_shapes=[
                pltpu.VMEM((PAIRS, N_BLOCKS, MOBA_BLOCK, 256), BF16),
                pltpu.VMEM((PAIRS, N_BLOCKS, 2, PV_ROWS, MOBA_BLOCK), BF16),
                pltpu.VMEM((PAIRS, N_BLOCKS, 128), F32),
                pltpu.VMEM((PAIRS, 2, 256, 2 * MOBA_BLOCK), BF16),
                pltpu.VMEM((PAIRS, N_BLOCKS + 1, 2, PV_ROWS, MOBA_BLOCK), F32),
                pltpu.VMEM((PAIRS, 16, 2 * MOBA_BLOCK), F32),
                pltpu.VMEM((PAIRS, N_BLOCKS + 1, MOBA_BLOCK, 2 * MOBA_BLOCK), F32),
                pltpu.VMEM((PAIRS, N_BLOCKS + 1, MOBA_BLOCK, 2 * MOBA_BLOCK), BF16),
            ],
        ),
        out_shape=[jax.ShapeDtypeStruct((BATCH, SEQ // 2, ATTN_WIDTH), BF16)] * 2,
        compiler_params=pltpu.CompilerParams(
            dimension_semantics=("arbitrary", "arbitrary", "arbitrary"), vmem_limit_bytes=VMEM_LIMIT),
        name="moba_attn",
    )(slopes, q, q, k, v)
    return o_a, o_b


def _route(logits):
    row8 = lax.broadcasted_iota(I32, (8, TM), 0).astype(F32)
    gl = jnp.where(row8 < float(N_GROUPS), logits[0:8, :], NEG_INF)
    gexp = jnp.exp(gl - jnp.max(gl, axis=0, keepdims=True))
    gprob = gexp / jnp.sum(gexp, axis=0, keepdims=True)
    ptop = jnp.max(gprob, axis=0, keepdims=True)
    gtop = jnp.min(jnp.where(gprob == ptop, row8, 8.0), axis=0, keepdims=True)
    el = logits[8:ROUTER_ROWS, :]
    eg = jnp.where(gtop == 0.0, el[0:8, :],
                   jnp.where(gtop == 1.0, el[8:16, :], jnp.where(gtop == 2.0, el[16:24, :], el[24:32, :])))
    m1 = jnp.max(eg, axis=0, keepdims=True)
    i1 = jnp.min(jnp.where(eg == m1, row8, 8.0), axis=0, keepdims=True)
    eg2 = jnp.where(row8 == i1, NEG_INF, eg)
    m2 = jnp.max(eg2, axis=0, keepdims=True)
    i2 = jnp.min(jnp.where(eg2 == m2, row8, 8.0), axis=0, keepdims=True)
    t2 = jnp.exp(m2 - m1)
    gate1 = ptop * (1.0 / (1.0 + t2))
    gate2 = ptop * (t2 / (1.0 + t2))
    erow = lax.broadcasted_iota(I32, (N_EXPERTS, TM), 0).astype(F32)
    oh1 = jnp.where(erow == gtop * float(EXPERTS_PER_GROUP) + i1, 1.0, 0.0)
    oh2 = jnp.where(erow == gtop * float(EXPERTS_PER_GROUP) + i2, 1.0, 0.0)
    return gate1, gate2, oh1, oh2


def _merge_kernel(oa_ref, ob_ref, za_ref, sgb_ref, x_ref, woa_f32_ref, wo_f32_ref, g_ref, b_ref,
                  wr_hi_ref, wr_lo_ref, x1_ref, xs_ref, rf_ref, mt_ref, woa_ref, wo_ref):
    i = pl.program_id(0)

    @pl.when(i == 0)
    def _():
        woa_ref[...] = woa_f32_ref[...].astype(BF16)
        wo_ref[...] = wo_f32_ref[...].astype(BF16)

    subs = range(MERGE_SUB)
    rows = [slice(s * TM, (s + 1) * TM) for s in subs]
    steps_per_batch = N_BLOCKS // MERGE_SUB
    in_oa = lax.rem(i, steps_per_batch) < steps_per_batch // 2
    o = [jnp.where(in_oa, oa_ref[0, r, :], ob_ref[0, r, :]) for r in rows]
    y_attn = [jnp.dot(o[s], woa_ref[...], preferred_element_type=F32) for s in subs]
    y = [(za_ref[rows[s], :].astype(F32) + sgb_ref[rows[s], :].astype(F32) * y_attn[s]).astype(BF16)
         for s in subs]
    mix = [jnp.dot(y[s], wo_ref[...], preferred_element_type=F32) for s in subs]
    x1 = []
    for s in subs:
        h = ALPHA * x_ref[rows[s], :] + mix[s]
        mu = jnp.mean(h, axis=-1, keepdims=True)
        hc = h - mu
        var = jnp.mean(hc * hc, axis=-1, keepdims=True)
        x1.append(hc * lax.rsqrt(var + LN_EPS) * g_ref[...] + b_ref[...])
        x1_ref[rows[s], :] = x1[s]

    xh = [x1[s].astype(BF16) for s in subs]
    xl = [(x1[s] - xh[s].astype(F32)).astype(BF16) for s in subs]
    wh = wr_hi_ref[...]
    logits = [(jnp.dot(xh[s], wh, preferred_element_type=F32)
               + jnp.dot(xl[s], wh, preferred_element_type=F32)
               + jnp.dot(xh[s], wr_lo_ref[...], preferred_element_type=F32)).T for s in subs]
    routes = [_route(logits[s]) for s in subs]

    ta = lax.broadcasted_iota(I32, (TM, TM), 0)
    tb = lax.broadcasted_iota(I32, (TM, TM), 1)
    upper = jnp.where(ta < tb, 1.0, 0.0).astype(BF16)
    ea = lax.broadcasted_iota(I32, (N_EXPERTS, N_EXPERTS), 0)
    eb = lax.broadcasted_iota(I32, (N_EXPERTS, N_EXPERTS), 1)
    lower = jnp.where(eb < ea, 1.0, 0.0).astype(BF16)
    lrow = lax.broadcasted_iota(I32, (LOCAL_ROWS, TM), 0).astype(F32)
    zero = jnp.zeros((1, TM), F32)
    cum = [jnp.dot((routes[s][2] + routes[s][3]).astype(BF16), upper, preferred_element_type=F32) for s in subs]
    perm = []
    for s in subs:
        gate1, gate2, oh1, oh2 = routes[s]
        n_e = jnp.sum(oh1 + oh2, axis=1, keepdims=True)
        m_rep = jnp.broadcast_to(jnp.floor((n_e + float(GRAN - 1)) * (1.0 / GRAN)), (N_EXPERTS, 128))
        run_start = jnp.dot(lower, m_rep.astype(BF16), preferred_element_type=F32)
        tot = cum[s] + float(GRAN) * run_start[:, 0:1]
        lp1 = jnp.sum(oh1 * tot, axis=0, keepdims=True)
        lp2 = jnp.sum(oh2 * tot, axis=0, keepdims=True)
        perm.append(jnp.where((lrow == lp1) | (lrow == lp2), 1.0, 0.0).astype(BF16))
        rf_ref[:, rows[s]] = jnp.concatenate([gate1, gate2, lp1, lp2, zero, zero, zero, zero], axis=0)
        mt_ref[s * N_EXPERTS:(s + 1) * N_EXPERTS, :] = m_rep
    for s in subs:
        xs_ref[s * LOCAL_ROWS:(s + 1) * LOCAL_ROWS, :] = jnp.dot(
            perm[s], xh[s], preferred_element_type=F32).astype(BF16)


def _merge_call(o_a, o_b, za, sgb, x, woa, wo, g, b, wr_hi, wr_lo):
    tm = MERGE_SUB * TM
    tok = lambda c: pl.BlockSpec((tm, c), lambda i: (i, 0))
    full = lambda shape: pl.BlockSpec(shape, lambda i: (0,) * len(shape))
    per_batch = SEQ // tm
    half = per_batch // 2
    o_a_spec = pl.BlockSpec((1, tm, ATTN_WIDTH), lambda i: (i // per_batch, jnp.minimum(i % per_batch, half - 1), 0))
    o_b_spec = pl.BlockSpec((1, tm, ATTN_WIDTH), lambda i: (i // per_batch, jnp.maximum(i % per_batch - half, 0), 0))
    return pl.pallas_call(
        _merge_kernel,
        grid=(TOKENS // tm,),
        in_specs=[o_a_spec, o_b_spec, tok(D_MODEL), tok(D_MODEL), tok(D_MODEL),
                  full((ATTN_WIDTH, D_MODEL)), full((D_MODEL, D_MODEL)), full((1, D_MODEL)),
                  full((1, D_MODEL)), full((D_MODEL, 128)), full((D_MODEL, 128))],
        out_specs=[tok(D_MODEL), pl.BlockSpec((MERGE_SUB * LOCAL_ROWS, D_MODEL), lambda i: (i, 0)),
                   pl.BlockSpec((8, tm), lambda i: (0, i)),
                   pl.BlockSpec((MERGE_SUB * N_EXPERTS, 128), lambda i: (i, 0))],
        out_shape=[jax.ShapeDtypeStruct((TOKENS, D_MODEL), F32),
                   jax.ShapeDtypeStruct((N_TOK_TILES * LOCAL_ROWS, D_MODEL), BF16),
                   jax.ShapeDtypeStruct((8, TOKENS), F32),
                   jax.ShapeDtypeStruct((N_TOK_TILES * N_EXPERTS, 128), F32)],
        scratch_shapes=[pltpu.VMEM((ATTN_WIDTH, D_MODEL), BF16), pltpu.VMEM((D_MODEL, D_MODEL), BF16)],
        compiler_params=pltpu.CompilerParams(
            dimension_semantics=("arbitrary",), vmem_limit_bytes=VMEM_LIMIT),
        name="merge_ln1_route",
    )(o_a, o_b, za, sgb, x, woa, wo, g, b, wr_hi, wr_lo)


def _granule_copy(src_ref, src_gran, dst_ref, dst_gran, sem):
    src = pl.multiple_of(src_gran * GRAN, GRAN)
    dst = pl.multiple_of(dst_gran * GRAN, GRAN)
    return pltpu.make_async_copy(src_ref.at[pl.ds(src, GRAN), :], dst_ref.at[pl.ds(dst, GRAN), :], sem)


def _expert_kernel(te_ref, nt_ref, gsrc_ref, gdst_ref, ug_ref, wpar_ref, wnext_ref,
                   xs_ref, wg_hbm_ref, wu_hbm_ref, wd_hbm_ref,
                   ys_ref, xbuf, ybuf, zbuf, wg_ref, wu_ref, wd_ref, wg_stage, wu_stage, wd_stage,
                   in_sem, out_sem, zero_sem, w_sem):
    j = pl.program_id(0)
    n_tiles = nt_ref[0]
    slot = lax.rem(j, NBUF)
    prev_slot = lax.rem(j + NBUF - 1, NBUF)

    def tile_gather(step, s):
        for g in range(STEP_GRANS):
            _granule_copy(xs_ref, gsrc_ref[step * STEP_GRANS + g], xbuf.at[s], g,
                          in_sem.at[s]).start(priority=g % 2)

    def prev_scatter():
        for g in range(STEP_GRANS):
            _granule_copy(ybuf.at[prev_slot], g, ys_ref, gdst_ref[j * STEP_GRANS + g],
                          out_sem.at[prev_slot]).start(priority=g % 2)

    @pl.when(j == 0)
    def _():
        tile_gather(0, 0)
        ybuf[NBUF - 1] = jnp.zeros((STEP_ROWS, D_MODEL), BF16)
        zbuf[...] = jnp.zeros((GRAN, D_MODEL), BF16)
        for part in range(NBUF):
            spare = pltpu.make_async_copy(
                ybuf.at[NBUF - 1], ys_ref.at[pl.ds((SPARE_GRAN + part * STEP_GRANS) * GRAN, STEP_ROWS), :],
                out_sem.at[NBUF - 1])
            spare.start()
            spare.wait()

        for ahead in range(1, NBUF - 1):
            tile_gather(ahead, ahead)

    def zero_copy(t, g):
        return _granule_copy(zbuf, 0, ys_ref, t * LOCAL_GRANS + g, zero_sem)

    @pl.when(jnp.logical_and(j >= 1, j <= N_TOK_TILES))
    def _():
        def wait(g, c):
            zero_copy(j - 1, g).wait()
            return c

        lax.fori_loop(ug_ref[j - 1], LOCAL_GRANS, wait, 0)

    @pl.when(j < N_TOK_TILES)
    def _():
        def start(g, c):
            zero_copy(j, g).start()
            return c

        lax.fori_loop(ug_ref[j], LOCAL_GRANS, start, 0)

    def gather_wait():
        pltpu.make_async_copy(xs_ref.at[pl.ds(0, STEP_ROWS), :], xbuf.at[slot], in_sem.at[slot]).wait()

    @pl.when(jnp.logical_and(j >= NBUF - 1, j - NBUF < n_tiles))
    def _():
        pltpu.make_async_copy(ybuf.at[slot], ys_ref.at[pl.ds(0, STEP_ROWS), :], out_sem.at[slot]).wait()

    @pl.when(jnp.logical_and(j >= n_tiles, j < n_tiles + NBUF - 1))
    def _():
        gather_wait()

    @pl.when(j == n_tiles)
    def _():
        prev_scatter()

    def weight_copies(expert, s):
        return [pltpu.make_async_copy(hbm.at[expert], stage.at[s], w_sem.at[s])
                for hbm, stage in ((wg_hbm_ref, wg_stage), (wu_hbm_ref, wu_stage), (wd_hbm_ref, wd_stage))]

    @pl.when(j == 0)
    def _():
        for cp in weight_copies(te_ref[0], 0):
            cp.start()

    @pl.when(jnp.logical_and(j < n_tiles, jnp.logical_or(j == 0, te_ref[j] != te_ref[jnp.maximum(j - 1, 0)])))
    def _():
        s = wpar_ref[j]
        for cp in weight_copies(te_ref[j], s):
            cp.wait()
        wg_ref[...] = wg_stage[s].astype(BF16)
        wu_ref[...] = wu_stage[s].astype(BF16)
        wd_ref[...] = wd_stage[s].astype(BF16)
        nxt = wnext_ref[te_ref[j]]

        @pl.when(nxt >= 0)
        def _():
            for cp in weight_copies(nxt, 1 - s):
                cp.start()

    @pl.when(j < n_tiles)
    def _():
        gather_wait()
        chains = range(0, TE, CHAIN_ROWS)
        xb = [xbuf[slot, r:r + CHAIN_ROWS, :] for r in chains]
        hg = [jnp.dot(x, wg_ref[...], preferred_element_type=F32) for x in xb]
        hu = [jnp.dot(x, wu_ref[...], preferred_element_type=F32) for x in xb]
        tile_gather(j + NBUF - 1, prev_slot)
        prev_scatter()
        h = [(a * _sigmoid(a) * b).astype(BF16) for a, b in zip(hg, hu)]
        for hc, r in zip(h, chains):
            ybuf[slot, r:r + CHAIN_ROWS, :] = jnp.dot(hc, wd_ref[...], preferred_element_type=F32).astype(BF16)


def _expert_call(tile_expert, n_steps, gsrc, gdst, used_grans, w_parity, w_next, xs, wg, wu, wd):
    hbm = pl.BlockSpec(memory_space=pl.ANY)
    return pl.pallas_call(
        _expert_kernel,
        grid_spec=pltpu.PrefetchScalarGridSpec(
            num_scalar_prefetch=7,
            grid=(MAX_STEPS + NBUF,),
            in_specs=[hbm, hbm, hbm, hbm],
            out_specs=hbm,
            scratch_shapes=[pltpu.VMEM((NBUF, STEP_ROWS, D_MODEL), BF16),
                            pltpu.VMEM((NBUF, STEP_ROWS, D_MODEL), BF16),
                            pltpu.VMEM((GRAN, D_MODEL), BF16),
                            pltpu.VMEM((D_MODEL, D_EXPERT), BF16), pltpu.VMEM((D_MODEL, D_EXPERT), BF16),
                            pltpu.VMEM((D_EXPERT, D_MODEL), BF16),
                            pltpu.VMEM((2, D_MODEL, D_EXPERT), F32), pltpu.VMEM((2, D_MODEL, D_EXPERT), F32),
                            pltpu.VMEM((2, D_EXPERT, D_MODEL), F32),
                            pltpu.SemaphoreType.DMA((NBUF,)), pltpu.SemaphoreType.DMA((NBUF,)),
                            pltpu.SemaphoreType.DMA, pltpu.SemaphoreType.DMA((2,))],
        ),
        out_shape=jax.ShapeDtypeStruct(((SPARE_GRAN + NBUF * STEP_GRANS) * GRAN, D_MODEL), BF16),
        compiler_params=pltpu.CompilerParams(
            dimension_semantics=("arbitrary",), vmem_limit_bytes=VMEM_LIMIT),
        name="experts",
    )(tile_expert, n_steps, gsrc, gdst, used_grans, w_parity, w_next, xs, wg, wu, wd)


def _combine_kernel(ys_ref, x1_ref, rf_ref, g_ref, b_ref, out_ref):
    subs = range(MERGE_SUB)
    col = lax.broadcasted_iota(I32, (TM, LOCAL_ROWS), 1).astype(F32)
    route = [rf_ref[:, s * TM:(s + 1) * TM].T for s in subs]
    unsort = [(jnp.where(col == r[:, 2:3], r[:, 0:1], 0.0)
               + jnp.where(col == r[:, 3:4], r[:, 1:2], 0.0)).astype(BF16) for r in route]
    ffn = [jnp.dot(unsort[s], ys_ref[s * LOCAL_ROWS:(s + 1) * LOCAL_ROWS, :], preferred_element_type=F32)
           for s in subs]
    for s in subs:
        h = ALPHA * x1_ref[s * TM:(s + 1) * TM, :] + ffn[s]
        mu = jnp.mean(h, axis=-1, keepdims=True)
        hc = h - mu
        var = jnp.mean(hc * hc, axis=-1, keepdims=True)
        out_ref[s * TM:(s + 1) * TM, :] = hc * lax.rsqrt(var + LN_EPS) * g_ref[...] + b_ref[...]


def _combine_call(ys, x1, rf, g, b):
    tm = MERGE_SUB * TM
    return pl.pallas_call(
        _combine_kernel,
        grid=(TOKENS // tm,),
        in_specs=[pl.BlockSpec((MERGE_SUB * LOCAL_ROWS, D_MODEL), lambda i: (i, 0)),
                  pl.BlockSpec((tm, D_MODEL), lambda i: (i, 0)),
                  pl.BlockSpec((8, tm), lambda i: (0, i)),
                  pl.BlockSpec((1, D_MODEL), lambda i: (0, 0)),
                  pl.BlockSpec((1, D_MODEL), lambda i: (0, 0))],
        out_specs=pl.BlockSpec((tm, D_MODEL), lambda i: (i, 0)),
        out_shape=jax.ShapeDtypeStruct((TOKENS, D_MODEL), F32),
        compiler_params=pltpu.CompilerParams(
            dimension_semantics=("arbitrary",), vmem_limit_bytes=VMEM_LIMIT),
        name="combine_ln2",
    )(ys, x1, rf, g, b)


def _router_cols(w_router_group, w_router_expert):
    w = jnp.concatenate([w_router_group, jnp.zeros((D_MODEL, 4), F32), w_router_expert,
                         jnp.zeros((D_MODEL, 128 - ROUTER_ROWS), F32)], axis=1)
    hi = w.astype(BF16)
    lo = (w - hi.astype(F32)).astype(BF16)
    return hi, lo


def _layer(x, w_in, conv_w, w_out_conv, w_out_attn, w_o, ln1_g, ln1_b,
           w_router_group, w_router_expert, w_gate, w_up, w_down, ln2_g, ln2_b):
    slopes = jnp.asarray([2.0 ** (-8.0 * (h + 1) / N_HEADS) for h in range(N_HEADS)], F32)
    q, k, v, za, sgb = _proj_call(x, w_in, conv_w, w_out_conv)
    o_a, o_b = _attn_call(slopes, q, k, v)

    wr_hi, wr_lo = _router_cols(w_router_group, w_router_expert)
    x1, xs, rf, mt = _merge_call(
        o_a, o_b, za.reshape(TOKENS, D_MODEL), sgb.reshape(TOKENS, D_MODEL),
        x.reshape(TOKENS, D_MODEL), w_out_attn, w_o,
        ln1_g.reshape(1, D_MODEL), ln1_b.reshape(1, D_MODEL), wr_hi, wr_lo)

    grans = mt.reshape(N_TOK_TILES, N_EXPERTS, 128)[:, :, 0].astype(I32)
    local_start = jnp.cumsum(grans, axis=1) - grans
    grans_t = grans.T
    tiles_e = (jnp.sum(grans_t, axis=1) + TILE_GRANS - 1) // TILE_GRANS
    tile_end = jnp.cumsum(tiles_e)
    n_steps = ((tile_end[-1] + EXP_SUB - 1) // EXP_SUB).reshape(1)
    all_tiles = (MAX_STEPS + NBUF) * EXP_SUB
    tile_ids = jnp.arange(all_tiles, dtype=I32)
    tile_expert = jnp.minimum(
        jnp.sum((tile_ids[:, None] >= tile_end[None, :]).astype(I32), axis=1), N_EXPERTS - 1)
    run_slot = TILE_GRANS * (tile_end - tiles_e)[:, None] + jnp.cumsum(grans_t, axis=1) - grans_t
    run_src = jnp.arange(N_TOK_TILES, dtype=I32)[None, :] * LOCAL_GRANS + local_start.T
    pick = (tile_expert[:all_tiles, None] == jnp.arange(N_EXPERTS, dtype=I32)[None, :])[:, :, None]
    t_slot = jnp.sum(jnp.where(pick, run_slot[None], 0), axis=1)
    t_len = jnp.sum(jnp.where(pick, grans_t[None], 0), axis=1)
    t_src = jnp.sum(jnp.where(pick, run_src[None], 0), axis=1)
    slots = jnp.arange(all_tiles * TILE_GRANS, dtype=I32).reshape(all_tiles, TILE_GRANS)
    k = slots[:, :, None] - t_slot[:, None, :]
    hit = (k >= 0) & (k < t_len[:, None, :])
    gran = jnp.sum(jnp.where(hit, t_src[:, None, :] + k, 0), axis=2).reshape(-1)
    filled = (jnp.sum(hit.astype(I32), axis=2) > 0).reshape(-1)
    slots = slots.reshape(-1)
    gsrc = jnp.where(filled, gran, 0)
    gdst = jnp.where(filled, gran, SPARE_GRAN + slots % (NBUF * STEP_GRANS))
    gdst = jnp.concatenate([SPARE_GRAN + (NBUF - 1) * STEP_GRANS + jnp.arange(STEP_GRANS, dtype=I32), gdst])

    starts = jnp.concatenate([jnp.ones((1,), I32), (tile_expert[1:] != tile_expert[:-1]).astype(I32)])
    w_parity = (jnp.cumsum(starts) - 1) % 2
    ids = jnp.arange(N_EXPERTS, dtype=I32)
    later = jnp.where((tiles_e > 0)[None, :] & (ids[None, :] > ids[:, None]), ids[None, :], N_EXPERTS)
    w_next = jnp.min(later, axis=1)
    w_next = jnp.where(w_next == N_EXPERTS, -1, w_next)

    ys = _expert_call(tile_expert, n_steps, gsrc, gdst, jnp.sum(grans, axis=1), w_parity, w_next, xs,
                      w_gate, w_up, w_down)
    out = _combine_call(ys, x1, rf, ln2_g.reshape(1, D_MODEL), ln2_b.reshape(1, D_MODEL))
    return out.reshape(BATCH, SEQ, D_MODEL)


def kernel(x, w_in, conv_w, w_out_conv, w_out_attn, w_o, ln1_g, ln1_b, w_router_group, w_router_expert, w_gate, w_up, w_down, ln2_g, ln2_b):
    depth = w_in.shape[0]
    for l in range(depth):
        x = _layer(x, w_in[l], conv_w[l], w_out_conv[l], w_out_attn[l], w_o[l], ln1_g[l], ln1_b[l],
                   w_router_group[l], w_router_expert[l], w_gate[l], w_up[l], w_down[l], ln2_g[l], ln2_b[l])
    return x
```

```python
import functools

import jax
import jax.numpy as jnp
from jax import lax
from jax.experimental import pallas as pl
from jax.experimental.pallas import tpu as pltpu

F32 = jnp.float32
BF16 = jnp.bfloat16
U32 = jnp.uint32
I32 = jnp.int32

D_MODEL = 1024
BATCH = 8
SEQ = 2048
TOKENS = BATCH * SEQ
CONV_WIDTH = 512
N_HEADS = 8
HEAD_DIM = 64
ATTN_WIDTH = N_HEADS * HEAD_DIM
MOBA_BLOCK = 256
N_BLOCKS = SEQ // MOBA_BLOCK
MOBA_TOPK = 3
N_GROUPS = 4
EXPERTS_PER_GROUP = 8
N_EXPERTS = N_GROUPS * EXPERTS_PER_GROUP
D_EXPERT = 256
LN_EPS = 1e-5
ALPHA = 2.0 ** 0.25
IN_COLS = 3 * CONV_WIDTH + 3 * ATTN_WIDTH + 2 * D_MODEL
HALF = D_MODEL // 2

TM = 256
TM_PROJ = 512
TE = 512
CHAIN_ROWS = 256
PV_ROWS = HEAD_DIM + 16
GRAN = 16
TILE_GRANS = TE // GRAN
N_TOK_TILES = TOKENS // TM
LOCAL_ROWS = -(-(2 * TM + N_EXPERTS * (GRAN - 1)) // 256) * 256
LOCAL_GRANS = LOCAL_ROWS // GRAN
SPARE_GRAN = N_TOK_TILES * LOCAL_GRANS
MAX_TILES = (2 * TOKENS + N_TOK_TILES * N_EXPERTS * (GRAN - 1)) // TE + N_EXPERTS
YS_CHUNK = 256
PAIRS = 2
MERGE_SUB = 2
EXP_SUB = 1
STEP_ROWS = EXP_SUB * TE
STEP_GRANS = EXP_SUB * TILE_GRANS
MAX_STEPS = -(-MAX_TILES // EXP_SUB)
NBUF = 4
ROUTER_ROWS = 40
VMEM_LIMIT = 56 * 1024 * 1024
NEG_INF = float("-inf")


def _sigmoid(z):
    return 1.0 / (1.0 + jnp.exp(-z))


def _proj_kernel(x_ref, w_in_hbm_ref, convw_ref, woc_f32_ref, q_ref, k_ref, v_ref, za_ref, sgb_ref,
                 ubuf, w_in_ref, woc_ref, stage, stage_sem):
    s = pl.program_id(1)
    tm = TM_PROJ

    @pl.when((pl.program_id(0) == 0) & (s == 0))
    def _():
        def chunk(c):
            return pltpu.make_async_copy(w_in_hbm_ref.at[:, c * CONV_WIDTH:(c + 1) * CONV_WIDTH],
                                         stage.at[c % 2], stage_sem.at[c % 2])

        n_chunks = IN_COLS // CONV_WIDTH
        chunk(0).start()
        for c in range(n_chunks):
            if c + 1 < n_chunks:
                chunk(c + 1).start()
            chunk(c).wait()
            w_in_ref[:, c * CONV_WIDTH:(c + 1) * CONV_WIDTH] = stage[c % 2].astype(BF16)
        woc_ref[...] = woc_f32_ref[...].astype(BF16)

    xb = x_ref[0].astype(BF16)

    def proj(c0, c1):
        return jnp.dot(xb, w_in_ref[:, c0:c1], preferred_element_type=F32)

    c_b = proj(0, CONV_WIDTH)
    u = proj(CONV_WIDTH, 2 * CONV_WIDTH) * proj(2 * CONV_WIDTH, 3 * CONV_WIDTH)

    @pl.when(s == 0)
    def _():
        ubuf[0:8, :] = jnp.zeros((8, CONV_WIDTH), F32)

    ubuf[8:8 + tm, :] = u
    w = convw_ref[...]
    conv = w[2:3, :] * u + w[1:2, :] * ubuf[7:7 + tm, :] + w[0:1, :] * ubuf[6:6 + tm, :]
    ubuf[0:8, :] = u[tm - 8:tm, :]
    hc = (c_b * conv).astype(BF16)
    y_conv = jnp.dot(hc, woc_ref[...], preferred_element_type=F32)

    o = 3 * CONV_WIDTH
    q_ref[0] = (proj(o, o + ATTN_WIDTH) * (HEAD_DIM ** -0.5)).astype(BF16)
    k_ref[0] = proj(o + ATTN_WIDTH, o + 2 * ATTN_WIDTH).astype(BF16)
    v_ref[0] = proj(o + 2 * ATTN_WIDTH, o + 3 * ATTN_WIDTH).astype(BF16)
    o += 3 * ATTN_WIDTH
    za_ref[0] = (_sigmoid(proj(o, o + D_MODEL)) * y_conv).astype(BF16)
    sgb_ref[0] = _sigmoid(proj(o + D_MODEL, o + 2 * D_MODEL)).astype(BF16)


def _proj_call(x, w_in, conv_w, w_out_conv):
    tok_spec = lambda c: pl.BlockSpec((1, TM_PROJ, c), lambda b, s: (b, s, 0))
    full = lambda shape: pl.BlockSpec(shape, lambda b, s: (0,) * len(shape))
    once = lambda shape: pl.BlockSpec(shape, lambda b, s: (0,) * len(shape), pipeline_mode=pl.Buffered(1))
    return pl.pallas_call(
        _proj_kernel,
        grid=(BATCH, SEQ // TM_PROJ),
        in_specs=[tok_spec(D_MODEL), pl.BlockSpec(memory_space=pl.ANY), full((3, CONV_WIDTH)),
                  once((CONV_WIDTH, D_MODEL))],
        out_specs=[tok_spec(ATTN_WIDTH), tok_spec(ATTN_WIDTH), tok_spec(ATTN_WIDTH),
                   tok_spec(D_MODEL), tok_spec(D_MODEL)],
        out_shape=[jax.ShapeDtypeStruct((BATCH, SEQ, ATTN_WIDTH), BF16)] * 3
        + [jax.ShapeDtypeStruct((BATCH, SEQ, D_MODEL), BF16)] * 2,
        scratch_shapes=[pltpu.VMEM((TM_PROJ + 8, CONV_WIDTH), F32), pltpu.VMEM((D_MODEL, IN_COLS), BF16),
                        pltpu.VMEM((CONV_WIDTH, D_MODEL), BF16),
                        pltpu.VMEM((2, D_MODEL, CONV_WIDTH), F32), pltpu.SemaphoreType.DMA((2,))],
        compiler_params=pltpu.CompilerParams(
            dimension_semantics=("arbitrary", "arbitrary"), vmem_limit_bytes=VMEM_LIMIT),
        name="proj",
    )(x, w_in, conv_w, w_out_conv)


def _attn_kernel(slopes_ref, qa_ref, qb_ref, k_ref, v_ref, oa_ref, ob_ref,
                 kaug_ref, vt_ref, kmean_ref, qaug_ref, pv_ref, mloc_ref, t_ref, p_ref):
    hq = pl.program_id(1)
    j = pl.program_id(2)
    blk = MOBA_BLOCK
    pairs = range(PAIRS)
    lanes = [slice(128 * pp, 128 * (pp + 1)) for pp in pairs]

    @pl.when(j == 0)
    def _():
        klane = lax.broadcasted_iota(I32, (blk, 128), 1)
        koff = lax.broadcasted_iota(I32, (blk, 128), 0).astype(F32)
        k_extra = jnp.where(klane == 0, koff, jnp.where(klane == 1, 1.0, 0.0)).astype(BF16)
        orow = lax.broadcasted_iota(I32, (PV_ROWS - HEAD_DIM, blk), 0)
        ones_rows = jnp.where(orow == 0, 1.0, 0.0).astype(BF16)
        for pp in pairs:
            for n in range(N_BLOCKS):
                kblk = k_ref[0, n * blk:(n + 1) * blk, lanes[pp]]
                kaug_ref[pp, n, :, 0:128] = kblk
                kaug_ref[pp, n, :, 128:256] = k_extra
                kmean_ref[pp, n:n + 1, :] = jnp.mean(kblk.astype(F32), axis=0, keepdims=True)
                v_t = v_ref[0, n * blk:(n + 1) * blk, lanes[pp]].astype(F32).T.astype(BF16)
                for hh in range(2):
                    vt_ref[pp, n, hh, 0:HEAD_DIM, :] = v_t[hh * HEAD_DIM:(hh + 1) * HEAD_DIM, :]
                    vt_ref[pp, n, hh, HEAD_DIM:PV_ROWS, :] = ones_rows

    lane = lax.broadcasted_iota(I32, (1, 2 * blk), 1)
    qoff_row = jnp.where(lane < blk, lane, lane - blk).astype(F32)
    feat = lax.broadcasted_iota(I32, (2 * HEAD_DIM, blk), 0)
    arow = lax.broadcasted_iota(I32, (2 * HEAD_DIM, 2 * blk), 0)
    blk_i = lax.broadcasted_iota(I32, (N_BLOCKS, 2 * blk), 0)
    key_i = lax.broadcasted_iota(I32, (blk, 2 * blk), 0)
    qry_j = lax.broadcasted_iota(I32, (blk, 2 * blk), 1)
    causal = key_i <= jnp.where(qry_j < blk, qry_j, qry_j - blk)
    slope_rows, q_extras = [], []
    for pp in pairs:
        head = 2 * (PAIRS * hq + pp)
        slope_rows.append(jnp.where(lane < blk, slopes_ref[head], slopes_ref[head + 1]))
        q_extras.append(jnp.where(arow == 0, slope_rows[pp],
                                  jnp.where(arow == 1, -slope_rows[pp] * qoff_row, 0.0)).astype(BF16))

    def prepare(q_ref, pp, slot, qblock):
        q_t = q_ref[0, :, lanes[pp]].astype(F32).T
        qcat = jnp.concatenate([jnp.where(feat < HEAD_DIM, q_t, 0.0), jnp.where(feat >= HEAD_DIM, q_t, 0.0)],
                               axis=1).astype(BF16)
        qaug_ref[pp, slot, 0:2 * HEAD_DIM, :] = qcat
        qaug_ref[pp, slot, 2 * HEAD_DIM:4 * HEAD_DIM, :] = q_extras[pp]
        gate = jnp.dot(kmean_ref[pp].astype(BF16), qcat, preferred_element_type=F32)
        cnt = jnp.zeros((N_BLOCKS, 2 * blk), F32)
        for m in range(N_BLOCKS):
            gm = gate[m:m + 1, :]
            beats = (gm > gate) | ((gm == gate) & (blk_i > m))
            cnt = cnt + jnp.where(beats & (qblock > m), 1.0, 0.0)
        return jnp.where((blk_i < qblock) & (cnt < float(MOBA_TOPK)), 1.0, 0.0)

    qblock_a = j
    qblock_b = N_BLOCKS - 1 - j
    sel_a = [prepare(qa_ref, pp, 0, qblock_a) for pp in pairs]
    sel_b = [prepare(qb_ref, pp, 1, qblock_b) for pp in pairs]

    n_mid = N_BLOCKS - 1
    slots = [(0, 0, qblock_a, True)]
    mids = []
    for s in range(1, n_mid + 1):
        is_a = s <= j
        slots.append((s, jnp.where(is_a, 0, 1), jnp.where(is_a, s - 1, s - 1 - j), False))
        mids.append((is_a, slots[-1][2]))
    slots.append((n_mid + 1, 1, qblock_b, True))

    for pp in pairs:
        for s, which, kb, _ in slots:
            t_ref[pp, s] = jnp.dot(kaug_ref[pp, kb], qaug_ref[pp, which], preferred_element_type=F32)
    for pp in pairs:
        for s, _, _, own in slots:
            t = t_ref[pp, s]
            if own:
                t = jnp.where(causal, t, NEG_INF)
            m_loc = jnp.max(t, axis=0, keepdims=True)
            p_ref[pp, s] = jnp.exp((t - m_loc).astype(BF16))
            mloc_ref[pp, s:s + 1, :] = m_loc
    for pp in pairs:
        for s, _, kb, _ in slots:
            pv_ref[pp, s, 0] = jnp.dot(vt_ref[pp, kb, 0], p_ref[pp, s, :, 0:blk], preferred_element_type=F32)
            pv_ref[pp, s, 1] = jnp.dot(vt_ref[pp, kb, 1], p_ref[pp, s, :, blk:2 * blk],
                                       preferred_element_type=F32)

    def combine(o_ref, pp, own_slot, sel, qblock, mine):
        neg = jnp.full((1, 2 * blk), -1e30, F32)
        pieces = [(own_slot, mloc_ref[pp, own_slot:own_slot + 1, :])]
        for s, (is_a, kb) in enumerate(mids, start=1):
            selrow = jnp.sum(jnp.where(blk_i == kb, sel, 0.0), axis=0, keepdims=True)
            belongs = jnp.where(is_a, 1.0, 0.0) if mine else jnp.where(is_a, 0.0, 1.0)
            used = selrow * belongs > 0.5
            shift = slope_rows[pp] * ((kb - qblock) * blk).astype(F32)
            pieces.append((s, jnp.where(used, mloc_ref[pp, s:s + 1, :] + shift, neg)))
        m_all = pieces[0][1]
        for _, m_s in pieces[1:]:
            m_all = jnp.maximum(m_all, m_s)
        acc = [jnp.zeros((PV_ROWS, blk), F32), jnp.zeros((PV_ROWS, blk), F32)]
        for s, m_s in pieces:
            w = jnp.exp(m_s - m_all)
            for hh in range(2):
                acc[hh] = acc[hh] + pv_ref[pp, s, hh] * w[:, hh * blk:(hh + 1) * blk]
        o_t = jnp.concatenate([a[0:HEAD_DIM, :] / a[HEAD_DIM:HEAD_DIM + 1, :] for a in acc], axis=0)
        o_ref[0, :, lanes[pp]] = o_t.T.astype(BF16)

    for pp in pairs:
        combine(oa_ref, pp, 0, sel_a[pp], qblock_a, True)
        combine(ob_ref, pp, n_mid + 1, sel_b[pp], qblock_b, False)


def _attn_call(slopes, q, k, v):
    half = N_BLOCKS // 2
    width = 128 * PAIRS
    o_a, o_b = pl.pallas_call(
        _attn_kernel,
        grid_spec=pltpu.PrefetchScalarGridSpec(
            num_scalar_prefetch=1,
            grid=(BATCH, N_HEADS // (2 * PAIRS), half),
            in_specs=[
                pl.BlockSpec((1, MOBA_BLOCK, width), lambda b, h, j, sl: (b, j, h)),
                pl.BlockSpec((1, MOBA_BLOCK, width), lambda b, h, j, sl: (b, N_BLOCKS - 1 - j, h)),
                pl.BlockSpec((1, SEQ, width), lambda b, h, j, sl: (b, 0, h)),
                pl.BlockSpec((1, SEQ, width), lambda b, h, j, sl: (b, 0, h)),
            ],
            out_specs=[pl.BlockSpec((1, MOBA_BLOCK, width), lambda b, h, j, sl: (b, j, h)),
                       pl.BlockSpec((1, MOBA_BLOCK, width), lambda b, h, j, sl: (b, half - 1 - j, h))],
            scratch_shapes=[
                pltpu.VMEM((PAIRS, N_BLOCKS, MOBA_BLOCK, 256), BF16),
                pltpu.VMEM((PAIRS, N_BLOCKS, 2, PV_ROWS, MOBA_BLOCK), BF16),
                pltpu.VMEM((PAIRS, N_BLOCKS, 128), F32),
                pltpu.VMEM((PAIRS, 2, 256, 2 * MOBA_BLOCK), BF16),
                pltpu.VMEM((PAIRS, N_BLOCKS + 1, 2, PV_ROWS, MOBA_BLOCK), F32),
                pltpu.VMEM((PAIRS, 16, 2 * MOBA_BLOCK), F32),
                pltpu.VMEM((PAIRS, N_BLOCKS + 1, MOBA_BLOCK, 2 * MOBA_BLOCK), F32),
                pltpu.VMEM((PAIRS, N_BLOCKS + 1, MOBA_BLOCK, 2 * MOBA_BLOCK), BF16),
            ],
        ),
        out_shape=[jax.ShapeDtypeStruct((BATCH, SEQ // 2, ATTN_WIDTH), BF16)] * 2,
        compiler_params=pltpu.CompilerParams(
            dimension_semantics=("arbitrary", "arbitrary", "arbitrary"), vmem_limit_bytes=VMEM_LIMIT),
        name="moba_attn",
    )(slopes, q, q, k, v)
    return o_a, o_b


def _route(logits):
    row8 = lax.broadcasted_iota(I32, (8, TM), 0).astype(F32)
    gl = jnp.where(row8 < float(N_GROUPS), logits[0:8, :], NEG_INF)
    gexp = jnp.exp(gl - jnp.max(gl, axis=0, keepdims=True))
    gprob = gexp / jnp.sum(gexp, axis=0, keepdims=True)
    ptop = jnp.max(gprob, axis=0, keepdims=True)
    gtop = jnp.min(jnp.where(gprob == ptop, row8, 8.0), axis=0, keepdims=True)
    el = logits[8:ROUTER_ROWS, :]
    eg = jnp.where(gtop == 0.0, el[0:8, :],
                   jnp.where(gtop == 1.0, el[8:16, :], jnp.where(gtop == 2.0, el[16:24, :], el[24:32, :])))
    m1 = jnp.max(eg, axis=0, keepdims=True)
    i1 = jnp.min(jnp.where(eg == m1, row8, 8.0), axis=0, keepdims=True)
    eg2 = jnp.where(row8 == i1, NEG_INF, eg)
    m2 = jnp.max(eg2, axis=0, keepdims=True)
    i2 = jnp.min(jnp.where(eg2 == m2, row8, 8.0), axis=0, keepdims=True)
    t2 = jnp.exp(m2 - m1)
    gate1 = ptop * (1.0 / (1.0 + t2))
    gate2 = ptop * (t2 / (1.0 + t2))
    erow = lax.broadcasted_iota(I32, (N_EXPERTS, TM), 0).astype(F32)
    oh1 = jnp.where(erow == gtop * float(EXPERTS_PER_GROUP) + i1, 1.0, 0.0)
    oh2 = jnp.where(erow == gtop * float(EXPERTS_PER_GROUP) + i2, 1.0, 0.0)
    return gate1, gate2, oh1, oh2


def _merge_kernel(oa_ref, ob_ref, za_ref, sgb_ref, x_ref, woa_f32_ref, wo_f32_ref, g_ref, b_ref,
                  wr_hi_ref, wr_lo_ref, x1_ref, xs_ref, rf_ref, mt_ref, woa_ref, wo_ref):
    i = pl.program_id(0)

    @pl.when(i == 0)
    def _():
        woa_ref[...] = woa_f32_ref[...].astype(BF16)
        wo_ref[...] = wo_f32_ref[...].astype(BF16)

    subs = range(MERGE_SUB)
    rows = [slice(s * TM, (s + 1) * TM) for s in subs]
    steps_per_batch = N_BLOCKS // MERGE_SUB
    in_oa = lax.rem(i, steps_per_batch) < steps_per_batch // 2
    o = [jnp.where(in_oa, oa_ref[0, r, :], ob_ref[0, r, :]) for r in rows]
    y_attn = [jnp.dot(o[s], woa_ref[...], preferred_element_type=F32) for s in subs]
    y = [(za_ref[rows[s], :].astype(F32) + sgb_ref[rows[s], :].astype(F32) * y_attn[s]).astype(BF16)
         for s in subs]
    mix = [jnp.dot(y[s], wo_ref[...], preferred_element_type=F32) for s in subs]
    x1 = []
    for s in subs:
        h = ALPHA * x_ref[rows[s], :] + mix[s]
        mu = jnp.mean(h, axis=-1, keepdims=True)
        hc = h - mu
        var = jnp.mean(hc * hc, axis=-1, keepdims=True)
        x1.append(hc * lax.rsqrt(var + LN_EPS) * g_ref[...] + b_ref[...])
        x1_ref[rows[s], :] = x1[s]

    xh = [x1[s].astype(BF16) for s in subs]
    xl = [(x1[s] - xh[s].astype(F32)).astype(BF16) for s in subs]
    wh = wr_hi_ref[...]
    logits = [(jnp.dot(xh[s], wh, preferred_element_type=F32)
               + jnp.dot(xl[s], wh, preferred_element_type=F32)
               + jnp.dot(xh[s], wr_lo_ref[...], preferred_element_type=F32)).T for s in subs]
    routes = [_route(logits[s]) for s in subs]

    ta = lax.broadcasted_iota(I32, (TM, TM), 0)
    tb = lax.broadcasted_iota(I32, (TM, TM), 1)
    upper = jnp.where(ta < tb, 1.0, 0.0).astype(BF16)
    ea = lax.broadcasted_iota(I32, (N_EXPERTS, N_EXPERTS), 0)
    eb = lax.broadcasted_iota(I32, (N_EXPERTS, N_EXPERTS), 1)
    lower = jnp.where(eb < ea, 1.0, 0.0).astype(BF16)
    lrow = lax.broadcasted_iota(I32, (LOCAL_ROWS, TM), 0).astype(F32)
    zero = jnp.zeros((1, TM), F32)
    cum = [jnp.dot((routes[s][2] + routes[s][3]).astype(BF16), upper, preferred_element_type=F32) for s in subs]
    perm = []
    for s in subs:
        gate1, gate2, oh1, oh2 = routes[s]
        n_e = jnp.sum(oh1 + oh2, axis=1, keepdims=True)
        m_rep = jnp.broadcast_to(jnp.floor((n_e + float(GRAN - 1)) * (1.0 / GRAN)), (N_EXPERTS, 128))
        run_start = jnp.dot(lower, m_rep.astype(BF16), preferred_element_type=F32)
        tot = cum[s] + float(GRAN) * run_start[:, 0:1]
        lp1 = jnp.sum(oh1 * tot, axis=0, keepdims=True)
        lp2 = jnp.sum(oh2 * tot, axis=0, keepdims=True)
        perm.append(jnp.where((lrow == lp1) | (lrow == lp2), 1.0, 0.0).astype(BF16))
        rf_ref[:, rows[s]] = jnp.concatenate([gate1, gate2, lp1, lp2, zero, zero, zero, zero], axis=0)
        mt_ref[s * N_EXPERTS:(s + 1) * N_EXPERTS, :] = m_rep
    for s in subs:
        xs_ref[s * LOCAL_ROWS:(s + 1) * LOCAL_ROWS, :] = jnp.dot(
            perm[s], xh[s], preferred_element_type=F32).astype(BF16)


def _merge_call(o_a, o_b, za, sgb, x, woa, wo, g, b, wr_hi, wr_lo):
    tm = MERGE_SUB * TM
    tok = lambda c: pl.BlockSpec((tm, c), lambda i: (i, 0))
    full = lambda shape: pl.BlockSpec(shape, lambda i: (0,) * len(shape))
    per_batch = SEQ // tm
    half = per_batch // 2
    o_a_spec = pl.BlockSpec((1, tm, ATTN_WIDTH), lambda i: (i // per_batch, jnp.minimum(i % per_batch, half - 1), 0))
    o_b_spec = pl.BlockSpec((1, tm, ATTN_WIDTH), lambda i: (i // per_batch, jnp.maximum(i % per_batch - half, 0), 0))
    return pl.pallas_call(
        _merge_kernel,
        grid=(TOKENS // tm,),
        in_specs=[o_a_spec, o_b_spec, tok(D_MODEL), tok(D_MODEL), tok(D_MODEL),
                  full((ATTN_WIDTH, D_MODEL)), full((D_MODEL, D_MODEL)), full((1, D_MODEL)),
                  full((1, D_MODEL)), full((D_MODEL, 128)), full((D_MODEL, 128))],
        out_specs=[tok(D_MODEL), pl.BlockSpec((MERGE_SUB * LOCAL_ROWS, D_MODEL), lambda i: (i, 0)),
                   pl.BlockSpec((8, tm), lambda i: (0, i)),
                   pl.BlockSpec((MERGE_SUB * N_EXPERTS, 128), lambda i: (i, 0))],
        out_shape=[jax.ShapeDtypeStruct((TOKENS, D_MODEL), F32),
                   jax.ShapeDtypeStruct((N_TOK_TILES * LOCAL_ROWS, D_MODEL), BF16),
                   jax.ShapeDtypeStruct((8, TOKENS), F32),
                   jax.ShapeDtypeStruct((N_TOK_TILES * N_EXPERTS, 128), F32)],
        scratch_shapes=[pltpu.VMEM((ATTN_WIDTH, D_MODEL), BF16), pltpu.VMEM((D_MODEL, D_MODEL), BF16)],
        compiler_params=pltpu.CompilerParams(
            dimension_semantics=("arbitrary",), vmem_limit_bytes=VMEM_LIMIT),
        name="merge_ln1_route",
    )(o_a, o_b, za, sgb, x, woa, wo, g, b, wr_hi, wr_lo)


def _granule_copy(src_ref, src_gran, dst_ref, dst_gran, sem):
    src = pl.multiple_of(src_gran * GRAN, GRAN)
    dst = pl.multiple_of(dst_gran * GRAN, GRAN)
    return pltpu.make_async_copy(src_ref.at[pl.ds(src, GRAN), :], dst_ref.at[pl.ds(dst, GRAN), :], sem)


def _expert_kernel(te_ref, nt_ref, gsrc_ref, gdst_ref, ug_ref, wpar_ref, wnext_ref,
                   xs_ref, wg_hbm_ref, wu_hbm_ref, wd_hbm_ref,
                   ys_ref, xbuf, ybuf, zbuf, wg_ref, wu_ref, wd_ref, wg_stage, wu_stage, wd_stage,
                   in_sem, out_sem, zero_sem, w_sem):
    j = pl.program_id(0)
    n_tiles = nt_ref[0]
    slot = lax.rem(j, NBUF)
    prev_slot = lax.rem(j + NBUF - 1, NBUF)

    def tile_gather(step, s):
        for g in range(STEP_GRANS):
            _granule_copy(xs_ref, gsrc_ref[step * STEP_GRANS + g], xbuf.at[s], g,
                          in_sem.at[s]).start(priority=g % 2)

    def prev_scatter():
        for g in range(STEP_GRANS):
            _granule_copy(ybuf.at[prev_slot], g, ys_ref, gdst_ref[j * STEP_GRANS + g],
                          out_sem.at[prev_slot]).start(priority=g % 2)

    @pl.when(j == 0)
    def _():
        tile_gather(0, 0)
        ybuf[NBUF - 1] = jnp.zeros((STEP_ROWS, D_MODEL), BF16)
        zbuf[...] = jnp.zeros((GRAN, D_MODEL), BF16)
        for part in range(NBUF):
            spare = pltpu.make_async_copy(
                ybuf.at[NBUF - 1], ys_ref.at[pl.ds((SPARE_GRAN + part * STEP_GRANS) * GRAN, STEP_ROWS), :],
                out_sem.at[NBUF - 1])
            spare.start()
            spare.wait()

        for ahead in range(1, NBUF - 1):
            tile_gather(ahead, ahead)

    def zero_copy(t, g):
        return _granule_copy(zbuf, 0, ys_ref, t * LOCAL_GRANS + g, zero_sem)

    @pl.when(jnp.logical_and(j >= 1, j <= N_TOK_TILES))
    def _():
        def wait(g, c):
            zero_copy(j - 1, g).wait()
            return c

        lax.fori_loop(ug_ref[j - 1], LOCAL_GRANS, wait, 0)

    @pl.when(j < N_TOK_TILES)
    def _():
        def start(g, c):
            zero_copy(j, g).start()
            return c

        lax.fori_loop(ug_ref[j], LOCAL_GRANS, start, 0)

    def gather_wait():
        pltpu.make_async_copy(xs_ref.at[pl.ds(0, STEP_ROWS), :], xbuf.at[slot], in_sem.at[slot]).wait()

    @pl.when(jnp.logical_and(j >= NBUF - 1, j - NBUF < n_tiles))
    def _():
        pltpu.make_async_copy(ybuf.at[slot], ys_ref.at[pl.ds(0, STEP_ROWS), :], out_sem.at[slot]).wait()

    @pl.when(jnp.logical_and(j >= n_tiles, j < n_tiles + NBUF - 1))
    def _():
        gather_wait()

    @pl.when(j == n_tiles)
    def _():
        prev_scatter()

    def weight_copies(expert, s):
        return [pltpu.make_async_copy(hbm.at[expert], stage.at[s], w_sem.at[s])
                for hbm, stage in ((wg_hbm_ref, wg_stage), (wu_hbm_ref, wu_stage), (wd_hbm_ref, wd_stage))]

    @pl.when(j == 0)
    def _():
        for cp in weight_copies(te_ref[0], 0):
            cp.start()

    @pl.when(jnp.logical_and(j < n_tiles, jnp.logical_or(j == 0, te_ref[j] != te_ref[jnp.maximum(j - 1, 0)])))
    def _():
        s = wpar_ref[j]
        for cp in weight_copies(te_ref[j], s):
            cp.wait()
        wg_ref[...] = wg_stage[s].astype(BF16)
        wu_ref[...] = wu_stage[s].astype(BF16)
        wd_ref[...] = wd_stage[s].astype(BF16)
        nxt = wnext_ref[te_ref[j]]

        @pl.when(nxt >= 0)
        def _():
            for cp in weight_copies(nxt, 1 - s):
                cp.start()

    @pl.when(j < n_tiles)
    def _():
        gather_wait()
        chains = range(0, TE, CHAIN_ROWS)
        xb = [xbuf[slot, r:r + CHAIN_ROWS, :] for r in chains]
        hg = [jnp.dot(x, wg_ref[...], preferred_element_type=F32) for x in xb]
        hu = [jnp.dot(x, wu_ref[...], preferred_element_type=F32) for x in xb]
        tile_gather(j + NBUF - 1, prev_slot)
        prev_scatter()
        h = [(a * _sigmoid(a) * b).astype(BF16) for a, b in zip(hg, hu)]
        for hc, r in zip(h, chains):
            ybuf[slot, r:r + CHAIN_ROWS, :] = jnp.dot(hc, wd_ref[...], preferred_element_type=F32).astype(BF16)


def _expert_call(tile_expert, n_steps, gsrc, gdst, used_grans, w_parity, w_next, xs, wg, wu, wd):
    hbm = pl.BlockSpec(memory_space=pl.ANY)
    return pl.pallas_call(
        _expert_kernel,
        grid_spec=pltpu.PrefetchScalarGridSpec(
            num_scalar_prefetch=7,
            grid=(MAX_STEPS + NBUF,),
            in_specs=[hbm, hbm, hbm, hbm],
            out_specs=hbm,
            scratch_shapes=[pltpu.VMEM((NBUF, STEP_ROWS, D_MODEL), BF16),
                            pltpu.VMEM((NBUF, STEP_ROWS, D_MODEL), BF16),
                            pltpu.VMEM((GRAN, D_MODEL), BF16),
                            pltpu.VMEM((D_MODEL, D_EXPERT), BF16), pltpu.VMEM((D_MODEL, D_EXPERT), BF16),
                            pltpu.VMEM((D_EXPERT, D_MODEL), BF16),
                            pltpu.VMEM((2, D_MODEL, D_EXPERT), F32), pltpu.VMEM((2, D_MODEL, D_EXPERT), F32),
                            pltpu.VMEM((2, D_EXPERT, D_MODEL), F32),
                            pltpu.SemaphoreType.DMA((NBUF,)), pltpu.SemaphoreType.DMA((NBUF,)),
                            pltpu.SemaphoreType.DMA, pltpu.SemaphoreType.DMA((2,))],
        ),
        out_shape=jax.ShapeDtypeStruct(((SPARE_GRAN + NBUF * STEP_GRANS) * GRAN, D_MODEL), BF16),
        compiler_params=pltpu.CompilerParams(
            dimension_semantics=("arbitrary",), vmem_limit_bytes=VMEM_LIMIT),
        name="experts",
    )(tile_expert, n_steps, gsrc, gdst, used_grans, w_parity, w_next, xs, wg, wu, wd)


def _combine_kernel(uc_ref, ys_ref, x1_ref, rf_ref, g_ref, b_ref, out_ref, ysbuf, ys_sem):
    i = pl.program_id(0)
    subs = range(MERGE_SUB)
    chunks_per_tile = LOCAL_ROWS // YS_CHUNK

    def chunk_copies(step, do):
        for s in subs:
            tile = step * MERGE_SUB + s
            for c in range(chunks_per_tile):
                @pl.when(c < uc_ref[tile])
                def _():
                    do(pltpu.make_async_copy(
                        ys_ref.at[pl.ds(tile * LOCAL_ROWS + c * YS_CHUNK, YS_CHUNK), :],
                        ysbuf.at[lax.rem(step, 2), pl.ds(s * LOCAL_ROWS + c * YS_CHUNK, YS_CHUNK), :],
                        ys_sem.at[lax.rem(step, 2)]))

    @pl.when(i == 0)
    def _():
        ysbuf[...] = jnp.zeros(ysbuf.shape, BF16)
        chunk_copies(0, lambda cp: cp.start())

    @pl.when(i + 1 < pl.num_programs(0))
    def _():
        chunk_copies(i + 1, lambda cp: cp.start())

    chunk_copies(i, lambda cp: cp.wait())
    ys = ysbuf.at[lax.rem(i, 2)]

    col = lax.broadcasted_iota(I32, (TM, LOCAL_ROWS), 1).astype(F32)
    route = [rf_ref[:, s * TM:(s + 1) * TM].T for s in subs]
    unsort = [(jnp.where(col == r[:, 2:3], r[:, 0:1], 0.0)
               + jnp.where(col == r[:, 3:4], r[:, 1:2], 0.0)).astype(BF16) for r in route]
    ffn = [jnp.dot(unsort[s], ys[s * LOCAL_ROWS:(s + 1) * LOCAL_ROWS, :], preferred_element_type=F32)
           for s in subs]
    for s in subs:
        h = ALPHA * x1_ref[s * TM:(s + 1) * TM, :] + ffn[s]
        mu = jnp.mean(h, axis=-1, keepdims=True)
        hc = h - mu
        var = jnp.mean(hc * hc, axis=-1, keepdims=True)
        out_ref[s * TM:(s + 1) * TM, :] = hc * lax.rsqrt(var + LN_EPS) * g_ref[...] + b_ref[...]


def _combine_call(used_chunks, ys, x1, rf, g, b):
    tm = MERGE_SUB * TM
    return pl.pallas_call(
        _combine_kernel,
        grid_spec=pltpu.PrefetchScalarGridSpec(
            num_scalar_prefetch=1,
            grid=(TOKENS // tm,),
            in_specs=[pl.BlockSpec(memory_space=pl.ANY),
                      pl.BlockSpec((tm, D_MODEL), lambda i, uc: (i, 0)),
                      pl.BlockSpec((8, tm), lambda i, uc: (0, i)),
                      pl.BlockSpec((1, D_MODEL), lambda i, uc: (0, 0)),
                      pl.BlockSpec((1, D_MODEL), lambda i, uc: (0, 0))],
            out_specs=pl.BlockSpec((tm, D_MODEL), lambda i, uc: (i, 0)),
            scratch_shapes=[pltpu.VMEM((2, MERGE_SUB * LOCAL_ROWS, D_MODEL), BF16),
                            pltpu.SemaphoreType.DMA((2,))],
        ),
        out_shape=jax.ShapeDtypeStruct((TOKENS, D_MODEL), F32),
        compiler_params=pltpu.CompilerParams(
            dimension_semantics=("arbitrary",), vmem_limit_bytes=VMEM_LIMIT),
        name="combine_ln2",
    )(used_chunks, ys, x1, rf, g, b)


def _router_cols(w_router_group, w_router_expert):
    w = jnp.concatenate([w_router_group, jnp.zeros((D_MODEL, 4), F32), w_router_expert,
                         jnp.zeros((D_MODEL, 128 - ROUTER_ROWS), F32)], axis=1)
    hi = w.astype(BF16)
    lo = (w - hi.astype(F32)).astype(BF16)
    return hi, lo


def _layer(x, w_in, conv_w, w_out_conv, w_out_attn, w_o, ln1_g, ln1_b,
           w_router_group, w_router_expert, w_gate, w_up, w_down, ln2_g, ln2_b):
    slopes = jnp.asarray([2.0 ** (-8.0 * (h + 1) / N_HEADS) for h in range(N_HEADS)], F32)
    q, k, v, za, sgb = _proj_call(x, w_in, conv_w, w_out_conv)
    o_a, o_b = _attn_call(slopes, q, k, v)

    wr_hi, wr_lo = _router_cols(w_router_group, w_router_expert)
    x1, xs, rf, mt = _merge_call(
        o_a, o_b, za.reshape(TOKENS, D_MODEL), sgb.reshape(TOKENS, D_MODEL),
        x.reshape(TOKENS, D_MODEL), w_out_attn, w_o,
        ln1_g.reshape(1, D_MODEL), ln1_b.reshape(1, D_MODEL), wr_hi, wr_lo)

    grans = mt.reshape(N_TOK_TILES, N_EXPERTS, 128)[:, :, 0].astype(I32)
    local_start = jnp.cumsum(grans, axis=1) - grans
    grans_t = grans.T
    tiles_e = (jnp.sum(grans_t, axis=1) + TILE_GRANS - 1) // TILE_GRANS
    tile_end = jnp.cumsum(tiles_e)
    n_steps = ((tile_end[-1] + EXP_SUB - 1) // EXP_SUB).reshape(1)
    all_tiles = (MAX_STEPS + NBUF) * EXP_SUB
    tile_ids = jnp.arange(all_tiles, dtype=I32)
    tile_expert = jnp.minimum(
        jnp.sum((tile_ids[:, None] >= tile_end[None, :]).astype(I32), axis=1), N_EXPERTS - 1)
    run_slot = TILE_GRANS * (tile_end - tiles_e)[:, None] + jnp.cumsum(grans_t, axis=1) - grans_t
    run_src = jnp.arange(N_TOK_TILES, dtype=I32)[None, :] * LOCAL_GRANS + local_start.T
    pick = (tile_expert[:all_tiles, None] == jnp.arange(N_EXPERTS, dtype=I32)[None, :])[:, :, None]
    t_slot = jnp.sum(jnp.where(pick, run_slot[None], 0), axis=1)
    t_len = jnp.sum(jnp.where(pick, grans_t[None], 0), axis=1)
    t_src = jnp.sum(jnp.where(pick, run_src[None], 0), axis=1)
    slots = jnp.arange(all_tiles * TILE_GRANS, dtype=I32).reshape(all_tiles, TILE_GRANS)
    k = slots[:, :, None] - t_slot[:, None, :]
    hit = (k >= 0) & (k < t_len[:, None, :])
    gran = jnp.sum(jnp.where(hit, t_src[:, None, :] + k, 0), axis=2).reshape(-1)
    filled = (jnp.sum(hit.astype(I32), axis=2) > 0).reshape(-1)
    slots = slots.reshape(-1)
    gsrc = jnp.where(filled, gran, 0)
    gdst = jnp.where(filled, gran, SPARE_GRAN + slots % (NBUF * STEP_GRANS))
    gdst = jnp.concatenate([SPARE_GRAN + (NBUF - 1) * STEP_GRANS + jnp.arange(STEP_GRANS, dtype=I32), gdst])

    starts = jnp.concatenate([jnp.ones((1,), I32), (tile_expert[1:] != tile_expert[:-1]).astype(I32)])
    w_parity = (jnp.cumsum(starts) - 1) % 2
    ids = jnp.arange(N_EXPERTS, dtype=I32)
    later = jnp.where((tiles_e > 0)[None, :] & (ids[None, :] > ids[:, None]), ids[None, :], N_EXPERTS)
    w_next = jnp.min(later, axis=1)
    w_next = jnp.where(w_next == N_EXPERTS, -1, w_next)

    ys = _expert_call(tile_expert, n_steps, gsrc, gdst, jnp.sum(grans, axis=1), w_parity, w_next, xs,
                      w_gate, w_up, w_down)
    used_chunks = (jnp.sum(grans, axis=1) * GRAN + YS_CHUNK - 1) // YS_CHUNK
    out = _combine_call(used_chunks, ys, x1, rf, ln2_g.reshape(1, D_MODEL), ln2_b.reshape(1, D_MODEL))
    return out.reshape(BATCH, SEQ, D_MODEL)


def kernel(x, w_in, conv_w, w_out_conv, w_out_attn, w_o, ln1_g, ln1_b, w_router_group, w_router_expert, w_gate, w_up, w_down, ln2_g, ln2_b):
    depth = w_in.shape[0]
    for l in range(depth):
        x = _layer(x, w_in[l], conv_w[l], w_out_conv[l], w_out_attn[l], w_o[l], ln1_g[l], ln1_b[l],
                   w_router_group[l], w_router_expert[l], w_gate[l], w_up[l], w_down[l], ln2_g[l], ln2_b[l])
    return x
```

```python
import functools

import jax
import jax.numpy as jnp
from jax import lax
from jax.experimental import pallas as pl
from jax.experimental.pallas import tpu as pltpu

F32 = jnp.float32
BF16 = jnp.bfloat16
U32 = jnp.uint32
I32 = jnp.int32

D_MODEL = 1024
BATCH = 8
SEQ = 2048
TOKENS = BATCH * SEQ
CONV_WIDTH = 512
N_HEADS = 8
HEAD_DIM = 64
ATTN_WIDTH = N_HEADS * HEAD_DIM
MOBA_BLOCK = 256
N_BLOCKS = SEQ // MOBA_BLOCK
MOBA_TOPK = 3
N_GROUPS = 4
EXPERTS_PER_GROUP = 8
N_EXPERTS = N_GROUPS * EXPERTS_PER_GROUP
D_EXPERT = 256
LN_EPS = 1e-5
ALPHA = 2.0 ** 0.25
IN_COLS = 3 * CONV_WIDTH + 3 * ATTN_WIDTH + 2 * D_MODEL
HALF = D_MODEL // 2

TM = 256
TM_PROJ = 512
TE = 512
CHAIN_ROWS = 256
PV_ROWS = HEAD_DIM + 16
GRAN = 16
TILE_GRANS = TE // GRAN
N_TOK_TILES = TOKENS // TM
LOCAL_ROWS = -(-(2 * TM + N_EXPERTS * (GRAN - 1)) // 256) * 256
LOCAL_GRANS = LOCAL_ROWS // GRAN
SPARE_GRAN = N_TOK_TILES * LOCAL_GRANS
MAX_TILES = (2 * TOKENS + N_TOK_TILES * N_EXPERTS * (GRAN - 1)) // TE + N_EXPERTS
PAIRS = 4
MERGE_SUB = 2
COMBINE_SUB = 4
EXP_SUB = 1
STEP_ROWS = EXP_SUB * TE
STEP_GRANS = EXP_SUB * TILE_GRANS
MAX_STEPS = -(-MAX_TILES // EXP_SUB)
NBUF = 4
ROUTER_ROWS = 40
VMEM_LIMIT = 56 * 1024 * 1024
NEG_INF = float("-inf")


def _sigmoid(z):
    return 1.0 / (1.0 + jnp.exp(-z))


def _proj_kernel(x_ref, w_in_hbm_ref, convw_ref, woc_f32_ref, q_ref, k_ref, v_ref, za_ref, sgb_ref,
                 ubuf, w_in_ref, woc_ref, stage, stage_sem):
    s = pl.program_id(1)
    tm = TM_PROJ

    @pl.when((pl.program_id(0) == 0) & (s == 0))
    def _():
        def chunk(c):
            return pltpu.make_async_copy(w_in_hbm_ref.at[:, c * CONV_WIDTH:(c + 1) * CONV_WIDTH],
                                         stage.at[c % 2], stage_sem.at[c % 2])

        n_chunks = IN_COLS // CONV_WIDTH
        chunk(0).start()
        for c in range(n_chunks):
            if c + 1 < n_chunks:
                chunk(c + 1).start()
            chunk(c).wait()
            w_in_ref[:, c * CONV_WIDTH:(c + 1) * CONV_WIDTH] = stage[c % 2].astype(BF16)
        woc_ref[...] = woc_f32_ref[...].astype(BF16)

    xb = x_ref[0].astype(BF16)

    def proj(c0, c1):
        return jnp.dot(xb, w_in_ref[:, c0:c1], preferred_element_type=F32)

    c_b = proj(0, CONV_WIDTH)
    u = proj(CONV_WIDTH, 2 * CONV_WIDTH) * proj(2 * CONV_WIDTH, 3 * CONV_WIDTH)

    @pl.when(s == 0)
    def _():
        ubuf[0:8, :] = jnp.zeros((8, CONV_WIDTH), F32)

    ubuf[8:8 + tm, :] = u
    w = convw_ref[...]
    conv = w[2:3, :] * u + w[1:2, :] * ubuf[7:7 + tm, :] + w[0:1, :] * ubuf[6:6 + tm, :]
    ubuf[0:8, :] = u[tm - 8:tm, :]
    hc = (c_b * conv).astype(BF16)
    y_conv = jnp.dot(hc, woc_ref[...], preferred_element_type=F32)

    o = 3 * CONV_WIDTH
    q_ref[0] = (proj(o, o + ATTN_WIDTH) * (HEAD_DIM ** -0.5)).astype(BF16)
    k_ref[0] = proj(o + ATTN_WIDTH, o + 2 * ATTN_WIDTH).astype(BF16)
    v_ref[0] = proj(o + 2 * ATTN_WIDTH, o + 3 * ATTN_WIDTH).astype(BF16)
    o += 3 * ATTN_WIDTH
    za_ref[0] = (_sigmoid(proj(o, o + D_MODEL)) * y_conv).astype(BF16)
    sgb_ref[0] = _sigmoid(proj(o + D_MODEL, o + 2 * D_MODEL)).astype(BF16)


def _proj_call(x, w_in, conv_w, w_out_conv):
    tok_spec = lambda c: pl.BlockSpec((1, TM_PROJ, c), lambda b, s: (b, s, 0))
    full = lambda shape: pl.BlockSpec(shape, lambda b, s: (0,) * len(shape))
    once = lambda shape: pl.BlockSpec(shape, lambda b, s: (0,) * len(shape), pipeline_mode=pl.Buffered(1))
    return pl.pallas_call(
        _proj_kernel,
        grid=(BATCH, SEQ // TM_PROJ),
        in_specs=[tok_spec(D_MODEL), pl.BlockSpec(memory_space=pl.ANY), full((3, CONV_WIDTH)),
                  once((CONV_WIDTH, D_MODEL))],
        out_specs=[tok_spec(ATTN_WIDTH), tok_spec(ATTN_WIDTH), tok_spec(ATTN_WIDTH),
                   tok_spec(D_MODEL), tok_spec(D_MODEL)],
        out_shape=[jax.ShapeDtypeStruct((BATCH, SEQ, ATTN_WIDTH), BF16)] * 3
        + [jax.ShapeDtypeStruct((BATCH, SEQ, D_MODEL), BF16)] * 2,
        scratch_shapes=[pltpu.VMEM((TM_PROJ + 8, CONV_WIDTH), F32), pltpu.VMEM((D_MODEL, IN_COLS), BF16),
                        pltpu.VMEM((CONV_WIDTH, D_MODEL), BF16),
                        pltpu.VMEM((2, D_MODEL, CONV_WIDTH), F32), pltpu.SemaphoreType.DMA((2,))],
        compiler_params=pltpu.CompilerParams(
            dimension_semantics=("arbitrary", "arbitrary"), vmem_limit_bytes=VMEM_LIMIT),
        name="proj",
    )(x, w_in, conv_w, w_out_conv)


def _attn_kernel(slopes_ref, qa_ref, qb_ref, k_ref, v_ref, oa_ref, ob_ref,
                 kaug_ref, vt_ref, kmean_ref, qaug_ref, pv_ref, mloc_ref, t_ref, p_ref):
    hq = pl.program_id(1)
    j = pl.program_id(2)
    blk = MOBA_BLOCK
    pairs = range(PAIRS)
    lanes = [slice(128 * pp, 128 * (pp + 1)) for pp in pairs]

    @pl.when(j == 0)
    def _():
        klane = lax.broadcasted_iota(I32, (blk, 128), 1)
        koff = lax.broadcasted_iota(I32, (blk, 128), 0).astype(F32)
        k_extra = jnp.where(klane == 0, koff, jnp.where(klane == 1, 1.0, 0.0)).astype(BF16)
        orow = lax.broadcasted_iota(I32, (PV_ROWS - HEAD_DIM, blk), 0)
        ones_rows = jnp.where(orow == 0, 1.0, 0.0).astype(BF16)
        for pp in pairs:
            for n in range(N_BLOCKS):
                kblk = k_ref[0, n * blk:(n + 1) * blk, lanes[pp]]
                kaug_ref[pp, n, :, 0:128] = kblk
                kaug_ref[pp, n, :, 128:256] = k_extra
                kmean_ref[pp, n:n + 1, :] = jnp.mean(kblk.astype(F32), axis=0, keepdims=True)
                v_t = v_ref[0, n * blk:(n + 1) * blk, lanes[pp]].astype(F32).T.astype(BF16)
                for hh in range(2):
                    vt_ref[pp, n, hh, 0:HEAD_DIM, :] = v_t[hh * HEAD_DIM:(hh + 1) * HEAD_DIM, :]
                    vt_ref[pp, n, hh, HEAD_DIM:PV_ROWS, :] = ones_rows

    lane = lax.broadcasted_iota(I32, (1, 2 * blk), 1)
    qoff_row = jnp.where(lane < blk, lane, lane - blk).astype(F32)
    feat = lax.broadcasted_iota(I32, (2 * HEAD_DIM, blk), 0)
    arow = lax.broadcasted_iota(I32, (2 * HEAD_DIM, 2 * blk), 0)
    blk_i = lax.broadcasted_iota(I32, (N_BLOCKS, 2 * blk), 0)
    key_i = lax.broadcasted_iota(I32, (blk, 2 * blk), 0)
    qry_j = lax.broadcasted_iota(I32, (blk, 2 * blk), 1)
    causal = key_i <= jnp.where(qry_j < blk, qry_j, qry_j - blk)
    slope_rows, q_extras = [], []
    for pp in pairs:
        head = 2 * (PAIRS * hq + pp)
        slope_rows.append(jnp.where(lane < blk, slopes_ref[head], slopes_ref[head + 1]))
        q_extras.append(jnp.where(arow == 0, slope_rows[pp],
                                  jnp.where(arow == 1, -slope_rows[pp] * qoff_row, 0.0)).astype(BF16))

    def prepare(q_ref, pp, slot, qblock):
        q_t = q_ref[0, :, lanes[pp]].astype(F32).T
        qcat = jnp.concatenate([jnp.where(feat < HEAD_DIM, q_t, 0.0), jnp.where(feat >= HEAD_DIM, q_t, 0.0)],
                               axis=1).astype(BF16)
        qaug_ref[pp, slot, 0:2 * HEAD_DIM, :] = qcat
        qaug_ref[pp, slot, 2 * HEAD_DIM:4 * HEAD_DIM, :] = q_extras[pp]
        gate = jnp.dot(kmean_ref[pp].astype(BF16), qcat, preferred_element_type=F32)
        cnt = jnp.zeros((N_BLOCKS, 2 * blk), F32)
        for m in range(N_BLOCKS):
            gm = gate[m:m + 1, :]
            beats = (gm > gate) | ((gm == gate) & (blk_i > m))
            cnt = cnt + jnp.where(beats & (qblock > m), 1.0, 0.0)
        return jnp.where((blk_i < qblock) & (cnt < float(MOBA_TOPK)), 1.0, 0.0)

    qblock_a = j
    qblock_b = N_BLOCKS - 1 - j
    sel_a = [prepare(qa_ref, pp, 0, qblock_a) for pp in pairs]
    sel_b = [prepare(qb_ref, pp, 1, qblock_b) for pp in pairs]

    n_mid = N_BLOCKS - 1
    slots = [(0, 0, qblock_a, True)]
    mids = []
    for s in range(1, n_mid + 1):
        is_a = s <= j
        slots.append((s, jnp.where(is_a, 0, 1), jnp.where(is_a, s - 1, s - 1 - j), False))
        mids.append((is_a, slots[-1][2]))
    slots.append((n_mid + 1, 1, qblock_b, True))

    for pp in pairs:
        for s, which, kb, _ in slots:
            t_ref[pp, s] = jnp.dot(kaug_ref[pp, kb], qaug_ref[pp, which], preferred_element_type=F32)
    for pp in pairs:
        for s, _, _, own in slots:
            t = t_ref[pp, s]
            if own:
                t = jnp.where(causal, t, NEG_INF)
            m_loc = jnp.max(t, axis=0, keepdims=True)
            p_ref[pp, s] = jnp.exp((t - m_loc).astype(BF16))
            mloc_ref[pp, s:s + 1, :] = m_loc
    for pp in pairs:
        for s, _, kb, _ in slots:
            pv_ref[pp, s, 0] = jnp.dot(vt_ref[pp, kb, 0], p_ref[pp, s, :, 0:blk], preferred_element_type=F32)
            pv_ref[pp, s, 1] = jnp.dot(vt_ref[pp, kb, 1], p_ref[pp, s, :, blk:2 * blk],
                                       preferred_element_type=F32)

    def combine(o_ref, pp, own_slot, sel, qblock, mine):
        neg = jnp.full((1, 2 * blk), -1e30, F32)
        pieces = [(own_slot, mloc_ref[pp, own_slot:own_slot + 1, :])]
        for s, (is_a, kb) in enumerate(mids, start=1):
            selrow = jnp.sum(jnp.where(blk_i == kb, sel, 0.0), axis=0, keepdims=True)
            belongs = jnp.where(is_a, 1.0, 0.0) if mine else jnp.where(is_a, 0.0, 1.0)
            used = selrow * belongs > 0.5
            shift = slope_rows[pp] * ((kb - qblock) * blk).astype(F32)
            pieces.append((s, jnp.where(used, mloc_ref[pp, s:s + 1, :] + shift, neg)))
        m_all = pieces[0][1]
        for _, m_s in pieces[1:]:
            m_all = jnp.maximum(m_all, m_s)
        acc = [jnp.zeros((PV_ROWS, blk), F32), jnp.zeros((PV_ROWS, blk), F32)]
        for s, m_s in pieces:
            w = jnp.exp(m_s - m_all)
            for hh in range(2):
                acc[hh] = acc[hh] + pv_ref[pp, s, hh] * w[:, hh * blk:(hh + 1) * blk]
        o_t = jnp.concatenate([a[0:HEAD_DIM, :] / a[HEAD_DIM:HEAD_DIM + 1, :] for a in acc], axis=0)
        o_ref[0, :, lanes[pp]] = o_t.T.astype(BF16)

    for pp in pairs:
        combine(oa_ref, pp, 0, sel_a[pp], qblock_a, True)
        combine(ob_ref, pp, n_mid + 1, sel_b[pp], qblock_b, False)


def _attn_call(slopes, q, k, v):
    half = N_BLOCKS // 2
    width = 128 * PAIRS
    o_a, o_b = pl.pallas_call(
        _attn_kernel,
        grid_spec=pltpu.PrefetchScalarGridSpec(
            num_scalar_prefetch=1,
            grid=(BATCH, N_HEADS // (2 * PAIRS), half),
            in_specs=[
                pl.BlockSpec((1, MOBA_BLOCK, width), lambda b, h, j, sl: (b, j, h)),
                pl.BlockSpec((1, MOBA_BLOCK, width), lambda b, h, j, sl: (b, N_BLOCKS - 1 - j, h)),
                pl.BlockSpec((1, SEQ, width), lambda b, h, j, sl: (b, 0, h)),
                pl.BlockSpec((1, SEQ, width), lambda b, h, j, sl: (b, 0, h)),
            ],
            out_specs=[pl.BlockSpec((1, MOBA_BLOCK, width), lambda b, h, j, sl: (b, j, h)),
                       pl.BlockSpec((1, MOBA_BLOCK, width), lambda b, h, j, sl: (b, half - 1 - j, h))],
            scratch_shapes=[
                pltpu.VMEM((PAIRS, N_BLOCKS, MOBA_BLOCK, 256), BF16),
                pltpu.VMEM((PAIRS, N_BLOCKS, 2, PV_ROWS, MOBA_BLOCK), BF16),
                pltpu.VMEM((PAIRS, N_BLOCKS, 128), F32),
                pltpu.VMEM((PAIRS, 2, 256, 2 * MOBA_BLOCK), BF16),
                pltpu.VMEM((PAIRS, N_BLOCKS + 1, 2, PV_ROWS, MOBA_BLOCK), F32),
                pltpu.VMEM((PAIRS, 16, 2 * MOBA_BLOCK), F32),
                pltpu.VMEM((PAIRS, N_BLOCKS + 1, MOBA_BLOCK, 2 * MOBA_BLOCK), F32),
                pltpu.VMEM((PAIRS, N_BLOCKS + 1, MOBA_BLOCK, 2 * MOBA_BLOCK), BF16),
            ],
        ),
        out_shape=[jax.ShapeDtypeStruct((BATCH, SEQ // 2, ATTN_WIDTH), BF16)] * 2,
        compiler_params=pltpu.CompilerParams(
            dimension_semantics=("arbitrary", "arbitrary", "arbitrary"), vmem_limit_bytes=VMEM_LIMIT),
        name="moba_attn",
    )(slopes, q, q, k, v)
    return o_a, o_b


def _route(logits):
    row8 = lax.broadcasted_iota(I32, (8, TM), 0).astype(F32)
    gl = jnp.where(row8 < float(N_GROUPS), logits[0:8, :], NEG_INF)
    gexp = jnp.exp(gl - jnp.max(gl, axis=0, keepdims=True))
    gprob = gexp / jnp.sum(gexp, axis=0, keepdims=True)
    ptop = jnp.max(gprob, axis=0, keepdims=True)
    gtop = jnp.min(jnp.where(gprob == ptop, row8, 8.0), axis=0, keepdims=True)
    el = logits[8:ROUTER_ROWS, :]
    eg = jnp.where(gtop == 0.0, el[0:8, :],
                   jnp.where(gtop == 1.0, el[8:16, :], jnp.where(gtop == 2.0, el[16:24, :], el[24:32, :])))
    m1 = jnp.max(eg, axis=0, keepdims=True)
    i1 = jnp.min(jnp.where(eg == m1, row8, 8.0), axis=0, keepdims=True)
    eg2 = jnp.where(row8 == i1, NEG_INF, eg)
    m2 = jnp.max(eg2, axis=0, keepdims=True)
    i2 = jnp.min(jnp.where(eg2 == m2, row8, 8.0), axis=0, keepdims=True)
    t2 = jnp.exp(m2 - m1)
    gate1 = ptop * (1.0 / (1.0 + t2))
    gate2 = ptop * (t2 / (1.0 + t2))
    erow = lax.broadcasted_iota(I32, (N_EXPERTS, TM), 0).astype(F32)
    oh1 = jnp.where(erow == gtop * float(EXPERTS_PER_GROUP) + i1, 1.0, 0.0)
    oh2 = jnp.where(erow == gtop * float(EXPERTS_PER_GROUP) + i2, 1.0, 0.0)
    return gate1, gate2, oh1, oh2


def _merge_kernel(oa_ref, ob_ref, za_ref, sgb_ref, x_ref, woa_f32_ref, wo_f32_ref, g_ref, b_ref,
                  wr_hi_ref, wr_lo_ref, x1_ref, xs_ref, rf_ref, mt_ref, woa_ref, wo_ref):
    i = pl.program_id(0)

    @pl.when(i == 0)
    def _():
        woa_ref[...] = woa_f32_ref[...].astype(BF16)
        wo_ref[...] = wo_f32_ref[...].astype(BF16)

    subs = range(MERGE_SUB)
    rows = [slice(s * TM, (s + 1) * TM) for s in subs]
    steps_per_batch = N_BLOCKS // MERGE_SUB
    in_oa = lax.rem(i, steps_per_batch) < steps_per_batch // 2
    o = [jnp.where(in_oa, oa_ref[0, r, :], ob_ref[0, r, :]) for r in rows]
    y_attn = [jnp.dot(o[s], woa_ref[...], preferred_element_type=F32) for s in subs]
    y = [(za_ref[rows[s], :].astype(F32) + sgb_ref[rows[s], :].astype(F32) * y_attn[s]).astype(BF16)
         for s in subs]
    mix = [jnp.dot(y[s], wo_ref[...], preferred_element_type=F32) for s in subs]
    x1 = []
    for s in subs:
        h = ALPHA * x_ref[rows[s], :] + mix[s]
        mu = jnp.mean(h, axis=-1, keepdims=True)
        hc = h - mu
        var = jnp.mean(hc * hc, axis=-1, keepdims=True)
        x1.append(hc * lax.rsqrt(var + LN_EPS) * g_ref[...] + b_ref[...])
        x1_ref[rows[s], :] = x1[s]

    xh = [x1[s].astype(BF16) for s in subs]
    xl = [(x1[s] - xh[s].astype(F32)).astype(BF16) for s in subs]
    wh = wr_hi_ref[...]
    logits = [(jnp.dot(xh[s], wh, preferred_element_type=F32)
               + jnp.dot(xl[s], wh, preferred_element_type=F32)
               + jnp.dot(xh[s], wr_lo_ref[...], preferred_element_type=F32)).T for s in subs]
    routes = [_route(logits[s]) for s in subs]

    ta = lax.broadcasted_iota(I32, (TM, TM), 0)
    tb = lax.broadcasted_iota(I32, (TM, TM), 1)
    upper = jnp.where(ta < tb, 1.0, 0.0).astype(BF16)
    ea = lax.broadcasted_iota(I32, (N_EXPERTS, N_EXPERTS), 0)
    eb = lax.broadcasted_iota(I32, (N_EXPERTS, N_EXPERTS), 1)
    lower = jnp.where(eb < ea, 1.0, 0.0).astype(BF16)
    lrow = lax.broadcasted_iota(I32, (LOCAL_ROWS, TM), 0).astype(F32)
    zero = jnp.zeros((1, TM), F32)
    cum = [jnp.dot((routes[s][2] + routes[s][3]).astype(BF16), upper, preferred_element_type=F32) for s in subs]
    perm = []
    for s in subs:
        gate1, gate2, oh1, oh2 = routes[s]
        n_e = jnp.sum(oh1 + oh2, axis=1, keepdims=True)
        m_rep = jnp.broadcast_to(jnp.floor((n_e + float(GRAN - 1)) * (1.0 / GRAN)), (N_EXPERTS, 128))
        run_start = jnp.dot(lower, m_rep.astype(BF16), preferred_element_type=F32)
        tot = cum[s] + float(GRAN) * run_start[:, 0:1]
        lp1 = jnp.sum(oh1 * tot, axis=0, keepdims=True)
        lp2 = jnp.sum(oh2 * tot, axis=0, keepdims=True)
        perm.append(jnp.where((lrow == lp1) | (lrow == lp2), 1.0, 0.0).astype(BF16))
        rf_ref[:, rows[s]] = jnp.concatenate([gate1, gate2, lp1, lp2, zero, zero, zero, zero], axis=0)
        mt_ref[s * N_EXPERTS:(s + 1) * N_EXPERTS, :] = m_rep
    for s in subs:
        xs_ref[s * LOCAL_ROWS:(s + 1) * LOCAL_ROWS, :] = jnp.dot(
            perm[s], xh[s], preferred_element_type=F32).astype(BF16)


def _merge_call(o_a, o_b, za, sgb, x, woa, wo, g, b, wr_hi, wr_lo):
    tm = MERGE_SUB * TM
    tok = lambda c: pl.BlockSpec((tm, c), lambda i: (i, 0))
    full = lambda shape: pl.BlockSpec(shape, lambda i: (0,) * len(shape))
    per_batch = SEQ // tm
    half = per_batch // 2
    o_a_spec = pl.BlockSpec((1, tm, ATTN_WIDTH), lambda i: (i // per_batch, jnp.minimum(i % per_batch, half - 1), 0))
    o_b_spec = pl.BlockSpec((1, tm, ATTN_WIDTH), lambda i: (i // per_batch, jnp.maximum(i % per_batch - half, 0), 0))
    return pl.pallas_call(
        _merge_kernel,
        grid=(TOKENS // tm,),
        in_specs=[o_a_spec, o_b_spec, tok(D_MODEL), tok(D_MODEL), tok(D_MODEL),
                  full((ATTN_WIDTH, D_MODEL)), full((D_MODEL, D_MODEL)), full((1, D_MODEL)),
                  full((1, D_MODEL)), full((D_MODEL, 128)), full((D_MODEL, 128))],
        out_specs=[tok(D_MODEL), pl.BlockSpec((MERGE_SUB * LOCAL_ROWS, D_MODEL), lambda i: (i, 0)),
                   pl.BlockSpec((8, tm), lambda i: (0, i)),
                   pl.BlockSpec((MERGE_SUB * N_EXPERTS, 128), lambda i: (i, 0))],
        out_shape=[jax.ShapeDtypeStruct((TOKENS, D_MODEL), F32),
                   jax.ShapeDtypeStruct((N_TOK_TILES * LOCAL_ROWS, D_MODEL), BF16),
                   jax.ShapeDtypeStruct((8, TOKENS), F32),
                   jax.ShapeDtypeStruct((N_TOK_TILES * N_EXPERTS, 128), F32)],
        scratch_shapes=[pltpu.VMEM((ATTN_WIDTH, D_MODEL), BF16), pltpu.VMEM((D_MODEL, D_MODEL), BF16)],
        compiler_params=pltpu.CompilerParams(
            dimension_semantics=("arbitrary",), vmem_limit_bytes=VMEM_LIMIT),
        name="merge_ln1_route",
    )(o_a, o_b, za, sgb, x, woa, wo, g, b, wr_hi, wr_lo)


def _granule_copy(src_ref, src_gran, dst_ref, dst_gran, sem):
    src = pl.multiple_of(src_gran * GRAN, GRAN)
    dst = pl.multiple_of(dst_gran * GRAN, GRAN)
    return pltpu.make_async_copy(src_ref.at[pl.ds(src, GRAN), :], dst_ref.at[pl.ds(dst, GRAN), :], sem)


def _expert_kernel(te_ref, nt_ref, gsrc_ref, gdst_ref, ug_ref, wpar_ref, wnext_ref,
                   xs_ref, wg_hbm_ref, wu_hbm_ref, wd_hbm_ref,
                   ys_ref, xbuf, ybuf, zbuf, wg_ref, wu_ref, wd_ref, wg_stage, wu_stage, wd_stage,
                   in_sem, out_sem, zero_sem, w_sem):
    j = pl.program_id(0)
    n_tiles = nt_ref[0]
    slot = lax.rem(j, NBUF)
    prev_slot = lax.rem(j + NBUF - 1, NBUF)

    def tile_gather(step, s):
        for g in range(STEP_GRANS):
            _granule_copy(xs_ref, gsrc_ref[step * STEP_GRANS + g], xbuf.at[s], g,
                          in_sem.at[s]).start(priority=g % 2)

    def prev_scatter():
        for g in range(STEP_GRANS):
            _granule_copy(ybuf.at[prev_slot], g, ys_ref, gdst_ref[j * STEP_GRANS + g],
                          out_sem.at[prev_slot]).start(priority=g % 2)

    @pl.when(j == 0)
    def _():
        tile_gather(0, 0)
        ybuf[NBUF - 1] = jnp.zeros((STEP_ROWS, D_MODEL), BF16)
        zbuf[...] = jnp.zeros((GRAN, D_MODEL), BF16)
        for part in range(NBUF):
            spare = pltpu.make_async_copy(
                ybuf.at[NBUF - 1], ys_ref.at[pl.ds((SPARE_GRAN + part * STEP_GRANS) * GRAN, STEP_ROWS), :],
                out_sem.at[NBUF - 1])
            spare.start()
            spare.wait()

        for ahead in range(1, NBUF - 1):
            tile_gather(ahead, ahead)

    def zero_copy(t, g):
        return _granule_copy(zbuf, 0, ys_ref, t * LOCAL_GRANS + g, zero_sem)

    @pl.when(jnp.logical_and(j >= 1, j <= N_TOK_TILES))
    def _():
        def wait(g, c):
            zero_copy(j - 1, g).wait()
            return c

        lax.fori_loop(ug_ref[j - 1], LOCAL_GRANS, wait, 0)

    @pl.when(j < N_TOK_TILES)
    def _():
        def start(g, c):
            zero_copy(j, g).start()
            return c

        lax.fori_loop(ug_ref[j], LOCAL_GRANS, start, 0)

    def gather_wait():
        pltpu.make_async_copy(xs_ref.at[pl.ds(0, STEP_ROWS), :], xbuf.at[slot], in_sem.at[slot]).wait()

    @pl.when(jnp.logical_and(j >= NBUF - 1, j - NBUF < n_tiles))
    def _():
        pltpu.make_async_copy(ybuf.at[slot], ys_ref.at[pl.ds(0, STEP_ROWS), :], out_sem.at[slot]).wait()

    @pl.when(jnp.logical_and(j >= n_tiles, j < n_tiles + NBUF - 1))
    def _():
        gather_wait()

    @pl.when(j == n_tiles)
    def _():
        prev_scatter()

    def weight_copies(expert, s):
        return [pltpu.make_async_copy(hbm.at[expert], stage.at[s], w_sem.at[s])
                for hbm, stage in ((wg_hbm_ref, wg_stage), (wu_hbm_ref, wu_stage), (wd_hbm_ref, wd_stage))]

    @pl.when(j == 0)
    def _():
        for cp in weight_copies(te_ref[0], 0):
            cp.start()

    @pl.when(jnp.logical_and(j < n_tiles, jnp.logical_or(j == 0, te_ref[j] != te_ref[jnp.maximum(j - 1, 0)])))
    def _():
        s = wpar_ref[j]
        for cp in weight_copies(te_ref[j], s):
            cp.wait()
        wg_ref[...] = wg_stage[s].astype(BF16)
        wu_ref[...] = wu_stage[s].astype(BF16)
        wd_ref[...] = wd_stage[s].astype(BF16)
        nxt = wnext_ref[te_ref[j]]

        @pl.when(nxt >= 0)
        def _():
            for cp in weight_copies(nxt, 1 - s):
                cp.start()

    @pl.when(j < n_tiles)
    def _():
        gather_wait()
        chains = range(0, TE, CHAIN_ROWS)
        xb = [xbuf[slot, r:r + CHAIN_ROWS, :] for r in chains]
        hg = [jnp.dot(x, wg_ref[...], preferred_element_type=F32) for x in xb]
        hu = [jnp.dot(x, wu_ref[...], preferred_element_type=F32) for x in xb]
        tile_gather(j + NBUF - 1, prev_slot)
        prev_scatter()
        h = [(a * _sigmoid(a) * b).astype(BF16) for a, b in zip(hg, hu)]
        for hc, r in zip(h, chains):
            ybuf[slot, r:r + CHAIN_ROWS, :] = jnp.dot(hc, wd_ref[...], preferred_element_type=F32).astype(BF16)


def _expert_call(tile_expert, n_steps, gsrc, gdst, used_grans, w_parity, w_next, xs, wg, wu, wd):
    hbm = pl.BlockSpec(memory_space=pl.ANY)
    return pl.pallas_call(
        _expert_kernel,
        grid_spec=pltpu.PrefetchScalarGridSpec(
            num_scalar_prefetch=7,
            grid=(MAX_STEPS + NBUF,),
            in_specs=[hbm, hbm, hbm, hbm],
            out_specs=hbm,
            scratch_shapes=[pltpu.VMEM((NBUF, STEP_ROWS, D_MODEL), BF16),
                            pltpu.VMEM((NBUF, STEP_ROWS, D_MODEL), BF16),
                            pltpu.VMEM((GRAN, D_MODEL), BF16),
                            pltpu.VMEM((D_MODEL, D_EXPERT), BF16), pltpu.VMEM((D_MODEL, D_EXPERT), BF16),
                            pltpu.VMEM((D_EXPERT, D_MODEL), BF16),
                            pltpu.VMEM((2, D_MODEL, D_EXPERT), F32), pltpu.VMEM((2, D_MODEL, D_EXPERT), F32),
                            pltpu.VMEM((2, D_EXPERT, D_MODEL), F32),
                            pltpu.SemaphoreType.DMA((NBUF,)), pltpu.SemaphoreType.DMA((NBUF,)),
                            pltpu.SemaphoreType.DMA, pltpu.SemaphoreType.DMA((2,))],
        ),
        out_shape=jax.ShapeDtypeStruct(((SPARE_GRAN + NBUF * STEP_GRANS) * GRAN, D_MODEL), BF16),
        compiler_params=pltpu.CompilerParams(
            dimension_semantics=("arbitrary",), vmem_limit_bytes=VMEM_LIMIT),
        name="experts",
    )(tile_expert, n_steps, gsrc, gdst, used_grans, w_parity, w_next, xs, wg, wu, wd)


def _combine_kernel(ys_ref, x1_ref, rf_ref, g_ref, b_ref, out_ref):
    subs = range(COMBINE_SUB)
    col = lax.broadcasted_iota(I32, (TM, LOCAL_ROWS), 1).astype(F32)
    route = [rf_ref[:, s * TM:(s + 1) * TM].T for s in subs]
    unsort = [(jnp.where(col == r[:, 2:3], r[:, 0:1], 0.0)
               + jnp.where(col == r[:, 3:4], r[:, 1:2], 0.0)).astype(BF16) for r in route]
    ffn = [jnp.dot(unsort[s], ys_ref[s * LOCAL_ROWS:(s + 1) * LOCAL_ROWS, :], preferred_element_type=F32)
           for s in subs]
    for s in subs:
        h = ALPHA * x1_ref[s * TM:(s + 1) * TM, :] + ffn[s]
        mu = jnp.mean(h, axis=-1, keepdims=True)
        hc = h - mu
        var = jnp.mean(hc * hc, axis=-1, keepdims=True)
        out_ref[s * TM:(s + 1) * TM, :] = hc * lax.rsqrt(var + LN_EPS) * g_ref[...] + b_ref[...]


def _combine_call(ys, x1, rf, g, b):
    tm = COMBINE_SUB * TM
    return pl.pallas_call(
        _combine_kernel,
        grid=(TOKENS // tm,),
        in_specs=[pl.BlockSpec((COMBINE_SUB * LOCAL_ROWS, D_MODEL), lambda i: (i, 0)),
                  pl.BlockSpec((tm, D_MODEL), lambda i: (i, 0)),
                  pl.BlockSpec((8, tm), lambda i: (0, i)),
                  pl.BlockSpec((1, D_MODEL), lambda i: (0, 0)),
                  pl.BlockSpec((1, D_MODEL), lambda i: (0, 0))],
        out_specs=pl.BlockSpec((tm, D_MODEL), lambda i: (i, 0)),
        out_shape=jax.ShapeDtypeStruct((TOKENS, D_MODEL), F32),
        compiler_params=pltpu.CompilerParams(
            dimension_semantics=("arbitrary",), vmem_limit_bytes=VMEM_LIMIT),
        name="combine_ln2",
    )(ys, x1, rf, g, b)


def _router_cols(w_router_group, w_router_expert):
    w = jnp.concatenate([w_router_group, jnp.zeros((D_MODEL, 4), F32), w_router_expert,
                         jnp.zeros((D_MODEL, 128 - ROUTER_ROWS), F32)], axis=1)
    hi = w.astype(BF16)
    lo = (w - hi.astype(F32)).astype(BF16)
    return hi, lo


def _layer(x, w_in, conv_w, w_out_conv, w_out_attn, w_o, ln1_g, ln1_b,
           w_router_group, w_router_expert, w_gate, w_up, w_down, ln2_g, ln2_b):
    slopes = jnp.asarray([2.0 ** (-8.0 * (h + 1) / N_HEADS) for h in range(N_HEADS)], F32)
    q, k, v, za, sgb = _proj_call(x, w_in, conv_w, w_out_conv)
    o_a, o_b = _attn_call(slopes, q, k, v)

    wr_hi, wr_lo = _router_cols(w_router_group, w_router_expert)
    x1, xs, rf, mt = _merge_call(
        o_a, o_b, za.reshape(TOKENS, D_MODEL), sgb.reshape(TOKENS, D_MODEL),
        x.reshape(TOKENS, D_MODEL), w_out_attn, w_o,
        ln1_g.reshape(1, D_MODEL), ln1_b.reshape(1, D_MODEL), wr_hi, wr_lo)

    grans = mt.reshape(N_TOK_TILES, N_EXPERTS, 128)[:, :, 0].astype(I32)
    local_start = jnp.cumsum(grans, axis=1) - grans
    grans_t = grans.T
    tiles_e = (jnp.sum(grans_t, axis=1) + TILE_GRANS - 1) // TILE_GRANS
    tile_end = jnp.cumsum(tiles_e)
    n_steps = ((tile_end[-1] + EXP_SUB - 1) // EXP_SUB).reshape(1)
    all_tiles = (MAX_STEPS + NBUF) * EXP_SUB
    tile_ids = jnp.arange(all_tiles, dtype=I32)
    tile_expert = jnp.minimum(
        jnp.sum((tile_ids[:, None] >= tile_end[None, :]).astype(I32), axis=1), N_EXPERTS - 1)
    run_slot = TILE_GRANS * (tile_end - tiles_e)[:, None] + jnp.cumsum(grans_t, axis=1) - grans_t
    run_src = jnp.arange(N_TOK_TILES, dtype=I32)[None, :] * LOCAL_GRANS + local_start.T
    pick = (tile_expert[:all_tiles, None] == jnp.arange(N_EXPERTS, dtype=I32)[None, :])[:, :, None]
    t_slot = jnp.sum(jnp.where(pick, run_slot[None], 0), axis=1)
    t_len = jnp.sum(jnp.where(pick, grans_t[None], 0), axis=1)
    t_src = jnp.sum(jnp.where(pick, run_src[None], 0), axis=1)
    slots = jnp.arange(all_tiles * TILE_GRANS, dtype=I32).reshape(all_tiles, TILE_GRANS)
    k = slots[:, :, None] - t_slot[:, None, :]
    hit = (k >= 0) & (k < t_len[:, None, :])
    gran = jnp.sum(jnp.where(hit, t_src[:, None, :] + k, 0), axis=2).reshape(-1)
    filled = (jnp.sum(hit.astype(I32), axis=2) > 0).reshape(-1)
    slots = slots.reshape(-1)
    gsrc = jnp.where(filled, gran, 0)
    gdst = jnp.where(filled, gran, SPARE_GRAN + slots % (NBUF * STEP_GRANS))
    gdst = jnp.concatenate([SPARE_GRAN + (NBUF - 1) * STEP_GRANS + jnp.arange(STEP_GRANS, dtype=I32), gdst])

    starts = jnp.concatenate([jnp.ones((1,), I32), (tile_expert[1:] != tile_expert[:-1]).astype(I32)])
    w_parity = (jnp.cumsum(starts) - 1) % 2
    ids = jnp.arange(N_EXPERTS, dtype=I32)
    later = jnp.where((tiles_e > 0)[None, :] & (ids[None, :] > ids[:, None]), ids[None, :], N_EXPERTS)
    w_next = jnp.min(later, axis=1)
    w_next = jnp.where(w_next == N_EXPERTS, -1, w_next)

    ys = _expert_call(tile_expert, n_steps, gsrc, gdst, jnp.sum(grans, axis=1), w_parity, w_next, xs,
                      w_gate, w_up, w_down)
    out = _combine_call(ys, x1, rf, ln2_g.reshape(1, D_MODEL), ln2_b.reshape(1, D_MODEL))
    return out.reshape(BATCH, SEQ, D_MODEL)


def kernel(x, w_in, conv_w, w_out_conv, w_out_attn, w_o, ln1_g, ln1_b, w_router_group, w_router_expert, w_gate, w_up, w_down, ln2_g, ln2_b):
    depth = w_in.shape[0]
    for l in range(depth):
        x = _layer(x, w_in[l], conv_w[l], w_out_conv[l], w_out_attn[l], w_o[l], ln1_g[l], ln1_b[l],
                   w_router_group[l], w_router_expert[l], w_gate[l], w_up[l], w_down[l], ln2_g[l], ln2_b[l])
    return x
```

```python
import jax
import jax.numpy as jnp
from jax import lax
from jax.experimental import pallas as pl
from jax.experimental.pallas import tpu as pltpu

F32 = jnp.float32
BF16 = jnp.bfloat16
I32 = jnp.int32

V7X_VMEM_BYTES = 64 * 1024 * 1024
V7X_LANES = 128
V7X_MXU_DIM = 256

D_MODEL = 1024
BATCH = 8
SEQ = 2048
TOKENS = BATCH * SEQ
CONV_WIDTH = 512
N_HEADS = 8
HEAD_DIM = 64
ATTN_WIDTH = N_HEADS * HEAD_DIM
MOBA_BLOCK = 256
N_BLOCKS = SEQ // MOBA_BLOCK
MOBA_TOPK = 3
N_GROUPS = 4
EXPERTS_PER_GROUP = 8
N_EXPERTS = N_GROUPS * EXPERTS_PER_GROUP
D_EXPERT = 256
LN_EPS = 1e-5
ALPHA = 2.0 ** 0.25
IN_COLS = 3 * CONV_WIDTH + 3 * ATTN_WIDTH + 2 * D_MODEL

TM = 256
TM_PROJ = 512
TE = 512
CHAIN_ROWS = 256
PV_ROWS = HEAD_DIM + 16
GRAN = 16
TILE_GRANS = TE // GRAN
N_TOK_TILES = TOKENS // TM
LOCAL_ROWS = -(-(2 * TM + N_EXPERTS * (GRAN - 1)) // V7X_MXU_DIM) * V7X_MXU_DIM
LOCAL_GRANS = LOCAL_ROWS // GRAN
SPARE_GRAN = N_TOK_TILES * LOCAL_GRANS
MAX_TILES = (2 * TOKENS + N_TOK_TILES * N_EXPERTS * (GRAN - 1)) // TE + N_EXPERTS
PAIRS = 4
MERGE_SUB = 2
COMBINE_SUB = 4
STEP_ROWS = TE
STEP_GRANS = TILE_GRANS
MAX_STEPS = MAX_TILES
NBUF = 4
ROUTER_ROWS = 40
VMEM_LIMIT = V7X_VMEM_BYTES - 8 * 1024 * 1024
NEG_INF = float("-inf")


def _sigmoid(z):
    return 1.0 / (1.0 + jnp.exp(-z))


def _proj_kernel(x_ref, w_in_hbm_ref, convw_ref, woc_f32_ref, q_ref, k_ref, v_ref, za_ref, sgb_ref,
                 ubuf, w_in_ref, woc_ref, stage, stage_sem):
    s = pl.program_id(1)
    tm = TM_PROJ

    @pl.when((pl.program_id(0) == 0) & (s == 0))
    def _():
        def chunk(c):
            return pltpu.make_async_copy(w_in_hbm_ref.at[:, c * CONV_WIDTH:(c + 1) * CONV_WIDTH],
                                         stage.at[c % 2], stage_sem.at[c % 2])

        n_chunks = IN_COLS // CONV_WIDTH
        chunk(0).start()
        for c in range(n_chunks):
            if c + 1 < n_chunks:
                chunk(c + 1).start()
            chunk(c).wait()
            w_in_ref[:, c * CONV_WIDTH:(c + 1) * CONV_WIDTH] = stage[c % 2].astype(BF16)
        woc_ref[...] = woc_f32_ref[...].astype(BF16)

    xb = x_ref[0].astype(BF16)

    def proj(c0, c1):
        return jnp.dot(xb, w_in_ref[:, c0:c1], preferred_element_type=F32)

    c_b = proj(0, CONV_WIDTH)
    u = proj(CONV_WIDTH, 2 * CONV_WIDTH) * proj(2 * CONV_WIDTH, 3 * CONV_WIDTH)

    @pl.when(s == 0)
    def _():
        ubuf[0:8, :] = jnp.zeros((8, CONV_WIDTH), F32)

    ubuf[8:8 + tm, :] = u
    w = convw_ref[...]
    conv = w[2:3, :] * u + w[1:2, :] * ubuf[7:7 + tm, :] + w[0:1, :] * ubuf[6:6 + tm, :]
    ubuf[0:8, :] = u[tm - 8:tm, :]
    hc = (c_b * conv).astype(BF16)
    y_conv = jnp.dot(hc, woc_ref[...], preferred_element_type=F32)

    o = 3 * CONV_WIDTH
    q_ref[0] = (proj(o, o + ATTN_WIDTH) * (HEAD_DIM ** -0.5)).astype(BF16)
    k_ref[0] = proj(o + ATTN_WIDTH, o + 2 * ATTN_WIDTH).astype(BF16)
    v_ref[0] = proj(o + 2 * ATTN_WIDTH, o + 3 * ATTN_WIDTH).astype(BF16)
    o += 3 * ATTN_WIDTH
    za_ref[0] = (_sigmoid(proj(o, o + D_MODEL)) * y_conv).astype(BF16)
    sgb_ref[0] = _sigmoid(proj(o + D_MODEL, o + 2 * D_MODEL)).astype(BF16)


def _proj_call(x, w_in, conv_w, w_out_conv):
    tok_spec = lambda c: pl.BlockSpec((1, TM_PROJ, c), lambda b, s: (b, s, 0))
    full = lambda shape: pl.BlockSpec(shape, lambda b, s: (0,) * len(shape))
    once = lambda shape: pl.BlockSpec(shape, lambda b, s: (0,) * len(shape), pipeline_mode=pl.Buffered(1))
    return pl.pallas_call(
        _proj_kernel,
        grid=(BATCH, SEQ // TM_PROJ),
        in_specs=[tok_spec(D_MODEL), pl.BlockSpec(memory_space=pl.ANY), full((3, CONV_WIDTH)),
                  once((CONV_WIDTH, D_MODEL))],
        out_specs=[tok_spec(ATTN_WIDTH), tok_spec(ATTN_WIDTH), tok_spec(ATTN_WIDTH),
                   tok_spec(D_MODEL), tok_spec(D_MODEL)],
        out_shape=[jax.ShapeDtypeStruct((BATCH, SEQ, ATTN_WIDTH), BF16)] * 3
        + [jax.ShapeDtypeStruct((BATCH, SEQ, D_MODEL), BF16)] * 2,
        scratch_shapes=[pltpu.VMEM((TM_PROJ + 8, CONV_WIDTH), F32), pltpu.VMEM((D_MODEL, IN_COLS), BF16),
                        pltpu.VMEM((CONV_WIDTH, D_MODEL), BF16),
                        pltpu.VMEM((2, D_MODEL, CONV_WIDTH), F32), pltpu.SemaphoreType.DMA((2,))],
        compiler_params=pltpu.CompilerParams(
            dimension_semantics=("arbitrary", "arbitrary"), vmem_limit_bytes=VMEM_LIMIT),
        name="proj",
    )(x, w_in, conv_w, w_out_conv)


def _attn_kernel(slopes_ref, qa_ref, qb_ref, k_ref, v_ref, oa_ref, ob_ref,
                 kaug_ref, vt_ref, kmean_ref, qaug_ref, pv_ref, mloc_ref, t_ref, p_ref):
    hq = pl.program_id(1)
    j = pl.program_id(2)
    blk = MOBA_BLOCK
    pairs = range(PAIRS)
    pair_w = 2 * HEAD_DIM
    lanes = [slice(pair_w * pp, pair_w * (pp + 1)) for pp in pairs]

    @pl.when(j == 0)
    def _():
        klane = lax.broadcasted_iota(I32, (blk, pair_w), 1)
        koff = lax.broadcasted_iota(I32, (blk, pair_w), 0).astype(F32)
        k_extra = jnp.where(klane == 0, koff, jnp.where(klane == 1, 1.0, 0.0)).astype(BF16)
        orow = lax.broadcasted_iota(I32, (PV_ROWS - HEAD_DIM, blk), 0)
        ones_rows = jnp.where(orow == 0, 1.0, 0.0).astype(BF16)
        for pp in pairs:
            for n in range(N_BLOCKS):
                kblk = k_ref[0, n * blk:(n + 1) * blk, lanes[pp]]
                kaug_ref[pp, n, :, 0:pair_w] = kblk
                kaug_ref[pp, n, :, pair_w:2 * pair_w] = k_extra
                kmean_ref[pp, n:n + 1, :] = jnp.mean(kblk.astype(F32), axis=0, keepdims=True)
                v_t = v_ref[0, n * blk:(n + 1) * blk, lanes[pp]].astype(F32).T.astype(BF16)
                for hh in range(2):
                    vt_ref[pp, n, hh, 0:HEAD_DIM, :] = v_t[hh * HEAD_DIM:(hh + 1) * HEAD_DIM, :]
                    vt_ref[pp, n, hh, HEAD_DIM:PV_ROWS, :] = ones_rows

    lane = lax.broadcasted_iota(I32, (1, 2 * blk), 1)
    qoff_row = jnp.where(lane < blk, lane, lane - blk).astype(F32)
    feat = lax.broadcasted_iota(I32, (2 * HEAD_DIM, blk), 0)
    arow = lax.broadcasted_iota(I32, (2 * HEAD_DIM, 2 * blk), 0)
    blk_i = lax.broadcasted_iota(I32, (N_BLOCKS, 2 * blk), 0)
    key_i = lax.broadcasted_iota(I32, (blk, 2 * blk), 0)
    qry_j = lax.broadcasted_iota(I32, (blk, 2 * blk), 1)
    causal = key_i <= jnp.where(qry_j < blk, qry_j, qry_j - blk)
    slope_rows, q_extras = [], []
    for pp in pairs:
        head = 2 * (PAIRS * hq + pp)
        slope_rows.append(jnp.where(lane < blk, slopes_ref[head], slopes_ref[head + 1]))
        q_extras.append(jnp.where(arow == 0, slope_rows[pp],
                                  jnp.where(arow == 1, -slope_rows[pp] * qoff_row, 0.0)).astype(BF16))

    def prepare(q_ref, pp, slot, qblock):
        q_t = q_ref[0, :, lanes[pp]].astype(F32).T
        qcat = jnp.concatenate([jnp.where(feat < HEAD_DIM, q_t, 0.0), jnp.where(feat >= HEAD_DIM, q_t, 0.0)],
                               axis=1).astype(BF16)
        qaug_ref[pp, slot, 0:2 * HEAD_DIM, :] = qcat
        qaug_ref[pp, slot, 2 * HEAD_DIM:4 * HEAD_DIM, :] = q_extras[pp]
        gate = jnp.dot(kmean_ref[pp].astype(BF16), qcat, preferred_element_type=F32)
        cnt = jnp.zeros((N_BLOCKS, 2 * blk), F32)
        for m in range(N_BLOCKS):
            gm = gate[m:m + 1, :]
            beats = (gm > gate) | ((gm == gate) & (blk_i > m))
            cnt = cnt + jnp.where(beats & (qblock > m), 1.0, 0.0)
        return jnp.where((blk_i < qblock) & (cnt < float(MOBA_TOPK)), 1.0, 0.0)

    qblock_a = j
    qblock_b = N_BLOCKS - 1 - j
    sel_a = [prepare(qa_ref, pp, 0, qblock_a) for pp in pairs]
    sel_b = [prepare(qb_ref, pp, 1, qblock_b) for pp in pairs]

    n_mid = N_BLOCKS - 1
    slots = [(0, 0, qblock_a, True)]
    mids = []
    for s in range(1, n_mid + 1):
        is_a = s <= j
        slots.append((s, jnp.where(is_a, 0, 1), jnp.where(is_a, s - 1, s - 1 - j), False))
        mids.append((is_a, slots[-1][2]))
    slots.append((n_mid + 1, 1, qblock_b, True))

    for pp in pairs:
        for s, which, kb, _ in slots:
            t_ref[pp, s] = jnp.dot(kaug_ref[pp, kb], qaug_ref[pp, which], preferred_element_type=F32)
    for pp in pairs:
        for s, _, _, own in slots:
            t = t_ref[pp, s]
            if own:
                t = jnp.where(causal, t, NEG_INF)
            m_loc = jnp.max(t, axis=0, keepdims=True)
            p_ref[pp, s] = jnp.exp((t - m_loc).astype(BF16))
            mloc_ref[pp, s:s + 1, :] = m_loc
    for pp in pairs:
        for s, _, kb, _ in slots:
            pv_ref[pp, s, 0] = jnp.dot(vt_ref[pp, kb, 0], p_ref[pp, s, :, 0:blk], preferred_element_type=F32)
            pv_ref[pp, s, 1] = jnp.dot(vt_ref[pp, kb, 1], p_ref[pp, s, :, blk:2 * blk],
                                       preferred_element_type=F32)

    def combine(o_ref, pp, own_slot, sel, qblock, mine):
        neg = jnp.full((1, 2 * blk), -1e30, F32)
        pieces = [(own_slot, mloc_ref[pp, own_slot:own_slot + 1, :])]
        for s, (is_a, kb) in enumerate(mids, start=1):
            selrow = jnp.sum(jnp.where(blk_i == kb, sel, 0.0), axis=0, keepdims=True)
            belongs = jnp.where(is_a, 1.0, 0.0) if mine else jnp.where(is_a, 0.0, 1.0)
            used = selrow * belongs > 0.5
            shift = slope_rows[pp] * ((kb - qblock) * blk).astype(F32)
            pieces.append((s, jnp.where(used, mloc_ref[pp, s:s + 1, :] + shift, neg)))
        m_all = pieces[0][1]
        for _, m_s in pieces[1:]:
            m_all = jnp.maximum(m_all, m_s)
        acc = [jnp.zeros((PV_ROWS, blk), F32), jnp.zeros((PV_ROWS, blk), F32)]
        for s, m_s in pieces:
            w = jnp.exp(m_s - m_all)
            for hh in range(2):
                acc[hh] = acc[hh] + pv_ref[pp, s, hh] * w[:, hh * blk:(hh + 1) * blk]
        o_t = jnp.concatenate([a[0:HEAD_DIM, :] / a[HEAD_DIM:HEAD_DIM + 1, :] for a in acc], axis=0)
        o_ref[0, :, lanes[pp]] = o_t.T.astype(BF16)

    for pp in pairs:
        combine(oa_ref, pp, 0, sel_a[pp], qblock_a, True)
        combine(ob_ref, pp, n_mid + 1, sel_b[pp], qblock_b, False)


def _attn_call(slopes, q, k, v):
    half = N_BLOCKS // 2
    assert 2 * HEAD_DIM == V7X_LANES
    width = 2 * HEAD_DIM * PAIRS
    o_a, o_b = pl.pallas_call(
        _attn_kernel,
        grid_spec=pltpu.PrefetchScalarGridSpec(
            num_scalar_prefetch=1,
            grid=(BATCH, N_HEADS // (2 * PAIRS), half),
            in_specs=[
                pl.BlockSpec((1, MOBA_BLOCK, width), lambda b, h, j, sl: (b, j, h)),
                pl.BlockSpec((1, MOBA_BLOCK, width), lambda b, h, j, sl: (b, N_BLOCKS - 1 - j, h)),
                pl.BlockSpec((1, SEQ, width), lambda b, h, j, sl: (b, 0, h)),
                pl.BlockSpec((1, SEQ, width), lambda b, h, j, sl: (b, 0, h)),
            ],
            out_specs=[pl.BlockSpec((1, MOBA_BLOCK, width), lambda b, h, j, sl: (b, j, h)),
                       pl.BlockSpec((1, MOBA_BLOCK, width), lambda b, h, j, sl: (b, half - 1 - j, h))],
            scratch_shapes=[
                pltpu.VMEM((PAIRS, N_BLOCKS, MOBA_BLOCK, 4 * HEAD_DIM), BF16),
                pltpu.VMEM((PAIRS, N_BLOCKS, 2, PV_ROWS, MOBA_BLOCK), BF16),
                pltpu.VMEM((PAIRS, N_BLOCKS, 2 * HEAD_DIM), F32),
                pltpu.VMEM((PAIRS, 2, 4 * HEAD_DIM, 2 * MOBA_BLOCK), BF16),
                pltpu.VMEM((PAIRS, N_BLOCKS + 1, 2, PV_ROWS, MOBA_BLOCK), F32),
                pltpu.VMEM((PAIRS, 16, 2 * MOBA_BLOCK), F32),
                pltpu.VMEM((PAIRS, N_BLOCKS + 1, MOBA_BLOCK, 2 * MOBA_BLOCK), F32),
                pltpu.VMEM((PAIRS, N_BLOCKS + 1, MOBA_BLOCK, 2 * MOBA_BLOCK), BF16),
            ],
        ),
        out_shape=[jax.ShapeDtypeStruct((BATCH, SEQ // 2, ATTN_WIDTH), BF16)] * 2,
        compiler_params=pltpu.CompilerParams(
            dimension_semantics=("arbitrary", "arbitrary", "arbitrary"), vmem_limit_bytes=VMEM_LIMIT),
        name="moba_attn",
    )(slopes, q, q, k, v)
    return o_a, o_b


def _route(logits):
    row8 = lax.broadcasted_iota(I32, (8, TM), 0).astype(F32)
    gl = jnp.where(row8 < float(N_GROUPS), logits[0:8, :], NEG_INF)
    gexp = jnp.exp(gl - jnp.max(gl, axis=0, keepdims=True))
    gprob = gexp / jnp.sum(gexp, axis=0, keepdims=True)
    ptop = jnp.max(gprob, axis=0, keepdims=True)
    gtop = jnp.min(jnp.where(gprob == ptop, row8, 8.0), axis=0, keepdims=True)
    el = logits[8:ROUTER_ROWS, :]
    eg = jnp.where(gtop == 0.0, el[0:8, :],
                   jnp.where(gtop == 1.0, el[8:16, :], jnp.where(gtop == 2.0, el[16:24, :], el[24:32, :])))
    m1 = jnp.max(eg, axis=0, keepdims=True)
    i1 = jnp.min(jnp.where(eg == m1, row8, 8.0), axis=0, keepdims=True)
    eg2 = jnp.where(row8 == i1, NEG_INF, eg)
    m2 = jnp.max(eg2, axis=0, keepdims=True)
    i2 = jnp.min(jnp.where(eg2 == m2, row8, 8.0), axis=0, keepdims=True)
    t2 = jnp.exp(m2 - m1)
    gate1 = ptop * (1.0 / (1.0 + t2))
    gate2 = ptop * (t2 / (1.0 + t2))
    erow = lax.broadcasted_iota(I32, (N_EXPERTS, TM), 0).astype(F32)
    oh1 = jnp.where(erow == gtop * float(EXPERTS_PER_GROUP) + i1, 1.0, 0.0)
    oh2 = jnp.where(erow == gtop * float(EXPERTS_PER_GROUP) + i2, 1.0, 0.0)
    return gate1, gate2, oh1, oh2


def _merge_kernel(oa_ref, ob_ref, za_ref, sgb_ref, x_ref, woa_f32_ref, wo_f32_ref, g_ref, b_ref,
                  wr_hi_ref, wr_lo_ref, x1_ref, xs_ref, rf_ref, mt_ref, woa_ref, wo_ref):
    i = pl.program_id(0)

    @pl.when(i == 0)
    def _():
        woa_ref[...] = woa_f32_ref[...].astype(BF16)
        wo_ref[...] = wo_f32_ref[...].astype(BF16)

    subs = range(MERGE_SUB)
    rows = [slice(s * TM, (s + 1) * TM) for s in subs]
    steps_per_batch = N_BLOCKS // MERGE_SUB
    in_oa = lax.rem(i, steps_per_batch) < steps_per_batch // 2
    o = [jnp.where(in_oa, oa_ref[0, r, :], ob_ref[0, r, :]) for r in rows]
    y_attn = [jnp.dot(o[s], woa_ref[...], preferred_element_type=F32) for s in subs]
    y = [(za_ref[rows[s], :].astype(F32) + sgb_ref[rows[s], :].astype(F32) * y_attn[s]).astype(BF16)
         for s in subs]
    mix = [jnp.dot(y[s], wo_ref[...], preferred_element_type=F32) for s in subs]
    x1 = []
    for s in subs:
        h = ALPHA * x_ref[rows[s], :] + mix[s]
        mu = jnp.mean(h, axis=-1, keepdims=True)
        hc = h - mu
        var = jnp.mean(hc * hc, axis=-1, keepdims=True)
        x1.append(hc * lax.rsqrt(var + LN_EPS) * g_ref[...] + b_ref[...])
        x1_ref[rows[s], :] = x1[s]

    xh = [x1[s].astype(BF16) for s in subs]
    xl = [(x1[s] - xh[s].astype(F32)).astype(BF16) for s in subs]
    wh = wr_hi_ref[...]
    logits = [(jnp.dot(xh[s], wh, preferred_element_type=F32)
               + jnp.dot(xl[s], wh, preferred_element_type=F32)
               + jnp.dot(xh[s], wr_lo_ref[...], preferred_element_type=F32)).T for s in subs]
    routes = [_route(logits[s]) for s in subs]

    ta = lax.broadcasted_iota(I32, (TM, TM), 0)
    tb = lax.broadcasted_iota(I32, (TM, TM), 1)
    upper = jnp.where(ta < tb, 1.0, 0.0).astype(BF16)
    ea = lax.broadcasted_iota(I32, (N_EXPERTS, N_EXPERTS), 0)
    eb = lax.broadcasted_iota(I32, (N_EXPERTS, N_EXPERTS), 1)
    lower = jnp.where(eb < ea, 1.0, 0.0).astype(BF16)
    lrow = lax.broadcasted_iota(I32, (LOCAL_ROWS, TM), 0).astype(F32)
    zero = jnp.zeros((1, TM), F32)
    cum = [jnp.dot((routes[s][2] + routes[s][3]).astype(BF16), upper, preferred_element_type=F32) for s in subs]
    perm = []
    for s in subs:
        gate1, gate2, oh1, oh2 = routes[s]
        n_e = jnp.sum(oh1 + oh2, axis=1, keepdims=True)
        m_rep = jnp.broadcast_to(jnp.floor((n_e + float(GRAN - 1)) * (1.0 / GRAN)), (N_EXPERTS, V7X_LANES))
        run_start = jnp.dot(lower, m_rep.astype(BF16), preferred_element_type=F32)
        tot = cum[s] + float(GRAN) * run_start[:, 0:1]
        lp1 = jnp.sum(oh1 * tot, axis=0, keepdims=True)
        lp2 = jnp.sum(oh2 * tot, axis=0, keepdims=True)
        perm.append(jnp.where((lrow == lp1) | (lrow == lp2), 1.0, 0.0).astype(BF16))
        rf_ref[:, rows[s]] = jnp.concatenate([gate1, gate2, lp1, lp2, zero, zero, zero, zero], axis=0)
        mt_ref[s * N_EXPERTS:(s + 1) * N_EXPERTS, :] = m_rep
    for s in subs:
        xs_ref[s * LOCAL_ROWS:(s + 1) * LOCAL_ROWS, :] = jnp.dot(
            perm[s], xh[s], preferred_element_type=F32).astype(BF16)


def _merge_call(o_a, o_b, za, sgb, x, woa, wo, g, b, wr_hi, wr_lo):
    tm = MERGE_SUB * TM
    tok = lambda c: pl.BlockSpec((tm, c), lambda i: (i, 0))
    full = lambda shape: pl.BlockSpec(shape, lambda i: (0,) * len(shape))
    per_batch = SEQ // tm
    half = per_batch // 2
    o_a_spec = pl.BlockSpec((1, tm, ATTN_WIDTH), lambda i: (i // per_batch, jnp.minimum(i % per_batch, half - 1), 0))
    o_b_spec = pl.BlockSpec((1, tm, ATTN_WIDTH), lambda i: (i // per_batch, jnp.maximum(i % per_batch - half, 0), 0))
    return pl.pallas_call(
        _merge_kernel,
        grid=(TOKENS // tm,),
        in_specs=[o_a_spec, o_b_spec, tok(D_MODEL), tok(D_MODEL), tok(D_MODEL),
                  full((ATTN_WIDTH, D_MODEL)), full((D_MODEL, D_MODEL)), full((1, D_MODEL)),
                  full((1, D_MODEL)), full((D_MODEL, V7X_LANES)), full((D_MODEL, V7X_LANES))],
        out_specs=[tok(D_MODEL), pl.BlockSpec((MERGE_SUB * LOCAL_ROWS, D_MODEL), lambda i: (i, 0)),
                   pl.BlockSpec((8, tm), lambda i: (0, i)),
                   pl.BlockSpec((MERGE_SUB * N_EXPERTS, V7X_LANES), lambda i: (i, 0))],
        out_shape=[jax.ShapeDtypeStruct((TOKENS, D_MODEL), F32),
                   jax.ShapeDtypeStruct((N_TOK_TILES * LOCAL_ROWS, D_MODEL), BF16),
                   jax.ShapeDtypeStruct((8, TOKENS), F32),
                   jax.ShapeDtypeStruct((N_TOK_TILES * N_EXPERTS, V7X_LANES), F32)],
        scratch_shapes=[pltpu.VMEM((ATTN_WIDTH, D_MODEL), BF16), pltpu.VMEM((D_MODEL, D_MODEL), BF16)],
        compiler_params=pltpu.CompilerParams(
            dimension_semantics=("arbitrary",), vmem_limit_bytes=VMEM_LIMIT),
        name="merge_ln1_route",
    )(o_a, o_b, za, sgb, x, woa, wo, g, b, wr_hi, wr_lo)


def _granule_copy(src_ref, src_gran, dst_ref, dst_gran, sem):
    src = pl.multiple_of(src_gran * GRAN, GRAN)
    dst = pl.multiple_of(dst_gran * GRAN, GRAN)
    return pltpu.make_async_copy(src_ref.at[pl.ds(src, GRAN), :], dst_ref.at[pl.ds(dst, GRAN), :], sem)


def _expert_kernel(te_ref, nt_ref, gsrc_ref, gdst_ref, ug_ref, wpar_ref, wnext_ref,
                   xs_ref, wg_hbm_ref, wu_hbm_ref, wd_hbm_ref,
                   ys_ref, xbuf, ybuf, zbuf, wg_ref, wu_ref, wd_ref, wg_stage, wu_stage, wd_stage,
                   in_sem, out_sem, zero_sem, w_sem):
    j = pl.program_id(0)
    n_tiles = nt_ref[0]
    slot = lax.rem(j, NBUF)
    prev_slot = lax.rem(j + NBUF - 1, NBUF)

    def tile_gather(step, s):
        for g in range(STEP_GRANS):
            _granule_copy(xs_ref, gsrc_ref[step * STEP_GRANS + g], xbuf.at[s], g,
                          in_sem.at[s]).start(priority=g % 2)

    def prev_scatter():
        for g in range(STEP_GRANS):
            _granule_copy(ybuf.at[prev_slot], g, ys_ref, gdst_ref[j * STEP_GRANS + g],
                          out_sem.at[prev_slot]).start(priority=g % 2)

    @pl.when(j == 0)
    def _():
        tile_gather(0, 0)
        ybuf[NBUF - 1] = jnp.zeros((STEP_ROWS, D_MODEL), BF16)
        zbuf[...] = jnp.zeros((GRAN, D_MODEL), BF16)
        for part in range(NBUF):
            spare = pltpu.make_async_copy(
                ybuf.at[NBUF - 1], ys_ref.at[pl.ds((SPARE_GRAN + part * STEP_GRANS) * GRAN, STEP_ROWS), :],
                out_sem.at[NBUF - 1])
            spare.start()
            spare.wait()

        for ahead in range(1, NBUF - 1):
            tile_gather(ahead, ahead)

    def zero_copy(t, g):
        return _granule_copy(zbuf, 0, ys_ref, t * LOCAL_GRANS + g, zero_sem)

    @pl.when(jnp.logical_and(j >= 1, j <= N_TOK_TILES))
    def _():
        def wait(g, c):
            zero_copy(j - 1, g).wait()
            return c

        lax.fori_loop(ug_ref[j - 1], LOCAL_GRANS, wait, 0)

    @pl.when(j < N_TOK_TILES)
    def _():
        def start(g, c):
            zero_copy(j, g).start()
            return c

        lax.fori_loop(ug_ref[j], LOCAL_GRANS, start, 0)

    def gather_wait():
        pltpu.make_async_copy(xs_ref.at[pl.ds(0, STEP_ROWS), :], xbuf.at[slot], in_sem.at[slot]).wait()

    @pl.when(jnp.logical_and(j >= NBUF - 1, j - NBUF < n_tiles))
    def _():
        pltpu.make_async_copy(ybuf.at[slot], ys_ref.at[pl.ds(0, STEP_ROWS), :], out_sem.at[slot]).wait()

    @pl.when(jnp.logical_and(j >= n_tiles, j < n_tiles + NBUF - 1))
    def _():
        gather_wait()

    @pl.when(j == n_tiles)
    def _():
        prev_scatter()

    def weight_copies(expert, s):
        return [pltpu.make_async_copy(hbm.at[expert], stage.at[s], w_sem.at[s])
                for hbm, stage in ((wg_hbm_ref, wg_stage), (wu_hbm_ref, wu_stage), (wd_hbm_ref, wd_stage))]

    @pl.when(j == 0)
    def _():
        for cp in weight_copies(te_ref[0], 0):
            cp.start()

    @pl.when(jnp.logical_and(j < n_tiles, jnp.logical_or(j == 0, te_ref[j] != te_ref[jnp.maximum(j - 1, 0)])))
    def _():
        s = wpar_ref[j]
        for cp in weight_copies(te_ref[j], s):
            cp.wait()
        wg_ref[...] = wg_stage[s].astype(BF16)
        wu_ref[...] = wu_stage[s].astype(BF16)
        wd_ref[...] = wd_stage[s].astype(BF16)
        nxt = wnext_ref[te_ref[j]]

        @pl.when(nxt >= 0)
        def _():
            for cp in weight_copies(nxt, 1 - s):
                cp.start()

    @pl.when(j < n_tiles)
    def _():
        gather_wait()
        chains = range(0, TE, CHAIN_ROWS)
        xb = [xbuf[slot, r:r + CHAIN_ROWS, :] for r in chains]
        hg = [jnp.dot(x, wg_ref[...], preferred_element_type=F32) for x in xb]
        hu = [jnp.dot(x, wu_ref[...], preferred_element_type=F32) for x in xb]
        tile_gather(j + NBUF - 1, prev_slot)
        prev_scatter()
        h = [(a * _sigmoid(a) * b).astype(BF16) for a, b in zip(hg, hu)]
        for hc, r in zip(h, chains):
            ybuf[slot, r:r + CHAIN_ROWS, :] = jnp.dot(hc, wd_ref[...], preferred_element_type=F32).astype(BF16)


def _expert_call(tile_expert, n_steps, gsrc, gdst, used_grans, w_parity, w_next, xs, wg, wu, wd):
    hbm = pl.BlockSpec(memory_space=pl.ANY)
    return pl.pallas_call(
        _expert_kernel,
        grid_spec=pltpu.PrefetchScalarGridSpec(
            num_scalar_prefetch=7,
            grid=(MAX_STEPS + NBUF,),
            in_specs=[hbm, hbm, hbm, hbm],
            out_specs=hbm,
            scratch_shapes=[pltpu.VMEM((NBUF, STEP_ROWS, D_MODEL), BF16),
                            pltpu.VMEM((NBUF, STEP_ROWS, D_MODEL), BF16),
                            pltpu.VMEM((GRAN, D_MODEL), BF16),
                            pltpu.VMEM((D_MODEL, D_EXPERT), BF16), pltpu.VMEM((D_MODEL, D_EXPERT), BF16),
                            pltpu.VMEM((D_EXPERT, D_MODEL), BF16),
                            pltpu.VMEM((2, D_MODEL, D_EXPERT), F32), pltpu.VMEM((2, D_MODEL, D_EXPERT), F32),
                            pltpu.VMEM((2, D_EXPERT, D_MODEL), F32),
                            pltpu.SemaphoreType.DMA((NBUF,)), pltpu.SemaphoreType.DMA((NBUF,)),
                            pltpu.SemaphoreType.DMA, pltpu.SemaphoreType.DMA((2,))],
        ),
        out_shape=jax.ShapeDtypeStruct(((SPARE_GRAN + NBUF * STEP_GRANS) * GRAN, D_MODEL), BF16),
        compiler_params=pltpu.CompilerParams(
            dimension_semantics=("arbitrary",), vmem_limit_bytes=VMEM_LIMIT),
        name="experts",
    )(tile_expert, n_steps, gsrc, gdst, used_grans, w_parity, w_next, xs, wg, wu, wd)


def _combine_kernel(ys_ref, x1_ref, rf_ref, g_ref, b_ref, out_ref):
    subs = range(COMBINE_SUB)
    col = lax.broadcasted_iota(I32, (TM, LOCAL_ROWS), 1).astype(F32)
    route = [rf_ref[:, s * TM:(s + 1) * TM].T for s in subs]
    unsort = [(jnp.where(col == r[:, 2:3], r[:, 0:1], 0.0)
               + jnp.where(col == r[:, 3:4], r[:, 1:2], 0.0)).astype(BF16) for r in route]
    ffn = [jnp.dot(unsort[s], ys_ref[s * LOCAL_ROWS:(s + 1) * LOCAL_ROWS, :], preferred_element_type=F32)
           for s in subs]
    for s in subs:
        h = ALPHA * x1_ref[s * TM:(s + 1) * TM, :] + ffn[s]
        mu = jnp.mean(h, axis=-1, keepdims=True)
        hc = h - mu
        var = jnp.mean(hc * hc, axis=-1, keepdims=True)
        out_ref[s * TM:(s + 1) * TM, :] = hc * lax.rsqrt(var + LN_EPS) * g_ref[...] + b_ref[...]


def _combine_call(ys, x1, rf, g, b):
    tm = COMBINE_SUB * TM
    return pl.pallas_call(
        _combine_kernel,
        grid=(TOKENS // tm,),
        in_specs=[pl.BlockSpec((COMBINE_SUB * LOCAL_ROWS, D_MODEL), lambda i: (i, 0)),
                  pl.BlockSpec((tm, D_MODEL), lambda i: (i, 0)),
                  pl.BlockSpec((8, tm), lambda i: (0, i)),
                  pl.BlockSpec((1, D_MODEL), lambda i: (0, 0)),
                  pl.BlockSpec((1, D_MODEL), lambda i: (0, 0))],
        out_specs=pl.BlockSpec((tm, D_MODEL), lambda i: (i, 0)),
        out_shape=jax.ShapeDtypeStruct((TOKENS, D_MODEL), F32),
        compiler_params=pltpu.CompilerParams(
            dimension_semantics=("arbitrary",), vmem_limit_bytes=VMEM_LIMIT),
        name="combine_ln2",
    )(ys, x1, rf, g, b)


def _router_cols(w_router_group, w_router_expert):
    w = jnp.concatenate([w_router_group, jnp.zeros((D_MODEL, 4), F32), w_router_expert,
                         jnp.zeros((D_MODEL, V7X_LANES - ROUTER_ROWS), F32)], axis=1)
    hi = w.astype(BF16)
    lo = (w - hi.astype(F32)).astype(BF16)
    return hi, lo


def _layer(x, w_in, conv_w, w_out_conv, w_out_attn, w_o, ln1_g, ln1_b,
           w_router_group, w_router_expert, w_gate, w_up, w_down, ln2_g, ln2_b):
    slopes = jnp.asarray([2.0 ** (-8.0 * (h + 1) / N_HEADS) for h in range(N_HEADS)], F32)
    q, k, v, za, sgb = _proj_call(x, w_in, conv_w, w_out_conv)
    o_a, o_b = _attn_call(slopes, q, k, v)

    wr_hi, wr_lo = _router_cols(w_router_group, w_router_expert)
    x1, xs, rf, mt = _merge_call(
        o_a, o_b, za.reshape(TOKENS, D_MODEL), sgb.reshape(TOKENS, D_MODEL),
        x.reshape(TOKENS, D_MODEL), w_out_attn, w_o,
        ln1_g.reshape(1, D_MODEL), ln1_b.reshape(1, D_MODEL), wr_hi, wr_lo)

    grans = mt.reshape(N_TOK_TILES, N_EXPERTS, V7X_LANES)[:, :, 0].astype(I32)
    local_start = jnp.cumsum(grans, axis=1) - grans
    grans_t = grans.T
    tiles_e = (jnp.sum(grans_t, axis=1) + TILE_GRANS - 1) // TILE_GRANS
    tile_end = jnp.cumsum(tiles_e)
    n_steps = tile_end[-1].reshape(1)
    all_tiles = MAX_STEPS + NBUF
    tile_ids = jnp.arange(all_tiles, dtype=I32)
    tile_expert = jnp.minimum(
        jnp.sum((tile_ids[:, None] >= tile_end[None, :]).astype(I32), axis=1), N_EXPERTS - 1)
    run_slot = TILE_GRANS * (tile_end - tiles_e)[:, None] + jnp.cumsum(grans_t, axis=1) - grans_t
    run_src = jnp.arange(N_TOK_TILES, dtype=I32)[None, :] * LOCAL_GRANS + local_start.T
    pick = (tile_expert[:all_tiles, None] == jnp.arange(N_EXPERTS, dtype=I32)[None, :])[:, :, None]
    t_slot = jnp.sum(jnp.where(pick, run_slot[None], 0), axis=1)
    t_len = jnp.sum(jnp.where(pick, grans_t[None], 0), axis=1)
    t_src = jnp.sum(jnp.where(pick, run_src[None], 0), axis=1)
    slots = jnp.arange(all_tiles * TILE_GRANS, dtype=I32).reshape(all_tiles, TILE_GRANS)
    k = slots[:, :, None] - t_slot[:, None, :]
    hit = (k >= 0) & (k < t_len[:, None, :])
    gran = jnp.sum(jnp.where(hit, t_src[:, None, :] + k, 0), axis=2).reshape(-1)
    filled = (jnp.sum(hit.astype(I32), axis=2) > 0).reshape(-1)
    slots = slots.reshape(-1)
    gsrc = jnp.where(filled, gran, 0)
    gdst = jnp.where(filled, gran, SPARE_GRAN + slots % (NBUF * STEP_GRANS))
    gdst = jnp.concatenate([SPARE_GRAN + (NBUF - 1) * STEP_GRANS + jnp.arange(STEP_GRANS, dtype=I32), gdst])

    starts = jnp.concatenate([jnp.ones((1,), I32), (tile_expert[1:] != tile_expert[:-1]).astype(I32)])
    w_parity = (jnp.cumsum(starts) - 1) % 2
    ids = jnp.arange(N_EXPERTS, dtype=I32)
    later = jnp.where((tiles_e > 0)[None, :] & (ids[None, :] > ids[:, None]), ids[None, :], N_EXPERTS)
    w_next = jnp.min(later, axis=1)
    w_next = jnp.where(w_next == N_EXPERTS, -1, w_next)

    ys = _expert_call(tile_expert, n_steps, gsrc, gdst, jnp.sum(grans, axis=1), w_parity, w_next, xs,
                      w_gate, w_up, w_down)
    out = _combine_call(ys, x1, rf, ln2_g.reshape(1, D_MODEL), ln2_b.reshape(1, D_MODEL))
    return out.reshape(BATCH, SEQ, D_MODEL)


def kernel(x, w_in, conv_w, w_out_conv, w_out_attn, w_o, ln1_g, ln1_b, w_router_group, w_router_expert, w_gate, w_up, w_down, ln2_g, ln2_b):
    depth = w_in.shape[0]
    for l in range(depth):
        x = _layer(x, w_in[l], conv_w[l], w_out_conv[l], w_out_attn[l], w_o[l], ln1_g[l], ln1_b[l],
                   w_router_group[l], w_router_expert[l], w_gate[l], w_up[l], w_down[l], ln2_g[l], ln2_b[l])
    return x
```

```python
import jax
import jax.numpy as jnp
from jax import lax
from jax.experimental import pallas as pl
from jax.experimental.pallas import tpu as pltpu

F32 = jnp.float32
BF16 = jnp.bfloat16
I32 = jnp.int32

V7X_VMEM_BYTES = 64 * 1024 * 1024
V7X_LANES = 128
V7X_MXU_DIM = 256

D_MODEL = 1024
BATCH = 8
SEQ = 2048
TOKENS = BATCH * SEQ
CONV_WIDTH = 512
N_HEADS = 8
HEAD_DIM = 64
ATTN_WIDTH = N_HEADS * HEAD_DIM
MOBA_BLOCK = 256
N_BLOCKS = SEQ // MOBA_BLOCK
MOBA_TOPK = 3
N_GROUPS = 4
EXPERTS_PER_GROUP = 8
N_EXPERTS = N_GROUPS * EXPERTS_PER_GROUP
D_EXPERT = 256
LN_EPS = 1e-5
ALPHA = 2.0 ** 0.25
IN_COLS = 3 * CONV_WIDTH + 3 * ATTN_WIDTH + 2 * D_MODEL

TM = 256
TM_PROJ = 512
TE = 512
CHAIN_ROWS = 256
PV_ROWS = HEAD_DIM + 16
GRAN = 16
TILE_GRANS = TE // GRAN
N_TOK_TILES = TOKENS // TM
LOCAL_ROWS = -(-(2 * TM + N_EXPERTS * (GRAN - 1)) // V7X_MXU_DIM) * V7X_MXU_DIM
LOCAL_GRANS = LOCAL_ROWS // GRAN
SPARE_GRAN = N_TOK_TILES * LOCAL_GRANS
MAX_TILES = (2 * TOKENS + N_TOK_TILES * N_EXPERTS * (GRAN - 1)) // TE + N_EXPERTS
PAIRS = 4
MERGE_SUB = 2
COMBINE_SUB = 4
STEP_ROWS = TE
STEP_GRANS = TILE_GRANS
MAX_STEPS = MAX_TILES
NBUF = 4
ROUTER_ROWS = 40
VMEM_LIMIT = V7X_VMEM_BYTES - 8 * 1024 * 1024
NEG_INF = float("-inf")


def _sigmoid(z):
    return 1.0 / (1.0 + jnp.exp(-z))


def _proj_kernel(x_ref, w_in_hbm_ref, convw_ref, woc_f32_ref, q_ref, k_ref, v_ref, za_ref, sgb_ref,
                 ubuf, w_in_ref, woc_ref, stage, stage_sem):
    s = pl.program_id(1)
    tm = TM_PROJ

    @pl.when((pl.program_id(0) == 0) & (s == 0))
    def _():
        def chunk(c):
            return pltpu.make_async_copy(w_in_hbm_ref.at[:, c * CONV_WIDTH:(c + 1) * CONV_WIDTH],
                                         stage.at[c % 2], stage_sem.at[c % 2])

        n_chunks = IN_COLS // CONV_WIDTH
        chunk(0).start()
        for c in range(n_chunks):
            if c + 1 < n_chunks:
                chunk(c + 1).start()
            chunk(c).wait()
            w_in_ref[:, c * CONV_WIDTH:(c + 1) * CONV_WIDTH] = stage[c % 2].astype(BF16)
        woc_ref[...] = woc_f32_ref[...].astype(BF16)

    xb = x_ref[0].astype(BF16)

    def proj(c0, c1):
        return jnp.dot(xb, w_in_ref[:, c0:c1], preferred_element_type=F32)

    c_b = proj(0, CONV_WIDTH)
    u = proj(CONV_WIDTH, 2 * CONV_WIDTH) * proj(2 * CONV_WIDTH, 3 * CONV_WIDTH)

    @pl.when(s == 0)
    def _():
        ubuf[0:8, :] = jnp.zeros((8, CONV_WIDTH), F32)

    ubuf[8:8 + tm, :] = u
    w = convw_ref[...]
    conv = w[2:3, :] * u + w[1:2, :] * ubuf[7:7 + tm, :] + w[0:1, :] * ubuf[6:6 + tm, :]
    ubuf[0:8, :] = u[tm - 8:tm, :]
    hc = (c_b * conv).astype(BF16)
    y_conv = jnp.dot(hc, woc_ref[...], preferred_element_type=F32)

    o = 3 * CONV_WIDTH
    q_ref[0] = (proj(o, o + ATTN_WIDTH) * (HEAD_DIM ** -0.5)).astype(BF16)
    k_ref[0] = proj(o + ATTN_WIDTH, o + 2 * ATTN_WIDTH).astype(BF16)
    v_ref[0] = proj(o + 2 * ATTN_WIDTH, o + 3 * ATTN_WIDTH).astype(BF16)
    o += 3 * ATTN_WIDTH
    za_ref[0] = (_sigmoid(proj(o, o + D_MODEL)) * y_conv).astype(BF16)
    sgb_ref[0] = _sigmoid(proj(o + D_MODEL, o + 2 * D_MODEL)).astype(BF16)


def _proj_call(x, w_in, conv_w, w_out_conv):
    tok_spec = lambda c: pl.BlockSpec((1, TM_PROJ, c), lambda b, s: (b, s, 0))
    full = lambda shape: pl.BlockSpec(shape, lambda b, s: (0,) * len(shape))
    once = lambda shape: pl.BlockSpec(shape, lambda b, s: (0,) * len(shape), pipeline_mode=pl.Buffered(1))
    return pl.pallas_call(
        _proj_kernel,
        grid=(BATCH, SEQ // TM_PROJ),
        in_specs=[tok_spec(D_MODEL), pl.BlockSpec(memory_space=pl.ANY), full((3, CONV_WIDTH)),
                  once((CONV_WIDTH, D_MODEL))],
        out_specs=[tok_spec(ATTN_WIDTH), tok_spec(ATTN_WIDTH), tok_spec(ATTN_WIDTH),
                   tok_spec(D_MODEL), tok_spec(D_MODEL)],
        out_shape=[jax.ShapeDtypeStruct((BATCH, SEQ, ATTN_WIDTH), BF16)] * 3
        + [jax.ShapeDtypeStruct((BATCH, SEQ, D_MODEL), BF16)] * 2,
        scratch_shapes=[pltpu.VMEM((TM_PROJ + 8, CONV_WIDTH), F32), pltpu.VMEM((D_MODEL, IN_COLS), BF16),
                        pltpu.VMEM((CONV_WIDTH, D_MODEL), BF16),
                        pltpu.VMEM((2, D_MODEL, CONV_WIDTH), F32), pltpu.SemaphoreType.DMA((2,))],
        compiler_params=pltpu.CompilerParams(
            dimension_semantics=("arbitrary", "arbitrary"), vmem_limit_bytes=VMEM_LIMIT),
        name="proj",
    )(x, w_in, conv_w, w_out_conv)


def _attn_kernel(slopes_ref, qa_ref, qb_ref, k_ref, v_ref, oa_ref, ob_ref,
                 kaug_ref, vt_ref, kmean_ref, qaug_ref, pv_ref, mloc_ref, t_ref, p_ref):
    hq = pl.program_id(1)
    j = pl.program_id(2)
    blk = MOBA_BLOCK
    pairs = range(PAIRS)
    pair_w = 2 * HEAD_DIM
    lanes = [slice(pair_w * pp, pair_w * (pp + 1)) for pp in pairs]

    @pl.when(j == 0)
    def _():
        klane = lax.broadcasted_iota(I32, (blk, pair_w), 1)
        koff = lax.broadcasted_iota(I32, (blk, pair_w), 0).astype(F32)
        k_extra = jnp.where(klane == 0, koff, jnp.where(klane == 1, 1.0, 0.0)).astype(BF16)
        orow = lax.broadcasted_iota(I32, (PV_ROWS - HEAD_DIM, blk), 0)
        ones_rows = jnp.where(orow == 0, 1.0, 0.0).astype(BF16)
        for pp in pairs:
            for n in range(N_BLOCKS):
                kblk = k_ref[0, n * blk:(n + 1) * blk, lanes[pp]]
                kaug_ref[pp, n, :, 0:pair_w] = kblk
                kaug_ref[pp, n, :, pair_w:2 * pair_w] = k_extra
                kmean_ref[pp, n:n + 1, :] = jnp.mean(kblk.astype(F32), axis=0, keepdims=True)
                v_t = v_ref[0, n * blk:(n + 1) * blk, lanes[pp]].astype(F32).T.astype(BF16)
                for hh in range(2):
                    vt_ref[pp, n, hh, 0:HEAD_DIM, :] = v_t[hh * HEAD_DIM:(hh + 1) * HEAD_DIM, :]
                    vt_ref[pp, n, hh, HEAD_DIM:PV_ROWS, :] = ones_rows

    lane = lax.broadcasted_iota(I32, (1, 2 * blk), 1)
    qoff_row = jnp.where(lane < blk, lane, lane - blk).astype(F32)
    feat = lax.broadcasted_iota(I32, (2 * HEAD_DIM, blk), 0)
    arow = lax.broadcasted_iota(I32, (2 * HEAD_DIM, 2 * blk), 0)
    blk_i = lax.broadcasted_iota(I32, (N_BLOCKS, 2 * blk), 0)
    key_i = lax.broadcasted_iota(I32, (blk, 2 * blk), 0)
    qry_j = lax.broadcasted_iota(I32, (blk, 2 * blk), 1)
    causal = key_i <= jnp.where(qry_j < blk, qry_j, qry_j - blk)
    slope_rows, q_extras = [], []
    for pp in pairs:
        head = 2 * (PAIRS * hq + pp)
        slope_rows.append(jnp.where(lane < blk, slopes_ref[head], slopes_ref[head + 1]))
        q_extras.append(jnp.where(arow == 0, slope_rows[pp],
                                  jnp.where(arow == 1, -slope_rows[pp] * qoff_row, 0.0)).astype(BF16))

    def prepare(q_ref, pp, slot, qblock):
        q_t = q_ref[0, :, lanes[pp]].astype(F32).T
        qcat = jnp.concatenate([jnp.where(feat < HEAD_DIM, q_t, 0.0), jnp.where(feat >= HEAD_DIM, q_t, 0.0)],
                               axis=1).astype(BF16)
        qaug_ref[pp, slot, 0:2 * HEAD_DIM, :] = qcat
        qaug_ref[pp, slot, 2 * HEAD_DIM:4 * HEAD_DIM, :] = q_extras[pp]
        gate = jnp.dot(kmean_ref[pp].astype(BF16), qcat, preferred_element_type=F32)
        cnt = jnp.zeros((N_BLOCKS, 2 * blk), F32)
        for m in range(N_BLOCKS):
            gm = gate[m:m + 1, :]
            beats = (gm > gate) | ((gm == gate) & (blk_i > m))
            cnt = cnt + jnp.where(beats & (qblock > m), 1.0, 0.0)
        return jnp.where((blk_i < qblock) & (cnt < float(MOBA_TOPK)), 1.0, 0.0)

    qblock_a = j
    qblock_b = N_BLOCKS - 1 - j
    sel_a = [prepare(qa_ref, pp, 0, qblock_a) for pp in pairs]
    sel_b = [prepare(qb_ref, pp, 1, qblock_b) for pp in pairs]

    n_mid = N_BLOCKS - 1
    slots = [(0, 0, qblock_a, True)]
    mids = []
    for s in range(1, n_mid + 1):
        is_a = s <= j
        slots.append((s, jnp.where(is_a, 0, 1), jnp.where(is_a, s - 1, s - 1 - j), False))
        mids.append((is_a, slots[-1][2]))
    slots.append((n_mid + 1, 1, qblock_b, True))

    for pp in pairs:
        for s, which, kb, _ in slots:
            t_ref[pp, s] = jnp.dot(kaug_ref[pp, kb], qaug_ref[pp, which], preferred_element_type=F32)
    for pp in pairs:
        for s, _, _, own in slots:
            t = t_ref[pp, s]
            if own:
                t = jnp.where(causal, t, NEG_INF)
            m_loc = jnp.max(t, axis=0, keepdims=True)
            p_ref[pp, s] = jnp.exp(t - m_loc).astype(BF16)
            mloc_ref[pp, s:s + 1, :] = m_loc
    for pp in pairs:
        for s, _, kb, _ in slots:
            pv_ref[pp, s, 0] = jnp.dot(vt_ref[pp, kb, 0], p_ref[pp, s, :, 0:blk], preferred_element_type=F32)
            pv_ref[pp, s, 1] = jnp.dot(vt_ref[pp, kb, 1], p_ref[pp, s, :, blk:2 * blk],
                                       preferred_element_type=F32)

    def combine(o_ref, pp, own_slot, sel, qblock, mine):
        neg = jnp.full((1, 2 * blk), -1e30, F32)
        pieces = [(own_slot, mloc_ref[pp, own_slot:own_slot + 1, :])]
        for s, (is_a, kb) in enumerate(mids, start=1):
            selrow = jnp.sum(jnp.where(blk_i == kb, sel, 0.0), axis=0, keepdims=True)
            belongs = jnp.where(is_a, 1.0, 0.0) if mine else jnp.where(is_a, 0.0, 1.0)
            used = selrow * belongs > 0.5
            shift = slope_rows[pp] * ((kb - qblock) * blk).astype(F32)
            pieces.append((s, jnp.where(used, mloc_ref[pp, s:s + 1, :] + shift, neg)))
        m_all = pieces[0][1]
        for _, m_s in pieces[1:]:
            m_all = jnp.maximum(m_all, m_s)
        acc = [jnp.zeros((PV_ROWS, blk), F32), jnp.zeros((PV_ROWS, blk), F32)]
        for s, m_s in pieces:
            w = jnp.exp(m_s - m_all)
            for hh in range(2):
                acc[hh] = acc[hh] + pv_ref[pp, s, hh] * w[:, hh * blk:(hh + 1) * blk]
        o_t = jnp.concatenate([a[0:HEAD_DIM, :] / a[HEAD_DIM:HEAD_DIM + 1, :] for a in acc], axis=0)
        o_ref[0, :, lanes[pp]] = o_t.T.astype(BF16)

    for pp in pairs:
        combine(oa_ref, pp, 0, sel_a[pp], qblock_a, True)
        combine(ob_ref, pp, n_mid + 1, sel_b[pp], qblock_b, False)


def _attn_call(slopes, q, k, v):
    half = N_BLOCKS // 2
    assert 2 * HEAD_DIM == V7X_LANES
    width = 2 * HEAD_DIM * PAIRS
    o_a, o_b = pl.pallas_call(
        _attn_kernel,
        grid_spec=pltpu.PrefetchScalarGridSpec(
            num_scalar_prefetch=1,
            grid=(BATCH, N_HEADS // (2 * PAIRS), half),
            in_specs=[
                pl.BlockSpec((1, MOBA_BLOCK, width), lambda b, h, j, sl: (b, j, h)),
                pl.BlockSpec((1, MOBA_BLOCK, width), lambda b, h, j, sl: (b, N_BLOCKS - 1 - j, h)),
                pl.BlockSpec((1, SEQ, width), lambda b, h, j, sl: (b, 0, h)),
                pl.BlockSpec((1, SEQ, width), lambda b, h, j, sl: (b, 0, h)),
            ],
            out_specs=[pl.BlockSpec((1, MOBA_BLOCK, width), lambda b, h, j, sl: (b, j, h)),
                       pl.BlockSpec((1, MOBA_BLOCK, width), lambda b, h, j, sl: (b, half - 1 - j, h))],
            scratch_shapes=[
                pltpu.VMEM((PAIRS, N_BLOCKS, MOBA_BLOCK, 4 * HEAD_DIM), BF16),
                pltpu.VMEM((PAIRS, N_BLOCKS, 2, PV_ROWS, MOBA_BLOCK), BF16),
                pltpu.VMEM((PAIRS, N_BLOCKS, 2 * HEAD_DIM), F32),
                pltpu.VMEM((PAIRS, 2, 4 * HEAD_DIM, 2 * MOBA_BLOCK), BF16),
                pltpu.VMEM((PAIRS, N_BLOCKS + 1, 2, PV_ROWS, MOBA_BLOCK), F32),
                pltpu.VMEM((PAIRS, 16, 2 * MOBA_BLOCK), F32),
                pltpu.VMEM((PAIRS, N_BLOCKS + 1, MOBA_BLOCK, 2 * MOBA_BLOCK), F32),
                pltpu.VMEM((PAIRS, N_BLOCKS + 1, MOBA_BLOCK, 2 * MOBA_BLOCK), BF16),
            ],
        ),
        out_shape=[jax.ShapeDtypeStruct((BATCH, SEQ // 2, ATTN_WIDTH), BF16)] * 2,
        compiler_params=pltpu.CompilerParams(
            dimension_semantics=("arbitrary", "arbitrary", "arbitrary"), vmem_limit_bytes=VMEM_LIMIT),
        name="moba_attn",
    )(slopes, q, q, k, v)
    return o_a, o_b


def _route(logits):
    row8 = lax.broadcasted_iota(I32, (8, TM), 0).astype(F32)
    gl = jnp.where(row8 < float(N_GROUPS), logits[0:8, :], NEG_INF)
    gexp = jnp.exp(gl - jnp.max(gl, axis=0, keepdims=True))
    gprob = gexp / jnp.sum(gexp, axis=0, keepdims=True)
    ptop = jnp.max(gprob, axis=0, keepdims=True)
    gtop = jnp.min(jnp.where(gprob == ptop, row8, 8.0), axis=0, keepdims=True)
    el = logits[8:ROUTER_ROWS, :]
    eg = jnp.where(gtop == 0.0, el[0:8, :],
                   jnp.where(gtop == 1.0, el[8:16, :], jnp.where(gtop == 2.0, el[16:24, :], el[24:32, :])))
    m1 = jnp.max(eg, axis=0, keepdims=True)
    i1 = jnp.min(jnp.where(eg == m1, row8, 8.0), axis=0, keepdims=True)
    eg2 = jnp.where(row8 == i1, NEG_INF, eg)
    m2 = jnp.max(eg2, axis=0, keepdims=True)
    i2 = jnp.min(jnp.where(eg2 == m2, row8, 8.0), axis=0, keepdims=True)
    t2 = jnp.exp(m2 - m1)
    gate1 = ptop * (1.0 / (1.0 + t2))
    gate2 = ptop * (t2 / (1.0 + t2))
    erow = lax.broadcasted_iota(I32, (N_EXPERTS, TM), 0).astype(F32)
    oh1 = jnp.where(erow == gtop * float(EXPERTS_PER_GROUP) + i1, 1.0, 0.0)
    oh2 = jnp.where(erow == gtop * float(EXPERTS_PER_GROUP) + i2, 1.0, 0.0)
    return gate1, gate2, oh1, oh2


def _merge_kernel(oa_ref, ob_ref, za_ref, sgb_ref, x_ref, woa_f32_ref, wo_f32_ref, g_ref, b_ref,
                  wr_hi_ref, wr_lo_ref, x1_ref, xs_ref, rf_ref, mt_ref, woa_ref, wo_ref):
    i = pl.program_id(0)

    @pl.when(i == 0)
    def _():
        woa_ref[...] = woa_f32_ref[...].astype(BF16)
        wo_ref[...] = wo_f32_ref[...].astype(BF16)

    subs = range(MERGE_SUB)
    rows = [slice(s * TM, (s + 1) * TM) for s in subs]
    steps_per_batch = N_BLOCKS // MERGE_SUB
    in_oa = lax.rem(i, steps_per_batch) < steps_per_batch // 2
    o = [jnp.where(in_oa, oa_ref[0, r, :], ob_ref[0, r, :]) for r in rows]
    y_attn = [jnp.dot(o[s], woa_ref[...], preferred_element_type=F32) for s in subs]
    y = [(za_ref[rows[s], :].astype(F32) + sgb_ref[rows[s], :].astype(F32) * y_attn[s]).astype(BF16)
         for s in subs]
    mix = [jnp.dot(y[s], wo_ref[...], preferred_element_type=F32) for s in subs]
    x1 = []
    for s in subs:
        h = ALPHA * x_ref[rows[s], :] + mix[s]
        mu = jnp.mean(h, axis=-1, keepdims=True)
        hc = h - mu
        var = jnp.mean(hc * hc, axis=-1, keepdims=True)
        x1.append(hc * lax.rsqrt(var + LN_EPS) * g_ref[...] + b_ref[...])
        x1_ref[rows[s], :] = x1[s]

    xh = [x1[s].astype(BF16) for s in subs]
    xl = [(x1[s] - xh[s].astype(F32)).astype(BF16) for s in subs]
    wh = wr_hi_ref[...]
    logits = [(jnp.dot(xh[s], wh, preferred_element_type=F32)
               + jnp.dot(xl[s], wh, preferred_element_type=F32)
               + jnp.dot(xh[s], wr_lo_ref[...], preferred_element_type=F32)).T for s in subs]
    routes = [_route(logits[s]) for s in subs]

    ta = lax.broadcasted_iota(I32, (TM, TM), 0)
    tb = lax.broadcasted_iota(I32, (TM, TM), 1)
    upper = jnp.where(ta < tb, 1.0, 0.0).astype(BF16)
    ea = lax.broadcasted_iota(I32, (N_EXPERTS, N_EXPERTS), 0)
    eb = lax.broadcasted_iota(I32, (N_EXPERTS, N_EXPERTS), 1)
    lower = jnp.where(eb < ea, 1.0, 0.0).astype(BF16)
    lrow = lax.broadcasted_iota(I32, (LOCAL_ROWS, TM), 0).astype(F32)
    zero = jnp.zeros((1, TM), F32)
    cum = [jnp.dot((routes[s][2] + routes[s][3]).astype(BF16), upper, preferred_element_type=F32) for s in subs]
    perm = []
    for s in subs:
        gate1, gate2, oh1, oh2 = routes[s]
        n_e = jnp.sum(oh1 + oh2, axis=1, keepdims=True)
        m_rep = jnp.broadcast_to(jnp.floor((n_e + float(GRAN - 1)) * (1.0 / GRAN)), (N_EXPERTS, V7X_LANES))
        run_start = jnp.dot(lower, m_rep.astype(BF16), preferred_element_type=F32)
        tot = cum[s] + float(GRAN) * run_start[:, 0:1]
        lp1 = jnp.sum(oh1 * tot, axis=0, keepdims=True)
        lp2 = jnp.sum(oh2 * tot, axis=0, keepdims=True)
        perm.append(jnp.where((lrow == lp1) | (lrow == lp2), 1.0, 0.0).astype(BF16))
        rf_ref[:, rows[s]] = jnp.concatenate([gate1, gate2, lp1, lp2, zero, zero, zero, zero], axis=0)
        mt_ref[s * N_EXPERTS:(s + 1) * N_EXPERTS, :] = m_rep
    for s in subs:
        xs_ref[s * LOCAL_ROWS:(s + 1) * LOCAL_ROWS, :] = jnp.dot(
            perm[s], xh[s], preferred_element_type=F32).astype(BF16)


def _merge_call(o_a, o_b, za, sgb, x, woa, wo, g, b, wr_hi, wr_lo):
    tm = MERGE_SUB * TM
    tok = lambda c: pl.BlockSpec((tm, c), lambda i: (i, 0))
    full = lambda shape: pl.BlockSpec(shape, lambda i: (0,) * len(shape))
    per_batch = SEQ // tm
    half = per_batch // 2
    o_a_spec = pl.BlockSpec((1, tm, ATTN_WIDTH), lambda i: (i // per_batch, jnp.minimum(i % per_batch, half - 1), 0))
    o_b_spec = pl.BlockSpec((1, tm, ATTN_WIDTH), lambda i: (i // per_batch, jnp.maximum(i % per_batch - half, 0), 0))
    return pl.pallas_call(
        _merge_kernel,
        grid=(TOKENS // tm,),
        in_specs=[o_a_spec, o_b_spec, tok(D_MODEL), tok(D_MODEL), tok(D_MODEL),
                  full((ATTN_WIDTH, D_MODEL)), full((D_MODEL, D_MODEL)), full((1, D_MODEL)),
                  full((1, D_MODEL)), full((D_MODEL, V7X_LANES)), full((D_MODEL, V7X_LANES))],
        out_specs=[tok(D_MODEL), pl.BlockSpec((MERGE_SUB * LOCAL_ROWS, D_MODEL), lambda i: (i, 0)),
                   pl.BlockSpec((8, tm), lambda i: (0, i)),
                   pl.BlockSpec((MERGE_SUB * N_EXPERTS, V7X_LANES), lambda i: (i, 0))],
        out_shape=[jax.ShapeDtypeStruct((TOKENS, D_MODEL), F32),
                   jax.ShapeDtypeStruct((N_TOK_TILES * LOCAL_ROWS, D_MODEL), BF16),
                   jax.ShapeDtypeStruct((8, TOKENS), F32),
                   jax.ShapeDtypeStruct((N_TOK_TILES * N_EXPERTS, V7X_LANES), F32)],
        scratch_shapes=[pltpu.VMEM((ATTN_WIDTH, D_MODEL), BF16), pltpu.VMEM((D_MODEL, D_MODEL), BF16)],
        compiler_params=pltpu.CompilerParams(
            dimension_semantics=("arbitrary",), vmem_limit_bytes=VMEM_LIMIT),
        name="merge_ln1_route",
    )(o_a, o_b, za, sgb, x, woa, wo, g, b, wr_hi, wr_lo)


def _granule_copy(src_ref, src_gran, dst_ref, dst_gran, sem):
    src = pl.multiple_of(src_gran * GRAN, GRAN)
    dst = pl.multiple_of(dst_gran * GRAN, GRAN)
    return pltpu.make_async_copy(src_ref.at[pl.ds(src, GRAN), :], dst_ref.at[pl.ds(dst, GRAN), :], sem)


def _expert_kernel(te_ref, nt_ref, gsrc_ref, gdst_ref, ug_ref, wpar_ref, wnext_ref,
                   xs_ref, wg_hbm_ref, wu_hbm_ref, wd_hbm_ref,
                   ys_ref, xbuf, ybuf, zbuf, wg_ref, wu_ref, wd_ref, wg_stage, wu_stage, wd_stage,
                   in_sem, out_sem, zero_sem, w_sem):
    j = pl.program_id(0)
    n_tiles = nt_ref[0]
    slot = lax.rem(j, NBUF)
    prev_slot = lax.rem(j + NBUF - 1, NBUF)

    def tile_gather(step, s):
        for g in range(STEP_GRANS):
            _granule_copy(xs_ref, gsrc_ref[step * STEP_GRANS + g], xbuf.at[s], g,
                          in_sem.at[s]).start(priority=g % 2)

    def prev_scatter():
        for g in range(STEP_GRANS):
            _granule_copy(ybuf.at[prev_slot], g, ys_ref, gdst_ref[j * STEP_GRANS + g],
                          out_sem.at[prev_slot]).start(priority=g % 2)

    @pl.when(j == 0)
    def _():
        tile_gather(0, 0)
        ybuf[NBUF - 1] = jnp.zeros((STEP_ROWS, D_MODEL), BF16)
        zbuf[...] = jnp.zeros((GRAN, D_MODEL), BF16)
        for part in range(NBUF):
            spare = pltpu.make_async_copy(
                ybuf.at[NBUF - 1], ys_ref.at[pl.ds((SPARE_GRAN + part * STEP_GRANS) * GRAN, STEP_ROWS), :],
                out_sem.at[NBUF - 1])
            spare.start()
            spare.wait()

        for ahead in range(1, NBUF - 1):
            tile_gather(ahead, ahead)

    def zero_copy(t, g):
        return _granule_copy(zbuf, 0, ys_ref, t * LOCAL_GRANS + g, zero_sem)

    @pl.when(jnp.logical_and(j >= 1, j <= N_TOK_TILES))
    def _():
        def wait(g, c):
            zero_copy(j - 1, g).wait()
            return c

        lax.fori_loop(ug_ref[j - 1], LOCAL_GRANS, wait, 0)

    @pl.when(j < N_TOK_TILES)
    def _():
        def start(g, c):
            zero_copy(j, g).start()
            return c

        lax.fori_loop(ug_ref[j], LOCAL_GRANS, start, 0)

    def gather_wait():
        pltpu.make_async_copy(xs_ref.at[pl.ds(0, STEP_ROWS), :], xbuf.at[slot], in_sem.at[slot]).wait()

    @pl.when(jnp.logical_and(j >= NBUF - 1, j - NBUF < n_tiles))
    def _():
        pltpu.make_async_copy(ybuf.at[slot], ys_ref.at[pl.ds(0, STEP_ROWS), :], out_sem.at[slot]).wait()

    @pl.when(jnp.logical_and(j >= n_tiles, j < n_tiles + NBUF - 1))
    def _():
        gather_wait()

    @pl.when(j == n_tiles)
    def _():
        prev_scatter()

    def weight_copies(expert, s):
        return [pltpu.make_async_copy(hbm.at[expert], stage.at[s], w_sem.at[s])
                for hbm, stage in ((wg_hbm_ref, wg_stage), (wu_hbm_ref, wu_stage), (wd_hbm_ref, wd_stage))]

    @pl.when(j == 0)
    def _():
        for cp in weight_copies(te_ref[0], 0):
            cp.start()

    @pl.when(jnp.logical_and(j < n_tiles, jnp.logical_or(j == 0, te_ref[j] != te_ref[jnp.maximum(j - 1, 0)])))
    def _():
        s = wpar_ref[j]
        for cp in weight_copies(te_ref[j], s):
            cp.wait()
        wg_ref[...] = wg_stage[s].astype(BF16)
        wu_ref[...] = wu_stage[s].astype(BF16)
        wd_ref[...] = wd_stage[s].astype(BF16)
        nxt = wnext_ref[te_ref[j]]

        @pl.when(nxt >= 0)
        def _():
            for cp in weight_copies(nxt, 1 - s):
                cp.start()

    @pl.when(j < n_tiles)
    def _():
        gather_wait()
        chains = range(0, TE, CHAIN_ROWS)
        xb = [xbuf[slot, r:r + CHAIN_ROWS, :] for r in chains]
        hg = [jnp.dot(x, wg_ref[...], preferred_element_type=F32) for x in xb]
        hu = [jnp.dot(x, wu_ref[...], preferred_element_type=F32) for x in xb]
        tile_gather(j + NBUF - 1, prev_slot)
        prev_scatter()
        h = [(a * _sigmoid(a) * b).astype(BF16) for a, b in zip(hg, hu)]
        for hc, r in zip(h, chains):
            ybuf[slot, r:r + CHAIN_ROWS, :] = jnp.dot(hc, wd_ref[...], preferred_element_type=F32).astype(BF16)


def _expert_call(tile_expert, n_steps, gsrc, gdst, used_grans, w_parity, w_next, xs, wg, wu, wd):
    hbm = pl.BlockSpec(memory_space=pl.ANY)
    return pl.pallas_call(
        _expert_kernel,
        grid_spec=pltpu.PrefetchScalarGridSpec(
            num_scalar_prefetch=7,
            grid=(MAX_STEPS + NBUF,),
            in_specs=[hbm, hbm, hbm, hbm],
            out_specs=hbm,
            scratch_shapes=[pltpu.VMEM((NBUF, STEP_ROWS, D_MODEL), BF16),
                            pltpu.VMEM((NBUF, STEP_ROWS, D_MODEL), BF16),
                            pltpu.VMEM((GRAN, D_MODEL), BF16),
                            pltpu.VMEM((D_MODEL, D_EXPERT), BF16), pltpu.VMEM((D_MODEL, D_EXPERT), BF16),
                            pltpu.VMEM((D_EXPERT, D_MODEL), BF16),
                            pltpu.VMEM((2, D_MODEL, D_EXPERT), F32), pltpu.VMEM((2, D_MODEL, D_EXPERT), F32),
                            pltpu.VMEM((2, D_EXPERT, D_MODEL), F32),
                            pltpu.SemaphoreType.DMA((NBUF,)), pltpu.SemaphoreType.DMA((NBUF,)),
                            pltpu.SemaphoreType.DMA, pltpu.SemaphoreType.DMA((2,))],
        ),
        out_shape=jax.ShapeDtypeStruct(((SPARE_GRAN + NBUF * STEP_GRANS) * GRAN, D_MODEL), BF16),
        compiler_params=pltpu.CompilerParams(
            dimension_semantics=("arbitrary",), vmem_limit_bytes=VMEM_LIMIT),
        name="experts",
    )(tile_expert, n_steps, gsrc, gdst, used_grans, w_parity, w_next, xs, wg, wu, wd)


def _combine_kernel(ys_ref, x1_ref, rf_ref, g_ref, b_ref, out_ref):
    subs = range(COMBINE_SUB)
    col = lax.broadcasted_iota(I32, (TM, LOCAL_ROWS), 1).astype(F32)
    route = [rf_ref[:, s * TM:(s + 1) * TM].T for s in subs]
    unsort = [(jnp.where(col == r[:, 2:3], r[:, 0:1], 0.0)
               + jnp.where(col == r[:, 3:4], r[:, 1:2], 0.0)).astype(BF16) for r in route]
    ffn = [jnp.dot(unsort[s], ys_ref[s * LOCAL_ROWS:(s + 1) * LOCAL_ROWS, :], preferred_element_type=F32)
           for s in subs]
    for s in subs:
        h = ALPHA * x1_ref[s * TM:(s + 1) * TM, :] + ffn[s]
        mu = jnp.mean(h, axis=-1, keepdims=True)
        hc = h - mu
        var = jnp.mean(hc * hc, axis=-1, keepdims=True)
        out_ref[s * TM:(s + 1) * TM, :] = hc * lax.rsqrt(var + LN_EPS) * g_ref[...] + b_ref[...]


def _combine_call(ys, x1, rf, g, b):
    tm = COMBINE_SUB * TM
    return pl.pallas_call(
        _combine_kernel,
        grid=(TOKENS // tm,),
        in_specs=[pl.BlockSpec((COMBINE_SUB * LOCAL_ROWS, D_MODEL), lambda i: (i, 0)),
                  pl.BlockSpec((tm, D_MODEL), lambda i: (i, 0)),
                  pl.BlockSpec((8, tm), lambda i: (0, i)),
                  pl.BlockSpec((1, D_MODEL), lambda i: (0, 0)),
                  pl.BlockSpec((1, D_MODEL), lambda i: (0, 0))],
        out_specs=pl.BlockSpec((tm, D_MODEL), lambda i: (i, 0)),
        out_shape=jax.ShapeDtypeStruct((TOKENS, D_MODEL), F32),
        compiler_params=pltpu.CompilerParams(
            dimension_semantics=("arbitrary",), vmem_limit_bytes=VMEM_LIMIT),
        name="combine_ln2",
    )(ys, x1, rf, g, b)


def _router_cols(w_router_group, w_router_expert):
    w = jnp.concatenate([w_router_group, jnp.zeros((D_MODEL, 4), F32), w_router_expert,
                         jnp.zeros((D_MODEL, V7X_LANES - ROUTER_ROWS), F32)], axis=1)
    hi = w.astype(BF16)
    lo = (w - hi.astype(F32)).astype(BF16)
    return hi, lo


def _layer(x, w_in, conv_w, w_out_conv, w_out_attn, w_o, ln1_g, ln1_b,
           w_router_group, w_router_expert, w_gate, w_up, w_down, ln2_g, ln2_b):
    slopes = jnp.asarray([2.0 ** (-8.0 * (h + 1) / N_HEADS) for h in range(N_HEADS)], F32)
    q, k, v, za, sgb = _proj_call(x, w_in, conv_w, w_out_conv)
    o_a, o_b = _attn_call(slopes, q, k, v)

    wr_hi, wr_lo = _router_cols(w_router_group, w_router_expert)
    x1, xs, rf, mt = _merge_call(
        o_a, o_b, za.reshape(TOKENS, D_MODEL), sgb.reshape(TOKENS, D_MODEL),
        x.reshape(TOKENS, D_MODEL), w_out_attn, w_o,
        ln1_g.reshape(1, D_MODEL), ln1_b.reshape(1, D_MODEL), wr_hi, wr_lo)

    grans = mt.reshape(N_TOK_TILES, N_EXPERTS, V7X_LANES)[:, :, 0].astype(I32)
    local_start = jnp.cumsum(grans, axis=1) - grans
    grans_t = grans.T
    tiles_e = (jnp.sum(grans_t, axis=1) + TILE_GRANS - 1) // TILE_GRANS
    tile_end = jnp.cumsum(tiles_e)
    n_steps = tile_end[-1].reshape(1)
    all_tiles = MAX_STEPS + NBUF
    tile_ids = jnp.arange(all_tiles, dtype=I32)
    tile_expert = jnp.minimum(
        jnp.sum((tile_ids[:, None] >= tile_end[None, :]).astype(I32), axis=1), N_EXPERTS - 1)
    run_slot = TILE_GRANS * (tile_end - tiles_e)[:, None] + jnp.cumsum(grans_t, axis=1) - grans_t
    run_src = jnp.arange(N_TOK_TILES, dtype=I32)[None, :] * LOCAL_GRANS + local_start.T
    pick = (tile_expert[:all_tiles, None] == jnp.arange(N_EXPERTS, dtype=I32)[None, :])[:, :, None]
    t_slot = jnp.sum(jnp.where(pick, run_slot[None], 0), axis=1)
    t_len = jnp.sum(jnp.where(pick, grans_t[None], 0), axis=1)
    t_src = jnp.sum(jnp.where(pick, run_src[None], 0), axis=1)
    slots = jnp.arange(all_tiles * TILE_GRANS, dtype=I32).reshape(all_tiles, TILE_GRANS)
    k = slots[:, :, None] - t_slot[:, None, :]
    hit = (k >= 0) & (k < t_len[:, None, :])
    gran = jnp.sum(jnp.where(hit, t_src[:, None, :] + k, 0), axis=2).reshape(-1)
    filled = (jnp.sum(hit.astype(I32), axis=2) > 0).reshape(-1)
    slots = slots.reshape(-1)
    gsrc = jnp.where(filled, gran, 0)
    gdst = jnp.where(filled, gran, SPARE_GRAN + slots % (NBUF * STEP_GRANS))
    gdst = jnp.concatenate([SPARE_GRAN + (NBUF - 1) * STEP_GRANS + jnp.arange(STEP_GRANS, dtype=I32), gdst])

    starts = jnp.concatenate([jnp.ones((1,), I32), (tile_expert[1:] != tile_expert[:-1]).astype(I32)])
    w_parity = (jnp.cumsum(starts) - 1) % 2
    ids = jnp.arange(N_EXPERTS, dtype=I32)
    later = jnp.where((tiles_e > 0)[None, :] & (ids[None, :] > ids[:, None]), ids[None, :], N_EXPERTS)
    w_next = jnp.min(later, axis=1)
    w_next = jnp.where(w_next == N_EXPERTS, -1, w_next)

    ys = _expert_call(tile_expert, n_steps, gsrc, gdst, jnp.sum(grans, axis=1), w_parity, w_next, xs,
                      w_gate, w_up, w_down)
    out = _combine_call(ys, x1, rf, ln2_g.reshape(1, D_MODEL), ln2_b.reshape(1, D_MODEL))
    return out.reshape(BATCH, SEQ, D_MODEL)


def kernel(x, w_in, conv_w, w_out_conv, w_out_attn, w_o, ln1_g, ln1_b, w_router_group, w_router_expert, w_gate, w_up, w_down, ln2_g, ln2_b):
    depth = w_in.shape[0]
    for l in range(depth):
        x = _layer(x, w_in[l], conv_w[l], w_out_conv[l], w_out_attn[l], w_o[l], ln1_g[l], ln1_b[l],
                   w_router_group[l], w_router_expert[l], w_gate[l], w_up[l], w_down[l], ln2_g[l], ln2_b[l])
    return x
```

```python
import jax
import jax.numpy as jnp
from jax import lax
from jax.experimental import pallas as pl
from jax.experimental.pallas import tpu as pltpu

F32 = jnp.float32
BF16 = jnp.bfloat16
I32 = jnp.int32

V7X_VMEM_BYTES = 64 * 1024 * 1024
V7X_LANES = 128
V7X_MXU_DIM = 256

D_MODEL = 1024
BATCH = 8
SEQ = 2048
TOKENS = BATCH * SEQ
CONV_WIDTH = 512
N_HEADS = 8
HEAD_DIM = 64
ATTN_WIDTH = N_HEADS * HEAD_DIM
MOBA_BLOCK = 256
N_BLOCKS = SEQ // MOBA_BLOCK
MOBA_TOPK = 3
N_GROUPS = 4
EXPERTS_PER_GROUP = 8
N_EXPERTS = N_GROUPS * EXPERTS_PER_GROUP
D_EXPERT = 256
LN_EPS = 1e-5
ALPHA = 2.0 ** 0.25
IN_COLS = 3 * CONV_WIDTH + 3 * ATTN_WIDTH + 2 * D_MODEL

TM = 256
TM_PROJ = 512
TE = 512
CHAIN_ROWS = 256
PV_ROWS = HEAD_DIM + 16
GRAN = 16
TILE_GRANS = TE // GRAN
N_TOK_TILES = TOKENS // TM
LOCAL_ROWS = -(-(2 * TM + N_EXPERTS * (GRAN - 1)) // V7X_MXU_DIM) * V7X_MXU_DIM
LOCAL_GRANS = LOCAL_ROWS // GRAN
SPARE_GRAN = N_TOK_TILES * LOCAL_GRANS
MAX_TILES = (2 * TOKENS + N_TOK_TILES * N_EXPERTS * (GRAN - 1)) // TE + N_EXPERTS
PAIRS = 4
MERGE_SUB = 2
COMBINE_SUB = 4
STEP_ROWS = TE
STEP_GRANS = TILE_GRANS
MAX_STEPS = MAX_TILES
NBUF = 4
ROUTER_ROWS = 40
VMEM_LIMIT = V7X_VMEM_BYTES - 8 * 1024 * 1024
NEG_INF = float("-inf")


def _sigmoid(z):
    return 1.0 / (1.0 + jnp.exp(-z))


def _proj_kernel(x_ref, w_in_hbm_ref, convw_ref, woc_f32_ref, q_ref, k_ref, v_ref, za_ref, sgb_ref,
                 ubuf, w_in_ref, woc_ref, stage, stage_sem):
    s = pl.program_id(1)
    tm = TM_PROJ

    @pl.when((pl.program_id(0) == 0) & (s == 0))
    def _():
        def chunk(c):
            return pltpu.make_async_copy(w_in_hbm_ref.at[:, c * CONV_WIDTH:(c + 1) * CONV_WIDTH],
                                         stage.at[c % 2], stage_sem.at[c % 2])

        n_chunks = IN_COLS // CONV_WIDTH
        chunk(0).start()
        for c in range(n_chunks):
            if c + 1 < n_chunks:
                chunk(c + 1).start()
            chunk(c).wait()
            w_in_ref[:, c * CONV_WIDTH:(c + 1) * CONV_WIDTH] = stage[c % 2].astype(BF16)
        woc_ref[...] = woc_f32_ref[...].astype(BF16)

    xb = x_ref[0].astype(BF16)

    def proj(c0, c1):
        return jnp.dot(xb, w_in_ref[:, c0:c1], preferred_element_type=F32)

    c_b = proj(0, CONV_WIDTH)
    u = proj(CONV_WIDTH, 2 * CONV_WIDTH) * proj(2 * CONV_WIDTH, 3 * CONV_WIDTH)

    @pl.when(s == 0)
    def _():
        ubuf[0:8, :] = jnp.zeros((8, CONV_WIDTH), F32)

    ubuf[8:8 + tm, :] = u
    w = convw_ref[...]
    conv = w[2:3, :] * u + w[1:2, :] * ubuf[7:7 + tm, :] + w[0:1, :] * ubuf[6:6 + tm, :]
    ubuf[0:8, :] = u[tm - 8:tm, :]
    hc = (c_b * conv).astype(BF16)
    y_conv = jnp.dot(hc, woc_ref[...], preferred_element_type=F32)

    o = 3 * CONV_WIDTH
    q_ref[0] = (proj(o, o + ATTN_WIDTH) * (HEAD_DIM ** -0.5)).astype(BF16)
    k_ref[0] = proj(o + ATTN_WIDTH, o + 2 * ATTN_WIDTH).astype(BF16)
    v_ref[0] = proj(o + 2 * ATTN_WIDTH, o + 3 * ATTN_WIDTH).astype(BF16)
    o += 3 * ATTN_WIDTH
    za_ref[0] = (_sigmoid(proj(o, o + D_MODEL)) * y_conv).astype(BF16)
    sgb_ref[0] = _sigmoid(proj(o + D_MODEL, o + 2 * D_MODEL)).astype(BF16)


def _proj_call(x, w_in, conv_w, w_out_conv):
    tok_spec = lambda c: pl.BlockSpec((1, TM_PROJ, c), lambda b, s: (b, s, 0))
    full = lambda shape: pl.BlockSpec(shape, lambda b, s: (0,) * len(shape))
    once = lambda shape: pl.BlockSpec(shape, lambda b, s: (0,) * len(shape), pipeline_mode=pl.Buffered(1))
    return pl.pallas_call(
        _proj_kernel,
        grid=(BATCH, SEQ // TM_PROJ),
        in_specs=[tok_spec(D_MODEL), pl.BlockSpec(memory_space=pl.ANY), full((3, CONV_WIDTH)),
                  once((CONV_WIDTH, D_MODEL))],
        out_specs=[tok_spec(ATTN_WIDTH), tok_spec(ATTN_WIDTH), tok_spec(ATTN_WIDTH),
                   tok_spec(D_MODEL), tok_spec(D_MODEL)],
        out_shape=[jax.ShapeDtypeStruct((BATCH, SEQ, ATTN_WIDTH), BF16)] * 3
        + [jax.ShapeDtypeStruct((BATCH, SEQ, D_MODEL), BF16)] * 2,
        scratch_shapes=[pltpu.VMEM((TM_PROJ + 8, CONV_WIDTH), F32), pltpu.VMEM((D_MODEL, IN_COLS), BF16),
                        pltpu.VMEM((CONV_WIDTH, D_MODEL), BF16),
                        pltpu.VMEM((2, D_MODEL, CONV_WIDTH), F32), pltpu.SemaphoreType.DMA((2,))],
        compiler_params=pltpu.CompilerParams(
            dimension_semantics=("arbitrary", "arbitrary"), vmem_limit_bytes=VMEM_LIMIT),
        name="proj",
    )(x, w_in, conv_w, w_out_conv)


def _attn_kernel(slopes_ref, qa_ref, qb_ref, k_ref, v_ref, wg_ref, wu_ref, wd_ref,
                 oa_ref, ob_ref, wg_bf_ref, wu_bf_ref, wd_bf_ref,
                 kaug_ref, vt_ref, kmean_ref, qaug_ref, pv_ref, mloc_ref, t_ref, p_ref):
    wg_bf_ref[...] = wg_ref[...].astype(BF16)
    wu_bf_ref[...] = wu_ref[...].astype(BF16)
    wd_bf_ref[...] = wd_ref[...].astype(BF16)

    hq = pl.program_id(1)
    j = pl.program_id(2)
    blk = MOBA_BLOCK
    pairs = range(PAIRS)
    pair_w = 2 * HEAD_DIM
    lanes = [slice(pair_w * pp, pair_w * (pp + 1)) for pp in pairs]

    @pl.when(j == 0)
    def _():
        klane = lax.broadcasted_iota(I32, (blk, pair_w), 1)
        koff = lax.broadcasted_iota(I32, (blk, pair_w), 0).astype(F32)
        k_extra = jnp.where(klane == 0, koff, jnp.where(klane == 1, 1.0, 0.0)).astype(BF16)
        orow = lax.broadcasted_iota(I32, (PV_ROWS - HEAD_DIM, blk), 0)
        ones_rows = jnp.where(orow == 0, 1.0, 0.0).astype(BF16)
        for pp in pairs:
            for n in range(N_BLOCKS):
                kblk = k_ref[0, n * blk:(n + 1) * blk, lanes[pp]]
                kaug_ref[pp, n, :, 0:pair_w] = kblk
                kaug_ref[pp, n, :, pair_w:2 * pair_w] = k_extra
                kmean_ref[pp, n:n + 1, :] = jnp.mean(kblk.astype(F32), axis=0, keepdims=True)
                v_t = v_ref[0, n * blk:(n + 1) * blk, lanes[pp]].astype(F32).T.astype(BF16)
                for hh in range(2):
                    vt_ref[pp, n, hh, 0:HEAD_DIM, :] = v_t[hh * HEAD_DIM:(hh + 1) * HEAD_DIM, :]
                    vt_ref[pp, n, hh, HEAD_DIM:PV_ROWS, :] = ones_rows

    lane = lax.broadcasted_iota(I32, (1, 2 * blk), 1)
    qoff_row = jnp.where(lane < blk, lane, lane - blk).astype(F32)
    feat = lax.broadcasted_iota(I32, (2 * HEAD_DIM, blk), 0)
    arow = lax.broadcasted_iota(I32, (2 * HEAD_DIM, 2 * blk), 0)
    blk_i = lax.broadcasted_iota(I32, (N_BLOCKS, 2 * blk), 0)
    key_i = lax.broadcasted_iota(I32, (blk, 2 * blk), 0)
    qry_j = lax.broadcasted_iota(I32, (blk, 2 * blk), 1)
    causal = key_i <= jnp.where(qry_j < blk, qry_j, qry_j - blk)
    slope_rows, q_extras = [], []
    for pp in pairs:
        head = 2 * (PAIRS * hq + pp)
        slope_rows.append(jnp.where(lane < blk, slopes_ref[head], slopes_ref[head + 1]))
        q_extras.append(jnp.where(arow == 0, slope_rows[pp],
                                  jnp.where(arow == 1, -slope_rows[pp] * qoff_row, 0.0)).astype(BF16))

    def prepare(q_ref, pp, slot, qblock):
        q_t = q_ref[0, :, lanes[pp]].astype(F32).T
        qcat = jnp.concatenate([jnp.where(feat < HEAD_DIM, q_t, 0.0), jnp.where(feat >= HEAD_DIM, q_t, 0.0)],
                               axis=1).astype(BF16)
        qaug_ref[pp, slot, 0:2 * HEAD_DIM, :] = qcat
        qaug_ref[pp, slot, 2 * HEAD_DIM:4 * HEAD_DIM, :] = q_extras[pp]
        gate = jnp.dot(kmean_ref[pp].astype(BF16), qcat, preferred_element_type=F32)
        cnt = jnp.zeros((N_BLOCKS, 2 * blk), F32)
        for m in range(N_BLOCKS):
            gm = gate[m:m + 1, :]
            beats = (gm > gate) | ((gm == gate) & (blk_i > m))
            cnt = cnt + jnp.where(beats & (qblock > m), 1.0, 0.0)
        return jnp.where((blk_i < qblock) & (cnt < float(MOBA_TOPK)), 1.0, 0.0)

    qblock_a = j
    qblock_b = N_BLOCKS - 1 - j
    sel_a = [prepare(qa_ref, pp, 0, qblock_a) for pp in pairs]
    sel_b = [prepare(qb_ref, pp, 1, qblock_b) for pp in pairs]

    n_mid = N_BLOCKS - 1
    slots = [(0, 0, qblock_a, True)]
    mids = []
    for s in range(1, n_mid + 1):
        is_a = s <= j
        slots.append((s, jnp.where(is_a, 0, 1), jnp.where(is_a, s - 1, s - 1 - j), False))
        mids.append((is_a, slots[-1][2]))
    slots.append((n_mid + 1, 1, qblock_b, True))

    for pp in pairs:
        for s, which, kb, _ in slots:
            t_ref[pp, s] = jnp.dot(kaug_ref[pp, kb], qaug_ref[pp, which], preferred_element_type=F32)
    for pp in pairs:
        for s, _, _, own in slots:
            t = t_ref[pp, s]
            if own:
                t = jnp.where(causal, t, NEG_INF)
            m_loc = jnp.max(t, axis=0, keepdims=True)
            p_ref[pp, s] = jnp.exp((t - m_loc).astype(BF16))
            mloc_ref[pp, s:s + 1, :] = m_loc
    for pp in pairs:
        for s, _, kb, _ in slots:
            pv_ref[pp, s, 0] = jnp.dot(vt_ref[pp, kb, 0], p_ref[pp, s, :, 0:blk], preferred_element_type=F32)
            pv_ref[pp, s, 1] = jnp.dot(vt_ref[pp, kb, 1], p_ref[pp, s, :, blk:2 * blk],
                                       preferred_element_type=F32)

    def combine(o_ref, pp, own_slot, sel, qblock, mine):
        neg = jnp.full((1, 2 * blk), -1e30, F32)
        pieces = [(own_slot, mloc_ref[pp, own_slot:own_slot + 1, :])]
        for s, (is_a, kb) in enumerate(mids, start=1):
            selrow = jnp.sum(jnp.where(blk_i == kb, sel, 0.0), axis=0, keepdims=True)
            belongs = jnp.where(is_a, 1.0, 0.0) if mine else jnp.where(is_a, 0.0, 1.0)
            used = selrow * belongs > 0.5
            shift = slope_rows[pp] * ((kb - qblock) * blk).astype(F32)
            pieces.append((s, jnp.where(used, mloc_ref[pp, s:s + 1, :] + shift, neg)))
        m_all = pieces[0][1]
        for _, m_s in pieces[1:]:
            m_all = jnp.maximum(m_all, m_s)
        acc = [jnp.zeros((PV_ROWS, blk), F32), jnp.zeros((PV_ROWS, blk), F32)]
        for s, m_s in pieces:
            w = jnp.exp(m_s - m_all)
            for hh in range(2):
                acc[hh] = acc[hh] + pv_ref[pp, s, hh] * w[:, hh * blk:(hh + 1) * blk]
        o_t = jnp.concatenate([a[0:HEAD_DIM, :] / a[HEAD_DIM:HEAD_DIM + 1, :] for a in acc], axis=0)
        o_ref[0, :, lanes[pp]] = o_t.T.astype(BF16)

    for pp in pairs:
        combine(oa_ref, pp, 0, sel_a[pp], qblock_a, True)
        combine(ob_ref, pp, n_mid + 1, sel_b[pp], qblock_b, False)


def _attn_call(slopes, q, k, v, w_gate, w_up, w_down):
    half = N_BLOCKS // 2
    assert 2 * HEAD_DIM == V7X_LANES
    width = 2 * HEAD_DIM * PAIRS
    steps = (BATCH, N_HEADS // (2 * PAIRS), half)
    assert steps[0] * steps[1] * steps[2] == N_EXPERTS
    expert = lambda b, h, j, sl: ((b * steps[1] + h) * steps[2] + j, 0, 0)
    w_specs = [pl.BlockSpec((1, D_MODEL, D_EXPERT), expert), pl.BlockSpec((1, D_MODEL, D_EXPERT), expert),
               pl.BlockSpec((1, D_EXPERT, D_MODEL), expert)]
    o_a, o_b, wg_bf, wu_bf, wd_bf = pl.pallas_call(
        _attn_kernel,
        grid_spec=pltpu.PrefetchScalarGridSpec(
            num_scalar_prefetch=1,
            grid=steps,
            in_specs=[
                pl.BlockSpec((1, MOBA_BLOCK, width), lambda b, h, j, sl: (b, j, h)),
                pl.BlockSpec((1, MOBA_BLOCK, width), lambda b, h, j, sl: (b, N_BLOCKS - 1 - j, h)),
                pl.BlockSpec((1, SEQ, width), lambda b, h, j, sl: (b, 0, h)),
                pl.BlockSpec((1, SEQ, width), lambda b, h, j, sl: (b, 0, h)),
            ] + w_specs,
            out_specs=[pl.BlockSpec((1, MOBA_BLOCK, width), lambda b, h, j, sl: (b, j, h)),
                       pl.BlockSpec((1, MOBA_BLOCK, width), lambda b, h, j, sl: (b, half - 1 - j, h))] + w_specs,
            scratch_shapes=[
                pltpu.VMEM((PAIRS, N_BLOCKS, MOBA_BLOCK, 4 * HEAD_DIM), BF16),
                pltpu.VMEM((PAIRS, N_BLOCKS, 2, PV_ROWS, MOBA_BLOCK), BF16),
                pltpu.VMEM((PAIRS, N_BLOCKS, 2 * HEAD_DIM), F32),
                pltpu.VMEM((PAIRS, 2, 4 * HEAD_DIM, 2 * MOBA_BLOCK), BF16),
                pltpu.VMEM((PAIRS, N_BLOCKS + 1, 2, PV_ROWS, MOBA_BLOCK), F32),
                pltpu.VMEM((PAIRS, 16, 2 * MOBA_BLOCK), F32),
                pltpu.VMEM((PAIRS, N_BLOCKS + 1, MOBA_BLOCK, 2 * MOBA_BLOCK), F32),
                pltpu.VMEM((PAIRS, N_BLOCKS + 1, MOBA_BLOCK, 2 * MOBA_BLOCK), BF16),
            ],
        ),
        out_shape=[jax.ShapeDtypeStruct((BATCH, SEQ // 2, ATTN_WIDTH), BF16)] * 2
        + [jax.ShapeDtypeStruct(w.shape, BF16) for w in (w_gate, w_up, w_down)],
        compiler_params=pltpu.CompilerParams(
            dimension_semantics=("arbitrary", "arbitrary", "arbitrary"), vmem_limit_bytes=VMEM_LIMIT),
        name="moba_attn",
    )(slopes, q, q, k, v, w_gate, w_up, w_down)
    return o_a, o_b, (wg_bf, wu_bf, wd_bf)


def _route(logits):
    row8 = lax.broadcasted_iota(I32, (8, TM), 0).astype(F32)
    gl = jnp.where(row8 < float(N_GROUPS), logits[0:8, :], NEG_INF)
    gexp = jnp.exp(gl - jnp.max(gl, axis=0, keepdims=True))
    gprob = gexp / jnp.sum(gexp, axis=0, keepdims=True)
    ptop = jnp.max(gprob, axis=0, keepdims=True)
    gtop = jnp.min(jnp.where(gprob == ptop, row8, 8.0), axis=0, keepdims=True)
    el = logits[8:ROUTER_ROWS, :]
    eg = jnp.where(gtop == 0.0, el[0:8, :],
                   jnp.where(gtop == 1.0, el[8:16, :], jnp.where(gtop == 2.0, el[16:24, :], el[24:32, :])))
    m1 = jnp.max(eg, axis=0, keepdims=True)
    i1 = jnp.min(jnp.where(eg == m1, row8, 8.0), axis=0, keepdims=True)
    eg2 = jnp.where(row8 == i1, NEG_INF, eg)
    m2 = jnp.max(eg2, axis=0, keepdims=True)
    i2 = jnp.min(jnp.where(eg2 == m2, row8, 8.0), axis=0, keepdims=True)
    t2 = jnp.exp(m2 - m1)
    gate1 = ptop * (1.0 / (1.0 + t2))
    gate2 = ptop * (t2 / (1.0 + t2))
    erow = lax.broadcasted_iota(I32, (N_EXPERTS, TM), 0).astype(F32)
    oh1 = jnp.where(erow == gtop * float(EXPERTS_PER_GROUP) + i1, 1.0, 0.0)
    oh2 = jnp.where(erow == gtop * float(EXPERTS_PER_GROUP) + i2, 1.0, 0.0)
    return gate1, gate2, oh1, oh2


def _merge_kernel(oa_ref, ob_ref, za_ref, sgb_ref, x_ref, woa_f32_ref, wo_f32_ref, g_ref, b_ref,
                  wr_hi_ref, wr_lo_ref, x1_ref, xs_ref, rf_ref, mt_ref, woa_ref, wo_ref):
    i = pl.program_id(0)

    @pl.when(i == 0)
    def _():
        woa_ref[...] = woa_f32_ref[...].astype(BF16)
        wo_ref[...] = wo_f32_ref[...].astype(BF16)

    subs = range(MERGE_SUB)
    rows = [slice(s * TM, (s + 1) * TM) for s in subs]
    steps_per_batch = N_BLOCKS // MERGE_SUB
    in_oa = lax.rem(i, steps_per_batch) < steps_per_batch // 2
    o = [jnp.where(in_oa, oa_ref[0, r, :], ob_ref[0, r, :]) for r in rows]
    y_attn = [jnp.dot(o[s], woa_ref[...], preferred_element_type=F32) for s in subs]
    y = [(za_ref[rows[s], :].astype(F32) + sgb_ref[rows[s], :].astype(F32) * y_attn[s]).astype(BF16)
         for s in subs]
    mix = [jnp.dot(y[s], wo_ref[...], preferred_element_type=F32) for s in subs]
    x1 = []
    for s in subs:
        h = ALPHA * x_ref[rows[s], :] + mix[s]
        mu = jnp.mean(h, axis=-1, keepdims=True)
        hc = h - mu
        var = jnp.mean(hc * hc, axis=-1, keepdims=True)
        x1.append(hc * lax.rsqrt(var + LN_EPS) * g_ref[...] + b_ref[...])
        x1_ref[rows[s], :] = x1[s]

    xh = [x1[s].astype(BF16) for s in subs]
    xl = [(x1[s] - xh[s].astype(F32)).astype(BF16) for s in subs]
    wh = wr_hi_ref[...]
    logits = [(jnp.dot(xh[s], wh, preferred_element_type=F32)
               + jnp.dot(xl[s], wh, preferred_element_type=F32)
               + jnp.dot(xh[s], wr_lo_ref[...], preferred_element_type=F32)).T for s in subs]
    routes = [_route(logits[s]) for s in subs]

    ta = lax.broadcasted_iota(I32, (TM, TM), 0)
    tb = lax.broadcasted_iota(I32, (TM, TM), 1)
    upper = jnp.where(ta < tb, 1.0, 0.0).astype(BF16)
    ea = lax.broadcasted_iota(I32, (N_EXPERTS, N_EXPERTS), 0)
    eb = lax.broadcasted_iota(I32, (N_EXPERTS, N_EXPERTS), 1)
    lower = jnp.where(eb < ea, 1.0, 0.0).astype(BF16)
    lrow = lax.broadcasted_iota(I32, (LOCAL_ROWS, TM), 0).astype(F32)
    zero = jnp.zeros((1, TM), F32)
    cum = [jnp.dot((routes[s][2] + routes[s][3]).astype(BF16), upper, preferred_element_type=F32) for s in subs]
    perm = []
    for s in subs:
        gate1, gate2, oh1, oh2 = routes[s]
        n_e = jnp.sum(oh1 + oh2, axis=1, keepdims=True)
        m_rep = jnp.broadcast_to(jnp.floor((n_e + float(GRAN - 1)) * (1.0 / GRAN)), (N_EXPERTS, V7X_LANES))
        run_start = jnp.dot(lower, m_rep.astype(BF16), preferred_element_type=F32)
        tot = cum[s] + float(GRAN) * run_start[:, 0:1]
        lp1 = jnp.sum(oh1 * tot, axis=0, keepdims=True)
        lp2 = jnp.sum(oh2 * tot, axis=0, keepdims=True)
        perm.append(jnp.where((lrow == lp1) | (lrow == lp2), 1.0, 0.0).astype(BF16))
        rf_ref[:, rows[s]] = jnp.concatenate([gate1, gate2, lp1, lp2, zero, zero, zero, zero], axis=0)
        mt_ref[s * N_EXPERTS:(s + 1) * N_EXPERTS, :] = m_rep
    for s in subs:
        xs_ref[s * LOCAL_ROWS:(s + 1) * LOCAL_ROWS, :] = jnp.dot(
            perm[s], xh[s], preferred_element_type=F32).astype(BF16)


def _merge_call(o_a, o_b, za, sgb, x, woa, wo, g, b, wr_hi, wr_lo):
    tm = MERGE_SUB * TM
    tok = lambda c: pl.BlockSpec((tm, c), lambda i: (i, 0))
    full = lambda shape: pl.BlockSpec(shape, lambda i: (0,) * len(shape))
    per_batch = SEQ // tm
    half = per_batch // 2
    o_a_spec = pl.BlockSpec((1, tm, ATTN_WIDTH), lambda i: (i // per_batch, jnp.minimum(i % per_batch, half - 1), 0))
    o_b_spec = pl.BlockSpec((1, tm, ATTN_WIDTH), lambda i: (i // per_batch, jnp.maximum(i % per_batch - half, 0), 0))
    return pl.pallas_call(
        _merge_kernel,
        grid=(TOKENS // tm,),
        in_specs=[o_a_spec, o_b_spec, tok(D_MODEL), tok(D_MODEL), tok(D_MODEL),
                  full((ATTN_WIDTH, D_MODEL)), full((D_MODEL, D_MODEL)), full((1, D_MODEL)),
                  full((1, D_MODEL)), full((D_MODEL, V7X_LANES)), full((D_MODEL, V7X_LANES))],
        out_specs=[tok(D_MODEL), pl.BlockSpec((MERGE_SUB * LOCAL_ROWS, D_MODEL), lambda i: (i, 0)),
                   pl.BlockSpec((8, tm), lambda i: (0, i)),
                   pl.BlockSpec((MERGE_SUB * N_EXPERTS, V7X_LANES), lambda i: (i, 0))],
        out_shape=[jax.ShapeDtypeStruct((TOKENS, D_MODEL), F32),
                   jax.ShapeDtypeStruct((N_TOK_TILES * LOCAL_ROWS, D_MODEL), BF16),
                   jax.ShapeDtypeStruct((8, TOKENS), F32),
                   jax.ShapeDtypeStruct((N_TOK_TILES * N_EXPERTS, V7X_LANES), F32)],
        scratch_shapes=[pltpu.VMEM((ATTN_WIDTH, D_MODEL), BF16), pltpu.VMEM((D_MODEL, D_MODEL), BF16)],
        compiler_params=pltpu.CompilerParams(
            dimension_semantics=("arbitrary",), vmem_limit_bytes=VMEM_LIMIT),
        name="merge_ln1_route",
    )(o_a, o_b, za, sgb, x, woa, wo, g, b, wr_hi, wr_lo)


def _granule_copy(src_ref, src_gran, dst_ref, dst_gran, sem):
    src = pl.multiple_of(src_gran * GRAN, GRAN)
    dst = pl.multiple_of(dst_gran * GRAN, GRAN)
    return pltpu.make_async_copy(src_ref.at[pl.ds(src, GRAN), :], dst_ref.at[pl.ds(dst, GRAN), :], sem)


def _expert_kernel(te_ref, nt_ref, gsrc_ref, gdst_ref, ug_ref, wpar_ref, wnext_ref,
                   xs_ref, wg_hbm_ref, wu_hbm_ref, wd_hbm_ref,
                   ys_ref, xbuf, ybuf, zbuf, wg_stage, wu_stage, wd_stage,
                   in_sem, out_sem, zero_sem, w_sem):
    j = pl.program_id(0)
    n_tiles = nt_ref[0]
    slot = lax.rem(j, NBUF)
    prev_slot = lax.rem(j + NBUF - 1, NBUF)

    def tile_gather(step, s):
        for g in range(STEP_GRANS):
            _granule_copy(xs_ref, gsrc_ref[step * STEP_GRANS + g], xbuf.at[s], g,
                          in_sem.at[s]).start(priority=g % 2)

    def prev_scatter():
        for g in range(STEP_GRANS):
            _granule_copy(ybuf.at[prev_slot], g, ys_ref, gdst_ref[j * STEP_GRANS + g],
                          out_sem.at[prev_slot]).start(priority=g % 2)

    @pl.when(j == 0)
    def _():
        tile_gather(0, 0)
        ybuf[NBUF - 1] = jnp.zeros((STEP_ROWS, D_MODEL), BF16)
        zbuf[...] = jnp.zeros((GRAN, D_MODEL), BF16)
        for part in range(NBUF):
            spare = pltpu.make_async_copy(
                ybuf.at[NBUF - 1], ys_ref.at[pl.ds((SPARE_GRAN + part * STEP_GRANS) * GRAN, STEP_ROWS), :],
                out_sem.at[NBUF - 1])
            spare.start()
            spare.wait()

        for ahead in range(1, NBUF - 1):
            tile_gather(ahead, ahead)

    def zero_copy(t, g):
        return _granule_copy(zbuf, 0, ys_ref, t * LOCAL_GRANS + g, zero_sem)

    @pl.when(jnp.logical_and(j >= 1, j <= N_TOK_TILES))
    def _():
        def wait(g, c):
            zero_copy(j - 1, g).wait()
            return c

        lax.fori_loop(ug_ref[j - 1], LOCAL_GRANS, wait, 0)

    @pl.when(j < N_TOK_TILES)
    def _():
        def start(g, c):
            zero_copy(j, g).start()
            return c

        lax.fori_loop(ug_ref[j], LOCAL_GRANS, start, 0)

    def gather_wait():
        pltpu.make_async_copy(xs_ref.at[pl.ds(0, STEP_ROWS), :], xbuf.at[slot], in_sem.at[slot]).wait()

    @pl.when(jnp.logical_and(j >= NBUF - 1, j - NBUF < n_tiles))
    def _():
        pltpu.make_async_copy(ybuf.at[slot], ys_ref.at[pl.ds(0, STEP_ROWS), :], out_sem.at[slot]).wait()

    @pl.when(jnp.logical_and(j >= n_tiles, j < n_tiles + NBUF - 1))
    def _():
        gather_wait()

    @pl.when(j == n_tiles)
    def _():
        prev_scatter()

    def weight_copies(expert, s):
        return [pltpu.make_async_copy(hbm.at[expert], stage.at[s], w_sem.at[s])
                for hbm, stage in ((wg_hbm_ref, wg_stage), (wu_hbm_ref, wu_stage), (wd_hbm_ref, wd_stage))]

    @pl.when(j == 0)
    def _():
        for cp in weight_copies(te_ref[0], 0):
            cp.start()

    @pl.when(jnp.logical_and(j < n_tiles, jnp.logical_or(j == 0, te_ref[j] != te_ref[jnp.maximum(j - 1, 0)])))
    def _():
        s = wpar_ref[j]
        for cp in weight_copies(te_ref[j], s):
            cp.wait()
        nxt = wnext_ref[te_ref[j]]

        @pl.when(nxt >= 0)
        def _():
            for cp in weight_copies(nxt, 1 - s):
                cp.start()

    @pl.when(j < n_tiles)
    def _():
        gather_wait()
        chains = range(0, TE, CHAIN_ROWS)
        xb = [xbuf[slot, r:r + CHAIN_ROWS, :] for r in chains]
        ws = wpar_ref[j]
        hg = [jnp.dot(x, wg_stage[ws], preferred_element_type=F32) for x in xb]
        hu = [jnp.dot(x, wu_stage[ws], preferred_element_type=F32) for x in xb]
        tile_gather(j + NBUF - 1, prev_slot)
        prev_scatter()
        h = [(a * _sigmoid(a) * b).astype(BF16) for a, b in zip(hg, hu)]
        for hc, r in zip(h, chains):
            ybuf[slot, r:r + CHAIN_ROWS, :] = jnp.dot(hc, wd_stage[ws], preferred_element_type=F32).astype(BF16)


def _expert_call(tile_expert, n_steps, gsrc, gdst, used_grans, w_parity, w_next, xs, wg, wu, wd):
    hbm = pl.BlockSpec(memory_space=pl.ANY)
    return pl.pallas_call(
        _expert_kernel,
        grid_spec=pltpu.PrefetchScalarGridSpec(
            num_scalar_prefetch=7,
            grid=(MAX_STEPS + NBUF,),
            in_specs=[hbm, hbm, hbm, hbm],
            out_specs=hbm,
            scratch_shapes=[pltpu.VMEM((NBUF, STEP_ROWS, D_MODEL), BF16),
                            pltpu.VMEM((NBUF, STEP_ROWS, D_MODEL), BF16),
                            pltpu.VMEM((GRAN, D_MODEL), BF16),
                            pltpu.VMEM((2, D_MODEL, D_EXPERT), BF16), pltpu.VMEM((2, D_MODEL, D_EXPERT), BF16),
                            pltpu.VMEM((2, D_EXPERT, D_MODEL), BF16),
                            pltpu.SemaphoreType.DMA((NBUF,)), pltpu.SemaphoreType.DMA((NBUF,)),
                            pltpu.SemaphoreType.DMA, pltpu.SemaphoreType.DMA((2,))],
        ),
        out_shape=jax.ShapeDtypeStruct(((SPARE_GRAN + NBUF * STEP_GRANS) * GRAN, D_MODEL), BF16),
        compiler_params=pltpu.CompilerParams(
            dimension_semantics=("arbitrary",), vmem_limit_bytes=VMEM_LIMIT),
        name="experts",
    )(tile_expert, n_steps, gsrc, gdst, used_grans, w_parity, w_next, xs, wg, wu, wd)


def _combine_kernel(ys_ref, x1_ref, rf_ref, g_ref, b_ref, out_ref):
    subs = range(COMBINE_SUB)
    col = lax.broadcasted_iota(I32, (TM, LOCAL_ROWS), 1).astype(F32)
    route = [rf_ref[:, s * TM:(s + 1) * TM].T for s in subs]
    unsort = [(jnp.where(col == r[:, 2:3], r[:, 0:1], 0.0)
               + jnp.where(col == r[:, 3:4], r[:, 1:2], 0.0)).astype(BF16) for r in route]
    ffn = [jnp.dot(unsort[s], ys_ref[s * LOCAL_ROWS:(s + 1) * LOCAL_ROWS, :], preferred_element_type=F32)
           for s in subs]
    for s in subs:
        h = ALPHA * x1_ref[s * TM:(s + 1) * TM, :] + ffn[s]
        mu = jnp.mean(h, axis=-1, keepdims=True)
        hc = h - mu
        var = jnp.mean(hc * hc, axis=-1, keepdims=True)
        out_ref[s * TM:(s + 1) * TM, :] = hc * lax.rsqrt(var + LN_EPS) * g_ref[...] + b_ref[...]


def _combine_call(ys, x1, rf, g, b):
    tm = COMBINE_SUB * TM
    return pl.pallas_call(
        _combine_kernel,
        grid=(TOKENS // tm,),
        in_specs=[pl.BlockSpec((COMBINE_SUB * LOCAL_ROWS, D_MODEL), lambda i: (i, 0)),
                  pl.BlockSpec((tm, D_MODEL), lambda i: (i, 0)),
                  pl.BlockSpec((8, tm), lambda i: (0, i)),
                  pl.BlockSpec((1, D_MODEL), lambda i: (0, 0)),
                  pl.BlockSpec((1, D_MODEL), lambda i: (0, 0))],
        out_specs=pl.BlockSpec((tm, D_MODEL), lambda i: (i, 0)),
        out_shape=jax.ShapeDtypeStruct((TOKENS, D_MODEL), F32),
        compiler_params=pltpu.CompilerParams(
            dimension_semantics=("arbitrary",), vmem_limit_bytes=VMEM_LIMIT),
        name="combine_ln2",
    )(ys, x1, rf, g, b)


def _router_cols(w_router_group, w_router_expert):
    w = jnp.concatenate([w_router_group, jnp.zeros((D_MODEL, 4), F32), w_router_expert,
                         jnp.zeros((D_MODEL, V7X_LANES - ROUTER_ROWS), F32)], axis=1)
    hi = w.astype(BF16)
    lo = (w - hi.astype(F32)).astype(BF16)
    return hi, lo


def _layer(x, w_in, conv_w, w_out_conv, w_out_attn, w_o, ln1_g, ln1_b,
           w_router_group, w_router_expert, w_gate, w_up, w_down, ln2_g, ln2_b):
    slopes = jnp.asarray([2.0 ** (-8.0 * (h + 1) / N_HEADS) for h in range(N_HEADS)], F32)
    q, k, v, za, sgb = _proj_call(x, w_in, conv_w, w_out_conv)
    o_a, o_b, expert_weights = _attn_call(slopes, q, k, v, w_gate, w_up, w_down)

    wr_hi, wr_lo = _router_cols(w_router_group, w_router_expert)
    x1, xs, rf, mt = _merge_call(
        o_a, o_b, za.reshape(TOKENS, D_MODEL), sgb.reshape(TOKENS, D_MODEL),
        x.reshape(TOKENS, D_MODEL), w_out_attn, w_o,
        ln1_g.reshape(1, D_MODEL), ln1_b.reshape(1, D_MODEL), wr_hi, wr_lo)

    grans = mt.reshape(N_TOK_TILES, N_EXPERTS, V7X_LANES)[:, :, 0].astype(I32)
    local_start = jnp.cumsum(grans, axis=1) - grans
    grans_t = grans.T
    tiles_e = (jnp.sum(grans_t, axis=1) + TILE_GRANS - 1) // TILE_GRANS
    tile_end = jnp.cumsum(tiles_e)
    n_steps = tile_end[-1].reshape(1)
    all_tiles = MAX_STEPS + NBUF
    tile_ids = jnp.arange(all_tiles, dtype=I32)
    tile_expert = jnp.minimum(
        jnp.sum((tile_ids[:, None] >= tile_end[None, :]).astype(I32), axis=1), N_EXPERTS - 1)
    run_slot = TILE_GRANS * (tile_end - tiles_e)[:, None] + jnp.cumsum(grans_t, axis=1) - grans_t
    run_src = jnp.arange(N_TOK_TILES, dtype=I32)[None, :] * LOCAL_GRANS + local_start.T
    pick = (tile_expert[:all_tiles, None] == jnp.arange(N_EXPERTS, dtype=I32)[None, :])[:, :, None]
    t_slot = jnp.sum(jnp.where(pick, run_slot[None], 0), axis=1)
    t_len = jnp.sum(jnp.where(pick, grans_t[None], 0), axis=1)
    t_src = jnp.sum(jnp.where(pick, run_src[None], 0), axis=1)
    slots = jnp.arange(all_tiles * TILE_GRANS, dtype=I32).reshape(all_tiles, TILE_GRANS)
    k = slots[:, :, None] - t_slot[:, None, :]
    hit = (k >= 0) & (k < t_len[:, None, :])
    gran = jnp.sum(jnp.where(hit, t_src[:, None, :] + k, 0), axis=2).reshape(-1)
    filled = (jnp.sum(hit.astype(I32), axis=2) > 0).reshape(-1)
    slots = slots.reshape(-1)
    gsrc = jnp.where(filled, gran, 0)
    gdst = jnp.where(filled, gran, SPARE_GRAN + slots % (NBUF * STEP_GRANS))
    gdst = jnp.concatenate([SPARE_GRAN + (NBUF - 1) * STEP_GRANS + jnp.arange(STEP_GRANS, dtype=I32), gdst])

    starts = jnp.concatenate([jnp.ones((1,), I32), (tile_expert[1:] != tile_expert[:-1]).astype(I32)])
    w_parity = (jnp.cumsum(starts) - 1) % 2
    ids = jnp.arange(N_EXPERTS, dtype=I32)
    later = jnp.where((tiles_e > 0)[None, :] & (ids[None, :] > ids[:, None]), ids[None, :], N_EXPERTS)
    w_next = jnp.min(later, axis=1)
    w_next = jnp.where(w_next == N_EXPERTS, -1, w_next)

    ys = _expert_call(tile_expert, n_steps, gsrc, gdst, jnp.sum(grans, axis=1), w_parity, w_next, xs,
                      *expert_weights)
    out = _combine_call(ys, x1, rf, ln2_g.reshape(1, D_MODEL), ln2_b.reshape(1, D_MODEL))
    return out.reshape(BATCH, SEQ, D_MODEL)


def kernel(x, w_in, conv_w, w_out_conv, w_out_attn, w_o, ln1_g, ln1_b, w_router_group, w_router_expert, w_gate, w_up, w_down, ln2_g, ln2_b):
    depth = w_in.shape[0]
    for l in range(depth):
        x = _layer(x, w_in[l], conv_w[l], w_out_conv[l], w_out_attn[l], w_o[l], ln1_g[l], ln1_b[l],
                   w_router_group[l], w_router_expert[l], w_gate[l], w_up[l], w_down[l], ln2_g[l], ln2_b[l])
    return x
```

```python
import jax
import jax.numpy as jnp
from jax import lax
from jax.experimental import pallas as pl
from jax.experimental.pallas import tpu as pltpu

F32 = jnp.float32
BF16 = jnp.bfloat16
I32 = jnp.int32

V7X_VMEM_BYTES = 64 * 1024 * 1024
V7X_LANES = 128
V7X_MXU_DIM = 256

D_MODEL = 1024
BATCH = 8
SEQ = 2048
TOKENS = BATCH * SEQ
CONV_WIDTH = 512
N_HEADS = 8
HEAD_DIM = 64
ATTN_WIDTH = N_HEADS * HEAD_DIM
MOBA_BLOCK = 256
N_BLOCKS = SEQ // MOBA_BLOCK
MOBA_TOPK = 3
N_GROUPS = 4
EXPERTS_PER_GROUP = 8
N_EXPERTS = N_GROUPS * EXPERTS_PER_GROUP
D_EXPERT = 256
LN_EPS = 1e-5
ALPHA = 2.0 ** 0.25
IN_COLS = 3 * CONV_WIDTH + 3 * ATTN_WIDTH + 2 * D_MODEL

TM = 256
TM_PROJ = 1024
TE = 512
CHAIN_ROWS = 256
PV_ROWS = HEAD_DIM + 16
GRAN = 16
TILE_GRANS = TE // GRAN
N_TOK_TILES = TOKENS // TM
LOCAL_ROWS = -(-(2 * TM + N_EXPERTS * (GRAN - 1)) // V7X_MXU_DIM) * V7X_MXU_DIM
LOCAL_GRANS = LOCAL_ROWS // GRAN
SPARE_GRAN = N_TOK_TILES * LOCAL_GRANS
MAX_TILES = (2 * TOKENS + N_TOK_TILES * N_EXPERTS * (GRAN - 1)) // TE + N_EXPERTS
PAIRS = 4
MERGE_SUB = 2
COMBINE_SUB = 4
STEP_ROWS = TE
STEP_GRANS = TILE_GRANS
MAX_STEPS = MAX_TILES
NBUF = 4
ROUTER_ROWS = 40
VMEM_LIMIT = V7X_VMEM_BYTES - 8 * 1024 * 1024
NEG_INF = float("-inf")


def _sigmoid(z):
    return 1.0 / (1.0 + jnp.exp(-z))


def _proj_kernel(x_ref, w_in_hbm_ref, convw_ref, woc_f32_ref, q_ref, k_ref, v_ref, za_ref, sgb_ref,
                 ubuf, w_in_ref, woc_ref, stage, stage_sem):
    s = pl.program_id(1)
    tm = TM_PROJ

    @pl.when((pl.program_id(0) == 0) & (s == 0))
    def _():
        def chunk(c):
            return pltpu.make_async_copy(w_in_hbm_ref.at[:, c * CONV_WIDTH:(c + 1) * CONV_WIDTH],
                                         stage.at[c % 2], stage_sem.at[c % 2])

        n_chunks = IN_COLS // CONV_WIDTH
        chunk(0).start()
        for c in range(n_chunks):
            if c + 1 < n_chunks:
                chunk(c + 1).start()
            chunk(c).wait()
            w_in_ref[:, c * CONV_WIDTH:(c + 1) * CONV_WIDTH] = stage[c % 2].astype(BF16)
        woc_ref[...] = woc_f32_ref[...].astype(BF16)

    xb = x_ref[0].astype(BF16)

    def proj(c0, c1):
        return jnp.dot(xb, w_in_ref[:, c0:c1], preferred_element_type=F32)

    c_b = proj(0, CONV_WIDTH)
    u = proj(CONV_WIDTH, 2 * CONV_WIDTH) * proj(2 * CONV_WIDTH, 3 * CONV_WIDTH)

    @pl.when(s == 0)
    def _():
        ubuf[0:8, :] = jnp.zeros((8, CONV_WIDTH), F32)

    ubuf[8:8 + tm, :] = u
    w = convw_ref[...]
    conv = w[2:3, :] * u + w[1:2, :] * ubuf[7:7 + tm, :] + w[0:1, :] * ubuf[6:6 + tm, :]
    ubuf[0:8, :] = u[tm - 8:tm, :]
    hc = (c_b * conv).astype(BF16)
    y_conv = jnp.dot(hc, woc_ref[...], preferred_element_type=F32)

    o = 3 * CONV_WIDTH
    q_ref[0] = (proj(o, o + ATTN_WIDTH) * (HEAD_DIM ** -0.5)).astype(BF16)
    k_ref[0] = proj(o + ATTN_WIDTH, o + 2 * ATTN_WIDTH).astype(BF16)
    v_ref[0] = proj(o + 2 * ATTN_WIDTH, o + 3 * ATTN_WIDTH).astype(BF16)
    o += 3 * ATTN_WIDTH
    za_ref[0] = (_sigmoid(proj(o, o + D_MODEL)) * y_conv).astype(BF16)
    sgb_ref[0] = _sigmoid(proj(o + D_MODEL, o + 2 * D_MODEL)).astype(BF16)


def _proj_call(x, w_in, conv_w, w_out_conv):
    tok_spec = lambda c: pl.BlockSpec((1, TM_PROJ, c), lambda b, s: (b, s, 0))
    full = lambda shape: pl.BlockSpec(shape, lambda b, s: (0,) * len(shape))
    once = lambda shape: pl.BlockSpec(shape, lambda b, s: (0,) * len(shape), pipeline_mode=pl.Buffered(1))
    return pl.pallas_call(
        _proj_kernel,
        grid=(BATCH, SEQ // TM_PROJ),
        in_specs=[tok_spec(D_MODEL), pl.BlockSpec(memory_space=pl.ANY), full((3, CONV_WIDTH)),
                  once((CONV_WIDTH, D_MODEL))],
        out_specs=[tok_spec(ATTN_WIDTH), tok_spec(ATTN_WIDTH), tok_spec(ATTN_WIDTH),
                   tok_spec(D_MODEL), tok_spec(D_MODEL)],
        out_shape=[jax.ShapeDtypeStruct((BATCH, SEQ, ATTN_WIDTH), BF16)] * 3
        + [jax.ShapeDtypeStruct((BATCH, SEQ, D_MODEL), BF16)] * 2,
        scratch_shapes=[pltpu.VMEM((TM_PROJ + 8, CONV_WIDTH), F32), pltpu.VMEM((D_MODEL, IN_COLS), BF16),
                        pltpu.VMEM((CONV_WIDTH, D_MODEL), BF16),
                        pltpu.VMEM((2, D_MODEL, CONV_WIDTH), F32), pltpu.SemaphoreType.DMA((2,))],
        compiler_params=pltpu.CompilerParams(
            dimension_semantics=("arbitrary", "arbitrary"), vmem_limit_bytes=VMEM_LIMIT),
        name="proj",
    )(x, w_in, conv_w, w_out_conv)


def _attn_kernel(slopes_ref, qa_ref, qb_ref, k_ref, v_ref, oa_ref, ob_ref,
                 kaug_ref, vt_ref, kmean_ref, qaug_ref, pv_ref, mloc_ref, t_ref, p_ref):
    hq = pl.program_id(1)
    j = pl.program_id(2)
    blk = MOBA_BLOCK
    pairs = range(PAIRS)
    pair_w = 2 * HEAD_DIM
    lanes = [slice(pair_w * pp, pair_w * (pp + 1)) for pp in pairs]

    @pl.when(j == 0)
    def _():
        klane = lax.broadcasted_iota(I32, (blk, pair_w), 1)
        koff = lax.broadcasted_iota(I32, (blk, pair_w), 0).astype(F32)
        k_extra = jnp.where(klane == 0, koff, jnp.where(klane == 1, 1.0, 0.0)).astype(BF16)
        orow = lax.broadcasted_iota(I32, (PV_ROWS - HEAD_DIM, blk), 0)
        ones_rows = jnp.where(orow == 0, 1.0, 0.0).astype(BF16)
        for pp in pairs:
            for n in range(N_BLOCKS):
                kblk = k_ref[0, n * blk:(n + 1) * blk, lanes[pp]]
                kaug_ref[pp, n, :, 0:pair_w] = kblk
                kaug_ref[pp, n, :, pair_w:2 * pair_w] = k_extra
                kmean_ref[pp, n:n + 1, :] = jnp.mean(kblk.astype(F32), axis=0, keepdims=True)
                v_t = v_ref[0, n * blk:(n + 1) * blk, lanes[pp]].astype(F32).T.astype(BF16)
                for hh in range(2):
                    vt_ref[pp, n, hh, 0:HEAD_DIM, :] = v_t[hh * HEAD_DIM:(hh + 1) * HEAD_DIM, :]
                    vt_ref[pp, n, hh, HEAD_DIM:PV_ROWS, :] = ones_rows

    lane = lax.broadcasted_iota(I32, (1, 2 * blk), 1)
    qoff_row = jnp.where(lane < blk, lane, lane - blk).astype(F32)
    feat = lax.broadcasted_iota(I32, (2 * HEAD_DIM, blk), 0)
    arow = lax.broadcasted_iota(I32, (2 * HEAD_DIM, 2 * blk), 0)
    blk_i = lax.broadcasted_iota(I32, (N_BLOCKS, 2 * blk), 0)
    key_i = lax.broadcasted_iota(I32, (blk, 2 * blk), 0)
    qry_j = lax.broadcasted_iota(I32, (blk, 2 * blk), 1)
    causal = key_i <= jnp.where(qry_j < blk, qry_j, qry_j - blk)
    slope_rows, q_extras = [], []
    for pp in pairs:
        head = 2 * (PAIRS * hq + pp)
        slope_rows.append(jnp.where(lane < blk, slopes_ref[head], slopes_ref[head + 1]))
        q_extras.append(jnp.where(arow == 0, slope_rows[pp],
                                  jnp.where(arow == 1, -slope_rows[pp] * qoff_row, 0.0)).astype(BF16))

    def prepare(q_ref, pp, slot, qblock):
        q_t = q_ref[0, :, lanes[pp]].astype(F32).T
        qcat = jnp.concatenate([jnp.where(feat < HEAD_DIM, q_t, 0.0), jnp.where(feat >= HEAD_DIM, q_t, 0.0)],
                               axis=1).astype(BF16)
        qaug_ref[pp, slot, 0:2 * HEAD_DIM, :] = qcat
        qaug_ref[pp, slot, 2 * HEAD_DIM:4 * HEAD_DIM, :] = q_extras[pp]
        gate = jnp.dot(kmean_ref[pp].astype(BF16), qcat, preferred_element_type=F32)
        cnt = jnp.zeros((N_BLOCKS, 2 * blk), F32)
        for m in range(N_BLOCKS):
            gm = gate[m:m + 1, :]
            beats = (gm > gate) | ((gm == gate) & (blk_i > m))
            cnt = cnt + jnp.where(beats & (qblock > m), 1.0, 0.0)
        return jnp.where((blk_i < qblock) & (cnt < float(MOBA_TOPK)), 1.0, 0.0)

    qblock_a = j
    qblock_b = N_BLOCKS - 1 - j
    sel_a = [prepare(qa_ref, pp, 0, qblock_a) for pp in pairs]
    sel_b = [prepare(qb_ref, pp, 1, qblock_b) for pp in pairs]

    n_mid = N_BLOCKS - 1
    slots = [(0, 0, qblock_a, True)]
    mids = []
    for s in range(1, n_mid + 1):
        is_a = s <= j
        slots.append((s, jnp.where(is_a, 0, 1), jnp.where(is_a, s - 1, s - 1 - j), False))
        mids.append((is_a, slots[-1][2]))
    slots.append((n_mid + 1, 1, qblock_b, True))

    for pp in pairs:
        for s, which, kb, _ in slots:
            t_ref[pp, s] = jnp.dot(kaug_ref[pp, kb], qaug_ref[pp, which], preferred_element_type=F32)
    for pp in pairs:
        for s, _, _, own in slots:
            t = t_ref[pp, s]
            if own:
                t = jnp.where(causal, t, NEG_INF)
            m_loc = jnp.max(t, axis=0, keepdims=True)
            p_ref[pp, s] = jnp.exp((t - m_loc).astype(BF16))
            mloc_ref[pp, s:s + 1, :] = m_loc
    for pp in pairs:
        for s, _, kb, _ in slots:
            pv_ref[pp, s, 0] = jnp.dot(vt_ref[pp, kb, 0], p_ref[pp, s, :, 0:blk], preferred_element_type=F32)
            pv_ref[pp, s, 1] = jnp.dot(vt_ref[pp, kb, 1], p_ref[pp, s, :, blk:2 * blk],
                                       preferred_element_type=F32)

    def combine(o_ref, pp, own_slot, sel, qblock, mine):
        neg = jnp.full((1, 2 * blk), -1e30, F32)
        pieces = [(own_slot, mloc_ref[pp, own_slot:own_slot + 1, :])]
        for s, (is_a, kb) in enumerate(mids, start=1):
            selrow = jnp.sum(jnp.where(blk_i == kb, sel, 0.0), axis=0, keepdims=True)
            belongs = jnp.where(is_a, 1.0, 0.0) if mine else jnp.where(is_a, 0.0, 1.0)
            used = selrow * belongs > 0.5
            shift = slope_rows[pp] * ((kb - qblock) * blk).astype(F32)
            pieces.append((s, jnp.where(used, mloc_ref[pp, s:s + 1, :] + shift, neg)))
        m_all = pieces[0][1]
        for _, m_s in pieces[1:]:
            m_all = jnp.maximum(m_all, m_s)
        acc = [jnp.zeros((PV_ROWS, blk), F32), jnp.zeros((PV_ROWS, blk), F32)]
        for s, m_s in pieces:
            w = jnp.exp(m_s - m_all)
            for hh in range(2):
                acc[hh] = acc[hh] + pv_ref[pp, s, hh] * w[:, hh * blk:(hh + 1) * blk]
        o_t = jnp.concatenate([a[0:HEAD_DIM, :] / a[HEAD_DIM:HEAD_DIM + 1, :] for a in acc], axis=0)
        o_ref[0, :, lanes[pp]] = o_t.T.astype(BF16)

    for pp in pairs:
        combine(oa_ref, pp, 0, sel_a[pp], qblock_a, True)
        combine(ob_ref, pp, n_mid + 1, sel_b[pp], qblock_b, False)


def _attn_call(slopes, q, k, v):
    half = N_BLOCKS // 2
    assert 2 * HEAD_DIM == V7X_LANES
    width = 2 * HEAD_DIM * PAIRS
    o_a, o_b = pl.pallas_call(
        _attn_kernel,
        grid_spec=pltpu.PrefetchScalarGridSpec(
            num_scalar_prefetch=1,
            grid=(BATCH, N_HEADS // (2 * PAIRS), half),
            in_specs=[
                pl.BlockSpec((1, MOBA_BLOCK, width), lambda b, h, j, sl: (b, j, h)),
                pl.BlockSpec((1, MOBA_BLOCK, width), lambda b, h, j, sl: (b, N_BLOCKS - 1 - j, h)),
                pl.BlockSpec((1, SEQ, width), lambda b, h, j, sl: (b, 0, h)),
                pl.BlockSpec((1, SEQ, width), lambda b, h, j, sl: (b, 0, h)),
            ],
            out_specs=[pl.BlockSpec((1, MOBA_BLOCK, width), lambda b, h, j, sl: (b, j, h)),
                       pl.BlockSpec((1, MOBA_BLOCK, width), lambda b, h, j, sl: (b, half - 1 - j, h))],
            scratch_shapes=[
                pltpu.VMEM((PAIRS, N_BLOCKS, MOBA_BLOCK, 4 * HEAD_DIM), BF16),
                pltpu.VMEM((PAIRS, N_BLOCKS, 2, PV_ROWS, MOBA_BLOCK), BF16),
                pltpu.VMEM((PAIRS, N_BLOCKS, 2 * HEAD_DIM), F32),
                pltpu.VMEM((PAIRS, 2, 4 * HEAD_DIM, 2 * MOBA_BLOCK), BF16),
                pltpu.VMEM((PAIRS, N_BLOCKS + 1, 2, PV_ROWS, MOBA_BLOCK), F32),
                pltpu.VMEM((PAIRS, 16, 2 * MOBA_BLOCK), F32),
                pltpu.VMEM((PAIRS, N_BLOCKS + 1, MOBA_BLOCK, 2 * MOBA_BLOCK), F32),
                pltpu.VMEM((PAIRS, N_BLOCKS + 1, MOBA_BLOCK, 2 * MOBA_BLOCK), BF16),
            ],
        ),
        out_shape=[jax.ShapeDtypeStruct((BATCH, SEQ // 2, ATTN_WIDTH), BF16)] * 2,
        compiler_params=pltpu.CompilerParams(
            dimension_semantics=("arbitrary", "arbitrary", "arbitrary"), vmem_limit_bytes=VMEM_LIMIT),
        name="moba_attn",
    )(slopes, q, q, k, v)
    return o_a, o_b


def _route(logits):
    row8 = lax.broadcasted_iota(I32, (8, TM), 0).astype(F32)
    gl = jnp.where(row8 < float(N_GROUPS), logits[0:8, :], NEG_INF)
    gexp = jnp.exp(gl - jnp.max(gl, axis=0, keepdims=True))
    gprob = gexp / jnp.sum(gexp, axis=0, keepdims=True)
    ptop = jnp.max(gprob, axis=0, keepdims=True)
    gtop = jnp.min(jnp.where(gprob == ptop, row8, 8.0), axis=0, keepdims=True)
    el = logits[8:ROUTER_ROWS, :]
    eg = jnp.where(gtop == 0.0, el[0:8, :],
                   jnp.where(gtop == 1.0, el[8:16, :], jnp.where(gtop == 2.0, el[16:24, :], el[24:32, :])))
    m1 = jnp.max(eg, axis=0, keepdims=True)
    i1 = jnp.min(jnp.where(eg == m1, row8, 8.0), axis=0, keepdims=True)
    eg2 = jnp.where(row8 == i1, NEG_INF, eg)
    m2 = jnp.max(eg2, axis=0, keepdims=True)
    i2 = jnp.min(jnp.where(eg2 == m2, row8, 8.0), axis=0, keepdims=True)
    t2 = jnp.exp(m2 - m1)
    gate1 = ptop * (1.0 / (1.0 + t2))
    gate2 = ptop * (t2 / (1.0 + t2))
    erow = lax.broadcasted_iota(I32, (N_EXPERTS, TM), 0).astype(F32)
    oh1 = jnp.where(erow == gtop * float(EXPERTS_PER_GROUP) + i1, 1.0, 0.0)
    oh2 = jnp.where(erow == gtop * float(EXPERTS_PER_GROUP) + i2, 1.0, 0.0)
    return gate1, gate2, oh1, oh2


def _merge_kernel(oa_ref, ob_ref, za_ref, sgb_ref, x_ref, woa_f32_ref, wo_f32_ref, g_ref, b_ref,
                  wr_hi_ref, wr_lo_ref, x1_ref, xs_ref, rf_ref, mt_ref, woa_ref, wo_ref):
    i = pl.program_id(0)

    @pl.when(i == 0)
    def _():
        woa_ref[...] = woa_f32_ref[...].astype(BF16)
        wo_ref[...] = wo_f32_ref[...].astype(BF16)

    subs = range(MERGE_SUB)
    rows = [slice(s * TM, (s + 1) * TM) for s in subs]
    steps_per_batch = N_BLOCKS // MERGE_SUB
    in_oa = lax.rem(i, steps_per_batch) < steps_per_batch // 2
    o = [jnp.where(in_oa, oa_ref[0, r, :], ob_ref[0, r, :]) for r in rows]
    y_attn = [jnp.dot(o[s], woa_ref[...], preferred_element_type=F32) for s in subs]
    y = [(za_ref[rows[s], :].astype(F32) + sgb_ref[rows[s], :].astype(F32) * y_attn[s]).astype(BF16)
         for s in subs]
    mix = [jnp.dot(y[s], wo_ref[...], preferred_element_type=F32) for s in subs]
    x1 = []
    for s in subs:
        h = ALPHA * x_ref[rows[s], :] + mix[s]
        mu = jnp.mean(h, axis=-1, keepdims=True)
        hc = h - mu
        var = jnp.mean(hc * hc, axis=-1, keepdims=True)
        x1.append(hc * lax.rsqrt(var + LN_EPS) * g_ref[...] + b_ref[...])
        x1_ref[rows[s], :] = x1[s]

    xh = [x1[s].astype(BF16) for s in subs]
    xl = [(x1[s] - xh[s].astype(F32)).astype(BF16) for s in subs]
    wh = wr_hi_ref[...]
    logits = [(jnp.dot(xh[s], wh, preferred_element_type=F32)
               + jnp.dot(xl[s], wh, preferred_element_type=F32)
               + jnp.dot(xh[s], wr_lo_ref[...], preferred_element_type=F32)).T for s in subs]
    routes = [_route(logits[s]) for s in subs]

    ta = lax.broadcasted_iota(I32, (TM, TM), 0)
    tb = lax.broadcasted_iota(I32, (TM, TM), 1)
    upper = jnp.where(ta < tb, 1.0, 0.0).astype(BF16)
    ea = lax.broadcasted_iota(I32, (N_EXPERTS, N_EXPERTS), 0)
    eb = lax.broadcasted_iota(I32, (N_EXPERTS, N_EXPERTS), 1)
    lower = jnp.where(eb < ea, 1.0, 0.0).astype(BF16)
    lrow = lax.broadcasted_iota(I32, (LOCAL_ROWS, TM), 0).astype(F32)
    zero = jnp.zeros((1, TM), F32)
    cum = [jnp.dot((routes[s][2] + routes[s][3]).astype(BF16), upper, preferred_element_type=F32) for s in subs]
    perm = []
    for s in subs:
        gate1, gate2, oh1, oh2 = routes[s]
        n_e = jnp.sum(oh1 + oh2, axis=1, keepdims=True)
        m_rep = jnp.broadcast_to(jnp.floor((n_e + float(GRAN - 1)) * (1.0 / GRAN)), (N_EXPERTS, V7X_LANES))
        run_start = jnp.dot(lower, m_rep.astype(BF16), preferred_element_type=F32)
        tot = cum[s] + float(GRAN) * run_start[:, 0:1]
        lp1 = jnp.sum(oh1 * tot, axis=0, keepdims=True)
        lp2 = jnp.sum(oh2 * tot, axis=0, keepdims=True)
        perm.append(jnp.where((lrow == lp1) | (lrow == lp2), 1.0, 0.0).astype(BF16))
        rf_ref[:, rows[s]] = jnp.concatenate([gate1, gate2, lp1, lp2, zero, zero, zero, zero], axis=0)
        mt_ref[s * N_EXPERTS:(s + 1) * N_EXPERTS, :] = m_rep
    for s in subs:
        xs_ref[s * LOCAL_ROWS:(s + 1) * LOCAL_ROWS, :] = jnp.dot(
            perm[s], xh[s], preferred_element_type=F32).astype(BF16)


def _merge_call(o_a, o_b, za, sgb, x, woa, wo, g, b, wr_hi, wr_lo):
    tm = MERGE_SUB * TM
    tok = lambda c: pl.BlockSpec((tm, c), lambda i: (i, 0))
    full = lambda shape: pl.BlockSpec(shape, lambda i: (0,) * len(shape))
    per_batch = SEQ // tm
    half = per_batch // 2
    o_a_spec = pl.BlockSpec((1, tm, ATTN_WIDTH), lambda i: (i // per_batch, jnp.minimum(i % per_batch, half - 1), 0))
    o_b_spec = pl.BlockSpec((1, tm, ATTN_WIDTH), lambda i: (i // per_batch, jnp.maximum(i % per_batch - half, 0), 0))
    return pl.pallas_call(
        _merge_kernel,
        grid=(TOKENS // tm,),
        in_specs=[o_a_spec, o_b_spec, tok(D_MODEL), tok(D_MODEL), tok(D_MODEL),
                  full((ATTN_WIDTH, D_MODEL)), full((D_MODEL, D_MODEL)), full((1, D_MODEL)),
                  full((1, D_MODEL)), full((D_MODEL, V7X_LANES)), full((D_MODEL, V7X_LANES))],
        out_specs=[tok(D_MODEL), pl.BlockSpec((MERGE_SUB * LOCAL_ROWS, D_MODEL), lambda i: (i, 0)),
                   pl.BlockSpec((8, tm), lambda i: (0, i)),
                   pl.BlockSpec((MERGE_SUB * N_EXPERTS, V7X_LANES), lambda i: (i, 0))],
        out_shape=[jax.ShapeDtypeStruct((TOKENS, D_MODEL), F32),
                   jax.ShapeDtypeStruct((N_TOK_TILES * LOCAL_ROWS, D_MODEL), BF16),
                   jax.ShapeDtypeStruct((8, TOKENS), F32),
                   jax.ShapeDtypeStruct((N_TOK_TILES * N_EXPERTS, V7X_LANES), F32)],
        scratch_shapes=[pltpu.VMEM((ATTN_WIDTH, D_MODEL), BF16), pltpu.VMEM((D_MODEL, D_MODEL), BF16)],
        compiler_params=pltpu.CompilerParams(
            dimension_semantics=("arbitrary",), vmem_limit_bytes=VMEM_LIMIT),
        name="merge_ln1_route",
    )(o_a, o_b, za, sgb, x, woa, wo, g, b, wr_hi, wr_lo)


def _granule_copy(src_ref, src_gran, dst_ref, dst_gran, sem):
    src = pl.multiple_of(src_gran * GRAN, GRAN)
    dst = pl.multiple_of(dst_gran * GRAN, GRAN)
    return pltpu.make_async_copy(src_ref.at[pl.ds(src, GRAN), :], dst_ref.at[pl.ds(dst, GRAN), :], sem)


def _expert_kernel(te_ref, nt_ref, gsrc_ref, gdst_ref, ug_ref, wpar_ref, wnext_ref,
                   xs_ref, wg_hbm_ref, wu_hbm_ref, wd_hbm_ref,
                   ys_ref, xbuf, ybuf, zbuf, wg_ref, wu_ref, wd_ref, wg_stage, wu_stage, wd_stage,
                   in_sem, out_sem, zero_sem, w_sem):
    j = pl.program_id(0)
    n_tiles = nt_ref[0]
    slot = lax.rem(j, NBUF)
    prev_slot = lax.rem(j + NBUF - 1, NBUF)

    def tile_gather(step, s):
        for g in range(STEP_GRANS):
            _granule_copy(xs_ref, gsrc_ref[step * STEP_GRANS + g], xbuf.at[s], g,
                          in_sem.at[s]).start(priority=g % 2)

    def prev_scatter():
        for g in range(STEP_GRANS):
            _granule_copy(ybuf.at[prev_slot], g, ys_ref, gdst_ref[j * STEP_GRANS + g],
                          out_sem.at[prev_slot]).start(priority=g % 2)

    @pl.when(j == 0)
    def _():
        tile_gather(0, 0)
        ybuf[NBUF - 1] = jnp.zeros((STEP_ROWS, D_MODEL), BF16)
        zbuf[...] = jnp.zeros((GRAN, D_MODEL), BF16)
        for part in range(NBUF):
            spare = pltpu.make_async_copy(
                ybuf.at[NBUF - 1], ys_ref.at[pl.ds((SPARE_GRAN + part * STEP_GRANS) * GRAN, STEP_ROWS), :],
                out_sem.at[NBUF - 1])
            spare.start()
            spare.wait()

        for ahead in range(1, NBUF - 1):
            tile_gather(ahead, ahead)

    def zero_copy(t, g):
        return _granule_copy(zbuf, 0, ys_ref, t * LOCAL_GRANS + g, zero_sem)

    @pl.when(jnp.logical_and(j >= 1, j <= N_TOK_TILES))
    def _():
        def wait(g, c):
            zero_copy(j - 1, g).wait()
            return c

        lax.fori_loop(ug_ref[j - 1], LOCAL_GRANS, wait, 0)

    @pl.when(j < N_TOK_TILES)
    def _():
        def start(g, c):
            zero_copy(j, g).start()
            return c

        lax.fori_loop(ug_ref[j], LOCAL_GRANS, start, 0)

    def gather_wait():
        pltpu.make_async_copy(xs_ref.at[pl.ds(0, STEP_ROWS), :], xbuf.at[slot], in_sem.at[slot]).wait()

    @pl.when(jnp.logical_and(j >= NBUF - 1, j - NBUF < n_tiles))
    def _():
        pltpu.make_async_copy(ybuf.at[slot], ys_ref.at[pl.ds(0, STEP_ROWS), :], out_sem.at[slot]).wait()

    @pl.when(jnp.logical_and(j >= n_tiles, j < n_tiles + NBUF - 1))
    def _():
        gather_wait()

    @pl.when(j == n_tiles)
    def _():
        prev_scatter()

    def weight_copies(expert, s):
        return [pltpu.make_async_copy(hbm.at[expert], stage.at[s], w_sem.at[s])
                for hbm, stage in ((wg_hbm_ref, wg_stage), (wu_hbm_ref, wu_stage), (wd_hbm_ref, wd_stage))]

    @pl.when(j == 0)
    def _():
        for cp in weight_copies(te_ref[0], 0):
            cp.start()

    @pl.when(jnp.logical_and(j < n_tiles, jnp.logical_or(j == 0, te_ref[j] != te_ref[jnp.maximum(j - 1, 0)])))
    def _():
        s = wpar_ref[j]
        for cp in weight_copies(te_ref[j], s):
            cp.wait()
        wg_ref[...] = wg_stage[s].astype(BF16)
        wu_ref[...] = wu_stage[s].astype(BF16)
        wd_ref[...] = wd_stage[s].astype(BF16)
        nxt = wnext_ref[te_ref[j]]

        @pl.when(nxt >= 0)
        def _():
            for cp in weight_copies(nxt, 1 - s):
                cp.start()

    @pl.when(j < n_tiles)
    def _():
        gather_wait()
        chains = range(0, TE, CHAIN_ROWS)
        xb = [xbuf[slot, r:r + CHAIN_ROWS, :] for r in chains]
        hg = [jnp.dot(x, wg_ref[...], preferred_element_type=F32) for x in xb]
        hu = [jnp.dot(x, wu_ref[...], preferred_element_type=F32) for x in xb]
        tile_gather(j + NBUF - 1, prev_slot)
        prev_scatter()
        h = [(a * _sigmoid(a) * b).astype(BF16) for a, b in zip(hg, hu)]
        for hc, r in zip(h, chains):
            ybuf[slot, r:r + CHAIN_ROWS, :] = jnp.dot(hc, wd_ref[...], preferred_element_type=F32).astype(BF16)


def _expert_call(tile_expert, n_steps, gsrc, gdst, used_grans, w_parity, w_next, xs, wg, wu, wd):
    hbm = pl.BlockSpec(memory_space=pl.ANY)
    return pl.pallas_call(
        _expert_kernel,
        grid_spec=pltpu.PrefetchScalarGridSpec(
            num_scalar_prefetch=7,
            grid=(MAX_STEPS + NBUF,),
            in_specs=[hbm, hbm, hbm, hbm],
            out_specs=hbm,
            scratch_shapes=[pltpu.VMEM((NBUF, STEP_ROWS, D_MODEL), BF16),
                            pltpu.VMEM((NBUF, STEP_ROWS, D_MODEL), BF16),
                            pltpu.VMEM((GRAN, D_MODEL), BF16),
                            pltpu.VMEM((D_MODEL, D_EXPERT), BF16), pltpu.VMEM((D_MODEL, D_EXPERT), BF16),
                            pltpu.VMEM((D_EXPERT, D_MODEL), BF16),
                            pltpu.VMEM((2, D_MODEL, D_EXPERT), F32), pltpu.VMEM((2, D_MODEL, D_EXPERT), F32),
                            pltpu.VMEM((2, D_EXPERT, D_MODEL), F32),
                            pltpu.SemaphoreType.DMA((NBUF,)), pltpu.SemaphoreType.DMA((NBUF,)),
                            pltpu.SemaphoreType.DMA, pltpu.SemaphoreType.DMA((2,))],
        ),
        out_shape=jax.ShapeDtypeStruct(((SPARE_GRAN + NBUF * STEP_GRANS) * GRAN, D_MODEL), BF16),
        compiler_params=pltpu.CompilerParams(
            dimension_semantics=("arbitrary",), vmem_limit_bytes=VMEM_LIMIT),
        name="experts",
    )(tile_expert, n_steps, gsrc, gdst, used_grans, w_parity, w_next, xs, wg, wu, wd)


def _combine_kernel(ys_ref, x1_ref, rf_ref, g_ref, b_ref, out_ref):
    subs = range(COMBINE_SUB)
    col = lax.broadcasted_iota(I32, (TM, LOCAL_ROWS), 1).astype(F32)
    route = [rf_ref[:, s * TM:(s + 1) * TM].T for s in subs]
    unsort = [(jnp.where(col == r[:, 2:3], r[:, 0:1], 0.0)
               + jnp.where(col == r[:, 3:4], r[:, 1:2], 0.0)).astype(BF16) for r in route]
    ffn = [jnp.dot(unsort[s], ys_ref[s * LOCAL_ROWS:(s + 1) * LOCAL_ROWS, :], preferred_element_type=F32)
           for s in subs]
    for s in subs:
        h = ALPHA * x1_ref[s * TM:(s + 1) * TM, :] + ffn[s]
        mu = jnp.mean(h, axis=-1, keepdims=True)
        hc = h - mu
        var = jnp.mean(hc * hc, axis=-1, keepdims=True)
        out_ref[s * TM:(s + 1) * TM, :] = hc * lax.rsqrt(var + LN_EPS) * g_ref[...] + b_ref[...]


def _combine_call(ys, x1, rf, g, b):
    tm = COMBINE_SUB * TM
    return pl.pallas_call(
        _combine_kernel,
        grid=(TOKENS // tm,),
        in_specs=[pl.BlockSpec((COMBINE_SUB * LOCAL_ROWS, D_MODEL), lambda i: (i, 0)),
                  pl.BlockSpec((tm, D_MODEL), lambda i: (i, 0)),
                  pl.BlockSpec((8, tm), lambda i: (0, i)),
                  pl.BlockSpec((1, D_MODEL), lambda i: (0, 0)),
                  pl.BlockSpec((1, D_MODEL), lambda i: (0, 0))],
        out_specs=pl.BlockSpec((tm, D_MODEL), lambda i: (i, 0)),
        out_shape=jax.ShapeDtypeStruct((TOKENS, D_MODEL), F32),
        compiler_params=pltpu.CompilerParams(
            dimension_semantics=("arbitrary",), vmem_limit_bytes=VMEM_LIMIT),
        name="combine_ln2",
    )(ys, x1, rf, g, b)


def _router_cols(w_router_group, w_router_expert):
    w = jnp.concatenate([w_router_group, jnp.zeros((D_MODEL, 4), F32), w_router_expert,
                         jnp.zeros((D_MODEL, V7X_LANES - ROUTER_ROWS), F32)], axis=1)
    hi = w.astype(BF16)
    lo = (w - hi.astype(F32)).astype(BF16)
    return hi, lo


def _layer(x, w_in, conv_w, w_out_conv, w_out_attn, w_o, ln1_g, ln1_b,
           w_router_group, w_router_expert, w_gate, w_up, w_down, ln2_g, ln2_b):
    slopes = jnp.asarray([2.0 ** (-8.0 * (h + 1) / N_HEADS) for h in range(N_HEADS)], F32)
    q, k, v, za, sgb = _proj_call(x, w_in, conv_w, w_out_conv)
    o_a, o_b = _attn_call(slopes, q, k, v)

    wr_hi, wr_lo = _router_cols(w_router_group, w_router_expert)
    x1, xs, rf, mt = _merge_call(
        o_a, o_b, za.reshape(TOKENS, D_MODEL), sgb.reshape(TOKENS, D_MODEL),
        x.reshape(TOKENS, D_MODEL), w_out_attn, w_o,
        ln1_g.reshape(1, D_MODEL), ln1_b.reshape(1, D_MODEL), wr_hi, wr_lo)

    grans = mt.reshape(N_TOK_TILES, N_EXPERTS, V7X_LANES)[:, :, 0].astype(I32)
    local_start = jnp.cumsum(grans, axis=1) - grans
    grans_t = grans.T
    tiles_e = (jnp.sum(grans_t, axis=1) + TILE_GRANS - 1) // TILE_GRANS
    tile_end = jnp.cumsum(tiles_e)
    n_steps = tile_end[-1].reshape(1)
    all_tiles = MAX_STEPS + NBUF
    tile_ids = jnp.arange(all_tiles, dtype=I32)
    tile_expert = jnp.minimum(
        jnp.sum((tile_ids[:, None] >= tile_end[None, :]).astype(I32), axis=1), N_EXPERTS - 1)
    run_slot = TILE_GRANS * (tile_end - tiles_e)[:, None] + jnp.cumsum(grans_t, axis=1) - grans_t
    run_src = jnp.arange(N_TOK_TILES, dtype=I32)[None, :] * LOCAL_GRANS + local_start.T
    pick = (tile_expert[:all_tiles, None] == jnp.arange(N_EXPERTS, dtype=I32)[None, :])[:, :, None]
    t_slot = jnp.sum(jnp.where(pick, run_slot[None], 0), axis=1)
    t_len = jnp.sum(jnp.where(pick, grans_t[None], 0), axis=1)
    t_src = jnp.sum(jnp.where(pick, run_src[None], 0), axis=1)
    slots = jnp.arange(all_tiles * TILE_GRANS, dtype=I32).reshape(all_tiles, TILE_GRANS)
    k = slots[:, :, None] - t_slot[:, None, :]
    hit = (k >= 0) & (k < t_len[:, None, :])
    gran = jnp.sum(jnp.where(hit, t_src[:, None, :] + k, 0), axis=2).reshape(-1)
    filled = (jnp.sum(hit.astype(I32), axis=2) > 0).reshape(-1)
    slots = slots.reshape(-1)
    gsrc = jnp.where(filled, gran, 0)
    gdst = jnp.where(filled, gran, SPARE_GRAN + slots % (NBUF * STEP_GRANS))
    gdst = jnp.concatenate([SPARE_GRAN + (NBUF - 1) * STEP_GRANS + jnp.arange(STEP_GRANS, dtype=I32), gdst])

    starts = jnp.concatenate([jnp.ones((1,), I32), (tile_expert[1:] != tile_expert[:-1]).astype(I32)])
    w_parity = (jnp.cumsum(starts) - 1) % 2
    ids = jnp.arange(N_EXPERTS, dtype=I32)
    later = jnp.where((tiles_e > 0)[None, :] & (ids[None, :] > ids[:, None]), ids[None, :], N_EXPERTS)
    w_next = jnp.min(later, axis=1)
    w_next = jnp.where(w_next == N_EXPERTS, -1, w_next)

    ys = _expert_call(tile_expert, n_steps, gsrc, gdst, jnp.sum(grans, axis=1), w_parity, w_next, xs,
                      w_gate, w_up, w_down)
    out = _combine_call(ys, x1, rf, ln2_g.reshape(1, D_MODEL), ln2_b.reshape(1, D_MODEL))
    return out.reshape(BATCH, SEQ, D_MODEL)


def kernel(x, w_in, conv_w, w_out_conv, w_out_attn, w_o, ln1_g, ln1_b, w_router_group, w_router_expert, w_gate, w_up, w_down, ln2_g, ln2_b):
    depth = w_in.shape[0]
    for l in range(depth):
        x = _layer(x, w_in[l], conv_w[l], w_out_conv[l], w_out_attn[l], w_o[l], ln1_g[l], ln1_b[l],
                   w_router_group[l], w_router_expert[l], w_gate[l], w_up[l], w_down[l], ln2_g[l], ln2_b[l])
    return x
```

```python
import jax
import jax.numpy as jnp
from jax import lax
from jax.experimental import pallas as pl
from jax.experimental.pallas import tpu as pltpu

F32 = jnp.float32
BF16 = jnp.bfloat16
I32 = jnp.int32

V7X_VMEM_BYTES = 64 * 1024 * 1024
V7X_LANES = 128
V7X_MXU_DIM = 256

D_MODEL = 1024
BATCH = 8
SEQ = 2048
TOKENS = BATCH * SEQ
CONV_WIDTH = 512
N_HEADS = 8
HEAD_DIM = 64
ATTN_WIDTH = N_HEADS * HEAD_DIM
MOBA_BLOCK = 256
N_BLOCKS = SEQ // MOBA_BLOCK
MOBA_TOPK = 3
N_GROUPS = 4
EXPERTS_PER_GROUP = 8
N_EXPERTS = N_GROUPS * EXPERTS_PER_GROUP
D_EXPERT = 256
LN_EPS = 1e-5
ALPHA = 2.0 ** 0.25
IN_COLS = 3 * CONV_WIDTH + 3 * ATTN_WIDTH + 2 * D_MODEL

TM = 256
TM_PROJ = 1024
TE = 512
CHAIN_ROWS = 256
PV_ROWS = HEAD_DIM + 16
GRAN = 16
TILE_GRANS = TE // GRAN
N_TOK_TILES = TOKENS // TM
LOCAL_ROWS = -(-(2 * TM + N_EXPERTS * (GRAN - 1)) // V7X_MXU_DIM) * V7X_MXU_DIM
LOCAL_GRANS = LOCAL_ROWS // GRAN
SPARE_GRAN = N_TOK_TILES * LOCAL_GRANS
MAX_TILES = (2 * TOKENS + N_TOK_TILES * N_EXPERTS * (GRAN - 1)) // TE + N_EXPERTS
PAIRS = 4
MERGE_SUB = 2
COMBINE_SUB = 4
STEP_ROWS = TE
STEP_GRANS = TILE_GRANS
MAX_STEPS = MAX_TILES
NBUF = 5
ROUTER_ROWS = 40
VMEM_LIMIT = V7X_VMEM_BYTES - 8 * 1024 * 1024
NEG_INF = float("-inf")


def _sigmoid(z):
    return 1.0 / (1.0 + jnp.exp(-z))


def _proj_kernel(x_ref, w_in_hbm_ref, convw_ref, woc_f32_ref, q_ref, k_ref, v_ref, za_ref, sgb_ref,
                 ubuf, w_in_ref, woc_ref, stage, stage_sem):
    s = pl.program_id(1)
    tm = TM_PROJ

    @pl.when((pl.program_id(0) == 0) & (s == 0))
    def _():
        def chunk(c):
            return pltpu.make_async_copy(w_in_hbm_ref.at[:, c * CONV_WIDTH:(c + 1) * CONV_WIDTH],
                                         stage.at[c % 2], stage_sem.at[c % 2])

        n_chunks = IN_COLS // CONV_WIDTH
        chunk(0).start()
        for c in range(n_chunks):
            if c + 1 < n_chunks:
                chunk(c + 1).start()
            chunk(c).wait()
            w_in_ref[:, c * CONV_WIDTH:(c + 1) * CONV_WIDTH] = stage[c % 2].astype(BF16)
        woc_ref[...] = woc_f32_ref[...].astype(BF16)

    xb = x_ref[0].astype(BF16)

    def proj(c0, c1):
        return jnp.dot(xb, w_in_ref[:, c0:c1], preferred_element_type=F32)

    c_b = proj(0, CONV_WIDTH)
    u = proj(CONV_WIDTH, 2 * CONV_WIDTH) * proj(2 * CONV_WIDTH, 3 * CONV_WIDTH)

    @pl.when(s == 0)
    def _():
        ubuf[0:8, :] = jnp.zeros((8, CONV_WIDTH), F32)

    ubuf[8:8 + tm, :] = u
    w = convw_ref[...]
    conv = w[2:3, :] * u + w[1:2, :] * ubuf[7:7 + tm, :] + w[0:1, :] * ubuf[6:6 + tm, :]
    ubuf[0:8, :] = u[tm - 8:tm, :]
    hc = (c_b * conv).astype(BF16)
    y_conv = jnp.dot(hc, woc_ref[...], preferred_element_type=F32)

    o = 3 * CONV_WIDTH
    q_ref[0] = (proj(o, o + ATTN_WIDTH) * (HEAD_DIM ** -0.5)).astype(BF16)
    k_ref[0] = proj(o + ATTN_WIDTH, o + 2 * ATTN_WIDTH).astype(BF16)
    v_ref[0] = proj(o + 2 * ATTN_WIDTH, o + 3 * ATTN_WIDTH).astype(BF16)
    o += 3 * ATTN_WIDTH
    za_ref[0] = (_sigmoid(proj(o, o + D_MODEL)) * y_conv).astype(BF16)
    sgb_ref[0] = _sigmoid(proj(o + D_MODEL, o + 2 * D_MODEL)).astype(BF16)


def _proj_call(x, w_in, conv_w, w_out_conv):
    tok_spec = lambda c: pl.BlockSpec((1, TM_PROJ, c), lambda b, s: (b, s, 0))
    full = lambda shape: pl.BlockSpec(shape, lambda b, s: (0,) * len(shape))
    once = lambda shape: pl.BlockSpec(shape, lambda b, s: (0,) * len(shape), pipeline_mode=pl.Buffered(1))
    return pl.pallas_call(
        _proj_kernel,
        grid=(BATCH, SEQ // TM_PROJ),
        in_specs=[tok_spec(D_MODEL), pl.BlockSpec(memory_space=pl.ANY), full((3, CONV_WIDTH)),
                  once((CONV_WIDTH, D_MODEL))],
        out_specs=[tok_spec(ATTN_WIDTH), tok_spec(ATTN_WIDTH), tok_spec(ATTN_WIDTH),
                   tok_spec(D_MODEL), tok_spec(D_MODEL)],
        out_shape=[jax.ShapeDtypeStruct((BATCH, SEQ, ATTN_WIDTH), BF16)] * 3
        + [jax.ShapeDtypeStruct((BATCH, SEQ, D_MODEL), BF16)] * 2,
        scratch_shapes=[pltpu.VMEM((TM_PROJ + 8, CONV_WIDTH), F32), pltpu.VMEM((D_MODEL, IN_COLS), BF16),
                        pltpu.VMEM((CONV_WIDTH, D_MODEL), BF16),
                        pltpu.VMEM((2, D_MODEL, CONV_WIDTH), F32), pltpu.SemaphoreType.DMA((2,))],
        compiler_params=pltpu.CompilerParams(
            dimension_semantics=("arbitrary", "arbitrary"), vmem_limit_bytes=VMEM_LIMIT),
        name="proj",
    )(x, w_in, conv_w, w_out_conv)


def _attn_kernel(slopes_ref, qa_ref, qb_ref, k_ref, v_ref, oa_ref, ob_ref,
                 kaug_ref, vt_ref, kmean_ref, qaug_ref, pv_ref, mloc_ref, t_ref, p_ref):
    hq = pl.program_id(1)
    j = pl.program_id(2)
    blk = MOBA_BLOCK
    pairs = range(PAIRS)
    pair_w = 2 * HEAD_DIM
    lanes = [slice(pair_w * pp, pair_w * (pp + 1)) for pp in pairs]

    @pl.when(j == 0)
    def _():
        klane = lax.broadcasted_iota(I32, (blk, pair_w), 1)
        koff = lax.broadcasted_iota(I32, (blk, pair_w), 0).astype(F32)
        k_extra = jnp.where(klane == 0, koff, jnp.where(klane == 1, 1.0, 0.0)).astype(BF16)
        orow = lax.broadcasted_iota(I32, (PV_ROWS - HEAD_DIM, blk), 0)
        ones_rows = jnp.where(orow == 0, 1.0, 0.0).astype(BF16)
        for pp in pairs:
            for n in range(N_BLOCKS):
                kblk = k_ref[0, n * blk:(n + 1) * blk, lanes[pp]]
                kaug_ref[pp, n, :, 0:pair_w] = kblk
                kaug_ref[pp, n, :, pair_w:2 * pair_w] = k_extra
                kmean_ref[pp, n:n + 1, :] = jnp.mean(kblk.astype(F32), axis=0, keepdims=True)
                v_t = v_ref[0, n * blk:(n + 1) * blk, lanes[pp]].astype(F32).T.astype(BF16)
                for hh in range(2):
                    vt_ref[pp, n, hh, 0:HEAD_DIM, :] = v_t[hh * HEAD_DIM:(hh + 1) * HEAD_DIM, :]
                    vt_ref[pp, n, hh, HEAD_DIM:PV_ROWS, :] = ones_rows

    lane = lax.broadcasted_iota(I32, (1, 2 * blk), 1)
    qoff_row = jnp.where(lane < blk, lane, lane - blk).astype(F32)
    feat = lax.broadcasted_iota(I32, (2 * HEAD_DIM, blk), 0)
    arow = lax.broadcasted_iota(I32, (2 * HEAD_DIM, 2 * blk), 0)
    blk_i = lax.broadcasted_iota(I32, (N_BLOCKS, 2 * blk), 0)
    key_i = lax.broadcasted_iota(I32, (blk, 2 * blk), 0)
    qry_j = lax.broadcasted_iota(I32, (blk, 2 * blk), 1)
    causal = key_i <= jnp.where(qry_j < blk, qry_j, qry_j - blk)
    slope_rows, q_extras = [], []
    for pp in pairs:
        head = 2 * (PAIRS * hq + pp)
        slope_rows.append(jnp.where(lane < blk, slopes_ref[head], slopes_ref[head + 1]))
        q_extras.append(jnp.where(arow == 0, slope_rows[pp],
                                  jnp.where(arow == 1, -slope_rows[pp] * qoff_row, 0.0)).astype(BF16))

    def prepare(q_ref, pp, slot, qblock):
        q_t = q_ref[0, :, lanes[pp]].astype(F32).T
        qcat = jnp.concatenate([jnp.where(feat < HEAD_DIM, q_t, 0.0), jnp.where(feat >= HEAD_DIM, q_t, 0.0)],
                               axis=1).astype(BF16)
        qaug_ref[pp, slot, 0:2 * HEAD_DIM, :] = qcat
        qaug_ref[pp, slot, 2 * HEAD_DIM:4 * HEAD_DIM, :] = q_extras[pp]
        gate = jnp.dot(kmean_ref[pp].astype(BF16), qcat, preferred_element_type=F32)
        cnt = jnp.zeros((N_BLOCKS, 2 * blk), F32)
        for m in range(N_BLOCKS):
            gm = gate[m:m + 1, :]
            beats = (gm > gate) | ((gm == gate) & (blk_i > m))
            cnt = cnt + jnp.where(beats & (qblock > m), 1.0, 0.0)
        return jnp.where((blk_i < qblock) & (cnt < float(MOBA_TOPK)), 1.0, 0.0)

    qblock_a = j
    qblock_b = N_BLOCKS - 1 - j
    sel_a = [prepare(qa_ref, pp, 0, qblock_a) for pp in pairs]
    sel_b = [prepare(qb_ref, pp, 1, qblock_b) for pp in pairs]

    n_mid = N_BLOCKS - 1
    slots = [(0, 0, qblock_a, True)]
    mids = []
    for s in range(1, n_mid + 1):
        is_a = s <= j
        slots.append((s, jnp.where(is_a, 0, 1), jnp.where(is_a, s - 1, s - 1 - j), False))
        mids.append((is_a, slots[-1][2]))
    slots.append((n_mid + 1, 1, qblock_b, True))

    for pp in pairs:
        for s, which, kb, _ in slots:
            t_ref[pp, s] = jnp.dot(kaug_ref[pp, kb], qaug_ref[pp, which], preferred_element_type=F32)
    for pp in pairs:
        for s, _, _, own in slots:
            t = t_ref[pp, s]
            if own:
                t = jnp.where(causal, t, NEG_INF)
            m_loc = jnp.max(t, axis=0, keepdims=True)
            p_ref[pp, s] = jnp.exp((t - m_loc).astype(BF16))
            mloc_ref[pp, s:s + 1, :] = m_loc
    for pp in pairs:
        for s, _, kb, _ in slots:
            pv_ref[pp, s, 0] = jnp.dot(vt_ref[pp, kb, 0], p_ref[pp, s, :, 0:blk], preferred_element_type=F32)
            pv_ref[pp, s, 1] = jnp.dot(vt_ref[pp, kb, 1], p_ref[pp, s, :, blk:2 * blk],
                                       preferred_element_type=F32)

    def combine(o_ref, pp, own_slot, sel, qblock, mine):
        neg = jnp.full((1, 2 * blk), -1e30, F32)
        pieces = [(own_slot, mloc_ref[pp, own_slot:own_slot + 1, :])]
        for s, (is_a, kb) in enumerate(mids, start=1):
            selrow = jnp.sum(jnp.where(blk_i == kb, sel, 0.0), axis=0, keepdims=True)
            belongs = jnp.where(is_a, 1.0, 0.0) if mine else jnp.where(is_a, 0.0, 1.0)
            used = selrow * belongs > 0.5
            shift = slope_rows[pp] * ((kb - qblock) * blk).astype(F32)
            pieces.append((s, jnp.where(used, mloc_ref[pp, s:s + 1, :] + shift, neg)))
        m_all = pieces[0][1]
        for _, m_s in pieces[1:]:
            m_all = jnp.maximum(m_all, m_s)
        acc = [jnp.zeros((PV_ROWS, blk), F32), jnp.zeros((PV_ROWS, blk), F32)]
        for s, m_s in pieces:
            w = jnp.exp(m_s - m_all)
            for hh in range(2):
                acc[hh] = acc[hh] + pv_ref[pp, s, hh] * w[:, hh * blk:(hh + 1) * blk]
        o_t = jnp.concatenate([a[0:HEAD_DIM, :] / a[HEAD_DIM:HEAD_DIM + 1, :] for a in acc], axis=0)
        o_ref[0, :, lanes[pp]] = o_t.T.astype(BF16)

    for pp in pairs:
        combine(oa_ref, pp, 0, sel_a[pp], qblock_a, True)
        combine(ob_ref, pp, n_mid + 1, sel_b[pp], qblock_b, False)


def _attn_call(slopes, q, k, v):
    half = N_BLOCKS // 2
    assert 2 * HEAD_DIM == V7X_LANES
    width = 2 * HEAD_DIM * PAIRS
    o_a, o_b = pl.pallas_call(
        _attn_kernel,
        grid_spec=pltpu.PrefetchScalarGridSpec(
            num_scalar_prefetch=1,
            grid=(BATCH, N_HEADS // (2 * PAIRS), half),
            in_specs=[
                pl.BlockSpec((1, MOBA_BLOCK, width), lambda b, h, j, sl: (b, j, h)),
                pl.BlockSpec((1, MOBA_BLOCK, width), lambda b, h, j, sl: (b, N_BLOCKS - 1 - j, h)),
                pl.BlockSpec((1, SEQ, width), lambda b, h, j, sl: (b, 0, h)),
                pl.BlockSpec((1, SEQ, width), lambda b, h, j, sl: (b, 0, h)),
            ],
            out_specs=[pl.BlockSpec((1, MOBA_BLOCK, width), lambda b, h, j, sl: (b, j, h)),
                       pl.BlockSpec((1, MOBA_BLOCK, width), lambda b, h, j, sl: (b, half - 1 - j, h))],
            scratch_shapes=[
                pltpu.VMEM((PAIRS, N_BLOCKS, MOBA_BLOCK, 4 * HEAD_DIM), BF16),
                pltpu.VMEM((PAIRS, N_BLOCKS, 2, PV_ROWS, MOBA_BLOCK), BF16),
                pltpu.VMEM((PAIRS, N_BLOCKS, 2 * HEAD_DIM), F32),
                pltpu.VMEM((PAIRS, 2, 4 * HEAD_DIM, 2 * MOBA_BLOCK), BF16),
                pltpu.VMEM((PAIRS, N_BLOCKS + 1, 2, PV_ROWS, MOBA_BLOCK), F32),
                pltpu.VMEM((PAIRS, 16, 2 * MOBA_BLOCK), F32),
                pltpu.VMEM((PAIRS, N_BLOCKS + 1, MOBA_BLOCK, 2 * MOBA_BLOCK), F32),
                pltpu.VMEM((PAIRS, N_BLOCKS + 1, MOBA_BLOCK, 2 * MOBA_BLOCK), BF16),
            ],
        ),
        out_shape=[jax.ShapeDtypeStruct((BATCH, SEQ // 2, ATTN_WIDTH), BF16)] * 2,
        compiler_params=pltpu.CompilerParams(
            dimension_semantics=("arbitrary", "arbitrary", "arbitrary"), vmem_limit_bytes=VMEM_LIMIT),
        name="moba_attn",
    )(slopes, q, q, k, v)
    return o_a, o_b


def _route(logits):
    row8 = lax.broadcasted_iota(I32, (8, TM), 0).astype(F32)
    gl = jnp.where(row8 < float(N_GROUPS), logits[0:8, :], NEG_INF)
    gexp = jnp.exp(gl - jnp.max(gl, axis=0, keepdims=True))
    gprob = gexp / jnp.sum(gexp, axis=0, keepdims=True)
    ptop = jnp.max(gprob, axis=0, keepdims=True)
    gtop = jnp.min(jnp.where(gprob == ptop, row8, 8.0), axis=0, keepdims=True)
    el = logits[8:ROUTER_ROWS, :]
    eg = jnp.where(gtop == 0.0, el[0:8, :],
                   jnp.where(gtop == 1.0, el[8:16, :], jnp.where(gtop == 2.0, el[16:24, :], el[24:32, :])))
    m1 = jnp.max(eg, axis=0, keepdims=True)
    i1 = jnp.min(jnp.where(eg == m1, row8, 8.0), axis=0, keepdims=True)
    eg2 = jnp.where(row8 == i1, NEG_INF, eg)
    m2 = jnp.max(eg2, axis=0, keepdims=True)
    i2 = jnp.min(jnp.where(eg2 == m2, row8, 8.0), axis=0, keepdims=True)
    t2 = jnp.exp(m2 - m1)
    gate1 = ptop * (1.0 / (1.0 + t2))
    gate2 = ptop * (t2 / (1.0 + t2))
    erow = lax.broadcasted_iota(I32, (N_EXPERTS, TM), 0).astype(F32)
    oh1 = jnp.where(erow == gtop * float(EXPERTS_PER_GROUP) + i1, 1.0, 0.0)
    oh2 = jnp.where(erow == gtop * float(EXPERTS_PER_GROUP) + i2, 1.0, 0.0)
    return gate1, gate2, oh1, oh2


def _merge_kernel(oa_ref, ob_ref, za_ref, sgb_ref, x_ref, woa_f32_ref, wo_f32_ref, g_ref, b_ref,
                  wr_hi_ref, wr_lo_ref, x1_ref, xs_ref, rf_ref, mt_ref, woa_ref, wo_ref):
    i = pl.program_id(0)

    @pl.when(i == 0)
    def _():
        woa_ref[...] = woa_f32_ref[...].astype(BF16)
        wo_ref[...] = wo_f32_ref[...].astype(BF16)

    subs = range(MERGE_SUB)
    rows = [slice(s * TM, (s + 1) * TM) for s in subs]
    steps_per_batch = N_BLOCKS // MERGE_SUB
    in_oa = lax.rem(i, steps_per_batch) < steps_per_batch // 2
    o = [jnp.where(in_oa, oa_ref[0, r, :], ob_ref[0, r, :]) for r in rows]
    y_attn = [jnp.dot(o[s], woa_ref[...], preferred_element_type=F32) for s in subs]
    y = [(za_ref[rows[s], :].astype(F32) + sgb_ref[rows[s], :].astype(F32) * y_attn[s]).astype(BF16)
         for s in subs]
    mix = [jnp.dot(y[s], wo_ref[...], preferred_element_type=F32) for s in subs]
    x1 = []
    for s in subs:
        h = ALPHA * x_ref[rows[s], :] + mix[s]
        mu = jnp.mean(h, axis=-1, keepdims=True)
        hc = h - mu
        var = jnp.mean(hc * hc, axis=-1, keepdims=True)
        x1.append(hc * lax.rsqrt(var + LN_EPS) * g_ref[...] + b_ref[...])
        x1_ref[rows[s], :] = x1[s]

    xh = [x1[s].astype(BF16) for s in subs]
    xl = [(x1[s] - xh[s].astype(F32)).astype(BF16) for s in subs]
    wh = wr_hi_ref[...]
    logits = [(jnp.dot(xh[s], wh, preferred_element_type=F32)
               + jnp.dot(xl[s], wh, preferred_element_type=F32)
               + jnp.dot(xh[s], wr_lo_ref[...], preferred_element_type=F32)).T for s in subs]
    routes = [_route(logits[s]) for s in subs]

    ta = lax.broadcasted_iota(I32, (TM, TM), 0)
    tb = lax.broadcasted_iota(I32, (TM, TM), 1)
    upper = jnp.where(ta < tb, 1.0, 0.0).astype(BF16)
    ea = lax.broadcasted_iota(I32, (N_EXPERTS, N_EXPERTS), 0)
    eb = lax.broadcasted_iota(I32, (N_EXPERTS, N_EXPERTS), 1)
    lower = jnp.where(eb < ea, 1.0, 0.0).astype(BF16)
    lrow = lax.broadcasted_iota(I32, (LOCAL_ROWS, TM), 0).astype(F32)
    zero = jnp.zeros((1, TM), F32)
    cum = [jnp.dot((routes[s][2] + routes[s][3]).astype(BF16), upper, preferred_element_type=F32) for s in subs]
    perm = []
    for s in subs:
        gate1, gate2, oh1, oh2 = routes[s]
        n_e = jnp.sum(oh1 + oh2, axis=1, keepdims=True)
        m_rep = jnp.broadcast_to(jnp.floor((n_e + float(GRAN - 1)) * (1.0 / GRAN)), (N_EXPERTS, V7X_LANES))
        run_start = jnp.dot(lower, m_rep.astype(BF16), preferred_element_type=F32)
        tot = cum[s] + float(GRAN) * run_start[:, 0:1]
        lp1 = jnp.sum(oh1 * tot, axis=0, keepdims=True)
        lp2 = jnp.sum(oh2 * tot, axis=0, keepdims=True)
        perm.append(jnp.where((lrow == lp1) | (lrow == lp2), 1.0, 0.0).astype(BF16))
        rf_ref[:, rows[s]] = jnp.concatenate([gate1, gate2, lp1, lp2, zero, zero, zero, zero], axis=0)
        mt_ref[s * N_EXPERTS:(s + 1) * N_EXPERTS, :] = m_rep
    for s in subs:
        xs_ref[s * LOCAL_ROWS:(s + 1) * LOCAL_ROWS, :] = jnp.dot(
            perm[s], xh[s], preferred_element_type=F32).astype(BF16)


def _merge_call(o_a, o_b, za, sgb, x, woa, wo, g, b, wr_hi, wr_lo):
    tm = MERGE_SUB * TM
    tok = lambda c: pl.BlockSpec((tm, c), lambda i: (i, 0))
    full = lambda shape: pl.BlockSpec(shape, lambda i: (0,) * len(shape))
    per_batch = SEQ // tm
    half = per_batch // 2
    o_a_spec = pl.BlockSpec((1, tm, ATTN_WIDTH), lambda i: (i // per_batch, jnp.minimum(i % per_batch, half - 1), 0))
    o_b_spec = pl.BlockSpec((1, tm, ATTN_WIDTH), lambda i: (i // per_batch, jnp.maximum(i % per_batch - half, 0), 0))
    return pl.pallas_call(
        _merge_kernel,
        grid=(TOKENS // tm,),
        in_specs=[o_a_spec, o_b_spec, tok(D_MODEL), tok(D_MODEL), tok(D_MODEL),
                  full((ATTN_WIDTH, D_MODEL)), full((D_MODEL, D_MODEL)), full((1, D_MODEL)),
                  full((1, D_MODEL)), full((D_MODEL, V7X_LANES)), full((D_MODEL, V7X_LANES))],
        out_specs=[tok(D_MODEL), pl.BlockSpec((MERGE_SUB * LOCAL_ROWS, D_MODEL), lambda i: (i, 0)),
                   pl.BlockSpec((8, tm), lambda i: (0, i)),
                   pl.BlockSpec((MERGE_SUB * N_EXPERTS, V7X_LANES), lambda i: (i, 0))],
        out_shape=[jax.ShapeDtypeStruct((TOKENS, D_MODEL), F32),
                   jax.ShapeDtypeStruct((N_TOK_TILES * LOCAL_ROWS, D_MODEL), BF16),
                   jax.ShapeDtypeStruct((8, TOKENS), F32),
                   jax.ShapeDtypeStruct((N_TOK_TILES * N_EXPERTS, V7X_LANES), F32)],
        scratch_shapes=[pltpu.VMEM((ATTN_WIDTH, D_MODEL), BF16), pltpu.VMEM((D_MODEL, D_MODEL), BF16)],
        compiler_params=pltpu.CompilerParams(
            dimension_semantics=("arbitrary",), vmem_limit_bytes=VMEM_LIMIT),
        name="merge_ln1_route",
    )(o_a, o_b, za, sgb, x, woa, wo, g, b, wr_hi, wr_lo)


def _granule_copy(src_ref, src_gran, dst_ref, dst_gran, sem):
    src = pl.multiple_of(src_gran * GRAN, GRAN)
    dst = pl.multiple_of(dst_gran * GRAN, GRAN)
    return pltpu.make_async_copy(src_ref.at[pl.ds(src, GRAN), :], dst_ref.at[pl.ds(dst, GRAN), :], sem)


def _expert_kernel(te_ref, nt_ref, gsrc_ref, gdst_ref, ug_ref, wpar_ref, wnext_ref,
                   xs_ref, wg_hbm_ref, wu_hbm_ref, wd_hbm_ref,
                   ys_ref, xbuf, ybuf, zbuf, wg_ref, wu_ref, wd_ref, wg_stage, wu_stage, wd_stage,
                   in_sem, out_sem, zero_sem, w_sem):
    j = pl.program_id(0)
    n_tiles = nt_ref[0]
    slot = lax.rem(j, NBUF)
    prev_slot = lax.rem(j + NBUF - 1, NBUF)

    def tile_gather(step, s):
        for g in range(STEP_GRANS):
            _granule_copy(xs_ref, gsrc_ref[step * STEP_GRANS + g], xbuf.at[s], g,
                          in_sem.at[s]).start(priority=g % 2)

    def prev_scatter():
        for g in range(STEP_GRANS):
            _granule_copy(ybuf.at[prev_slot], g, ys_ref, gdst_ref[j * STEP_GRANS + g],
                          out_sem.at[prev_slot]).start(priority=g % 2)

    @pl.when(j == 0)
    def _():
        tile_gather(0, 0)
        ybuf[NBUF - 1] = jnp.zeros((STEP_ROWS, D_MODEL), BF16)
        zbuf[...] = jnp.zeros((GRAN, D_MODEL), BF16)
        for part in range(NBUF):
            spare = pltpu.make_async_copy(
                ybuf.at[NBUF - 1], ys_ref.at[pl.ds((SPARE_GRAN + part * STEP_GRANS) * GRAN, STEP_ROWS), :],
                out_sem.at[NBUF - 1])
            spare.start()
            spare.wait()

        for ahead in range(1, NBUF - 1):
            tile_gather(ahead, ahead)

    def zero_copy(t, g):
        return _granule_copy(zbuf, 0, ys_ref, t * LOCAL_GRANS + g, zero_sem)

    @pl.when(jnp.logical_and(j >= 1, j <= N_TOK_TILES))
    def _():
        def wait(g, c):
            zero_copy(j - 1, g).wait()
            return c

        lax.fori_loop(ug_ref[j - 1], LOCAL_GRANS, wait, 0)

    @pl.when(j < N_TOK_TILES)
    def _():
        def start(g, c):
            zero_copy(j, g).start()
            return c

        lax.fori_loop(ug_ref[j], LOCAL_GRANS, start, 0)

    def gather_wait():
        pltpu.make_async_copy(xs_ref.at[pl.ds(0, STEP_ROWS), :], xbuf.at[slot], in_sem.at[slot]).wait()

    @pl.when(jnp.logical_and(j >= NBUF - 1, j - NBUF < n_tiles))
    def _():
        pltpu.make_async_copy(ybuf.at[slot], ys_ref.at[pl.ds(0, STEP_ROWS), :], out_sem.at[slot]).wait()

    @pl.when(jnp.logical_and(j >= n_tiles, j < n_tiles + NBUF - 1))
    def _():
        gather_wait()

    @pl.when(j == n_tiles)
    def _():
        prev_scatter()

    def weight_copies(expert, s):
        return [pltpu.make_async_copy(hbm.at[expert], stage.at[s], w_sem.at[s])
                for hbm, stage in ((wg_hbm_ref, wg_stage), (wu_hbm_ref, wu_stage), (wd_hbm_ref, wd_stage))]

    @pl.when(j == 0)
    def _():
        for cp in weight_copies(te_ref[0], 0):
            cp.start()

    @pl.when(jnp.logical_and(j < n_tiles, jnp.logical_or(j == 0, te_ref[j] != te_ref[jnp.maximum(j - 1, 0)])))
    def _():
        s = wpar_ref[j]
        for cp in weight_copies(te_ref[j], s):
            cp.wait()
        wg_ref[...] = wg_stage[s].astype(BF16)
        wu_ref[...] = wu_stage[s].astype(BF16)
        wd_ref[...] = wd_stage[s].astype(BF16)
        nxt = wnext_ref[te_ref[j]]

        @pl.when(nxt >= 0)
        def _():
            for cp in weight_copies(nxt, 1 - s):
                cp.start()

    @pl.when(j < n_tiles)
    def _():
        gather_wait()
        chains = range(0, TE, CHAIN_ROWS)
        xb = [xbuf[slot, r:r + CHAIN_ROWS, :] for r in chains]
        hg = [jnp.dot(x, wg_ref[...], preferred_element_type=F32) for x in xb]
        hu = [jnp.dot(x, wu_ref[...], preferred_element_type=F32) for x in xb]
        tile_gather(j + NBUF - 1, prev_slot)
        prev_scatter()
        h = [(a * _sigmoid(a) * b).astype(BF16) for a, b in zip(hg, hu)]
        for hc, r in zip(h, chains):
            ybuf[slot, r:r + CHAIN_ROWS, :] = jnp.dot(hc, wd_ref[...], preferred_element_type=F32).astype(BF16)


def _expert_call(tile_expert, n_steps, gsrc, gdst, used_grans, w_parity, w_next, xs, wg, wu, wd):
    hbm = pl.BlockSpec(memory_space=pl.ANY)
    return pl.pallas_call(
        _expert_kernel,
        grid_spec=pltpu.PrefetchScalarGridSpec(
            num_scalar_prefetch=7,
            grid=(MAX_STEPS + NBUF,),
            in_specs=[hbm, hbm, hbm, hbm],
            out_specs=hbm,
            scratch_shapes=[pltpu.VMEM((NBUF, STEP_ROWS, D_MODEL), BF16),
                            pltpu.VMEM((NBUF, STEP_ROWS, D_MODEL), BF16),
                            pltpu.VMEM((GRAN, D_MODEL), BF16),
                            pltpu.VMEM((D_MODEL, D_EXPERT), BF16), pltpu.VMEM((D_MODEL, D_EXPERT), BF16),
                            pltpu.VMEM((D_EXPERT, D_MODEL), BF16),
                            pltpu.VMEM((2, D_MODEL, D_EXPERT), F32), pltpu.VMEM((2, D_MODEL, D_EXPERT), F32),
                            pltpu.VMEM((2, D_EXPERT, D_MODEL), F32),
                            pltpu.SemaphoreType.DMA((NBUF,)), pltpu.SemaphoreType.DMA((NBUF,)),
                            pltpu.SemaphoreType.DMA, pltpu.SemaphoreType.DMA((2,))],
        ),
        out_shape=jax.ShapeDtypeStruct(((SPARE_GRAN + NBUF * STEP_GRANS) * GRAN, D_MODEL), BF16),
        compiler_params=pltpu.CompilerParams(
            dimension_semantics=("arbitrary",), vmem_limit_bytes=VMEM_LIMIT),
        name="experts",
    )(tile_expert, n_steps, gsrc, gdst, used_grans, w_parity, w_next, xs, wg, wu, wd)


def _combine_kernel(ys_ref, x1_ref, rf_ref, g_ref, b_ref, out_ref):
    subs = range(COMBINE_SUB)
    col = lax.broadcasted_iota(I32, (TM, LOCAL_ROWS), 1).astype(F32)
    route = [rf_ref[:, s * TM:(s + 1) * TM].T for s in subs]
    unsort = [(jnp.where(col == r[:, 2:3], r[:, 0:1], 0.0)
               + jnp.where(col == r[:, 3:4], r[:, 1:2], 0.0)).astype(BF16) for r in route]
    ffn = [jnp.dot(unsort[s], ys_ref[s * LOCAL_ROWS:(s + 1) * LOCAL_ROWS, :], preferred_element_type=F32)
           for s in subs]
    for s in subs:
        h = ALPHA * x1_ref[s * TM:(s + 1) * TM, :] + ffn[s]
        mu = jnp.mean(h, axis=-1, keepdims=True)
        hc = h - mu
        var = jnp.mean(hc * hc, axis=-1, keepdims=True)
        out_ref[s * TM:(s + 1) * TM, :] = hc * lax.rsqrt(var + LN_EPS) * g_ref[...] + b_ref[...]


def _combine_call(ys, x1, rf, g, b):
    tm = COMBINE_SUB * TM
    return pl.pallas_call(
        _combine_kernel,
        grid=(TOKENS // tm,),
        in_specs=[pl.BlockSpec((COMBINE_SUB * LOCAL_ROWS, D_MODEL), lambda i: (i, 0)),
                  pl.BlockSpec((tm, D_MODEL), lambda i: (i, 0)),
                  pl.BlockSpec((8, tm), lambda i: (0, i)),
                  pl.BlockSpec((1, D_MODEL), lambda i: (0, 0)),
                  pl.BlockSpec((1, D_MODEL), lambda i: (0, 0))],
        out_specs=pl.BlockSpec((tm, D_MODEL), lambda i: (i, 0)),
        out_shape=jax.ShapeDtypeStruct((TOKENS, D_MODEL), F32),
        compiler_params=pltpu.CompilerParams(
            dimension_semantics=("arbitrary",), vmem_limit_bytes=VMEM_LIMIT),
        name="combine_ln2",
    )(ys, x1, rf, g, b)


def _router_cols(w_router_group, w_router_expert):
    w = jnp.concatenate([w_router_group, jnp.zeros((D_MODEL, 4), F32), w_router_expert,
                         jnp.zeros((D_MODEL, V7X_LANES - ROUTER_ROWS), F32)], axis=1)
    hi = w.astype(BF16)
    lo = (w - hi.astype(F32)).astype(BF16)
    return hi, lo


def _layer(x, w_in, conv_w, w_out_conv, w_out_attn, w_o, ln1_g, ln1_b,
           w_router_group, w_router_expert, w_gate, w_up, w_down, ln2_g, ln2_b):
    slopes = jnp.asarray([2.0 ** (-8.0 * (h + 1) / N_HEADS) for h in range(N_HEADS)], F32)
    q, k, v, za, sgb = _proj_call(x, w_in, conv_w, w_out_conv)
    o_a, o_b = _attn_call(slopes, q, k, v)

    wr_hi, wr_lo = _router_cols(w_router_group, w_router_expert)
    x1, xs, rf, mt = _merge_call(
        o_a, o_b, za.reshape(TOKENS, D_MODEL), sgb.reshape(TOKENS, D_MODEL),
        x.reshape(TOKENS, D_MODEL), w_out_attn, w_o,
        ln1_g.reshape(1, D_MODEL), ln1_b.reshape(1, D_MODEL), wr_hi, wr_lo)

    grans = mt.reshape(N_TOK_TILES, N_EXPERTS, V7X_LANES)[:, :, 0].astype(I32)
    local_start = jnp.cumsum(grans, axis=1) - grans
    grans_t = grans.T
    tiles_e = (jnp.sum(grans_t, axis=1) + TILE_GRANS - 1) // TILE_GRANS
    tile_end = jnp.cumsum(tiles_e)
    n_steps = tile_end[-1].reshape(1)
    all_tiles = MAX_STEPS + NBUF
    tile_ids = jnp.arange(all_tiles, dtype=I32)
    tile_expert = jnp.minimum(
        jnp.sum((tile_ids[:, None] >= tile_end[None, :]).astype(I32), axis=1), N_EXPERTS - 1)
    run_slot = TILE_GRANS * (tile_end - tiles_e)[:, None] + jnp.cumsum(grans_t, axis=1) - grans_t
    run_src = jnp.arange(N_TOK_TILES, dtype=I32)[None, :] * LOCAL_GRANS + local_start.T
    pick = (tile_expert[:all_tiles, None] == jnp.arange(N_EXPERTS, dtype=I32)[None, :])[:, :, None]
    t_slot = jnp.sum(jnp.where(pick, run_slot[None], 0), axis=1)
    t_len = jnp.sum(jnp.where(pick, grans_t[None], 0), axis=1)
    t_src = jnp.sum(jnp.where(pick, run_src[None], 0), axis=1)
    slots = jnp.arange(all_tiles * TILE_GRANS, dtype=I32).reshape(all_tiles, TILE_GRANS)
    k = slots[:, :, None] - t_slot[:, None, :]
    hit = (k >= 0) & (k < t_len[:, None, :])
    gran = jnp.sum(jnp.where(hit, t_src[:, None, :] + k, 0), axis=2).reshape(-1)
    filled = (jnp.sum(hit.astype(I32), axis=2) > 0).reshape(-1)
    slots = slots.reshape(-1)
    gsrc = jnp.where(filled, gran, 0)
    gdst = jnp.where(filled, gran, SPARE_GRAN + slots % (NBUF * STEP_GRANS))
    gdst = jnp.concatenate([SPARE_GRAN + (NBUF - 1) * STEP_GRANS + jnp.arange(STEP_GRANS, dtype=I32), gdst])

    starts = jnp.concatenate([jnp.ones((1,), I32), (tile_expert[1:] != tile_expert[:-1]).astype(I32)])
    w_parity = (jnp.cumsum(starts) - 1) % 2
    ids = jnp.arange(N_EXPERTS, dtype=I32)
    later = jnp.where((tiles_e > 0)[None, :] & (ids[None, :] > ids[:, None]), ids[None, :], N_EXPERTS)
    w_next = jnp.min(later, axis=1)
    w_next = jnp.where(w_next == N_EXPERTS, -1, w_next)

    ys = _expert_call(tile_expert, n_steps, gsrc, gdst, jnp.sum(grans, axis=1), w_parity, w_next, xs,
                      w_gate, w_up, w_down)
    out = _combine_call(ys, x1, rf, ln2_g.reshape(1, D_MODEL), ln2_b.reshape(1, D_MODEL))
    return out.reshape(BATCH, SEQ, D_MODEL)


def kernel(x, w_in, conv_w, w_out_conv, w_out_attn, w_o, ln1_g, ln1_b, w_router_group, w_router_expert, w_gate, w_up, w_down, ln2_g, ln2_b):
    depth = w_in.shape[0]
    for l in range(depth):
        x = _layer(x, w_in[l], conv_w[l], w_out_conv[l], w_out_attn[l], w_o[l], ln1_g[l], ln1_b[l],
                   w_router_group[l], w_router_expert[l], w_gate[l], w_up[l], w_down[l], ln2_g[l], ln2_b[l])
    return x
```

```python
import jax
import jax.numpy as jnp
from jax import lax
from jax.experimental import pallas as pl
from jax.experimental.pallas import tpu as pltpu

F32 = jnp.float32
BF16 = jnp.bfloat16
I32 = jnp.int32

V7X_VMEM_BYTES = 64 * 1024 * 1024
V7X_LANES = 128
V7X_MXU_DIM = 256

D_MODEL = 1024
BATCH = 8
SEQ = 2048
TOKENS = BATCH * SEQ
CONV_WIDTH = 512
N_HEADS = 8
HEAD_DIM = 64
ATTN_WIDTH = N_HEADS * HEAD_DIM
MOBA_BLOCK = 256
N_BLOCKS = SEQ // MOBA_BLOCK
MOBA_TOPK = 3
N_GROUPS = 4
EXPERTS_PER_GROUP = 8
N_EXPERTS = N_GROUPS * EXPERTS_PER_GROUP
D_EXPERT = 256
LN_EPS = 1e-5
ALPHA = 2.0 ** 0.25
IN_COLS = 3 * CONV_WIDTH + 3 * ATTN_WIDTH + 2 * D_MODEL

TM = 256
TM_PROJ = 1024
TE = 512
CHAIN_ROWS = 256
PV_ROWS = HEAD_DIM + 16
GRAN = 16
TILE_GRANS = TE // GRAN
N_TOK_TILES = TOKENS // TM
LOCAL_ROWS = -(-(2 * TM + N_EXPERTS * (GRAN - 1)) // V7X_MXU_DIM) * V7X_MXU_DIM
LOCAL_GRANS = LOCAL_ROWS // GRAN
SPARE_GRAN = N_TOK_TILES * LOCAL_GRANS
ZERO_BITS = (LOCAL_GRANS - 2 * TM // GRAN).bit_length()
ZERO_ROWS = GRAN << (ZERO_BITS - 1)
MAX_TILES = (2 * TOKENS + N_TOK_TILES * N_EXPERTS * (GRAN - 1)) // TE + N_EXPERTS
PAIRS = 4
MERGE_SUB = 2
COMBINE_SUB = 4
STEP_ROWS = TE
STEP_GRANS = TILE_GRANS
MAX_STEPS = MAX_TILES
NBUF = 5
ROUTER_ROWS = 40
VMEM_LIMIT = V7X_VMEM_BYTES - 8 * 1024 * 1024
NEG_INF = float("-inf")


def _sigmoid(z):
    return 1.0 / (1.0 + jnp.exp(-z))


def _proj_kernel(x_ref, w_in_hbm_ref, convw_ref, woc_f32_ref, q_ref, k_ref, v_ref, za_ref, sgb_ref,
                 ubuf, w_in_ref, woc_ref, stage, stage_sem):
    s = pl.program_id(1)
    tm = TM_PROJ

    @pl.when((pl.program_id(0) == 0) & (s == 0))
    def _():
        def chunk(c):
            return pltpu.make_async_copy(w_in_hbm_ref.at[:, c * CONV_WIDTH:(c + 1) * CONV_WIDTH],
                                         stage.at[c % 2], stage_sem.at[c % 2])

        n_chunks = IN_COLS // CONV_WIDTH
        chunk(0).start()
        for c in range(n_chunks):
            if c + 1 < n_chunks:
                chunk(c + 1).start()
            chunk(c).wait()
            w_in_ref[:, c * CONV_WIDTH:(c + 1) * CONV_WIDTH] = stage[c % 2].astype(BF16)
        woc_ref[...] = woc_f32_ref[...].astype(BF16)

    xb = x_ref[0].astype(BF16)

    def proj(c0, c1):
        return jnp.dot(xb, w_in_ref[:, c0:c1], preferred_element_type=F32)

    c_b = proj(0, CONV_WIDTH)
    u = proj(CONV_WIDTH, 2 * CONV_WIDTH) * proj(2 * CONV_WIDTH, 3 * CONV_WIDTH)

    @pl.when(s == 0)
    def _():
        ubuf[0:8, :] = jnp.zeros((8, CONV_WIDTH), F32)

    ubuf[8:8 + tm, :] = u
    w = convw_ref[...]
    conv = w[2:3, :] * u + w[1:2, :] * ubuf[7:7 + tm, :] + w[0:1, :] * ubuf[6:6 + tm, :]
    ubuf[0:8, :] = u[tm - 8:tm, :]
    hc = (c_b * conv).astype(BF16)
    y_conv = jnp.dot(hc, woc_ref[...], preferred_element_type=F32)

    o = 3 * CONV_WIDTH
    q_ref[0] = (proj(o, o + ATTN_WIDTH) * (HEAD_DIM ** -0.5)).astype(BF16)
    k_ref[0] = proj(o + ATTN_WIDTH, o + 2 * ATTN_WIDTH).astype(BF16)
    v_ref[0] = proj(o + 2 * ATTN_WIDTH, o + 3 * ATTN_WIDTH).astype(BF16)
    o += 3 * ATTN_WIDTH
    za_ref[0] = (_sigmoid(proj(o, o + D_MODEL)) * y_conv).astype(BF16)
    sgb_ref[0] = _sigmoid(proj(o + D_MODEL, o + 2 * D_MODEL)).astype(BF16)


def _proj_call(x, w_in, conv_w, w_out_conv):
    tok_spec = lambda c: pl.BlockSpec((1, TM_PROJ, c), lambda b, s: (b, s, 0))
    full = lambda shape: pl.BlockSpec(shape, lambda b, s: (0,) * len(shape))
    once = lambda shape: pl.BlockSpec(shape, lambda b, s: (0,) * len(shape), pipeline_mode=pl.Buffered(1))
    return pl.pallas_call(
        _proj_kernel,
        grid=(BATCH, SEQ // TM_PROJ),
        in_specs=[tok_spec(D_MODEL), pl.BlockSpec(memory_space=pl.ANY), full((3, CONV_WIDTH)),
                  once((CONV_WIDTH, D_MODEL))],
        out_specs=[tok_spec(ATTN_WIDTH), tok_spec(ATTN_WIDTH), tok_spec(ATTN_WIDTH),
                   tok_spec(D_MODEL), tok_spec(D_MODEL)],
        out_shape=[jax.ShapeDtypeStruct((BATCH, SEQ, ATTN_WIDTH), BF16)] * 3
        + [jax.ShapeDtypeStruct((BATCH, SEQ, D_MODEL), BF16)] * 2,
        scratch_shapes=[pltpu.VMEM((TM_PROJ + 8, CONV_WIDTH), F32), pltpu.VMEM((D_MODEL, IN_COLS), BF16),
                        pltpu.VMEM((CONV_WIDTH, D_MODEL), BF16),
                        pltpu.VMEM((2, D_MODEL, CONV_WIDTH), F32), pltpu.SemaphoreType.DMA((2,))],
        compiler_params=pltpu.CompilerParams(
            dimension_semantics=("arbitrary", "arbitrary"), vmem_limit_bytes=VMEM_LIMIT),
        name="proj",
    )(x, w_in, conv_w, w_out_conv)


def _attn_kernel(slopes_ref, qa_ref, qb_ref, k_ref, v_ref, oa_ref, ob_ref,
                 kaug_ref, vt_ref, kmean_ref, qaug_ref, pv_ref, mloc_ref, t_ref, p_ref):
    hq = pl.program_id(1)
    j = pl.program_id(2)
    blk = MOBA_BLOCK
    pairs = range(PAIRS)
    pair_w = 2 * HEAD_DIM
    lanes = [slice(pair_w * pp, pair_w * (pp + 1)) for pp in pairs]

    @pl.when(j == 0)
    def _():
        klane = lax.broadcasted_iota(I32, (blk, pair_w), 1)
        koff = lax.broadcasted_iota(I32, (blk, pair_w), 0).astype(F32)
        k_extra = jnp.where(klane == 0, koff, jnp.where(klane == 1, 1.0, 0.0)).astype(BF16)
        orow = lax.broadcasted_iota(I32, (PV_ROWS - HEAD_DIM, blk), 0)
        ones_rows = jnp.where(orow == 0, 1.0, 0.0).astype(BF16)
        for pp in pairs:
            for n in range(N_BLOCKS):
                kblk = k_ref[0, n * blk:(n + 1) * blk, lanes[pp]]
                kaug_ref[pp, n, :, 0:pair_w] = kblk
                kaug_ref[pp, n, :, pair_w:2 * pair_w] = k_extra
                kmean_ref[pp, n:n + 1, :] = jnp.mean(kblk.astype(F32), axis=0, keepdims=True)
                v_t = v_ref[0, n * blk:(n + 1) * blk, lanes[pp]].astype(F32).T.astype(BF16)
                for hh in range(2):
                    vt_ref[pp, n, hh, 0:HEAD_DIM, :] = v_t[hh * HEAD_DIM:(hh + 1) * HEAD_DIM, :]
                    vt_ref[pp, n, hh, HEAD_DIM:PV_ROWS, :] = ones_rows

    lane = lax.broadcasted_iota(I32, (1, 2 * blk), 1)
    qoff_row = jnp.where(lane < blk, lane, lane - blk).astype(F32)
    feat = lax.broadcasted_iota(I32, (2 * HEAD_DIM, blk), 0)
    arow = lax.broadcasted_iota(I32, (2 * HEAD_DIM, 2 * blk), 0)
    blk_i = lax.broadcasted_iota(I32, (N_BLOCKS, 2 * blk), 0)
    key_i = lax.broadcasted_iota(I32, (blk, 2 * blk), 0)
    qry_j = lax.broadcasted_iota(I32, (blk, 2 * blk), 1)
    causal = key_i <= jnp.where(qry_j < blk, qry_j, qry_j - blk)
    slope_rows, q_extras = [], []
    for pp in pairs:
        head = 2 * (PAIRS * hq + pp)
        slope_rows.append(jnp.where(lane < blk, slopes_ref[head], slopes_ref[head + 1]))
        q_extras.append(jnp.where(arow == 0, slope_rows[pp],
                                  jnp.where(arow == 1, -slope_rows[pp] * qoff_row, 0.0)).astype(BF16))

    def prepare(q_ref, pp, slot, qblock):
        q_t = q_ref[0, :, lanes[pp]].astype(F32).T
        qcat = jnp.concatenate([jnp.where(feat < HEAD_DIM, q_t, 0.0), jnp.where(feat >= HEAD_DIM, q_t, 0.0)],
                               axis=1).astype(BF16)
        qaug_ref[pp, slot, 0:2 * HEAD_DIM, :] = qcat
        qaug_ref[pp, slot, 2 * HEAD_DIM:4 * HEAD_DIM, :] = q_extras[pp]
        gate = jnp.dot(kmean_ref[pp].astype(BF16), qcat, preferred_element_type=F32)
        cnt = jnp.zeros((N_BLOCKS, 2 * blk), F32)
        for m in range(N_BLOCKS):
            gm = gate[m:m + 1, :]
            beats = (gm > gate) | ((gm == gate) & (blk_i > m))
            cnt = cnt + jnp.where(beats & (qblock > m), 1.0, 0.0)
        return jnp.where((blk_i < qblock) & (cnt < float(MOBA_TOPK)), 1.0, 0.0)

    qblock_a = j
    qblock_b = N_BLOCKS - 1 - j
    sel_a = [prepare(qa_ref, pp, 0, qblock_a) for pp in pairs]
    sel_b = [prepare(qb_ref, pp, 1, qblock_b) for pp in pairs]

    n_mid = N_BLOCKS - 1
    slots = [(0, 0, qblock_a, True)]
    mids = []
    for s in range(1, n_mid + 1):
        is_a = s <= j
        slots.append((s, jnp.where(is_a, 0, 1), jnp.where(is_a, s - 1, s - 1 - j), False))
        mids.append((is_a, slots[-1][2]))
    slots.append((n_mid + 1, 1, qblock_b, True))

    for pp in pairs:
        for s, which, kb, _ in slots:
            t_ref[pp, s] = jnp.dot(kaug_ref[pp, kb], qaug_ref[pp, which], preferred_element_type=F32)
    for pp in pairs:
        for s, _, _, own in slots:
            t = t_ref[pp, s]
            if own:
                t = jnp.where(causal, t, NEG_INF)
            m_loc = jnp.max(t, axis=0, keepdims=True)
            p_ref[pp, s] = jnp.exp((t - m_loc).astype(BF16))
            mloc_ref[pp, s:s + 1, :] = m_loc
    for pp in pairs:
        for s, _, kb, _ in slots:
            pv_ref[pp, s, 0] = jnp.dot(vt_ref[pp, kb, 0], p_ref[pp, s, :, 0:blk], preferred_element_type=F32)
            pv_ref[pp, s, 1] = jnp.dot(vt_ref[pp, kb, 1], p_ref[pp, s, :, blk:2 * blk],
                                       preferred_element_type=F32)

    def combine(o_ref, pp, own_slot, sel, qblock, mine):
        neg = jnp.full((1, 2 * blk), -1e30, F32)
        pieces = [(own_slot, mloc_ref[pp, own_slot:own_slot + 1, :])]
        for s, (is_a, kb) in enumerate(mids, start=1):
            selrow = jnp.sum(jnp.where(blk_i == kb, sel, 0.0), axis=0, keepdims=True)
            belongs = jnp.where(is_a, 1.0, 0.0) if mine else jnp.where(is_a, 0.0, 1.0)
            used = selrow * belongs > 0.5
            shift = slope_rows[pp] * ((kb - qblock) * blk).astype(F32)
            pieces.append((s, jnp.where(used, mloc_ref[pp, s:s + 1, :] + shift, neg)))
        m_all = pieces[0][1]
        for _, m_s in pieces[1:]:
            m_all = jnp.maximum(m_all, m_s)
        acc = [jnp.zeros((PV_ROWS, blk), F32), jnp.zeros((PV_ROWS, blk), F32)]
        for s, m_s in pieces:
            w = jnp.exp(m_s - m_all)
            for hh in range(2):
                acc[hh] = acc[hh] + pv_ref[pp, s, hh] * w[:, hh * blk:(hh + 1) * blk]
        o_t = jnp.concatenate([a[0:HEAD_DIM, :] / a[HEAD_DIM:HEAD_DIM + 1, :] for a in acc], axis=0)
        o_ref[0, :, lanes[pp]] = o_t.T.astype(BF16)

    for pp in pairs:
        combine(oa_ref, pp, 0, sel_a[pp], qblock_a, True)
        combine(ob_ref, pp, n_mid + 1, sel_b[pp], qblock_b, False)


def _attn_call(slopes, q, k, v):
    half = N_BLOCKS // 2
    assert 2 * HEAD_DIM == V7X_LANES
    width = 2 * HEAD_DIM * PAIRS
    o_a, o_b = pl.pallas_call(
        _attn_kernel,
        grid_spec=pltpu.PrefetchScalarGridSpec(
            num_scalar_prefetch=1,
            grid=(BATCH, N_HEADS // (2 * PAIRS), half),
            in_specs=[
                pl.BlockSpec((1, MOBA_BLOCK, width), lambda b, h, j, sl: (b, j, h)),
                pl.BlockSpec((1, MOBA_BLOCK, width), lambda b, h, j, sl: (b, N_BLOCKS - 1 - j, h)),
                pl.BlockSpec((1, SEQ, width), lambda b, h, j, sl: (b, 0, h)),
                pl.BlockSpec((1, SEQ, width), lambda b, h, j, sl: (b, 0, h)),
            ],
            out_specs=[pl.BlockSpec((1, MOBA_BLOCK, width), lambda b, h, j, sl: (b, j, h)),
                       pl.BlockSpec((1, MOBA_BLOCK, width), lambda b, h, j, sl: (b, half - 1 - j, h))],
            scratch_shapes=[
                pltpu.VMEM((PAIRS, N_BLOCKS, MOBA_BLOCK, 4 * HEAD_DIM), BF16),
                pltpu.VMEM((PAIRS, N_BLOCKS, 2, PV_ROWS, MOBA_BLOCK), BF16),
                pltpu.VMEM((PAIRS, N_BLOCKS, 2 * HEAD_DIM), F32),
                pltpu.VMEM((PAIRS, 2, 4 * HEAD_DIM, 2 * MOBA_BLOCK), BF16),
                pltpu.VMEM((PAIRS, N_BLOCKS + 1, 2, PV_ROWS, MOBA_BLOCK), F32),
                pltpu.VMEM((PAIRS, 16, 2 * MOBA_BLOCK), F32),
                pltpu.VMEM((PAIRS, N_BLOCKS + 1, MOBA_BLOCK, 2 * MOBA_BLOCK), F32),
                pltpu.VMEM((PAIRS, N_BLOCKS + 1, MOBA_BLOCK, 2 * MOBA_BLOCK), BF16),
            ],
        ),
        out_shape=[jax.ShapeDtypeStruct((BATCH, SEQ // 2, ATTN_WIDTH), BF16)] * 2,
        compiler_params=pltpu.CompilerParams(
            dimension_semantics=("arbitrary", "arbitrary", "arbitrary"), vmem_limit_bytes=VMEM_LIMIT),
        name="moba_attn",
    )(slopes, q, q, k, v)
    return o_a, o_b


def _route(logits):
    row8 = lax.broadcasted_iota(I32, (8, TM), 0).astype(F32)
    gl = jnp.where(row8 < float(N_GROUPS), logits[0:8, :], NEG_INF)
    gexp = jnp.exp(gl - jnp.max(gl, axis=0, keepdims=True))
    gprob = gexp / jnp.sum(gexp, axis=0, keepdims=True)
    ptop = jnp.max(gprob, axis=0, keepdims=True)
    gtop = jnp.min(jnp.where(gprob == ptop, row8, 8.0), axis=0, keepdims=True)
    el = logits[8:ROUTER_ROWS, :]
    eg = jnp.where(gtop == 0.0, el[0:8, :],
                   jnp.where(gtop == 1.0, el[8:16, :], jnp.where(gtop == 2.0, el[16:24, :], el[24:32, :])))
    m1 = jnp.max(eg, axis=0, keepdims=True)
    i1 = jnp.min(jnp.where(eg == m1, row8, 8.0), axis=0, keepdims=True)
    eg2 = jnp.where(row8 == i1, NEG_INF, eg)
    m2 = jnp.max(eg2, axis=0, keepdims=True)
    i2 = jnp.min(jnp.where(eg2 == m2, row8, 8.0), axis=0, keepdims=True)
    t2 = jnp.exp(m2 - m1)
    gate1 = ptop * (1.0 / (1.0 + t2))
    gate2 = ptop * (t2 / (1.0 + t2))
    erow = lax.broadcasted_iota(I32, (N_EXPERTS, TM), 0).astype(F32)
    oh1 = jnp.where(erow == gtop * float(EXPERTS_PER_GROUP) + i1, 1.0, 0.0)
    oh2 = jnp.where(erow == gtop * float(EXPERTS_PER_GROUP) + i2, 1.0, 0.0)
    return gate1, gate2, oh1, oh2


def _merge_kernel(oa_ref, ob_ref, za_ref, sgb_ref, x_ref, woa_f32_ref, wo_f32_ref, g_ref, b_ref,
                  wr_hi_ref, wr_lo_ref, x1_ref, xs_ref, rf_ref, mt_ref, woa_ref, wo_ref):
    i = pl.program_id(0)

    @pl.when(i == 0)
    def _():
        woa_ref[...] = woa_f32_ref[...].astype(BF16)
        wo_ref[...] = wo_f32_ref[...].astype(BF16)

    subs = range(MERGE_SUB)
    rows = [slice(s * TM, (s + 1) * TM) for s in subs]
    steps_per_batch = N_BLOCKS // MERGE_SUB
    in_oa = lax.rem(i, steps_per_batch) < steps_per_batch // 2
    o = [jnp.where(in_oa, oa_ref[0, r, :], ob_ref[0, r, :]) for r in rows]
    y_attn = [jnp.dot(o[s], woa_ref[...], preferred_element_type=F32) for s in subs]
    y = [(za_ref[rows[s], :].astype(F32) + sgb_ref[rows[s], :].astype(F32) * y_attn[s]).astype(BF16)
         for s in subs]
    mix = [jnp.dot(y[s], wo_ref[...], preferred_element_type=F32) for s in subs]
    x1 = []
    for s in subs:
        h = ALPHA * x_ref[rows[s], :] + mix[s]
        mu = jnp.mean(h, axis=-1, keepdims=True)
        hc = h - mu
        var = jnp.mean(hc * hc, axis=-1, keepdims=True)
        x1.append(hc * lax.rsqrt(var + LN_EPS) * g_ref[...] + b_ref[...])
        x1_ref[rows[s], :] = x1[s]

    xh = [x1[s].astype(BF16) for s in subs]
    xl = [(x1[s] - xh[s].astype(F32)).astype(BF16) for s in subs]
    wh = wr_hi_ref[...]
    logits = [(jnp.dot(xh[s], wh, preferred_element_type=F32)
               + jnp.dot(xl[s], wh, preferred_element_type=F32)
               + jnp.dot(xh[s], wr_lo_ref[...], preferred_element_type=F32)).T for s in subs]
    routes = [_route(logits[s]) for s in subs]

    ta = lax.broadcasted_iota(I32, (TM, TM), 0)
    tb = lax.broadcasted_iota(I32, (TM, TM), 1)
    upper = jnp.where(ta < tb, 1.0, 0.0).astype(BF16)
    ea = lax.broadcasted_iota(I32, (N_EXPERTS, N_EXPERTS), 0)
    eb = lax.broadcasted_iota(I32, (N_EXPERTS, N_EXPERTS), 1)
    lower = jnp.where(eb < ea, 1.0, 0.0).astype(BF16)
    lrow = lax.broadcasted_iota(I32, (LOCAL_ROWS, TM), 0).astype(F32)
    zero = jnp.zeros((1, TM), F32)
    cum = [jnp.dot((routes[s][2] + routes[s][3]).astype(BF16), upper, preferred_element_type=F32) for s in subs]
    perm = []
    for s in subs:
        gate1, gate2, oh1, oh2 = routes[s]
        n_e = jnp.sum(oh1 + oh2, axis=1, keepdims=True)
        m_rep = jnp.broadcast_to(jnp.floor((n_e + float(GRAN - 1)) * (1.0 / GRAN)), (N_EXPERTS, V7X_LANES))
        run_start = jnp.dot(lower, m_rep.astype(BF16), preferred_element_type=F32)
        tot = cum[s] + float(GRAN) * run_start[:, 0:1]
        lp1 = jnp.sum(oh1 * tot, axis=0, keepdims=True)
        lp2 = jnp.sum(oh2 * tot, axis=0, keepdims=True)
        perm.append(jnp.where((lrow == lp1) | (lrow == lp2), 1.0, 0.0).astype(BF16))
        rf_ref[:, rows[s]] = jnp.concatenate([gate1, gate2, lp1, lp2, zero, zero, zero, zero], axis=0)
        mt_ref[s * N_EXPERTS:(s + 1) * N_EXPERTS, :] = m_rep
    for s in subs:
        xs_ref[s * LOCAL_ROWS:(s + 1) * LOCAL_ROWS, :] = jnp.dot(
            perm[s], xh[s], preferred_element_type=F32).astype(BF16)


def _merge_call(o_a, o_b, za, sgb, x, woa, wo, g, b, wr_hi, wr_lo):
    tm = MERGE_SUB * TM
    tok = lambda c: pl.BlockSpec((tm, c), lambda i: (i, 0))
    full = lambda shape: pl.BlockSpec(shape, lambda i: (0,) * len(shape))
    per_batch = SEQ // tm
    half = per_batch // 2
    o_a_spec = pl.BlockSpec((1, tm, ATTN_WIDTH), lambda i: (i // per_batch, jnp.minimum(i % per_batch, half - 1), 0))
    o_b_spec = pl.BlockSpec((1, tm, ATTN_WIDTH), lambda i: (i // per_batch, jnp.maximum(i % per_batch - half, 0), 0))
    return pl.pallas_call(
        _merge_kernel,
        grid=(TOKENS // tm,),
        in_specs=[o_a_spec, o_b_spec, tok(D_MODEL), tok(D_MODEL), tok(D_MODEL),
                  full((ATTN_WIDTH, D_MODEL)), full((D_MODEL, D_MODEL)), full((1, D_MODEL)),
                  full((1, D_MODEL)), full((D_MODEL, V7X_LANES)), full((D_MODEL, V7X_LANES))],
        out_specs=[tok(D_MODEL), pl.BlockSpec((MERGE_SUB * LOCAL_ROWS, D_MODEL), lambda i: (i, 0)),
                   pl.BlockSpec((8, tm), lambda i: (0, i)),
                   pl.BlockSpec((MERGE_SUB * N_EXPERTS, V7X_LANES), lambda i: (i, 0))],
        out_shape=[jax.ShapeDtypeStruct((TOKENS, D_MODEL), F32),
                   jax.ShapeDtypeStruct((N_TOK_TILES * LOCAL_ROWS, D_MODEL), BF16),
                   jax.ShapeDtypeStruct((8, TOKENS), F32),
                   jax.ShapeDtypeStruct((N_TOK_TILES * N_EXPERTS, V7X_LANES), F32)],
        scratch_shapes=[pltpu.VMEM((ATTN_WIDTH, D_MODEL), BF16), pltpu.VMEM((D_MODEL, D_MODEL), BF16)],
        compiler_params=pltpu.CompilerParams(
            dimension_semantics=("arbitrary",), vmem_limit_bytes=VMEM_LIMIT),
        name="merge_ln1_route",
    )(o_a, o_b, za, sgb, x, woa, wo, g, b, wr_hi, wr_lo)


def _granule_copy(src_ref, src_gran, dst_ref, dst_gran, sem):
    src = pl.multiple_of(src_gran * GRAN, GRAN)
    dst = pl.multiple_of(dst_gran * GRAN, GRAN)
    return pltpu.make_async_copy(src_ref.at[pl.ds(src, GRAN), :], dst_ref.at[pl.ds(dst, GRAN), :], sem)


def _expert_kernel(te_ref, nt_ref, gsrc_ref, gdst_ref, ug_ref, wpar_ref, wnext_ref,
                   xs_ref, wg_hbm_ref, wu_hbm_ref, wd_hbm_ref,
                   ys_ref, xbuf, ybuf, zbuf, wg_ref, wu_ref, wd_ref, wg_stage, wu_stage, wd_stage,
                   in_sem, out_sem, zero_sem, w_sem):
    j = pl.program_id(0)
    n_tiles = nt_ref[0]
    slot = lax.rem(j, NBUF)
    prev_slot = lax.rem(j + NBUF - 1, NBUF)

    def tile_gather(step, s):
        for g in range(STEP_GRANS):
            _granule_copy(xs_ref, gsrc_ref[step * STEP_GRANS + g], xbuf.at[s], g,
                          in_sem.at[s]).start(priority=g % 2)

    def prev_scatter():
        for g in range(STEP_GRANS):
            _granule_copy(ybuf.at[prev_slot], g, ys_ref, gdst_ref[j * STEP_GRANS + g],
                          out_sem.at[prev_slot]).start(priority=g % 2)

    @pl.when(j == 0)
    def _():
        tile_gather(0, 0)
        ybuf[NBUF - 1] = jnp.zeros((STEP_ROWS, D_MODEL), BF16)
        zbuf[...] = jnp.zeros((ZERO_ROWS, D_MODEL), BF16)
        for part in range(NBUF):
            spare = pltpu.make_async_copy(
                ybuf.at[NBUF - 1], ys_ref.at[pl.ds((SPARE_GRAN + part * STEP_GRANS) * GRAN, STEP_ROWS), :],
                out_sem.at[NBUF - 1])
            spare.start()
            spare.wait()

        for ahead in range(1, NBUF - 1):
            tile_gather(ahead, ahead)

    def zero_fill(t, act):
        count = LOCAL_GRANS - ug_ref[t]
        for bit in range(ZERO_BITS):
            size = GRAN << bit

            @pl.when(((count >> bit) & 1) == 1)
            def _():
                end = LOCAL_GRANS - (count & ((1 << bit) - 1))
                first = pl.multiple_of((t * LOCAL_GRANS + end) * GRAN - size, GRAN)
                act(pltpu.make_async_copy(zbuf.at[pl.ds(0, size), :], ys_ref.at[pl.ds(first, size), :], zero_sem))

    @pl.when(jnp.logical_and(j >= 1, j <= N_TOK_TILES))
    def _():
        zero_fill(j - 1, lambda cp: cp.wait())

    @pl.when(j < N_TOK_TILES)
    def _():
        zero_fill(j, lambda cp: cp.start())

    def gather_wait():
        pltpu.make_async_copy(xs_ref.at[pl.ds(0, STEP_ROWS), :], xbuf.at[slot], in_sem.at[slot]).wait()

    @pl.when(jnp.logical_and(j >= NBUF - 1, j - NBUF < n_tiles))
    def _():
        pltpu.make_async_copy(ybuf.at[slot], ys_ref.at[pl.ds(0, STEP_ROWS), :], out_sem.at[slot]).wait()

    @pl.when(jnp.logical_and(j >= n_tiles, j < n_tiles + NBUF - 1))
    def _():
        gather_wait()

    @pl.when(j == n_tiles)
    def _():
        prev_scatter()

    def weight_copies(expert, s):
        return [pltpu.make_async_copy(hbm.at[expert], stage.at[s], w_sem.at[s])
                for hbm, stage in ((wg_hbm_ref, wg_stage), (wu_hbm_ref, wu_stage), (wd_hbm_ref, wd_stage))]

    @pl.when(j == 0)
    def _():
        for cp in weight_copies(te_ref[0], 0):
            cp.start()

    @pl.when(jnp.logical_and(j < n_tiles, jnp.logical_or(j == 0, te_ref[j] != te_ref[jnp.maximum(j - 1, 0)])))
    def _():
        s = wpar_ref[j]
        for cp in weight_copies(te_ref[j], s):
            cp.wait()
        wg_ref[...] = wg_stage[s].astype(BF16)
        wu_ref[...] = wu_stage[s].astype(BF16)
        wd_ref[...] = wd_stage[s].astype(BF16)
        nxt = wnext_ref[te_ref[j]]

        @pl.when(nxt >= 0)
        def _():
            for cp in weight_copies(nxt, 1 - s):
                cp.start()

    @pl.when(j < n_tiles)
    def _():
        gather_wait()
        chains = range(0, TE, CHAIN_ROWS)
        xb = [xbuf[slot, r:r + CHAIN_ROWS, :] for r in chains]
        hg = [jnp.dot(x, wg_ref[...], preferred_element_type=F32) for x in xb]
        hu = [jnp.dot(x, wu_ref[...], preferred_element_type=F32) for x in xb]
        tile_gather(j + NBUF - 1, prev_slot)
        prev_scatter()
        h = [(a * _sigmoid(a) * b).astype(BF16) for a, b in zip(hg, hu)]
        for hc, r in zip(h, chains):
            ybuf[slot, r:r + CHAIN_ROWS, :] = jnp.dot(hc, wd_ref[...], preferred_element_type=F32).astype(BF16)


def _expert_call(tile_expert, n_steps, gsrc, gdst, used_grans, w_parity, w_next, xs, wg, wu, wd):
    hbm = pl.BlockSpec(memory_space=pl.ANY)
    return pl.pallas_call(
        _expert_kernel,
        grid_spec=pltpu.PrefetchScalarGridSpec(
            num_scalar_prefetch=7,
            grid=(MAX_STEPS + NBUF,),
            in_specs=[hbm, hbm, hbm, hbm],
            out_specs=hbm,
            scratch_shapes=[pltpu.VMEM((NBUF, STEP_ROWS, D_MODEL), BF16),
                            pltpu.VMEM((NBUF, STEP_ROWS, D_MODEL), BF16),
                            pltpu.VMEM((ZERO_ROWS, D_MODEL), BF16),
                            pltpu.VMEM((D_MODEL, D_EXPERT), BF16), pltpu.VMEM((D_MODEL, D_EXPERT), BF16),
                            pltpu.VMEM((D_EXPERT, D_MODEL), BF16),
                            pltpu.VMEM((2, D_MODEL, D_EXPERT), F32), pltpu.VMEM((2, D_MODEL, D_EXPERT), F32),
                            pltpu.VMEM((2, D_EXPERT, D_MODEL), F32),
                            pltpu.SemaphoreType.DMA((NBUF,)), pltpu.SemaphoreType.DMA((NBUF,)),
                            pltpu.SemaphoreType.DMA, pltpu.SemaphoreType.DMA((2,))],
        ),
        out_shape=jax.ShapeDtypeStruct(((SPARE_GRAN + NBUF * STEP_GRANS) * GRAN, D_MODEL), BF16),
        compiler_params=pltpu.CompilerParams(
            dimension_semantics=("arbitrary",), vmem_limit_bytes=VMEM_LIMIT),
        name="experts",
    )(tile_expert, n_steps, gsrc, gdst, used_grans, w_parity, w_next, xs, wg, wu, wd)


def _combine_kernel(ys_ref, x1_ref, rf_ref, g_ref, b_ref, out_ref):
    subs = range(COMBINE_SUB)
    col = lax.broadcasted_iota(I32, (TM, LOCAL_ROWS), 1).astype(F32)
    route = [rf_ref[:, s * TM:(s + 1) * TM].T for s in subs]
    unsort = [(jnp.where(col == r[:, 2:3], r[:, 0:1], 0.0)
               + jnp.where(col == r[:, 3:4], r[:, 1:2], 0.0)).astype(BF16) for r in route]
    ffn = [jnp.dot(unsort[s], ys_ref[s * LOCAL_ROWS:(s + 1) * LOCAL_ROWS, :], preferred_element_type=F32)
           for s in subs]
    for s in subs:
        h = ALPHA * x1_ref[s * TM:(s + 1) * TM, :] + ffn[s]
        mu = jnp.mean(h, axis=-1, keepdims=True)
        hc = h - mu
        var = jnp.mean(hc * hc, axis=-1, keepdims=True)
        out_ref[s * TM:(s + 1) * TM, :] = hc * lax.rsqrt(var + LN_EPS) * g_ref[...] + b_ref[...]


def _combine_call(ys, x1, rf, g, b):
    tm = COMBINE_SUB * TM
    return pl.pallas_call(
        _combine_kernel,
        grid=(TOKENS // tm,),
        in_specs=[pl.BlockSpec((COMBINE_SUB * LOCAL_ROWS, D_MODEL), lambda i: (i, 0)),
                  pl.BlockSpec((tm, D_MODEL), lambda i: (i, 0)),
                  pl.BlockSpec((8, tm), lambda i: (0, i)),
                  pl.BlockSpec((1, D_MODEL), lambda i: (0, 0)),
                  pl.BlockSpec((1, D_MODEL), lambda i: (0, 0))],
        out_specs=pl.BlockSpec((tm, D_MODEL), lambda i: (i, 0)),
        out_shape=jax.ShapeDtypeStruct((TOKENS, D_MODEL), F32),
        compiler_params=pltpu.CompilerParams(
            dimension_semantics=("arbitrary",), vmem_limit_bytes=VMEM_LIMIT),
        name="combine_ln2",
    )(ys, x1, rf, g, b)


def _router_cols(w_router_group, w_router_expert):
    w = jnp.concatenate([w_router_group, jnp.zeros((D_MODEL, 4), F32), w_router_expert,
                         jnp.zeros((D_MODEL, V7X_LANES - ROUTER_ROWS), F32)], axis=1)
    hi = w.astype(BF16)
    lo = (w - hi.astype(F32)).astype(BF16)
    return hi, lo


def _layer(x, w_in, conv_w, w_out_conv, w_out_attn, w_o, ln1_g, ln1_b,
           w_router_group, w_router_expert, w_gate, w_up, w_down, ln2_g, ln2_b):
    slopes = jnp.asarray([2.0 ** (-8.0 * (h + 1) / N_HEADS) for h in range(N_HEADS)], F32)
    q, k, v, za, sgb = _proj_call(x, w_in, conv_w, w_out_conv)
    o_a, o_b = _attn_call(slopes, q, k, v)

    wr_hi, wr_lo = _router_cols(w_router_group, w_router_expert)
    x1, xs, rf, mt = _merge_call(
        o_a, o_b, za.reshape(TOKENS, D_MODEL), sgb.reshape(TOKENS, D_MODEL),
        x.reshape(TOKENS, D_MODEL), w_out_attn, w_o,
        ln1_g.reshape(1, D_MODEL), ln1_b.reshape(1, D_MODEL), wr_hi, wr_lo)

    grans = mt.reshape(N_TOK_TILES, N_EXPERTS, V7X_LANES)[:, :, 0].astype(I32)
    local_start = jnp.cumsum(grans, axis=1) - grans
    grans_t = grans.T
    tiles_e = (jnp.sum(grans_t, axis=1) + TILE_GRANS - 1) // TILE_GRANS
    tile_end = jnp.cumsum(tiles_e)
    n_steps = tile_end[-1].reshape(1)
    all_tiles = MAX_STEPS + NBUF
    tile_ids = jnp.arange(all_tiles, dtype=I32)
    tile_expert = jnp.minimum(
        jnp.sum((tile_ids[:, None] >= tile_end[None, :]).astype(I32), axis=1), N_EXPERTS - 1)
    run_slot = TILE_GRANS * (tile_end - tiles_e)[:, None] + jnp.cumsum(grans_t, axis=1) - grans_t
    run_src = jnp.arange(N_TOK_TILES, dtype=I32)[None, :] * LOCAL_GRANS + local_start.T
    pick = (tile_expert[:all_tiles, None] == jnp.arange(N_EXPERTS, dtype=I32)[None, :])[:, :, None]
    t_slot = jnp.sum(jnp.where(pick, run_slot[None], 0), axis=1)
    t_len = jnp.sum(jnp.where(pick, grans_t[None], 0), axis=1)
    t_src = jnp.sum(jnp.where(pick, run_src[None], 0), axis=1)
    slots = jnp.arange(all_tiles * TILE_GRANS, dtype=I32).reshape(all_tiles, TILE_GRANS)
    k = slots[:, :, None] - t_slot[:, None, :]
    hit = (k >= 0) & (k < t_len[:, None, :])
    gran = jnp.sum(jnp.where(hit, t_src[:, None, :] + k, 0), axis=2).reshape(-1)
    filled = (jnp.sum(hit.astype(I32), axis=2) > 0).reshape(-1)
    slots = slots.reshape(-1)
    gsrc = jnp.where(filled, gran, 0)
    gdst = jnp.where(filled, gran, SPARE_GRAN + slots % (NBUF * STEP_GRANS))
    gdst = jnp.concatenate([SPARE_GRAN + (NBUF - 1) * STEP_GRANS + jnp.arange(STEP_GRANS, dtype=I32), gdst])

    starts = jnp.concatenate([jnp.ones((1,), I32), (tile_expert[1:] != tile_expert[:-1]).astype(I32)])
    w_parity = (jnp.cumsum(starts) - 1) % 2
    ids = jnp.arange(N_EXPERTS, dtype=I32)
    later = jnp.where((tiles_e > 0)[None, :] & (ids[None, :] > ids[:, None]), ids[None, :], N_EXPERTS)
    w_next = jnp.min(later, axis=1)
    w_next = jnp.where(w_next == N_EXPERTS, -1, w_next)

    ys = _expert_call(tile_expert, n_steps, gsrc, gdst, jnp.sum(grans, axis=1), w_parity, w_next, xs,
                      w_gate, w_up, w_down)
    out = _combine_call(ys, x1, rf, ln2_g.reshape(1, D_MODEL), ln2_b.reshape(1, D_MODEL))
    return out.reshape(BATCH, SEQ, D_MODEL)


def kernel(x, w_in, conv_w, w_out_conv, w_out_attn, w_o, ln1_g, ln1_b, w_router_group, w_router_expert, w_gate, w_up, w_down, ln2_g, ln2_b):
    depth = w_in.shape[0]
    for l in range(depth):
        x = _layer(x, w_in[l], conv_w[l], w_out_conv[l], w_out_attn[l], w_o[l], ln1_g[l], ln1_b[l],
                   w_router_group[l], w_router_expert[l], w_gate[l], w_up[l], w_down[l], ln2_g[l], ln2_b[l])
    return x
```

```python
import jax
import jax.numpy as jnp
from jax import lax
from jax.experimental import pallas as pl
from jax.experimental.pallas import tpu as pltpu

F32 = jnp.float32
BF16 = jnp.bfloat16
I32 = jnp.int32

V7X_VMEM_BYTES = 64 * 1024 * 1024
V7X_LANES = 128
V7X_MXU_DIM = 256

D_MODEL = 1024
BATCH = 8
SEQ = 2048
TOKENS = BATCH * SEQ
CONV_WIDTH = 512
N_HEADS = 8
HEAD_DIM = 64
ATTN_WIDTH = N_HEADS * HEAD_DIM
MOBA_BLOCK = 256
N_BLOCKS = SEQ // MOBA_BLOCK
MOBA_TOPK = 3
N_GROUPS = 4
EXPERTS_PER_GROUP = 8
N_EXPERTS = N_GROUPS * EXPERTS_PER_GROUP
D_EXPERT = 256
LN_EPS = 1e-5
ALPHA = 2.0 ** 0.25
IN_COLS = 3 * CONV_WIDTH + 3 * ATTN_WIDTH + 2 * D_MODEL

TM = 256
TM_PROJ = 1024
TE = 512
CHAIN_ROWS = 128
PV_ROWS = HEAD_DIM + 16
GRAN = 16
TILE_GRANS = TE // GRAN
N_TOK_TILES = TOKENS // TM
LOCAL_ROWS = -(-(2 * TM + N_EXPERTS * (GRAN - 1)) // V7X_MXU_DIM) * V7X_MXU_DIM
LOCAL_GRANS = LOCAL_ROWS // GRAN
SPARE_GRAN = N_TOK_TILES * LOCAL_GRANS
ZERO_BITS = (LOCAL_GRANS - 2 * TM // GRAN).bit_length()
ZERO_ROWS = GRAN << (ZERO_BITS - 1)
MAX_TILES = (2 * TOKENS + N_TOK_TILES * N_EXPERTS * (GRAN - 1)) // TE + N_EXPERTS
PAIRS = 4
MERGE_SUB = 2
COMBINE_SUB = 4
STEP_ROWS = TE
STEP_GRANS = TILE_GRANS
MAX_STEPS = MAX_TILES
NBUF = 5
ROUTER_ROWS = 40
VMEM_LIMIT = V7X_VMEM_BYTES - 8 * 1024 * 1024
NEG_INF = float("-inf")


def _sigmoid(z):
    return 1.0 / (1.0 + jnp.exp(-z))


def _proj_kernel(x_ref, w_in_hbm_ref, convw_ref, woc_f32_ref, q_ref, k_ref, v_ref, za_ref, sgb_ref,
                 ubuf, w_in_ref, woc_ref, stage, stage_sem):
    s = pl.program_id(1)
    tm = TM_PROJ

    @pl.when((pl.program_id(0) == 0) & (s == 0))
    def _():
        def chunk(c):
            return pltpu.make_async_copy(w_in_hbm_ref.at[:, c * CONV_WIDTH:(c + 1) * CONV_WIDTH],
                                         stage.at[c % 2], stage_sem.at[c % 2])

        n_chunks = IN_COLS // CONV_WIDTH
        chunk(0).start()
        for c in range(n_chunks):
            if c + 1 < n_chunks:
                chunk(c + 1).start()
            chunk(c).wait()
            w_in_ref[:, c * CONV_WIDTH:(c + 1) * CONV_WIDTH] = stage[c % 2].astype(BF16)
        woc_ref[...] = woc_f32_ref[...].astype(BF16)

    xb = x_ref[0].astype(BF16)

    def proj(c0, c1):
        return jnp.dot(xb, w_in_ref[:, c0:c1], preferred_element_type=F32)

    c_b = proj(0, CONV_WIDTH)
    u = proj(CONV_WIDTH, 2 * CONV_WIDTH) * proj(2 * CONV_WIDTH, 3 * CONV_WIDTH)

    @pl.when(s == 0)
    def _():
        ubuf[0:8, :] = jnp.zeros((8, CONV_WIDTH), F32)

    ubuf[8:8 + tm, :] = u
    w = convw_ref[...]
    conv = w[2:3, :] * u + w[1:2, :] * ubuf[7:7 + tm, :] + w[0:1, :] * ubuf[6:6 + tm, :]
    ubuf[0:8, :] = u[tm - 8:tm, :]
    hc = (c_b * conv).astype(BF16)
    y_conv = jnp.dot(hc, woc_ref[...], preferred_element_type=F32)

    o = 3 * CONV_WIDTH
    q_ref[0] = (proj(o, o + ATTN_WIDTH) * (HEAD_DIM ** -0.5)).astype(BF16)
    k_ref[0] = proj(o + ATTN_WIDTH, o + 2 * ATTN_WIDTH).astype(BF16)
    v_ref[0] = proj(o + 2 * ATTN_WIDTH, o + 3 * ATTN_WIDTH).astype(BF16)
    o += 3 * ATTN_WIDTH
    za_ref[0] = (_sigmoid(proj(o, o + D_MODEL)) * y_conv).astype(BF16)
    sgb_ref[0] = _sigmoid(proj(o + D_MODEL, o + 2 * D_MODEL)).astype(BF16)


def _proj_call(x, w_in, conv_w, w_out_conv):
    tok_spec = lambda c: pl.BlockSpec((1, TM_PROJ, c), lambda b, s: (b, s, 0))
    full = lambda shape: pl.BlockSpec(shape, lambda b, s: (0,) * len(shape))
    once = lambda shape: pl.BlockSpec(shape, lambda b, s: (0,) * len(shape), pipeline_mode=pl.Buffered(1))
    return pl.pallas_call(
        _proj_kernel,
        grid=(BATCH, SEQ // TM_PROJ),
        in_specs=[tok_spec(D_MODEL), pl.BlockSpec(memory_space=pl.ANY), full((3, CONV_WIDTH)),
                  once((CONV_WIDTH, D_MODEL))],
        out_specs=[tok_spec(ATTN_WIDTH), tok_spec(ATTN_WIDTH), tok_spec(ATTN_WIDTH),
                   tok_spec(D_MODEL), tok_spec(D_MODEL)],
        out_shape=[jax.ShapeDtypeStruct((BATCH, SEQ, ATTN_WIDTH), BF16)] * 3
        + [jax.ShapeDtypeStruct((BATCH, SEQ, D_MODEL), BF16)] * 2,
        scratch_shapes=[pltpu.VMEM((TM_PROJ + 8, CONV_WIDTH), F32), pltpu.VMEM((D_MODEL, IN_COLS), BF16),
                        pltpu.VMEM((CONV_WIDTH, D_MODEL), BF16),
                        pltpu.VMEM((2, D_MODEL, CONV_WIDTH), F32), pltpu.SemaphoreType.DMA((2,))],
        compiler_params=pltpu.CompilerParams(
            dimension_semantics=("arbitrary", "arbitrary"), vmem_limit_bytes=VMEM_LIMIT),
        name="proj",
    )(x, w_in, conv_w, w_out_conv)


def _attn_kernel(slopes_ref, qa_ref, qb_ref, k_ref, v_ref, oa_ref, ob_ref,
                 kaug_ref, vt_ref, kmean_ref, qaug_ref, pv_ref, mloc_ref, t_ref, p_ref):
    hq = pl.program_id(1)
    j = pl.program_id(2)
    blk = MOBA_BLOCK
    pairs = range(PAIRS)
    pair_w = 2 * HEAD_DIM
    lanes = [slice(pair_w * pp, pair_w * (pp + 1)) for pp in pairs]

    @pl.when(j == 0)
    def _():
        klane = lax.broadcasted_iota(I32, (blk, pair_w), 1)
        koff = lax.broadcasted_iota(I32, (blk, pair_w), 0).astype(F32)
        k_extra = jnp.where(klane == 0, koff, jnp.where(klane == 1, 1.0, 0.0)).astype(BF16)
        orow = lax.broadcasted_iota(I32, (PV_ROWS - HEAD_DIM, blk), 0)
        ones_rows = jnp.where(orow == 0, 1.0, 0.0).astype(BF16)
        for pp in pairs:
            for n in range(N_BLOCKS):
                kblk = k_ref[0, n * blk:(n + 1) * blk, lanes[pp]]
                kaug_ref[pp, n, :, 0:pair_w] = kblk
                kaug_ref[pp, n, :, pair_w:2 * pair_w] = k_extra
                kmean_ref[pp, n:n + 1, :] = jnp.mean(kblk.astype(F32), axis=0, keepdims=True)
                v_t = v_ref[0, n * blk:(n + 1) * blk, lanes[pp]].astype(F32).T.astype(BF16)
                for hh in range(2):
                    vt_ref[pp, n, hh, 0:HEAD_DIM, :] = v_t[hh * HEAD_DIM:(hh + 1) * HEAD_DIM, :]
                    vt_ref[pp, n, hh, HEAD_DIM:PV_ROWS, :] = ones_rows

    lane = lax.broadcasted_iota(I32, (1, 2 * blk), 1)
    qoff_row = jnp.where(lane < blk, lane, lane - blk).astype(F32)
    feat = lax.broadcasted_iota(I32, (2 * HEAD_DIM, blk), 0)
    arow = lax.broadcasted_iota(I32, (2 * HEAD_DIM, 2 * blk), 0)
    blk_i = lax.broadcasted_iota(I32, (N_BLOCKS, 2 * blk), 0)
    key_i = lax.broadcasted_iota(I32, (blk, 2 * blk), 0)
    qry_j = lax.broadcasted_iota(I32, (blk, 2 * blk), 1)
    causal = key_i <= jnp.where(qry_j < blk, qry_j, qry_j - blk)
    slope_rows, q_extras = [], []
    for pp in pairs:
        head = 2 * (PAIRS * hq + pp)
        slope_rows.append(jnp.where(lane < blk, slopes_ref[head], slopes_ref[head + 1]))
        q_extras.append(jnp.where(arow == 0, slope_rows[pp],
                                  jnp.where(arow == 1, -slope_rows[pp] * qoff_row, 0.0)).astype(BF16))

    def prepare(q_ref, pp, slot, qblock):
        q_t = q_ref[0, :, lanes[pp]].astype(F32).T
        qcat = jnp.concatenate([jnp.where(feat < HEAD_DIM, q_t, 0.0), jnp.where(feat >= HEAD_DIM, q_t, 0.0)],
                               axis=1).astype(BF16)
        qaug_ref[pp, slot, 0:2 * HEAD_DIM, :] = qcat
        qaug_ref[pp, slot, 2 * HEAD_DIM:4 * HEAD_DIM, :] = q_extras[pp]
        gate = jnp.dot(kmean_ref[pp].astype(BF16), qcat, preferred_element_type=F32)
        cnt = jnp.zeros((N_BLOCKS, 2 * blk), F32)
        for m in range(N_BLOCKS):
            gm = gate[m:m + 1, :]
            beats = (gm > gate) | ((gm == gate) & (blk_i > m))
            cnt = cnt + jnp.where(beats & (qblock > m), 1.0, 0.0)
        return jnp.where((blk_i < qblock) & (cnt < float(MOBA_TOPK)), 1.0, 0.0)

    qblock_a = j
    qblock_b = N_BLOCKS - 1 - j
    sel_a = [prepare(qa_ref, pp, 0, qblock_a) for pp in pairs]
    sel_b = [prepare(qb_ref, pp, 1, qblock_b) for pp in pairs]

    n_mid = N_BLOCKS - 1
    slots = [(0, 0, qblock_a, True)]
    mids = []
    for s in range(1, n_mid + 1):
        is_a = s <= j
        slots.append((s, jnp.where(is_a, 0, 1), jnp.where(is_a, s - 1, s - 1 - j), False))
        mids.append((is_a, slots[-1][2]))
    slots.append((n_mid + 1, 1, qblock_b, True))

    for pp in pairs:
        for s, which, kb, _ in slots:
            t_ref[pp, s] = jnp.dot(kaug_ref[pp, kb], qaug_ref[pp, which], preferred_element_type=F32)
    for pp in pairs:
        for s, _, _, own in slots:
            t = t_ref[pp, s]
            if own:
                t = jnp.where(causal, t, NEG_INF)
            m_loc = jnp.max(t, axis=0, keepdims=True)
            p_ref[pp, s] = jnp.exp((t - m_loc).astype(BF16))
            mloc_ref[pp, s:s + 1, :] = m_loc
    for pp in pairs:
        for s, _, kb, _ in slots:
            pv_ref[pp, s, 0] = jnp.dot(vt_ref[pp, kb, 0], p_ref[pp, s, :, 0:blk], preferred_element_type=F32)
            pv_ref[pp, s, 1] = jnp.dot(vt_ref[pp, kb, 1], p_ref[pp, s, :, blk:2 * blk],
                                       preferred_element_type=F32)

    def combine(o_ref, pp, own_slot, sel, qblock, mine):
        neg = jnp.full((1, 2 * blk), -1e30, F32)
        pieces = [(own_slot, mloc_ref[pp, own_slot:own_slot + 1, :])]
        for s, (is_a, kb) in enumerate(mids, start=1):
            selrow = jnp.sum(jnp.where(blk_i == kb, sel, 0.0), axis=0, keepdims=True)
            belongs = jnp.where(is_a, 1.0, 0.0) if mine else jnp.where(is_a, 0.0, 1.0)
            used = selrow * belongs > 0.5
            shift = slope_rows[pp] * ((kb - qblock) * blk).astype(F32)
            pieces.append((s, jnp.where(used, mloc_ref[pp, s:s + 1, :] + shift, neg)))
        m_all = pieces[0][1]
        for _, m_s in pieces[1:]:
            m_all = jnp.maximum(m_all, m_s)
        acc = [jnp.zeros((PV_ROWS, blk), F32), jnp.zeros((PV_ROWS, blk), F32)]
        for s, m_s in pieces:
            w = jnp.exp(m_s - m_all)
            for hh in range(2):
                acc[hh] = acc[hh] + pv_ref[pp, s, hh] * w[:, hh * blk:(hh + 1) * blk]
        o_t = jnp.concatenate([a[0:HEAD_DIM, :] / a[HEAD_DIM:HEAD_DIM + 1, :] for a in acc], axis=0)
        o_ref[0, :, lanes[pp]] = o_t.T.astype(BF16)

    for pp in pairs:
        combine(oa_ref, pp, 0, sel_a[pp], qblock_a, True)
        combine(ob_ref, pp, n_mid + 1, sel_b[pp], qblock_b, False)


def _attn_call(slopes, q, k, v):
    half = N_BLOCKS // 2
    assert 2 * HEAD_DIM == V7X_LANES
    width = 2 * HEAD_DIM * PAIRS
    o_a, o_b = pl.pallas_call(
        _attn_kernel,
        grid_spec=pltpu.PrefetchScalarGridSpec(
            num_scalar_prefetch=1,
            grid=(BATCH, N_HEADS // (2 * PAIRS), half),
            in_specs=[
                pl.BlockSpec((1, MOBA_BLOCK, width), lambda b, h, j, sl: (b, j, h)),
                pl.BlockSpec((1, MOBA_BLOCK, width), lambda b, h, j, sl: (b, N_BLOCKS - 1 - j, h)),
                pl.BlockSpec((1, SEQ, width), lambda b, h, j, sl: (b, 0, h)),
                pl.BlockSpec((1, SEQ, width), lambda b, h, j, sl: (b, 0, h)),
            ],
            out_specs=[pl.BlockSpec((1, MOBA_BLOCK, width), lambda b, h, j, sl: (b, j, h)),
                       pl.BlockSpec((1, MOBA_BLOCK, width), lambda b, h, j, sl: (b, half - 1 - j, h))],
            scratch_shapes=[
                pltpu.VMEM((PAIRS, N_BLOCKS, MOBA_BLOCK, 4 * HEAD_DIM), BF16),
                pltpu.VMEM((PAIRS, N_BLOCKS, 2, PV_ROWS, MOBA_BLOCK), BF16),
                pltpu.VMEM((PAIRS, N_BLOCKS, 2 * HEAD_DIM), F32),
                pltpu.VMEM((PAIRS, 2, 4 * HEAD_DIM, 2 * MOBA_BLOCK), BF16),
                pltpu.VMEM((PAIRS, N_BLOCKS + 1, 2, PV_ROWS, MOBA_BLOCK), F32),
                pltpu.VMEM((PAIRS, 16, 2 * MOBA_BLOCK), F32),
                pltpu.VMEM((PAIRS, N_BLOCKS + 1, MOBA_BLOCK, 2 * MOBA_BLOCK), F32),
                pltpu.VMEM((PAIRS, N_BLOCKS + 1, MOBA_BLOCK, 2 * MOBA_BLOCK), BF16),
            ],
        ),
        out_shape=[jax.ShapeDtypeStruct((BATCH, SEQ // 2, ATTN_WIDTH), BF16)] * 2,
        compiler_params=pltpu.CompilerParams(
            dimension_semantics=("arbitrary", "arbitrary", "arbitrary"), vmem_limit_bytes=VMEM_LIMIT),
        name="moba_attn",
    )(slopes, q, q, k, v)
    return o_a, o_b


def _route(logits):
    row8 = lax.broadcasted_iota(I32, (8, TM), 0).astype(F32)
    gl = jnp.where(row8 < float(N_GROUPS), logits[0:8, :], NEG_INF)
    gexp = jnp.exp(gl - jnp.max(gl, axis=0, keepdims=True))
    gprob = gexp / jnp.sum(gexp, axis=0, keepdims=True)
    ptop = jnp.max(gprob, axis=0, keepdims=True)
    gtop = jnp.min(jnp.where(gprob == ptop, row8, 8.0), axis=0, keepdims=True)
    el = logits[8:ROUTER_ROWS, :]
    eg = jnp.where(gtop == 0.0, el[0:8, :],
                   jnp.where(gtop == 1.0, el[8:16, :], jnp.where(gtop == 2.0, el[16:24, :], el[24:32, :])))
    m1 = jnp.max(eg, axis=0, keepdims=True)
    i1 = jnp.min(jnp.where(eg == m1, row8, 8.0), axis=0, keepdims=True)
    eg2 = jnp.where(row8 == i1, NEG_INF, eg)
    m2 = jnp.max(eg2, axis=0, keepdims=True)
    i2 = jnp.min(jnp.where(eg2 == m2, row8, 8.0), axis=0, keepdims=True)
    t2 = jnp.exp(m2 - m1)
    gate1 = ptop * (1.0 / (1.0 + t2))
    gate2 = ptop * (t2 / (1.0 + t2))
    erow = lax.broadcasted_iota(I32, (N_EXPERTS, TM), 0).astype(F32)
    oh1 = jnp.where(erow == gtop * float(EXPERTS_PER_GROUP) + i1, 1.0, 0.0)
    oh2 = jnp.where(erow == gtop * float(EXPERTS_PER_GROUP) + i2, 1.0, 0.0)
    return gate1, gate2, oh1, oh2


def _merge_kernel(oa_ref, ob_ref, za_ref, sgb_ref, x_ref, woa_f32_ref, wo_f32_ref, g_ref, b_ref,
                  wr_hi_ref, wr_lo_ref, x1_ref, xs_ref, rf_ref, mt_ref, woa_ref, wo_ref):
    i = pl.program_id(0)

    @pl.when(i == 0)
    def _():
        woa_ref[...] = woa_f32_ref[...].astype(BF16)
        wo_ref[...] = wo_f32_ref[...].astype(BF16)

    subs = range(MERGE_SUB)
    rows = [slice(s * TM, (s + 1) * TM) for s in subs]
    steps_per_batch = N_BLOCKS // MERGE_SUB
    in_oa = lax.rem(i, steps_per_batch) < steps_per_batch // 2
    o = [jnp.where(in_oa, oa_ref[0, r, :], ob_ref[0, r, :]) for r in rows]
    y_attn = [jnp.dot(o[s], woa_ref[...], preferred_element_type=F32) for s in subs]
    y = [(za_ref[rows[s], :].astype(F32) + sgb_ref[rows[s], :].astype(F32) * y_attn[s]).astype(BF16)
         for s in subs]
    mix = [jnp.dot(y[s], wo_ref[...], preferred_element_type=F32) for s in subs]
    x1 = []
    for s in subs:
        h = ALPHA * x_ref[rows[s], :] + mix[s]
        mu = jnp.mean(h, axis=-1, keepdims=True)
        hc = h - mu
        var = jnp.mean(hc * hc, axis=-1, keepdims=True)
        x1.append(hc * lax.rsqrt(var + LN_EPS) * g_ref[...] + b_ref[...])
        x1_ref[rows[s], :] = x1[s]

    xh = [x1[s].astype(BF16) for s in subs]
    xl = [(x1[s] - xh[s].astype(F32)).astype(BF16) for s in subs]
    wh = wr_hi_ref[...]
    logits = [(jnp.dot(xh[s], wh, preferred_element_type=F32)
               + jnp.dot(xl[s], wh, preferred_element_type=F32)
               + jnp.dot(xh[s], wr_lo_ref[...], preferred_element_type=F32)).T for s in subs]
    routes = [_route(logits[s]) for s in subs]

    ta = lax.broadcasted_iota(I32, (TM, TM), 0)
    tb = lax.broadcasted_iota(I32, (TM, TM), 1)
    upper = jnp.where(ta < tb, 1.0, 0.0).astype(BF16)
    ea = lax.broadcasted_iota(I32, (N_EXPERTS, N_EXPERTS), 0)
    eb = lax.broadcasted_iota(I32, (N_EXPERTS, N_EXPERTS), 1)
    lower = jnp.where(eb < ea, 1.0, 0.0).astype(BF16)
    lrow = lax.broadcasted_iota(I32, (LOCAL_ROWS, TM), 0).astype(F32)
    zero = jnp.zeros((1, TM), F32)
    cum = [jnp.dot((routes[s][2] + routes[s][3]).astype(BF16), upper, preferred_element_type=F32) for s in subs]
    perm = []
    for s in subs:
        gate1, gate2, oh1, oh2 = routes[s]
        n_e = jnp.sum(oh1 + oh2, axis=1, keepdims=True)
        m_rep = jnp.broadcast_to(jnp.floor((n_e + float(GRAN - 1)) * (1.0 / GRAN)), (N_EXPERTS, V7X_LANES))
        run_start = jnp.dot(lower, m_rep.astype(BF16), preferred_element_type=F32)
        tot = cum[s] + float(GRAN) * run_start[:, 0:1]
        lp1 = jnp.sum(oh1 * tot, axis=0, keepdims=True)
        lp2 = jnp.sum(oh2 * tot, axis=0, keepdims=True)
        perm.append(jnp.where((lrow == lp1) | (lrow == lp2), 1.0, 0.0).astype(BF16))
        rf_ref[:, rows[s]] = jnp.concatenate([gate1, gate2, lp1, lp2, zero, zero, zero, zero], axis=0)
        mt_ref[s * N_EXPERTS:(s + 1) * N_EXPERTS, :] = m_rep
    for s in subs:
        xs_ref[s * LOCAL_ROWS:(s + 1) * LOCAL_ROWS, :] = jnp.dot(
            perm[s], xh[s], preferred_element_type=F32).astype(BF16)


def _merge_call(o_a, o_b, za, sgb, x, woa, wo, g, b, wr_hi, wr_lo):
    tm = MERGE_SUB * TM
    tok = lambda c: pl.BlockSpec((tm, c), lambda i: (i, 0))
    full = lambda shape: pl.BlockSpec(shape, lambda i: (0,) * len(shape))
    per_batch = SEQ // tm
    half = per_batch // 2
    o_a_spec = pl.BlockSpec((1, tm, ATTN_WIDTH), lambda i: (i // per_batch, jnp.minimum(i % per_batch, half - 1), 0))
    o_b_spec = pl.BlockSpec((1, tm, ATTN_WIDTH), lambda i: (i // per_batch, jnp.maximum(i % per_batch - half, 0), 0))
    return pl.pallas_call(
        _merge_kernel,
        grid=(TOKENS // tm,),
        in_specs=[o_a_spec, o_b_spec, tok(D_MODEL), tok(D_MODEL), tok(D_MODEL),
                  full((ATTN_WIDTH, D_MODEL)), full((D_MODEL, D_MODEL)), full((1, D_MODEL)),
                  full((1, D_MODEL)), full((D_MODEL, V7X_LANES)), full((D_MODEL, V7X_LANES))],
        out_specs=[tok(D_MODEL), pl.BlockSpec((MERGE_SUB * LOCAL_ROWS, D_MODEL), lambda i: (i, 0)),
                   pl.BlockSpec((8, tm), lambda i: (0, i)),
                   pl.BlockSpec((MERGE_SUB * N_EXPERTS, V7X_LANES), lambda i: (i, 0))],
        out_shape=[jax.ShapeDtypeStruct((TOKENS, D_MODEL), F32),
                   jax.ShapeDtypeStruct((N_TOK_TILES * LOCAL_ROWS, D_MODEL), BF16),
                   jax.ShapeDtypeStruct((8, TOKENS), F32),
                   jax.ShapeDtypeStruct((N_TOK_TILES * N_EXPERTS, V7X_LANES), F32)],
        scratch_shapes=[pltpu.VMEM((ATTN_WIDTH, D_MODEL), BF16), pltpu.VMEM((D_MODEL, D_MODEL), BF16)],
        compiler_params=pltpu.CompilerParams(
            dimension_semantics=("arbitrary",), vmem_limit_bytes=VMEM_LIMIT),
        name="merge_ln1_route",
    )(o_a, o_b, za, sgb, x, woa, wo, g, b, wr_hi, wr_lo)


def _granule_copy(src_ref, src_gran, dst_ref, dst_gran, sem):
    src = pl.multiple_of(src_gran * GRAN, GRAN)
    dst = pl.multiple_of(dst_gran * GRAN, GRAN)
    return pltpu.make_async_copy(src_ref.at[pl.ds(src, GRAN), :], dst_ref.at[pl.ds(dst, GRAN), :], sem)


def _expert_kernel(te_ref, nt_ref, gsrc_ref, gdst_ref, ug_ref, wpar_ref, wnext_ref,
                   xs_ref, wg_hbm_ref, wu_hbm_ref, wd_hbm_ref,
                   ys_ref, xbuf, ybuf, zbuf, wg_ref, wu_ref, wd_ref, wg_stage, wu_stage, wd_stage,
                   in_sem, out_sem, zero_sem, w_sem):
    j = pl.program_id(0)
    n_tiles = nt_ref[0]
    slot = lax.rem(j, NBUF)
    prev_slot = lax.rem(j + NBUF - 1, NBUF)

    def tile_gather(step, s):
        for g in range(STEP_GRANS):
            _granule_copy(xs_ref, gsrc_ref[step * STEP_GRANS + g], xbuf.at[s], g,
                          in_sem.at[s]).start(priority=g % 2)

    def prev_scatter():
        for g in range(STEP_GRANS):
            _granule_copy(ybuf.at[prev_slot], g, ys_ref, gdst_ref[j * STEP_GRANS + g],
                          out_sem.at[prev_slot]).start(priority=g % 2)

    @pl.when(j == 0)
    def _():
        tile_gather(0, 0)
        ybuf[NBUF - 1] = jnp.zeros((STEP_ROWS, D_MODEL), BF16)
        zbuf[...] = jnp.zeros((ZERO_ROWS, D_MODEL), BF16)
        for part in range(NBUF):
            spare = pltpu.make_async_copy(
                ybuf.at[NBUF - 1], ys_ref.at[pl.ds((SPARE_GRAN + part * STEP_GRANS) * GRAN, STEP_ROWS), :],
                out_sem.at[NBUF - 1])
            spare.start()
            spare.wait()

        for ahead in range(1, NBUF - 1):
            tile_gather(ahead, ahead)

    def zero_fill(t, act):
        count = LOCAL_GRANS - ug_ref[t]
        for bit in range(ZERO_BITS):
            size = GRAN << bit

            @pl.when(((count >> bit) & 1) == 1)
            def _():
                end = LOCAL_GRANS - (count & ((1 << bit) - 1))
                first = pl.multiple_of((t * LOCAL_GRANS + end) * GRAN - size, GRAN)
                act(pltpu.make_async_copy(zbuf.at[pl.ds(0, size), :], ys_ref.at[pl.ds(first, size), :], zero_sem))

    @pl.when(jnp.logical_and(j >= 1, j <= N_TOK_TILES))
    def _():
        zero_fill(j - 1, lambda cp: cp.wait())

    @pl.when(j < N_TOK_TILES)
    def _():
        zero_fill(j, lambda cp: cp.start())

    def gather_wait():
        pltpu.make_async_copy(xs_ref.at[pl.ds(0, STEP_ROWS), :], xbuf.at[slot], in_sem.at[slot]).wait()

    @pl.when(jnp.logical_and(j >= NBUF - 1, j - NBUF < n_tiles))
    def _():
        pltpu.make_async_copy(ybuf.at[slot], ys_ref.at[pl.ds(0, STEP_ROWS), :], out_sem.at[slot]).wait()

    @pl.when(jnp.logical_and(j >= n_tiles, j < n_tiles + NBUF - 1))
    def _():
        gather_wait()

    @pl.when(j == n_tiles)
    def _():
        prev_scatter()

    def weight_copies(expert, s):
        return [pltpu.make_async_copy(hbm.at[expert], stage.at[s], w_sem.at[s])
                for hbm, stage in ((wg_hbm_ref, wg_stage), (wu_hbm_ref, wu_stage), (wd_hbm_ref, wd_stage))]

    @pl.when(j == 0)
    def _():
        for cp in weight_copies(te_ref[0], 0):
            cp.start()

    @pl.when(jnp.logical_and(j < n_tiles, jnp.logical_or(j == 0, te_ref[j] != te_ref[jnp.maximum(j - 1, 0)])))
    def _():
        s = wpar_ref[j]
        for cp in weight_copies(te_ref[j], s):
            cp.wait()
        wg_ref[...] = wg_stage[s].astype(BF16)
        wu_ref[...] = wu_stage[s].astype(BF16)
        wd_ref[...] = wd_stage[s].astype(BF16)
        nxt = wnext_ref[te_ref[j]]

        @pl.when(nxt >= 0)
        def _():
            for cp in weight_copies(nxt, 1 - s):
                cp.start()

    @pl.when(j < n_tiles)
    def _():
        gather_wait()
        chains = range(0, TE, CHAIN_ROWS)
        xb = [xbuf[slot, r:r + CHAIN_ROWS, :] for r in chains]
        hg = [jnp.dot(x, wg_ref[...], preferred_element_type=F32) for x in xb]
        hu = [jnp.dot(x, wu_ref[...], preferred_element_type=F32) for x in xb]
        tile_gather(j + NBUF - 1, prev_slot)
        prev_scatter()
        h = [(a * _sigmoid(a) * b).astype(BF16) for a, b in zip(hg, hu)]
        for hc, r in zip(h, chains):
            ybuf[slot, r:r + CHAIN_ROWS, :] = jnp.dot(hc, wd_ref[...], preferred_element_type=F32).astype(BF16)


def _expert_call(tile_expert, n_steps, gsrc, gdst, used_grans, w_parity, w_next, xs, wg, wu, wd):
    hbm = pl.BlockSpec(memory_space=pl.ANY)
    return pl.pallas_call(
        _expert_kernel,
        grid_spec=pltpu.PrefetchScalarGridSpec(
            num_scalar_prefetch=7,
            grid=(MAX_STEPS + NBUF,),
            in_specs=[hbm, hbm, hbm, hbm],
            out_specs=hbm,
            scratch_shapes=[pltpu.VMEM((NBUF, STEP_ROWS, D_MODEL), BF16),
                            pltpu.VMEM((NBUF, STEP_ROWS, D_MODEL), BF16),
                            pltpu.VMEM((ZERO_ROWS, D_MODEL), BF16),
                            pltpu.VMEM((D_MODEL, D_EXPERT), BF16), pltpu.VMEM((D_MODEL, D_EXPERT), BF16),
                            pltpu.VMEM((D_EXPERT, D_MODEL), BF16),
                            pltpu.VMEM((2, D_MODEL, D_EXPERT), F32), pltpu.VMEM((2, D_MODEL, D_EXPERT), F32),
                            pltpu.VMEM((2, D_EXPERT, D_MODEL), F32),
                            pltpu.SemaphoreType.DMA((NBUF,)), pltpu.SemaphoreType.DMA((NBUF,)),
                            pltpu.SemaphoreType.DMA, pltpu.SemaphoreType.DMA((2,))],
        ),
        out_shape=jax.ShapeDtypeStruct(((SPARE_GRAN + NBUF * STEP_GRANS) * GRAN, D_MODEL), BF16),
        compiler_params=pltpu.CompilerParams(
            dimension_semantics=("arbitrary",), vmem_limit_bytes=VMEM_LIMIT),
        name="experts",
    )(tile_expert, n_steps, gsrc, gdst, used_grans, w_parity, w_next, xs, wg, wu, wd)


def _combine_kernel(ys_ref, x1_ref, rf_ref, g_ref, b_ref, out_ref):
    subs = range(COMBINE_SUB)
    col = lax.broadcasted_iota(I32, (TM, LOCAL_ROWS), 1).astype(F32)
    route = [rf_ref[:, s * TM:(s + 1) * TM].T for s in subs]
    unsort = [(jnp.where(col == r[:, 2:3], r[:, 0:1], 0.0)
               + jnp.where(col == r[:, 3:4], r[:, 1:2], 0.0)).astype(BF16) for r in route]
    ffn = [jnp.dot(unsort[s], ys_ref[s * LOCAL_ROWS:(s + 1) * LOCAL_ROWS, :], preferred_element_type=F32)
           for s in subs]
    for s in subs:
        h = ALPHA * x1_ref[s * TM:(s + 1) * TM, :] + ffn[s]
        mu = jnp.mean(h, axis=-1, keepdims=True)
        hc = h - mu
        var = jnp.mean(hc * hc, axis=-1, keepdims=True)
        out_ref[s * TM:(s + 1) * TM, :] = hc * lax.rsqrt(var + LN_EPS) * g_ref[...] + b_ref[...]


def _combine_call(ys, x1, rf, g, b):
    tm = COMBINE_SUB * TM
    return pl.pallas_call(
        _combine_kernel,
        grid=(TOKENS // tm,),
        in_specs=[pl.BlockSpec((COMBINE_SUB * LOCAL_ROWS, D_MODEL), lambda i: (i, 0)),
                  pl.BlockSpec((tm, D_MODEL), lambda i: (i, 0)),
                  pl.BlockSpec((8, tm), lambda i: (0, i)),
                  pl.BlockSpec((1, D_MODEL), lambda i: (0, 0)),
                  pl.BlockSpec((1, D_MODEL), lambda i: (0, 0))],
        out_specs=pl.BlockSpec((tm, D_MODEL), lambda i: (i, 0)),
        out_shape=jax.ShapeDtypeStruct((TOKENS, D_MODEL), F32),
        compiler_params=pltpu.CompilerParams(
            dimension_semantics=("arbitrary",), vmem_limit_bytes=VMEM_LIMIT),
        name="combine_ln2",
    )(ys, x1, rf, g, b)


def _router_cols(w_router_group, w_router_expert):
    w = jnp.concatenate([w_router_group, jnp.zeros((D_MODEL, 4), F32), w_router_expert,
                         jnp.zeros((D_MODEL, V7X_LANES - ROUTER_ROWS), F32)], axis=1)
    hi = w.astype(BF16)
    lo = (w - hi.astype(F32)).astype(BF16)
    return hi, lo


def _layer(x, w_in, conv_w, w_out_conv, w_out_attn, w_o, ln1_g, ln1_b,
           w_router_group, w_router_expert, w_gate, w_up, w_down, ln2_g, ln2_b):
    slopes = jnp.asarray([2.0 ** (-8.0 * (h + 1) / N_HEADS) for h in range(N_HEADS)], F32)
    q, k, v, za, sgb = _proj_call(x, w_in, conv_w, w_out_conv)
    o_a, o_b = _attn_call(slopes, q, k, v)

    wr_hi, wr_lo = _router_cols(w_router_group, w_router_expert)
    x1, xs, rf, mt = _merge_call(
        o_a, o_b, za.reshape(TOKENS, D_MODEL), sgb.reshape(TOKENS, D_MODEL),
        x.reshape(TOKENS, D_MODEL), w_out_attn, w_o,
        ln1_g.reshape(1, D_MODEL), ln1_b.reshape(1, D_MODEL), wr_hi, wr_lo)

    grans = mt.reshape(N_TOK_TILES, N_EXPERTS, V7X_LANES)[:, :, 0].astype(I32)
    local_start = jnp.cumsum(grans, axis=1) - grans
    grans_t = grans.T
    tiles_e = (jnp.sum(grans_t, axis=1) + TILE_GRANS - 1) // TILE_GRANS
    tile_end = jnp.cumsum(tiles_e)
    n_steps = tile_end[-1].reshape(1)
    all_tiles = MAX_STEPS + NBUF
    tile_ids = jnp.arange(all_tiles, dtype=I32)
    tile_expert = jnp.minimum(
        jnp.sum((tile_ids[:, None] >= tile_end[None, :]).astype(I32), axis=1), N_EXPERTS - 1)
    run_slot = TILE_GRANS * (tile_end - tiles_e)[:, None] + jnp.cumsum(grans_t, axis=1) - grans_t
    run_src = jnp.arange(N_TOK_TILES, dtype=I32)[None, :] * LOCAL_GRANS + local_start.T
    pick = (tile_expert[:all_tiles, None] == jnp.arange(N_EXPERTS, dtype=I32)[None, :])[:, :, None]
    t_slot = jnp.sum(jnp.where(pick, run_slot[None], 0), axis=1)
    t_len = jnp.sum(jnp.where(pick, grans_t[None], 0), axis=1)
    t_src = jnp.sum(jnp.where(pick, run_src[None], 0), axis=1)
    slots = jnp.arange(all_tiles * TILE_GRANS, dtype=I32).reshape(all_tiles, TILE_GRANS)
    k = slots[:, :, None] - t_slot[:, None, :]
    hit = (k >= 0) & (k < t_len[:, None, :])
    gran = jnp.sum(jnp.where(hit, t_src[:, None, :] + k, 0), axis=2).reshape(-1)
    filled = (jnp.sum(hit.astype(I32), axis=2) > 0).reshape(-1)
    slots = slots.reshape(-1)
    gsrc = jnp.where(filled, gran, 0)
    gdst = jnp.where(filled, gran, SPARE_GRAN + slots % (NBUF * STEP_GRANS))
    gdst = jnp.concatenate([SPARE_GRAN + (NBUF - 1) * STEP_GRANS + jnp.arange(STEP_GRANS, dtype=I32), gdst])

    starts = jnp.concatenate([jnp.ones((1,), I32), (tile_expert[1:] != tile_expert[:-1]).astype(I32)])
    w_parity = (jnp.cumsum(starts) - 1) % 2
    ids = jnp.arange(N_EXPERTS, dtype=I32)
    later = jnp.where((tiles_e > 0)[None, :] & (ids[None, :] > ids[:, None]), ids[None, :], N_EXPERTS)
    w_next = jnp.min(later, axis=1)
    w_next = jnp.where(w_next == N_EXPERTS, -1, w_next)

    ys = _expert_call(tile_expert, n_steps, gsrc, gdst, jnp.sum(grans, axis=1), w_parity, w_next, xs,
                      w_gate, w_up, w_down)
    out = _combine_call(ys, x1, rf, ln2_g.reshape(1, D_MODEL), ln2_b.reshape(1, D_MODEL))
    return out.reshape(BATCH, SEQ, D_MODEL)


def kernel(x, w_in, conv_w, w_out_conv, w_out_attn, w_o, ln1_g, ln1_b, w_router_group, w_router_expert, w_gate, w_up, w_down, ln2_g, ln2_b):
    depth = w_in.shape[0]
    for l in range(depth):
        x = _layer(x, w_in[l], conv_w[l], w_out_conv[l], w_out_attn[l], w_o[l], ln1_g[l], ln1_b[l],
                   w_router_group[l], w_router_expert[l], w_gate[l], w_up[l], w_down[l], ln2_g[l], ln2_b[l])
    return x
```

```python
import jax
import jax.numpy as jnp
from jax import lax
from jax.experimental import pallas as pl
from jax.experimental.pallas import tpu as pltpu

F32 = jnp.float32
BF16 = jnp.bfloat16
I32 = jnp.int32

V7X_VMEM_BYTES = 64 * 1024 * 1024
V7X_LANES = 128
V7X_MXU_DIM = 256

D_MODEL = 1024
BATCH = 8
SEQ = 2048
TOKENS = BATCH * SEQ
CONV_WIDTH = 512
N_HEADS = 8
HEAD_DIM = 64
ATTN_WIDTH = N_HEADS * HEAD_DIM
MOBA_BLOCK = 256
N_BLOCKS = SEQ // MOBA_BLOCK
MOBA_TOPK = 3
N_GROUPS = 4
EXPERTS_PER_GROUP = 8
N_EXPERTS = N_GROUPS * EXPERTS_PER_GROUP
D_EXPERT = 256
LN_EPS = 1e-5
ALPHA = 2.0 ** 0.25
IN_COLS = 3 * CONV_WIDTH + 3 * ATTN_WIDTH + 2 * D_MODEL

TM = 256
TM_PROJ = 1024
TE = 512
CHAIN_ROWS = 128
PV_ROWS = HEAD_DIM + 16
GRAN = 16
TILE_GRANS = TE // GRAN
N_TOK_TILES = TOKENS // TM
LOCAL_ROWS = -(-(2 * TM + N_EXPERTS * (GRAN - 1)) // V7X_MXU_DIM) * V7X_MXU_DIM
LOCAL_GRANS = LOCAL_ROWS // GRAN
SPARE_GRAN = N_TOK_TILES * LOCAL_GRANS
ZERO_BITS = (LOCAL_GRANS - 2 * TM // GRAN).bit_length()
ZERO_ROWS = GRAN << (ZERO_BITS - 1)
MAX_TILES = (2 * TOKENS + N_TOK_TILES * N_EXPERTS * (GRAN - 1)) // TE + N_EXPERTS
PAIRS = 4
MERGE_SUB = 2
MERGE_RING = 3
COMBINE_SUB = 4
STEP_ROWS = TE
STEP_GRANS = TILE_GRANS
MAX_STEPS = MAX_TILES
NBUF = 5
ROUTER_ROWS = 40
VMEM_LIMIT = V7X_VMEM_BYTES - 8 * 1024 * 1024
NEG_INF = float("-inf")


def _sigmoid(z):
    return 1.0 / (1.0 + jnp.exp(-z))


def _proj_kernel(x_ref, w_in_hbm_ref, convw_ref, woc_f32_ref, q_ref, k_ref, v_ref, za_ref, sgb_ref,
                 ubuf, w_in_ref, woc_ref, stage, stage_sem):
    s = pl.program_id(1)
    tm = TM_PROJ

    @pl.when((pl.program_id(0) == 0) & (s == 0))
    def _():
        def chunk(c):
            return pltpu.make_async_copy(w_in_hbm_ref.at[:, c * CONV_WIDTH:(c + 1) * CONV_WIDTH],
                                         stage.at[c % 2], stage_sem.at[c % 2])

        n_chunks = IN_COLS // CONV_WIDTH
        chunk(0).start()
        for c in range(n_chunks):
            if c + 1 < n_chunks:
                chunk(c + 1).start()
            chunk(c).wait()
            w_in_ref[:, c * CONV_WIDTH:(c + 1) * CONV_WIDTH] = stage[c % 2].astype(BF16)
        woc_ref[...] = woc_f32_ref[...].astype(BF16)

    xb = x_ref[0].astype(BF16)

    def proj(c0, c1):
        return jnp.dot(xb, w_in_ref[:, c0:c1], preferred_element_type=F32)

    c_b = proj(0, CONV_WIDTH)
    u = proj(CONV_WIDTH, 2 * CONV_WIDTH) * proj(2 * CONV_WIDTH, 3 * CONV_WIDTH)

    @pl.when(s == 0)
    def _():
        ubuf[0:8, :] = jnp.zeros((8, CONV_WIDTH), F32)

    ubuf[8:8 + tm, :] = u
    w = convw_ref[...]
    conv = w[2:3, :] * u + w[1:2, :] * ubuf[7:7 + tm, :] + w[0:1, :] * ubuf[6:6 + tm, :]
    ubuf[0:8, :] = u[tm - 8:tm, :]
    hc = (c_b * conv).astype(BF16)
    y_conv = jnp.dot(hc, woc_ref[...], preferred_element_type=F32)

    o = 3 * CONV_WIDTH
    q_ref[0] = (proj(o, o + ATTN_WIDTH) * (HEAD_DIM ** -0.5)).astype(BF16)
    k_ref[0] = proj(o + ATTN_WIDTH, o + 2 * ATTN_WIDTH).astype(BF16)
    v_ref[0] = proj(o + 2 * ATTN_WIDTH, o + 3 * ATTN_WIDTH).astype(BF16)
    o += 3 * ATTN_WIDTH
    za_ref[0] = (_sigmoid(proj(o, o + D_MODEL)) * y_conv).astype(BF16)
    sgb_ref[0] = _sigmoid(proj(o + D_MODEL, o + 2 * D_MODEL)).astype(BF16)


def _proj_call(x, w_in, conv_w, w_out_conv):
    tok_spec = lambda c: pl.BlockSpec((1, TM_PROJ, c), lambda b, s: (b, s, 0))
    full = lambda shape: pl.BlockSpec(shape, lambda b, s: (0,) * len(shape))
    once = lambda shape: pl.BlockSpec(shape, lambda b, s: (0,) * len(shape), pipeline_mode=pl.Buffered(1))
    return pl.pallas_call(
        _proj_kernel,
        grid=(BATCH, SEQ // TM_PROJ),
        in_specs=[tok_spec(D_MODEL), pl.BlockSpec(memory_space=pl.ANY), full((3, CONV_WIDTH)),
                  once((CONV_WIDTH, D_MODEL))],
        out_specs=[tok_spec(ATTN_WIDTH), tok_spec(ATTN_WIDTH), tok_spec(ATTN_WIDTH),
                   tok_spec(D_MODEL), tok_spec(D_MODEL)],
        out_shape=[jax.ShapeDtypeStruct((BATCH, SEQ, ATTN_WIDTH), BF16)] * 3
        + [jax.ShapeDtypeStruct((BATCH, SEQ, D_MODEL), BF16)] * 2,
        scratch_shapes=[pltpu.VMEM((TM_PROJ + 8, CONV_WIDTH), F32), pltpu.VMEM((D_MODEL, IN_COLS), BF16),
                        pltpu.VMEM((CONV_WIDTH, D_MODEL), BF16),
                        pltpu.VMEM((2, D_MODEL, CONV_WIDTH), F32), pltpu.SemaphoreType.DMA((2,))],
        compiler_params=pltpu.CompilerParams(
            dimension_semantics=("arbitrary", "arbitrary"), vmem_limit_bytes=VMEM_LIMIT),
        name="proj",
    )(x, w_in, conv_w, w_out_conv)


def _attn_kernel(slopes_ref, qa_ref, qb_ref, k_ref, v_ref, oa_ref, ob_ref,
                 kaug_ref, vt_ref, kmean_ref, qaug_ref, pv_ref, mloc_ref, t_ref, p_ref):
    hq = pl.program_id(1)
    j = pl.program_id(2)
    blk = MOBA_BLOCK
    pairs = range(PAIRS)
    pair_w = 2 * HEAD_DIM
    lanes = [slice(pair_w * pp, pair_w * (pp + 1)) for pp in pairs]

    @pl.when(j == 0)
    def _():
        klane = lax.broadcasted_iota(I32, (blk, pair_w), 1)
        koff = lax.broadcasted_iota(I32, (blk, pair_w), 0).astype(F32)
        k_extra = jnp.where(klane == 0, koff, jnp.where(klane == 1, 1.0, 0.0)).astype(BF16)
        orow = lax.broadcasted_iota(I32, (PV_ROWS - HEAD_DIM, blk), 0)
        ones_rows = jnp.where(orow == 0, 1.0, 0.0).astype(BF16)
        for pp in pairs:
            for n in range(N_BLOCKS):
                kblk = k_ref[0, n * blk:(n + 1) * blk, lanes[pp]]
                kaug_ref[pp, n, :, 0:pair_w] = kblk
                kaug_ref[pp, n, :, pair_w:2 * pair_w] = k_extra
                kmean_ref[pp, n:n + 1, :] = jnp.mean(kblk.astype(F32), axis=0, keepdims=True)
                v_t = v_ref[0, n * blk:(n + 1) * blk, lanes[pp]].astype(F32).T.astype(BF16)
                for hh in range(2):
                    vt_ref[pp, n, hh, 0:HEAD_DIM, :] = v_t[hh * HEAD_DIM:(hh + 1) * HEAD_DIM, :]
                    vt_ref[pp, n, hh, HEAD_DIM:PV_ROWS, :] = ones_rows

    lane = lax.broadcasted_iota(I32, (1, 2 * blk), 1)
    qoff_row = jnp.where(lane < blk, lane, lane - blk).astype(F32)
    feat = lax.broadcasted_iota(I32, (2 * HEAD_DIM, blk), 0)
    arow = lax.broadcasted_iota(I32, (2 * HEAD_DIM, 2 * blk), 0)
    blk_i = lax.broadcasted_iota(I32, (N_BLOCKS, 2 * blk), 0)
    key_i = lax.broadcasted_iota(I32, (blk, 2 * blk), 0)
    qry_j = lax.broadcasted_iota(I32, (blk, 2 * blk), 1)
    causal = key_i <= jnp.where(qry_j < blk, qry_j, qry_j - blk)
    slope_rows, q_extras = [], []
    for pp in pairs:
        head = 2 * (PAIRS * hq + pp)
        slope_rows.append(jnp.where(lane < blk, slopes_ref[head], slopes_ref[head + 1]))
        q_extras.append(jnp.where(arow == 0, slope_rows[pp],
                                  jnp.where(arow == 1, -slope_rows[pp] * qoff_row, 0.0)).astype(BF16))

    def prepare(q_ref, pp, slot, qblock):
        q_t = q_ref[0, :, lanes[pp]].astype(F32).T
        qcat = jnp.concatenate([jnp.where(feat < HEAD_DIM, q_t, 0.0), jnp.where(feat >= HEAD_DIM, q_t, 0.0)],
                               axis=1).astype(BF16)
        qaug_ref[pp, slot, 0:2 * HEAD_DIM, :] = qcat
        qaug_ref[pp, slot, 2 * HEAD_DIM:4 * HEAD_DIM, :] = q_extras[pp]
        gate = jnp.dot(kmean_ref[pp].astype(BF16), qcat, preferred_element_type=F32)
        cnt = jnp.zeros((N_BLOCKS, 2 * blk), F32)
        for m in range(N_BLOCKS):
            gm = gate[m:m + 1, :]
            beats = (gm > gate) | ((gm == gate) & (blk_i > m))
            cnt = cnt + jnp.where(beats & (qblock > m), 1.0, 0.0)
        return jnp.where((blk_i < qblock) & (cnt < float(MOBA_TOPK)), 1.0, 0.0)

    qblock_a = j
    qblock_b = N_BLOCKS - 1 - j
    sel_a = [prepare(qa_ref, pp, 0, qblock_a) for pp in pairs]
    sel_b = [prepare(qb_ref, pp, 1, qblock_b) for pp in pairs]

    n_mid = N_BLOCKS - 1
    slots = [(0, 0, qblock_a, True)]
    mids = []
    for s in range(1, n_mid + 1):
        is_a = s <= j
        slots.append((s, jnp.where(is_a, 0, 1), jnp.where(is_a, s - 1, s - 1 - j), False))
        mids.append((is_a, slots[-1][2]))
    slots.append((n_mid + 1, 1, qblock_b, True))

    for pp in pairs:
        for s, which, kb, _ in slots:
            t_ref[pp, s] = jnp.dot(kaug_ref[pp, kb], qaug_ref[pp, which], preferred_element_type=F32)
    for pp in pairs:
        for s, _, _, own in slots:
            t = t_ref[pp, s]
            if own:
                t = jnp.where(causal, t, NEG_INF)
            m_loc = jnp.max(t, axis=0, keepdims=True)
            p_ref[pp, s] = jnp.exp((t - m_loc).astype(BF16))
            mloc_ref[pp, s:s + 1, :] = m_loc
    for pp in pairs:
        for s, _, kb, _ in slots:
            pv_ref[pp, s, 0] = jnp.dot(vt_ref[pp, kb, 0], p_ref[pp, s, :, 0:blk], preferred_element_type=F32)
            pv_ref[pp, s, 1] = jnp.dot(vt_ref[pp, kb, 1], p_ref[pp, s, :, blk:2 * blk],
                                       preferred_element_type=F32)

    def combine(o_ref, pp, own_slot, sel, qblock, mine):
        neg = jnp.full((1, 2 * blk), -1e30, F32)
        pieces = [(own_slot, mloc_ref[pp, own_slot:own_slot + 1, :])]
        for s, (is_a, kb) in enumerate(mids, start=1):
            selrow = jnp.sum(jnp.where(blk_i == kb, sel, 0.0), axis=0, keepdims=True)
            belongs = jnp.where(is_a, 1.0, 0.0) if mine else jnp.where(is_a, 0.0, 1.0)
            used = selrow * belongs > 0.5
            shift = slope_rows[pp] * ((kb - qblock) * blk).astype(F32)
            pieces.append((s, jnp.where(used, mloc_ref[pp, s:s + 1, :] + shift, neg)))
        m_all = pieces[0][1]
        for _, m_s in pieces[1:]:
            m_all = jnp.maximum(m_all, m_s)
        acc = [jnp.zeros((PV_ROWS, blk), F32), jnp.zeros((PV_ROWS, blk), F32)]
        for s, m_s in pieces:
            w = jnp.exp(m_s - m_all)
            for hh in range(2):
                acc[hh] = acc[hh] + pv_ref[pp, s, hh] * w[:, hh * blk:(hh + 1) * blk]
        o_t = jnp.concatenate([a[0:HEAD_DIM, :] / a[HEAD_DIM:HEAD_DIM + 1, :] for a in acc], axis=0)
        o_ref[0, :, lanes[pp]] = o_t.T.astype(BF16)

    for pp in pairs:
        combine(oa_ref, pp, 0, sel_a[pp], qblock_a, True)
        combine(ob_ref, pp, n_mid + 1, sel_b[pp], qblock_b, False)


def _attn_call(slopes, q, k, v):
    half = N_BLOCKS // 2
    assert 2 * HEAD_DIM == V7X_LANES
    width = 2 * HEAD_DIM * PAIRS
    o_a, o_b = pl.pallas_call(
        _attn_kernel,
        grid_spec=pltpu.PrefetchScalarGridSpec(
            num_scalar_prefetch=1,
            grid=(BATCH, N_HEADS // (2 * PAIRS), half),
            in_specs=[
                pl.BlockSpec((1, MOBA_BLOCK, width), lambda b, h, j, sl: (b, j, h)),
                pl.BlockSpec((1, MOBA_BLOCK, width), lambda b, h, j, sl: (b, N_BLOCKS - 1 - j, h)),
                pl.BlockSpec((1, SEQ, width), lambda b, h, j, sl: (b, 0, h)),
                pl.BlockSpec((1, SEQ, width), lambda b, h, j, sl: (b, 0, h)),
            ],
            out_specs=[pl.BlockSpec((1, MOBA_BLOCK, width), lambda b, h, j, sl: (b, j, h)),
                       pl.BlockSpec((1, MOBA_BLOCK, width), lambda b, h, j, sl: (b, half - 1 - j, h))],
            scratch_shapes=[
                pltpu.VMEM((PAIRS, N_BLOCKS, MOBA_BLOCK, 4 * HEAD_DIM), BF16),
                pltpu.VMEM((PAIRS, N_BLOCKS, 2, PV_ROWS, MOBA_BLOCK), BF16),
                pltpu.VMEM((PAIRS, N_BLOCKS, 2 * HEAD_DIM), F32),
                pltpu.VMEM((PAIRS, 2, 4 * HEAD_DIM, 2 * MOBA_BLOCK), BF16),
                pltpu.VMEM((PAIRS, N_BLOCKS + 1, 2, PV_ROWS, MOBA_BLOCK), F32),
                pltpu.VMEM((PAIRS, 16, 2 * MOBA_BLOCK), F32),
                pltpu.VMEM((PAIRS, N_BLOCKS + 1, MOBA_BLOCK, 2 * MOBA_BLOCK), F32),
                pltpu.VMEM((PAIRS, N_BLOCKS + 1, MOBA_BLOCK, 2 * MOBA_BLOCK), BF16),
            ],
        ),
        out_shape=[jax.ShapeDtypeStruct((BATCH, SEQ // 2, ATTN_WIDTH), BF16)] * 2,
        compiler_params=pltpu.CompilerParams(
            dimension_semantics=("arbitrary", "arbitrary", "arbitrary"), vmem_limit_bytes=VMEM_LIMIT),
        name="moba_attn",
    )(slopes, q, q, k, v)
    return o_a, o_b


def _route(logits):
    row8 = lax.broadcasted_iota(I32, (8, TM), 0).astype(F32)
    gl = jnp.where(row8 < float(N_GROUPS), logits[0:8, :], NEG_INF)
    gexp = jnp.exp(gl - jnp.max(gl, axis=0, keepdims=True))
    gprob = gexp / jnp.sum(gexp, axis=0, keepdims=True)
    ptop = jnp.max(gprob, axis=0, keepdims=True)
    gtop = jnp.min(jnp.where(gprob == ptop, row8, 8.0), axis=0, keepdims=True)
    el = logits[8:ROUTER_ROWS, :]
    eg = jnp.where(gtop == 0.0, el[0:8, :],
                   jnp.where(gtop == 1.0, el[8:16, :], jnp.where(gtop == 2.0, el[16:24, :], el[24:32, :])))
    m1 = jnp.max(eg, axis=0, keepdims=True)
    i1 = jnp.min(jnp.where(eg == m1, row8, 8.0), axis=0, keepdims=True)
    eg2 = jnp.where(row8 == i1, NEG_INF, eg)
    m2 = jnp.max(eg2, axis=0, keepdims=True)
    i2 = jnp.min(jnp.where(eg2 == m2, row8, 8.0), axis=0, keepdims=True)
    t2 = jnp.exp(m2 - m1)
    gate1 = ptop * (1.0 / (1.0 + t2))
    gate2 = ptop * (t2 / (1.0 + t2))
    erow = lax.broadcasted_iota(I32, (N_EXPERTS, TM), 0).astype(F32)
    oh1 = jnp.where(erow == gtop * float(EXPERTS_PER_GROUP) + i1, 1.0, 0.0)
    oh2 = jnp.where(erow == gtop * float(EXPERTS_PER_GROUP) + i2, 1.0, 0.0)
    return gate1, gate2, oh1, oh2


def _merge_kernel(oa_ref, ob_ref, za_hbm_ref, sgb_hbm_ref, x_hbm_ref, woa_f32_ref, wo_f32_ref, g_ref, b_ref,
                  wr_hi_ref, wr_lo_ref, x1_ref, xs_ref, rf_ref, mt_ref, woa_ref, wo_ref,
                  za_ring, sgb_ring, x_ring, ring_sem, xs_ring, xs_sem):
    i = pl.program_id(0)
    n_steps = pl.num_programs(0)
    tm = MERGE_SUB * TM

    def fetch(step, act):
        slot = lax.rem(step, MERGE_RING)
        first = pl.multiple_of(step * tm, tm)
        for k, (hbm, ring) in enumerate(((za_hbm_ref, za_ring), (sgb_hbm_ref, sgb_ring), (x_hbm_ref, x_ring))):
            act(pltpu.make_async_copy(hbm.at[pl.ds(first, tm), :], ring.at[slot], ring_sem.at[k, slot]))

    @pl.when(i == 0)
    def _():
        for ahead in range(MERGE_RING - 1):
            fetch(ahead, lambda cp: cp.start())
        woa_ref[...] = woa_f32_ref[...].astype(BF16)
        wo_ref[...] = wo_f32_ref[...].astype(BF16)

    @pl.when(i + MERGE_RING - 1 < n_steps)
    def _():
        fetch(i + MERGE_RING - 1, lambda cp: cp.start())

    def xs_write(step):
        rows_per_step = MERGE_SUB * LOCAL_ROWS
        first = pl.multiple_of(step * rows_per_step, rows_per_step)
        s_ = lax.rem(step, MERGE_RING)
        return pltpu.make_async_copy(xs_ring.at[s_], xs_ref.at[pl.ds(first, rows_per_step), :], xs_sem.at[s_])

    @pl.when(i >= MERGE_RING)
    def _():
        xs_write(i - MERGE_RING).wait()

    fetch(i, lambda cp: cp.wait())
    slot = lax.rem(i, MERGE_RING)
    za_ref, sgb_ref, x_ref = za_ring.at[slot], sgb_ring.at[slot], x_ring.at[slot]

    subs = range(MERGE_SUB)
    rows = [slice(s * TM, (s + 1) * TM) for s in subs]
    steps_per_batch = N_BLOCKS // MERGE_SUB
    in_oa = lax.rem(i, steps_per_batch) < steps_per_batch // 2
    o = [jnp.where(in_oa, oa_ref[0, r, :], ob_ref[0, r, :]) for r in rows]
    y_attn = [jnp.dot(o[s], woa_ref[...], preferred_element_type=F32) for s in subs]
    y = [(za_ref[rows[s], :].astype(F32) + sgb_ref[rows[s], :].astype(F32) * y_attn[s]).astype(BF16)
         for s in subs]
    mix = [jnp.dot(y[s], wo_ref[...], preferred_element_type=F32) for s in subs]
    x1 = []
    for s in subs:
        h = ALPHA * x_ref[rows[s], :] + mix[s]
        mu = jnp.mean(h, axis=-1, keepdims=True)
        hc = h - mu
        var = jnp.mean(hc * hc, axis=-1, keepdims=True)
        x1.append(hc * lax.rsqrt(var + LN_EPS) * g_ref[...] + b_ref[...])
        x1_ref[rows[s], :] = x1[s]

    xh = [x1[s].astype(BF16) for s in subs]
    xl = [(x1[s] - xh[s].astype(F32)).astype(BF16) for s in subs]
    wh = wr_hi_ref[...]
    logits = [(jnp.dot(xh[s], wh, preferred_element_type=F32)
               + jnp.dot(xl[s], wh, preferred_element_type=F32)
               + jnp.dot(xh[s], wr_lo_ref[...], preferred_element_type=F32)).T for s in subs]
    routes = [_route(logits[s]) for s in subs]

    ta = lax.broadcasted_iota(I32, (TM, TM), 0)
    tb = lax.broadcasted_iota(I32, (TM, TM), 1)
    upper = jnp.where(ta < tb, 1.0, 0.0).astype(BF16)
    ea = lax.broadcasted_iota(I32, (N_EXPERTS, N_EXPERTS), 0)
    eb = lax.broadcasted_iota(I32, (N_EXPERTS, N_EXPERTS), 1)
    lower = jnp.where(eb < ea, 1.0, 0.0).astype(BF16)
    lrow = lax.broadcasted_iota(I32, (LOCAL_ROWS, TM), 0).astype(F32)
    zero = jnp.zeros((1, TM), F32)
    cum = [jnp.dot((routes[s][2] + routes[s][3]).astype(BF16), upper, preferred_element_type=F32) for s in subs]
    perm = []
    for s in subs:
        gate1, gate2, oh1, oh2 = routes[s]
        n_e = jnp.sum(oh1 + oh2, axis=1, keepdims=True)
        m_rep = jnp.broadcast_to(jnp.floor((n_e + float(GRAN - 1)) * (1.0 / GRAN)), (N_EXPERTS, V7X_LANES))
        run_start = jnp.dot(lower, m_rep.astype(BF16), preferred_element_type=F32)
        tot = cum[s] + float(GRAN) * run_start[:, 0:1]
        lp1 = jnp.sum(oh1 * tot, axis=0, keepdims=True)
        lp2 = jnp.sum(oh2 * tot, axis=0, keepdims=True)
        perm.append(jnp.where((lrow == lp1) | (lrow == lp2), 1.0, 0.0).astype(BF16))
        rf_ref[:, rows[s]] = jnp.concatenate([gate1, gate2, lp1, lp2, zero, zero, zero, zero], axis=0)
        mt_ref[s * N_EXPERTS:(s + 1) * N_EXPERTS, :] = m_rep
    for s in subs:
        xs_ring[slot, s * LOCAL_ROWS:(s + 1) * LOCAL_ROWS, :] = jnp.dot(
            perm[s], xh[s], preferred_element_type=F32).astype(BF16)
    xs_write(i).start()

    @pl.when(i == n_steps - 1)
    def _():
        for back in range(MERGE_RING):
            xs_write(i - back).wait()


def _merge_call(o_a, o_b, za, sgb, x, woa, wo, g, b, wr_hi, wr_lo):
    tm = MERGE_SUB * TM
    tok = lambda c: pl.BlockSpec((tm, c), lambda i: (i, 0))
    full = lambda shape: pl.BlockSpec(shape, lambda i: (0,) * len(shape))
    hbm = pl.BlockSpec(memory_space=pl.ANY)
    per_batch = SEQ // tm
    half = per_batch // 2
    o_a_spec = pl.BlockSpec((1, tm, ATTN_WIDTH), lambda i: (i // per_batch, jnp.minimum(i % per_batch, half - 1), 0))
    o_b_spec = pl.BlockSpec((1, tm, ATTN_WIDTH), lambda i: (i // per_batch, jnp.maximum(i % per_batch - half, 0), 0))
    return pl.pallas_call(
        _merge_kernel,
        grid=(TOKENS // tm,),
        in_specs=[o_a_spec, o_b_spec, hbm, hbm, hbm,
                  full((ATTN_WIDTH, D_MODEL)), full((D_MODEL, D_MODEL)), full((1, D_MODEL)),
                  full((1, D_MODEL)), full((D_MODEL, V7X_LANES)), full((D_MODEL, V7X_LANES))],
        out_specs=[tok(D_MODEL), hbm,
                   pl.BlockSpec((8, tm), lambda i: (0, i)),
                   pl.BlockSpec((MERGE_SUB * N_EXPERTS, V7X_LANES), lambda i: (i, 0))],
        out_shape=[jax.ShapeDtypeStruct((TOKENS, D_MODEL), F32),
                   jax.ShapeDtypeStruct((N_TOK_TILES * LOCAL_ROWS, D_MODEL), BF16),
                   jax.ShapeDtypeStruct((8, TOKENS), F32),
                   jax.ShapeDtypeStruct((N_TOK_TILES * N_EXPERTS, V7X_LANES), F32)],
        scratch_shapes=[pltpu.VMEM((ATTN_WIDTH, D_MODEL), BF16), pltpu.VMEM((D_MODEL, D_MODEL), BF16),
                        pltpu.VMEM((MERGE_RING, tm, D_MODEL), BF16), pltpu.VMEM((MERGE_RING, tm, D_MODEL), BF16),
                        pltpu.VMEM((MERGE_RING, tm, D_MODEL), F32), pltpu.SemaphoreType.DMA((3, MERGE_RING)),
                        pltpu.VMEM((MERGE_RING, MERGE_SUB * LOCAL_ROWS, D_MODEL), BF16),
                        pltpu.SemaphoreType.DMA((MERGE_RING,))],
        compiler_params=pltpu.CompilerParams(
            dimension_semantics=("arbitrary",), vmem_limit_bytes=VMEM_LIMIT),
        name="merge_ln1_route",
    )(o_a, o_b, za, sgb, x, woa, wo, g, b, wr_hi, wr_lo)


def _granule_copy(src_ref, src_gran, dst_ref, dst_gran, sem):
    src = pl.multiple_of(src_gran * GRAN, GRAN)
    dst = pl.multiple_of(dst_gran * GRAN, GRAN)
    return pltpu.make_async_copy(src_ref.at[pl.ds(src, GRAN), :], dst_ref.at[pl.ds(dst, GRAN), :], sem)


def _expert_kernel(te_ref, nt_ref, gsrc_ref, gdst_ref, ug_ref, wpar_ref, wnext_ref,
                   xs_ref, wg_hbm_ref, wu_hbm_ref, wd_hbm_ref,
                   ys_ref, xbuf, ybuf, zbuf, wg_ref, wu_ref, wd_ref, wg_stage, wu_stage, wd_stage,
                   in_sem, out_sem, zero_sem, w_sem):
    j = pl.program_id(0)
    n_tiles = nt_ref[0]
    slot = lax.rem(j, NBUF)
    prev_slot = lax.rem(j + NBUF - 1, NBUF)

    def tile_gather(step, s):
        for g in range(STEP_GRANS):
            _granule_copy(xs_ref, gsrc_ref[step * STEP_GRANS + g], xbuf.at[s], g,
                          in_sem.at[s]).start(priority=g % 2)

    def prev_scatter():
        for g in range(STEP_GRANS):
            _granule_copy(ybuf.at[prev_slot], g, ys_ref, gdst_ref[j * STEP_GRANS + g],
                          out_sem.at[prev_slot]).start(priority=g % 2)

    @pl.when(j == 0)
    def _():
        tile_gather(0, 0)
        ybuf[NBUF - 1] = jnp.zeros((STEP_ROWS, D_MODEL), BF16)
        zbuf[...] = jnp.zeros((ZERO_ROWS, D_MODEL), BF16)
        for part in range(NBUF):
            spare = pltpu.make_async_copy(
                ybuf.at[NBUF - 1], ys_ref.at[pl.ds((SPARE_GRAN + part * STEP_GRANS) * GRAN, STEP_ROWS), :],
                out_sem.at[NBUF - 1])
            spare.start()
            spare.wait()

        for ahead in range(1, NBUF - 1):
            tile_gather(ahead, ahead)

    def zero_fill(t, act):
        count = LOCAL_GRANS - ug_ref[t]
        for bit in range(ZERO_BITS):
            size = GRAN << bit

            @pl.when(((count >> bit) & 1) == 1)
            def _():
                end = LOCAL_GRANS - (count & ((1 << bit) - 1))
                first = pl.multiple_of((t * LOCAL_GRANS + end) * GRAN - size, GRAN)
                act(pltpu.make_async_copy(zbuf.at[pl.ds(0, size), :], ys_ref.at[pl.ds(first, size), :], zero_sem))

    @pl.when(jnp.logical_and(j >= 1, j <= N_TOK_TILES))
    def _():
        zero_fill(j - 1, lambda cp: cp.wait())

    @pl.when(j < N_TOK_TILES)
    def _():
        zero_fill(j, lambda cp: cp.start())

    def gather_wait():
        pltpu.make_async_copy(xs_ref.at[pl.ds(0, STEP_ROWS), :], xbuf.at[slot], in_sem.at[slot]).wait()

    @pl.when(jnp.logical_and(j >= NBUF - 1, j - NBUF < n_tiles))
    def _():
        pltpu.make_async_copy(ybuf.at[slot], ys_ref.at[pl.ds(0, STEP_ROWS), :], out_sem.at[slot]).wait()

    @pl.when(jnp.logical_and(j >= n_tiles, j < n_tiles + NBUF - 1))
    def _():
        gather_wait()

    @pl.when(j == n_tiles)
    def _():
        prev_scatter()

    def weight_copies(expert, s):
        return [pltpu.make_async_copy(hbm.at[expert], stage.at[s], w_sem.at[s])
                for hbm, stage in ((wg_hbm_ref, wg_stage), (wu_hbm_ref, wu_stage), (wd_hbm_ref, wd_stage))]

    @pl.when(j == 0)
    def _():
        for cp in weight_copies(te_ref[0], 0):
            cp.start()

    @pl.when(jnp.logical_and(j < n_tiles, jnp.logical_or(j == 0, te_ref[j] != te_ref[jnp.maximum(j - 1, 0)])))
    def _():
        s = wpar_ref[j]
        for cp in weight_copies(te_ref[j], s):
            cp.wait()
        wg_ref[...] = wg_stage[s].astype(BF16)
        wu_ref[...] = wu_stage[s].astype(BF16)
        wd_ref[...] = wd_stage[s].astype(BF16)
        nxt = wnext_ref[te_ref[j]]

        @pl.when(nxt >= 0)
        def _():
            for cp in weight_copies(nxt, 1 - s):
                cp.start()

    @pl.when(j < n_tiles)
    def _():
        gather_wait()
        chains = range(0, TE, CHAIN_ROWS)
        xb = [xbuf[slot, r:r + CHAIN_ROWS, :] for r in chains]
        hg = [jnp.dot(x, wg_ref[...], preferred_element_type=F32) for x in xb]
        hu = [jnp.dot(x, wu_ref[...], preferred_element_type=F32) for x in xb]
        tile_gather(j + NBUF - 1, prev_slot)
        prev_scatter()
        h = [(a * _sigmoid(a) * b).astype(BF16) for a, b in zip(hg, hu)]
        for hc, r in zip(h, chains):
            ybuf[slot, r:r + CHAIN_ROWS, :] = jnp.dot(hc, wd_ref[...], preferred_element_type=F32).astype(BF16)


def _expert_call(tile_expert, n_steps, gsrc, gdst, used_grans, w_parity, w_next, xs, wg, wu, wd):
    hbm = pl.BlockSpec(memory_space=pl.ANY)
    return pl.pallas_call(
        _expert_kernel,
        grid_spec=pltpu.PrefetchScalarGridSpec(
            num_scalar_prefetch=7,
            grid=(MAX_STEPS + NBUF,),
            in_specs=[hbm, hbm, hbm, hbm],
            out_specs=hbm,
            scratch_shapes=[pltpu.VMEM((NBUF, STEP_ROWS, D_MODEL), BF16),
                            pltpu.VMEM((NBUF, STEP_ROWS, D_MODEL), BF16),
                            pltpu.VMEM((ZERO_ROWS, D_MODEL), BF16),
                            pltpu.VMEM((D_MODEL, D_EXPERT), BF16), pltpu.VMEM((D_MODEL, D_EXPERT), BF16),
                            pltpu.VMEM((D_EXPERT, D_MODEL), BF16),
                            pltpu.VMEM((2, D_MODEL, D_EXPERT), F32), pltpu.VMEM((2, D_MODEL, D_EXPERT), F32),
                            pltpu.VMEM((2, D_EXPERT, D_MODEL), F32),
                            pltpu.SemaphoreType.DMA((NBUF,)), pltpu.SemaphoreType.DMA((NBUF,)),
                            pltpu.SemaphoreType.DMA, pltpu.SemaphoreType.DMA((2,))],
        ),
        out_shape=jax.ShapeDtypeStruct(((SPARE_GRAN + NBUF * STEP_GRANS) * GRAN, D_MODEL), BF16),
        compiler_params=pltpu.CompilerParams(
            dimension_semantics=("arbitrary",), vmem_limit_bytes=VMEM_LIMIT),
        name="experts",
    )(tile_expert, n_steps, gsrc, gdst, used_grans, w_parity, w_next, xs, wg, wu, wd)


def _combine_kernel(ys_ref, x1_ref, rf_ref, g_ref, b_ref, out_ref):
    subs = range(COMBINE_SUB)
    col = lax.broadcasted_iota(I32, (TM, LOCAL_ROWS), 1).astype(F32)
    route = [rf_ref[:, s * TM:(s + 1) * TM].T for s in subs]
    unsort = [(jnp.where(col == r[:, 2:3], r[:, 0:1], 0.0)
               + jnp.where(col == r[:, 3:4], r[:, 1:2], 0.0)).astype(BF16) for r in route]
    ffn = [jnp.dot(unsort[s], ys_ref[s * LOCAL_ROWS:(s + 1) * LOCAL_ROWS, :], preferred_element_type=F32)
           for s in subs]
    for s in subs:
        h = ALPHA * x1_ref[s * TM:(s + 1) * TM, :] + ffn[s]
        mu = jnp.mean(h, axis=-1, keepdims=True)
        hc = h - mu
        var = jnp.mean(hc * hc, axis=-1, keepdims=True)
        out_ref[s * TM:(s + 1) * TM, :] = hc * lax.rsqrt(var + LN_EPS) * g_ref[...] + b_ref[...]


def _combine_call(ys, x1, rf, g, b):
    tm = COMBINE_SUB * TM
    return pl.pallas_call(
        _combine_kernel,
        grid=(TOKENS // tm,),
        in_specs=[pl.BlockSpec((COMBINE_SUB * LOCAL_ROWS, D_MODEL), lambda i: (i, 0)),
                  pl.BlockSpec((tm, D_MODEL), lambda i: (i, 0)),
                  pl.BlockSpec((8, tm), lambda i: (0, i)),
                  pl.BlockSpec((1, D_MODEL), lambda i: (0, 0)),
                  pl.BlockSpec((1, D_MODEL), lambda i: (0, 0))],
        out_specs=pl.BlockSpec((tm, D_MODEL), lambda i: (i, 0)),
        out_shape=jax.ShapeDtypeStruct((TOKENS, D_MODEL), F32),
        compiler_params=pltpu.CompilerParams(
            dimension_semantics=("arbitrary",), vmem_limit_bytes=VMEM_LIMIT),
        name="combine_ln2",
    )(ys, x1, rf, g, b)


def _router_cols(w_router_group, w_router_expert):
    w = jnp.concatenate([w_router_group, jnp.zeros((D_MODEL, 4), F32), w_router_expert,
                         jnp.zeros((D_MODEL, V7X_LANES - ROUTER_ROWS), F32)], axis=1)
    hi = w.astype(BF16)
    lo = (w - hi.astype(F32)).astype(BF16)
    return hi, lo


def _layer(x, w_in, conv_w, w_out_conv, w_out_attn, w_o, ln1_g, ln1_b,
           w_router_group, w_router_expert, w_gate, w_up, w_down, ln2_g, ln2_b):
    slopes = jnp.asarray([2.0 ** (-8.0 * (h + 1) / N_HEADS) for h in range(N_HEADS)], F32)
    q, k, v, za, sgb = _proj_call(x, w_in, conv_w, w_out_conv)
    o_a, o_b = _attn_call(slopes, q, k, v)

    wr_hi, wr_lo = _router_cols(w_router_group, w_router_expert)
    x1, xs, rf, mt = _merge_call(
        o_a, o_b, za.reshape(TOKENS, D_MODEL), sgb.reshape(TOKENS, D_MODEL),
        x.reshape(TOKENS, D_MODEL), w_out_attn, w_o,
        ln1_g.reshape(1, D_MODEL), ln1_b.reshape(1, D_MODEL), wr_hi, wr_lo)

    grans = mt.reshape(N_TOK_TILES, N_EXPERTS, V7X_LANES)[:, :, 0].astype(I32)
    local_start = jnp.cumsum(grans, axis=1) - grans
    grans_t = grans.T
    tiles_e = (jnp.sum(grans_t, axis=1) + TILE_GRANS - 1) // TILE_GRANS
    tile_end = jnp.cumsum(tiles_e)
    n_steps = tile_end[-1].reshape(1)
    all_tiles = MAX_STEPS + NBUF
    tile_ids = jnp.arange(all_tiles, dtype=I32)
    tile_expert = jnp.minimum(
        jnp.sum((tile_ids[:, None] >= tile_end[None, :]).astype(I32), axis=1), N_EXPERTS - 1)
    run_slot = TILE_GRANS * (tile_end - tiles_e)[:, None] + jnp.cumsum(grans_t, axis=1) - grans_t
    run_src = jnp.arange(N_TOK_TILES, dtype=I32)[None, :] * LOCAL_GRANS + local_start.T
    pick = (tile_expert[:all_tiles, None] == jnp.arange(N_EXPERTS, dtype=I32)[None, :])[:, :, None]
    t_slot = jnp.sum(jnp.where(pick, run_slot[None], 0), axis=1)
    t_len = jnp.sum(jnp.where(pick, grans_t[None], 0), axis=1)
    t_src = jnp.sum(jnp.where(pick, run_src[None], 0), axis=1)
    slots = jnp.arange(all_tiles * TILE_GRANS, dtype=I32).reshape(all_tiles, TILE_GRANS)
    k = slots[:, :, None] - t_slot[:, None, :]
    hit = (k >= 0) & (k < t_len[:, None, :])
    gran = jnp.sum(jnp.where(hit, t_src[:, None, :] + k, 0), axis=2).reshape(-1)
    filled = (jnp.sum(hit.astype(I32), axis=2) > 0).reshape(-1)
    slots = slots.reshape(-1)
    gsrc = jnp.where(filled, gran, 0)
    gdst = jnp.where(filled, gran, SPARE_GRAN + slots % (NBUF * STEP_GRANS))
    gdst = jnp.concatenate([SPARE_GRAN + (NBUF - 1) * STEP_GRANS + jnp.arange(STEP_GRANS, dtype=I32), gdst])

    starts = jnp.concatenate([jnp.ones((1,), I32), (tile_expert[1:] != tile_expert[:-1]).astype(I32)])
    w_parity = (jnp.cumsum(starts) - 1) % 2
    ids = jnp.arange(N_EXPERTS, dtype=I32)
    later = jnp.where((tiles_e > 0)[None, :] & (ids[None, :] > ids[:, None]), ids[None, :], N_EXPERTS)
    w_next = jnp.min(later, axis=1)
    w_next = jnp.where(w_next == N_EXPERTS, -1, w_next)

    ys = _expert_call(tile_expert, n_steps, gsrc, gdst, jnp.sum(grans, axis=1), w_parity, w_next, xs,
                      w_gate, w_up, w_down)
    out = _combine_call(ys, x1, rf, ln2_g.reshape(1, D_MODEL), ln2_b.reshape(1, D_MODEL))
    return out.reshape(BATCH, SEQ, D_MODEL)


def kernel(x, w_in, conv_w, w_out_conv, w_out_attn, w_o, ln1_g, ln1_b, w_router_group, w_router_expert, w_gate, w_up, w_down, ln2_g, ln2_b):
    depth = w_in.shape[0]
    for l in range(depth):
        x = _layer(x, w_in[l], conv_w[l], w_out_conv[l], w_out_attn[l], w_o[l], ln1_g[l], ln1_b[l],
                   w_router_group[l], w_router_expert[l], w_gate[l], w_up[l], w_down[l], ln2_g[l], ln2_b[l])
    return x
```

```python
import jax
import jax.numpy as jnp
from jax import lax
from jax.experimental import pallas as pl
from jax.experimental.pallas import tpu as pltpu

F32 = jnp.float32
BF16 = jnp.bfloat16
I32 = jnp.int32

V7X_VMEM_BYTES = 64 * 1024 * 1024
V7X_LANES = 128
V7X_MXU_DIM = 256

D_MODEL = 1024
BATCH = 8
SEQ = 2048
TOKENS = BATCH * SEQ
CONV_WIDTH = 512
N_HEADS = 8
HEAD_DIM = 64
ATTN_WIDTH = N_HEADS * HEAD_DIM
MOBA_BLOCK = 256
N_BLOCKS = SEQ // MOBA_BLOCK
MOBA_TOPK = 3
N_GROUPS = 4
EXPERTS_PER_GROUP = 8
N_EXPERTS = N_GROUPS * EXPERTS_PER_GROUP
D_EXPERT = 256
LN_EPS = 1e-5
ALPHA = 2.0 ** 0.25
IN_COLS = 3 * CONV_WIDTH + 3 * ATTN_WIDTH + 2 * D_MODEL

TM = 256
TM_PROJ = 1024
TE = 512
CHAIN_ROWS = 128
PV_ROWS = HEAD_DIM + 16
GRAN = 16
TILE_GRANS = TE // GRAN
N_TOK_TILES = TOKENS // TM
LOCAL_ROWS = -(-(2 * TM + N_EXPERTS * (GRAN - 1)) // V7X_MXU_DIM) * V7X_MXU_DIM
LOCAL_GRANS = LOCAL_ROWS // GRAN
SPARE_GRAN = N_TOK_TILES * LOCAL_GRANS
ZERO_BITS = (LOCAL_GRANS - 2 * TM // GRAN).bit_length()
ZERO_ROWS = GRAN << (ZERO_BITS - 1)
MAX_TILES = (2 * TOKENS + N_TOK_TILES * N_EXPERTS * (GRAN - 1)) // TE + N_EXPERTS
PAIRS = 4
MERGE_SUB = 2
COMBINE_SUB = 4
STEP_ROWS = TE
STEP_GRANS = TILE_GRANS
MAX_STEPS = MAX_TILES
NBUF = 5
ROUTER_ROWS = 40
VMEM_LIMIT = V7X_VMEM_BYTES - 8 * 1024 * 1024
NEG_INF = float("-inf")


def _sigmoid(z):
    return 1.0 / (1.0 + jnp.exp(-z))


def _proj_kernel(x_ref, w_in_hbm_ref, convw_ref, woc_f32_ref, q_ref, k_ref, v_ref, za_ref, sgb_ref,
                 ubuf, w_in_ref, woc_ref, stage, stage_sem):
    s = pl.program_id(1)
    tm = TM_PROJ

    @pl.when((pl.program_id(0) == 0) & (s == 0))
    def _():
        def chunk(c):
            return pltpu.make_async_copy(w_in_hbm_ref.at[:, c * CONV_WIDTH:(c + 1) * CONV_WIDTH],
                                         stage.at[c % 2], stage_sem.at[c % 2])

        n_chunks = IN_COLS // CONV_WIDTH
        chunk(0).start()
        for c in range(n_chunks):
            if c + 1 < n_chunks:
                chunk(c + 1).start()
            chunk(c).wait()
            w_in_ref[:, c * CONV_WIDTH:(c + 1) * CONV_WIDTH] = stage[c % 2].astype(BF16)
        woc_ref[...] = woc_f32_ref[...].astype(BF16)

    xb = x_ref[0].astype(BF16)

    def proj(c0, c1):
        return jnp.dot(xb, w_in_ref[:, c0:c1], preferred_element_type=F32)

    c_b = proj(0, CONV_WIDTH)
    u = proj(CONV_WIDTH, 2 * CONV_WIDTH) * proj(2 * CONV_WIDTH, 3 * CONV_WIDTH)

    @pl.when(s == 0)
    def _():
        ubuf[0:8, :] = jnp.zeros((8, CONV_WIDTH), F32)

    ubuf[8:8 + tm, :] = u
    w = convw_ref[...]
    conv = w[2:3, :] * u + w[1:2, :] * ubuf[7:7 + tm, :] + w[0:1, :] * ubuf[6:6 + tm, :]
    ubuf[0:8, :] = u[tm - 8:tm, :]
    hc = (c_b * conv).astype(BF16)
    y_conv = jnp.dot(hc, woc_ref[...], preferred_element_type=F32)

    o = 3 * CONV_WIDTH
    q_ref[0] = (proj(o, o + ATTN_WIDTH) * (HEAD_DIM ** -0.5)).astype(BF16)
    k_ref[0] = proj(o + ATTN_WIDTH, o + 2 * ATTN_WIDTH).astype(BF16)
    v_ref[0] = proj(o + 2 * ATTN_WIDTH, o + 3 * ATTN_WIDTH).astype(BF16)
    o += 3 * ATTN_WIDTH
    za_ref[0] = (_sigmoid(proj(o, o + D_MODEL)) * y_conv).astype(BF16)
    sgb_ref[0] = _sigmoid(proj(o + D_MODEL, o + 2 * D_MODEL)).astype(BF16)


def _proj_call(x, w_in, conv_w, w_out_conv):
    tok_spec = lambda c: pl.BlockSpec((1, TM_PROJ, c), lambda b, s: (b, s, 0))
    full = lambda shape: pl.BlockSpec(shape, lambda b, s: (0,) * len(shape))
    once = lambda shape: pl.BlockSpec(shape, lambda b, s: (0,) * len(shape), pipeline_mode=pl.Buffered(1))
    return pl.pallas_call(
        _proj_kernel,
        grid=(BATCH, SEQ // TM_PROJ),
        in_specs=[tok_spec(D_MODEL), pl.BlockSpec(memory_space=pl.ANY), full((3, CONV_WIDTH)),
                  once((CONV_WIDTH, D_MODEL))],
        out_specs=[tok_spec(ATTN_WIDTH), tok_spec(ATTN_WIDTH), tok_spec(ATTN_WIDTH),
                   tok_spec(D_MODEL), tok_spec(D_MODEL)],
        out_shape=[jax.ShapeDtypeStruct((BATCH, SEQ, ATTN_WIDTH), BF16)] * 3
        + [jax.ShapeDtypeStruct((BATCH, SEQ, D_MODEL), BF16)] * 2,
        scratch_shapes=[pltpu.VMEM((TM_PROJ + 8, CONV_WIDTH), F32), pltpu.VMEM((D_MODEL, IN_COLS), BF16),
                        pltpu.VMEM((CONV_WIDTH, D_MODEL), BF16),
                        pltpu.VMEM((2, D_MODEL, CONV_WIDTH), F32), pltpu.SemaphoreType.DMA((2,))],
        compiler_params=pltpu.CompilerParams(
            dimension_semantics=("arbitrary", "arbitrary"), vmem_limit_bytes=VMEM_LIMIT),
        name="proj",
    )(x, w_in, conv_w, w_out_conv)


def _attn_kernel(slopes_ref, qa_ref, qb_ref, k_ref, v_ref, oa_ref, ob_ref,
                 kaug_ref, vt_ref, kmean_ref, qaug_ref, pv_ref, mloc_ref, t_ref, p_ref):
    hq = pl.program_id(1)
    j = pl.program_id(2)
    blk = MOBA_BLOCK
    pairs = range(PAIRS)
    pair_w = 2 * HEAD_DIM
    lanes = [slice(pair_w * pp, pair_w * (pp + 1)) for pp in pairs]

    @pl.when(j == 0)
    def _():
        klane = lax.broadcasted_iota(I32, (blk, pair_w), 1)
        koff = lax.broadcasted_iota(I32, (blk, pair_w), 0).astype(F32)
        k_extra = jnp.where(klane == 0, koff, jnp.where(klane == 1, 1.0, 0.0)).astype(BF16)
        orow = lax.broadcasted_iota(I32, (PV_ROWS - HEAD_DIM, blk), 0)
        ones_rows = jnp.where(orow == 0, 1.0, 0.0).astype(BF16)
        for pp in pairs:
            for n in range(N_BLOCKS):
                kblk = k_ref[0, n * blk:(n + 1) * blk, lanes[pp]]
                kaug_ref[pp, n, :, 0:pair_w] = kblk
                kaug_ref[pp, n, :, pair_w:2 * pair_w] = k_extra
                kmean_ref[pp, n:n + 1, :] = jnp.mean(kblk.astype(F32), axis=0, keepdims=True)
                v_t = v_ref[0, n * blk:(n + 1) * blk, lanes[pp]].astype(F32).T.astype(BF16)
                for hh in range(2):
                    vt_ref[pp, n, hh, 0:HEAD_DIM, :] = v_t[hh * HEAD_DIM:(hh + 1) * HEAD_DIM, :]
                    vt_ref[pp, n, hh, HEAD_DIM:PV_ROWS, :] = ones_rows

    lane = lax.broadcasted_iota(I32, (1, 2 * blk), 1)
    qoff_row = jnp.where(lane < blk, lane, lane - blk).astype(F32)
    feat = lax.broadcasted_iota(I32, (2 * HEAD_DIM, blk), 0)
    arow = lax.broadcasted_iota(I32, (2 * HEAD_DIM, 2 * blk), 0)
    blk_i = lax.broadcasted_iota(I32, (N_BLOCKS, 2 * blk), 0)
    key_i = lax.broadcasted_iota(I32, (blk, 2 * blk), 0)
    qry_j = lax.broadcasted_iota(I32, (blk, 2 * blk), 1)
    causal = key_i <= jnp.where(qry_j < blk, qry_j, qry_j - blk)
    slope_rows, q_extras = [], []
    for pp in pairs:
        head = 2 * (PAIRS * hq + pp)
        slope_rows.append(jnp.where(lane < blk, slopes_ref[head], slopes_ref[head + 1]))
        q_extras.append(jnp.where(arow == 0, slope_rows[pp],
                                  jnp.where(arow == 1, -slope_rows[pp] * qoff_row, 0.0)).astype(BF16))

    def prepare(q_ref, pp, slot, qblock):
        q_t = q_ref[0, :, lanes[pp]].astype(F32).T
        qcat = jnp.concatenate([jnp.where(feat < HEAD_DIM, q_t, 0.0), jnp.where(feat >= HEAD_DIM, q_t, 0.0)],
                               axis=1).astype(BF16)
        qaug_ref[pp, slot, 0:2 * HEAD_DIM, :] = qcat
        qaug_ref[pp, slot, 2 * HEAD_DIM:4 * HEAD_DIM, :] = q_extras[pp]
        gate = jnp.dot(kmean_ref[pp].astype(BF16), qcat, preferred_element_type=F32)
        cnt = jnp.zeros((N_BLOCKS, 2 * blk), F32)
        for m in range(N_BLOCKS):
            gm = gate[m:m + 1, :]
            beats = (gm > gate) | ((gm == gate) & (blk_i > m))
            cnt = cnt + jnp.where(beats & (qblock > m), 1.0, 0.0)
        return jnp.where((blk_i < qblock) & (cnt < float(MOBA_TOPK)), 1.0, 0.0)

    qblock_a = j
    qblock_b = N_BLOCKS - 1 - j
    sel_a = [prepare(qa_ref, pp, 0, qblock_a) for pp in pairs]
    sel_b = [prepare(qb_ref, pp, 1, qblock_b) for pp in pairs]

    n_mid = N_BLOCKS - 1
    slots = [(0, 0, qblock_a, True)]
    mids = []
    for s in range(1, n_mid + 1):
        is_a = s <= j
        slots.append((s, jnp.where(is_a, 0, 1), jnp.where(is_a, s - 1, s - 1 - j), False))
        mids.append((is_a, slots[-1][2]))
    slots.append((n_mid + 1, 1, qblock_b, True))

    for pp in pairs:
        for s, which, kb, _ in slots:
            t_ref[pp, s] = jnp.dot(kaug_ref[pp, kb], qaug_ref[pp, which], preferred_element_type=F32)
    for pp in pairs:
        for s, _, _, own in slots:
            t = t_ref[pp, s]
            if own:
                t = jnp.where(causal, t, NEG_INF)
            m_loc = jnp.max(t, axis=0, keepdims=True)
            p_ref[pp, s] = jnp.exp((t - m_loc).astype(BF16))
            mloc_ref[pp, s:s + 1, :] = m_loc
    for pp in pairs:
        for s, _, kb, _ in slots:
            pv_ref[pp, s, 0] = jnp.dot(vt_ref[pp, kb, 0], p_ref[pp, s, :, 0:blk], preferred_element_type=F32)
            pv_ref[pp, s, 1] = jnp.dot(vt_ref[pp, kb, 1], p_ref[pp, s, :, blk:2 * blk],
                                       preferred_element_type=F32)

    def combine(o_ref, pp, own_slot, sel, qblock, mine):
        neg = jnp.full((1, 2 * blk), -1e30, F32)
        pieces = [(own_slot, mloc_ref[pp, own_slot:own_slot + 1, :])]
        for s, (is_a, kb) in enumerate(mids, start=1):
            selrow = jnp.sum(jnp.where(blk_i == kb, sel, 0.0), axis=0, keepdims=True)
            belongs = jnp.where(is_a, 1.0, 0.0) if mine else jnp.where(is_a, 0.0, 1.0)
            used = selrow * belongs > 0.5
            shift = slope_rows[pp] * ((kb - qblock) * blk).astype(F32)
            pieces.append((s, jnp.where(used, mloc_ref[pp, s:s + 1, :] + shift, neg)))
        m_all = pieces[0][1]
        for _, m_s in pieces[1:]:
            m_all = jnp.maximum(m_all, m_s)
        acc = [jnp.zeros((PV_ROWS, blk), F32), jnp.zeros((PV_ROWS, blk), F32)]
        for s, m_s in pieces:
            w = jnp.exp(m_s - m_all)
            for hh in range(2):
                acc[hh] = acc[hh] + pv_ref[pp, s, hh] * w[:, hh * blk:(hh + 1) * blk]
        o_t = jnp.concatenate([a[0:HEAD_DIM, :] / a[HEAD_DIM:HEAD_DIM + 1, :] for a in acc], axis=0)
        o_ref[0, :, lanes[pp]] = o_t.T.astype(BF16)

    for pp in pairs:
        combine(oa_ref, pp, 0, sel_a[pp], qblock_a, True)
        combine(ob_ref, pp, n_mid + 1, sel_b[pp], qblock_b, False)


def _attn_call(slopes, q, k, v):
    half = N_BLOCKS // 2
    assert 2 * HEAD_DIM == V7X_LANES
    width = 2 * HEAD_DIM * PAIRS
    o_a, o_b = pl.pallas_call(
        _attn_kernel,
        grid_spec=pltpu.PrefetchScalarGridSpec(
            num_scalar_prefetch=1,
            grid=(BATCH, N_HEADS // (2 * PAIRS), half),
            in_specs=[
                pl.BlockSpec((1, MOBA_BLOCK, width), lambda b, h, j, sl: (b, j, h)),
                pl.BlockSpec((1, MOBA_BLOCK, width), lambda b, h, j, sl: (b, N_BLOCKS - 1 - j, h)),
                pl.BlockSpec((1, SEQ, width), lambda b, h, j, sl: (b, 0, h)),
                pl.BlockSpec((1, SEQ, width), lambda b, h, j, sl: (b, 0, h)),
            ],
            out_specs=[pl.BlockSpec((1, MOBA_BLOCK, width), lambda b, h, j, sl: (b, j, h)),
                       pl.BlockSpec((1, MOBA_BLOCK, width), lambda b, h, j, sl: (b, half - 1 - j, h))],
            scratch_shapes=[
                pltpu.VMEM((PAIRS, N_BLOCKS, MOBA_BLOCK, 4 * HEAD_DIM), BF16),
                pltpu.VMEM((PAIRS, N_BLOCKS, 2, PV_ROWS, MOBA_BLOCK), BF16),
                pltpu.VMEM((PAIRS, N_BLOCKS, 2 * HEAD_DIM), F32),
                pltpu.VMEM((PAIRS, 2, 4 * HEAD_DIM, 2 * MOBA_BLOCK), BF16),
                pltpu.VMEM((PAIRS, N_BLOCKS + 1, 2, PV_ROWS, MOBA_BLOCK), F32),
                pltpu.VMEM((PAIRS, 16, 2 * MOBA_BLOCK), F32),
                pltpu.VMEM((PAIRS, N_BLOCKS + 1, MOBA_BLOCK, 2 * MOBA_BLOCK), F32),
                pltpu.VMEM((PAIRS, N_BLOCKS + 1, MOBA_BLOCK, 2 * MOBA_BLOCK), BF16),
            ],
        ),
        out_shape=[jax.ShapeDtypeStruct((BATCH, SEQ // 2, ATTN_WIDTH), BF16)] * 2,
        compiler_params=pltpu.CompilerParams(
            dimension_semantics=("arbitrary", "arbitrary", "arbitrary"), vmem_limit_bytes=VMEM_LIMIT),
        name="moba_attn",
    )(slopes, q, q, k, v)
    return o_a, o_b


def _route(logits):
    row8 = lax.broadcasted_iota(I32, (8, TM), 0).astype(F32)
    gl = jnp.where(row8 < float(N_GROUPS), logits[0:8, :], NEG_INF)
    gexp = jnp.exp(gl - jnp.max(gl, axis=0, keepdims=True))
    gprob = gexp / jnp.sum(gexp, axis=0, keepdims=True)
    ptop = jnp.max(gprob, axis=0, keepdims=True)
    gtop = jnp.min(jnp.where(gprob == ptop, row8, 8.0), axis=0, keepdims=True)
    el = logits[8:ROUTER_ROWS, :]
    eg = jnp.where(gtop == 0.0, el[0:8, :],
                   jnp.where(gtop == 1.0, el[8:16, :], jnp.where(gtop == 2.0, el[16:24, :], el[24:32, :])))
    m1 = jnp.max(eg, axis=0, keepdims=True)
    i1 = jnp.min(jnp.where(eg == m1, row8, 8.0), axis=0, keepdims=True)
    eg2 = jnp.where(row8 == i1, NEG_INF, eg)
    m2 = jnp.max(eg2, axis=0, keepdims=True)
    i2 = jnp.min(jnp.where(eg2 == m2, row8, 8.0), axis=0, keepdims=True)
    t2 = jnp.exp(m2 - m1)
    gate1 = ptop * (1.0 / (1.0 + t2))
    gate2 = ptop * (t2 / (1.0 + t2))
    erow = lax.broadcasted_iota(I32, (N_EXPERTS, TM), 0).astype(F32)
    oh1 = jnp.where(erow == gtop * float(EXPERTS_PER_GROUP) + i1, 1.0, 0.0)
    oh2 = jnp.where(erow == gtop * float(EXPERTS_PER_GROUP) + i2, 1.0, 0.0)
    return gate1, gate2, oh1, oh2


def _merge_kernel(oa_ref, ob_ref, za_ref, sgb_ref, x_ref, woa_f32_ref, wo_f32_ref, g_ref, b_ref,
                  wr_hi_ref, wr_lo_ref, x1_ref, xs_ref, rf_ref, mt_ref, woa_ref, wo_ref):
    i = pl.program_id(0)

    @pl.when(i == 0)
    def _():
        woa_ref[...] = woa_f32_ref[...].astype(BF16)
        wo_ref[...] = wo_f32_ref[...].astype(BF16)

    subs = range(MERGE_SUB)
    rows = [slice(s * TM, (s + 1) * TM) for s in subs]
    steps_per_batch = N_BLOCKS // MERGE_SUB
    in_oa = lax.rem(i, steps_per_batch) < steps_per_batch // 2
    o = [jnp.where(in_oa, oa_ref[0, r, :], ob_ref[0, r, :]) for r in rows]
    y_attn = [jnp.dot(o[s], woa_ref[...], preferred_element_type=F32) for s in subs]
    y = [(za_ref[rows[s], :].astype(F32) + sgb_ref[rows[s], :].astype(F32) * y_attn[s]).astype(BF16)
         for s in subs]
    mix = [jnp.dot(y[s], wo_ref[...], preferred_element_type=F32) for s in subs]
    x1 = []
    for s in subs:
        h = ALPHA * x_ref[rows[s], :] + mix[s]
        mu = jnp.mean(h, axis=-1, keepdims=True)
        hc = h - mu
        var = jnp.mean(hc * hc, axis=-1, keepdims=True)
        x1.append(hc * lax.rsqrt(var + LN_EPS) * g_ref[...] + b_ref[...])
        x1_ref[rows[s], :] = x1[s]

    xh = [x1[s].astype(BF16) for s in subs]
    xl = [(x1[s] - xh[s].astype(F32)).astype(BF16) for s in subs]
    wh = wr_hi_ref[...]
    logits = [(jnp.dot(xh[s], wh, preferred_element_type=F32)
               + jnp.dot(xl[s], wh, preferred_element_type=F32)
               + jnp.dot(xh[s], wr_lo_ref[...], preferred_element_type=F32)).T for s in subs]
    routes = [_route(logits[s]) for s in subs]

    ta = lax.broadcasted_iota(I32, (TM, TM), 0)
    tb = lax.broadcasted_iota(I32, (TM, TM), 1)
    upper = jnp.where(ta < tb, 1.0, 0.0).astype(BF16)
    ea = lax.broadcasted_iota(I32, (N_EXPERTS, N_EXPERTS), 0)
    eb = lax.broadcasted_iota(I32, (N_EXPERTS, N_EXPERTS), 1)
    lower = jnp.where(eb < ea, 1.0, 0.0).astype(BF16)
    lrow = lax.broadcasted_iota(I32, (LOCAL_ROWS, TM), 0).astype(F32)
    zero = jnp.zeros((1, TM), F32)
    cum = [jnp.dot((routes[s][2] + routes[s][3]).astype(BF16), upper, preferred_element_type=F32) for s in subs]
    perm = []
    for s in subs:
        gate1, gate2, oh1, oh2 = routes[s]
        n_e = jnp.sum(oh1 + oh2, axis=1, keepdims=True)
        m_rep = jnp.broadcast_to(jnp.floor((n_e + float(GRAN - 1)) * (1.0 / GRAN)), (N_EXPERTS, V7X_LANES))
        run_start = jnp.dot(lower, m_rep.astype(BF16), preferred_element_type=F32)
        tot = cum[s] + float(GRAN) * run_start[:, 0:1]
        lp1 = jnp.sum(oh1 * tot, axis=0, keepdims=True)
        lp2 = jnp.sum(oh2 * tot, axis=0, keepdims=True)
        perm.append(jnp.where((lrow == lp1) | (lrow == lp2), 1.0, 0.0).astype(BF16))
        rf_ref[:, rows[s]] = jnp.concatenate([gate1, gate2, lp1, lp2, zero, zero, zero, zero], axis=0)
        mt_ref[s * N_EXPERTS:(s + 1) * N_EXPERTS, :] = m_rep
    for s in subs:
        xs_ref[s * LOCAL_ROWS:(s + 1) * LOCAL_ROWS, :] = jnp.dot(
            perm[s], xh[s], preferred_element_type=F32).astype(BF16)


def _merge_call(o_a, o_b, za, sgb, x, woa, wo, g, b, wr_hi, wr_lo):
    tm = MERGE_SUB * TM
    tok = lambda c: pl.BlockSpec((tm, c), lambda i: (i, 0))
    full = lambda shape: pl.BlockSpec(shape, lambda i: (0,) * len(shape))
    per_batch = SEQ // tm
    half = per_batch // 2
    o_a_spec = pl.BlockSpec((1, tm, ATTN_WIDTH), lambda i: (i // per_batch, jnp.minimum(i % per_batch, half - 1), 0))
    o_b_spec = pl.BlockSpec((1, tm, ATTN_WIDTH), lambda i: (i // per_batch, jnp.maximum(i % per_batch - half, 0), 0))
    return pl.pallas_call(
        _merge_kernel,
        grid=(TOKENS // tm,),
        in_specs=[o_a_spec, o_b_spec, tok(D_MODEL), tok(D_MODEL), tok(D_MODEL),
                  full((ATTN_WIDTH, D_MODEL)), full((D_MODEL, D_MODEL)), full((1, D_MODEL)),
                  full((1, D_MODEL)), full((D_MODEL, V7X_LANES)), full((D_MODEL, V7X_LANES))],
        out_specs=[tok(D_MODEL), pl.BlockSpec((MERGE_SUB * LOCAL_ROWS, D_MODEL), lambda i: (i, 0)),
                   pl.BlockSpec((8, tm), lambda i: (0, i)),
                   pl.BlockSpec((MERGE_SUB * N_EXPERTS, V7X_LANES), lambda i: (i, 0))],
        out_shape=[jax.ShapeDtypeStruct((TOKENS, D_MODEL), F32),
                   jax.ShapeDtypeStruct((N_TOK_TILES * LOCAL_ROWS, D_MODEL), BF16),
                   jax.ShapeDtypeStruct((8, TOKENS), F32),
                   jax.ShapeDtypeStruct((N_TOK_TILES * N_EXPERTS, V7X_LANES), F32)],
        scratch_shapes=[pltpu.VMEM((ATTN_WIDTH, D_MODEL), BF16), pltpu.VMEM((D_MODEL, D_MODEL), BF16)],
        compiler_params=pltpu.CompilerParams(
            dimension_semantics=("arbitrary",), vmem_limit_bytes=VMEM_LIMIT),
        name="merge_ln1_route",
    )(o_a, o_b, za, sgb, x, woa, wo, g, b, wr_hi, wr_lo)


def _granule_copy(src_ref, src_gran, dst_ref, dst_gran, sem):
    src = pl.multiple_of(src_gran * GRAN, GRAN)
    dst = pl.multiple_of(dst_gran * GRAN, GRAN)
    return pltpu.make_async_copy(src_ref.at[pl.ds(src, GRAN), :], dst_ref.at[pl.ds(dst, GRAN), :], sem)


def _expert_kernel(te_ref, nt_ref, gsrc_ref, gdst_ref, ug_ref, wpar_ref, wnext_ref,
                   xs_ref, wg_hbm_ref, wu_hbm_ref, wd_hbm_ref,
                   ys_ref, xbuf, ybuf, zbuf, wg_ref, wu_ref, wd_ref, wg_stage, wu_stage, wd_stage,
                   in_sem, out_sem, zero_sem, w_sem):
    j = pl.program_id(0)
    n_tiles = nt_ref[0]
    slot = lax.rem(j, NBUF)
    prev_slot = lax.rem(j + NBUF - 1, NBUF)

    def tile_gather(step, s):
        for g in range(STEP_GRANS):
            _granule_copy(xs_ref, gsrc_ref[step * STEP_GRANS + g], xbuf.at[s], g,
                          in_sem.at[s]).start(priority=g % 2)

    def prev_scatter():
        for g in range(STEP_GRANS):
            _granule_copy(ybuf.at[prev_slot], g, ys_ref, gdst_ref[j * STEP_GRANS + g],
                          out_sem.at[prev_slot]).start(priority=g % 2)

    @pl.when(j == 0)
    def _():
        tile_gather(0, 0)
        ybuf[NBUF - 1] = jnp.zeros((STEP_ROWS, D_MODEL), BF16)
        zbuf[...] = jnp.zeros((ZERO_ROWS, D_MODEL), BF16)
        for part in range(NBUF):
            spare = pltpu.make_async_copy(
                ybuf.at[NBUF - 1], ys_ref.at[pl.ds((SPARE_GRAN + part * STEP_GRANS) * GRAN, STEP_ROWS), :],
                out_sem.at[NBUF - 1])
            spare.start()
            spare.wait()

        for ahead in range(1, NBUF - 1):
            tile_gather(ahead, ahead)

    def zero_fill(t, act):
        count = LOCAL_GRANS - ug_ref[t]
        for bit in range(ZERO_BITS):
            size = GRAN << bit

            @pl.when(((count >> bit) & 1) == 1)
            def _():
                end = LOCAL_GRANS - (count & ((1 << bit) - 1))
                first = pl.multiple_of((t * LOCAL_GRANS + end) * GRAN - size, GRAN)
                act(pltpu.make_async_copy(zbuf.at[pl.ds(0, size), :], ys_ref.at[pl.ds(first, size), :], zero_sem))

    @pl.when(jnp.logical_and(j >= 1, j <= N_TOK_TILES))
    def _():
        zero_fill(j - 1, lambda cp: cp.wait())

    @pl.when(j < N_TOK_TILES)
    def _():
        zero_fill(j, lambda cp: cp.start())

    def gather_wait():
        pltpu.make_async_copy(xs_ref.at[pl.ds(0, STEP_ROWS), :], xbuf.at[slot], in_sem.at[slot]).wait()

    @pl.when(jnp.logical_and(j >= NBUF - 1, j - NBUF < n_tiles))
    def _():
        pltpu.make_async_copy(ybuf.at[slot], ys_ref.at[pl.ds(0, STEP_ROWS), :], out_sem.at[slot]).wait()

    @pl.when(jnp.logical_and(j >= n_tiles, j < n_tiles + NBUF - 1))
    def _():
        gather_wait()

    @pl.when(j == n_tiles)
    def _():
        prev_scatter()

    def weight_copies(expert, s):
        return [pltpu.make_async_copy(hbm.at[expert], stage.at[s], w_sem.at[s])
                for hbm, stage in ((wg_hbm_ref, wg_stage), (wu_hbm_ref, wu_stage), (wd_hbm_ref, wd_stage))]

    @pl.when(j == 0)
    def _():
        for cp in weight_copies(te_ref[0], 0):
            cp.start()

    @pl.when(jnp.logical_and(j < n_tiles, jnp.logical_or(j == 0, te_ref[j] != te_ref[jnp.maximum(j - 1, 0)])))
    def _():
        s = wpar_ref[j]
        for cp in weight_copies(te_ref[j], s):
            cp.wait()
        wg_ref[...] = wg_stage[s].astype(BF16)
        wu_ref[...] = wu_stage[s].astype(BF16)
        wd_ref[...] = wd_stage[s].astype(BF16)
        nxt = wnext_ref[te_ref[j]]

        @pl.when(nxt >= 0)
        def _():
            for cp in weight_copies(nxt, 1 - s):
                cp.start()

    @pl.when(j < n_tiles)
    def _():
        gather_wait()
        tile_gather(j + NBUF - 1, prev_slot)
        prev_scatter()
        chains = range(0, TE, CHAIN_ROWS)
        xb = [xbuf[slot, r:r + CHAIN_ROWS, :] for r in chains]
        hg = [jnp.dot(x, wg_ref[...], preferred_element_type=F32) for x in xb]
        hu = [jnp.dot(x, wu_ref[...], preferred_element_type=F32) for x in xb]
        h = [(a * _sigmoid(a) * b).astype(BF16) for a, b in zip(hg, hu)]
        for hc, r in zip(h, chains):
            ybuf[slot, r:r + CHAIN_ROWS, :] = jnp.dot(hc, wd_ref[...], preferred_element_type=F32).astype(BF16)


def _expert_call(tile_expert, n_steps, gsrc, gdst, used_grans, w_parity, w_next, xs, wg, wu, wd):
    hbm = pl.BlockSpec(memory_space=pl.ANY)
    return pl.pallas_call(
        _expert_kernel,
        grid_spec=pltpu.PrefetchScalarGridSpec(
            num_scalar_prefetch=7,
            grid=(MAX_STEPS + NBUF,),
            in_specs=[hbm, hbm, hbm, hbm],
            out_specs=hbm,
            scratch_shapes=[pltpu.VMEM((NBUF, STEP_ROWS, D_MODEL), BF16),
                            pltpu.VMEM((NBUF, STEP_ROWS, D_MODEL), BF16),
                            pltpu.VMEM((ZERO_ROWS, D_MODEL), BF16),
                            pltpu.VMEM((D_MODEL, D_EXPERT), BF16), pltpu.VMEM((D_MODEL, D_EXPERT), BF16),
                            pltpu.VMEM((D_EXPERT, D_MODEL), BF16),
                            pltpu.VMEM((2, D_MODEL, D_EXPERT), F32), pltpu.VMEM((2, D_MODEL, D_EXPERT), F32),
                            pltpu.VMEM((2, D_EXPERT, D_MODEL), F32),
                            pltpu.SemaphoreType.DMA((NBUF,)), pltpu.SemaphoreType.DMA((NBUF,)),
                            pltpu.SemaphoreType.DMA, pltpu.SemaphoreType.DMA((2,))],
        ),
        out_shape=jax.ShapeDtypeStruct(((SPARE_GRAN + NBUF * STEP_GRANS) * GRAN, D_MODEL), BF16),
        compiler_params=pltpu.CompilerParams(
            dimension_semantics=("arbitrary",), vmem_limit_bytes=VMEM_LIMIT),
        name="experts",
    )(tile_expert, n_steps, gsrc, gdst, used_grans, w_parity, w_next, xs, wg, wu, wd)


def _combine_kernel(ys_ref, x1_ref, rf_ref, g_ref, b_ref, out_ref):
    subs = range(COMBINE_SUB)
    col = lax.broadcasted_iota(I32, (TM, LOCAL_ROWS), 1).astype(F32)
    route = [rf_ref[:, s * TM:(s + 1) * TM].T for s in subs]
    unsort = [(jnp.where(col == r[:, 2:3], r[:, 0:1], 0.0)
               + jnp.where(col == r[:, 3:4], r[:, 1:2], 0.0)).astype(BF16) for r in route]
    ffn = [jnp.dot(unsort[s], ys_ref[s * LOCAL_ROWS:(s + 1) * LOCAL_ROWS, :], preferred_element_type=F32)
           for s in subs]
    for s in subs:
        h = ALPHA * x1_ref[s * TM:(s + 1) * TM, :] + ffn[s]
        mu = jnp.mean(h, axis=-1, keepdims=True)
        hc = h - mu
        var = jnp.mean(hc * hc, axis=-1, keepdims=True)
        out_ref[s * TM:(s + 1) * TM, :] = hc * lax.rsqrt(var + LN_EPS) * g_ref[...] + b_ref[...]


def _combine_call(ys, x1, rf, g, b):
    tm = COMBINE_SUB * TM
    return pl.pallas_call(
        _combine_kernel,
        grid=(TOKENS // tm,),
        in_specs=[pl.BlockSpec((COMBINE_SUB * LOCAL_ROWS, D_MODEL), lambda i: (i, 0)),
                  pl.BlockSpec((tm, D_MODEL), lambda i: (i, 0)),
                  pl.BlockSpec((8, tm), lambda i: (0, i)),
                  pl.BlockSpec((1, D_MODEL), lambda i: (0, 0)),
                  pl.BlockSpec((1, D_MODEL), lambda i: (0, 0))],
        out_specs=pl.BlockSpec((tm, D_MODEL), lambda i: (i, 0)),
        out_shape=jax.ShapeDtypeStruct((TOKENS, D_MODEL), F32),
        compiler_params=pltpu.CompilerParams(
            dimension_semantics=("arbitrary",), vmem_limit_bytes=VMEM_LIMIT),
        name="combine_ln2",
    )(ys, x1, rf, g, b)


def _router_cols(w_router_group, w_router_expert):
    w = jnp.concatenate([w_router_group, jnp.zeros((D_MODEL, 4), F32), w_router_expert,
                         jnp.zeros((D_MODEL, V7X_LANES - ROUTER_ROWS), F32)], axis=1)
    hi = w.astype(BF16)
    lo = (w - hi.astype(F32)).astype(BF16)
    return hi, lo


def _layer(x, w_in, conv_w, w_out_conv, w_out_attn, w_o, ln1_g, ln1_b,
           w_router_group, w_router_expert, w_gate, w_up, w_down, ln2_g, ln2_b):
    slopes = jnp.asarray([2.0 ** (-8.0 * (h + 1) / N_HEADS) for h in range(N_HEADS)], F32)
    q, k, v, za, sgb = _proj_call(x, w_in, conv_w, w_out_conv)
    o_a, o_b = _attn_call(slopes, q, k, v)

    wr_hi, wr_lo = _router_cols(w_router_group, w_router_expert)
    x1, xs, rf, mt = _merge_call(
        o_a, o_b, za.reshape(TOKENS, D_MODEL), sgb.reshape(TOKENS, D_MODEL),
        x.reshape(TOKENS, D_MODEL), w_out_attn, w_o,
        ln1_g.reshape(1, D_MODEL), ln1_b.reshape(1, D_MODEL), wr_hi, wr_lo)

    grans = mt.reshape(N_TOK_TILES, N_EXPERTS, V7X_LANES)[:, :, 0].astype(I32)
    local_start = jnp.cumsum(grans, axis=1) - grans
    grans_t = grans.T
    tiles_e = (jnp.sum(grans_t, axis=1) + TILE_GRANS - 1) // TILE_GRANS
    tile_end = jnp.cumsum(tiles_e)
    n_steps = tile_end[-1].reshape(1)
    all_tiles = MAX_STEPS + NBUF
    tile_ids = jnp.arange(all_tiles, dtype=I32)
    tile_expert = jnp.minimum(
        jnp.sum((tile_ids[:, None] >= tile_end[None, :]).astype(I32), axis=1), N_EXPERTS - 1)
    run_slot = TILE_GRANS * (tile_end - tiles_e)[:, None] + jnp.cumsum(grans_t, axis=1) - grans_t
    run_src = jnp.arange(N_TOK_TILES, dtype=I32)[None, :] * LOCAL_GRANS + local_start.T
    pick = (tile_expert[:all_tiles, None] == jnp.arange(N_EXPERTS, dtype=I32)[None, :])[:, :, None]
    t_slot = jnp.sum(jnp.where(pick, run_slot[None], 0), axis=1)
    t_len = jnp.sum(jnp.where(pick, grans_t[None], 0), axis=1)
    t_src = jnp.sum(jnp.where(pick, run_src[None], 0), axis=1)
    slots = jnp.arange(all_tiles * TILE_GRANS, dtype=I32).reshape(all_tiles, TILE_GRANS)
    k = slots[:, :, None] - t_slot[:, None, :]
    hit = (k >= 0) & (k < t_len[:, None, :])
    gran = jnp.sum(jnp.where(hit, t_src[:, None, :] + k, 0), axis=2).reshape(-1)
    filled = (jnp.sum(hit.astype(I32), axis=2) > 0).reshape(-1)
    slots = slots.reshape(-1)
    gsrc = jnp.where(filled, gran, 0)
    gdst = jnp.where(filled, gran, SPARE_GRAN + slots % (NBUF * STEP_GRANS))
    gdst = jnp.concatenate([SPARE_GRAN + (NBUF - 1) * STEP_GRANS + jnp.arange(STEP_GRANS, dtype=I32), gdst])

    starts = jnp.concatenate([jnp.ones((1,), I32), (tile_expert[1:] != tile_expert[:-1]).astype(I32)])
    w_parity = (jnp.cumsum(starts) - 1) % 2
    ids = jnp.arange(N_EXPERTS, dtype=I32)
    later = jnp.where((tiles_e > 0)[None, :] & (ids[None, :] > ids[:, None]), ids[None, :], N_EXPERTS)
    w_next = jnp.min(later, axis=1)
    w_next = jnp.where(w_next == N_EXPERTS, -1, w_next)

    ys = _expert_call(tile_expert, n_steps, gsrc, gdst, jnp.sum(grans, axis=1), w_parity, w_next, xs,
                      w_gate, w_up, w_down)
    out = _combine_call(ys, x1, rf, ln2_g.reshape(1, D_MODEL), ln2_b.reshape(1, D_MODEL))
    return out.reshape(BATCH, SEQ, D_MODEL)


def kernel(x, w_in, conv_w, w_out_conv, w_out_attn, w_o, ln1_g, ln1_b, w_router_group, w_router_expert, w_gate, w_up, w_down, ln2_g, ln2_b):
    depth = w_in.shape[0]
    for l in range(depth):
        x = _layer(x, w_in[l], conv_w[l], w_out_conv[l], w_out_attn[l], w_o[l], ln1_g[l], ln1_b[l],
                   w_router_group[l], w_router_expert[l], w_gate[l], w_up[l], w_down[l], ln2_g[l], ln2_b[l])
    return x
```
